```python
import math
import jax, jax.numpy as jnp
from jax import lax
import numpy as np

D_MODEL = 1024
BATCH = 8
SEQ = 8192
DEPTH = 4

D_FF = 2816
D_POOL = 256
POOL_WINDOWS = (2, 4, 8, 16)
POOL_GROUP = D_POOL // len(POOL_WINDOWS)
D_CONV = 256
CONV_WIDTH = 3
NA_HEADS = 8
NA_HEAD_DIM = 64
D_NA = NA_HEADS * NA_HEAD_DIM
D_MIX = D_POOL + D_CONV + D_NA
GRID_W = 64
NA_ROWS = 8
NA_COLS = 16
D_IN = D_POOL + 3 * D_CONV + 3 * D_NA
ALPHA = (2.0 * DEPTH) ** 0.25
BETA = (8.0 * DEPTH) ** -0.25
LN_EPS = 1e-5
NEG_INF = -1e30

kernel_name = "hybrid_pool_conv_natten_encoder"


def layer_norm(x, g, b):
    xf = x.astype(jnp.float32)
    mu = jnp.mean(xf, axis=-1, keepdims=True)
    var = jnp.mean(jnp.square(xf - mu), axis=-1, keepdims=True)
    y = (xf - mu) * lax.rsqrt(var + LN_EPS)
    return (y * g.astype(jnp.float32) + b.astype(jnp.float32)).astype(x.dtype)


def swiglu(x, w_gate, w_up, w_down):
    return (jax.nn.silu(x @ w_gate) * (x @ w_up)) @ w_down


def pool_mixer(u, pool_w, pool_scale):
    bsz, s, _ = u.shape
    ng = len(POOL_WINDOWS)
    uf = u.astype(jnp.float32).reshape(bsz, s, ng, POOL_GROUP)
    cs = jnp.concatenate([jnp.zeros((bsz, 1, ng, POOL_GROUP), jnp.float32),
                          jnp.cumsum(uf, axis=1)], axis=1)
    t = jnp.arange(s)
    outs = []
    for g, w in enumerate(POOL_WINDOWS):
        lo = jnp.clip(t - w // 2, 0, s)
        hi = jnp.clip(t - w // 2 + w, 0, s)
        cnt = (hi - lo).astype(jnp.float32)[None, :, None]
        mean = (cs[:, hi, g] - cs[:, lo, g]) / cnt
        outs.append(mean - uf[:, :, g])
    p = jnp.stack(outs, axis=2).astype(u.dtype)
    y = jnp.einsum('bsgc,gcd->bsgd', p, pool_w) * pool_scale.reshape(ng, POOL_GROUP)
    return y.reshape(bsz, s, D_POOL)


def gated_conv_mixer(gate_b, gate_c, h, conv_w):
    z = gate_c * h
    zp = jnp.pad(z, ((0, 0), (1, 1), (0, 0)))
    y = conv_w[0] * zp[:, :-2] + conv_w[1] * zp[:, 1:-1] + conv_w[2] * zp[:, 2:]
    return gate_b * y


def neighbourhood_attention(q, k, v, rpb):
    bsz, s, _ = q.shape
    rows = s // GRID_W
    kr = min(NA_ROWS, rows)
    shp = (bsz, rows, GRID_W, NA_HEADS, NA_HEAD_DIM)
    q, k, v = q.reshape(shp), k.reshape(shp), v.reshape(shp)
    r = jnp.arange(rows)
    row_start = jnp.clip(r - kr // 2, 0, rows - kr)
    row_idx = row_start[:, None] + jnp.arange(kr)[None, :]
    kb = k[:, row_idx]
    vb = v[:, row_idx]
    c = jnp.arange(GRID_W)
    col_start = jnp.clip(c - NA_COLS // 2, 0, GRID_W - NA_COLS)
    col_valid = (c[None, :] >= col_start[:, None]) & (c[None, :] < col_start[:, None] + NA_COLS)
    dr = row_idx - r[:, None] + (NA_ROWS - 1)
    dc = jnp.clip(c[None, :] - c[:, None], -(NA_COLS - 1), NA_COLS - 1) + (NA_COLS - 1)
    bias = rpb[:, dr[:, None, :, None], dc[None, :, None, :]]
    scores = jnp.einsum('brqhd,brikhd->bhrqik', q, kb).astype(jnp.float32) * (NA_HEAD_DIM ** -0.5)
    scores = scores + bias.astype(jnp.float32)
    scores = jnp.where(col_valid[:, None, :], scores, NEG_INF)
    p = jax.nn.softmax(scores, axis=(-2, -1)).astype(v.dtype)
    o = jnp.einsum('bhrqik,brikhd->brqhd', p, vb)
    return o.reshape(bsz, s, D_NA)


def _fwd_setup_inputs(seed: int = 0) -> dict:
    key = jax.random.key(seed)
    ks = jax.random.split(key, 16)
    L, D, F = DEPTH, D_MODEL, D_FF
    nrm = lambda k, shp: jax.random.normal(k, shp, jnp.float32)
    x = nrm(ks[0], (BATCH, SEQ, D))
    ffn1_w_gate = nrm(ks[1], (L, D, F)) * D ** -0.5
    ffn1_w_up = nrm(ks[2], (L, D, F)) * D ** -0.5
    ffn1_w_down = nrm(ks[3], (L, F, D)) * (BETA * F ** -0.5)
    ffn2_w_gate = nrm(ks[4], (L, D, F)) * D ** -0.5
    ffn2_w_up = nrm(ks[5], (L, D, F)) * D ** -0.5
    ffn2_w_down = nrm(ks[6], (L, F, D)) * (BETA * F ** -0.5)
    col_scale = jnp.concatenate([
        jnp.ones((D_POOL + 2 * D_CONV,), jnp.float32),
        jnp.full((D_CONV,), BETA, jnp.float32),
        jnp.ones((2 * D_NA,), jnp.float32),
        jnp.full((D_NA,), BETA, jnp.float32)])
    w_in = nrm(ks[7], (L, D, D_IN)) * D ** -0.5 * col_scale
    pool_w = nrm(ks[8], (L, len(POOL_WINDOWS), POOL_GROUP, POOL_GROUP)) * POOL_GROUP ** -0.5
    pool_scale = 1.0 + 0.1 * nrm(ks[9], (L, D_POOL))
    conv_w = nrm(ks[10], (L, CONV_WIDTH, D_CONV)) * CONV_WIDTH ** -0.5
    rpb = 0.02 * nrm(ks[11], (L, NA_HEADS, 2 * NA_ROWS - 1, 2 * NA_COLS - 1))
    w_out = nrm(ks[12], (L, D_MIX, D)) * (BETA * D_MIX ** -0.5)
    ln_g = 1.0 + 0.05 * nrm(ks[13], (L, 3, D))
    ln_b = 0.02 * nrm(ks[14], (L, 3, D))
    return {"x": x,
            "ffn1_w_gate": ffn1_w_gate, "ffn1_w_up": ffn1_w_up, "ffn1_w_down": ffn1_w_down,
            "ffn2_w_gate": ffn2_w_gate, "ffn2_w_up": ffn2_w_up, "ffn2_w_down": ffn2_w_down,
            "w_in": w_in, "pool_w": pool_w, "pool_scale": pool_scale, "conv_w": conv_w,
            "rpb": rpb, "w_out": w_out, "ln_g": ln_g, "ln_b": ln_b}


def _fwd_reference(x, ffn1_w_gate, ffn1_w_up, ffn1_w_down, ffn2_w_gate, ffn2_w_up, ffn2_w_down,
              w_in, pool_w, pool_scale, conv_w, rpb, w_out, ln_g, ln_b):
    splits = np.cumsum([D_POOL, D_CONV, D_CONV, D_CONV, D_NA, D_NA])
    for l in range(DEPTH):
        x = layer_norm(ALPHA * x + 0.5 * swiglu(x, ffn1_w_gate[l], ffn1_w_up[l], ffn1_w_down[l]),
                       ln_g[l, 0], ln_b[l, 0])
        proj = x @ w_in[l]
        u, gb, gc, h, q, k, v = jnp.split(proj, splits, axis=-1)
        y_a = pool_mixer(u, pool_w[l], pool_scale[l])
        y_b = gated_conv_mixer(gb, gc, h, conv_w[l])
        y_c = neighbourhood_attention(q, k, v, rpb[l])
        y = jnp.concatenate([y_a, y_b, y_c], axis=-1) @ w_out[l]
        x = layer_norm(ALPHA * x + y, ln_g[l, 1], ln_b[l, 1])
        x = layer_norm(ALPHA * x + 0.5 * swiglu(x, ffn2_w_gate[l], ffn2_w_up[l], ffn2_w_down[l]),
                       ln_g[l, 2], ln_b[l, 2])
    return x


import jax as _jax
import jax.numpy as _jnp

TWIN_FORMAT = 'train_step'
FWD_PARAMS = ['x', 'ffn1_w_gate', 'ffn1_w_up', 'ffn1_w_down', 'ffn2_w_gate', 'ffn2_w_up', 'ffn2_w_down', 'w_in', 'pool_w', 'pool_scale', 'conv_w', 'rpb', 'w_out', 'ln_g', 'ln_b']
TWIN_WEIGHTS = ['ffn1_w_gate', 'ffn1_w_up', 'ffn1_w_down', 'ffn2_w_gate', 'ffn2_w_up', 'ffn2_w_down', 'w_in', 'pool_w', 'pool_scale', 'conv_w', 'rpb', 'w_out', 'ln_g', 'ln_b']
TWIN_DIFF_INPUT = 'x'
TWIN_INPUTS = ['x', 'ffn1_w_gate', 'ffn1_w_up', 'ffn1_w_down', 'ffn2_w_gate', 'ffn2_w_up', 'ffn2_w_down', 'w_in', 'pool_w', 'pool_scale', 'conv_w', 'rpb', 'w_out', 'ln_g', 'ln_b', 'loss_target', 'm_ffn1_w_gate', 'm_ffn1_w_up', 'm_ffn1_w_down', 'm_ffn2_w_gate', 'm_ffn2_w_up', 'm_ffn2_w_down', 'm_w_in', 'm_pool_w', 'm_pool_scale', 'm_conv_w', 'm_rpb', 'm_w_out', 'm_ln_g', 'm_ln_b', 'v_ffn1_w_gate', 'v_ffn1_w_up', 'v_ffn1_w_down', 'v_ffn2_w_gate', 'v_ffn2_w_up', 'v_ffn2_w_down', 'v_w_in', 'v_pool_w', 'v_pool_scale', 'v_conv_w', 'v_rpb', 'v_w_out', 'v_ln_g', 'v_ln_b']
TWIN_OUTPUTS = ['loss', 'grad_x', 'grad_ffn1_w_gate', 'grad_ffn1_w_up', 'grad_ffn1_w_down', 'grad_ffn2_w_gate', 'grad_ffn2_w_up', 'grad_ffn2_w_down', 'grad_w_in', 'grad_pool_w', 'grad_pool_scale', 'grad_conv_w', 'grad_rpb', 'grad_w_out', 'grad_ln_g', 'grad_ln_b', 'delta_ffn1_w_gate', 'delta_ffn1_w_up', 'delta_ffn1_w_down', 'delta_ffn2_w_gate', 'delta_ffn2_w_up', 'delta_ffn2_w_down', 'delta_w_in', 'delta_pool_w', 'delta_pool_scale', 'delta_conv_w', 'delta_rpb', 'delta_w_out', 'delta_ln_g', 'delta_ln_b', 'new_m_ffn1_w_gate', 'new_m_ffn1_w_up', 'new_m_ffn1_w_down', 'new_m_ffn2_w_gate', 'new_m_ffn2_w_up', 'new_m_ffn2_w_down', 'new_m_w_in', 'new_m_pool_w', 'new_m_pool_scale', 'new_m_conv_w', 'new_m_rpb', 'new_m_w_out', 'new_m_ln_g', 'new_m_ln_b', 'new_v_ffn1_w_gate', 'new_v_ffn1_w_up', 'new_v_ffn1_w_down', 'new_v_ffn2_w_gate', 'new_v_ffn2_w_up', 'new_v_ffn2_w_down', 'new_v_w_in', 'new_v_pool_w', 'new_v_pool_scale', 'new_v_conv_w', 'new_v_rpb', 'new_v_w_out', 'new_v_ln_g', 'new_v_ln_b']
TWIN_LEAF_KINDS = {'loss': 'loss', 'grad_x': 'grad_x', 'grad_ffn1_w_gate': 'grad_w', 'grad_ffn1_w_up': 'grad_w', 'grad_ffn1_w_down': 'grad_w', 'grad_ffn2_w_gate': 'grad_w', 'grad_ffn2_w_up': 'grad_w', 'grad_ffn2_w_down': 'grad_w', 'grad_w_in': 'grad_w', 'grad_pool_w': 'grad_w', 'grad_pool_scale': 'grad_w', 'grad_conv_w': 'grad_w', 'grad_rpb': 'grad_w', 'grad_w_out': 'grad_w', 'grad_ln_g': 'grad_w', 'grad_ln_b': 'grad_w', 'delta_ffn1_w_gate': 'delta_w', 'delta_ffn1_w_up': 'delta_w', 'delta_ffn1_w_down': 'delta_w', 'delta_ffn2_w_gate': 'delta_w', 'delta_ffn2_w_up': 'delta_w', 'delta_ffn2_w_down': 'delta_w', 'delta_w_in': 'delta_w', 'delta_pool_w': 'delta_w', 'delta_pool_scale': 'delta_w', 'delta_conv_w': 'delta_w', 'delta_rpb': 'delta_w', 'delta_w_out': 'delta_w', 'delta_ln_g': 'delta_w', 'delta_ln_b': 'delta_w', 'new_m_ffn1_w_gate': 'new_m', 'new_m_ffn1_w_up': 'new_m', 'new_m_ffn1_w_down': 'new_m', 'new_m_ffn2_w_gate': 'new_m', 'new_m_ffn2_w_up': 'new_m', 'new_m_ffn2_w_down': 'new_m', 'new_m_w_in': 'new_m', 'new_m_pool_w': 'new_m', 'new_m_pool_scale': 'new_m', 'new_m_conv_w': 'new_m', 'new_m_rpb': 'new_m', 'new_m_w_out': 'new_m', 'new_m_ln_g': 'new_m', 'new_m_ln_b': 'new_m', 'new_v_ffn1_w_gate': 'new_v', 'new_v_ffn1_w_up': 'new_v', 'new_v_ffn1_w_down': 'new_v', 'new_v_ffn2_w_gate': 'new_v', 'new_v_ffn2_w_up': 'new_v', 'new_v_ffn2_w_down': 'new_v', 'new_v_w_in': 'new_v', 'new_v_pool_w': 'new_v', 'new_v_pool_scale': 'new_v', 'new_v_conv_w': 'new_v', 'new_v_rpb': 'new_v', 'new_v_w_out': 'new_v', 'new_v_ln_g': 'new_v', 'new_v_ln_b': 'new_v'}


def _forward(args):
    return _fwd_reference(*[args[k] for k in FWD_PARAMS])


def _output_shape():
    def fwd():
        inp = _fwd_setup_inputs(0)
        return _fwd_reference(*[inp[k] for k in FWD_PARAMS])
    out = _jax.eval_shape(fwd)
    return out.shape, out.dtype

N_MICROBATCH = 1
ADAM_LR = 0.001
ADAM_B1 = 0.9
ADAM_B2 = 0.999
ADAM_EPS = 1e-08
ADAM_WD = 0.01
ADAM_STEP = 10
PER_EXAMPLE_BATCH_AXIS = {'x': 0, 'loss_target': 0}
SHARED_INPUTS = []
_WEIGHT_DTYPES = {'ffn1_w_gate': _jnp.float32, 'ffn1_w_up': _jnp.float32, 'ffn1_w_down': _jnp.float32, 'ffn2_w_gate': _jnp.float32, 'ffn2_w_up': _jnp.float32, 'ffn2_w_down': _jnp.float32, 'w_in': _jnp.float32, 'pool_w': _jnp.float32, 'pool_scale': _jnp.float32, 'conv_w': _jnp.float32, 'rpb': _jnp.float32, 'w_out': _jnp.float32, 'ln_g': _jnp.float32, 'ln_b': _jnp.float32}
MOMENT_SCALE = {'ffn1_w_gate': 1.190619e-02, 'ffn1_w_up': 1.167113e-02, 'ffn1_w_down': 4.611286e-02, 'ffn2_w_gate': 1.185588e-02, 'ffn2_w_up': 1.161150e-02, 'ffn2_w_down': 4.588516e-02, 'w_in': 3.002023e-02, 'pool_w': 6.032346e-02, 'pool_scale': 5.907649e-02, 'conv_w': 2.641974e-02, 'rpb': 1.813736e-03, 'w_out': 7.878232e-02, 'ln_g': 2.032720e+01, 'ln_b': 1.498912e+00}


def _to_microbatches(a, axis):
    t = _jnp.moveaxis(a, axis, 0)
    t = t.reshape((N_MICROBATCH, t.shape[0] // N_MICROBATCH) + t.shape[1:])
    return _jnp.moveaxis(t, 1, axis + 1)


def setup_inputs(seed: int = 0) -> dict:
    inp = _fwd_setup_inputs(seed)
    key = _jax.random.fold_in(_jax.random.key(seed), 7919)
    shape, _ = _output_shape()
    out = dict(inp)
    out["loss_target"] = _jax.random.normal(_jax.random.fold_in(key, 0), shape, _jnp.float32)
    for i, name in enumerate(TWIN_WEIGHTS):
        w = inp[name].astype(_jnp.float32)
        if MOMENT_SCALE is None:
            s = _jnp.sqrt(_jnp.mean(_jnp.square(w)) + 1e-30)
        else:
            s = MOMENT_SCALE[name]
        km, kv = _jax.random.split(_jax.random.fold_in(key, i + 1))
        out[name] = w
        out["m_" + name] = s * _jax.random.normal(km, w.shape, _jnp.float32)
        out["v_" + name] = (s * s) * _jax.random.uniform(kv, w.shape, _jnp.float32, 0.5, 1.5)
    if N_MICROBATCH > 1:
        for name, axis in PER_EXAMPLE_BATCH_AXIS.items():
            out[name] = _to_microbatches(out[name], axis)
    return {'x': out['x'], 'ffn1_w_gate': out['ffn1_w_gate'], 'ffn1_w_up': out['ffn1_w_up'], 'ffn1_w_down': out['ffn1_w_down'], 'ffn2_w_gate': out['ffn2_w_gate'], 'ffn2_w_up': out['ffn2_w_up'], 'ffn2_w_down': out['ffn2_w_down'], 'w_in': out['w_in'], 'pool_w': out['pool_w'], 'pool_scale': out['pool_scale'], 'conv_w': out['conv_w'], 'rpb': out['rpb'], 'w_out': out['w_out'], 'ln_g': out['ln_g'], 'ln_b': out['ln_b'], 'loss_target': out['loss_target'], 'm_ffn1_w_gate': out['m_ffn1_w_gate'], 'm_ffn1_w_up': out['m_ffn1_w_up'], 'm_ffn1_w_down': out['m_ffn1_w_down'], 'm_ffn2_w_gate': out['m_ffn2_w_gate'], 'm_ffn2_w_up': out['m_ffn2_w_up'], 'm_ffn2_w_down': out['m_ffn2_w_down'], 'm_w_in': out['m_w_in'], 'm_pool_w': out['m_pool_w'], 'm_pool_scale': out['m_pool_scale'], 'm_conv_w': out['m_conv_w'], 'm_rpb': out['m_rpb'], 'm_w_out': out['m_w_out'], 'm_ln_g': out['m_ln_g'], 'm_ln_b': out['m_ln_b'], 'v_ffn1_w_gate': out['v_ffn1_w_gate'], 'v_ffn1_w_up': out['v_ffn1_w_up'], 'v_ffn1_w_down': out['v_ffn1_w_down'], 'v_ffn2_w_gate': out['v_ffn2_w_gate'], 'v_ffn2_w_up': out['v_ffn2_w_up'], 'v_ffn2_w_down': out['v_ffn2_w_down'], 'v_w_in': out['v_w_in'], 'v_pool_w': out['v_pool_w'], 'v_pool_scale': out['v_pool_scale'], 'v_conv_w': out['v_conv_w'], 'v_rpb': out['v_rpb'], 'v_w_out': out['v_w_out'], 'v_ln_g': out['v_ln_g'], 'v_ln_b': out['v_ln_b']}


def _loss(weights, diff, rest, loss_target):
    with _jax.named_scope("forward"):
        args = {**rest, TWIN_DIFF_INPUT: diff, **{k: w.astype(_WEIGHT_DTYPES[k]) for k, w in weights.items()}}
        y = _forward(args)
    with _jax.named_scope("loss_head"):
        err = _jnp.square(y.astype(_jnp.float32) - loss_target)
        return 0.5 * _jnp.sum(_jnp.mean(err, axis=-1)) if err.ndim else 0.5 * err


def _adamw(w, g, m, v):
    m = ADAM_B1 * m + (1.0 - ADAM_B1) * g
    v = ADAM_B2 * v + (1.0 - ADAM_B2) * _jnp.square(g)
    m_hat = m / (1.0 - ADAM_B1 ** ADAM_STEP)
    v_hat = v / (1.0 - ADAM_B2 ** ADAM_STEP)
    delta = -ADAM_LR * (m_hat / (_jnp.sqrt(v_hat) + ADAM_EPS) + ADAM_WD * w)
    return delta, m, v


def reference(x, ffn1_w_gate, ffn1_w_up, ffn1_w_down, ffn2_w_gate, ffn2_w_up, ffn2_w_down, w_in, pool_w, pool_scale, conv_w, rpb, w_out, ln_g, ln_b, loss_target, m_ffn1_w_gate, m_ffn1_w_up, m_ffn1_w_down, m_ffn2_w_gate, m_ffn2_w_up, m_ffn2_w_down, m_w_in, m_pool_w, m_pool_scale, m_conv_w, m_rpb, m_w_out, m_ln_g, m_ln_b, v_ffn1_w_gate, v_ffn1_w_up, v_ffn1_w_down, v_ffn2_w_gate, v_ffn2_w_up, v_ffn2_w_down, v_w_in, v_pool_w, v_pool_scale, v_conv_w, v_rpb, v_w_out, v_ln_g, v_ln_b):
    given = dict(x=x, ffn1_w_gate=ffn1_w_gate, ffn1_w_up=ffn1_w_up, ffn1_w_down=ffn1_w_down, ffn2_w_gate=ffn2_w_gate, ffn2_w_up=ffn2_w_up, ffn2_w_down=ffn2_w_down, w_in=w_in, pool_w=pool_w, pool_scale=pool_scale, conv_w=conv_w, rpb=rpb, w_out=w_out, ln_g=ln_g, ln_b=ln_b, loss_target=loss_target, m_ffn1_w_gate=m_ffn1_w_gate, m_ffn1_w_up=m_ffn1_w_up, m_ffn1_w_down=m_ffn1_w_down, m_ffn2_w_gate=m_ffn2_w_gate, m_ffn2_w_up=m_ffn2_w_up, m_ffn2_w_down=m_ffn2_w_down, m_w_in=m_w_in, m_pool_w=m_pool_w, m_pool_scale=m_pool_scale, m_conv_w=m_conv_w, m_rpb=m_rpb, m_w_out=m_w_out, m_ln_g=m_ln_g, m_ln_b=m_ln_b, v_ffn1_w_gate=v_ffn1_w_gate, v_ffn1_w_up=v_ffn1_w_up, v_ffn1_w_down=v_ffn1_w_down, v_ffn2_w_gate=v_ffn2_w_gate, v_ffn2_w_up=v_ffn2_w_up, v_ffn2_w_down=v_ffn2_w_down, v_w_in=v_w_in, v_pool_w=v_pool_w, v_pool_scale=v_pool_scale, v_conv_w=v_conv_w, v_rpb=v_rpb, v_w_out=v_w_out, v_ln_g=v_ln_g, v_ln_b=v_ln_b)
    weights = {n: given[n] for n in TWIN_WEIGHTS}
    shared = {n: given[n] for n in SHARED_INPUTS}
    per_example = {n: given[n] for n in ['x']}
    grad_fn = _jax.value_and_grad(_loss, argnums=(0, 1))

    def one_microbatch(ex, loss_target):
        ex = dict(ex)
        diff = ex.pop(TWIN_DIFF_INPUT)
        return grad_fn(weights, diff, {**shared, **ex}, loss_target)

    if N_MICROBATCH == 1:
        loss, (grad_w, grad_x) = one_microbatch(per_example, given["loss_target"])
    else:
        def body(carry, xs):
            loss_sum, grad_sum = carry
            l_k, (gw_k, gx_k) = one_microbatch(xs[0], xs[1])
            with _jax.named_scope("update"):
                return (loss_sum + l_k, _jax.tree.map(_jnp.add, grad_sum, gw_k)), gx_k

        init = (_jnp.zeros((), _jnp.float32), _jax.tree.map(_jnp.zeros_like, weights))
        (loss, grad_w), grad_x = _jax.lax.scan(body, init, (per_example, given["loss_target"]))
    with _jax.named_scope("update"):
        delta_w, new_m, new_v = {}, {}, {}
        for n in TWIN_WEIGHTS:
            delta_w[n], new_m[n], new_v[n] = _adamw(weights[n], grad_w[n], given["m_" + n], given["v_" + n])
    return (loss, grad_x, *[grad_w[n] for n in TWIN_WEIGHTS], *[delta_w[n] for n in TWIN_WEIGHTS],
            *[new_m[n] for n in TWIN_WEIGHTS], *[new_v[n] for n in TWIN_WEIGHTS])
```

```python
import functools

import numpy as np
import jax
import jax.numpy as jnp
from jax import lax
from jax.experimental import pallas as pl
from jax.experimental.pallas import tpu as pltpu

F32, BF16 = jnp.float32, jnp.bfloat16
MESH = pl.DeviceIdType.MESH
N_DEV = 8
MESH_AXES = ("x", "y", "c")

LN_EPS = 1e-5
NEG_INF = -1e30
D_POOL = 256
POOL_WINDOWS = (2, 4, 8, 16)
POOL_GROUP = 64
D_CONV = 256
NA_HEADS = 8
NA_HEAD_DIM = 64
D_NA = NA_HEADS * NA_HEAD_DIM
GRID_W = 64
NA_ROWS = 8
NA_COLS = 16
D_LOC = D_POOL + 3 * D_CONV
D_MIX = D_POOL + D_CONV + D_NA
ADAM_LR, ADAM_B1, ADAM_B2, ADAM_EPS, ADAM_WD, ADAM_STEP = 0.001, 0.9, 0.999, 1e-08, 0.01, 10

VMEM_LIMIT_BYTES = 56 * 1024 * 1024
LANES = 128
BF16_ROWS = 16
HALO = 16
Q_ROWS = 8
K_ROWS = 16
Q_TOK = Q_ROWS * GRID_W
K_TOK = K_ROWS * GRID_W
K_BLK = 4 * GRID_W
HEAD_PAIR = 2 * NA_HEAD_DIM
FFN_CHUNK_DEVS = 4

NT = (((1,), (1,)), ((), ()))
TN = (((0,), (0,)), ((), ()))


def _dot(a, b):
    return jnp.dot(a, b, preferred_element_type=F32)


def _dot_nt(a, b):
    return lax.dot_general(a, b, NT, preferred_element_type=F32)


def _dot_tn(a, b):
    return lax.dot_general(a, b, TN, preferred_element_type=F32)


def _params():
    return pltpu.CompilerParams(vmem_limit_bytes=VMEM_LIMIT_BYTES)


def _row_tile(rows, pref, mult=BF16_ROWS):
    t = min(rows, pref)
    t -= t % mult
    while t > mult and rows % t:
        t -= mult
    assert t > 0 and rows % t == 0, (rows, pref)
    return t


def _mesh_pos():
    return tuple(lax.axis_index(a) for a in MESH_AXES)


def _any_spec():
    return pl.BlockSpec(memory_space=pl.ANY)


def _all_gather(shards, name):
    n = len(shards)

    def body(*refs):
        ins, outs = refs[:n], refs[n:2 * n]
        send_sems, recv_sems, local_sems = refs[2 * n:]
        x, y, c = _mesh_pos()
        me, sibling = (x, y, c), (x, y, 1 - c)
        chips = [(1 - x, y), (x, 1 - y), (1 - x, 1 - y)]

        def copy(a, k, block, to, src=None):
            dst = outs[a].at[4 * block[0] + 2 * block[1] + block[2]]
            return pltpu.make_async_remote_copy(
                src_ref=dst if src is None else src, dst_ref=dst,
                send_sem=send_sems.at[a, k], recv_sem=recv_sems.at[a, k],
                device_id=to, device_id_type=MESH)

        mine = [pltpu.make_async_copy(ins[a], outs[a].at[4 * x + 2 * y + c], local_sems.at[a]) for a in range(n)]
        for cp in mine:
            cp.start()
        first = []
        for a in range(n):
            first.append(copy(a, 0, me, sibling, src=ins[a]))
            first += [copy(a, 1 + j, me, (*chip, c), src=ins[a]) for j, chip in enumerate(chips)]
        for cp in first:
            cp.start()
        passed = []
        for j, chip in enumerate(chips):
            for a in range(n):
                copy(a, 1 + j, (*chip, c), me).wait_recv()
                cp = copy(a, 4 + j, (*chip, c), sibling)
                cp.start()
                passed.append(cp)
        for a in range(n):
            copy(a, 0, sibling, me).wait_recv()
            for j, chip in enumerate(chips):
                copy(a, 4 + j, (*chip, 1 - c), me).wait_recv()
        for cp in first + passed:
            cp.wait_send()
        for cp in mine:
            cp.wait()

    return pl.pallas_call(
        body, name=name,
        out_shape=[jax.ShapeDtypeStruct((N_DEV,) + s.shape, s.dtype) for s in shards],
        in_specs=[_any_spec()] * n, out_specs=[_any_spec()] * n,
        scratch_shapes=[pltpu.SemaphoreType.DMA((n, 7)), pltpu.SemaphoreType.DMA((n, 7)),
                        pltpu.SemaphoreType.DMA((n,))],
    )(*shards)


def _pair_exchange(slabs, name):
    n = len(slabs)

    def body(*refs):
        ins, outs = refs[:n], refs[n:2 * n]
        send_sems, recv_sems = refs[2 * n:]
        x, y, c = _mesh_pos()
        copies = [
            pltpu.make_async_remote_copy(
                src_ref=ins[a].at[2 * j + 1 - c], dst_ref=outs[a].at[j],
                send_sem=send_sems.at[a, j], recv_sem=recv_sems.at[a, j],
                device_id=(x, y, 1 - c), device_id_type=MESH)
            for a in range(n) for j in range(4)]
        for cp in copies:
            cp.start()
        for cp in copies:
            cp.wait_send()
        for cp in copies:
            cp.wait_recv()

    return pl.pallas_call(
        body, name=name,
        out_shape=[jax.ShapeDtypeStruct((4,) + s.shape[1:], s.dtype) for s in slabs],
        in_specs=[_any_spec()] * n, out_specs=[_any_spec()] * n,
        scratch_shapes=[pltpu.SemaphoreType.DMA((n, 4)), pltpu.SemaphoreType.DMA((n, 4))],
    )(*slabs)


def _chip_exchange(parts, name):
    n = len(parts)

    def body(*refs):
        ins, outs = refs[:n], refs[n:2 * n]
        send_sems, recv_sems, local_sems = refs[2 * n:]
        x, y, c = _mesh_pos()
        my_chip = 2 * x + y
        chips = [(1 - x, y), (x, 1 - y), (1 - x, 1 - y)]
        own = [pltpu.make_async_copy(ins[a].at[my_chip], outs[a].at[my_chip], local_sems.at[a]) for a in range(n)]
        for cp in own:
            cp.start()

        def copy(a, k, src_chip, dst_chip, to):
            return pltpu.make_async_remote_copy(
                src_ref=ins[a].at[src_chip], dst_ref=outs[a].at[dst_chip],
                send_sem=send_sems.at[a, k], recv_sem=recv_sems.at[a, k],
                device_id=to, device_id_type=MESH)

        sends = [copy(a, k, 2 * px + py, my_chip, (px, py, c)) for a in range(n) for k, (px, py) in enumerate(chips)]
        for cp in sends:
            cp.start()
        for cp in sends:
            cp.wait_send()
        for a in range(n):
            for k, (px, py) in enumerate(chips):
                copy(a, k, my_chip, 2 * px + py, (px, py, c)).wait_recv()
        for cp in own:
            cp.wait()

    return pl.pallas_call(
        body, name=name,
        out_shape=[jax.ShapeDtypeStruct(s.shape, s.dtype) for s in parts],
        in_specs=[_any_spec()] * n, out_specs=[_any_spec()] * n,
        scratch_shapes=[pltpu.SemaphoreType.DMA((n, 3)), pltpu.SemaphoreType.DMA((n, 3)),
                        pltpu.SemaphoreType.DMA((n,))],
    )(*parts)


def _pair_add(slab, got, core, name):
    _, rows, d = slab.shape
    tr = _row_tile(rows, 1024)

    def body(core_ref, mine_ref, got_ref, out_ref):
        out_ref[...] = (mine_ref[...].astype(F32) + got_ref[...].astype(F32)).astype(out_ref.dtype)

    grid_spec = pltpu.PrefetchScalarGridSpec(
        num_scalar_prefetch=1, grid=(4, rows // tr),
        in_specs=[pl.BlockSpec((1, tr, d), lambda j, r, core_ref: (2 * j + core_ref[0], r, 0)),
                  pl.BlockSpec((1, tr, d), lambda j, r, core_ref: (j, r, 0))],
        out_specs=pl.BlockSpec((1, tr, d), lambda j, r, core_ref: (j, r, 0)))
    return pl.pallas_call(body, name=name, grid_spec=grid_spec,
                          out_shape=jax.ShapeDtypeStruct((4, rows, d), slab.dtype),
                          compiler_params=_params())(core, slab, got)


def _sum_blocks(parts, name):
    k, rows, d = parts.shape
    tr = _row_tile(rows, 512, BF16_ROWS if parts.dtype == BF16 else 8)

    def body(in_ref, out_ref):
        acc = in_ref[0].astype(F32)
        for j in range(1, k):
            acc = acc + in_ref[j].astype(F32)
        out_ref[...] = acc

    return pl.pallas_call(
        body, name=name, grid=(rows // tr,),
        in_specs=[pl.BlockSpec((k, tr, d), lambda r: (0, r, 0))],
        out_specs=pl.BlockSpec((tr, d), lambda r: (r, 0)),
        out_shape=jax.ShapeDtypeStruct((rows, d), F32), compiler_params=_params())(parts)


def _ln_stats(z):
    mu = jnp.mean(z, axis=-1, keepdims=True)
    zc = z - mu
    var = jnp.mean(zc * zc, axis=-1, keepdims=True)
    rstd = lax.rsqrt(var + LN_EPS)
    return zc * rstd, rstd


def _ln_bwd(dy, z, g):
    zhat, rstd = _ln_stats(z)
    dyg = dy * g
    m1 = jnp.mean(dyg, axis=-1, keepdims=True)
    m2 = jnp.mean(dyg * zhat, axis=-1, keepdims=True)
    dz = rstd * (dyg - m1 - zhat * m2)
    return dz, jnp.sum(dy * zhat, axis=0, keepdims=True), jnp.sum(dy, axis=0, keepdims=True)


def _accumulate(ref, value, first):
    @pl.when(first)
    def _():
        ref[...] = value

    @pl.when(jnp.logical_not(first))
    def _():
        ref[...] += value


def _ffn_weight_specs(l, which, fs, d, index_of):
    def spec(row):
        return pl.BlockSpec((FFN_CHUNK_DEVS, 1, 1, fs, d), lambda *g: (index_of(*g), l, row, 0, 0))
    return [spec(2 * which), spec(2 * which + 1), spec(4 + which)]


def _ffn_fwd(x, w_ffn, l, which, ln_g, ln_b, alpha, name):
    s, d = x.shape
    fs = w_ffn.shape[3]
    tf = FFN_CHUNK_DEVS * fs
    n_c = N_DEV // FFN_CHUNK_DEVS
    tm = min(s, 512)

    def body(x_ref, wg_ref, wu_ref, wd_ref, g_ref, b_ref, a_ref, u_ref, z_ref, y_ref, xb_s, acc_s):
        c = pl.program_id(1)

        @pl.when(c == 0)
        def _():
            xb_s[...] = x_ref[...].astype(BF16)
            acc_s[...] = jnp.zeros_like(acc_s)

        xb = xb_s[...]
        a = _dot_nt(xb, wg_ref[...].reshape(tf, d))
        u = _dot_nt(xb, wu_ref[...].reshape(tf, d))
        a_ref[...] = a.astype(BF16)
        u_ref[...] = u.astype(BF16)
        h = (a * jax.nn.sigmoid(a)) * u
        acc_s[...] += _dot(h.astype(BF16), wd_ref[...].reshape(tf, d))

        @pl.when(c == n_c - 1)
        def _():
            z = alpha * x_ref[...] + 0.5 * acc_s[...]
            zhat, _ = _ln_stats(z)
            z_ref[...] = z
            y_ref[...] = zhat * g_ref[...] + b_ref[...]

    row = pl.BlockSpec((tm, d), lambda i, c: (i, 0))
    vec = pl.BlockSpec((1, d), lambda i, c: (0, 0))
    hid = pl.BlockSpec((tm, tf), lambda i, c: (i, c))
    return pl.pallas_call(
        body, name=name, grid=(s // tm, n_c),
        in_specs=[row] + _ffn_weight_specs(l, which, fs, d, lambda i, c: c) + [vec, vec],
        out_specs=[hid, hid, row, row],
        out_shape=[jax.ShapeDtypeStruct((s, N_DEV * fs), BF16)] * 2 + [jax.ShapeDtypeStruct((s, d), F32)] * 2,
        scratch_shapes=[pltpu.VMEM((tm, d), BF16), pltpu.VMEM((tm, d), F32)],
        compiler_params=_params(),
    )(x, w_ffn, w_ffn, w_ffn, ln_g, ln_b)


def _ffn_bwd_dx(dy, z, a, u, w_ffn, l, which, ln_g, alpha, name):
    s, d = dy.shape
    fs = w_ffn.shape[3]
    tf = FFN_CHUNK_DEVS * fs
    n_c = N_DEV // FFN_CHUNK_DEVS
    tm = min(s, 512)

    def body(dy_ref, z_ref, a_ref, u_ref, wg_ref, wu_ref, wd_ref, g_ref,
             dx_ref, da_ref, du_ref, df_ref, dg_ref, db_ref, df_s, acc_s):
        i, c = pl.program_id(0), pl.program_id(1)

        @pl.when(c == 0)
        def _():
            dz, dg, db = _ln_bwd(dy_ref[...], z_ref[...], g_ref[...])
            _accumulate(dg_ref, dg, i == 0)
            _accumulate(db_ref, db, i == 0)
            df = (0.5 * dz).astype(BF16)
            df_s[...] = df
            df_ref[...] = df
            acc_s[...] = alpha * dz

        av = a_ref[...].astype(F32)
        uv = u_ref[...].astype(F32)
        sg = jax.nn.sigmoid(av)
        dh = _dot_nt(df_s[...], wd_ref[...].reshape(tf, d))
        du = (dh * (av * sg)).astype(BF16)
        da = (dh * uv * (sg * (1.0 + av * (1.0 - sg)))).astype(BF16)
        da_ref[...] = da
        du_ref[...] = du
        acc_s[...] += _dot(da, wg_ref[...].reshape(tf, d)) + _dot(du, wu_ref[...].reshape(tf, d))

        @pl.when(c == n_c - 1)
        def _():
            dx_ref[...] = acc_s[...]

    row = pl.BlockSpec((tm, d), lambda i, c: (i, 0))
    vec = pl.BlockSpec((1, d), lambda i, c: (0, 0))
    hid = pl.BlockSpec((tm, tf), lambda i, c: (i, c))
    return pl.pallas_call(
        body, name=name, grid=(s // tm, n_c),
        in_specs=[row, row, hid, hid] + _ffn_weight_specs(l, which, fs, d, lambda i, c: c) + [vec],
        out_specs=[row, hid, hid, row, vec, vec],
        out_shape=[jax.ShapeDtypeStruct((s, d), F32)] + [jax.ShapeDtypeStruct((s, N_DEV * fs), BF16)] * 2
                  + [jax.ShapeDtypeStruct((s, d), BF16)] + [jax.ShapeDtypeStruct((1, d), F32)] * 2,
        scratch_shapes=[pltpu.VMEM((tm, d), BF16), pltpu.VMEM((tm, d), F32)],
        compiler_params=_params(),
    )(dy, z, a, u, w_ffn, w_ffn, w_ffn, ln_g)


def _slab_call(body, slab, n_in, **kw):
    if slab is None:
        return pl.pallas_call(body, **kw)
    kw["in_specs"] = list(kw["in_specs"]) + [_any_spec()]
    kw["input_output_aliases"] = {n_in: len(kw["out_shape"]) - 1}

    def aliased(*refs):
        body(*refs[:n_in], *refs[n_in + 1:])

    return lambda *args: pl.pallas_call(aliased, **kw)(*args, slab)


def _ffn_bwd_dwgu(da, du, x, slab, slab_shape, l, which, name):
    s, d = x.shape
    _, n_l, _, fs, _ = slab_shape
    tf = FFN_CHUNK_DEVS * fs
    n_c = N_DEV // FFN_CHUNK_DEVS
    tk = min(s, 512)
    n_k = s // tk

    def body(da_ref, du_ref, x_ref, out_ref, accg_s, accu_s):
        k = pl.program_id(1)
        xb = x_ref[...].astype(BF16)
        pg = _dot_tn(da_ref[...], xb)
        pu = _dot_tn(du_ref[...], xb)
        _accumulate(accg_s, pg, k == 0)
        _accumulate(accu_s, pu, k == 0)

        @pl.when(k == n_k - 1)
        def _():
            out_ref[:, 0, 0] = accg_s[...].astype(BF16).reshape(FFN_CHUNK_DEVS, fs, d)
            out_ref[:, 0, 1] = accu_s[...].astype(BF16).reshape(FFN_CHUNK_DEVS, fs, d)

    hid = pl.BlockSpec((tk, tf), lambda c, k: (k, c))
    call = _slab_call(
        body, slab, 3, name=name, grid=(n_c, n_k),
        in_specs=[hid, hid, pl.BlockSpec((tk, d), lambda c, k: (k, 0))],
        out_specs=[pl.BlockSpec((FFN_CHUNK_DEVS, 1, 2, fs, d), lambda c, k: (c, l, which, 0, 0))],
        out_shape=[jax.ShapeDtypeStruct(slab_shape, BF16)],
        scratch_shapes=[pltpu.VMEM((tf, d), F32), pltpu.VMEM((tf, d), F32)],
        compiler_params=_params())
    return call(da, du, x)[0]


def _ffn_bwd_dwd(a, u, df, slab, slab_shape, l, which, name):
    s, d = df.shape
    fs = slab_shape[3]
    tf = FFN_CHUNK_DEVS * fs
    n_c = N_DEV // FFN_CHUNK_DEVS
    tk = min(s, 512)
    n_k = s // tk

    def body(a_ref, u_ref, df_ref, out_ref, acc_s):
        k = pl.program_id(1)
        av = a_ref[...].astype(F32)
        h = ((av * jax.nn.sigmoid(av)) * u_ref[...].astype(F32)).astype(BF16)
        _accumulate(acc_s, _dot_tn(h, df_ref[...]), k == 0)

        @pl.when(k == n_k - 1)
        def _():
            out_ref[:, 0, 0] = acc_s[...].astype(BF16).reshape(FFN_CHUNK_DEVS, fs, d)

    hid = pl.BlockSpec((tk, tf), lambda c, k: (k, c))
    call = _slab_call(
        body, slab, 3, name=name, grid=(n_c, n_k),
        in_specs=[hid, hid, pl.BlockSpec((tk, d), lambda c, k: (k, 0))],
        out_specs=[pl.BlockSpec((FFN_CHUNK_DEVS, 1, 1, fs, d), lambda c, k: (c, l, 4 + which, 0, 0))],
        out_shape=[jax.ShapeDtypeStruct(slab_shape, BF16)],
        scratch_shapes=[pltpu.VMEM((tf, d), F32)],
        compiler_params=_params())
    return call(a, u, df)[0]


def _win_fwd(x, w_in, l, name):
    s, d = x.shape
    rs = w_in.shape[2]
    d_in = N_DEV * rs
    tm = min(s, 512)

    def body(x_ref, w_ref, out_ref):
        out_ref[...] = _dot_nt(x_ref[...].astype(BF16), w_ref[...].reshape(d_in, d))

    return pl.pallas_call(
        body, name=name, grid=(s // tm,),
        in_specs=[pl.BlockSpec((tm, d), lambda i: (i, 0)),
                  pl.BlockSpec((N_DEV, 1, rs, d), lambda i: (0, l, 0, 0))],
        out_specs=pl.BlockSpec((tm, d_in), lambda i: (i, 0)),
        out_shape=jax.ShapeDtypeStruct((s, d_in), F32), compiler_params=_params())(x, w_in)


def _wout_fwd(x, yab, yc, w_out, l, ln_g, ln_b, alpha, name):
    s, d = x.shape
    rs = w_out.shape[2]
    tm = min(s, 512)

    def body(x_ref, yab_ref, yc_ref, w_ref, g_ref, b_ref, z_ref, y_ref):
        mix = jnp.concatenate([yab_ref[...], yc_ref[...]], axis=1).astype(BF16)
        z = alpha * x_ref[...] + _dot(mix, w_ref[...].reshape(D_MIX, d))
        zhat, _ = _ln_stats(z)
        z_ref[...] = z
        y_ref[...] = zhat * g_ref[...] + b_ref[...]

    row = pl.BlockSpec((tm, d), lambda i: (i, 0))
    half = pl.BlockSpec((tm, D_MIX // 2), lambda i: (i, 0))
    vec = pl.BlockSpec((1, d), lambda i: (0, 0))
    return pl.pallas_call(
        body, name=name, grid=(s // tm,),
        in_specs=[row, half, half, pl.BlockSpec((N_DEV, 1, rs, d), lambda i: (0, l, 0, 0)), vec, vec],
        out_specs=[row, row], out_shape=[jax.ShapeDtypeStruct((s, d), F32)] * 2,
        compiler_params=_params())(x, yab, yc, w_out, ln_g, ln_b)


def _wout_bwd(dy, z, yab, yc, w_out, l, ln_g, alpha, slab, slab_shape, name):
    s, d = dy.shape
    rs = w_out.shape[2]
    tm = min(s, 512)
    n_i = s // tm

    def body(dy_ref, z_ref, yab_ref, yc_ref, w_ref, g_ref, dmix_ref, dxp_ref, dg_ref, db_ref, out_ref, acc_s):
        i = pl.program_id(0)
        dz, dg, db = _ln_bwd(dy_ref[...], z_ref[...], g_ref[...])
        _accumulate(dg_ref, dg, i == 0)
        _accumulate(db_ref, db, i == 0)
        dxp_ref[...] = alpha * dz
        dzb = dz.astype(BF16)
        dmix_ref[...] = _dot_nt(dzb, w_ref[...].reshape(D_MIX, d))
        mix = jnp.concatenate([yab_ref[...], yc_ref[...]], axis=1).astype(BF16)
        _accumulate(acc_s, _dot_tn(mix, dzb), i == 0)

        @pl.when(i == n_i - 1)
        def _():
            out_ref[:, 0] = acc_s[...].astype(BF16).reshape(N_DEV, rs, d)

    row = pl.BlockSpec((tm, d), lambda i: (i, 0))
    half = pl.BlockSpec((tm, D_MIX // 2), lambda i: (i, 0))
    vec = pl.BlockSpec((1, d), lambda i: (0, 0))
    wblk = pl.BlockSpec((N_DEV, 1, rs, d), lambda i: (0, l, 0, 0))
    call = _slab_call(
        body, slab, 6, name=name, grid=(n_i,),
        in_specs=[row, row, half, half, wblk, vec],
        out_specs=[pl.BlockSpec((tm, D_MIX), lambda i: (i, 0)), row, vec, vec, wblk],
        out_shape=[jax.ShapeDtypeStruct((s, D_MIX), F32), jax.ShapeDtypeStruct((s, d), F32),
                   jax.ShapeDtypeStruct((1, d), F32), jax.ShapeDtypeStruct((1, d), F32),
                   jax.ShapeDtypeStruct(slab_shape, BF16)],
        scratch_shapes=[pltpu.VMEM((D_MIX, d), F32)],
        compiler_params=_params())
    return call(dy, z, yab, yc, w_out, ln_g)


def _win_bwd(dxp, dloc, dq, dk, dv, x, w_in, l, slab, slab_shape, name):
    s, d = x.shape
    rs = w_in.shape[2]
    d_in = N_DEV * rs
    tm = min(s, 256)
    n_i = s // tm

    def body(dxp_ref, dloc_ref, dq_ref, dk_ref, dv_ref, x_ref, w_ref, dx_ref, out_ref, acc_s):
        i = pl.program_id(0)
        dp = jnp.concatenate([dloc_ref[...], dq_ref[...], dk_ref[...].astype(BF16), dv_ref[...].astype(BF16)], axis=1)
        dx_ref[...] = dxp_ref[...] + _dot(dp, w_ref[...].reshape(d_in, d))
        _accumulate(acc_s, _dot_tn(dp, x_ref[...].astype(BF16)), i == 0)

        @pl.when(i == n_i - 1)
        def _():
            out_ref[:, 0] = acc_s[...].astype(BF16).reshape(N_DEV, rs, d)

    row = pl.BlockSpec((tm, d), lambda i: (i, 0))
    na = pl.BlockSpec((tm, D_NA), lambda i: (i, 0))
    wblk = pl.BlockSpec((N_DEV, 1, rs, d), lambda i: (0, l, 0, 0))
    call = _slab_call(
        body, slab, 7, name=name, grid=(n_i,),
        in_specs=[row, pl.BlockSpec((tm, D_LOC), lambda i: (i, 0)), na, na, na, row, wblk],
        out_specs=[row, wblk],
        out_shape=[jax.ShapeDtypeStruct((s, d), F32), jax.ShapeDtypeStruct(slab_shape, BF16)],
        scratch_shapes=[pltpu.VMEM((d_in, d), F32)],
        compiler_params=_params())
    return call(dxp, dloc, dq, dk, dv, x, w_in)


def _shift_rows(v, k):
    n = v.shape[0]
    return pltpu.roll(v, k % n, 0)


def _halo_specs(tm, s, width, col):
    per = tm // HALO
    last = s // HALO - 1
    return [pl.BlockSpec((HALO, width), lambda i: (jnp.maximum(i * per - 1, 0), col)),
            pl.BlockSpec((tm, width), lambda i: (i, col)),
            pl.BlockSpec((HALO, width), lambda i: (jnp.minimum((i + 1) * per, last), col))]


def _token_index(i, tm):
    return i * tm - HALO + lax.broadcasted_iota(jnp.int32, (tm + 2 * HALO, 1), 0)


def _pool_lane_tables():
    lane = lax.broadcasted_iota(jnp.int32, (1, D_POOL), 1)
    group = sum((lane >= g * POOL_GROUP).astype(jnp.int32) for g in range(1, len(POOL_WINDOWS)))
    half = jnp.where(group == 0, 1, jnp.where(group == 1, 2, jnp.where(group == 2, 4, 8)))
    return group, half


def _window_sums(v, group, offsets):
    s2 = v + _shift_rows(v, 1)
    s4 = s2 + _shift_rows(s2, 2)
    s8 = s4 + _shift_rows(s4, 4)
    s16 = s8 + _shift_rows(s8, 8)
    parts = [_shift_rows(p, -o) if o else p for p, o in zip((s2, s4, s8, s16), offsets)]
    return jnp.where(group == 0, parts[0], jnp.where(group == 1, parts[1], jnp.where(group == 2, parts[2], parts[3])))


def _pool_counts(tok, half, s):
    return (jnp.minimum(tok + half, s) - jnp.maximum(tok - half, 0)).astype(F32)


def _pool_forward(u, tok, s):
    group, half = _pool_lane_tables()
    sums = _window_sums(u, group, [w // 2 - 1 for w in POOL_WINDOWS])
    return sums / _pool_counts(tok, half, s) - u


def _conv_forward(zc, cw_ref):
    return cw_ref[0:1, :] * _shift_rows(zc, 1) + cw_ref[1:2, :] * zc + cw_ref[2:3, :] * _shift_rows(zc, -1)


def _local_fwd(proj, pool_bd, pool_scale, conv_w, name):
    s = proj.shape[0]
    tm = min(s, 512)
    ctr = slice(HALO, HALO + tm)

    def body(prev_ref, cur_ref, next_ref, pw_ref, sc_ref, cw_ref, out_ref):
        i = pl.program_id(0)
        ext = jnp.concatenate([prev_ref[...], cur_ref[...], next_ref[...]], axis=0)
        tok = _token_index(i, tm)
        inside = (tok >= 0) & (tok < s)
        u = jnp.where(inside, ext[:, 0:D_POOL], 0.0)
        p = _pool_forward(u, tok, s)[ctr]
        ya = _dot(p.astype(BF16), pw_ref[...]) * sc_ref[...]
        gb = ext[:, D_POOL:D_POOL + D_CONV]
        zc = jnp.where(inside, ext[:, D_POOL + D_CONV:D_POOL + 2 * D_CONV] * ext[:, D_POOL + 2 * D_CONV:D_LOC], 0.0)
        yb = (gb * _conv_forward(zc, cw_ref))[ctr]
        out_ref[...] = jnp.concatenate([ya, yb], axis=1)

    return pl.pallas_call(
        body, name=name, grid=(s // tm,),
        in_specs=_halo_specs(tm, s, D_LOC, 0) + [
            pl.BlockSpec((D_POOL, D_POOL), lambda i: (0, 0)), pl.BlockSpec((1, D_POOL), lambda i: (0, 0)),
            pl.BlockSpec((3, D_CONV), lambda i: (0, 0))],
        out_specs=pl.BlockSpec((tm, D_POOL + D_CONV), lambda i: (i, 0)),
        out_shape=jax.ShapeDtypeStruct((s, D_POOL + D_CONV), F32),
        compiler_params=_params())(proj, proj, proj, pool_bd, pool_scale, conv_w)


def _local_bwd(proj, dmix, pool_bd, pool_scale, conv_w, name):
    s = proj.shape[0]
    tm = min(s, 512)
    ctr = slice(HALO, HALO + tm)

    def body(prev_ref, cur_ref, next_ref, dprev_ref, dcur_ref, dnext_ref, pw_ref, sc_ref, cw_ref,
             dloc_ref, dpw_ref, dsc_ref, dcw_ref):
        i = pl.program_id(0)
        first = i == 0
        ext = jnp.concatenate([prev_ref[...], cur_ref[...], next_ref[...]], axis=0)
        dext = jnp.concatenate([dprev_ref[...], dcur_ref[...], dnext_ref[...]], axis=0)
        tok = _token_index(i, tm)
        inside = (tok >= 0) & (tok < s)
        group, half = _pool_lane_tables()
        cnt = _pool_counts(tok, half, s)
        u = jnp.where(inside, ext[:, 0:D_POOL], 0.0)
        dya = jnp.where(inside, dext[:, 0:D_POOL], 0.0)
        p_c = _pool_forward(u, tok, s)[ctr].astype(BF16)
        lin = _dot(p_c, pw_ref[...])
        _accumulate(dsc_ref, jnp.sum(dya[ctr] * lin, axis=0, keepdims=True), first)
        e1 = (dya * sc_ref[...]).astype(BF16)
        _accumulate(dpw_ref, _dot_tn(p_c, e1[ctr]), first)
        dp = _dot_nt(e1, pw_ref[...])
        du = _window_sums(dp / cnt, group, [w // 2 for w in POOL_WINDOWS]) - dp
        gb = ext[:, D_POOL:D_POOL + D_CONV]
        gc = ext[:, D_POOL + D_CONV:D_POOL + 2 * D_CONV]
        hv = ext[:, D_POOL + 2 * D_CONV:D_LOC]
        zc = jnp.where(inside, gc * hv, 0.0)
        dyb = jnp.where(inside, dext[:, D_POOL:D_POOL + D_CONV], 0.0)
        dgb = dyb * _conv_forward(zc, cw_ref)
        dyc = dyb * gb
        for k in range(3):
            part = jnp.sum(dyc[ctr] * _shift_rows(zc, 1 - k)[ctr], axis=0, keepdims=True)
            _accumulate(dcw_ref.at[k:k + 1, :], part, first)
        dzc = cw_ref[0:1, :] * _shift_rows(dyc, -1) + cw_ref[1:2, :] * dyc + cw_ref[2:3, :] * _shift_rows(dyc, 1)
        dloc = jnp.concatenate([du, dgb, dzc * hv, dzc * gc], axis=1)
        dloc_ref[...] = dloc[ctr].astype(BF16)

    return pl.pallas_call(
        body, name=name, grid=(s // tm,),
        in_specs=_halo_specs(tm, s, D_LOC, 0) + _halo_specs(tm, s, D_POOL + D_CONV, 0) + [
            pl.BlockSpec((D_POOL, D_POOL), lambda i: (0, 0)), pl.BlockSpec((1, D_POOL), lambda i: (0, 0)),
            pl.BlockSpec((3, D_CONV), lambda i: (0, 0))],
        out_specs=[pl.BlockSpec((tm, D_LOC), lambda i: (i, 0)), pl.BlockSpec((D_POOL, D_POOL), lambda i: (0, 0)),
                   pl.BlockSpec((1, D_POOL), lambda i: (0, 0)), pl.BlockSpec((8, D_CONV), lambda i: (0, 0))],
        out_shape=[jax.ShapeDtypeStruct((s, D_LOC), BF16), jax.ShapeDtypeStruct((D_POOL, D_POOL), F32),
                   jax.ShapeDtypeStruct((1, D_POOL), F32), jax.ShapeDtypeStruct((8, D_CONV), F32)],
        compiler_params=_params())(proj, proj, proj, dmix, dmix, dmix, pool_bd, pool_scale, conv_w)


def _na_geometry(rows):
    n_j = rows // Q_ROWS
    dr = np.full((3, Q_ROWS, K_ROWS), 2 * NA_ROWS - 1, np.int64)
    for t, j in enumerate((0, min(1, n_j - 1), n_j - 1)):
        base = int(np.clip(Q_ROWS * j - NA_ROWS // 2, 0, rows - K_ROWS))
        for qr in range(Q_ROWS):
            r = Q_ROWS * j + qr
            start = int(np.clip(r - NA_ROWS // 2, 0, rows - NA_ROWS))
            for kr in range(K_ROWS):
                if start <= base + kr < start + NA_ROWS:
                    dr[t, qr, kr] = base + kr - r + NA_ROWS - 1
    return dr


def _na_col_tables():
    c = np.arange(GRID_W)
    start = np.clip(c - NA_COLS // 2, 0, GRID_W - NA_COLS)
    valid = (c[None, :] >= start[:, None]) & (c[None, :] < start[:, None] + NA_COLS)
    dc = np.clip(c[None, :] - c[:, None], -(NA_COLS - 1), NA_COLS - 1) + (NA_COLS - 1)
    return valid, dc


def _na_bias(rpb, rows):
    valid, dc = _na_col_tables()
    onehot = jnp.asarray((dc[None] == np.arange(2 * NA_COLS - 1)[:, None, None]).astype(np.float32))
    table = jnp.einsum("hrd,dqk->hrqk", rpb, onehot, precision=lax.Precision.HIGHEST)
    table = jnp.where(jnp.asarray(valid)[None, None], table, NEG_INF)
    outside = jnp.full((NA_HEADS, GRID_W, GRID_W), NEG_INF, F32)
    dr = _na_geometry(rows)
    blocks = []
    for t in range(3):
        strips = []
        for qr in range(Q_ROWS):
            strips.append(jnp.concatenate(
                [outside if dr[t, qr, kr] == 2 * NA_ROWS - 1 else table[:, dr[t, qr, kr]] for kr in range(K_ROWS)], axis=2))
        blocks.append(jnp.concatenate(strips, axis=1))
    return jnp.stack(blocks)


def _na_specs(s, proj_cols):
    n_blk = s // K_BLK
    per = Q_TOK // K_BLK

    def kv_spec(col0, m):
        return pl.BlockSpec((K_BLK, HEAD_PAIR), lambda hp, j: (jnp.clip(per * j - 1, 0, n_blk - 4) + m, col0 + hp))

    q_col, k_col, v_col = (c // HEAD_PAIR for c in proj_cols)
    return ([pl.BlockSpec((Q_TOK, HEAD_PAIR), lambda hp, j: (j, q_col + hp))]
            + [kv_spec(k_col, m) for m in range(4)] + [kv_spec(v_col, m) for m in range(4)])


def _na_block_type(j, n_j):
    return jnp.where(j == 0, 0, jnp.where(j == n_j - 1, 2, 1))


def _head_masks():
    lane = lax.broadcasted_iota(jnp.int32, (1, HEAD_PAIR), 1)
    return [lane < NA_HEAD_DIM, lane >= NA_HEAD_DIM]


def _attn_fwd(proj, bias, name):
    s = proj.shape[0]
    n_j = s // Q_TOK
    scale = NA_HEAD_DIM ** -0.5

    def body(q_ref, k0, k1, k2, k3, v0, v1, v2, v3, bias_ref, o_ref, lse_ref):
        q = q_ref[...]
        kb = jnp.concatenate([r[...] for r in (k0, k1, k2, k3)], axis=0).astype(BF16)
        v = jnp.concatenate([r[...] for r in (v0, v1, v2, v3)], axis=0)
        out = jnp.zeros((Q_TOK, HEAD_PAIR), F32)
        lse = []
        for hh, mask in enumerate(_head_masks()):
            sc = _dot_nt(jnp.where(mask, q, 0.0).astype(BF16), kb) * scale + bias_ref[0, hh]
            mx = jnp.max(sc, axis=-1, keepdims=True)
            p = jnp.exp(sc - mx)
            den = jnp.sum(p, axis=-1, keepdims=True)
            out = out + _dot((p / den).astype(BF16), jnp.where(mask, v, 0.0).astype(BF16))
            lse.append(mx + jnp.log(den))
        o_ref[...] = out
        lse_ref[0] = jnp.where(_head_masks()[0], lse[0], lse[1])

    return pl.pallas_call(
        body, name=name, grid=(NA_HEADS // 2, n_j),
        in_specs=_na_specs(s, (D_LOC, D_LOC + D_NA, D_LOC + 2 * D_NA)) + [
            pl.BlockSpec((1, 2, Q_TOK, K_TOK), lambda hp, j: (_na_block_type(j, n_j), hp, 0, 0))],
        out_specs=[pl.BlockSpec((Q_TOK, HEAD_PAIR), lambda hp, j: (j, hp)),
                   pl.BlockSpec((1, Q_TOK, HEAD_PAIR), lambda hp, j: (hp, j, 0))],
        out_shape=[jax.ShapeDtypeStruct((s, D_NA), F32), jax.ShapeDtypeStruct((NA_HEADS // 2, s, HEAD_PAIR), F32)],
        compiler_params=_params())(*([proj] * 9), bias)


def _attn_bwd(proj, bias, o, dmix, lse, name):
    s = proj.shape[0]
    n_j = s // Q_TOK
    n_blk = s // K_BLK
    per = Q_TOK // K_BLK
    scale = NA_HEAD_DIM ** -0.5
    do_col = (D_POOL + D_CONV) // HEAD_PAIR

    def body(q_ref, k0, k1, k2, k3, v0, v1, v2, v3, bias_ref, o_ref, do_ref, lse_ref,
             dq_ref, dk_ref, dv_ref, dbias_ref):
        j = pl.program_id(1)

        @pl.when(j == 0)
        def _():
            dk_ref[...] = jnp.zeros_like(dk_ref)
            dv_ref[...] = jnp.zeros_like(dv_ref)

        new_type = (j == 0) | (j == 1) | (j == n_j - 1)
        base = pl.multiple_of(jnp.clip(per * j - 1, 0, n_blk - 4) * K_BLK, K_BLK)
        q = q_ref[...]
        k = jnp.concatenate([r[...] for r in (k0, k1, k2, k3)], axis=0)
        vb = jnp.concatenate([r[...] for r in (v0, v1, v2, v3)], axis=0).astype(BF16)
        kb = k.astype(BF16)
        do = do_ref[...]
        ov = o_ref[...]
        lse = lse_ref[0]
        dq = jnp.zeros((Q_TOK, HEAD_PAIR), F32)
        dk = jnp.zeros((K_TOK, HEAD_PAIR), F32)
        dv = jnp.zeros((K_TOK, HEAD_PAIR), F32)
        lane = lax.broadcasted_iota(jnp.int32, (1, HEAD_PAIR), 1)
        for hh, mask in enumerate(_head_masks()):
            qh = jnp.where(mask, q, 0.0).astype(BF16)
            doh = jnp.where(mask, do, 0.0)
            dob = doh.astype(BF16)
            lse_h = jnp.sum(jnp.where(lane == hh * NA_HEAD_DIM, lse, 0.0), axis=-1, keepdims=True)
            p = jnp.exp(_dot_nt(qh, kb) * scale + bias_ref[0, hh] - lse_h)
            delta = jnp.sum(doh * ov, axis=-1, keepdims=True)
            ds = p * (_dot_nt(dob, vb) - delta)
            _accumulate(dbias_ref.at[0, hh], ds, new_type)
            dsb = ds.astype(BF16)
            dq = dq + _dot(dsb, jnp.where(mask, k, 0.0).astype(BF16))
            dk = dk + _dot_tn(dsb, qh)
            dv = dv + _dot_tn(p.astype(BF16), dob)
        dq_ref[...] = (dq * scale).astype(BF16)
        dk_ref[pl.ds(base, K_TOK), :] += dk * scale
        dv_ref[pl.ds(base, K_TOK), :] += dv

    pair = pl.BlockSpec((Q_TOK, HEAD_PAIR), lambda hp, j: (j, hp))
    whole = pl.BlockSpec((s, HEAD_PAIR), lambda hp, j: (0, hp))
    bias_spec = pl.BlockSpec((1, 2, Q_TOK, K_TOK), lambda hp, j: (_na_block_type(j, n_j), hp, 0, 0))
    return pl.pallas_call(
        body, name=name, grid=(NA_HEADS // 2, n_j),
        in_specs=_na_specs(s, (D_LOC, D_LOC + D_NA, D_LOC + 2 * D_NA)) + [
            bias_spec, pair, pl.BlockSpec((Q_TOK, HEAD_PAIR), lambda hp, j: (j, do_col + hp)),
            pl.BlockSpec((1, Q_TOK, HEAD_PAIR), lambda hp, j: (hp, j, 0))],
        out_specs=[pair, whole, whole, bias_spec],
        out_shape=[jax.ShapeDtypeStruct((s, D_NA), BF16), jax.ShapeDtypeStruct((s, D_NA), F32),
                   jax.ShapeDtypeStruct((s, D_NA), F32), jax.ShapeDtypeStruct(bias.shape, F32)],
        compiler_params=_params())(*([proj] * 9), bias, o, dmix, lse)


def _rpb_reduce(dbias, rows, name):
    dr = _na_geometry(rows)
    n_j = rows // Q_ROWS
    used = [0] + ([1] if n_j > 2 else []) + ([2] if n_j > 1 else [])
    n_slot = 2 * NA_ROWS

    def body(in_ref, out_ref):
        slots = [None] * n_slot
        for t in used:
            for qr in range(Q_ROWS):
                for kp in range(K_ROWS // 2):
                    even, odd = dr[t, qr, 2 * kp], dr[t, qr, 2 * kp + 1]
                    if even == 2 * NA_ROWS - 1 and odd == 2 * NA_ROWS - 1:
                        continue
                    e = (even if even != 2 * NA_ROWS - 1 else odd - 1) + 1
                    tile = in_ref[t, 0, qr * GRID_W:(qr + 1) * GRID_W, kp * 2 * GRID_W:(kp + 1) * 2 * GRID_W]
                    slots[e] = tile if slots[e] is None else slots[e] + tile
        for e in range(n_slot):
            out_ref[0, e] = jnp.zeros((GRID_W, 2 * GRID_W), F32) if slots[e] is None else slots[e]

    return pl.pallas_call(
        body, name=name, grid=(NA_HEADS,),
        in_specs=[pl.BlockSpec((3, 1, Q_TOK, K_TOK), lambda h: (0, h, 0, 0))],
        out_specs=pl.BlockSpec((1, n_slot, GRID_W, 2 * GRID_W), lambda h: (h, 0, 0, 0)),
        out_shape=jax.ShapeDtypeStruct((NA_HEADS, n_slot, GRID_W, 2 * GRID_W), F32),
        compiler_params=_params())(dbias)


def _rpb_finish(tiles, name):
    valid, dc = _na_col_tables()
    n_dc = 2 * NA_COLS - 1
    sel = np.zeros((GRID_W, 2 * GRID_W, LANES), np.float32)
    for qc in range(GRID_W):
        for kc in range(GRID_W):
            if valid[qc, kc]:
                sel[qc, kc, dc[qc, kc]] = 1.0
                sel[qc, GRID_W + kc, LANES // 2 + dc[qc, kc]] = 1.0
    sel = jnp.asarray(sel.reshape(GRID_W * 2 * GRID_W, LANES))
    flat = tiles.reshape(NA_HEADS * 2 * NA_ROWS, GRID_W * 2 * GRID_W)

    def body(a_ref, b_ref, out_ref):
        out_ref[...] = jnp.dot(a_ref[...], b_ref[...], preferred_element_type=F32, precision=lax.Precision.HIGHEST)

    sums = pl.pallas_call(
        body, name=name, out_shape=jax.ShapeDtypeStruct((flat.shape[0], LANES), F32),
        compiler_params=_params())(flat, sel).reshape(NA_HEADS, 2 * NA_ROWS, LANES)
    return sums[:, 1:, :n_dc] + sums[:, :2 * NA_ROWS - 1, LANES // 2:LANES // 2 + n_dc]


def _loss_grad(y, target, name):
    s, d = y.shape
    tm = min(s, 1024)

    def body(y_ref, t_ref, sum_ref, dy_ref):
        diff = y_ref[...] - t_ref[...]
        dy_ref[...] = diff * (1.0 / d)
        part = jnp.zeros((8, LANES), F32) + jnp.sum(diff * diff)
        _accumulate(sum_ref, part, pl.program_id(0) == 0)

    row = pl.BlockSpec((tm, d), lambda i: (i, 0))
    return pl.pallas_call(
        body, name=name, grid=(s // tm,), in_specs=[row, row],
        out_specs=[pl.BlockSpec((8, LANES), lambda i: (0, 0)), row],
        out_shape=[jax.ShapeDtypeStruct((8, LANES), F32), jax.ShapeDtypeStruct((s, d), F32)],
        compiler_params=_params())(y, target)


def _adamw(w, g, m, v, name):
    rows, cols = w.shape
    tr = _row_tile(rows, 512, 8)

    def body(w_ref, g_ref, m_ref, v_ref, d_ref, nm_ref, nv_ref):
        gv = g_ref[...]
        nm = ADAM_B1 * m_ref[...] + (1.0 - ADAM_B1) * gv
        nv = ADAM_B2 * v_ref[...] + (1.0 - ADAM_B2) * (gv * gv)
        m_hat = nm / (1.0 - ADAM_B1 ** ADAM_STEP)
        v_hat = nv / (1.0 - ADAM_B2 ** ADAM_STEP)
        d_ref[...] = -ADAM_LR * (m_hat / (jnp.sqrt(v_hat) + ADAM_EPS) + ADAM_WD * w_ref[...])
        nm_ref[...] = nm
        nv_ref[...] = nv

    blk = pl.BlockSpec((tr, cols), lambda r: (r, 0))
    return pl.pallas_call(
        body, name=name, grid=(rows // tr,), in_specs=[blk] * 4, out_specs=[blk] * 3,
        out_shape=[jax.ShapeDtypeStruct((rows, cols), F32)] * 3, compiler_params=_params())(w, g, m, v)


def _adamw_nd(w, g, m, v, name):
    shape = w.shape
    flat = lambda t: t.reshape(-1, shape[-1])
    return tuple(t.reshape(shape) for t in _adamw(flat(w), flat(g), flat(m), flat(v), name))


def _pack(parts, rows_mult=64):
    flat = jnp.concatenate([p.reshape(-1).astype(F32) for p in parts])
    per = LANES * rows_mult
    total = -(-flat.shape[0] // per) * per
    return jnp.pad(flat, (0, total - flat.shape[0])).reshape(-1, LANES)


def _unpack(packed, shapes):
    flat = packed.reshape(-1)
    out, pos = [], 0
    for shp in shapes:
        n = int(np.prod(shp))
        out.append(flat[pos:pos + n].reshape(shp))
        pos += n
    return out


def kernel(x, ffn1_w_gate, ffn1_w_up, ffn1_w_down, ffn2_w_gate, ffn2_w_up, ffn2_w_down, w_in, pool_w, pool_scale, conv_w, rpb, w_out, ln_g, ln_b, loss_target, m_ffn1_w_gate, m_ffn1_w_up, m_ffn1_w_down, m_ffn2_w_gate, m_ffn2_w_up, m_ffn2_w_down, m_w_in, m_pool_w, m_pool_scale, m_conv_w, m_rpb, m_w_out, m_ln_g, m_ln_b, v_ffn1_w_gate, v_ffn1_w_up, v_ffn1_w_down, v_ffn2_w_gate, v_ffn2_w_up, v_ffn2_w_down, v_w_in, v_pool_w, v_pool_scale, v_conv_w, v_rpb, v_w_out, v_ln_g, v_ln_b):
    n_l, d, fs = ffn1_w_gate.shape
    s = x.shape[1]
    rows = s // GRID_W
    assert x.shape[0] == 1 and s % Q_TOK == 0 and rows >= K_ROWS and fs % BF16_ROWS == 0
    alpha = (2.0 * n_l) ** 0.25
    xi, yi, ci = _mesh_pos()
    me = 4 * xi + 2 * yi + ci
    core = jnp.reshape(ci, (1,)).astype(jnp.int32)
    ln_w, cw_w = ln_g.shape[2], conv_w.shape[2]

    tr = lambda w: jnp.swapaxes(w, 1, 2)
    ffn_shard = jnp.stack([tr(ffn1_w_gate), tr(ffn1_w_up), tr(ffn2_w_gate), tr(ffn2_w_up), ffn1_w_down, ffn2_w_down],
                          axis=1).astype(BF16)
    small_shard = _pack([ln_g, ln_b, conv_w])
    w_ffn, w_in_t, w_out_r, small = _all_gather(
        [ffn_shard, tr(w_in).astype(BF16), w_out.astype(BF16), small_shard], "gather_weights")
    n_ln = n_l * 3 * ln_w
    small = small.reshape(N_DEV, -1)
    unshard = lambda t, width: jnp.moveaxis(t.reshape(N_DEV, n_l, 3, width), 0, 2).reshape(n_l, 3, N_DEV * width)
    ln_g_all = unshard(small[:, :n_ln], ln_w)
    ln_b_all = unshard(small[:, n_ln:2 * n_ln], ln_w)
    conv_all = unshard(small[:, 2 * n_ln:2 * n_ln + n_l * 3 * cw_w], cw_w)
    pool_bd = jnp.zeros((n_l, D_POOL, D_POOL), F32)
    for g in range(len(POOL_WINDOWS)):
        sl = slice(g * POOL_GROUP, (g + 1) * POOL_GROUP)
        pool_bd = pool_bd.at[:, sl, sl].set(pool_w[:, g])
    pool_bd = pool_bd.astype(BF16)
    lnp = lambda arr, l, j: arr[l, j].reshape(1, d)

    saved = []
    h = x.reshape(s, d)
    for l in range(n_l):
        a1, u1, z1, x1 = _ffn_fwd(h, w_ffn, l, 0, lnp(ln_g_all, l, 0), lnp(ln_b_all, l, 0), alpha, f"ffn1_fwd_{l}")
        proj = _win_fwd(x1, w_in_t, l, f"win_fwd_{l}")
        bias = _na_bias(rpb[l], rows)
        yab = _local_fwd(proj, pool_bd[l], pool_scale[l].reshape(1, D_POOL), conv_all[l], f"local_fwd_{l}")
        yc, lse = _attn_fwd(proj, bias, f"attn_fwd_{l}")
        z2, x2 = _wout_fwd(x1, yab, yc, w_out_r, l, lnp(ln_g_all, l, 1), lnp(ln_b_all, l, 1), alpha, f"wout_fwd_{l}")
        a2, u2, z3, x3 = _ffn_fwd(x2, w_ffn, l, 1, lnp(ln_g_all, l, 2), lnp(ln_b_all, l, 2), alpha, f"ffn2_fwd_{l}")
        saved.append((h, a1, u1, z1, x1, proj, bias, yab, yc, lse, z2, x2, a2, u2, z3))
        h = x3

    sq, dh = _loss_grad(h, loss_target.reshape(s, d), "loss_head")
    loss = lax.psum(sq[0, 0] * (0.5 / d), MESH_AXES)

    g_ffn, g_in, g_out = None, None, None
    small_grads = [None] * n_l
    for l in reversed(range(n_l)):
        x0, a1, u1, z1, x1, proj, bias, yab, yc, lse, z2, x2, a2, u2, z3 = saved[l]
        dx2, da, du, df, dg3, db3 = _ffn_bwd_dx(dh, z3, a2, u2, w_ffn, l, 1, lnp(ln_g_all, l, 2), alpha, f"ffn2_bwd_dx_{l}")
        g_ffn = _ffn_bwd_dwgu(da, du, x2, g_ffn, w_ffn.shape, l, 1, f"ffn2_bwd_dwgu_{l}")
        g_ffn = _ffn_bwd_dwd(a2, u2, df, g_ffn, w_ffn.shape, l, 1, f"ffn2_bwd_dwd_{l}")
        dmix, dxp, dg2, db2, g_out = _wout_bwd(dx2, z2, yab, yc, w_out_r, l, lnp(ln_g_all, l, 1), alpha,
                                               g_out, w_out_r.shape, f"wout_bwd_{l}")
        dq, dk, dv, dbias = _attn_bwd(proj, bias, yc, dmix, lse, f"attn_bwd_{l}")
        dloc, dpw, dsc, dcw = _local_bwd(proj, dmix, pool_bd[l], pool_scale[l].reshape(1, D_POOL), conv_all[l],
                                         f"local_bwd_{l}")
        dx1, g_in = _win_bwd(dxp, dloc, dq, dk, dv, x1, w_in_t, l, g_in, w_in_t.shape, f"win_bwd_{l}")
        dx0, da, du, df, dg1, db1 = _ffn_bwd_dx(dx1, z1, a1, u1, w_ffn, l, 0, lnp(ln_g_all, l, 0), alpha, f"ffn1_bwd_dx_{l}")
        g_ffn = _ffn_bwd_dwgu(da, du, x0, g_ffn, w_ffn.shape, l, 0, f"ffn1_bwd_dwgu_{l}")
        g_ffn = _ffn_bwd_dwd(a1, u1, df, g_ffn, w_ffn.shape, l, 0, f"ffn1_bwd_dwd_{l}")
        drpb = _rpb_finish(_rpb_reduce(dbias, rows, f"rpb_reduce_{l}"), f"rpb_finish_{l}")
        dpool = jnp.stack([dpw[g * POOL_GROUP:(g + 1) * POOL_GROUP, g * POOL_GROUP:(g + 1) * POOL_GROUP]
                           for g in range(len(POOL_WINDOWS))])
        small_grads[l] = (jnp.concatenate([dg1, dg2, dg3]), jnp.concatenate([db1, db2, db3]), dcw[0:3], dpool, dsc[0], drpb)
        dh = dx0
    grad_x = dh.reshape(x.shape)

    slabs = [g_ffn.reshape(N_DEV, -1, d), g_in.reshape(N_DEV, -1, d), g_out.reshape(N_DEV, -1, d)]
    got = _pair_exchange(slabs, "grads_pair_exchange")
    pair = [_pair_add(sl, gt, core, f"grads_pair_add_{i}") for i, (sl, gt) in enumerate(zip(slabs, got))]
    chip = _chip_exchange(pair, "grads_chip_exchange")
    r_ffn, r_in, r_out = [_sum_blocks(c, f"grads_chip_sum_{i}") for i, c in enumerate(chip)]
    r_ffn = r_ffn.reshape(n_l, 6, fs, d)
    grads = {
        "ffn1_w_gate": tr(r_ffn[:, 0]), "ffn1_w_up": tr(r_ffn[:, 1]), "ffn2_w_gate": tr(r_ffn[:, 2]),
        "ffn2_w_up": tr(r_ffn[:, 3]), "ffn1_w_down": r_ffn[:, 4], "ffn2_w_down": r_ffn[:, 5],
        "w_in": tr(r_in.reshape(n_l, -1, d)), "w_out": r_out.reshape(n_l, -1, d)}

    stack = lambda k: jnp.stack([small_grads[l][k] for l in range(n_l)])
    small_shapes = [(n_l, 3, d), (n_l, 3, d), (n_l, 3, D_CONV), pool_w.shape, pool_scale.shape, rpb.shape]
    (small_all,) = _all_gather([_pack([stack(k) for k in range(6)])], "gather_small_grads")
    small_sum = _sum_blocks(small_all, "small_grads_sum")
    g_ln_g, g_ln_b, g_conv, g_pool_w, g_pool_scale, g_rpb = _unpack(small_sum, small_shapes)
    own = lambda t, width: lax.dynamic_slice_in_dim(t, me * width, width, axis=2)
    grads.update({"ln_g": own(g_ln_g, ln_w), "ln_b": own(g_ln_b, ln_w), "conv_w": own(g_conv, cw_w),
                  "pool_w": g_pool_w, "pool_scale": g_pool_scale, "rpb": g_rpb})

    weights = dict(ffn1_w_gate=ffn1_w_gate, ffn1_w_up=ffn1_w_up, ffn1_w_down=ffn1_w_down, ffn2_w_gate=ffn2_w_gate,
                   ffn2_w_up=ffn2_w_up, ffn2_w_down=ffn2_w_down, w_in=w_in, pool_w=pool_w, pool_scale=pool_scale,
                   conv_w=conv_w, rpb=rpb, w_out=w_out, ln_g=ln_g, ln_b=ln_b)
    m_in = dict(ffn1_w_gate=m_ffn1_w_gate, ffn1_w_up=m_ffn1_w_up, ffn1_w_down=m_ffn1_w_down, ffn2_w_gate=m_ffn2_w_gate,
                ffn2_w_up=m_ffn2_w_up, ffn2_w_down=m_ffn2_w_down, w_in=m_w_in, pool_w=m_pool_w, pool_scale=m_pool_scale,
                conv_w=m_conv_w, rpb=m_rpb, w_out=m_w_out, ln_g=m_ln_g, ln_b=m_ln_b)
    v_in = dict(ffn1_w_gate=v_ffn1_w_gate, ffn1_w_up=v_ffn1_w_up, ffn1_w_down=v_ffn1_w_down, ffn2_w_gate=v_ffn2_w_gate,
                ffn2_w_up=v_ffn2_w_up, ffn2_w_down=v_ffn2_w_down, w_in=v_w_in, pool_w=v_pool_w, pool_scale=v_pool_scale,
                conv_w=v_conv_w, rpb=v_rpb, w_out=v_w_out, ln_g=v_ln_g, ln_b=v_ln_b)
    names = list(weights)
    large = ["ffn1_w_gate", "ffn1_w_up", "ffn1_w_down", "ffn2_w_gate", "ffn2_w_up", "ffn2_w_down", "w_in", "w_out"]
    tiny = [n for n in names if n not in large]
    delta, new_m, new_v = {}, {}, {}
    for n in large:
        delta[n], new_m[n], new_v[n] = _adamw_nd(weights[n], grads[n], m_in[n], v_in[n], f"adamw_{n}")
    packed = [_pack([t[n] for n in tiny]) for t in (weights, grads, m_in, v_in)]
    tiny_out = _adamw(*packed, "adamw_small")
    tiny_shapes = [weights[n].shape for n in tiny]
    for res, t in zip((delta, new_m, new_v), tiny_out):
        res.update(dict(zip(tiny, _unpack(t, tiny_shapes))))

    return (loss, grad_x, *[grads[n] for n in names], *[delta[n] for n in names],
            *[new_m[n] for n in names], *[new_v[n] for n in names])
```

```python
import functools

import numpy as np
import jax
import jax.numpy as jnp
from jax import lax
from jax.experimental import pallas as pl
from jax.experimental.pallas import tpu as pltpu

F32, BF16 = jnp.float32, jnp.bfloat16
MESH = pl.DeviceIdType.MESH
N_DEV = 8
MESH_AXES = ("x", "y", "c")

LN_EPS = 1e-5
NEG_INF = -1e30
D_POOL = 256
POOL_WINDOWS = (2, 4, 8, 16)
POOL_GROUP = 64
D_CONV = 256
NA_HEADS = 8
NA_HEAD_DIM = 64
D_NA = NA_HEADS * NA_HEAD_DIM
GRID_W = 64
NA_ROWS = 8
NA_COLS = 16
D_LOC = D_POOL + 3 * D_CONV
D_MIX = D_POOL + D_CONV + D_NA
ADAM_LR, ADAM_B1, ADAM_B2, ADAM_EPS, ADAM_WD, ADAM_STEP = 0.001, 0.9, 0.999, 1e-08, 0.01, 10

VMEM_LIMIT_BYTES = 56 * 1024 * 1024
LANES = 128
BF16_ROWS = 16
HALO = 16
Q_ROWS = 8
K_ROWS = 16
Q_TOK = Q_ROWS * GRID_W
K_TOK = K_ROWS * GRID_W
K_BLK = 4 * GRID_W
HEAD_PAIR = 2 * NA_HEAD_DIM
FFN_CHUNK_DEVS = 4

NT = (((1,), (1,)), ((), ()))
TN = (((0,), (0,)), ((), ()))


def _dot(a, b):
    return jnp.dot(a, b, preferred_element_type=F32)


def _dot_nt(a, b):
    return lax.dot_general(a, b, NT, preferred_element_type=F32)


def _dot_tn(a, b):
    return lax.dot_general(a, b, TN, preferred_element_type=F32)


def _params():
    return pltpu.CompilerParams(vmem_limit_bytes=VMEM_LIMIT_BYTES)


def _row_tile(rows, pref, mult=BF16_ROWS):
    t = min(rows, pref)
    t -= t % mult
    while t > mult and rows % t:
        t -= mult
    assert t > 0 and rows % t == 0, (rows, pref)
    return t


def _mesh_pos():
    return tuple(lax.axis_index(a) for a in MESH_AXES)


def _any_spec():
    return pl.BlockSpec(memory_space=pl.ANY)


def _all_gather(shards, name):
    n = len(shards)

    def body(*refs):
        ins, outs = refs[:n], refs[n:2 * n]
        send_sems, recv_sems, local_sems = refs[2 * n:]
        x, y, c = _mesh_pos()
        me, sibling = (x, y, c), (x, y, 1 - c)
        chips = [(1 - x, y), (x, 1 - y), (1 - x, 1 - y)]

        def copy(a, k, block, to, src=None):
            dst = outs[a].at[4 * block[0] + 2 * block[1] + block[2]]
            return pltpu.make_async_remote_copy(
                src_ref=dst if src is None else src, dst_ref=dst,
                send_sem=send_sems.at[a, k], recv_sem=recv_sems.at[a, k],
                device_id=to, device_id_type=MESH)

        mine = [pltpu.make_async_copy(ins[a], outs[a].at[4 * x + 2 * y + c], local_sems.at[a]) for a in range(n)]
        for cp in mine:
            cp.start()
        first = []
        for a in range(n):
            first.append(copy(a, 0, me, sibling, src=ins[a]))
            first += [copy(a, 1 + j, me, (*chip, c), src=ins[a]) for j, chip in enumerate(chips)]
        for cp in first:
            cp.start()
        passed = []
        for j, chip in enumerate(chips):
            for a in range(n):
                copy(a, 1 + j, (*chip, c), me).wait_recv()
                cp = copy(a, 4 + j, (*chip, c), sibling)
                cp.start()
                passed.append(cp)
        for a in range(n):
            copy(a, 0, sibling, me).wait_recv()
            for j, chip in enumerate(chips):
                copy(a, 4 + j, (*chip, 1 - c), me).wait_recv()
        for cp in first + passed:
            cp.wait_send()
        for cp in mine:
            cp.wait()

    return pl.pallas_call(
        body, name=name,
        out_shape=[jax.ShapeDtypeStruct((N_DEV,) + s.shape, s.dtype) for s in shards],
        in_specs=[_any_spec()] * n, out_specs=[_any_spec()] * n,
        scratch_shapes=[pltpu.SemaphoreType.DMA((n, 7)), pltpu.SemaphoreType.DMA((n, 7)),
                        pltpu.SemaphoreType.DMA((n,))],
    )(*shards)


def _pair_exchange(slabs, name):
    n = len(slabs)

    def body(*refs):
        ins, outs = refs[:n], refs[n:2 * n]
        send_sems, recv_sems = refs[2 * n:]
        x, y, c = _mesh_pos()
        copies = [
            pltpu.make_async_remote_copy(
                src_ref=ins[a].at[2 * j + 1 - c], dst_ref=outs[a].at[j],
                send_sem=send_sems.at[a, j], recv_sem=recv_sems.at[a, j],
                device_id=(x, y, 1 - c), device_id_type=MESH)
            for a in range(n) for j in range(4)]
        for cp in copies:
            cp.start()
        for cp in copies:
            cp.wait_send()
        for cp in copies:
            cp.wait_recv()

    return pl.pallas_call(
        body, name=name,
        out_shape=[jax.ShapeDtypeStruct((4,) + s.shape[1:], s.dtype) for s in slabs],
        in_specs=[_any_spec()] * n, out_specs=[_any_spec()] * n,
        scratch_shapes=[pltpu.SemaphoreType.DMA((n, 4)), pltpu.SemaphoreType.DMA((n, 4))],
    )(*slabs)


def _chip_exchange(parts, name):
    n = len(parts)

    def body(*refs):
        ins, outs = refs[:n], refs[n:2 * n]
        send_sems, recv_sems, local_sems = refs[2 * n:]
        x, y, c = _mesh_pos()
        my_chip = 2 * x + y
        chips = [(1 - x, y), (x, 1 - y), (1 - x, 1 - y)]
        own = [pltpu.make_async_copy(ins[a].at[my_chip], outs[a].at[my_chip], local_sems.at[a]) for a in range(n)]
        for cp in own:
            cp.start()

        def copy(a, k, src_chip, dst_chip, to):
            return pltpu.make_async_remote_copy(
                src_ref=ins[a].at[src_chip], dst_ref=outs[a].at[dst_chip],
                send_sem=send_sems.at[a, k], recv_sem=recv_sems.at[a, k],
                device_id=to, device_id_type=MESH)

        sends = [copy(a, k, 2 * px + py, my_chip, (px, py, c)) for a in range(n) for k, (px, py) in enumerate(chips)]
        for cp in sends:
            cp.start()
        for cp in sends:
            cp.wait_send()
        for a in range(n):
            for k, (px, py) in enumerate(chips):
                copy(a, k, my_chip, 2 * px + py, (px, py, c)).wait_recv()
        for cp in own:
            cp.wait()

    return pl.pallas_call(
        body, name=name,
        out_shape=[jax.ShapeDtypeStruct(s.shape, s.dtype) for s in parts],
        in_specs=[_any_spec()] * n, out_specs=[_any_spec()] * n,
        scratch_shapes=[pltpu.SemaphoreType.DMA((n, 3)), pltpu.SemaphoreType.DMA((n, 3)),
                        pltpu.SemaphoreType.DMA((n,))],
    )(*parts)


def _pair_add(slab, got, core, name):
    _, rows, d = slab.shape
    tr = _row_tile(rows, 1024)

    def body(core_ref, mine_ref, got_ref, out_ref):
        out_ref[...] = (mine_ref[...].astype(F32) + got_ref[...].astype(F32)).astype(out_ref.dtype)

    grid_spec = pltpu.PrefetchScalarGridSpec(
        num_scalar_prefetch=1, grid=(4, rows // tr),
        in_specs=[pl.BlockSpec((1, tr, d), lambda j, r, core_ref: (2 * j + core_ref[0], r, 0)),
                  pl.BlockSpec((1, tr, d), lambda j, r, core_ref: (j, r, 0))],
        out_specs=pl.BlockSpec((1, tr, d), lambda j, r, core_ref: (j, r, 0)))
    return pl.pallas_call(body, name=name, grid_spec=grid_spec,
                          out_shape=jax.ShapeDtypeStruct((4, rows, d), slab.dtype),
                          compiler_params=_params())(core, slab, got)


def _sum_blocks(parts, name):
    k, rows, d = parts.shape
    tr = _row_tile(rows, 512, BF16_ROWS if parts.dtype == BF16 else 8)

    def body(in_ref, out_ref):
        acc = in_ref[0].astype(F32)
        for j in range(1, k):
            acc = acc + in_ref[j].astype(F32)
        out_ref[...] = acc

    return pl.pallas_call(
        body, name=name, grid=(rows // tr,),
        in_specs=[pl.BlockSpec((k, tr, d), lambda r: (0, r, 0))],
        out_specs=pl.BlockSpec((tr, d), lambda r: (r, 0)),
        out_shape=jax.ShapeDtypeStruct((rows, d), F32), compiler_params=_params())(parts)


def _ln_stats(z):
    mu = jnp.mean(z, axis=-1, keepdims=True)
    zc = z - mu
    var = jnp.mean(zc * zc, axis=-1, keepdims=True)
    rstd = lax.rsqrt(var + LN_EPS)
    return zc * rstd, rstd


def _ln_bwd(dy, z, g):
    zhat, rstd = _ln_stats(z)
    dyg = dy * g
    m1 = jnp.mean(dyg, axis=-1, keepdims=True)
    m2 = jnp.mean(dyg * zhat, axis=-1, keepdims=True)
    dz = rstd * (dyg - m1 - zhat * m2)
    return dz, jnp.sum(dy * zhat, axis=0, keepdims=True), jnp.sum(dy, axis=0, keepdims=True)


def _accumulate(ref, value, first):
    @pl.when(first)
    def _():
        ref[...] = value

    @pl.when(jnp.logical_not(first))
    def _():
        ref[...] += value


def _add_matmul(acc_ref, first, matmul):
    @pl.when(first)
    def _():
        acc_ref[...] = jnp.zeros_like(acc_ref)

    acc_ref[...] += matmul()


def _ffn_weight_specs(l, which, fs, d, index_of):
    def spec(row):
        return pl.BlockSpec((FFN_CHUNK_DEVS, 1, 1, fs, d), lambda *g: (index_of(*g), l, row, 0, 0))
    return [spec(2 * which), spec(2 * which + 1), spec(4 + which)]


def _ffn_fwd(x, w_ffn, l, which, ln_g, ln_b, alpha, name):
    s, d = x.shape
    fs = w_ffn.shape[3]
    tf = FFN_CHUNK_DEVS * fs
    n_c = N_DEV // FFN_CHUNK_DEVS
    tm = min(s, 512)

    def body(x_ref, wg_ref, wu_ref, wd_ref, g_ref, b_ref, a_ref, u_ref, z_ref, y_ref, xb_s, acc_s):
        c = pl.program_id(1)

        @pl.when(c == 0)
        def _():
            xb_s[...] = x_ref[...].astype(BF16)
            acc_s[...] = jnp.zeros_like(acc_s)

        xb = xb_s[...]
        a = _dot_nt(xb, wg_ref[...].reshape(tf, d))
        u = _dot_nt(xb, wu_ref[...].reshape(tf, d))
        a_ref[...] = a.astype(BF16)
        u_ref[...] = u.astype(BF16)
        h = (a * jax.nn.sigmoid(a)) * u
        acc_s[...] += _dot(h.astype(BF16), wd_ref[...].reshape(tf, d))

        @pl.when(c == n_c - 1)
        def _():
            z = alpha * x_ref[...] + 0.5 * acc_s[...]
            zhat, _ = _ln_stats(z)
            z_ref[...] = z
            y_ref[...] = zhat * g_ref[...] + b_ref[...]

    row = pl.BlockSpec((tm, d), lambda i, c: (i, 0))
    vec = pl.BlockSpec((1, d), lambda i, c: (0, 0))
    hid = pl.BlockSpec((tm, tf), lambda i, c: (i, c))
    return pl.pallas_call(
        body, name=name, grid=(s // tm, n_c),
        in_specs=[row] + _ffn_weight_specs(l, which, fs, d, lambda i, c: c) + [vec, vec],
        out_specs=[hid, hid, row, row],
        out_shape=[jax.ShapeDtypeStruct((s, N_DEV * fs), BF16)] * 2 + [jax.ShapeDtypeStruct((s, d), F32)] * 2,
        scratch_shapes=[pltpu.VMEM((tm, d), BF16), pltpu.VMEM((tm, d), F32)],
        compiler_params=_params(),
    )(x, w_ffn, w_ffn, w_ffn, ln_g, ln_b)


def _ffn_bwd_dx(dy, z, a, u, w_ffn, l, which, ln_g, alpha, name):
    s, d = dy.shape
    fs = w_ffn.shape[3]
    tf = FFN_CHUNK_DEVS * fs
    n_c = N_DEV // FFN_CHUNK_DEVS
    tm = min(s, 512)

    def body(dy_ref, z_ref, a_ref, u_ref, wg_ref, wu_ref, wd_ref, g_ref,
             dx_ref, da_ref, du_ref, df_ref, dg_ref, db_ref, df_s, acc_s):
        i, c = pl.program_id(0), pl.program_id(1)

        @pl.when(c == 0)
        def _():
            dz, dg, db = _ln_bwd(dy_ref[...], z_ref[...], g_ref[...])
            _accumulate(dg_ref, dg, i == 0)
            _accumulate(db_ref, db, i == 0)
            df = (0.5 * dz).astype(BF16)
            df_s[...] = df
            df_ref[...] = df
            acc_s[...] = alpha * dz

        av = a_ref[...].astype(F32)
        uv = u_ref[...].astype(F32)
        sg = jax.nn.sigmoid(av)
        dh = _dot_nt(df_s[...], wd_ref[...].reshape(tf, d))
        du = (dh * (av * sg)).astype(BF16)
        da = (dh * uv * (sg * (1.0 + av * (1.0 - sg)))).astype(BF16)
        da_ref[...] = da
        du_ref[...] = du
        acc_s[...] += _dot(da, wg_ref[...].reshape(tf, d)) + _dot(du, wu_ref[...].reshape(tf, d))

        @pl.when(c == n_c - 1)
        def _():
            dx_ref[...] = acc_s[...]

    row = pl.BlockSpec((tm, d), lambda i, c: (i, 0))
    vec = pl.BlockSpec((1, d), lambda i, c: (0, 0))
    hid = pl.BlockSpec((tm, tf), lambda i, c: (i, c))
    return pl.pallas_call(
        body, name=name, grid=(s // tm, n_c),
        in_specs=[row, row, hid, hid] + _ffn_weight_specs(l, which, fs, d, lambda i, c: c) + [vec],
        out_specs=[row, hid, hid, row, vec, vec],
        out_shape=[jax.ShapeDtypeStruct((s, d), F32)] + [jax.ShapeDtypeStruct((s, N_DEV * fs), BF16)] * 2
                  + [jax.ShapeDtypeStruct((s, d), BF16)] + [jax.ShapeDtypeStruct((1, d), F32)] * 2,
        scratch_shapes=[pltpu.VMEM((tm, d), BF16), pltpu.VMEM((tm, d), F32)],
        compiler_params=_params(),
    )(dy, z, a, u, w_ffn, w_ffn, w_ffn, ln_g)


def _slab_call(body, slab, n_in, **kw):
    if slab is None:
        return pl.pallas_call(body, **kw)
    kw["in_specs"] = list(kw["in_specs"]) + [_any_spec()]
    kw["input_output_aliases"] = {n_in: len(kw["out_shape"]) - 1}

    def aliased(*refs):
        body(*refs[:n_in], *refs[n_in + 1:])

    return lambda *args: pl.pallas_call(aliased, **kw)(*args, slab)


def _ffn_bwd_dwgu(da, du, x, slab, slab_shape, l, which, name):
    s, d = x.shape
    _, n_l, _, fs, _ = slab_shape
    tf = FFN_CHUNK_DEVS * fs
    n_c = N_DEV // FFN_CHUNK_DEVS
    tk = min(s, 512)
    n_k = s // tk

    def body(da_ref, du_ref, x_ref, out_ref, accg_s, accu_s):
        k = pl.program_id(1)
        xb = x_ref[...].astype(BF16)
        _add_matmul(accg_s, k == 0, lambda: _dot_tn(da_ref[...], xb))
        _add_matmul(accu_s, k == 0, lambda: _dot_tn(du_ref[...], xb))

        @pl.when(k == n_k - 1)
        def _():
            out_ref[:, 0, 0] = accg_s[...].astype(BF16).reshape(FFN_CHUNK_DEVS, fs, d)
            out_ref[:, 0, 1] = accu_s[...].astype(BF16).reshape(FFN_CHUNK_DEVS, fs, d)

    hid = pl.BlockSpec((tk, tf), lambda c, k: (k, c))
    call = _slab_call(
        body, slab, 3, name=name, grid=(n_c, n_k),
        in_specs=[hid, hid, pl.BlockSpec((tk, d), lambda c, k: (k, 0))],
        out_specs=[pl.BlockSpec((FFN_CHUNK_DEVS, 1, 2, fs, d), lambda c, k: (c, l, which, 0, 0))],
        out_shape=[jax.ShapeDtypeStruct(slab_shape, BF16)],
        scratch_shapes=[pltpu.VMEM((tf, d), F32), pltpu.VMEM((tf, d), F32)],
        compiler_params=_params())
    return call(da, du, x)[0]


def _ffn_bwd_dwd(a, u, df, slab, slab_shape, l, which, name):
    s, d = df.shape
    fs = slab_shape[3]
    tf = FFN_CHUNK_DEVS * fs
    n_c = N_DEV // FFN_CHUNK_DEVS
    tk = min(s, 512)
    n_k = s // tk

    def body(a_ref, u_ref, df_ref, out_ref, acc_s):
        k = pl.program_id(1)
        av = a_ref[...].astype(F32)
        h = ((av * jax.nn.sigmoid(av)) * u_ref[...].astype(F32)).astype(BF16)
        _add_matmul(acc_s, k == 0, lambda: _dot_tn(h, df_ref[...]))

        @pl.when(k == n_k - 1)
        def _():
            out_ref[:, 0, 0] = acc_s[...].astype(BF16).reshape(FFN_CHUNK_DEVS, fs, d)

    hid = pl.BlockSpec((tk, tf), lambda c, k: (k, c))
    call = _slab_call(
        body, slab, 3, name=name, grid=(n_c, n_k),
        in_specs=[hid, hid, pl.BlockSpec((tk, d), lambda c, k: (k, 0))],
        out_specs=[pl.BlockSpec((FFN_CHUNK_DEVS, 1, 1, fs, d), lambda c, k: (c, l, 4 + which, 0, 0))],
        out_shape=[jax.ShapeDtypeStruct(slab_shape, BF16)],
        scratch_shapes=[pltpu.VMEM((tf, d), F32)],
        compiler_params=_params())
    return call(a, u, df)[0]


def _win_fwd(x, w_in, l, name):
    s, d = x.shape
    rs = w_in.shape[2]
    d_in = N_DEV * rs
    tm = min(s, 512)

    def body(x_ref, w_ref, out_ref):
        out_ref[...] = _dot_nt(x_ref[...].astype(BF16), w_ref[...].reshape(d_in, d))

    return pl.pallas_call(
        body, name=name, grid=(s // tm,),
        in_specs=[pl.BlockSpec((tm, d), lambda i: (i, 0)),
                  pl.BlockSpec((N_DEV, 1, rs, d), lambda i: (0, l, 0, 0))],
        out_specs=pl.BlockSpec((tm, d_in), lambda i: (i, 0)),
        out_shape=jax.ShapeDtypeStruct((s, d_in), F32), compiler_params=_params())(x, w_in)


def _wout_fwd(x, yab, yc, w_out, l, ln_g, ln_b, alpha, name):
    s, d = x.shape
    rs = w_out.shape[2]
    tm = min(s, 512)

    def body(x_ref, yab_ref, yc_ref, w_ref, g_ref, b_ref, z_ref, y_ref):
        mix = jnp.concatenate([yab_ref[...], yc_ref[...]], axis=1).astype(BF16)
        z = alpha * x_ref[...] + _dot(mix, w_ref[...].reshape(D_MIX, d))
        zhat, _ = _ln_stats(z)
        z_ref[...] = z
        y_ref[...] = zhat * g_ref[...] + b_ref[...]

    row = pl.BlockSpec((tm, d), lambda i: (i, 0))
    half = pl.BlockSpec((tm, D_MIX // 2), lambda i: (i, 0))
    vec = pl.BlockSpec((1, d), lambda i: (0, 0))
    return pl.pallas_call(
        body, name=name, grid=(s // tm,),
        in_specs=[row, half, half, pl.BlockSpec((N_DEV, 1, rs, d), lambda i: (0, l, 0, 0)), vec, vec],
        out_specs=[row, row], out_shape=[jax.ShapeDtypeStruct((s, d), F32)] * 2,
        compiler_params=_params())(x, yab, yc, w_out, ln_g, ln_b)


def _wout_bwd(dy, z, yab, yc, w_out, l, ln_g, alpha, slab, slab_shape, name):
    s, d = dy.shape
    rs = w_out.shape[2]
    tm = min(s, 512)
    n_i = s // tm

    def body(dy_ref, z_ref, yab_ref, yc_ref, w_ref, g_ref, dmix_ref, dxp_ref, dg_ref, db_ref, out_ref, acc_s):
        i = pl.program_id(0)
        dz, dg, db = _ln_bwd(dy_ref[...], z_ref[...], g_ref[...])
        _accumulate(dg_ref, dg, i == 0)
        _accumulate(db_ref, db, i == 0)
        dxp_ref[...] = alpha * dz
        dzb = dz.astype(BF16)
        dmix_ref[...] = _dot_nt(dzb, w_ref[...].reshape(D_MIX, d))
        mix = jnp.concatenate([yab_ref[...], yc_ref[...]], axis=1).astype(BF16)
        _add_matmul(acc_s, i == 0, lambda: _dot_tn(mix, dzb))

        @pl.when(i == n_i - 1)
        def _():
            out_ref[:, 0] = acc_s[...].astype(BF16).reshape(N_DEV, rs, d)

    row = pl.BlockSpec((tm, d), lambda i: (i, 0))
    half = pl.BlockSpec((tm, D_MIX // 2), lambda i: (i, 0))
    vec = pl.BlockSpec((1, d), lambda i: (0, 0))
    wblk = pl.BlockSpec((N_DEV, 1, rs, d), lambda i: (0, l, 0, 0))
    call = _slab_call(
        body, slab, 6, name=name, grid=(n_i,),
        in_specs=[row, row, half, half, wblk, vec],
        out_specs=[pl.BlockSpec((tm, D_MIX), lambda i: (i, 0)), row, vec, vec, wblk],
        out_shape=[jax.ShapeDtypeStruct((s, D_MIX), F32), jax.ShapeDtypeStruct((s, d), F32),
                   jax.ShapeDtypeStruct((1, d), F32), jax.ShapeDtypeStruct((1, d), F32),
                   jax.ShapeDtypeStruct(slab_shape, BF16)],
        scratch_shapes=[pltpu.VMEM((D_MIX, d), F32)],
        compiler_params=_params())
    return call(dy, z, yab, yc, w_out, ln_g)


def _win_bwd(dxp, dloc, dq, dk, dv, x, w_in, l, slab, slab_shape, name):
    s, d = x.shape
    rs = w_in.shape[2]
    d_in = N_DEV * rs
    tm = min(s, 256)
    n_i = s // tm

    def body(dxp_ref, dloc_ref, dq_ref, dk_ref, dv_ref, x_ref, w_ref, dx_ref, out_ref, acc_s):
        i = pl.program_id(0)
        dp = jnp.concatenate([dloc_ref[...], dq_ref[...], dk_ref[...].astype(BF16), dv_ref[...].astype(BF16)], axis=1)
        dx_ref[...] = dxp_ref[...] + _dot(dp, w_ref[...].reshape(d_in, d))
        _add_matmul(acc_s, i == 0, lambda: _dot_tn(dp, x_ref[...].astype(BF16)))

        @pl.when(i == n_i - 1)
        def _():
            out_ref[:, 0] = acc_s[...].astype(BF16).reshape(N_DEV, rs, d)

    row = pl.BlockSpec((tm, d), lambda i: (i, 0))
    na = pl.BlockSpec((tm, D_NA), lambda i: (i, 0))
    wblk = pl.BlockSpec((N_DEV, 1, rs, d), lambda i: (0, l, 0, 0))
    call = _slab_call(
        body, slab, 7, name=name, grid=(n_i,),
        in_specs=[row, pl.BlockSpec((tm, D_LOC), lambda i: (i, 0)), na, na, na, row, wblk],
        out_specs=[row, wblk],
        out_shape=[jax.ShapeDtypeStruct((s, d), F32), jax.ShapeDtypeStruct(slab_shape, BF16)],
        scratch_shapes=[pltpu.VMEM((d_in, d), F32)],
        compiler_params=_params())
    return call(dxp, dloc, dq, dk, dv, x, w_in)


def _shift_rows(v, k):
    n = v.shape[0]
    return pltpu.roll(v, k % n, 0)


def _halo_specs(tm, s, width, col):
    per = tm // HALO
    last = s // HALO - 1
    return [pl.BlockSpec((HALO, width), lambda i: (jnp.maximum(i * per - 1, 0), col)),
            pl.BlockSpec((tm, width), lambda i: (i, col)),
            pl.BlockSpec((HALO, width), lambda i: (jnp.minimum((i + 1) * per, last), col))]


def _token_index(i, tm):
    return i * tm - HALO + lax.broadcasted_iota(jnp.int32, (tm + 2 * HALO, 1), 0)


def _pool_lane_tables():
    lane = lax.broadcasted_iota(jnp.int32, (1, D_POOL), 1)
    group = sum((lane >= g * POOL_GROUP).astype(jnp.int32) for g in range(1, len(POOL_WINDOWS)))
    half = jnp.where(group == 0, 1, jnp.where(group == 1, 2, jnp.where(group == 2, 4, 8)))
    return group, half


def _window_sums(v, group, offsets):
    s2 = v + _shift_rows(v, 1)
    s4 = s2 + _shift_rows(s2, 2)
    s8 = s4 + _shift_rows(s4, 4)
    s16 = s8 + _shift_rows(s8, 8)
    parts = [_shift_rows(p, -o) if o else p for p, o in zip((s2, s4, s8, s16), offsets)]
    return jnp.where(group == 0, parts[0], jnp.where(group == 1, parts[1], jnp.where(group == 2, parts[2], parts[3])))


def _pool_counts(tok, half, s):
    return (jnp.minimum(tok + half, s) - jnp.maximum(tok - half, 0)).astype(F32)


def _pool_forward(u, tok, s):
    group, half = _pool_lane_tables()
    sums = _window_sums(u, group, [w // 2 - 1 for w in POOL_WINDOWS])
    return sums / _pool_counts(tok, half, s) - u


def _conv_forward(zc, cw_ref):
    return cw_ref[0:1, :] * _shift_rows(zc, 1) + cw_ref[1:2, :] * zc + cw_ref[2:3, :] * _shift_rows(zc, -1)


def _local_fwd(proj, pool_bd, pool_scale, conv_w, name):
    s = proj.shape[0]
    tm = min(s, 512)
    ctr = slice(HALO, HALO + tm)

    def body(prev_ref, cur_ref, next_ref, pw_ref, sc_ref, cw_ref, out_ref):
        i = pl.program_id(0)
        ext = jnp.concatenate([prev_ref[...], cur_ref[...], next_ref[...]], axis=0)
        tok = _token_index(i, tm)
        inside = (tok >= 0) & (tok < s)
        u = jnp.where(inside, ext[:, 0:D_POOL], 0.0)
        p = _pool_forward(u, tok, s)[ctr]
        ya = _dot(p.astype(BF16), pw_ref[...]) * sc_ref[...]
        gb = ext[:, D_POOL:D_POOL + D_CONV]
        zc = jnp.where(inside, ext[:, D_POOL + D_CONV:D_POOL + 2 * D_CONV] * ext[:, D_POOL + 2 * D_CONV:D_LOC], 0.0)
        yb = (gb * _conv_forward(zc, cw_ref))[ctr]
        out_ref[...] = jnp.concatenate([ya, yb], axis=1)

    return pl.pallas_call(
        body, name=name, grid=(s // tm,),
        in_specs=_halo_specs(tm, s, D_LOC, 0) + [
            pl.BlockSpec((D_POOL, D_POOL), lambda i: (0, 0)), pl.BlockSpec((1, D_POOL), lambda i: (0, 0)),
            pl.BlockSpec((3, D_CONV), lambda i: (0, 0))],
        out_specs=pl.BlockSpec((tm, D_POOL + D_CONV), lambda i: (i, 0)),
        out_shape=jax.ShapeDtypeStruct((s, D_POOL + D_CONV), F32),
        compiler_params=_params())(proj, proj, proj, pool_bd, pool_scale, conv_w)


def _local_bwd(proj, dmix, pool_bd, pool_scale, conv_w, name):
    s = proj.shape[0]
    tm = min(s, 512)
    ctr = slice(HALO, HALO + tm)

    def body(prev_ref, cur_ref, next_ref, dprev_ref, dcur_ref, dnext_ref, pw_ref, sc_ref, cw_ref,
             dloc_ref, dpw_ref, dsc_ref, dcw_ref):
        i = pl.program_id(0)
        first = i == 0
        ext = jnp.concatenate([prev_ref[...], cur_ref[...], next_ref[...]], axis=0)
        dext = jnp.concatenate([dprev_ref[...], dcur_ref[...], dnext_ref[...]], axis=0)
        tok = _token_index(i, tm)
        inside = (tok >= 0) & (tok < s)
        group, half = _pool_lane_tables()
        cnt = _pool_counts(tok, half, s)
        u = jnp.where(inside, ext[:, 0:D_POOL], 0.0)
        dya = jnp.where(inside, dext[:, 0:D_POOL], 0.0)
        p_c = _pool_forward(u, tok, s)[ctr].astype(BF16)
        lin = _dot(p_c, pw_ref[...])
        _accumulate(dsc_ref, jnp.sum(dya[ctr] * lin, axis=0, keepdims=True), first)
        e1 = (dya * sc_ref[...]).astype(BF16)
        _accumulate(dpw_ref, _dot_tn(p_c, e1[ctr]), first)
        dp = _dot_nt(e1, pw_ref[...])
        du = _window_sums(dp / cnt, group, [w // 2 for w in POOL_WINDOWS]) - dp
        gb = ext[:, D_POOL:D_POOL + D_CONV]
        gc = ext[:, D_POOL + D_CONV:D_POOL + 2 * D_CONV]
        hv = ext[:, D_POOL + 2 * D_CONV:D_LOC]
        zc = jnp.where(inside, gc * hv, 0.0)
        dyb = jnp.where(inside, dext[:, D_POOL:D_POOL + D_CONV], 0.0)
        dgb = dyb * _conv_forward(zc, cw_ref)
        dyc = dyb * gb
        for k in range(3):
            part = jnp.sum(dyc[ctr] * _shift_rows(zc, 1 - k)[ctr], axis=0, keepdims=True)
            _accumulate(dcw_ref.at[k:k + 1, :], part, first)
        dzc = cw_ref[0:1, :] * _shift_rows(dyc, -1) + cw_ref[1:2, :] * dyc + cw_ref[2:3, :] * _shift_rows(dyc, 1)
        dloc = jnp.concatenate([du, dgb, dzc * hv, dzc * gc], axis=1)
        dloc_ref[...] = dloc[ctr].astype(BF16)

    return pl.pallas_call(
        body, name=name, grid=(s // tm,),
        in_specs=_halo_specs(tm, s, D_LOC, 0) + _halo_specs(tm, s, D_POOL + D_CONV, 0) + [
            pl.BlockSpec((D_POOL, D_POOL), lambda i: (0, 0)), pl.BlockSpec((1, D_POOL), lambda i: (0, 0)),
            pl.BlockSpec((3, D_CONV), lambda i: (0, 0))],
        out_specs=[pl.BlockSpec((tm, D_LOC), lambda i: (i, 0)), pl.BlockSpec((D_POOL, D_POOL), lambda i: (0, 0)),
                   pl.BlockSpec((1, D_POOL), lambda i: (0, 0)), pl.BlockSpec((8, D_CONV), lambda i: (0, 0))],
        out_shape=[jax.ShapeDtypeStruct((s, D_LOC), BF16), jax.ShapeDtypeStruct((D_POOL, D_POOL), F32),
                   jax.ShapeDtypeStruct((1, D_POOL), F32), jax.ShapeDtypeStruct((8, D_CONV), F32)],
        compiler_params=_params())(proj, proj, proj, dmix, dmix, dmix, pool_bd, pool_scale, conv_w)


def _na_geometry(rows):
    n_j = rows // Q_ROWS
    dr = np.full((3, Q_ROWS, K_ROWS), 2 * NA_ROWS - 1, np.int64)
    for t, j in enumerate((0, min(1, n_j - 1), n_j - 1)):
        base = int(np.clip(Q_ROWS * j - NA_ROWS // 2, 0, rows - K_ROWS))
        for qr in range(Q_ROWS):
            r = Q_ROWS * j + qr
            start = int(np.clip(r - NA_ROWS // 2, 0, rows - NA_ROWS))
            for kr in range(K_ROWS):
                if start <= base + kr < start + NA_ROWS:
                    dr[t, qr, kr] = base + kr - r + NA_ROWS - 1
    return dr


def _na_col_tables():
    c = np.arange(GRID_W)
    start = np.clip(c - NA_COLS // 2, 0, GRID_W - NA_COLS)
    valid = (c[None, :] >= start[:, None]) & (c[None, :] < start[:, None] + NA_COLS)
    dc = np.clip(c[None, :] - c[:, None], -(NA_COLS - 1), NA_COLS - 1) + (NA_COLS - 1)
    return valid, dc


NO_ROW = 2 * NA_ROWS - 1
N_SLOT = 2 * NA_ROWS


def _na_tiles(rpb):
    valid, dc = _na_col_tables()
    onehot = jnp.asarray((dc[None] == np.arange(2 * NA_COLS - 1)[:, None, None]).astype(np.float32))
    table = jnp.einsum("hrd,dqk->hrqk", rpb, onehot, precision=lax.Precision.HIGHEST)
    table = jnp.where(jnp.asarray(valid)[None, None], table, NEG_INF)
    outside = jnp.full((NA_HEADS, 1, GRID_W, GRID_W), NEG_INF, F32)
    padded = jnp.concatenate([outside, table, outside], axis=1)
    pairs = jnp.concatenate([padded[:, :N_SLOT], padded[:, 1:]], axis=-1)
    return jnp.concatenate([pairs, jnp.full((NA_HEADS, 1, GRID_W, 2 * GRID_W), NEG_INF, F32)], axis=1)


def _tile_slots(dr_t):
    out = []
    for qr in range(Q_ROWS):
        for kp in range(K_ROWS // 2):
            even, odd = int(dr_t[qr, 2 * kp]), int(dr_t[qr, 2 * kp + 1])
            if even == NO_ROW and odd == NO_ROW:
                continue
            out.append((qr, kp, (even if even != NO_ROW else odd - 1) + 1, even != NO_ROW, odd != NO_ROW))
    return out


def _tile_at(qr, kp):
    return slice(qr * GRID_W, (qr + 1) * GRID_W), slice(kp * 2 * GRID_W, (kp + 1) * 2 * GRID_W)


def _fill_bias(bias_s, tiles_ref, dr_t):
    left = lax.broadcasted_iota(jnp.int32, (1, 2 * GRID_W), 1) < GRID_W
    inside = {(qr, kp): (e, li, ri) for qr, kp, e, li, ri in _tile_slots(dr_t)}
    for hh in range(2):
        for qr in range(Q_ROWS):
            for kp in range(K_ROWS // 2):
                e, li, ri = inside.get((qr, kp), (N_SLOT, True, True))
                tile = tiles_ref[hh, e]
                if not li:
                    tile = jnp.where(left, NEG_INF, tile)
                if not ri:
                    tile = jnp.where(left, tile, NEG_INF)
                rs, cs = _tile_at(qr, kp)
                bias_s[hh, rs, cs] = tile


def _when_type_starts(j, n_j, dr, fn):
    starts = {0: 0, n_j - 1: 2}
    if n_j > 2:
        starts[1] = 1
    for j0, t in starts.items():
        pl.when(j == j0)(functools.partial(fn, dr[t]))


def _na_specs(s, proj_cols):
    n_blk = s // K_BLK
    per = Q_TOK // K_BLK

    def kv_spec(col0, m):
        return pl.BlockSpec((K_BLK, HEAD_PAIR), lambda hp, j: (jnp.clip(per * j - 1, 0, n_blk - 4) + m, col0 + hp))

    q_col, k_col, v_col = (c // HEAD_PAIR for c in proj_cols)
    return ([pl.BlockSpec((Q_TOK, HEAD_PAIR), lambda hp, j: (j, q_col + hp))]
            + [kv_spec(k_col, m) for m in range(4)] + [kv_spec(v_col, m) for m in range(4)])


def _na_block_type(j, n_j):
    return jnp.where(j == 0, 0, jnp.where(j == n_j - 1, 2, 1))


def _head_masks():
    lane = lax.broadcasted_iota(jnp.int32, (1, HEAD_PAIR), 1)
    return [lane < NA_HEAD_DIM, lane >= NA_HEAD_DIM]


def _attn_fwd(proj, tiles, name):
    s = proj.shape[0]
    n_j = s // Q_TOK
    scale = NA_HEAD_DIM ** -0.5
    dr = _na_geometry(s // GRID_W)

    def body(q_ref, k0, k1, k2, k3, v0, v1, v2, v3, tiles_ref, o_ref, lse_ref, bias_s):
        _when_type_starts(pl.program_id(1), n_j, dr, functools.partial(_fill_bias, bias_s, tiles_ref))
        q = q_ref[...]
        kb = jnp.concatenate([r[...] for r in (k0, k1, k2, k3)], axis=0).astype(BF16)
        v = jnp.concatenate([r[...] for r in (v0, v1, v2, v3)], axis=0)
        out = jnp.zeros((Q_TOK, HEAD_PAIR), F32)
        lse = []
        for hh, mask in enumerate(_head_masks()):
            sc = _dot_nt(jnp.where(mask, q, 0.0).astype(BF16), kb) * scale + bias_s[hh]
            mx = jnp.max(sc, axis=-1, keepdims=True)
            p = jnp.exp(sc - mx)
            den = jnp.sum(p, axis=-1, keepdims=True)
            out = out + _dot(p.astype(BF16), jnp.where(mask, v, 0.0).astype(BF16)) * (1.0 / den)
            lse.append(mx + jnp.log(den))
        o_ref[...] = out
        lse_ref[0] = jnp.where(_head_masks()[0], lse[0], lse[1])

    return pl.pallas_call(
        body, name=name, grid=(NA_HEADS // 2, n_j),
        in_specs=_na_specs(s, (D_LOC, D_LOC + D_NA, D_LOC + 2 * D_NA)) + [
            pl.BlockSpec((2, N_SLOT + 1, GRID_W, 2 * GRID_W), lambda hp, j: (hp, 0, 0, 0))],
        out_specs=[pl.BlockSpec((Q_TOK, HEAD_PAIR), lambda hp, j: (j, hp)),
                   pl.BlockSpec((1, Q_TOK, HEAD_PAIR), lambda hp, j: (hp, j, 0))],
        out_shape=[jax.ShapeDtypeStruct((s, D_NA), F32), jax.ShapeDtypeStruct((NA_HEADS // 2, s, HEAD_PAIR), F32)],
        scratch_shapes=[pltpu.VMEM((2, Q_TOK, K_TOK), F32)],
        compiler_params=_params())(*([proj] * 9), tiles)


def _add_tiles(dtile_ref, hh, ds, dr_t):
    sums = {}
    for qr, kp, e, _, _ in _tile_slots(dr_t):
        rs, cs = _tile_at(qr, kp)
        sums[e] = ds[rs, cs] if e not in sums else sums[e] + ds[rs, cs]
    for e, val in sums.items():
        dtile_ref[hh, e] += val


def _attn_bwd(proj, tiles, o, dmix, lse, name):
    s = proj.shape[0]
    n_j = s // Q_TOK
    n_blk = s // K_BLK
    per = Q_TOK // K_BLK
    scale = NA_HEAD_DIM ** -0.5
    do_col = (D_POOL + D_CONV) // HEAD_PAIR
    dr = _na_geometry(s // GRID_W)
    used_types = [0] + ([1] if n_j > 2 else []) + [2]

    def body(q_ref, k0, k1, k2, k3, v0, v1, v2, v3, tiles_ref, o_ref, do_ref, lse_ref,
             dq_ref, dk_ref, dv_ref, dtile_ref, bias_s):
        j = pl.program_id(1)

        @pl.when(j == 0)
        def _():
            dk_ref[...] = jnp.zeros_like(dk_ref)
            dv_ref[...] = jnp.zeros_like(dv_ref)
            dtile_ref[...] = jnp.zeros_like(dtile_ref)

        _when_type_starts(j, n_j, dr, functools.partial(_fill_bias, bias_s, tiles_ref))
        block_type = _na_block_type(j, n_j)
        base = pl.multiple_of(jnp.clip(per * j - 1, 0, n_blk - 4) * K_BLK, K_BLK)
        q = q_ref[...]
        k = jnp.concatenate([r[...] for r in (k0, k1, k2, k3)], axis=0)
        vb = jnp.concatenate([r[...] for r in (v0, v1, v2, v3)], axis=0).astype(BF16)
        kb = k.astype(BF16)
        do = do_ref[...]
        ov = o_ref[...]
        lse = lse_ref[0]
        dq = jnp.zeros((Q_TOK, HEAD_PAIR), F32)
        dk = jnp.zeros((K_TOK, HEAD_PAIR), F32)
        dv = jnp.zeros((K_TOK, HEAD_PAIR), F32)
        lane = lax.broadcasted_iota(jnp.int32, (1, HEAD_PAIR), 1)
        for hh, mask in enumerate(_head_masks()):
            qh = jnp.where(mask, q, 0.0).astype(BF16)
            doh = jnp.where(mask, do, 0.0)
            dob = doh.astype(BF16)
            lse_h = jnp.sum(jnp.where(lane == hh * NA_HEAD_DIM, lse, 0.0), axis=-1, keepdims=True)
            p = jnp.exp(_dot_nt(qh, kb) * scale + bias_s[hh] - lse_h)
            delta = jnp.sum(doh * ov, axis=-1, keepdims=True)
            ds = p * (_dot_nt(dob, vb) - delta)
            for t in used_types:
                pl.when(block_type == t)(functools.partial(_add_tiles, dtile_ref, hh, ds, dr[t]))
            dsb = ds.astype(BF16)
            dq = dq + _dot(dsb, jnp.where(mask, k, 0.0).astype(BF16))
            dk = dk + _dot_tn(dsb, qh)
            dv = dv + _dot_tn(p.astype(BF16), dob)
        dq_ref[...] = (dq * scale).astype(BF16)
        dk_ref[pl.ds(base, K_TOK), :] += dk * scale
        dv_ref[pl.ds(base, K_TOK), :] += dv

    pair = pl.BlockSpec((Q_TOK, HEAD_PAIR), lambda hp, j: (j, hp))
    whole = pl.BlockSpec((s, HEAD_PAIR), lambda hp, j: (0, hp))
    return pl.pallas_call(
        body, name=name, grid=(NA_HEADS // 2, n_j),
        in_specs=_na_specs(s, (D_LOC, D_LOC + D_NA, D_LOC + 2 * D_NA)) + [
            pl.BlockSpec((2, N_SLOT + 1, GRID_W, 2 * GRID_W), lambda hp, j: (hp, 0, 0, 0)),
            pair, pl.BlockSpec((Q_TOK, HEAD_PAIR), lambda hp, j: (j, do_col + hp)),
            pl.BlockSpec((1, Q_TOK, HEAD_PAIR), lambda hp, j: (hp, j, 0))],
        out_specs=[pair, whole, whole, pl.BlockSpec((2, N_SLOT, GRID_W, 2 * GRID_W), lambda hp, j: (hp, 0, 0, 0))],
        out_shape=[jax.ShapeDtypeStruct((s, D_NA), BF16), jax.ShapeDtypeStruct((s, D_NA), F32),
                   jax.ShapeDtypeStruct((s, D_NA), F32),
                   jax.ShapeDtypeStruct((NA_HEADS, N_SLOT, GRID_W, 2 * GRID_W), F32)],
        scratch_shapes=[pltpu.VMEM((2, Q_TOK, K_TOK), F32)],
        compiler_params=_params())(*([proj] * 9), tiles, o, dmix, lse)


def _rpb_finish(tiles, name):
    valid, dc = _na_col_tables()
    n_dc = 2 * NA_COLS - 1
    sel = np.zeros((GRID_W, 2 * GRID_W, LANES), np.float32)
    for qc in range(GRID_W):
        for kc in range(GRID_W):
            if valid[qc, kc]:
                sel[qc, kc, dc[qc, kc]] = 1.0
                sel[qc, GRID_W + kc, LANES // 2 + dc[qc, kc]] = 1.0
    sel = jnp.asarray(sel.reshape(GRID_W * 2 * GRID_W, LANES))
    flat = tiles.reshape(NA_HEADS * 2 * NA_ROWS, GRID_W * 2 * GRID_W)

    def body(a_ref, b_ref, out_ref):
        out_ref[...] = jnp.dot(a_ref[...], b_ref[...], preferred_element_type=F32, precision=lax.Precision.HIGHEST)

    sums = pl.pallas_call(
        body, name=name, out_shape=jax.ShapeDtypeStruct((flat.shape[0], LANES), F32),
        compiler_params=_params())(flat, sel).reshape(NA_HEADS, 2 * NA_ROWS, LANES)
    return sums[:, 1:, :n_dc] + sums[:, :2 * NA_ROWS - 1, LANES // 2:LANES // 2 + n_dc]


def _loss_grad(y, target, name):
    s, d = y.shape
    tm = min(s, 1024)

    def body(y_ref, t_ref, sum_ref, dy_ref):
        diff = y_ref[...] - t_ref[...]
        dy_ref[...] = diff * (1.0 / d)
        part = jnp.zeros((8, LANES), F32) + jnp.sum(diff * diff)
        _accumulate(sum_ref, part, pl.program_id(0) == 0)

    row = pl.BlockSpec((tm, d), lambda i: (i, 0))
    return pl.pallas_call(
        body, name=name, grid=(s // tm,), in_specs=[row, row],
        out_specs=[pl.BlockSpec((8, LANES), lambda i: (0, 0)), row],
        out_shape=[jax.ShapeDtypeStruct((8, LANES), F32), jax.ShapeDtypeStruct((s, d), F32)],
        compiler_params=_params())(y, target)


def _adamw(w, g, m, v, name):
    rows, cols = w.shape
    tr = _row_tile(rows, 512, 8)

    def body(w_ref, g_ref, m_ref, v_ref, d_ref, nm_ref, nv_ref):
        gv = g_ref[...]
        nm = ADAM_B1 * m_ref[...] + (1.0 - ADAM_B1) * gv
        nv = ADAM_B2 * v_ref[...] + (1.0 - ADAM_B2) * (gv * gv)
        m_hat = nm / (1.0 - ADAM_B1 ** ADAM_STEP)
        v_hat = nv / (1.0 - ADAM_B2 ** ADAM_STEP)
        d_ref[...] = -ADAM_LR * (m_hat / (jnp.sqrt(v_hat) + ADAM_EPS) + ADAM_WD * w_ref[...])
        nm_ref[...] = nm
        nv_ref[...] = nv

    blk = pl.BlockSpec((tr, cols), lambda r: (r, 0))
    return pl.pallas_call(
        body, name=name, grid=(rows // tr,), in_specs=[blk] * 4, out_specs=[blk] * 3,
        out_shape=[jax.ShapeDtypeStruct((rows, cols), F32)] * 3, compiler_params=_params())(w, g, m, v)


def _adamw_nd(w, g, m, v, name):
    shape = w.shape
    flat = lambda t: t.reshape(-1, shape[-1])
    return tuple(t.reshape(shape) for t in _adamw(flat(w), flat(g), flat(m), flat(v), name))


def _pack(parts, rows_mult=64):
    flat = jnp.concatenate([p.reshape(-1).astype(F32) for p in parts])
    per = LANES * rows_mult
    total = -(-flat.shape[0] // per) * per
    return jnp.pad(flat, (0, total - flat.shape[0])).reshape(-1, LANES)


def _unpack(packed, shapes):
    flat = packed.reshape(-1)
    out, pos = [], 0
    for shp in shapes:
        n = int(np.prod(shp))
        out.append(flat[pos:pos + n].reshape(shp))
        pos += n
    return out


def kernel(x, ffn1_w_gate, ffn1_w_up, ffn1_w_down, ffn2_w_gate, ffn2_w_up, ffn2_w_down, w_in, pool_w, pool_scale, conv_w, rpb, w_out, ln_g, ln_b, loss_target, m_ffn1_w_gate, m_ffn1_w_up, m_ffn1_w_down, m_ffn2_w_gate, m_ffn2_w_up, m_ffn2_w_down, m_w_in, m_pool_w, m_pool_scale, m_conv_w, m_rpb, m_w_out, m_ln_g, m_ln_b, v_ffn1_w_gate, v_ffn1_w_up, v_ffn1_w_down, v_ffn2_w_gate, v_ffn2_w_up, v_ffn2_w_down, v_w_in, v_pool_w, v_pool_scale, v_conv_w, v_rpb, v_w_out, v_ln_g, v_ln_b):
    n_l, d, fs = ffn1_w_gate.shape
    s = x.shape[1]
    rows = s // GRID_W
    assert x.shape[0] == 1 and s % Q_TOK == 0 and rows >= K_ROWS and fs % BF16_ROWS == 0
    alpha = (2.0 * n_l) ** 0.25
    xi, yi, ci = _mesh_pos()
    me = 4 * xi + 2 * yi + ci
    core = jnp.reshape(ci, (1,)).astype(jnp.int32)
    ln_w, cw_w = ln_g.shape[2], conv_w.shape[2]

    tr = lambda w: jnp.swapaxes(w, 1, 2)
    ffn_shard = jnp.stack([tr(ffn1_w_gate), tr(ffn1_w_up), tr(ffn2_w_gate), tr(ffn2_w_up), ffn1_w_down, ffn2_w_down],
                          axis=1).astype(BF16)
    small_shard = _pack([ln_g, ln_b, conv_w])
    w_ffn, w_in_t, w_out_r, small = _all_gather(
        [ffn_shard, tr(w_in).astype(BF16), w_out.astype(BF16), small_shard], "gather_weights")
    n_ln = n_l * 3 * ln_w
    small = small.reshape(N_DEV, -1)
    unshard = lambda t, width: jnp.moveaxis(t.reshape(N_DEV, n_l, 3, width), 0, 2).reshape(n_l, 3, N_DEV * width)
    ln_g_all = unshard(small[:, :n_ln], ln_w)
    ln_b_all = unshard(small[:, n_ln:2 * n_ln], ln_w)
    conv_all = unshard(small[:, 2 * n_ln:2 * n_ln + n_l * 3 * cw_w], cw_w)
    pool_bd = jnp.zeros((n_l, D_POOL, D_POOL), F32)
    for g in range(len(POOL_WINDOWS)):
        sl = slice(g * POOL_GROUP, (g + 1) * POOL_GROUP)
        pool_bd = pool_bd.at[:, sl, sl].set(pool_w[:, g])
    pool_bd = pool_bd.astype(BF16)
    lnp = lambda arr, l, j: arr[l, j].reshape(1, d)

    saved = []
    h = x.reshape(s, d)
    for l in range(n_l):
        a1, u1, z1, x1 = _ffn_fwd(h, w_ffn, l, 0, lnp(ln_g_all, l, 0), lnp(ln_b_all, l, 0), alpha, f"ffn1_fwd_{l}")
        proj = _win_fwd(x1, w_in_t, l, f"win_fwd_{l}")
        bias = _na_tiles(rpb[l])
        yab = _local_fwd(proj, pool_bd[l], pool_scale[l].reshape(1, D_POOL), conv_all[l], f"local_fwd_{l}")
        yc, lse = _attn_fwd(proj, bias, f"attn_fwd_{l}")
        z2, x2 = _wout_fwd(x1, yab, yc, w_out_r, l, lnp(ln_g_all, l, 1), lnp(ln_b_all, l, 1), alpha, f"wout_fwd_{l}")
        a2, u2, z3, x3 = _ffn_fwd(x2, w_ffn, l, 1, lnp(ln_g_all, l, 2), lnp(ln_b_all, l, 2), alpha, f"ffn2_fwd_{l}")
        saved.append((h, a1, u1, z1, x1, proj, bias, yab, yc, lse, z2, x2, a2, u2, z3))
        h = x3

    sq, dh = _loss_grad(h, loss_target.reshape(s, d), "loss_head")
    loss = lax.psum(sq[0, 0] * (0.5 / d), MESH_AXES)

    g_ffn, g_in, g_out = None, None, None
    small_grads = [None] * n_l
    for l in reversed(range(n_l)):
        x0, a1, u1, z1, x1, proj, bias, yab, yc, lse, z2, x2, a2, u2, z3 = saved[l]
        dx2, da, du, df, dg3, db3 = _ffn_bwd_dx(dh, z3, a2, u2, w_ffn, l, 1, lnp(ln_g_all, l, 2), alpha, f"ffn2_bwd_dx_{l}")
        g_ffn = _ffn_bwd_dwgu(da, du, x2, g_ffn, w_ffn.shape, l, 1, f"ffn2_bwd_dwgu_{l}")
        g_ffn = _ffn_bwd_dwd(a2, u2, df, g_ffn, w_ffn.shape, l, 1, f"ffn2_bwd_dwd_{l}")
        dmix, dxp, dg2, db2, g_out = _wout_bwd(dx2, z2, yab, yc, w_out_r, l, lnp(ln_g_all, l, 1), alpha,
                                               g_out, w_out_r.shape, f"wout_bwd_{l}")
        dq, dk, dv, dtiles = _attn_bwd(proj, bias, yc, dmix, lse, f"attn_bwd_{l}")
        dloc, dpw, dsc, dcw = _local_bwd(proj, dmix, pool_bd[l], pool_scale[l].reshape(1, D_POOL), conv_all[l],
                                         f"local_bwd_{l}")
        dx1, g_in = _win_bwd(dxp, dloc, dq, dk, dv, x1, w_in_t, l, g_in, w_in_t.shape, f"win_bwd_{l}")
        dx0, da, du, df, dg1, db1 = _ffn_bwd_dx(dx1, z1, a1, u1, w_ffn, l, 0, lnp(ln_g_all, l, 0), alpha, f"ffn1_bwd_dx_{l}")
        g_ffn = _ffn_bwd_dwgu(da, du, x0, g_ffn, w_ffn.shape, l, 0, f"ffn1_bwd_dwgu_{l}")
        g_ffn = _ffn_bwd_dwd(a1, u1, df, g_ffn, w_ffn.shape, l, 0, f"ffn1_bwd_dwd_{l}")
        drpb = _rpb_finish(dtiles, f"rpb_finish_{l}")
        dpool = jnp.stack([dpw[g * POOL_GROUP:(g + 1) * POOL_GROUP, g * POOL_GROUP:(g + 1) * POOL_GROUP]
                           for g in range(len(POOL_WINDOWS))])
        small_grads[l] = (jnp.concatenate([dg1, dg2, dg3]), jnp.concatenate([db1, db2, db3]), dcw[0:3], dpool, dsc[0], drpb)
        dh = dx0
    grad_x = dh.reshape(x.shape)

    slabs = [g_ffn.reshape(N_DEV, -1, d), g_in.reshape(N_DEV, -1, d), g_out.reshape(N_DEV, -1, d)]
    got = _pair_exchange(slabs, "grads_pair_exchange")
    pair = [_pair_add(sl, gt, core, f"grads_pair_add_{i}") for i, (sl, gt) in enumerate(zip(slabs, got))]
    chip = _chip_exchange(pair, "grads_chip_exchange")
    r_ffn, r_in, r_out = [_sum_blocks(c, f"grads_chip_sum_{i}") for i, c in enumerate(chip)]
    r_ffn = r_ffn.reshape(n_l, 6, fs, d)
    grads = {
        "ffn1_w_gate": tr(r_ffn[:, 0]), "ffn1_w_up": tr(r_ffn[:, 1]), "ffn2_w_gate": tr(r_ffn[:, 2]),
        "ffn2_w_up": tr(r_ffn[:, 3]), "ffn1_w_down": r_ffn[:, 4], "ffn2_w_down": r_ffn[:, 5],
        "w_in": tr(r_in.reshape(n_l, -1, d)), "w_out": r_out.reshape(n_l, -1, d)}

    stack = lambda k: jnp.stack([small_grads[l][k] for l in range(n_l)])
    small_shapes = [(n_l, 3, d), (n_l, 3, d), (n_l, 3, D_CONV), pool_w.shape, pool_scale.shape, rpb.shape]
    (small_all,) = _all_gather([_pack([stack(k) for k in range(6)])], "gather_small_grads")
    small_sum = _sum_blocks(small_all, "small_grads_sum")
    g_ln_g, g_ln_b, g_conv, g_pool_w, g_pool_scale, g_rpb = _unpack(small_sum, small_shapes)
    own = lambda t, width: lax.dynamic_slice_in_dim(t, me * width, width, axis=2)
    grads.update({"ln_g": own(g_ln_g, ln_w), "ln_b": own(g_ln_b, ln_w), "conv_w": own(g_conv, cw_w),
                  "pool_w": g_pool_w, "pool_scale": g_pool_scale, "rpb": g_rpb})

    weights = dict(ffn1_w_gate=ffn1_w_gate, ffn1_w_up=ffn1_w_up, ffn1_w_down=ffn1_w_down, ffn2_w_gate=ffn2_w_gate,
                   ffn2_w_up=ffn2_w_up, ffn2_w_down=ffn2_w_down, w_in=w_in, pool_w=pool_w, pool_scale=pool_scale,
                   conv_w=conv_w, rpb=rpb, w_out=w_out, ln_g=ln_g, ln_b=ln_b)
    m_in = dict(ffn1_w_gate=m_ffn1_w_gate, ffn1_w_up=m_ffn1_w_up, ffn1_w_down=m_ffn1_w_down, ffn2_w_gate=m_ffn2_w_gate,
                ffn2_w_up=m_ffn2_w_up, ffn2_w_down=m_ffn2_w_down, w_in=m_w_in, pool_w=m_pool_w, pool_scale=m_pool_scale,
                conv_w=m_conv_w, rpb=m_rpb, w_out=m_w_out, ln_g=m_ln_g, ln_b=m_ln_b)
    v_in = dict(ffn1_w_gate=v_ffn1_w_gate, ffn1_w_up=v_ffn1_w_up, ffn1_w_down=v_ffn1_w_down, ffn2_w_gate=v_ffn2_w_gate,
                ffn2_w_up=v_ffn2_w_up, ffn2_w_down=v_ffn2_w_down, w_in=v_w_in, pool_w=v_pool_w, pool_scale=v_pool_scale,
                conv_w=v_conv_w, rpb=v_rpb, w_out=v_w_out, ln_g=v_ln_g, ln_b=v_ln_b)
    names = list(weights)
    large = ["ffn1_w_gate", "ffn1_w_up", "ffn1_w_down", "ffn2_w_gate", "ffn2_w_up", "ffn2_w_down", "w_in", "w_out"]
    tiny = [n for n in names if n not in large]
    delta, new_m, new_v = {}, {}, {}
    for n in large:
        delta[n], new_m[n], new_v[n] = _adamw_nd(weights[n], grads[n], m_in[n], v_in[n], f"adamw_{n}")
    packed = [_pack([t[n] for n in tiny]) for t in (weights, grads, m_in, v_in)]
    tiny_out = _adamw(*packed, "adamw_small")
    tiny_shapes = [weights[n].shape for n in tiny]
    for res, t in zip((delta, new_m, new_v), tiny_out):
        res.update(dict(zip(tiny, _unpack(t, tiny_shapes))))

    return (loss, grad_x, *[grads[n] for n in names], *[delta[n] for n in names],
            *[new_m[n] for n in names], *[new_v[n] for n in names])
```

```python
import functools

import numpy as np
import jax
import jax.numpy as jnp
from jax import lax
from jax.experimental import pallas as pl
from jax.experimental.pallas import tpu as pltpu

F32, BF16 = jnp.float32, jnp.bfloat16
MESH = pl.DeviceIdType.MESH
N_DEV = 8
MESH_AXES = ("x", "y", "c")

LN_EPS = 1e-5
NEG_INF = -1e30
D_POOL = 256
POOL_WINDOWS = (2, 4, 8, 16)
POOL_GROUP = 64
D_CONV = 256
NA_HEADS = 8
NA_HEAD_DIM = 64
D_NA = NA_HEADS * NA_HEAD_DIM
GRID_W = 64
NA_ROWS = 8
NA_COLS = 16
D_LOC = D_POOL + 3 * D_CONV
D_MIX = D_POOL + D_CONV + D_NA
ADAM_LR, ADAM_B1, ADAM_B2, ADAM_EPS, ADAM_WD, ADAM_STEP = 0.001, 0.9, 0.999, 1e-08, 0.01, 10

VMEM_LIMIT_BYTES = 56 * 1024 * 1024
LANES = 128
BF16_ROWS = 16
HALO = 16
Q_ROWS = 8
K_ROWS = 16
Q_TOK = Q_ROWS * GRID_W
K_TOK = K_ROWS * GRID_W
K_BLK = 4 * GRID_W
HEAD_PAIR = 2 * NA_HEAD_DIM
FFN_CHUNK_DEVS = 4

NT = (((1,), (1,)), ((), ()))
TN = (((0,), (0,)), ((), ()))


def _dot(a, b):
    return jnp.dot(a, b, preferred_element_type=F32)


def _dot_nt(a, b):
    return lax.dot_general(a, b, NT, preferred_element_type=F32)


def _dot_tn(a, b):
    return lax.dot_general(a, b, TN, preferred_element_type=F32)


def _params():
    return pltpu.CompilerParams(vmem_limit_bytes=VMEM_LIMIT_BYTES)


def _row_tile(rows, pref, mult=BF16_ROWS):
    t = min(rows, pref)
    t -= t % mult
    while t > mult and rows % t:
        t -= mult
    assert t > 0 and rows % t == 0, (rows, pref)
    return t


def _mesh_pos():
    return tuple(lax.axis_index(a) for a in MESH_AXES)


def _any_spec():
    return pl.BlockSpec(memory_space=pl.ANY)


class _Gather:
    def __init__(self, shards):
        self.arrays = list(shards)
        n = len(shards)
        self.out_shape = [jax.ShapeDtypeStruct((N_DEV,) + s.shape, s.dtype) for s in shards]
        self.scratch = [pltpu.SemaphoreType.DMA((n, 7)), pltpu.SemaphoreType.DMA((n, 7)), pltpu.SemaphoreType.DMA((n,))]

    def phases(self, ins, outs, sems):
        n = len(ins)
        send_sems, recv_sems, local_sems = sems
        x, y, c = _mesh_pos()
        me, sibling = (x, y, c), (x, y, 1 - c)
        chips = [(1 - x, y), (x, 1 - y), (1 - x, 1 - y)]

        def copy(a, k, block, to, src=None):
            dst = outs[a].at[4 * block[0] + 2 * block[1] + block[2]]
            return pltpu.make_async_remote_copy(
                src_ref=dst if src is None else src, dst_ref=dst,
                send_sem=send_sems.at[a, k], recv_sem=recv_sems.at[a, k],
                device_id=to, device_id_type=MESH)

        def mine():
            return [pltpu.make_async_copy(ins[a], outs[a].at[4 * x + 2 * y + c], local_sems.at[a]) for a in range(n)]

        def first():
            return [cp for a in range(n) for cp in
                    [copy(a, 0, me, sibling, src=ins[a])]
                    + [copy(a, 1 + j, me, (*chip, c), src=ins[a]) for j, chip in enumerate(chips)]]

        def passed():
            return [copy(a, 4 + j, (*chip, c), sibling) for j, chip in enumerate(chips) for a in range(n)]

        def start():
            for cp in mine() + first():
                cp.start()

        def middle():
            for j, chip in enumerate(chips):
                for a in range(n):
                    copy(a, 1 + j, (*chip, c), me).wait_recv()
            for cp in passed():
                cp.start()

        def finish():
            for a in range(n):
                copy(a, 0, sibling, me).wait_recv()
                for j, chip in enumerate(chips):
                    copy(a, 4 + j, (*chip, 1 - c), me).wait_recv()
            for cp in first() + passed():
                cp.wait_send()
            for cp in mine():
                cp.wait()

        return start, middle, finish


class _ChipExchange:
    def __init__(self, parts):
        self.arrays = list(parts)
        n = len(parts)
        self.out_shape = [jax.ShapeDtypeStruct(s.shape, s.dtype) for s in parts]
        self.scratch = [pltpu.SemaphoreType.DMA((n, 3)), pltpu.SemaphoreType.DMA((n, 3)), pltpu.SemaphoreType.DMA((n,))]

    def phases(self, ins, outs, sems):
        n = len(ins)
        send_sems, recv_sems, local_sems = sems
        x, y, c = _mesh_pos()
        my_chip = 2 * x + y
        chips = [(1 - x, y), (x, 1 - y), (1 - x, 1 - y)]

        def own():
            return [pltpu.make_async_copy(ins[a].at[my_chip], outs[a].at[my_chip], local_sems.at[a]) for a in range(n)]

        def copy(a, k, src_chip, dst_chip, to):
            return pltpu.make_async_remote_copy(
                src_ref=ins[a].at[src_chip], dst_ref=outs[a].at[dst_chip],
                send_sem=send_sems.at[a, k], recv_sem=recv_sems.at[a, k],
                device_id=to, device_id_type=MESH)

        def sends():
            return [copy(a, k, 2 * px + py, my_chip, (px, py, c)) for a in range(n) for k, (px, py) in enumerate(chips)]

        def start():
            for cp in own() + sends():
                cp.start()

        def finish():
            for cp in sends():
                cp.wait_send()
            for a in range(n):
                for k, (px, py) in enumerate(chips):
                    copy(a, k, my_chip, 2 * px + py, (px, py, c)).wait_recv()
            for cp in own():
                cp.wait()

        return start, None, finish


def _exchange_alone(job, name):
    n = len(job.arrays)

    def body(*refs):
        for phase in job.phases(refs[:n], refs[n:2 * n], refs[2 * n:]):
            if phase is not None:
                phase()

    return pl.pallas_call(
        body, name=name, out_shape=job.out_shape,
        in_specs=[_any_spec()] * n, out_specs=[_any_spec()] * n, scratch_shapes=job.scratch,
    )(*job.arrays)


def _riding_call(body, job, n_in, n_out, n_steps, step, **kw):
    if job is None:
        return pl.pallas_call(body, **kw)
    n_job, n_sem = len(job.arrays), len(job.scratch)
    assert n_steps >= 3
    kw = dict(kw, in_specs=list(kw["in_specs"]) + [_any_spec()] * n_job,
              out_specs=list(kw["out_specs"]) + [_any_spec()] * n_job,
              out_shape=list(kw["out_shape"]) + job.out_shape,
              scratch_shapes=list(kw.get("scratch_shapes", ())) + job.scratch)

    def riding(*refs):
        ins, job_ins = refs[:n_in], refs[n_in:n_in + n_job]
        outs = refs[n_in + n_job:n_in + n_job + n_out]
        job_outs = refs[n_in + n_job + n_out:n_in + 2 * n_job + n_out]
        scratch = refs[n_in + 2 * n_job + n_out:]
        start, middle, finish = job.phases(job_ins, job_outs, scratch[len(scratch) - n_sem:])
        now = step()
        pl.when(now == 0)(start)
        if middle is not None:
            pl.when(now == (5 * n_steps) // 8)(middle)
        body(*ins, *outs, *scratch[:len(scratch) - n_sem])
        pl.when(now == n_steps - 1)(finish)

    call = pl.pallas_call(riding, **kw)
    return lambda *args: call(*args, *job.arrays)


def _pair_exchange(slabs, name):
    n = len(slabs)

    def body(*refs):
        ins, outs = refs[:n], refs[n:2 * n]
        send_sems, recv_sems = refs[2 * n:]
        x, y, c = _mesh_pos()
        copies = [
            pltpu.make_async_remote_copy(
                src_ref=ins[a].at[2 * j + 1 - c], dst_ref=outs[a].at[j],
                send_sem=send_sems.at[a, j], recv_sem=recv_sems.at[a, j],
                device_id=(x, y, 1 - c), device_id_type=MESH)
            for a in range(n) for j in range(4)]
        for cp in copies:
            cp.start()
        for cp in copies:
            cp.wait_send()
        for cp in copies:
            cp.wait_recv()

    return pl.pallas_call(
        body, name=name,
        out_shape=[jax.ShapeDtypeStruct((4,) + s.shape[1:], s.dtype) for s in slabs],
        in_specs=[_any_spec()] * n, out_specs=[_any_spec()] * n,
        scratch_shapes=[pltpu.SemaphoreType.DMA((n, 4)), pltpu.SemaphoreType.DMA((n, 4))],
    )(*slabs)


def _pair_add(slab, got, core, name):
    _, rows, d = slab.shape
    tr = _row_tile(rows, 1024)

    def body(core_ref, mine_ref, got_ref, out_ref):
        out_ref[...] = (mine_ref[...].astype(F32) + got_ref[...].astype(F32)).astype(out_ref.dtype)

    grid_spec = pltpu.PrefetchScalarGridSpec(
        num_scalar_prefetch=1, grid=(4, rows // tr),
        in_specs=[pl.BlockSpec((1, tr, d), lambda j, r, core_ref: (2 * j + core_ref[0], r, 0)),
                  pl.BlockSpec((1, tr, d), lambda j, r, core_ref: (j, r, 0))],
        out_specs=pl.BlockSpec((1, tr, d), lambda j, r, core_ref: (j, r, 0)))
    return pl.pallas_call(body, name=name, grid_spec=grid_spec,
                          out_shape=jax.ShapeDtypeStruct((4, rows, d), slab.dtype),
                          compiler_params=_params())(core, slab, got)


def _sum_blocks(parts, name):
    k, rows, d = parts.shape
    tr = _row_tile(rows, 512, BF16_ROWS if parts.dtype == BF16 else 8)

    def body(in_ref, out_ref):
        acc = in_ref[0].astype(F32)
        for j in range(1, k):
            acc = acc + in_ref[j].astype(F32)
        out_ref[...] = acc

    return pl.pallas_call(
        body, name=name, grid=(rows // tr,),
        in_specs=[pl.BlockSpec((k, tr, d), lambda r: (0, r, 0))],
        out_specs=pl.BlockSpec((tr, d), lambda r: (r, 0)),
        out_shape=jax.ShapeDtypeStruct((rows, d), F32), compiler_params=_params())(parts)


def _ln_stats(z):
    mu = jnp.mean(z, axis=-1, keepdims=True)
    zc = z - mu
    var = jnp.mean(zc * zc, axis=-1, keepdims=True)
    rstd = lax.rsqrt(var + LN_EPS)
    return zc * rstd, rstd


def _ln_bwd(dy, z, g):
    zhat, rstd = _ln_stats(z)
    dyg = dy * g
    m1 = jnp.mean(dyg, axis=-1, keepdims=True)
    m2 = jnp.mean(dyg * zhat, axis=-1, keepdims=True)
    dz = rstd * (dyg - m1 - zhat * m2)
    return dz, jnp.sum(dy * zhat, axis=0, keepdims=True), jnp.sum(dy, axis=0, keepdims=True)


def _accumulate(ref, value, first):
    @pl.when(first)
    def _():
        ref[...] = value

    @pl.when(jnp.logical_not(first))
    def _():
        ref[...] += value


def _add_matmul(acc_ref, first, matmul):
    @pl.when(first)
    def _():
        acc_ref[...] = jnp.zeros_like(acc_ref)

    acc_ref[...] += matmul()


def _ffn_weight_specs(fs, d, index_of):
    def spec(row):
        return pl.BlockSpec((FFN_CHUNK_DEVS, 1, fs, d), lambda *g: (index_of(*g), row, 0, 0))
    return [spec(0), spec(1), spec(2)]


def _ffn_fwd(x, w, ln_g, ln_b, alpha, name, job=None):
    s, d = x.shape
    fs = w.shape[2]
    tf = FFN_CHUNK_DEVS * fs
    n_c = N_DEV // FFN_CHUNK_DEVS
    tm = min(s, 512)

    def body(x_ref, wg_ref, wu_ref, wd_ref, g_ref, b_ref, a_ref, u_ref, z_ref, y_ref, xb_s, acc_s):
        c = pl.program_id(1)

        @pl.when(c == 0)
        def _():
            xb_s[...] = x_ref[...].astype(BF16)
            acc_s[...] = jnp.zeros_like(acc_s)

        xb = xb_s[...]
        a = _dot_nt(xb, wg_ref[...].reshape(tf, d))
        u = _dot_nt(xb, wu_ref[...].reshape(tf, d))
        a_ref[...] = a.astype(BF16)
        u_ref[...] = u.astype(BF16)
        h = (a * jax.nn.sigmoid(a)) * u
        acc_s[...] += _dot(h.astype(BF16), wd_ref[...].reshape(tf, d))

        @pl.when(c == n_c - 1)
        def _():
            z = alpha * x_ref[...] + 0.5 * acc_s[...]
            zhat, _ = _ln_stats(z)
            z_ref[...] = z
            y_ref[...] = zhat * g_ref[...] + b_ref[...]

    row = pl.BlockSpec((tm, d), lambda i, c: (i, 0))
    vec = pl.BlockSpec((1, d), lambda i, c: (0, 0))
    hid = pl.BlockSpec((tm, tf), lambda i, c: (i, c))
    call = _riding_call(
        body, job, 6, 4, (s // tm) * n_c, lambda: pl.program_id(0) * n_c + pl.program_id(1),
        name=name, grid=(s // tm, n_c),
        in_specs=[row] + _ffn_weight_specs(fs, d, lambda i, c: c) + [vec, vec],
        out_specs=[hid, hid, row, row],
        out_shape=[jax.ShapeDtypeStruct((s, N_DEV * fs), BF16)] * 2 + [jax.ShapeDtypeStruct((s, d), F32)] * 2,
        scratch_shapes=[pltpu.VMEM((tm, d), BF16), pltpu.VMEM((tm, d), F32)],
        compiler_params=_params())
    return call(x, w, w, w, ln_g, ln_b)


def _ffn_bwd_dx(dy, z, a, u, w, ln_g, alpha, name, job=None):
    s, d = dy.shape
    fs = w.shape[2]
    tf = FFN_CHUNK_DEVS * fs
    n_c = N_DEV // FFN_CHUNK_DEVS
    tm = min(s, 512)

    def body(dy_ref, z_ref, a_ref, u_ref, wg_ref, wu_ref, wd_ref, g_ref,
             dx_ref, da_ref, du_ref, df_ref, dg_ref, db_ref, df_s, acc_s):
        i, c = pl.program_id(0), pl.program_id(1)

        @pl.when(c == 0)
        def _():
            dz, dg, db = _ln_bwd(dy_ref[...], z_ref[...], g_ref[...])
            _accumulate(dg_ref, dg, i == 0)
            _accumulate(db_ref, db, i == 0)
            df = (0.5 * dz).astype(BF16)
            df_s[...] = df
            df_ref[...] = df
            acc_s[...] = alpha * dz

        av = a_ref[...].astype(F32)
        uv = u_ref[...].astype(F32)
        sg = jax.nn.sigmoid(av)
        dh = _dot_nt(df_s[...], wd_ref[...].reshape(tf, d))
        du = (dh * (av * sg)).astype(BF16)
        da = (dh * uv * (sg * (1.0 + av * (1.0 - sg)))).astype(BF16)
        da_ref[...] = da
        du_ref[...] = du
        acc_s[...] += _dot(da, wg_ref[...].reshape(tf, d)) + _dot(du, wu_ref[...].reshape(tf, d))

        @pl.when(c == n_c - 1)
        def _():
            dx_ref[...] = acc_s[...]

    row = pl.BlockSpec((tm, d), lambda i, c: (i, 0))
    vec = pl.BlockSpec((1, d), lambda i, c: (0, 0))
    hid = pl.BlockSpec((tm, tf), lambda i, c: (i, c))
    call = _riding_call(
        body, job, 8, 6, (s // tm) * n_c, lambda: pl.program_id(0) * n_c + pl.program_id(1),
        name=name, grid=(s // tm, n_c),
        in_specs=[row, row, hid, hid] + _ffn_weight_specs(fs, d, lambda i, c: c) + [vec],
        out_specs=[row, hid, hid, row, vec, vec],
        out_shape=[jax.ShapeDtypeStruct((s, d), F32)] + [jax.ShapeDtypeStruct((s, N_DEV * fs), BF16)] * 2
                  + [jax.ShapeDtypeStruct((s, d), BF16)] + [jax.ShapeDtypeStruct((1, d), F32)] * 2,
        scratch_shapes=[pltpu.VMEM((tm, d), BF16), pltpu.VMEM((tm, d), F32)],
        compiler_params=_params())
    return call(dy, z, a, u, w, w, w, ln_g)


def _ffn_bwd_dwgu(da, du, x, fs, name):
    s, d = x.shape
    tf = FFN_CHUNK_DEVS * fs
    n_c = N_DEV // FFN_CHUNK_DEVS
    tk = min(s, 512)
    n_k = s // tk

    def body(da_ref, du_ref, x_ref, out_ref, accg_s, accu_s):
        k = pl.program_id(1)
        xb = x_ref[...].astype(BF16)
        _add_matmul(accg_s, k == 0, lambda: _dot_tn(da_ref[...], xb))
        _add_matmul(accu_s, k == 0, lambda: _dot_tn(du_ref[...], xb))

        @pl.when(k == n_k - 1)
        def _():
            out_ref[:, 0] = accg_s[...].astype(BF16).reshape(FFN_CHUNK_DEVS, fs, d)
            out_ref[:, 1] = accu_s[...].astype(BF16).reshape(FFN_CHUNK_DEVS, fs, d)

    hid = pl.BlockSpec((tk, tf), lambda c, k: (k, c))
    return pl.pallas_call(
        body, name=name, grid=(n_c, n_k),
        in_specs=[hid, hid, pl.BlockSpec((tk, d), lambda c, k: (k, 0))],
        out_specs=pl.BlockSpec((FFN_CHUNK_DEVS, 2, fs, d), lambda c, k: (c, 0, 0, 0)),
        out_shape=jax.ShapeDtypeStruct((N_DEV, 3, fs, d), BF16),
        scratch_shapes=[pltpu.VMEM((tf, d), F32), pltpu.VMEM((tf, d), F32)],
        compiler_params=_params())(da, du, x)


def _ffn_bwd_dwd(a, u, df, blocks, name):
    s, d = df.shape
    fs = blocks.shape[2]
    tf = FFN_CHUNK_DEVS * fs
    n_c = N_DEV // FFN_CHUNK_DEVS
    tk = min(s, 512)
    n_k = s // tk

    def body(a_ref, u_ref, df_ref, blocks_ref, out_ref, acc_s):
        k = pl.program_id(1)
        av = a_ref[...].astype(F32)
        h = ((av * jax.nn.sigmoid(av)) * u_ref[...].astype(F32)).astype(BF16)
        _accumulate(acc_s, _dot_tn(h, df_ref[...]), k == 0)

        @pl.when(k == n_k - 1)
        def _():
            out_ref[:, 0] = acc_s[...].astype(BF16).reshape(FFN_CHUNK_DEVS, fs, d)

    hid = pl.BlockSpec((tk, tf), lambda c, k: (k, c))
    return pl.pallas_call(
        body, name=name, grid=(n_c, n_k),
        in_specs=[hid, hid, pl.BlockSpec((tk, d), lambda c, k: (k, 0)), _any_spec()],
        out_specs=pl.BlockSpec((FFN_CHUNK_DEVS, 1, fs, d), lambda c, k: (c, 2, 0, 0)),
        out_shape=jax.ShapeDtypeStruct(blocks.shape, BF16), input_output_aliases={3: 0},
        scratch_shapes=[pltpu.VMEM((tf, d), F32)],
        compiler_params=_params())(a, u, df, blocks)


def _whole(arr):
    return pl.BlockSpec(arr.shape, lambda i: (0,) * arr.ndim)


def _win_fwd(x, w_in, name):
    s, d = x.shape
    d_in = N_DEV * w_in.shape[1]
    tm = min(s, 512)

    def body(x_ref, w_ref, out_ref):
        out_ref[...] = _dot_nt(x_ref[...].astype(BF16), w_ref[...].reshape(d_in, d))

    return pl.pallas_call(
        body, name=name, grid=(s // tm,),
        in_specs=[pl.BlockSpec((tm, d), lambda i: (i, 0)), _whole(w_in)],
        out_specs=pl.BlockSpec((tm, d_in), lambda i: (i, 0)),
        out_shape=jax.ShapeDtypeStruct((s, d_in), F32), compiler_params=_params())(x, w_in)


def _wout_fwd(x, yab, yc, w_out, ln_g, ln_b, alpha, name):
    s, d = x.shape
    tm = min(s, 512)

    def body(x_ref, yab_ref, yc_ref, w_ref, g_ref, b_ref, z_ref, y_ref):
        mix = jnp.concatenate([yab_ref[...], yc_ref[...]], axis=1).astype(BF16)
        z = alpha * x_ref[...] + _dot(mix, w_ref[...].reshape(D_MIX, d))
        zhat, _ = _ln_stats(z)
        z_ref[...] = z
        y_ref[...] = zhat * g_ref[...] + b_ref[...]

    row = pl.BlockSpec((tm, d), lambda i: (i, 0))
    half = pl.BlockSpec((tm, D_MIX // 2), lambda i: (i, 0))
    vec = pl.BlockSpec((1, d), lambda i: (0, 0))
    return pl.pallas_call(
        body, name=name, grid=(s // tm,),
        in_specs=[row, half, half, _whole(w_out), vec, vec],
        out_specs=[row, row], out_shape=[jax.ShapeDtypeStruct((s, d), F32)] * 2,
        compiler_params=_params())(x, yab, yc, w_out, ln_g, ln_b)


def _wout_bwd(dy, z, yab, yc, w_out, ln_g, alpha, name):
    s, d = dy.shape
    rs = w_out.shape[1]
    tm = min(s, 512)
    n_i = s // tm

    def body(dy_ref, z_ref, yab_ref, yc_ref, w_ref, g_ref, dmix_ref, dxp_ref, dg_ref, db_ref, out_ref, acc_s):
        i = pl.program_id(0)
        dz, dg, db = _ln_bwd(dy_ref[...], z_ref[...], g_ref[...])
        _accumulate(dg_ref, dg, i == 0)
        _accumulate(db_ref, db, i == 0)
        dxp_ref[...] = alpha * dz
        dzb = dz.astype(BF16)
        dmix_ref[...] = _dot_nt(dzb, w_ref[...].reshape(D_MIX, d))
        mix = jnp.concatenate([yab_ref[...], yc_ref[...]], axis=1).astype(BF16)
        _add_matmul(acc_s, i == 0, lambda: _dot_tn(mix, dzb))

        @pl.when(i == n_i - 1)
        def _():
            out_ref[...] = acc_s[...].astype(BF16).reshape(N_DEV, rs, d)

    row = pl.BlockSpec((tm, d), lambda i: (i, 0))
    half = pl.BlockSpec((tm, D_MIX // 2), lambda i: (i, 0))
    vec = pl.BlockSpec((1, d), lambda i: (0, 0))
    return pl.pallas_call(
        body, name=name, grid=(n_i,),
        in_specs=[row, row, half, half, _whole(w_out), vec],
        out_specs=[pl.BlockSpec((tm, D_MIX), lambda i: (i, 0)), row, vec, vec, _whole(w_out)],
        out_shape=[jax.ShapeDtypeStruct((s, D_MIX), F32), jax.ShapeDtypeStruct((s, d), F32),
                   jax.ShapeDtypeStruct((1, d), F32), jax.ShapeDtypeStruct((1, d), F32),
                   jax.ShapeDtypeStruct(w_out.shape, BF16)],
        scratch_shapes=[pltpu.VMEM((D_MIX, d), F32)],
        compiler_params=_params())(dy, z, yab, yc, w_out, ln_g)


def _win_bwd(dxp, dloc, dq, dk, dv, x, w_in, name):
    s, d = x.shape
    rs = w_in.shape[1]
    d_in = N_DEV * rs
    tm = min(s, 256)
    n_i = s // tm

    def body(dxp_ref, dloc_ref, dq_ref, dk_ref, dv_ref, x_ref, w_ref, dx_ref, out_ref, acc_s):
        i = pl.program_id(0)
        dp = jnp.concatenate([dloc_ref[...], dq_ref[...], dk_ref[...].astype(BF16), dv_ref[...].astype(BF16)], axis=1)
        dx_ref[...] = dxp_ref[...] + _dot(dp, w_ref[...].reshape(d_in, d))
        _add_matmul(acc_s, i == 0, lambda: _dot_tn(dp, x_ref[...].astype(BF16)))

        @pl.when(i == n_i - 1)
        def _():
            out_ref[...] = acc_s[...].astype(BF16).reshape(N_DEV, rs, d)

    row = pl.BlockSpec((tm, d), lambda i: (i, 0))
    na = pl.BlockSpec((tm, D_NA), lambda i: (i, 0))
    return pl.pallas_call(
        body, name=name, grid=(n_i,),
        in_specs=[row, pl.BlockSpec((tm, D_LOC), lambda i: (i, 0)), na, na, na, row, _whole(w_in)],
        out_specs=[row, _whole(w_in)],
        out_shape=[jax.ShapeDtypeStruct((s, d), F32), jax.ShapeDtypeStruct(w_in.shape, BF16)],
        scratch_shapes=[pltpu.VMEM((d_in, d), F32)],
        compiler_params=_params())(dxp, dloc, dq, dk, dv, x, w_in)


def _shift_rows(v, k):
    n = v.shape[0]
    return pltpu.roll(v, k % n, 0)


def _halo_specs(tm, s, width, col):
    per = tm // HALO
    last = s // HALO - 1
    return [pl.BlockSpec((HALO, width), lambda i: (jnp.maximum(i * per - 1, 0), col)),
            pl.BlockSpec((tm, width), lambda i: (i, col)),
            pl.BlockSpec((HALO, width), lambda i: (jnp.minimum((i + 1) * per, last), col))]


def _token_index(i, tm):
    return i * tm - HALO + lax.broadcasted_iota(jnp.int32, (tm + 2 * HALO, 1), 0)


def _pool_lane_tables():
    lane = lax.broadcasted_iota(jnp.int32, (1, D_POOL), 1)
    group = sum((lane >= g * POOL_GROUP).astype(jnp.int32) for g in range(1, len(POOL_WINDOWS)))
    half = jnp.where(group == 0, 1, jnp.where(group == 1, 2, jnp.where(group == 2, 4, 8)))
    return group, half


def _window_sums(v, group, offsets):
    s2 = v + _shift_rows(v, 1)
    s4 = s2 + _shift_rows(s2, 2)
    s8 = s4 + _shift_rows(s4, 4)
    s16 = s8 + _shift_rows(s8, 8)
    parts = [_shift_rows(p, -o) if o else p for p, o in zip((s2, s4, s8, s16), offsets)]
    return jnp.where(group == 0, parts[0], jnp.where(group == 1, parts[1], jnp.where(group == 2, parts[2], parts[3])))


def _pool_counts(tok, half, s):
    return (jnp.minimum(tok + half, s) - jnp.maximum(tok - half, 0)).astype(F32)


def _pool_forward(u, tok, s):
    group, half = _pool_lane_tables()
    sums = _window_sums(u, group, [w // 2 - 1 for w in POOL_WINDOWS])
    return sums / _pool_counts(tok, half, s) - u


def _conv_forward(zc, cw_ref):
    return cw_ref[0:1, :] * _shift_rows(zc, 1) + cw_ref[1:2, :] * zc + cw_ref[2:3, :] * _shift_rows(zc, -1)


def _local_fwd(proj, pool_bd, pool_scale, conv_w, name):
    s = proj.shape[0]
    tm = min(s, 512)
    ctr = slice(HALO, HALO + tm)

    def body(prev_ref, cur_ref, next_ref, pw_ref, sc_ref, cw_ref, out_ref):
        i = pl.program_id(0)
        ext = jnp.concatenate([prev_ref[...], cur_ref[...], next_ref[...]], axis=0)
        tok = _token_index(i, tm)
        inside = (tok >= 0) & (tok < s)
        u = jnp.where(inside, ext[:, 0:D_POOL], 0.0)
        p = _pool_forward(u, tok, s)[ctr]
        ya = _dot(p.astype(BF16), pw_ref[...]) * sc_ref[...]
        gb = ext[:, D_POOL:D_POOL + D_CONV]
        zc = jnp.where(inside, ext[:, D_POOL + D_CONV:D_POOL + 2 * D_CONV] * ext[:, D_POOL + 2 * D_CONV:D_LOC], 0.0)
        yb = (gb * _conv_forward(zc, cw_ref))[ctr]
        out_ref[...] = jnp.concatenate([ya, yb], axis=1)

    return pl.pallas_call(
        body, name=name, grid=(s // tm,),
        in_specs=_halo_specs(tm, s, D_LOC, 0) + [
            pl.BlockSpec((D_POOL, D_POOL), lambda i: (0, 0)), pl.BlockSpec((1, D_POOL), lambda i: (0, 0)),
            pl.BlockSpec((3, D_CONV), lambda i: (0, 0))],
        out_specs=pl.BlockSpec((tm, D_POOL + D_CONV), lambda i: (i, 0)),
        out_shape=jax.ShapeDtypeStruct((s, D_POOL + D_CONV), F32),
        compiler_params=_params())(proj, proj, proj, pool_bd, pool_scale, conv_w)


def _local_bwd(proj, dmix, pool_bd, pool_scale, conv_w, name):
    s = proj.shape[0]
    tm = min(s, 512)
    ctr = slice(HALO, HALO + tm)

    def body(prev_ref, cur_ref, next_ref, dprev_ref, dcur_ref, dnext_ref, pw_ref, sc_ref, cw_ref,
             dloc_ref, dpw_ref, dsc_ref, dcw_ref):
        i = pl.program_id(0)
        first = i == 0
        ext = jnp.concatenate([prev_ref[...], cur_ref[...], next_ref[...]], axis=0)
        dext = jnp.concatenate([dprev_ref[...], dcur_ref[...], dnext_ref[...]], axis=0)
        tok = _token_index(i, tm)
        inside = (tok >= 0) & (tok < s)
        group, half = _pool_lane_tables()
        cnt = _pool_counts(tok, half, s)
        u = jnp.where(inside, ext[:, 0:D_POOL], 0.0)
        dya = jnp.where(inside, dext[:, 0:D_POOL], 0.0)
        p_c = _pool_forward(u, tok, s)[ctr].astype(BF16)
        lin = _dot(p_c, pw_ref[...])
        _accumulate(dsc_ref, jnp.sum(dya[ctr] * lin, axis=0, keepdims=True), first)
        e1 = (dya * sc_ref[...]).astype(BF16)
        _accumulate(dpw_ref, _dot_tn(p_c, e1[ctr]), first)
        dp = _dot_nt(e1, pw_ref[...])
        du = _window_sums(dp / cnt, group, [w // 2 for w in POOL_WINDOWS]) - dp
        gb = ext[:, D_POOL:D_POOL + D_CONV]
        gc = ext[:, D_POOL + D_CONV:D_POOL + 2 * D_CONV]
        hv = ext[:, D_POOL + 2 * D_CONV:D_LOC]
        zc = jnp.where(inside, gc * hv, 0.0)
        dyb = jnp.where(inside, dext[:, D_POOL:D_POOL + D_CONV], 0.0)
        dgb = dyb * _conv_forward(zc, cw_ref)
        dyc = dyb * gb
        for k in range(3):
            part = jnp.sum(dyc[ctr] * _shift_rows(zc, 1 - k)[ctr], axis=0, keepdims=True)
            _accumulate(dcw_ref.at[k:k + 1, :], part, first)
        dzc = cw_ref[0:1, :] * _shift_rows(dyc, -1) + cw_ref[1:2, :] * dyc + cw_ref[2:3, :] * _shift_rows(dyc, 1)
        dloc = jnp.concatenate([du, dgb, dzc * hv, dzc * gc], axis=1)
        dloc_ref[...] = dloc[ctr].astype(BF16)

    return pl.pallas_call(
        body, name=name, grid=(s // tm,),
        in_specs=_halo_specs(tm, s, D_LOC, 0) + _halo_specs(tm, s, D_POOL + D_CONV, 0) + [
            pl.BlockSpec((D_POOL, D_POOL), lambda i: (0, 0)), pl.BlockSpec((1, D_POOL), lambda i: (0, 0)),
            pl.BlockSpec((3, D_CONV), lambda i: (0, 0))],
        out_specs=[pl.BlockSpec((tm, D_LOC), lambda i: (i, 0)), pl.BlockSpec((D_POOL, D_POOL), lambda i: (0, 0)),
                   pl.BlockSpec((1, D_POOL), lambda i: (0, 0)), pl.BlockSpec((8, D_CONV), lambda i: (0, 0))],
        out_shape=[jax.ShapeDtypeStruct((s, D_LOC), BF16), jax.ShapeDtypeStruct((D_POOL, D_POOL), F32),
                   jax.ShapeDtypeStruct((1, D_POOL), F32), jax.ShapeDtypeStruct((8, D_CONV), F32)],
        compiler_params=_params())(proj, proj, proj, dmix, dmix, dmix, pool_bd, pool_scale, conv_w)


def _na_geometry(rows):
    n_j = rows // Q_ROWS
    dr = np.full((3, Q_ROWS, K_ROWS), 2 * NA_ROWS - 1, np.int64)
    for t, j in enumerate((0, min(1, n_j - 1), n_j - 1)):
        base = int(np.clip(Q_ROWS * j - NA_ROWS // 2, 0, rows - K_ROWS))
        for qr in range(Q_ROWS):
            r = Q_ROWS * j + qr
            start = int(np.clip(r - NA_ROWS // 2, 0, rows - NA_ROWS))
            for kr in range(K_ROWS):
                if start <= base + kr < start + NA_ROWS:
                    dr[t, qr, kr] = base + kr - r + NA_ROWS - 1
    return dr


def _na_col_tables():
    c = np.arange(GRID_W)
    start = np.clip(c - NA_COLS // 2, 0, GRID_W - NA_COLS)
    valid = (c[None, :] >= start[:, None]) & (c[None, :] < start[:, None] + NA_COLS)
    dc = np.clip(c[None, :] - c[:, None], -(NA_COLS - 1), NA_COLS - 1) + (NA_COLS - 1)
    return valid, dc


NO_ROW = 2 * NA_ROWS - 1
N_SLOT = 2 * NA_ROWS


def _na_tiles(rpb):
    valid, dc = _na_col_tables()
    onehot = jnp.asarray((dc[None] == np.arange(2 * NA_COLS - 1)[:, None, None]).astype(np.float32))
    table = jnp.einsum("hrd,dqk->hrqk", rpb, onehot, precision=lax.Precision.HIGHEST)
    table = jnp.where(jnp.asarray(valid)[None, None], table, NEG_INF)
    outside = jnp.full((NA_HEADS, 1, GRID_W, GRID_W), NEG_INF, F32)
    padded = jnp.concatenate([outside, table, outside], axis=1)
    pairs = jnp.concatenate([padded[:, :N_SLOT], padded[:, 1:]], axis=-1)
    return jnp.concatenate([pairs, jnp.full((NA_HEADS, 1, GRID_W, 2 * GRID_W), NEG_INF, F32)], axis=1)


def _tile_slots(dr_t):
    out = []
    for qr in range(Q_ROWS):
        for kp in range(K_ROWS // 2):
            even, odd = int(dr_t[qr, 2 * kp]), int(dr_t[qr, 2 * kp + 1])
            if even == NO_ROW and odd == NO_ROW:
                continue
            out.append((qr, kp, (even if even != NO_ROW else odd - 1) + 1, even != NO_ROW, odd != NO_ROW))
    return out


def _tile_at(qr, kp):
    return slice(qr * GRID_W, (qr + 1) * GRID_W), slice(kp * 2 * GRID_W, (kp + 1) * 2 * GRID_W)


def _fill_bias(bias_s, tiles_ref, dr_t):
    left = lax.broadcasted_iota(jnp.int32, (1, 2 * GRID_W), 1) < GRID_W
    inside = {(qr, kp): (e, li, ri) for qr, kp, e, li, ri in _tile_slots(dr_t)}
    for hh in range(2):
        for qr in range(Q_ROWS):
            for kp in range(K_ROWS // 2):
                e, li, ri = inside.get((qr, kp), (N_SLOT, True, True))
                tile = tiles_ref[hh, e]
                if not li:
                    tile = jnp.where(left, NEG_INF, tile)
                if not ri:
                    tile = jnp.where(left, tile, NEG_INF)
                rs, cs = _tile_at(qr, kp)
                bias_s[hh, rs, cs] = tile


def _when_type_starts(j, n_j, dr, fn):
    starts = {0: 0, n_j - 1: 2}
    if n_j > 2:
        starts[1] = 1
    for j0, t in starts.items():
        pl.when(j == j0)(functools.partial(fn, dr[t]))


def _na_specs(s, proj_cols):
    n_blk = s // K_BLK
    per = Q_TOK // K_BLK

    def kv_spec(col0, m):
        return pl.BlockSpec((K_BLK, HEAD_PAIR), lambda hp, j: (jnp.clip(per * j - 1, 0, n_blk - 4) + m, col0 + hp))

    q_col, k_col, v_col = (c // HEAD_PAIR for c in proj_cols)
    return ([pl.BlockSpec((Q_TOK, HEAD_PAIR), lambda hp, j: (j, q_col + hp))]
            + [kv_spec(k_col, m) for m in range(4)] + [kv_spec(v_col, m) for m in range(4)])


def _na_block_type(j, n_j):
    return jnp.where(j == 0, 0, jnp.where(j == n_j - 1, 2, 1))


def _head_masks():
    lane = lax.broadcasted_iota(jnp.int32, (1, HEAD_PAIR), 1)
    return [lane < NA_HEAD_DIM, lane >= NA_HEAD_DIM]


def _attn_fwd(proj, tiles, name):
    s = proj.shape[0]
    n_j = s // Q_TOK
    scale = NA_HEAD_DIM ** -0.5
    dr = _na_geometry(s // GRID_W)

    def body(q_ref, k0, k1, k2, k3, v0, v1, v2, v3, tiles_ref, o_ref, lse_ref, bias_s):
        _when_type_starts(pl.program_id(1), n_j, dr, functools.partial(_fill_bias, bias_s, tiles_ref))
        q = q_ref[...]
        kb = jnp.concatenate([r[...] for r in (k0, k1, k2, k3)], axis=0).astype(BF16)
        v = jnp.concatenate([r[...] for r in (v0, v1, v2, v3)], axis=0)
        out = jnp.zeros((Q_TOK, HEAD_PAIR), F32)
        lse = []
        for hh, mask in enumerate(_head_masks()):
            sc = _dot_nt(jnp.where(mask, q, 0.0).astype(BF16), kb) * scale + bias_s[hh]
            mx = jnp.max(sc, axis=-1, keepdims=True)
            p = jnp.exp(sc - mx)
            den = jnp.sum(p, axis=-1, keepdims=True)
            out = out + _dot(p.astype(BF16), jnp.where(mask, v, 0.0).astype(BF16)) * (1.0 / den)
            lse.append(mx + jnp.log(den))
        o_ref[...] = out
        lse_ref[0] = jnp.where(_head_masks()[0], lse[0], lse[1])

    return pl.pallas_call(
        body, name=name, grid=(NA_HEADS // 2, n_j),
        in_specs=_na_specs(s, (D_LOC, D_LOC + D_NA, D_LOC + 2 * D_NA)) + [
            pl.BlockSpec((2, N_SLOT + 1, GRID_W, 2 * GRID_W), lambda hp, j: (hp, 0, 0, 0))],
        out_specs=[pl.BlockSpec((Q_TOK, HEAD_PAIR), lambda hp, j: (j, hp)),
                   pl.BlockSpec((1, Q_TOK, HEAD_PAIR), lambda hp, j: (hp, j, 0))],
        out_shape=[jax.ShapeDtypeStruct((s, D_NA), F32), jax.ShapeDtypeStruct((NA_HEADS // 2, s, HEAD_PAIR), F32)],
        scratch_shapes=[pltpu.VMEM((2, Q_TOK, K_TOK), F32)],
        compiler_params=_params())(*([proj] * 9), tiles)


def _add_tiles(dtile_ref, hh, ds, dr_t):
    sums = {}
    for qr, kp, e, _, _ in _tile_slots(dr_t):
        rs, cs = _tile_at(qr, kp)
        sums[e] = ds[rs, cs] if e not in sums else sums[e] + ds[rs, cs]
    for e, val in sums.items():
        dtile_ref[hh, e] += val


def _attn_bwd(proj, tiles, o, dmix, lse, name, job=None):
    s = proj.shape[0]
    n_j = s // Q_TOK
    n_blk = s // K_BLK
    per = Q_TOK // K_BLK
    scale = NA_HEAD_DIM ** -0.5
    do_col = (D_POOL + D_CONV) // HEAD_PAIR
    dr = _na_geometry(s // GRID_W)
    used_types = [0] + ([1] if n_j > 2 else []) + [2]

    def body(q_ref, k0, k1, k2, k3, v0, v1, v2, v3, tiles_ref, o_ref, do_ref, lse_ref,
             dq_ref, dk_ref, dv_ref, dtile_ref, bias_s):
        j = pl.program_id(1)

        @pl.when(j == 0)
        def _():
            dk_ref[...] = jnp.zeros_like(dk_ref)
            dv_ref[...] = jnp.zeros_like(dv_ref)
            dtile_ref[...] = jnp.zeros_like(dtile_ref)

        _when_type_starts(j, n_j, dr, functools.partial(_fill_bias, bias_s, tiles_ref))
        block_type = _na_block_type(j, n_j)
        base = pl.multiple_of(jnp.clip(per * j - 1, 0, n_blk - 4) * K_BLK, K_BLK)
        q = q_ref[...]
        k = jnp.concatenate([r[...] for r in (k0, k1, k2, k3)], axis=0)
        vb = jnp.concatenate([r[...] for r in (v0, v1, v2, v3)], axis=0).astype(BF16)
        kb = k.astype(BF16)
        do = do_ref[...]
        ov = o_ref[...]
        lse = lse_ref[0]
        dq = jnp.zeros((Q_TOK, HEAD_PAIR), F32)
        dk = jnp.zeros((K_TOK, HEAD_PAIR), F32)
        dv = jnp.zeros((K_TOK, HEAD_PAIR), F32)
        lane = lax.broadcasted_iota(jnp.int32, (1, HEAD_PAIR), 1)
        for hh, mask in enumerate(_head_masks()):
            qh = jnp.where(mask, q, 0.0).astype(BF16)
            doh = jnp.where(mask, do, 0.0)
            dob = doh.astype(BF16)
            lse_h = jnp.sum(jnp.where(lane == hh * NA_HEAD_DIM, lse, 0.0), axis=-1, keepdims=True)
            p = jnp.exp(_dot_nt(qh, kb) * scale + bias_s[hh] - lse_h)
            delta = jnp.sum(doh * ov, axis=-1, keepdims=True)
            ds = p * (_dot_nt(dob, vb) - delta)
            for t in used_types:
                pl.when(block_type == t)(functools.partial(_add_tiles, dtile_ref, hh, ds, dr[t]))
            dsb = ds.astype(BF16)
            dq = dq + _dot(dsb, jnp.where(mask, k, 0.0).astype(BF16))
            dk = dk + _dot_tn(dsb, qh)
            dv = dv + _dot_tn(p.astype(BF16), dob)
        dq_ref[...] = (dq * scale).astype(BF16)
        dk_ref[pl.ds(base, K_TOK), :] += dk * scale
        dv_ref[pl.ds(base, K_TOK), :] += dv

    pair = pl.BlockSpec((Q_TOK, HEAD_PAIR), lambda hp, j: (j, hp))
    whole = pl.BlockSpec((s, HEAD_PAIR), lambda hp, j: (0, hp))
    call = _riding_call(
        body, job, 13, 4, (NA_HEADS // 2) * n_j, lambda: pl.program_id(0) * n_j + pl.program_id(1),
        name=name, grid=(NA_HEADS // 2, n_j),
        in_specs=_na_specs(s, (D_LOC, D_LOC + D_NA, D_LOC + 2 * D_NA)) + [
            pl.BlockSpec((2, N_SLOT + 1, GRID_W, 2 * GRID_W), lambda hp, j: (hp, 0, 0, 0)),
            pair, pl.BlockSpec((Q_TOK, HEAD_PAIR), lambda hp, j: (j, do_col + hp)),
            pl.BlockSpec((1, Q_TOK, HEAD_PAIR), lambda hp, j: (hp, j, 0))],
        out_specs=[pair, whole, whole, pl.BlockSpec((2, N_SLOT, GRID_W, 2 * GRID_W), lambda hp, j: (hp, 0, 0, 0))],
        out_shape=[jax.ShapeDtypeStruct((s, D_NA), BF16), jax.ShapeDtypeStruct((s, D_NA), F32),
                   jax.ShapeDtypeStruct((s, D_NA), F32),
                   jax.ShapeDtypeStruct((NA_HEADS, N_SLOT, GRID_W, 2 * GRID_W), F32)],
        scratch_shapes=[pltpu.VMEM((2, Q_TOK, K_TOK), F32)],
        compiler_params=_params())
    return call(*([proj] * 9), tiles, o, dmix, lse)


def _rpb_finish(tiles, name):
    valid, dc = _na_col_tables()
    n_dc = 2 * NA_COLS - 1
    sel = np.zeros((GRID_W, 2 * GRID_W, LANES), np.float32)
    for qc in range(GRID_W):
        for kc in range(GRID_W):
            if valid[qc, kc]:
                sel[qc, kc, dc[qc, kc]] = 1.0
                sel[qc, GRID_W + kc, LANES // 2 + dc[qc, kc]] = 1.0
    sel = jnp.asarray(sel.reshape(GRID_W * 2 * GRID_W, LANES))
    flat = tiles.reshape(NA_HEADS * 2 * NA_ROWS, GRID_W * 2 * GRID_W)

    def body(a_ref, b_ref, out_ref):
        out_ref[...] = jnp.dot(a_ref[...], b_ref[...], preferred_element_type=F32, precision=lax.Precision.HIGHEST)

    sums = pl.pallas_call(
        body, name=name, out_shape=jax.ShapeDtypeStruct((flat.shape[0], LANES), F32),
        compiler_params=_params())(flat, sel).reshape(NA_HEADS, 2 * NA_ROWS, LANES)
    return sums[:, 1:, :n_dc] + sums[:, :2 * NA_ROWS - 1, LANES // 2:LANES // 2 + n_dc]


def _loss_grad(y, target, name):
    s, d = y.shape
    tm = min(s, 1024)

    def body(y_ref, t_ref, sum_ref, dy_ref):
        diff = y_ref[...] - t_ref[...]
        dy_ref[...] = diff * (1.0 / d)
        part = jnp.zeros((8, LANES), F32) + jnp.sum(diff * diff)
        _accumulate(sum_ref, part, pl.program_id(0) == 0)

    row = pl.BlockSpec((tm, d), lambda i: (i, 0))
    return pl.pallas_call(
        body, name=name, grid=(s // tm,), in_specs=[row, row],
        out_specs=[pl.BlockSpec((8, LANES), lambda i: (0, 0)), row],
        out_shape=[jax.ShapeDtypeStruct((8, LANES), F32), jax.ShapeDtypeStruct((s, d), F32)],
        compiler_params=_params())(y, target)


def _adamw(w, g, m, v, name):
    rows, cols = w.shape
    tr = _row_tile(rows, 512, 8)

    def body(w_ref, g_ref, m_ref, v_ref, d_ref, nm_ref, nv_ref):
        gv = g_ref[...]
        nm = ADAM_B1 * m_ref[...] + (1.0 - ADAM_B1) * gv
        nv = ADAM_B2 * v_ref[...] + (1.0 - ADAM_B2) * (gv * gv)
        m_hat = nm / (1.0 - ADAM_B1 ** ADAM_STEP)
        v_hat = nv / (1.0 - ADAM_B2 ** ADAM_STEP)
        d_ref[...] = -ADAM_LR * (m_hat / (jnp.sqrt(v_hat) + ADAM_EPS) + ADAM_WD * w_ref[...])
        nm_ref[...] = nm
        nv_ref[...] = nv

    blk = pl.BlockSpec((tr, cols), lambda r: (r, 0))
    return pl.pallas_call(
        body, name=name, grid=(rows // tr,), in_specs=[blk] * 4, out_specs=[blk] * 3,
        out_shape=[jax.ShapeDtypeStruct((rows, cols), F32)] * 3, compiler_params=_params())(w, g, m, v)


def _adamw_nd(w, g, m, v, name):
    shape = w.shape
    flat = lambda t: t.reshape(-1, shape[-1])
    return tuple(t.reshape(shape) for t in _adamw(flat(w), flat(g), flat(m), flat(v), name))


def _pack(parts, rows_mult=64):
    flat = jnp.concatenate([p.reshape(-1).astype(F32) for p in parts])
    per = LANES * rows_mult
    total = -(-flat.shape[0] // per) * per
    return jnp.pad(flat, (0, total - flat.shape[0])).reshape(-1, LANES)


def _unpack(packed, shapes):
    flat = packed.reshape(-1)
    out, pos = [], 0
    for shp in shapes:
        n = int(np.prod(shp))
        out.append(flat[pos:pos + n].reshape(shp))
        pos += n
    return out


def kernel(x, ffn1_w_gate, ffn1_w_up, ffn1_w_down, ffn2_w_gate, ffn2_w_up, ffn2_w_down, w_in, pool_w, pool_scale, conv_w, rpb, w_out, ln_g, ln_b, loss_target, m_ffn1_w_gate, m_ffn1_w_up, m_ffn1_w_down, m_ffn2_w_gate, m_ffn2_w_up, m_ffn2_w_down, m_w_in, m_pool_w, m_pool_scale, m_conv_w, m_rpb, m_w_out, m_ln_g, m_ln_b, v_ffn1_w_gate, v_ffn1_w_up, v_ffn1_w_down, v_ffn2_w_gate, v_ffn2_w_up, v_ffn2_w_down, v_w_in, v_pool_w, v_pool_scale, v_conv_w, v_rpb, v_w_out, v_ln_g, v_ln_b):
    n_l, d, fs = ffn1_w_gate.shape
    s = x.shape[1]
    rows = s // GRID_W
    assert x.shape[0] == 1 and s % Q_TOK == 0 and rows >= K_ROWS and fs % BF16_ROWS == 0
    alpha = (2.0 * n_l) ** 0.25
    xi, yi, ci = _mesh_pos()
    me = 4 * xi + 2 * yi + ci
    core = jnp.reshape(ci, (1,)).astype(jnp.int32)
    ln_w, cw_w = ln_g.shape[2], conv_w.shape[2]

    tr = lambda w: jnp.swapaxes(w, 1, 2)
    ffn1_shard = jnp.stack([tr(ffn1_w_gate), tr(ffn1_w_up), ffn1_w_down], axis=1).astype(BF16)
    ffn2_shard = jnp.stack([tr(ffn2_w_gate), tr(ffn2_w_up), ffn2_w_down], axis=1).astype(BF16)
    win_shard, wout_shard = tr(w_in).astype(BF16), w_out.astype(BF16)
    small_shard = _pack([ln_g, ln_b, conv_w])
    w_ffn1, small = _exchange_alone(_Gather([ffn1_shard[0], small_shard]), "gather_first")
    n_ln = n_l * 3 * ln_w
    small = small.reshape(N_DEV, -1)
    unshard = lambda t, width: jnp.moveaxis(t.reshape(N_DEV, n_l, 3, width), 0, 2).reshape(n_l, 3, N_DEV * width)
    ln_g_all = unshard(small[:, :n_ln], ln_w)
    ln_b_all = unshard(small[:, n_ln:2 * n_ln], ln_w)
    conv_all = unshard(small[:, 2 * n_ln:2 * n_ln + n_l * 3 * cw_w], cw_w)
    pool_bd = jnp.zeros((n_l, D_POOL, D_POOL), F32)
    for g in range(len(POOL_WINDOWS)):
        sl = slice(g * POOL_GROUP, (g + 1) * POOL_GROUP)
        pool_bd = pool_bd.at[:, sl, sl].set(pool_w[:, g])
    pool_bd = pool_bd.astype(BF16)
    lnp = lambda arr, l, j: arr[l, j].reshape(1, d)

    saved = []
    h = x.reshape(s, d)
    for l in range(n_l):
        a1, u1, z1, x1, w_in_l, w_out_l, w_ffn2 = _ffn_fwd(
            h, w_ffn1, lnp(ln_g_all, l, 0), lnp(ln_b_all, l, 0), alpha, f"ffn1_fwd_{l}",
            job=_Gather([win_shard[l], wout_shard[l], ffn2_shard[l]]))
        proj = _win_fwd(x1, w_in_l, f"win_fwd_{l}")
        bias = _na_tiles(rpb[l])
        yab = _local_fwd(proj, pool_bd[l], pool_scale[l].reshape(1, D_POOL), conv_all[l], f"local_fwd_{l}")
        yc, lse = _attn_fwd(proj, bias, f"attn_fwd_{l}")
        z2, x2 = _wout_fwd(x1, yab, yc, w_out_l, lnp(ln_g_all, l, 1), lnp(ln_b_all, l, 1), alpha, f"wout_fwd_{l}")
        a2, u2, z3, x3, *w_next = _ffn_fwd(
            x2, w_ffn2, lnp(ln_g_all, l, 2), lnp(ln_b_all, l, 2), alpha, f"ffn2_fwd_{l}",
            job=_Gather([ffn1_shard[l + 1]]) if l + 1 < n_l else None)
        saved.append((h, a1, u1, z1, x1, proj, bias, yab, yc, lse, z2, x2, a2, u2, z3, w_ffn1, w_in_l, w_out_l, w_ffn2))
        h = x3
        if w_next:
            w_ffn1 = w_next[0]

    sq, dh = _loss_grad(h, loss_target.reshape(s, d), "loss_head")
    loss = lax.psum(sq[0, 0] * (0.5 / d), MESH_AXES)

    def pair_sum(blocks, tag):
        flat = [b.reshape(N_DEV, -1, d) for b in blocks]
        got = _pair_exchange(flat, f"grads_pair_exchange_{tag}")
        return [_pair_add(b, g, core, f"grads_pair_add_{tag}_{i}") for i, (b, g) in enumerate(zip(flat, got))]

    small_grads = [None] * n_l
    reduced = [None] * n_l
    waiting = None
    for l in reversed(range(n_l)):
        x0, a1, u1, z1, x1, proj, bias, yab, yc, lse, z2, x2, a2, u2, z3, w_ffn1, w_in_l, w_out_l, w_ffn2 = saved[l]
        dx2, da, du, df, dg3, db3, *crossed = _ffn_bwd_dx(
            dh, z3, a2, u2, w_ffn2, lnp(ln_g_all, l, 2), alpha, f"ffn2_bwd_dx_{l}",
            job=_ChipExchange(waiting) if waiting else None)
        if waiting:
            reduced[l + 1] += crossed
        g2 = _ffn_bwd_dwd(a2, u2, df, _ffn_bwd_dwgu(da, du, x2, fs, f"ffn2_bwd_dwgu_{l}"), f"ffn2_bwd_dwd_{l}")
        p2 = pair_sum([g2], f"ffn2_{l}")
        dmix, dxp, dg2, db2, g_out = _wout_bwd(dx2, z2, yab, yc, w_out_l, lnp(ln_g_all, l, 1), alpha, f"wout_bwd_{l}")
        dq, dk, dv, dtiles, *crossed = _attn_bwd(proj, bias, yc, dmix, lse, f"attn_bwd_{l}", job=_ChipExchange(p2))
        reduced[l] = list(crossed)
        dloc, dpw, dsc, dcw = _local_bwd(proj, dmix, pool_bd[l], pool_scale[l].reshape(1, D_POOL), conv_all[l],
                                         f"local_bwd_{l}")
        dx1, g_in = _win_bwd(dxp, dloc, dq, dk, dv, x1, w_in_l, f"win_bwd_{l}")
        dx0, da, du, df, dg1, db1 = _ffn_bwd_dx(dx1, z1, a1, u1, w_ffn1, lnp(ln_g_all, l, 0), alpha, f"ffn1_bwd_dx_{l}")
        g1 = _ffn_bwd_dwd(a1, u1, df, _ffn_bwd_dwgu(da, du, x0, fs, f"ffn1_bwd_dwgu_{l}"), f"ffn1_bwd_dwd_{l}")
        waiting = pair_sum([g_out, g_in, g1], f"mix_{l}")
        drpb = _rpb_finish(dtiles, f"rpb_finish_{l}")
        dpool = jnp.stack([dpw[g * POOL_GROUP:(g + 1) * POOL_GROUP, g * POOL_GROUP:(g + 1) * POOL_GROUP]
                           for g in range(len(POOL_WINDOWS))])
        small_grads[l] = (jnp.concatenate([dg1, dg2, dg3]), jnp.concatenate([db1, db2, db3]), dcw[0:3], dpool, dsc[0], drpb)
        dh = dx0
    grad_x = dh.reshape(x.shape)

    reduced[0] += _exchange_alone(_ChipExchange(waiting), "grads_chip_exchange_last")
    sums = [[_sum_blocks(q, f"grads_chip_sum_{l}_{i}") for i, q in enumerate(reduced[l])] for l in range(n_l)]
    r_ffn2, r_out, r_in, r_ffn1 = [jnp.stack([sums[l][i] for l in range(n_l)]) for i in range(4)]
    r_ffn1, r_ffn2 = r_ffn1.reshape(n_l, 3, fs, d), r_ffn2.reshape(n_l, 3, fs, d)
    grads = {
        "ffn1_w_gate": tr(r_ffn1[:, 0]), "ffn1_w_up": tr(r_ffn1[:, 1]), "ffn1_w_down": r_ffn1[:, 2],
        "ffn2_w_gate": tr(r_ffn2[:, 0]), "ffn2_w_up": tr(r_ffn2[:, 1]), "ffn2_w_down": r_ffn2[:, 2],
        "w_in": tr(r_in), "w_out": r_out}

    stack = lambda k: jnp.stack([small_grads[l][k] for l in range(n_l)])
    small_shapes = [(n_l, 3, d), (n_l, 3, d), (n_l, 3, D_CONV), pool_w.shape, pool_scale.shape, rpb.shape]
    (small_all,) = _exchange_alone(_Gather([_pack([stack(k) for k in range(6)])]), "gather_small_grads")
    small_sum = _sum_blocks(small_all, "small_grads_sum")
    g_ln_g, g_ln_b, g_conv, g_pool_w, g_pool_scale, g_rpb = _unpack(small_sum, small_shapes)
    own = lambda t, width: lax.dynamic_slice_in_dim(t, me * width, width, axis=2)
    grads.update({"ln_g": own(g_ln_g, ln_w), "ln_b": own(g_ln_b, ln_w), "conv_w": own(g_conv, cw_w),
                  "pool_w": g_pool_w, "pool_scale": g_pool_scale, "rpb": g_rpb})

    weights = dict(ffn1_w_gate=ffn1_w_gate, ffn1_w_up=ffn1_w_up, ffn1_w_down=ffn1_w_down, ffn2_w_gate=ffn2_w_gate,
                   ffn2_w_up=ffn2_w_up, ffn2_w_down=ffn2_w_down, w_in=w_in, pool_w=pool_w, pool_scale=pool_scale,
                   conv_w=conv_w, rpb=rpb, w_out=w_out, ln_g=ln_g, ln_b=ln_b)
    m_in = dict(ffn1_w_gate=m_ffn1_w_gate, ffn1_w_up=m_ffn1_w_up, ffn1_w_down=m_ffn1_w_down, ffn2_w_gate=m_ffn2_w_gate,
                ffn2_w_up=m_ffn2_w_up, ffn2_w_down=m_ffn2_w_down, w_in=m_w_in, pool_w=m_pool_w, pool_scale=m_pool_scale,
                conv_w=m_conv_w, rpb=m_rpb, w_out=m_w_out, ln_g=m_ln_g, ln_b=m_ln_b)
    v_in = dict(ffn1_w_gate=v_ffn1_w_gate, ffn1_w_up=v_ffn1_w_up, ffn1_w_down=v_ffn1_w_down, ffn2_w_gate=v_ffn2_w_gate,
                ffn2_w_up=v_ffn2_w_up, ffn2_w_down=v_ffn2_w_down, w_in=v_w_in, pool_w=v_pool_w, pool_scale=v_pool_scale,
                conv_w=v_conv_w, rpb=v_rpb, w_out=v_w_out, ln_g=v_ln_g, ln_b=v_ln_b)
    names = list(weights)
    large = ["ffn1_w_gate", "ffn1_w_up", "ffn1_w_down", "ffn2_w_gate", "ffn2_w_up", "ffn2_w_down", "w_in", "w_out"]
    tiny = [n for n in names if n not in large]
    delta, new_m, new_v = {}, {}, {}
    for n in large:
        delta[n], new_m[n], new_v[n] = _adamw_nd(weights[n], grads[n], m_in[n], v_in[n], f"adamw_{n}")
    packed = [_pack([t[n] for n in tiny]) for t in (weights, grads, m_in, v_in)]
    tiny_out = _adamw(*packed, "adamw_small")
    tiny_shapes = [weights[n].shape for n in tiny]
    for res, t in zip((delta, new_m, new_v), tiny_out):
        res.update(dict(zip(tiny, _unpack(t, tiny_shapes))))

    return (loss, grad_x, *[grads[n] for n in names], *[delta[n] for n in names],
            *[new_m[n] for n in names], *[new_v[n] for n in names])
```

```python
import functools

import numpy as np
import jax
import jax.numpy as jnp
from jax import lax
from jax.experimental import pallas as pl
from jax.experimental.pallas import tpu as pltpu

F32, BF16 = jnp.float32, jnp.bfloat16
MESH = pl.DeviceIdType.MESH
N_DEV = 8
MESH_AXES = ("x", "y", "c")

LN_EPS = 1e-5
NEG_INF = -1e30
D_POOL = 256
POOL_WINDOWS = (2, 4, 8, 16)
POOL_GROUP = 64
D_CONV = 256
NA_HEADS = 8
NA_HEAD_DIM = 64
D_NA = NA_HEADS * NA_HEAD_DIM
GRID_W = 64
NA_ROWS = 8
NA_COLS = 16
D_LOC = D_POOL + 3 * D_CONV
D_MIX = D_POOL + D_CONV + D_NA
ADAM_LR, ADAM_B1, ADAM_B2, ADAM_EPS, ADAM_WD, ADAM_STEP = 0.001, 0.9, 0.999, 1e-08, 0.01, 10

VMEM_LIMIT_BYTES = 56 * 1024 * 1024
LANES = 128
BF16_ROWS = 16
HALO = 16
Q_ROWS = 8
K_ROWS = 16
Q_TOK = Q_ROWS * GRID_W
K_TOK = K_ROWS * GRID_W
K_BLK = 4 * GRID_W
HEAD_PAIR = 2 * NA_HEAD_DIM
FFN_CHUNK_DEVS = 4

NT = (((1,), (1,)), ((), ()))
TN = (((0,), (0,)), ((), ()))


def _dot(a, b):
    return jnp.dot(a, b, preferred_element_type=F32)


def _dot_nt(a, b):
    return lax.dot_general(a, b, NT, preferred_element_type=F32)


def _dot_tn(a, b):
    return lax.dot_general(a, b, TN, preferred_element_type=F32)


def _params():
    return pltpu.CompilerParams(vmem_limit_bytes=VMEM_LIMIT_BYTES)


def _row_tile(rows, pref, mult=BF16_ROWS):
    t = min(rows, pref)
    t -= t % mult
    while t > mult and rows % t:
        t -= mult
    assert t > 0 and rows % t == 0, (rows, pref)
    return t


def _mesh_pos():
    return tuple(lax.axis_index(a) for a in MESH_AXES)


def _any_spec():
    return pl.BlockSpec(memory_space=pl.ANY)


class _Gather:
    def __init__(self, shards):
        self.arrays = list(shards)
        n = len(shards)
        self.out_shape = [jax.ShapeDtypeStruct((N_DEV,) + s.shape, s.dtype) for s in shards]
        self.scratch = [pltpu.SemaphoreType.DMA((n, 7)), pltpu.SemaphoreType.DMA((n, 7)), pltpu.SemaphoreType.DMA((n,))]

    def phases(self, ins, outs, sems):
        n = len(ins)
        send_sems, recv_sems, local_sems = sems
        x, y, c = _mesh_pos()
        me, sibling = (x, y, c), (x, y, 1 - c)
        chips = [(1 - x, y), (x, 1 - y), (1 - x, 1 - y)]

        def copy(a, k, block, to, src=None):
            dst = outs[a].at[4 * block[0] + 2 * block[1] + block[2]]
            return pltpu.make_async_remote_copy(
                src_ref=dst if src is None else src, dst_ref=dst,
                send_sem=send_sems.at[a, k], recv_sem=recv_sems.at[a, k],
                device_id=to, device_id_type=MESH)

        def mine():
            return [pltpu.make_async_copy(ins[a], outs[a].at[4 * x + 2 * y + c], local_sems.at[a]) for a in range(n)]

        def first():
            return [cp for a in range(n) for cp in
                    [copy(a, 0, me, sibling, src=ins[a])]
                    + [copy(a, 1 + j, me, (*chip, c), src=ins[a]) for j, chip in enumerate(chips)]]

        def passed():
            return [copy(a, 4 + j, (*chip, c), sibling) for j, chip in enumerate(chips) for a in range(n)]

        def start():
            for cp in mine() + first():
                cp.start()

        def middle():
            for j, chip in enumerate(chips):
                for a in range(n):
                    copy(a, 1 + j, (*chip, c), me).wait_recv()
            for cp in passed():
                cp.start()

        def finish():
            for a in range(n):
                copy(a, 0, sibling, me).wait_recv()
                for j, chip in enumerate(chips):
                    copy(a, 4 + j, (*chip, 1 - c), me).wait_recv()
            for cp in first() + passed():
                cp.wait_send()
            for cp in mine():
                cp.wait()

        return start, middle, finish


class _ChipExchange:
    def __init__(self, parts):
        self.arrays = list(parts)
        n = len(parts)
        self.out_shape = [jax.ShapeDtypeStruct(s.shape, s.dtype) for s in parts]
        self.scratch = [pltpu.SemaphoreType.DMA((n, 3)), pltpu.SemaphoreType.DMA((n, 3)), pltpu.SemaphoreType.DMA((n,))]

    def phases(self, ins, outs, sems):
        n = len(ins)
        send_sems, recv_sems, local_sems = sems
        x, y, c = _mesh_pos()
        my_chip = 2 * x + y
        chips = [(1 - x, y), (x, 1 - y), (1 - x, 1 - y)]

        def own():
            return [pltpu.make_async_copy(ins[a].at[my_chip], outs[a].at[my_chip], local_sems.at[a]) for a in range(n)]

        def copy(a, k, src_chip, dst_chip, to):
            return pltpu.make_async_remote_copy(
                src_ref=ins[a].at[src_chip], dst_ref=outs[a].at[dst_chip],
                send_sem=send_sems.at[a, k], recv_sem=recv_sems.at[a, k],
                device_id=to, device_id_type=MESH)

        def sends():
            return [copy(a, k, 2 * px + py, my_chip, (px, py, c)) for a in range(n) for k, (px, py) in enumerate(chips)]

        def start():
            for cp in own() + sends():
                cp.start()

        def finish():
            for cp in sends():
                cp.wait_send()
            for a in range(n):
                for k, (px, py) in enumerate(chips):
                    copy(a, k, my_chip, 2 * px + py, (px, py, c)).wait_recv()
            for cp in own():
                cp.wait()

        return start, None, finish


def _exchange_alone(job, name):
    n = len(job.arrays)

    def body(*refs):
        for phase in job.phases(refs[:n], refs[n:2 * n], refs[2 * n:]):
            if phase is not None:
                phase()

    return pl.pallas_call(
        body, name=name, out_shape=job.out_shape,
        in_specs=[_any_spec()] * n, out_specs=[_any_spec()] * n, scratch_shapes=job.scratch,
    )(*job.arrays)


def _riding_call(body, job, n_in, n_out, n_steps, step, **kw):
    if job is None:
        return pl.pallas_call(body, **kw)
    n_job, n_sem = len(job.arrays), len(job.scratch)
    assert n_steps >= 3
    kw = dict(kw, in_specs=list(kw["in_specs"]) + [_any_spec()] * n_job,
              out_specs=list(kw["out_specs"]) + [_any_spec()] * n_job,
              out_shape=list(kw["out_shape"]) + job.out_shape,
              scratch_shapes=list(kw.get("scratch_shapes", ())) + job.scratch)

    def riding(*refs):
        ins, job_ins = refs[:n_in], refs[n_in:n_in + n_job]
        outs = refs[n_in + n_job:n_in + n_job + n_out]
        job_outs = refs[n_in + n_job + n_out:n_in + 2 * n_job + n_out]
        scratch = refs[n_in + 2 * n_job + n_out:]
        start, middle, finish = job.phases(job_ins, job_outs, scratch[len(scratch) - n_sem:])
        now = step()
        pl.when(now == 0)(start)
        if middle is not None:
            pl.when(now == (5 * n_steps) // 8)(middle)
        body(*ins, *outs, *scratch[:len(scratch) - n_sem])
        pl.when(now == n_steps - 1)(finish)

    call = pl.pallas_call(riding, **kw)
    return lambda *args: call(*args, *job.arrays)


def _pair_exchange(slabs, name):
    n = len(slabs)

    def body(*refs):
        ins, outs = refs[:n], refs[n:2 * n]
        send_sems, recv_sems = refs[2 * n:]
        x, y, c = _mesh_pos()
        copies = [
            pltpu.make_async_remote_copy(
                src_ref=ins[a].at[2 * j + 1 - c], dst_ref=outs[a].at[j],
                send_sem=send_sems.at[a, j], recv_sem=recv_sems.at[a, j],
                device_id=(x, y, 1 - c), device_id_type=MESH)
            for a in range(n) for j in range(4)]
        for cp in copies:
            cp.start()
        for cp in copies:
            cp.wait_send()
        for cp in copies:
            cp.wait_recv()

    return pl.pallas_call(
        body, name=name,
        out_shape=[jax.ShapeDtypeStruct((4,) + s.shape[1:], s.dtype) for s in slabs],
        in_specs=[_any_spec()] * n, out_specs=[_any_spec()] * n,
        scratch_shapes=[pltpu.SemaphoreType.DMA((n, 4)), pltpu.SemaphoreType.DMA((n, 4))],
    )(*slabs)


def _pair_add(slab, got, core, name):
    _, rows, d = slab.shape
    tr = _row_tile(rows, 1024)

    def body(core_ref, mine_ref, got_ref, out_ref):
        out_ref[...] = (mine_ref[...].astype(F32) + got_ref[...].astype(F32)).astype(out_ref.dtype)

    grid_spec = pltpu.PrefetchScalarGridSpec(
        num_scalar_prefetch=1, grid=(4, rows // tr),
        in_specs=[pl.BlockSpec((1, tr, d), lambda j, r, core_ref: (2 * j + core_ref[0], r, 0)),
                  pl.BlockSpec((1, tr, d), lambda j, r, core_ref: (j, r, 0))],
        out_specs=pl.BlockSpec((1, tr, d), lambda j, r, core_ref: (j, r, 0)))
    return pl.pallas_call(body, name=name, grid_spec=grid_spec,
                          out_shape=jax.ShapeDtypeStruct((4, rows, d), slab.dtype),
                          compiler_params=_params())(core, slab, got)


def _sum_blocks(parts, name):
    k, rows, d = parts.shape
    tr = _row_tile(rows, 512, BF16_ROWS if parts.dtype == BF16 else 8)

    def body(in_ref, out_ref):
        acc = in_ref[0].astype(F32)
        for j in range(1, k):
            acc = acc + in_ref[j].astype(F32)
        out_ref[...] = acc

    return pl.pallas_call(
        body, name=name, grid=(rows // tr,),
        in_specs=[pl.BlockSpec((k, tr, d), lambda r: (0, r, 0))],
        out_specs=pl.BlockSpec((tr, d), lambda r: (r, 0)),
        out_shape=jax.ShapeDtypeStruct((rows, d), F32), compiler_params=_params())(parts)


def _ln_stats(z):
    mu = jnp.mean(z, axis=-1, keepdims=True)
    zc = z - mu
    var = jnp.mean(zc * zc, axis=-1, keepdims=True)
    rstd = lax.rsqrt(var + LN_EPS)
    return zc * rstd, rstd


def _ln_bwd(dy, z, g):
    zhat, rstd = _ln_stats(z)
    dyg = dy * g
    m1 = jnp.mean(dyg, axis=-1, keepdims=True)
    m2 = jnp.mean(dyg * zhat, axis=-1, keepdims=True)
    dz = rstd * (dyg - m1 - zhat * m2)
    return dz, jnp.sum(dy * zhat, axis=0, keepdims=True), jnp.sum(dy, axis=0, keepdims=True)


def _accumulate(ref, value, first):
    @pl.when(first)
    def _():
        ref[...] = value

    @pl.when(jnp.logical_not(first))
    def _():
        ref[...] += value


def _add_matmul(acc_ref, first, matmul):
    @pl.when(first)
    def _():
        acc_ref[...] = jnp.zeros_like(acc_ref)

    acc_ref[...] += matmul()


def _ffn_weight_specs(fs, d, index_of):
    def spec(row):
        return pl.BlockSpec((FFN_CHUNK_DEVS, 1, fs, d), lambda *g: (index_of(*g), row, 0, 0))
    return [spec(0), spec(1), spec(2)]


def _ffn_fwd(x, w, ln_g, ln_b, alpha, name, job=None):
    s, d = x.shape
    fs = w.shape[2]
    tf = FFN_CHUNK_DEVS * fs
    n_c = N_DEV // FFN_CHUNK_DEVS
    tm = min(s, 512)

    def body(x_ref, wg_ref, wu_ref, wd_ref, g_ref, b_ref, a_ref, u_ref, z_ref, y_ref, xb_s, acc_s):
        c = pl.program_id(1)

        @pl.when(c == 0)
        def _():
            xb_s[...] = x_ref[...].astype(BF16)
            acc_s[...] = jnp.zeros_like(acc_s)

        xb = xb_s[...]
        a = _dot_nt(xb, wg_ref[...].reshape(tf, d))
        u = _dot_nt(xb, wu_ref[...].reshape(tf, d))
        a_ref[...] = a.astype(BF16)
        u_ref[...] = u.astype(BF16)
        h = (a * jax.nn.sigmoid(a)) * u
        acc_s[...] += _dot(h.astype(BF16), wd_ref[...].reshape(tf, d))

        @pl.when(c == n_c - 1)
        def _():
            z = alpha * x_ref[...] + 0.5 * acc_s[...]
            zhat, _ = _ln_stats(z)
            z_ref[...] = z
            y_ref[...] = zhat * g_ref[...] + b_ref[...]

    row = pl.BlockSpec((tm, d), lambda i, c: (i, 0))
    vec = pl.BlockSpec((1, d), lambda i, c: (0, 0))
    hid = pl.BlockSpec((tm, tf), lambda i, c: (i, c))
    call = _riding_call(
        body, job, 6, 4, (s // tm) * n_c, lambda: pl.program_id(0) * n_c + pl.program_id(1),
        name=name, grid=(s // tm, n_c),
        in_specs=[row] + _ffn_weight_specs(fs, d, lambda i, c: c) + [vec, vec],
        out_specs=[hid, hid, row, row],
        out_shape=[jax.ShapeDtypeStruct((s, N_DEV * fs), BF16)] * 2 + [jax.ShapeDtypeStruct((s, d), F32)] * 2,
        scratch_shapes=[pltpu.VMEM((tm, d), BF16), pltpu.VMEM((tm, d), F32)],
        compiler_params=_params())
    return call(x, w, w, w, ln_g, ln_b)


def _ffn_bwd_dx(dy, z, a, u, w, ln_g, alpha, name, job=None):
    s, d = dy.shape
    fs = w.shape[2]
    tf = FFN_CHUNK_DEVS * fs
    n_c = N_DEV // FFN_CHUNK_DEVS
    tm = min(s, 512)

    def body(dy_ref, z_ref, a_ref, u_ref, wg_ref, wu_ref, wd_ref, g_ref,
             dx_ref, da_ref, du_ref, df_ref, dg_ref, db_ref, df_s, acc_s):
        i, c = pl.program_id(0), pl.program_id(1)

        @pl.when(c == 0)
        def _():
            dz, dg, db = _ln_bwd(dy_ref[...], z_ref[...], g_ref[...])
            _accumulate(dg_ref, dg, i == 0)
            _accumulate(db_ref, db, i == 0)
            df = (0.5 * dz).astype(BF16)
            df_s[...] = df
            df_ref[...] = df
            acc_s[...] = alpha * dz

        av = a_ref[...].astype(F32)
        uv = u_ref[...].astype(F32)
        sg = jax.nn.sigmoid(av)
        dh = _dot_nt(df_s[...], wd_ref[...].reshape(tf, d))
        du = (dh * (av * sg)).astype(BF16)
        da = (dh * uv * (sg * (1.0 + av * (1.0 - sg)))).astype(BF16)
        da_ref[...] = da
        du_ref[...] = du
        acc_s[...] += _dot(da, wg_ref[...].reshape(tf, d)) + _dot(du, wu_ref[...].reshape(tf, d))

        @pl.when(c == n_c - 1)
        def _():
            dx_ref[...] = acc_s[...]

    row = pl.BlockSpec((tm, d), lambda i, c: (i, 0))
    vec = pl.BlockSpec((1, d), lambda i, c: (0, 0))
    hid = pl.BlockSpec((tm, tf), lambda i, c: (i, c))
    call = _riding_call(
        body, job, 8, 6, (s // tm) * n_c, lambda: pl.program_id(0) * n_c + pl.program_id(1),
        name=name, grid=(s // tm, n_c),
        in_specs=[row, row, hid, hid] + _ffn_weight_specs(fs, d, lambda i, c: c) + [vec],
        out_specs=[row, hid, hid, row, vec, vec],
        out_shape=[jax.ShapeDtypeStruct((s, d), F32)] + [jax.ShapeDtypeStruct((s, N_DEV * fs), BF16)] * 2
                  + [jax.ShapeDtypeStruct((s, d), BF16)] + [jax.ShapeDtypeStruct((1, d), F32)] * 2,
        scratch_shapes=[pltpu.VMEM((tm, d), BF16), pltpu.VMEM((tm, d), F32)],
        compiler_params=_params())
    return call(dy, z, a, u, w, w, w, ln_g)


def _ffn_bwd_dwgu(da, du, x, fs, name):
    s, d = x.shape
    tf = FFN_CHUNK_DEVS * fs
    n_c = N_DEV // FFN_CHUNK_DEVS
    tk = min(s, 512)
    n_k = s // tk

    def body(da_ref, du_ref, x_ref, out_ref, accg_s, accu_s):
        k = pl.program_id(1)
        xb = x_ref[...].astype(BF16)
        _add_matmul(accg_s, k == 0, lambda: _dot_tn(da_ref[...], xb))
        _add_matmul(accu_s, k == 0, lambda: _dot_tn(du_ref[...], xb))

        @pl.when(k == n_k - 1)
        def _():
            out_ref[:, 0] = accg_s[...].astype(BF16).reshape(FFN_CHUNK_DEVS, fs, d)
            out_ref[:, 1] = accu_s[...].astype(BF16).reshape(FFN_CHUNK_DEVS, fs, d)

    hid = pl.BlockSpec((tk, tf), lambda c, k: (k, c))
    return pl.pallas_call(
        body, name=name, grid=(n_c, n_k),
        in_specs=[hid, hid, pl.BlockSpec((tk, d), lambda c, k: (k, 0))],
        out_specs=pl.BlockSpec((FFN_CHUNK_DEVS, 2, fs, d), lambda c, k: (c, 0, 0, 0)),
        out_shape=jax.ShapeDtypeStruct((N_DEV, 3, fs, d), BF16),
        scratch_shapes=[pltpu.VMEM((tf, d), F32), pltpu.VMEM((tf, d), F32)],
        compiler_params=_params())(da, du, x)


def _ffn_bwd_dwd(a, u, df, blocks, name):
    s, d = df.shape
    fs = blocks.shape[2]
    tf = FFN_CHUNK_DEVS * fs
    n_c = N_DEV // FFN_CHUNK_DEVS
    tk = min(s, 512)
    n_k = s // tk

    def body(a_ref, u_ref, df_ref, blocks_ref, out_ref, acc_s):
        k = pl.program_id(1)
        av = a_ref[...].astype(F32)
        h = ((av * jax.nn.sigmoid(av)) * u_ref[...].astype(F32)).astype(BF16)
        _accumulate(acc_s, _dot_tn(h, df_ref[...]), k == 0)

        @pl.when(k == n_k - 1)
        def _():
            out_ref[:, 0] = acc_s[...].astype(BF16).reshape(FFN_CHUNK_DEVS, fs, d)

    hid = pl.BlockSpec((tk, tf), lambda c, k: (k, c))
    return pl.pallas_call(
        body, name=name, grid=(n_c, n_k),
        in_specs=[hid, hid, pl.BlockSpec((tk, d), lambda c, k: (k, 0)), _any_spec()],
        out_specs=pl.BlockSpec((FFN_CHUNK_DEVS, 1, fs, d), lambda c, k: (c, 2, 0, 0)),
        out_shape=jax.ShapeDtypeStruct(blocks.shape, BF16), input_output_aliases={3: 0},
        scratch_shapes=[pltpu.VMEM((tf, d), F32)],
        compiler_params=_params())(a, u, df, blocks)


def _whole(arr):
    return pl.BlockSpec(arr.shape, lambda i: (0,) * arr.ndim)


def _win_fwd(x, w_in, name):
    s, d = x.shape
    d_in = N_DEV * w_in.shape[1]
    tm = min(s, 512)
    scale = NA_HEAD_DIM ** -0.5
    assert d_in == D_LOC + 3 * D_NA and scale == 0.125

    def body(x_ref, w_ref, loc_ref, qkv_ref):
        proj = _dot_nt(x_ref[...].astype(BF16), w_ref[...].reshape(d_in, d))
        loc_ref[...] = proj[:, :D_LOC]
        qkv_ref[:, :D_NA] = (proj[:, D_LOC:D_LOC + D_NA] * scale).astype(BF16)
        qkv_ref[:, D_NA:] = proj[:, D_LOC + D_NA:].astype(BF16)

    return pl.pallas_call(
        body, name=name, grid=(s // tm,),
        in_specs=[pl.BlockSpec((tm, d), lambda i: (i, 0)), _whole(w_in)],
        out_specs=[pl.BlockSpec((tm, D_LOC), lambda i: (i, 0)), pl.BlockSpec((tm, 3 * D_NA), lambda i: (i, 0))],
        out_shape=[jax.ShapeDtypeStruct((s, D_LOC), F32), jax.ShapeDtypeStruct((s, 3 * D_NA), BF16)],
        compiler_params=_params())(x, w_in)


def _wout_fwd(x, yab, yc, w_out, ln_g, ln_b, alpha, name):
    s, d = x.shape
    tm = min(s, 512)

    def body(x_ref, yab_ref, yc_ref, w_ref, g_ref, b_ref, z_ref, y_ref):
        mix = jnp.concatenate([yab_ref[...], yc_ref[...]], axis=1).astype(BF16)
        z = alpha * x_ref[...] + _dot(mix, w_ref[...].reshape(D_MIX, d))
        zhat, _ = _ln_stats(z)
        z_ref[...] = z
        y_ref[...] = zhat * g_ref[...] + b_ref[...]

    row = pl.BlockSpec((tm, d), lambda i: (i, 0))
    half = pl.BlockSpec((tm, D_MIX // 2), lambda i: (i, 0))
    vec = pl.BlockSpec((1, d), lambda i: (0, 0))
    return pl.pallas_call(
        body, name=name, grid=(s // tm,),
        in_specs=[row, half, half, _whole(w_out), vec, vec],
        out_specs=[row, row], out_shape=[jax.ShapeDtypeStruct((s, d), F32)] * 2,
        compiler_params=_params())(x, yab, yc, w_out, ln_g, ln_b)


def _wout_bwd(dy, z, yab, yc, w_out, ln_g, alpha, name):
    s, d = dy.shape
    rs = w_out.shape[1]
    tm = min(s, 512)
    n_i = s // tm

    def body(dy_ref, z_ref, yab_ref, yc_ref, w_ref, g_ref, dmix_ref, dxp_ref, dg_ref, db_ref, out_ref, acc_s):
        i = pl.program_id(0)
        dz, dg, db = _ln_bwd(dy_ref[...], z_ref[...], g_ref[...])
        _accumulate(dg_ref, dg, i == 0)
        _accumulate(db_ref, db, i == 0)
        dxp_ref[...] = alpha * dz
        dzb = dz.astype(BF16)
        dmix_ref[...] = _dot_nt(dzb, w_ref[...].reshape(D_MIX, d))
        mix = jnp.concatenate([yab_ref[...], yc_ref[...]], axis=1).astype(BF16)
        _add_matmul(acc_s, i == 0, lambda: _dot_tn(mix, dzb))

        @pl.when(i == n_i - 1)
        def _():
            out_ref[...] = acc_s[...].astype(BF16).reshape(N_DEV, rs, d)

    row = pl.BlockSpec((tm, d), lambda i: (i, 0))
    half = pl.BlockSpec((tm, D_MIX // 2), lambda i: (i, 0))
    vec = pl.BlockSpec((1, d), lambda i: (0, 0))
    return pl.pallas_call(
        body, name=name, grid=(n_i,),
        in_specs=[row, row, half, half, _whole(w_out), vec],
        out_specs=[pl.BlockSpec((tm, D_MIX), lambda i: (i, 0)), row, vec, vec, _whole(w_out)],
        out_shape=[jax.ShapeDtypeStruct((s, D_MIX), F32), jax.ShapeDtypeStruct((s, d), F32),
                   jax.ShapeDtypeStruct((1, d), F32), jax.ShapeDtypeStruct((1, d), F32),
                   jax.ShapeDtypeStruct(w_out.shape, BF16)],
        scratch_shapes=[pltpu.VMEM((D_MIX, d), F32)],
        compiler_params=_params())(dy, z, yab, yc, w_out, ln_g)


def _win_bwd(dxp, dloc, dq, dk, dv, x, w_in, name):
    s, d = x.shape
    rs = w_in.shape[1]
    d_in = N_DEV * rs
    tm = min(s, 256)
    n_i = s // tm

    def body(dxp_ref, dloc_ref, dq_ref, dk_ref, dv_ref, x_ref, w_ref, dx_ref, out_ref, acc_s):
        i = pl.program_id(0)
        dp = jnp.concatenate([dloc_ref[...], dq_ref[...], dk_ref[...].astype(BF16), dv_ref[...].astype(BF16)], axis=1)
        dx_ref[...] = dxp_ref[...] + _dot(dp, w_ref[...].reshape(d_in, d))
        _add_matmul(acc_s, i == 0, lambda: _dot_tn(dp, x_ref[...].astype(BF16)))

        @pl.when(i == n_i - 1)
        def _():
            out_ref[...] = acc_s[...].astype(BF16).reshape(N_DEV, rs, d)

    row = pl.BlockSpec((tm, d), lambda i: (i, 0))
    na = pl.BlockSpec((tm, D_NA), lambda i: (i, 0))
    return pl.pallas_call(
        body, name=name, grid=(n_i,),
        in_specs=[row, pl.BlockSpec((tm, D_LOC), lambda i: (i, 0)), na, na, na, row, _whole(w_in)],
        out_specs=[row, _whole(w_in)],
        out_shape=[jax.ShapeDtypeStruct((s, d), F32), jax.ShapeDtypeStruct(w_in.shape, BF16)],
        scratch_shapes=[pltpu.VMEM((d_in, d), F32)],
        compiler_params=_params())(dxp, dloc, dq, dk, dv, x, w_in)


def _shift_rows(v, k):
    n = v.shape[0]
    return pltpu.roll(v, k % n, 0)


def _halo_specs(tm, s, width, col):
    per = tm // HALO
    last = s // HALO - 1
    return [pl.BlockSpec((HALO, width), lambda i: (jnp.maximum(i * per - 1, 0), col)),
            pl.BlockSpec((tm, width), lambda i: (i, col)),
            pl.BlockSpec((HALO, width), lambda i: (jnp.minimum((i + 1) * per, last), col))]


def _token_index(i, tm):
    return i * tm - HALO + lax.broadcasted_iota(jnp.int32, (tm + 2 * HALO, 1), 0)


def _pool_lane_tables():
    lane = lax.broadcasted_iota(jnp.int32, (1, D_POOL), 1)
    group = sum((lane >= g * POOL_GROUP).astype(jnp.int32) for g in range(1, len(POOL_WINDOWS)))
    half = jnp.where(group == 0, 1, jnp.where(group == 1, 2, jnp.where(group == 2, 4, 8)))
    return group, half


def _window_sums(v, group, offsets):
    s2 = v + _shift_rows(v, 1)
    s4 = s2 + _shift_rows(s2, 2)
    s8 = s4 + _shift_rows(s4, 4)
    s16 = s8 + _shift_rows(s8, 8)
    parts = [_shift_rows(p, -o) if o else p for p, o in zip((s2, s4, s8, s16), offsets)]
    return jnp.where(group == 0, parts[0], jnp.where(group == 1, parts[1], jnp.where(group == 2, parts[2], parts[3])))


def _pool_counts(tok, half, s):
    return (jnp.minimum(tok + half, s) - jnp.maximum(tok - half, 0)).astype(F32)


def _pool_forward(u, tok, s):
    group, half = _pool_lane_tables()
    sums = _window_sums(u, group, [w // 2 - 1 for w in POOL_WINDOWS])
    return sums / _pool_counts(tok, half, s) - u


def _conv_forward(zc, cw_ref):
    return cw_ref[0:1, :] * _shift_rows(zc, 1) + cw_ref[1:2, :] * zc + cw_ref[2:3, :] * _shift_rows(zc, -1)


def _local_fwd(proj, pool_bd, pool_scale, conv_w, name):
    s = proj.shape[0]
    tm = min(s, 512)
    ctr = slice(HALO, HALO + tm)

    def body(prev_ref, cur_ref, next_ref, pw_ref, sc_ref, cw_ref, out_ref):
        i = pl.program_id(0)
        ext = jnp.concatenate([prev_ref[...], cur_ref[...], next_ref[...]], axis=0)
        tok = _token_index(i, tm)
        inside = (tok >= 0) & (tok < s)
        u = jnp.where(inside, ext[:, 0:D_POOL], 0.0)
        p = _pool_forward(u, tok, s)[ctr]
        ya = _dot(p.astype(BF16), pw_ref[...]) * sc_ref[...]
        gb = ext[:, D_POOL:D_POOL + D_CONV]
        zc = jnp.where(inside, ext[:, D_POOL + D_CONV:D_POOL + 2 * D_CONV] * ext[:, D_POOL + 2 * D_CONV:D_LOC], 0.0)
        yb = (gb * _conv_forward(zc, cw_ref))[ctr]
        out_ref[...] = jnp.concatenate([ya, yb], axis=1)

    return pl.pallas_call(
        body, name=name, grid=(s // tm,),
        in_specs=_halo_specs(tm, s, D_LOC, 0) + [
            pl.BlockSpec((D_POOL, D_POOL), lambda i: (0, 0)), pl.BlockSpec((1, D_POOL), lambda i: (0, 0)),
            pl.BlockSpec((3, D_CONV), lambda i: (0, 0))],
        out_specs=pl.BlockSpec((tm, D_POOL + D_CONV), lambda i: (i, 0)),
        out_shape=jax.ShapeDtypeStruct((s, D_POOL + D_CONV), F32),
        compiler_params=_params())(proj, proj, proj, pool_bd, pool_scale, conv_w)


def _local_bwd(proj, dmix, pool_bd, pool_scale, conv_w, name):
    s = proj.shape[0]
    tm = min(s, 512)
    ctr = slice(HALO, HALO + tm)

    def body(prev_ref, cur_ref, next_ref, dprev_ref, dcur_ref, dnext_ref, pw_ref, sc_ref, cw_ref,
             dloc_ref, dpw_ref, dsc_ref, dcw_ref):
        i = pl.program_id(0)
        first = i == 0
        ext = jnp.concatenate([prev_ref[...], cur_ref[...], next_ref[...]], axis=0)
        dext = jnp.concatenate([dprev_ref[...], dcur_ref[...], dnext_ref[...]], axis=0)
        tok = _token_index(i, tm)
        inside = (tok >= 0) & (tok < s)
        group, half = _pool_lane_tables()
        cnt = _pool_counts(tok, half, s)
        u = jnp.where(inside, ext[:, 0:D_POOL], 0.0)
        dya = jnp.where(inside, dext[:, 0:D_POOL], 0.0)
        p_c = _pool_forward(u, tok, s)[ctr].astype(BF16)
        lin = _dot(p_c, pw_ref[...])
        _accumulate(dsc_ref, jnp.sum(dya[ctr] * lin, axis=0, keepdims=True), first)
        e1 = (dya * sc_ref[...]).astype(BF16)
        _accumulate(dpw_ref, _dot_tn(p_c, e1[ctr]), first)
        dp = _dot_nt(e1, pw_ref[...])
        du = _window_sums(dp / cnt, group, [w // 2 for w in POOL_WINDOWS]) - dp
        gb = ext[:, D_POOL:D_POOL + D_CONV]
        gc = ext[:, D_POOL + D_CONV:D_POOL + 2 * D_CONV]
        hv = ext[:, D_POOL + 2 * D_CONV:D_LOC]
        zc = jnp.where(inside, gc * hv, 0.0)
        dyb = jnp.where(inside, dext[:, D_POOL:D_POOL + D_CONV], 0.0)
        dgb = dyb * _conv_forward(zc, cw_ref)
        dyc = dyb * gb
        for k in range(3):
            part = jnp.sum(dyc[ctr] * _shift_rows(zc, 1 - k)[ctr], axis=0, keepdims=True)
            _accumulate(dcw_ref.at[k:k + 1, :], part, first)
        dzc = cw_ref[0:1, :] * _shift_rows(dyc, -1) + cw_ref[1:2, :] * dyc + cw_ref[2:3, :] * _shift_rows(dyc, 1)
        dloc = jnp.concatenate([du, dgb, dzc * hv, dzc * gc], axis=1)
        dloc_ref[...] = dloc[ctr].astype(BF16)

    return pl.pallas_call(
        body, name=name, grid=(s // tm,),
        in_specs=_halo_specs(tm, s, D_LOC, 0) + _halo_specs(tm, s, D_POOL + D_CONV, 0) + [
            pl.BlockSpec((D_POOL, D_POOL), lambda i: (0, 0)), pl.BlockSpec((1, D_POOL), lambda i: (0, 0)),
            pl.BlockSpec((3, D_CONV), lambda i: (0, 0))],
        out_specs=[pl.BlockSpec((tm, D_LOC), lambda i: (i, 0)), pl.BlockSpec((D_POOL, D_POOL), lambda i: (0, 0)),
                   pl.BlockSpec((1, D_POOL), lambda i: (0, 0)), pl.BlockSpec((8, D_CONV), lambda i: (0, 0))],
        out_shape=[jax.ShapeDtypeStruct((s, D_LOC), BF16), jax.ShapeDtypeStruct((D_POOL, D_POOL), F32),
                   jax.ShapeDtypeStruct((1, D_POOL), F32), jax.ShapeDtypeStruct((8, D_CONV), F32)],
        compiler_params=_params())(proj, proj, proj, dmix, dmix, dmix, pool_bd, pool_scale, conv_w)


def _na_geometry(rows):
    n_j = rows // Q_ROWS
    dr = np.full((3, Q_ROWS, K_ROWS), 2 * NA_ROWS - 1, np.int64)
    for t, j in enumerate((0, min(1, n_j - 1), n_j - 1)):
        base = int(np.clip(Q_ROWS * j - NA_ROWS // 2, 0, rows - K_ROWS))
        for qr in range(Q_ROWS):
            r = Q_ROWS * j + qr
            start = int(np.clip(r - NA_ROWS // 2, 0, rows - NA_ROWS))
            for kr in range(K_ROWS):
                if start <= base + kr < start + NA_ROWS:
                    dr[t, qr, kr] = base + kr - r + NA_ROWS - 1
    return dr


def _na_col_tables():
    c = np.arange(GRID_W)
    start = np.clip(c - NA_COLS // 2, 0, GRID_W - NA_COLS)
    valid = (c[None, :] >= start[:, None]) & (c[None, :] < start[:, None] + NA_COLS)
    dc = np.clip(c[None, :] - c[:, None], -(NA_COLS - 1), NA_COLS - 1) + (NA_COLS - 1)
    return valid, dc


NO_ROW = 2 * NA_ROWS - 1
N_SLOT = 2 * NA_ROWS


def _na_tiles(rpb):
    valid, dc = _na_col_tables()
    onehot = jnp.asarray((dc[None] == np.arange(2 * NA_COLS - 1)[:, None, None]).astype(np.float32))
    table = jnp.einsum("hrd,dqk->hrqk", rpb, onehot, precision=lax.Precision.HIGHEST)
    table = jnp.where(jnp.asarray(valid)[None, None], table, NEG_INF)
    outside = jnp.full((NA_HEADS, 1, GRID_W, GRID_W), NEG_INF, F32)
    padded = jnp.concatenate([outside, table, outside], axis=1)
    pairs = jnp.concatenate([padded[:, :N_SLOT], padded[:, 1:]], axis=-1)
    return jnp.concatenate([pairs, jnp.full((NA_HEADS, 1, GRID_W, 2 * GRID_W), NEG_INF, F32)], axis=1)


G_ROWS = 2
N_GRP = Q_ROWS // G_ROWS
G_TOK = G_ROWS * GRID_W
GK_ROWS = NA_ROWS + G_ROWS
GK_TOK = GK_ROWS * GRID_W
STACK_TOK = N_GRP * 2 * G_TOK


def _na_group_tables(rows):
    dr = _na_geometry(rows)
    koff = np.zeros((3, N_GRP), np.int64)
    slot = np.zeros((3, N_GRP, G_ROWS, GK_ROWS // 2), np.int64)
    even_in, odd_in = np.zeros_like(slot), np.zeros_like(slot)
    for t in range(3):
        for g in range(N_GRP):
            qrs = range(G_ROWS * g, G_ROWS * (g + 1))
            inside = [kr for kr in range(K_ROWS) if any(dr[t, qr, kr] != NO_ROW for qr in qrs)]
            lo, hi = min(inside), max(inside) + 1
            off = min(lo - lo % 2, K_ROWS - GK_ROWS)
            assert off <= lo and hi <= off + GK_ROWS
            koff[t, g] = off
            for qq, qr in enumerate(qrs):
                for kp in range(GK_ROWS // 2):
                    even, odd = int(dr[t, qr, off + 2 * kp]), int(dr[t, qr, off + 2 * kp + 1])
                    even_in[t, g, qq, kp], odd_in[t, g, qq, kp] = even != NO_ROW, odd != NO_ROW
                    slot[t, g, qq, kp] = (N_SLOT if even == NO_ROW and odd == NO_ROW
                                          else (even if even != NO_ROW else odd - 1) + 1)
    return koff, slot, even_in, odd_in


def _by_type(block_type, per_type):
    a, b, c = (int(v) for v in per_type)
    if a == b == c:
        return a
    return jnp.where(block_type == 0, a, jnp.where(block_type == 2, c, b))


def _score_rows(g, hh):
    first = (2 * g + hh) * G_TOK
    return slice(first, first + G_TOK)


def _tile_at(g, hh, qq, kp):
    first = _score_rows(g, hh).start + qq * GRID_W
    return slice(first, first + GRID_W), slice(kp * 2 * GRID_W, (kp + 1) * 2 * GRID_W)


def _fill_bias(bias_s, tiles_ref, block_type, tables):
    _, slot, even_in, odd_in = tables
    left = lax.broadcasted_iota(jnp.int32, (1, 2 * GRID_W), 1) < GRID_W
    for hh in range(2):
        for g in range(N_GRP):
            for qq in range(G_ROWS):
                for kp in range(GK_ROWS // 2):
                    tile = tiles_ref[hh, _by_type(block_type, slot[:, g, qq, kp])]
                    tile = jnp.where(left & (_by_type(block_type, even_in[:, g, qq, kp]) == 0), NEG_INF, tile)
                    tile = jnp.where(jnp.logical_not(left) & (_by_type(block_type, odd_in[:, g, qq, kp]) == 0), NEG_INF, tile)
                    rs, cs = _tile_at(g, hh, qq, kp)
                    bias_s[rs, cs] = tile


def _group_offset(block_type, koff, g):
    off = _by_type(block_type, koff[:, g]) * GRID_W
    return off if isinstance(off, int) else pl.multiple_of(off, 2 * GRID_W)


def _na_specs(s, proj_cols):
    n_blk = s // K_BLK
    per = Q_TOK // K_BLK

    def kv_spec(col0, m):
        return pl.BlockSpec((K_BLK, HEAD_PAIR), lambda hp, j: (jnp.clip(per * j - 1, 0, n_blk - 4) + m, col0 + hp))

    q_col, k_col, v_col = (c // HEAD_PAIR for c in proj_cols)
    return ([pl.BlockSpec((Q_TOK, HEAD_PAIR), lambda hp, j: (j, q_col + hp))]
            + [kv_spec(k_col, m) for m in range(4)] + [kv_spec(v_col, m) for m in range(4)])


def _na_block_type(j, n_j):
    return jnp.where(j == 0, 0, jnp.where(j == n_j - 1, 2, 1))


def _head_masks():
    lane = lax.broadcasted_iota(jnp.int32, (1, HEAD_PAIR), 1)
    return [lane < NA_HEAD_DIM, lane >= NA_HEAD_DIM]


def _attn_fwd(qkv, tiles, name):
    s = qkv.shape[0]
    n_j = s // Q_TOK
    tables = _na_group_tables(s // GRID_W)
    koff = tables[0]

    def body(q_ref, k0, k1, k2, k3, v0, v1, v2, v3, tiles_ref, o_ref, lse_ref, bias_s, k_s, vh_s, sc_s, p_s):
        j = pl.program_id(1)
        block_type = _na_block_type(j, n_j)
        pl.when((j == 0) | (j == 1) | (j == n_j - 1))(functools.partial(_fill_bias, bias_s, tiles_ref, block_type, tables))
        masks = _head_masks()
        for m, (kr, vr) in enumerate(zip((k0, k1, k2, k3), (v0, v1, v2, v3))):
            rows = slice(m * K_BLK, (m + 1) * K_BLK)
            k_s[rows, :] = kr[...]
            v = vr[...]
            for hh, mask in enumerate(masks):
                vh_s[hh, rows, :] = jnp.where(mask, v, jnp.zeros_like(v))
        q = q_ref[...]
        qh = [jnp.where(mask, q, jnp.zeros_like(q)) for mask in masks]
        offs = [_group_offset(block_type, koff, g) for g in range(N_GRP)]
        for g in range(N_GRP):
            kg = k_s[pl.ds(offs[g], GK_TOK), :]
            for hh in range(2):
                sc_s[_score_rows(g, hh), :] = _dot_nt(qh[hh][g * G_TOK:(g + 1) * G_TOK], kg)
        sc = sc_s[...] + bias_s[...]
        mx = jnp.max(sc, axis=-1, keepdims=True)
        p = jnp.exp(sc - mx)
        den = jnp.sum(p, axis=-1, keepdims=True)
        p_s[...] = p.astype(BF16)
        inv = 1.0 / den
        lse = mx + jnp.log(den)
        for g in range(N_GRP):
            rows = slice(g * G_TOK, (g + 1) * G_TOK)
            out = jnp.zeros((G_TOK, HEAD_PAIR), F32)
            for hh in range(2):
                sr = _score_rows(g, hh)
                out = out + _dot(p_s[sr, :], vh_s[hh, pl.ds(offs[g], GK_TOK), :]) * inv[sr]
            o_ref[rows, :] = out
            lse_ref[0, rows, :] = jnp.where(masks[0], lse[_score_rows(g, 0)], lse[_score_rows(g, 1)])

    return pl.pallas_call(
        body, name=name, grid=(NA_HEADS // 2, n_j),
        in_specs=_na_specs(s, (0, D_NA, 2 * D_NA)) + [
            pl.BlockSpec((2, N_SLOT + 1, GRID_W, 2 * GRID_W), lambda hp, j: (hp, 0, 0, 0))],
        out_specs=[pl.BlockSpec((Q_TOK, HEAD_PAIR), lambda hp, j: (j, hp)),
                   pl.BlockSpec((1, Q_TOK, HEAD_PAIR), lambda hp, j: (hp, j, 0))],
        out_shape=[jax.ShapeDtypeStruct((s, D_NA), F32), jax.ShapeDtypeStruct((NA_HEADS // 2, s, HEAD_PAIR), F32)],
        scratch_shapes=[pltpu.VMEM((STACK_TOK, GK_TOK), F32), pltpu.VMEM((K_TOK, HEAD_PAIR), BF16),
                        pltpu.VMEM((2, K_TOK, HEAD_PAIR), BF16), pltpu.VMEM((STACK_TOK, GK_TOK), F32),
                        pltpu.VMEM((STACK_TOK, GK_TOK), BF16)],
        compiler_params=_params())(*([qkv] * 9), tiles)


def _add_tiles(dtile_ref, hh, g, ds_ref, block_type, slot):
    for qq in range(G_ROWS):
        for kp in range(GK_ROWS // 2):
            per_type = slot[:, g, qq, kp]
            if (per_type == N_SLOT).all():
                continue
            rs, cs = _tile_at(g, hh, qq, kp)
            dtile_ref[hh, _by_type(block_type, np.where(per_type == N_SLOT, 0, per_type))] += ds_ref[rs, cs]


def _attn_bwd(qkv, tiles, o, dmix, lse, name, job=None):
    s = qkv.shape[0]
    n_j = s // Q_TOK
    n_blk = s // K_BLK
    per = Q_TOK // K_BLK
    scale = NA_HEAD_DIM ** -0.5
    do_col = (D_POOL + D_CONV) // HEAD_PAIR
    tables = _na_group_tables(s // GRID_W)
    koff, slot = tables[0], tables[1]

    def body(q_ref, k0, k1, k2, k3, v0, v1, v2, v3, tiles_ref, o_ref, do_ref, lse_ref,
             dq_ref, dk_ref, dv_ref, dtile_ref, bias_s, k_s, kh_s, v_s, s_s, dp_s, pb_s, dsb_s):
        j = pl.program_id(1)

        @pl.when(j == 0)
        def _():
            dk_ref[...] = jnp.zeros_like(dk_ref)
            dv_ref[...] = jnp.zeros_like(dv_ref)
            dtile_ref[...] = jnp.zeros_like(dtile_ref)

        block_type = _na_block_type(j, n_j)
        pl.when((j == 0) | (j == 1) | (j == n_j - 1))(functools.partial(_fill_bias, bias_s, tiles_ref, block_type, tables))
        base = pl.multiple_of(jnp.clip(per * j - 1, 0, n_blk - 4) * K_BLK, K_BLK)
        masks = _head_masks()
        for m, (kr, vr) in enumerate(zip((k0, k1, k2, k3), (v0, v1, v2, v3))):
            rows = slice(m * K_BLK, (m + 1) * K_BLK)
            k = kr[...]
            k_s[rows, :] = k
            v_s[rows, :] = vr[...]
            for hh, mask in enumerate(masks):
                kh_s[hh, rows, :] = jnp.where(mask, k, jnp.zeros_like(k))
        q = q_ref[...]
        qh = [jnp.where(mask, q, jnp.zeros_like(q)) for mask in masks]
        lane = lax.broadcasted_iota(jnp.int32, (1, HEAD_PAIR), 1)
        offs = [_group_offset(block_type, koff, g) for g in range(N_GRP)]
        do, ov, lse = do_ref[...], o_ref[...], lse_ref[0]
        dob, lse_col, delta_col = {}, [], []
        for g in range(N_GRP):
            rows = slice(g * G_TOK, (g + 1) * G_TOK)
            kg = k_s[pl.ds(offs[g], GK_TOK), :]
            vg = v_s[pl.ds(offs[g], GK_TOK), :]
            for hh, mask in enumerate(masks):
                doh = jnp.where(mask, do[rows], 0.0)
                dob[g, hh] = doh.astype(BF16)
                lse_col.append(jnp.sum(jnp.where(lane == hh * NA_HEAD_DIM, lse[rows], 0.0), axis=-1, keepdims=True))
                delta_col.append(jnp.sum(doh * ov[rows], axis=-1, keepdims=True))
                s_s[_score_rows(g, hh), :] = _dot_nt(qh[hh][rows], kg)
                dp_s[_score_rows(g, hh), :] = _dot_nt(dob[g, hh], vg)
        p = jnp.exp(s_s[...] + bias_s[...] - jnp.concatenate(lse_col, axis=0))
        ds = p * (dp_s[...] - jnp.concatenate(delta_col, axis=0))
        pb_s[...] = p.astype(BF16)
        dsb_s[...] = ds.astype(BF16)
        s_s[...] = ds
        for g in range(N_GRP):
            rows = slice(g * G_TOK, (g + 1) * G_TOK)
            dq = jnp.zeros((G_TOK, HEAD_PAIR), F32)
            dk = jnp.zeros((GK_TOK, HEAD_PAIR), F32)
            dv = jnp.zeros((GK_TOK, HEAD_PAIR), F32)
            for hh in range(2):
                sr = _score_rows(g, hh)
                _add_tiles(dtile_ref, hh, g, s_s, block_type, slot)
                dsb = dsb_s[sr, :]
                dq = dq + _dot(dsb, kh_s[hh, pl.ds(offs[g], GK_TOK), :])
                dk = dk + _dot_tn(dsb, qh[hh][rows])
                dv = dv + _dot_tn(pb_s[sr, :], dob[g, hh])
            dq_ref[rows, :] = (dq * scale).astype(BF16)
            at = pl.multiple_of(base + offs[g], 2 * GRID_W)
            dk_ref[pl.ds(at, GK_TOK), :] += dk
            dv_ref[pl.ds(at, GK_TOK), :] += dv

    pair = pl.BlockSpec((Q_TOK, HEAD_PAIR), lambda hp, j: (j, hp))
    whole = pl.BlockSpec((s, HEAD_PAIR), lambda hp, j: (0, hp))
    call = _riding_call(
        body, job, 13, 4, (NA_HEADS // 2) * n_j, lambda: pl.program_id(0) * n_j + pl.program_id(1),
        name=name, grid=(NA_HEADS // 2, n_j),
        in_specs=_na_specs(s, (0, D_NA, 2 * D_NA)) + [
            pl.BlockSpec((2, N_SLOT + 1, GRID_W, 2 * GRID_W), lambda hp, j: (hp, 0, 0, 0)),
            pair, pl.BlockSpec((Q_TOK, HEAD_PAIR), lambda hp, j: (j, do_col + hp)),
            pl.BlockSpec((1, Q_TOK, HEAD_PAIR), lambda hp, j: (hp, j, 0))],
        out_specs=[pair, whole, whole, pl.BlockSpec((2, N_SLOT, GRID_W, 2 * GRID_W), lambda hp, j: (hp, 0, 0, 0))],
        out_shape=[jax.ShapeDtypeStruct((s, D_NA), BF16), jax.ShapeDtypeStruct((s, D_NA), F32),
                   jax.ShapeDtypeStruct((s, D_NA), F32),
                   jax.ShapeDtypeStruct((NA_HEADS, N_SLOT, GRID_W, 2 * GRID_W), F32)],
        scratch_shapes=[pltpu.VMEM((STACK_TOK, GK_TOK), F32), pltpu.VMEM((K_TOK, HEAD_PAIR), BF16),
                        pltpu.VMEM((2, K_TOK, HEAD_PAIR), BF16), pltpu.VMEM((K_TOK, HEAD_PAIR), BF16),
                        pltpu.VMEM((STACK_TOK, GK_TOK), F32), pltpu.VMEM((STACK_TOK, GK_TOK), F32),
                        pltpu.VMEM((STACK_TOK, GK_TOK), BF16), pltpu.VMEM((STACK_TOK, GK_TOK), BF16)],
        compiler_params=_params())
    return call(*([qkv] * 9), tiles, o, dmix, lse)


def _rpb_finish(tiles, name):
    valid, dc = _na_col_tables()
    n_dc = 2 * NA_COLS - 1
    sel = np.zeros((GRID_W, 2 * GRID_W, LANES), np.float32)
    for qc in range(GRID_W):
        for kc in range(GRID_W):
            if valid[qc, kc]:
                sel[qc, kc, dc[qc, kc]] = 1.0
                sel[qc, GRID_W + kc, LANES // 2 + dc[qc, kc]] = 1.0
    sel = jnp.asarray(sel.reshape(GRID_W * 2 * GRID_W, LANES))
    flat = tiles.reshape(NA_HEADS * 2 * NA_ROWS, GRID_W * 2 * GRID_W)

    def body(a_ref, b_ref, out_ref):
        out_ref[...] = jnp.dot(a_ref[...], b_ref[...], preferred_element_type=F32, precision=lax.Precision.HIGHEST)

    sums = pl.pallas_call(
        body, name=name, out_shape=jax.ShapeDtypeStruct((flat.shape[0], LANES), F32),
        compiler_params=_params())(flat, sel).reshape(NA_HEADS, 2 * NA_ROWS, LANES)
    return sums[:, 1:, :n_dc] + sums[:, :2 * NA_ROWS - 1, LANES // 2:LANES // 2 + n_dc]


def _loss_grad(y, target, name):
    s, d = y.shape
    tm = min(s, 1024)

    def body(y_ref, t_ref, sum_ref, dy_ref):
        diff = y_ref[...] - t_ref[...]
        dy_ref[...] = diff * (1.0 / d)
        part = jnp.zeros((8, LANES), F32) + jnp.sum(diff * diff)
        _accumulate(sum_ref, part, pl.program_id(0) == 0)

    row = pl.BlockSpec((tm, d), lambda i: (i, 0))
    return pl.pallas_call(
        body, name=name, grid=(s // tm,), in_specs=[row, row],
        out_specs=[pl.BlockSpec((8, LANES), lambda i: (0, 0)), row],
        out_shape=[jax.ShapeDtypeStruct((8, LANES), F32), jax.ShapeDtypeStruct((s, d), F32)],
        compiler_params=_params())(y, target)


def _adamw(w, g, m, v, name):
    rows, cols = w.shape
    tr = _row_tile(rows, 512, 8)

    def body(w_ref, g_ref, m_ref, v_ref, d_ref, nm_ref, nv_ref):
        gv = g_ref[...]
        nm = ADAM_B1 * m_ref[...] + (1.0 - ADAM_B1) * gv
        nv = ADAM_B2 * v_ref[...] + (1.0 - ADAM_B2) * (gv * gv)
        m_hat = nm / (1.0 - ADAM_B1 ** ADAM_STEP)
        v_hat = nv / (1.0 - ADAM_B2 ** ADAM_STEP)
        d_ref[...] = -ADAM_LR * (m_hat / (jnp.sqrt(v_hat) + ADAM_EPS) + ADAM_WD * w_ref[...])
        nm_ref[...] = nm
        nv_ref[...] = nv

    blk = pl.BlockSpec((tr, cols), lambda r: (r, 0))
    return pl.pallas_call(
        body, name=name, grid=(rows // tr,), in_specs=[blk] * 4, out_specs=[blk] * 3,
        out_shape=[jax.ShapeDtypeStruct((rows, cols), F32)] * 3, compiler_params=_params())(w, g, m, v)


def _adamw_nd(w, g, m, v, name):
    shape = w.shape
    flat = lambda t: t.reshape(-1, shape[-1])
    return tuple(t.reshape(shape) for t in _adamw(flat(w), flat(g), flat(m), flat(v), name))


def _pack(parts, rows_mult=64):
    flat = jnp.concatenate([p.reshape(-1).astype(F32) for p in parts])
    per = LANES * rows_mult
    total = -(-flat.shape[0] // per) * per
    return jnp.pad(flat, (0, total - flat.shape[0])).reshape(-1, LANES)


def _unpack(packed, shapes):
    flat = packed.reshape(-1)
    out, pos = [], 0
    for shp in shapes:
        n = int(np.prod(shp))
        out.append(flat[pos:pos + n].reshape(shp))
        pos += n
    return out


def kernel(x, ffn1_w_gate, ffn1_w_up, ffn1_w_down, ffn2_w_gate, ffn2_w_up, ffn2_w_down, w_in, pool_w, pool_scale, conv_w, rpb, w_out, ln_g, ln_b, loss_target, m_ffn1_w_gate, m_ffn1_w_up, m_ffn1_w_down, m_ffn2_w_gate, m_ffn2_w_up, m_ffn2_w_down, m_w_in, m_pool_w, m_pool_scale, m_conv_w, m_rpb, m_w_out, m_ln_g, m_ln_b, v_ffn1_w_gate, v_ffn1_w_up, v_ffn1_w_down, v_ffn2_w_gate, v_ffn2_w_up, v_ffn2_w_down, v_w_in, v_pool_w, v_pool_scale, v_conv_w, v_rpb, v_w_out, v_ln_g, v_ln_b):
    n_l, d, fs = ffn1_w_gate.shape
    s = x.shape[1]
    rows = s // GRID_W
    assert x.shape[0] == 1 and s % Q_TOK == 0 and rows >= K_ROWS and fs % BF16_ROWS == 0
    alpha = (2.0 * n_l) ** 0.25
    xi, yi, ci = _mesh_pos()
    me = 4 * xi + 2 * yi + ci
    core = jnp.reshape(ci, (1,)).astype(jnp.int32)
    ln_w, cw_w = ln_g.shape[2], conv_w.shape[2]

    tr = lambda w: jnp.swapaxes(w, 1, 2)
    ffn1_shard = jnp.stack([tr(ffn1_w_gate), tr(ffn1_w_up), ffn1_w_down], axis=1).astype(BF16)
    ffn2_shard = jnp.stack([tr(ffn2_w_gate), tr(ffn2_w_up), ffn2_w_down], axis=1).astype(BF16)
    win_shard, wout_shard = tr(w_in).astype(BF16), w_out.astype(BF16)
    small_shard = _pack([ln_g, ln_b, conv_w])
    w_ffn1, small = _exchange_alone(_Gather([ffn1_shard[0], small_shard]), "gather_first")
    n_ln = n_l * 3 * ln_w
    small = small.reshape(N_DEV, -1)
    unshard = lambda t, width: jnp.moveaxis(t.reshape(N_DEV, n_l, 3, width), 0, 2).reshape(n_l, 3, N_DEV * width)
    ln_g_all = unshard(small[:, :n_ln], ln_w)
    ln_b_all = unshard(small[:, n_ln:2 * n_ln], ln_w)
    conv_all = unshard(small[:, 2 * n_ln:2 * n_ln + n_l * 3 * cw_w], cw_w)
    pool_bd = jnp.zeros((n_l, D_POOL, D_POOL), F32)
    for g in range(len(POOL_WINDOWS)):
        sl = slice(g * POOL_GROUP, (g + 1) * POOL_GROUP)
        pool_bd = pool_bd.at[:, sl, sl].set(pool_w[:, g])
    pool_bd = pool_bd.astype(BF16)
    lnp = lambda arr, l, j: arr[l, j].reshape(1, d)

    saved = []
    h = x.reshape(s, d)
    for l in range(n_l):
        a1, u1, z1, x1, w_in_l, w_out_l, w_ffn2 = _ffn_fwd(
            h, w_ffn1, lnp(ln_g_all, l, 0), lnp(ln_b_all, l, 0), alpha, f"ffn1_fwd_{l}",
            job=_Gather([win_shard[l], wout_shard[l], ffn2_shard[l]]))
        proj = _win_fwd(x1, w_in_l, f"win_fwd_{l}")
        bias = _na_tiles(rpb[l])
        yab = _local_fwd(proj[0], pool_bd[l], pool_scale[l].reshape(1, D_POOL), conv_all[l], f"local_fwd_{l}")
        yc, lse = _attn_fwd(proj[1], bias, f"attn_fwd_{l}")
        z2, x2 = _wout_fwd(x1, yab, yc, w_out_l, lnp(ln_g_all, l, 1), lnp(ln_b_all, l, 1), alpha, f"wout_fwd_{l}")
        a2, u2, z3, x3, *w_next = _ffn_fwd(
            x2, w_ffn2, lnp(ln_g_all, l, 2), lnp(ln_b_all, l, 2), alpha, f"ffn2_fwd_{l}",
            job=_Gather([ffn1_shard[l + 1]]) if l + 1 < n_l else None)
        saved.append((h, a1, u1, z1, x1, proj, bias, yab, yc, lse, z2, x2, a2, u2, z3, w_ffn1, w_in_l, w_out_l, w_ffn2))
        h = x3
        if w_next:
            w_ffn1 = w_next[0]

    sq, dh = _loss_grad(h, loss_target.reshape(s, d), "loss_head")
    loss = lax.psum(sq[0, 0] * (0.5 / d), MESH_AXES)

    def pair_sum(blocks, tag):
        flat = [b.reshape(N_DEV, -1, d) for b in blocks]
        got = _pair_exchange(flat, f"grads_pair_exchange_{tag}")
        return [_pair_add(b, g, core, f"grads_pair_add_{tag}_{i}") for i, (b, g) in enumerate(zip(flat, got))]

    small_grads = [None] * n_l
    reduced = [None] * n_l
    waiting = None
    for l in reversed(range(n_l)):
        x0, a1, u1, z1, x1, proj, bias, yab, yc, lse, z2, x2, a2, u2, z3, w_ffn1, w_in_l, w_out_l, w_ffn2 = saved[l]
        dx2, da, du, df, dg3, db3, *crossed = _ffn_bwd_dx(
            dh, z3, a2, u2, w_ffn2, lnp(ln_g_all, l, 2), alpha, f"ffn2_bwd_dx_{l}",
            job=_ChipExchange(waiting) if waiting else None)
        if waiting:
            reduced[l + 1] += crossed
        g2 = _ffn_bwd_dwd(a2, u2, df, _ffn_bwd_dwgu(da, du, x2, fs, f"ffn2_bwd_dwgu_{l}"), f"ffn2_bwd_dwd_{l}")
        p2 = pair_sum([g2], f"ffn2_{l}")
        dmix, dxp, dg2, db2, g_out = _wout_bwd(dx2, z2, yab, yc, w_out_l, lnp(ln_g_all, l, 1), alpha, f"wout_bwd_{l}")
        dq, dk, dv, dtiles, *crossed = _attn_bwd(proj[1], bias, yc, dmix, lse, f"attn_bwd_{l}", job=_ChipExchange(p2))
        reduced[l] = list(crossed)
        dloc, dpw, dsc, dcw = _local_bwd(proj[0], dmix, pool_bd[l], pool_scale[l].reshape(1, D_POOL), conv_all[l],
                                         f"local_bwd_{l}")
        dx1, g_in = _win_bwd(dxp, dloc, dq, dk, dv, x1, w_in_l, f"win_bwd_{l}")
        dx0, da, du, df, dg1, db1 = _ffn_bwd_dx(dx1, z1, a1, u1, w_ffn1, lnp(ln_g_all, l, 0), alpha, f"ffn1_bwd_dx_{l}")
        g1 = _ffn_bwd_dwd(a1, u1, df, _ffn_bwd_dwgu(da, du, x0, fs, f"ffn1_bwd_dwgu_{l}"), f"ffn1_bwd_dwd_{l}")
        waiting = pair_sum([g_out, g_in, g1], f"mix_{l}")
        drpb = _rpb_finish(dtiles, f"rpb_finish_{l}")
        dpool = jnp.stack([dpw[g * POOL_GROUP:(g + 1) * POOL_GROUP, g * POOL_GROUP:(g + 1) * POOL_GROUP]
                           for g in range(len(POOL_WINDOWS))])
        small_grads[l] = (jnp.concatenate([dg1, dg2, dg3]), jnp.concatenate([db1, db2, db3]), dcw[0:3], dpool, dsc[0], drpb)
        dh = dx0
    grad_x = dh.reshape(x.shape)

    reduced[0] += _exchange_alone(_ChipExchange(waiting), "grads_chip_exchange_last")
    sums = [[_sum_blocks(q, f"grads_chip_sum_{l}_{i}") for i, q in enumerate(reduced[l])] for l in range(n_l)]
    r_ffn2, r_out, r_in, r_ffn1 = [jnp.stack([sums[l][i] for l in range(n_l)]) for i in range(4)]
    r_ffn1, r_ffn2 = r_ffn1.reshape(n_l, 3, fs, d), r_ffn2.reshape(n_l, 3, fs, d)
    grads = {
        "ffn1_w_gate": tr(r_ffn1[:, 0]), "ffn1_w_up": tr(r_ffn1[:, 1]), "ffn1_w_down": r_ffn1[:, 2],
        "ffn2_w_gate": tr(r_ffn2[:, 0]), "ffn2_w_up": tr(r_ffn2[:, 1]), "ffn2_w_down": r_ffn2[:, 2],
        "w_in": tr(r_in), "w_out": r_out}

    stack = lambda k: jnp.stack([small_grads[l][k] for l in range(n_l)])
    small_shapes = [(n_l, 3, d), (n_l, 3, d), (n_l, 3, D_CONV), pool_w.shape, pool_scale.shape, rpb.shape]
    (small_all,) = _exchange_alone(_Gather([_pack([stack(k) for k in range(6)])]), "gather_small_grads")
    small_sum = _sum_blocks(small_all, "small_grads_sum")
    g_ln_g, g_ln_b, g_conv, g_pool_w, g_pool_scale, g_rpb = _unpack(small_sum, small_shapes)
    own = lambda t, width: lax.dynamic_slice_in_dim(t, me * width, width, axis=2)
    grads.update({"ln_g": own(g_ln_g, ln_w), "ln_b": own(g_ln_b, ln_w), "conv_w": own(g_conv, cw_w),
                  "pool_w": g_pool_w, "pool_scale": g_pool_scale, "rpb": g_rpb})

    weights = dict(ffn1_w_gate=ffn1_w_gate, ffn1_w_up=ffn1_w_up, ffn1_w_down=ffn1_w_down, ffn2_w_gate=ffn2_w_gate,
                   ffn2_w_up=ffn2_w_up, ffn2_w_down=ffn2_w_down, w_in=w_in, pool_w=pool_w, pool_scale=pool_scale,
                   conv_w=conv_w, rpb=rpb, w_out=w_out, ln_g=ln_g, ln_b=ln_b)
    m_in = dict(ffn1_w_gate=m_ffn1_w_gate, ffn1_w_up=m_ffn1_w_up, ffn1_w_down=m_ffn1_w_down, ffn2_w_gate=m_ffn2_w_gate,
                ffn2_w_up=m_ffn2_w_up, ffn2_w_down=m_ffn2_w_down, w_in=m_w_in, pool_w=m_pool_w, pool_scale=m_pool_scale,
                conv_w=m_conv_w, rpb=m_rpb, w_out=m_w_out, ln_g=m_ln_g, ln_b=m_ln_b)
    v_in = dict(ffn1_w_gate=v_ffn1_w_gate, ffn1_w_up=v_ffn1_w_up, ffn1_w_down=v_ffn1_w_down, ffn2_w_gate=v_ffn2_w_gate,
                ffn2_w_up=v_ffn2_w_up, ffn2_w_down=v_ffn2_w_down, w_in=v_w_in, pool_w=v_pool_w, pool_scale=v_pool_scale,
                conv_w=v_conv_w, rpb=v_rpb, w_out=v_w_out, ln_g=v_ln_g, ln_b=v_ln_b)
    names = list(weights)
    large = ["ffn1_w_gate", "ffn1_w_up", "ffn1_w_down", "ffn2_w_gate", "ffn2_w_up", "ffn2_w_down", "w_in", "w_out"]
    tiny = [n for n in names if n not in large]
    delta, new_m, new_v = {}, {}, {}
    for n in large:
        delta[n], new_m[n], new_v[n] = _adamw_nd(weights[n], grads[n], m_in[n], v_in[n], f"adamw_{n}")
    packed = [_pack([t[n] for n in tiny]) for t in (weights, grads, m_in, v_in)]
    tiny_out = _adamw(*packed, "adamw_small")
    tiny_shapes = [weights[n].shape for n in tiny]
    for res, t in zip((delta, new_m, new_v), tiny_out):
        res.update(dict(zip(tiny, _unpack(t, tiny_shapes))))

    return (loss, grad_x, *[grads[n] for n in names], *[delta[n] for n in names],
            *[new_m[n] for n in names], *[new_v[n] for n in names])
```

```python
import functools

import numpy as np
import jax
import jax.numpy as jnp
from jax import lax
from jax.experimental import pallas as pl
from jax.experimental.pallas import tpu as pltpu

F32, BF16 = jnp.float32, jnp.bfloat16
MESH = pl.DeviceIdType.MESH
N_DEV = 8
MESH_AXES = ("x", "y", "c")

LN_EPS = 1e-5
NEG_INF = -1e30
D_POOL = 256
POOL_WINDOWS = (2, 4, 8, 16)
POOL_GROUP = 64
D_CONV = 256
NA_HEADS = 8
NA_HEAD_DIM = 64
D_NA = NA_HEADS * NA_HEAD_DIM
GRID_W = 64
NA_ROWS = 8
NA_COLS = 16
D_LOC = D_POOL + 3 * D_CONV
D_MIX = D_POOL + D_CONV + D_NA
ADAM_LR, ADAM_B1, ADAM_B2, ADAM_EPS, ADAM_WD, ADAM_STEP = 0.001, 0.9, 0.999, 1e-08, 0.01, 10

VMEM_LIMIT_BYTES = 56 * 1024 * 1024
LANES = 128
BF16_ROWS = 16
HALO = 16
Q_ROWS = 8
K_ROWS = 16
Q_TOK = Q_ROWS * GRID_W
K_TOK = K_ROWS * GRID_W
K_BLK = 4 * GRID_W
HEAD_PAIR = 2 * NA_HEAD_DIM
FFN_CHUNK_DEVS = 4

NT = (((1,), (1,)), ((), ()))
TN = (((0,), (0,)), ((), ()))


def _dot(a, b):
    return jnp.dot(a, b, preferred_element_type=F32)


def _dot_nt(a, b):
    return lax.dot_general(a, b, NT, preferred_element_type=F32)


def _dot_tn(a, b):
    return lax.dot_general(a, b, TN, preferred_element_type=F32)


def _params():
    return pltpu.CompilerParams(vmem_limit_bytes=VMEM_LIMIT_BYTES)


def _row_tile(rows, pref, mult=BF16_ROWS):
    t = min(rows, pref)
    t -= t % mult
    while t > mult and rows % t:
        t -= mult
    assert t > 0 and rows % t == 0, (rows, pref)
    return t


def _mesh_pos():
    return tuple(lax.axis_index(a) for a in MESH_AXES)


def _any_spec():
    return pl.BlockSpec(memory_space=pl.ANY)


class _Gather:
    def __init__(self, shards):
        self.arrays = list(shards)
        n = len(shards)
        self.out_shape = [jax.ShapeDtypeStruct((N_DEV,) + s.shape, s.dtype) for s in shards]
        self.scratch = [pltpu.SemaphoreType.DMA((n, 7)), pltpu.SemaphoreType.DMA((n, 7)), pltpu.SemaphoreType.DMA((n,))]

    def phases(self, ins, outs, sems):
        n = len(ins)
        send_sems, recv_sems, local_sems = sems
        x, y, c = _mesh_pos()
        me, sibling = (x, y, c), (x, y, 1 - c)
        chips = [(1 - x, y), (x, 1 - y), (1 - x, 1 - y)]

        def copy(a, k, block, to, src=None):
            dst = outs[a].at[4 * block[0] + 2 * block[1] + block[2]]
            return pltpu.make_async_remote_copy(
                src_ref=dst if src is None else src, dst_ref=dst,
                send_sem=send_sems.at[a, k], recv_sem=recv_sems.at[a, k],
                device_id=to, device_id_type=MESH)

        def mine():
            return [pltpu.make_async_copy(ins[a], outs[a].at[4 * x + 2 * y + c], local_sems.at[a]) for a in range(n)]

        def first():
            return [cp for a in range(n) for cp in
                    [copy(a, 0, me, sibling, src=ins[a])]
                    + [copy(a, 1 + j, me, (*chip, c), src=ins[a]) for j, chip in enumerate(chips)]]

        def passed():
            return [copy(a, 4 + j, (*chip, c), sibling) for j, chip in enumerate(chips) for a in range(n)]

        def start():
            for cp in mine() + first():
                cp.start()

        def middle():
            for j, chip in enumerate(chips):
                for a in range(n):
                    copy(a, 1 + j, (*chip, c), me).wait_recv()
            for cp in passed():
                cp.start()

        def finish():
            for a in range(n):
                copy(a, 0, sibling, me).wait_recv()
                for j, chip in enumerate(chips):
                    copy(a, 4 + j, (*chip, 1 - c), me).wait_recv()
            for cp in first() + passed():
                cp.wait_send()
            for cp in mine():
                cp.wait()

        return start, middle, finish


class _ChipExchange:
    def __init__(self, parts):
        self.arrays = list(parts)
        n = len(parts)
        self.out_shape = [jax.ShapeDtypeStruct(s.shape, s.dtype) for s in parts]
        self.scratch = [pltpu.SemaphoreType.DMA((n, 3)), pltpu.SemaphoreType.DMA((n, 3)), pltpu.SemaphoreType.DMA((n,))]

    def phases(self, ins, outs, sems):
        n = len(ins)
        send_sems, recv_sems, local_sems = sems
        x, y, c = _mesh_pos()
        my_chip = 2 * x + y
        chips = [(1 - x, y), (x, 1 - y), (1 - x, 1 - y)]

        def own():
            return [pltpu.make_async_copy(ins[a].at[my_chip], outs[a].at[my_chip], local_sems.at[a]) for a in range(n)]

        def copy(a, k, src_chip, dst_chip, to):
            return pltpu.make_async_remote_copy(
                src_ref=ins[a].at[src_chip], dst_ref=outs[a].at[dst_chip],
                send_sem=send_sems.at[a, k], recv_sem=recv_sems.at[a, k],
                device_id=to, device_id_type=MESH)

        def sends():
            return [copy(a, k, 2 * px + py, my_chip, (px, py, c)) for a in range(n) for k, (px, py) in enumerate(chips)]

        def start():
            for cp in own() + sends():
                cp.start()

        def finish():
            for cp in sends():
                cp.wait_send()
            for a in range(n):
                for k, (px, py) in enumerate(chips):
                    copy(a, k, my_chip, 2 * px + py, (px, py, c)).wait_recv()
            for cp in own():
                cp.wait()

        return start, None, finish


def _exchange_alone(job, name):
    n = len(job.arrays)

    def body(*refs):
        for phase in job.phases(refs[:n], refs[n:2 * n], refs[2 * n:]):
            if phase is not None:
                phase()

    return pl.pallas_call(
        body, name=name, out_shape=job.out_shape,
        in_specs=[_any_spec()] * n, out_specs=[_any_spec()] * n, scratch_shapes=job.scratch,
    )(*job.arrays)


def _riding_call(body, job, n_in, n_out, n_steps, step, **kw):
    if job is None:
        return pl.pallas_call(body, **kw)
    n_job, n_sem = len(job.arrays), len(job.scratch)
    assert n_steps >= 3
    kw = dict(kw, in_specs=list(kw["in_specs"]) + [_any_spec()] * n_job,
              out_specs=list(kw["out_specs"]) + [_any_spec()] * n_job,
              out_shape=list(kw["out_shape"]) + job.out_shape,
              scratch_shapes=list(kw.get("scratch_shapes", ())) + job.scratch)

    def riding(*refs):
        ins, job_ins = refs[:n_in], refs[n_in:n_in + n_job]
        outs = refs[n_in + n_job:n_in + n_job + n_out]
        job_outs = refs[n_in + n_job + n_out:n_in + 2 * n_job + n_out]
        scratch = refs[n_in + 2 * n_job + n_out:]
        start, middle, finish = job.phases(job_ins, job_outs, scratch[len(scratch) - n_sem:])
        now = step()
        pl.when(now == 0)(start)
        if middle is not None:
            pl.when(now == (5 * n_steps) // 8)(middle)
        body(*ins, *outs, *scratch[:len(scratch) - n_sem])
        pl.when(now == n_steps - 1)(finish)

    call = pl.pallas_call(riding, **kw)
    return lambda *args: call(*args, *job.arrays)


def _pair_exchange(slabs, name):
    n = len(slabs)

    def body(*refs):
        ins, outs = refs[:n], refs[n:2 * n]
        send_sems, recv_sems = refs[2 * n:]
        x, y, c = _mesh_pos()
        copies = [
            pltpu.make_async_remote_copy(
                src_ref=ins[a].at[2 * j + 1 - c], dst_ref=outs[a].at[j],
                send_sem=send_sems.at[a, j], recv_sem=recv_sems.at[a, j],
                device_id=(x, y, 1 - c), device_id_type=MESH)
            for a in range(n) for j in range(4)]
        for cp in copies:
            cp.start()
        for cp in copies:
            cp.wait_send()
        for cp in copies:
            cp.wait_recv()

    return pl.pallas_call(
        body, name=name,
        out_shape=[jax.ShapeDtypeStruct((4,) + s.shape[1:], s.dtype) for s in slabs],
        in_specs=[_any_spec()] * n, out_specs=[_any_spec()] * n,
        scratch_shapes=[pltpu.SemaphoreType.DMA((n, 4)), pltpu.SemaphoreType.DMA((n, 4))],
    )(*slabs)


def _pair_add(slab, got, core, name):
    _, rows, d = slab.shape
    tr = _row_tile(rows, 1024)

    def body(core_ref, mine_ref, got_ref, out_ref):
        out_ref[...] = (mine_ref[...].astype(F32) + got_ref[...].astype(F32)).astype(out_ref.dtype)

    grid_spec = pltpu.PrefetchScalarGridSpec(
        num_scalar_prefetch=1, grid=(4, rows // tr),
        in_specs=[pl.BlockSpec((1, tr, d), lambda j, r, core_ref: (2 * j + core_ref[0], r, 0)),
                  pl.BlockSpec((1, tr, d), lambda j, r, core_ref: (j, r, 0))],
        out_specs=pl.BlockSpec((1, tr, d), lambda j, r, core_ref: (j, r, 0)))
    return pl.pallas_call(body, name=name, grid_spec=grid_spec,
                          out_shape=jax.ShapeDtypeStruct((4, rows, d), slab.dtype),
                          compiler_params=_params())(core, slab, got)


def _sum_blocks(parts, name):
    k, rows, d = parts.shape
    tr = _row_tile(rows, 512, BF16_ROWS if parts.dtype == BF16 else 8)

    def body(in_ref, out_ref):
        acc = in_ref[0].astype(F32)
        for j in range(1, k):
            acc = acc + in_ref[j].astype(F32)
        out_ref[...] = acc

    return pl.pallas_call(
        body, name=name, grid=(rows // tr,),
        in_specs=[pl.BlockSpec((k, tr, d), lambda r: (0, r, 0))],
        out_specs=pl.BlockSpec((tr, d), lambda r: (r, 0)),
        out_shape=jax.ShapeDtypeStruct((rows, d), F32), compiler_params=_params())(parts)


def _ln_stats(z):
    mu = jnp.mean(z, axis=-1, keepdims=True)
    zc = z - mu
    var = jnp.mean(zc * zc, axis=-1, keepdims=True)
    rstd = lax.rsqrt(var + LN_EPS)
    return zc * rstd, rstd


def _ln_bwd(dy, z, g):
    zhat, rstd = _ln_stats(z)
    dyg = dy * g
    m1 = jnp.mean(dyg, axis=-1, keepdims=True)
    m2 = jnp.mean(dyg * zhat, axis=-1, keepdims=True)
    dz = rstd * (dyg - m1 - zhat * m2)
    return dz, jnp.sum(dy * zhat, axis=0, keepdims=True), jnp.sum(dy, axis=0, keepdims=True)


def _accumulate(ref, value, first):
    @pl.when(first)
    def _():
        ref[...] = value

    @pl.when(jnp.logical_not(first))
    def _():
        ref[...] += value


def _add_matmul(acc_ref, first, matmul):
    @pl.when(first)
    def _():
        acc_ref[...] = jnp.zeros_like(acc_ref)

    acc_ref[...] += matmul()


def _ffn_weight_specs(fs, d):
    def spec(row):
        return pl.BlockSpec((N_DEV, 1, fs, d), lambda i: (0, row, 0, 0), pipeline_mode=pl.Buffered(1))
    return [spec(0), spec(1), spec(2)]


def _ffn_fwd(x, w, ln_g, ln_b, alpha, name, job=None):
    s, d = x.shape
    fs = w.shape[2]
    f = N_DEV * fs
    tm = min(s, 256)

    def body(x_ref, wg_ref, wu_ref, wd_ref, g_ref, b_ref, a_ref, u_ref, z_ref, y_ref):
        xv = x_ref[...]
        xb = xv.astype(BF16)
        a = _dot_nt(xb, wg_ref[...].reshape(f, d))
        u = _dot_nt(xb, wu_ref[...].reshape(f, d))
        a_ref[...] = a.astype(BF16)
        u_ref[...] = u.astype(BF16)
        h = (a * jax.nn.sigmoid(a)) * u
        z = alpha * xv + 0.5 * _dot(h.astype(BF16), wd_ref[...].reshape(f, d))
        zhat, _ = _ln_stats(z)
        z_ref[...] = z
        y_ref[...] = zhat * g_ref[...] + b_ref[...]

    row = pl.BlockSpec((tm, d), lambda i: (i, 0))
    vec = pl.BlockSpec((1, d), lambda i: (0, 0))
    hid = pl.BlockSpec((tm, f), lambda i: (i, 0))
    call = _riding_call(
        body, job, 6, 4, s // tm, lambda: pl.program_id(0),
        name=name, grid=(s // tm,),
        in_specs=[row] + _ffn_weight_specs(fs, d) + [vec, vec],
        out_specs=[hid, hid, row, row],
        out_shape=[jax.ShapeDtypeStruct((s, f), BF16)] * 2 + [jax.ShapeDtypeStruct((s, d), F32)] * 2,
        compiler_params=_params())
    return call(x, w, w, w, ln_g, ln_b)


def _ffn_bwd_dx(dy, z, a, u, w, ln_g, alpha, name, job=None):
    s, d = dy.shape
    fs = w.shape[2]
    f = N_DEV * fs
    tm = min(s, 256)

    def body(dy_ref, z_ref, a_ref, u_ref, wg_ref, wu_ref, wd_ref, g_ref,
             dx_ref, da_ref, du_ref, df_ref, dg_ref, db_ref):
        i = pl.program_id(0)
        dz, dg, db = _ln_bwd(dy_ref[...], z_ref[...], g_ref[...])
        _accumulate(dg_ref, dg, i == 0)
        _accumulate(db_ref, db, i == 0)
        df = (0.5 * dz).astype(BF16)
        df_ref[...] = df
        av = a_ref[...].astype(F32)
        uv = u_ref[...].astype(F32)
        sg = jax.nn.sigmoid(av)
        dh = _dot_nt(df, wd_ref[...].reshape(f, d))
        du = (dh * (av * sg)).astype(BF16)
        da = (dh * uv * (sg * (1.0 + av * (1.0 - sg)))).astype(BF16)
        da_ref[...] = da
        du_ref[...] = du
        dx_ref[...] = alpha * dz + _dot(da, wg_ref[...].reshape(f, d)) + _dot(du, wu_ref[...].reshape(f, d))

    row = pl.BlockSpec((tm, d), lambda i: (i, 0))
    vec = pl.BlockSpec((1, d), lambda i: (0, 0))
    hid = pl.BlockSpec((tm, f), lambda i: (i, 0))
    call = _riding_call(
        body, job, 8, 6, s // tm, lambda: pl.program_id(0),
        name=name, grid=(s // tm,),
        in_specs=[row, row, hid, hid] + _ffn_weight_specs(fs, d) + [vec],
        out_specs=[row, hid, hid, row, vec, vec],
        out_shape=[jax.ShapeDtypeStruct((s, d), F32)] + [jax.ShapeDtypeStruct((s, f), BF16)] * 2
                  + [jax.ShapeDtypeStruct((s, d), BF16)] + [jax.ShapeDtypeStruct((1, d), F32)] * 2,
        compiler_params=_params())
    return call(dy, z, a, u, w, w, w, ln_g)


def _ffn_bwd_dwgu(da, du, x, fs, name):
    s, d = x.shape
    tf = FFN_CHUNK_DEVS * fs
    n_c = N_DEV // FFN_CHUNK_DEVS
    tk = min(s, 1024)
    n_k = s // tk

    def body(da_ref, du_ref, x_ref, out_ref, accg_s, accu_s):
        k = pl.program_id(1)
        xb = x_ref[...].astype(BF16)
        _add_matmul(accg_s, k == 0, lambda: _dot_tn(da_ref[...], xb))
        _add_matmul(accu_s, k == 0, lambda: _dot_tn(du_ref[...], xb))

        @pl.when(k == n_k - 1)
        def _():
            out_ref[:, 0] = accg_s[...].astype(BF16).reshape(FFN_CHUNK_DEVS, fs, d)
            out_ref[:, 1] = accu_s[...].astype(BF16).reshape(FFN_CHUNK_DEVS, fs, d)

    hid = pl.BlockSpec((tk, tf), lambda c, k: (k, c))
    return pl.pallas_call(
        body, name=name, grid=(n_c, n_k),
        in_specs=[hid, hid, pl.BlockSpec((tk, d), lambda c, k: (k, 0))],
        out_specs=pl.BlockSpec((FFN_CHUNK_DEVS, 2, fs, d), lambda c, k: (c, 0, 0, 0), pipeline_mode=pl.Buffered(1)),
        out_shape=jax.ShapeDtypeStruct((N_DEV, 3, fs, d), BF16),
        scratch_shapes=[pltpu.VMEM((tf, d), F32), pltpu.VMEM((tf, d), F32)],
        compiler_params=_params())(da, du, x)


def _ffn_bwd_dwd(a, u, df, blocks, name):
    s, d = df.shape
    fs = blocks.shape[2]
    tf = FFN_CHUNK_DEVS * fs
    n_c = N_DEV // FFN_CHUNK_DEVS
    tk = min(s, 512)
    n_k = s // tk

    def body(a_ref, u_ref, df_ref, blocks_ref, out_ref, acc_s):
        k = pl.program_id(1)
        av = a_ref[...].astype(F32)
        h = ((av * jax.nn.sigmoid(av)) * u_ref[...].astype(F32)).astype(BF16)
        _accumulate(acc_s, _dot_tn(h, df_ref[...]), k == 0)

        @pl.when(k == n_k - 1)
        def _():
            out_ref[:, 0] = acc_s[...].astype(BF16).reshape(FFN_CHUNK_DEVS, fs, d)

    hid = pl.BlockSpec((tk, tf), lambda c, k: (k, c))
    return pl.pallas_call(
        body, name=name, grid=(n_c, n_k),
        in_specs=[hid, hid, pl.BlockSpec((tk, d), lambda c, k: (k, 0)), _any_spec()],
        out_specs=pl.BlockSpec((FFN_CHUNK_DEVS, 1, fs, d), lambda c, k: (c, 2, 0, 0)),
        out_shape=jax.ShapeDtypeStruct(blocks.shape, BF16), input_output_aliases={3: 0},
        scratch_shapes=[pltpu.VMEM((tf, d), F32)],
        compiler_params=_params())(a, u, df, blocks)


def _whole(arr):
    return pl.BlockSpec(arr.shape, lambda i: (0,) * arr.ndim)


def _win_fwd(x, w_in, name):
    s, d = x.shape
    d_in = N_DEV * w_in.shape[1]
    tm = min(s, 512)
    scale = NA_HEAD_DIM ** -0.5
    assert d_in == D_LOC + 3 * D_NA and scale == 0.125

    def body(x_ref, w_ref, loc_ref, qkv_ref):
        proj = _dot_nt(x_ref[...].astype(BF16), w_ref[...].reshape(d_in, d))
        loc_ref[...] = proj[:, :D_LOC]
        qkv_ref[:, :D_NA] = (proj[:, D_LOC:D_LOC + D_NA] * scale).astype(BF16)
        qkv_ref[:, D_NA:] = proj[:, D_LOC + D_NA:].astype(BF16)

    return pl.pallas_call(
        body, name=name, grid=(s // tm,),
        in_specs=[pl.BlockSpec((tm, d), lambda i: (i, 0)), _whole(w_in)],
        out_specs=[pl.BlockSpec((tm, D_LOC), lambda i: (i, 0)), pl.BlockSpec((tm, 3 * D_NA), lambda i: (i, 0))],
        out_shape=[jax.ShapeDtypeStruct((s, D_LOC), F32), jax.ShapeDtypeStruct((s, 3 * D_NA), BF16)],
        compiler_params=_params())(x, w_in)


def _wout_fwd(x, yab, yc, w_out, ln_g, ln_b, alpha, name):
    s, d = x.shape
    tm = min(s, 512)

    def body(x_ref, yab_ref, yc_ref, w_ref, g_ref, b_ref, z_ref, y_ref):
        mix = jnp.concatenate([yab_ref[...], yc_ref[...]], axis=1).astype(BF16)
        z = alpha * x_ref[...] + _dot(mix, w_ref[...].reshape(D_MIX, d))
        zhat, _ = _ln_stats(z)
        z_ref[...] = z
        y_ref[...] = zhat * g_ref[...] + b_ref[...]

    row = pl.BlockSpec((tm, d), lambda i: (i, 0))
    half = pl.BlockSpec((tm, D_MIX // 2), lambda i: (i, 0))
    vec = pl.BlockSpec((1, d), lambda i: (0, 0))
    return pl.pallas_call(
        body, name=name, grid=(s // tm,),
        in_specs=[row, half, half, _whole(w_out), vec, vec],
        out_specs=[row, row], out_shape=[jax.ShapeDtypeStruct((s, d), F32)] * 2,
        compiler_params=_params())(x, yab, yc, w_out, ln_g, ln_b)


def _wout_bwd(dy, z, yab, yc, w_out, ln_g, alpha, name):
    s, d = dy.shape
    rs = w_out.shape[1]
    tm = min(s, 512)
    n_i = s // tm

    def body(dy_ref, z_ref, yab_ref, yc_ref, w_ref, g_ref, dmix_ref, dxp_ref, dg_ref, db_ref, out_ref, acc_s):
        i = pl.program_id(0)
        dz, dg, db = _ln_bwd(dy_ref[...], z_ref[...], g_ref[...])
        _accumulate(dg_ref, dg, i == 0)
        _accumulate(db_ref, db, i == 0)
        dxp_ref[...] = alpha * dz
        dzb = dz.astype(BF16)
        dmix_ref[...] = _dot_nt(dzb, w_ref[...].reshape(D_MIX, d))
        mix = jnp.concatenate([yab_ref[...], yc_ref[...]], axis=1).astype(BF16)
        _add_matmul(acc_s, i == 0, lambda: _dot_tn(mix, dzb))

        @pl.when(i == n_i - 1)
        def _():
            out_ref[...] = acc_s[...].astype(BF16).reshape(N_DEV, rs, d)

    row = pl.BlockSpec((tm, d), lambda i: (i, 0))
    half = pl.BlockSpec((tm, D_MIX // 2), lambda i: (i, 0))
    vec = pl.BlockSpec((1, d), lambda i: (0, 0))
    return pl.pallas_call(
        body, name=name, grid=(n_i,),
        in_specs=[row, row, half, half, _whole(w_out), vec],
        out_specs=[pl.BlockSpec((tm, D_MIX), lambda i: (i, 0)), row, vec, vec, _whole(w_out)],
        out_shape=[jax.ShapeDtypeStruct((s, D_MIX), F32), jax.ShapeDtypeStruct((s, d), F32),
                   jax.ShapeDtypeStruct((1, d), F32), jax.ShapeDtypeStruct((1, d), F32),
                   jax.ShapeDtypeStruct(w_out.shape, BF16)],
        scratch_shapes=[pltpu.VMEM((D_MIX, d), F32)],
        compiler_params=_params())(dy, z, yab, yc, w_out, ln_g)


def _win_bwd(dxp, dloc, dq, dk, dv, x, w_in, name):
    s, d = x.shape
    rs = w_in.shape[1]
    d_in = N_DEV * rs
    tm = min(s, 256)
    n_i = s // tm

    def body(dxp_ref, dloc_ref, dq_ref, dk_ref, dv_ref, x_ref, w_ref, dx_ref, out_ref, acc_s):
        i = pl.program_id(0)
        dp = jnp.concatenate([dloc_ref[...], dq_ref[...], dk_ref[...].astype(BF16), dv_ref[...].astype(BF16)], axis=1)
        dx_ref[...] = dxp_ref[...] + _dot(dp, w_ref[...].reshape(d_in, d))
        _add_matmul(acc_s, i == 0, lambda: _dot_tn(dp, x_ref[...].astype(BF16)))

        @pl.when(i == n_i - 1)
        def _():
            out_ref[...] = acc_s[...].astype(BF16).reshape(N_DEV, rs, d)

    row = pl.BlockSpec((tm, d), lambda i: (i, 0))
    na = pl.BlockSpec((tm, D_NA), lambda i: (i, 0))
    return pl.pallas_call(
        body, name=name, grid=(n_i,),
        in_specs=[row, pl.BlockSpec((tm, D_LOC), lambda i: (i, 0)), na, na, na, row, _whole(w_in)],
        out_specs=[row, _whole(w_in)],
        out_shape=[jax.ShapeDtypeStruct((s, d), F32), jax.ShapeDtypeStruct(w_in.shape, BF16)],
        scratch_shapes=[pltpu.VMEM((d_in, d), F32)],
        compiler_params=_params())(dxp, dloc, dq, dk, dv, x, w_in)


def _shift_rows(v, k):
    n = v.shape[0]
    return pltpu.roll(v, k % n, 0)


def _halo_specs(tm, s, width, col):
    per = tm // HALO
    last = s // HALO - 1
    return [pl.BlockSpec((HALO, width), lambda i: (jnp.maximum(i * per - 1, 0), col)),
            pl.BlockSpec((tm, width), lambda i: (i, col)),
            pl.BlockSpec((HALO, width), lambda i: (jnp.minimum((i + 1) * per, last), col))]


def _token_index(i, tm):
    return i * tm - HALO + lax.broadcasted_iota(jnp.int32, (tm + 2 * HALO, 1), 0)


def _pool_lane_tables():
    lane = lax.broadcasted_iota(jnp.int32, (1, D_POOL), 1)
    group = sum((lane >= g * POOL_GROUP).astype(jnp.int32) for g in range(1, len(POOL_WINDOWS)))
    half = jnp.where(group == 0, 1, jnp.where(group == 1, 2, jnp.where(group == 2, 4, 8)))
    return group, half


def _window_sums(v, group, offsets):
    s2 = v + _shift_rows(v, 1)
    s4 = s2 + _shift_rows(s2, 2)
    s8 = s4 + _shift_rows(s4, 4)
    s16 = s8 + _shift_rows(s8, 8)
    parts = [_shift_rows(p, -o) if o else p for p, o in zip((s2, s4, s8, s16), offsets)]
    return jnp.where(group == 0, parts[0], jnp.where(group == 1, parts[1], jnp.where(group == 2, parts[2], parts[3])))


def _pool_counts(tok, half, s):
    return (jnp.minimum(tok + half, s) - jnp.maximum(tok - half, 0)).astype(F32)


def _pool_forward(u, tok, s):
    group, half = _pool_lane_tables()
    sums = _window_sums(u, group, [w // 2 - 1 for w in POOL_WINDOWS])
    return sums / _pool_counts(tok, half, s) - u


def _conv_forward(zc, cw_ref):
    return cw_ref[0:1, :] * _shift_rows(zc, 1) + cw_ref[1:2, :] * zc + cw_ref[2:3, :] * _shift_rows(zc, -1)


def _local_fwd(proj, pool_bd, pool_scale, conv_w, name):
    s = proj.shape[0]
    tm = min(s, 512)
    ctr = slice(HALO, HALO + tm)

    def body(prev_ref, cur_ref, next_ref, pw_ref, sc_ref, cw_ref, out_ref):
        i = pl.program_id(0)
        ext = jnp.concatenate([prev_ref[...], cur_ref[...], next_ref[...]], axis=0)
        tok = _token_index(i, tm)
        inside = (tok >= 0) & (tok < s)
        u = jnp.where(inside, ext[:, 0:D_POOL], 0.0)
        p = _pool_forward(u, tok, s)[ctr]
        ya = _dot(p.astype(BF16), pw_ref[...]) * sc_ref[...]
        gb = ext[:, D_POOL:D_POOL + D_CONV]
        zc = jnp.where(inside, ext[:, D_POOL + D_CONV:D_POOL + 2 * D_CONV] * ext[:, D_POOL + 2 * D_CONV:D_LOC], 0.0)
        yb = (gb * _conv_forward(zc, cw_ref))[ctr]
        out_ref[...] = jnp.concatenate([ya, yb], axis=1)

    return pl.pallas_call(
        body, name=name, grid=(s // tm,),
        in_specs=_halo_specs(tm, s, D_LOC, 0) + [
            pl.BlockSpec((D_POOL, D_POOL), lambda i: (0, 0)), pl.BlockSpec((1, D_POOL), lambda i: (0, 0)),
            pl.BlockSpec((3, D_CONV), lambda i: (0, 0))],
        out_specs=pl.BlockSpec((tm, D_POOL + D_CONV), lambda i: (i, 0)),
        out_shape=jax.ShapeDtypeStruct((s, D_POOL + D_CONV), F32),
        compiler_params=_params())(proj, proj, proj, pool_bd, pool_scale, conv_w)


def _local_bwd(proj, dmix, pool_bd, pool_scale, conv_w, name):
    s = proj.shape[0]
    tm = min(s, 512)
    ctr = slice(HALO, HALO + tm)

    def body(prev_ref, cur_ref, next_ref, dprev_ref, dcur_ref, dnext_ref, pw_ref, sc_ref, cw_ref,
             dloc_ref, dpw_ref, dsc_ref, dcw_ref):
        i = pl.program_id(0)
        first = i == 0
        ext = jnp.concatenate([prev_ref[...], cur_ref[...], next_ref[...]], axis=0)
        dext = jnp.concatenate([dprev_ref[...], dcur_ref[...], dnext_ref[...]], axis=0)
        tok = _token_index(i, tm)
        inside = (tok >= 0) & (tok < s)
        group, half = _pool_lane_tables()
        cnt = _pool_counts(tok, half, s)
        u = jnp.where(inside, ext[:, 0:D_POOL], 0.0)
        dya = jnp.where(inside, dext[:, 0:D_POOL], 0.0)
        p_c = _pool_forward(u, tok, s)[ctr].astype(BF16)
        lin = _dot(p_c, pw_ref[...])
        _accumulate(dsc_ref, jnp.sum(dya[ctr] * lin, axis=0, keepdims=True), first)
        e1 = (dya * sc_ref[...]).astype(BF16)
        _accumulate(dpw_ref, _dot_tn(p_c, e1[ctr]), first)
        dp = _dot_nt(e1, pw_ref[...])
        du = _window_sums(dp / cnt, group, [w // 2 for w in POOL_WINDOWS]) - dp
        gb = ext[:, D_POOL:D_POOL + D_CONV]
        gc = ext[:, D_POOL + D_CONV:D_POOL + 2 * D_CONV]
        hv = ext[:, D_POOL + 2 * D_CONV:D_LOC]
        zc = jnp.where(inside, gc * hv, 0.0)
        dyb = jnp.where(inside, dext[:, D_POOL:D_POOL + D_CONV], 0.0)
        dgb = dyb * _conv_forward(zc, cw_ref)
        dyc = dyb * gb
        for k in range(3):
            part = jnp.sum(dyc[ctr] * _shift_rows(zc, 1 - k)[ctr], axis=0, keepdims=True)
            _accumulate(dcw_ref.at[k:k + 1, :], part, first)
        dzc = cw_ref[0:1, :] * _shift_rows(dyc, -1) + cw_ref[1:2, :] * dyc + cw_ref[2:3, :] * _shift_rows(dyc, 1)
        dloc = jnp.concatenate([du, dgb, dzc * hv, dzc * gc], axis=1)
        dloc_ref[...] = dloc[ctr].astype(BF16)

    return pl.pallas_call(
        body, name=name, grid=(s // tm,),
        in_specs=_halo_specs(tm, s, D_LOC, 0) + _halo_specs(tm, s, D_POOL + D_CONV, 0) + [
            pl.BlockSpec((D_POOL, D_POOL), lambda i: (0, 0)), pl.BlockSpec((1, D_POOL), lambda i: (0, 0)),
            pl.BlockSpec((3, D_CONV), lambda i: (0, 0))],
        out_specs=[pl.BlockSpec((tm, D_LOC), lambda i: (i, 0)), pl.BlockSpec((D_POOL, D_POOL), lambda i: (0, 0)),
                   pl.BlockSpec((1, D_POOL), lambda i: (0, 0)), pl.BlockSpec((8, D_CONV), lambda i: (0, 0))],
        out_shape=[jax.ShapeDtypeStruct((s, D_LOC), BF16), jax.ShapeDtypeStruct((D_POOL, D_POOL), F32),
                   jax.ShapeDtypeStruct((1, D_POOL), F32), jax.ShapeDtypeStruct((8, D_CONV), F32)],
        compiler_params=_params())(proj, proj, proj, dmix, dmix, dmix, pool_bd, pool_scale, conv_w)


def _na_geometry(rows):
    n_j = rows // Q_ROWS
    dr = np.full((3, Q_ROWS, K_ROWS), 2 * NA_ROWS - 1, np.int64)
    for t, j in enumerate((0, min(1, n_j - 1), n_j - 1)):
        base = int(np.clip(Q_ROWS * j - NA_ROWS // 2, 0, rows - K_ROWS))
        for qr in range(Q_ROWS):
            r = Q_ROWS * j + qr
            start = int(np.clip(r - NA_ROWS // 2, 0, rows - NA_ROWS))
            for kr in range(K_ROWS):
                if start <= base + kr < start + NA_ROWS:
                    dr[t, qr, kr] = base + kr - r + NA_ROWS - 1
    return dr


def _na_col_tables():
    c = np.arange(GRID_W)
    start = np.clip(c - NA_COLS // 2, 0, GRID_W - NA_COLS)
    valid = (c[None, :] >= start[:, None]) & (c[None, :] < start[:, None] + NA_COLS)
    dc = np.clip(c[None, :] - c[:, None], -(NA_COLS - 1), NA_COLS - 1) + (NA_COLS - 1)
    return valid, dc


NO_ROW = 2 * NA_ROWS - 1
N_SLOT = 2 * NA_ROWS


def _na_tiles(rpb):
    valid, dc = _na_col_tables()
    onehot = jnp.asarray((dc[None] == np.arange(2 * NA_COLS - 1)[:, None, None]).astype(np.float32))
    table = jnp.einsum("hrd,dqk->hrqk", rpb, onehot, precision=lax.Precision.HIGHEST)
    table = jnp.where(jnp.asarray(valid)[None, None], table, NEG_INF)
    outside = jnp.full((NA_HEADS, 1, GRID_W, GRID_W), NEG_INF, F32)
    padded = jnp.concatenate([outside, table, outside], axis=1)
    pairs = jnp.concatenate([padded[:, :N_SLOT], padded[:, 1:]], axis=-1)
    return jnp.concatenate([pairs, jnp.full((NA_HEADS, 1, GRID_W, 2 * GRID_W), NEG_INF, F32)], axis=1)


G_ROWS = 2
N_GRP = Q_ROWS // G_ROWS
G_TOK = G_ROWS * GRID_W
GK_ROWS = NA_ROWS + G_ROWS
GK_TOK = GK_ROWS * GRID_W
STACK_TOK = N_GRP * 2 * G_TOK


def _na_group_tables(rows):
    dr = _na_geometry(rows)
    koff = np.zeros((3, N_GRP), np.int64)
    slot = np.zeros((3, N_GRP, G_ROWS, GK_ROWS // 2), np.int64)
    even_in, odd_in = np.zeros_like(slot), np.zeros_like(slot)
    for t in range(3):
        for g in range(N_GRP):
            qrs = range(G_ROWS * g, G_ROWS * (g + 1))
            inside = [kr for kr in range(K_ROWS) if any(dr[t, qr, kr] != NO_ROW for qr in qrs)]
            lo, hi = min(inside), max(inside) + 1
            off = min(lo - lo % 2, K_ROWS - GK_ROWS)
            assert off <= lo and hi <= off + GK_ROWS
            koff[t, g] = off
            for qq, qr in enumerate(qrs):
                for kp in range(GK_ROWS // 2):
                    even, odd = int(dr[t, qr, off + 2 * kp]), int(dr[t, qr, off + 2 * kp + 1])
                    even_in[t, g, qq, kp], odd_in[t, g, qq, kp] = even != NO_ROW, odd != NO_ROW
                    slot[t, g, qq, kp] = (N_SLOT if even == NO_ROW and odd == NO_ROW
                                          else (even if even != NO_ROW else odd - 1) + 1)
    return koff, slot, even_in, odd_in


def _by_type(block_type, per_type):
    a, b, c = (int(v) for v in per_type)
    if a == b == c:
        return a
    return jnp.where(block_type == 0, a, jnp.where(block_type == 2, c, b))


def _score_rows(g, hh):
    first = (2 * g + hh) * G_TOK
    return slice(first, first + G_TOK)


def _tile_at(g, hh, qq, kp):
    first = _score_rows(g, hh).start + qq * GRID_W
    return slice(first, first + GRID_W), slice(kp * 2 * GRID_W, (kp + 1) * 2 * GRID_W)


def _fill_bias(bias_s, tiles_ref, block_type, tables):
    _, slot, even_in, odd_in = tables
    left = lax.broadcasted_iota(jnp.int32, (1, 2 * GRID_W), 1) < GRID_W
    for hh in range(2):
        for g in range(N_GRP):
            for qq in range(G_ROWS):
                for kp in range(GK_ROWS // 2):
                    tile = tiles_ref[hh, _by_type(block_type, slot[:, g, qq, kp])]
                    tile = jnp.where(left & (_by_type(block_type, even_in[:, g, qq, kp]) == 0), NEG_INF, tile)
                    tile = jnp.where(jnp.logical_not(left) & (_by_type(block_type, odd_in[:, g, qq, kp]) == 0), NEG_INF, tile)
                    rs, cs = _tile_at(g, hh, qq, kp)
                    bias_s[rs, cs] = tile


def _group_offset(block_type, koff, g):
    off = _by_type(block_type, koff[:, g]) * GRID_W
    return off if isinstance(off, int) else pl.multiple_of(off, 2 * GRID_W)


def _na_specs(s, proj_cols):
    n_blk = s // K_BLK
    per = Q_TOK // K_BLK

    def kv_spec(col0, m):
        return pl.BlockSpec((K_BLK, HEAD_PAIR), lambda hp, j: (jnp.clip(per * j - 1, 0, n_blk - 4) + m, col0 + hp))

    q_col, k_col, v_col = (c // HEAD_PAIR for c in proj_cols)
    return ([pl.BlockSpec((Q_TOK, HEAD_PAIR), lambda hp, j: (j, q_col + hp))]
            + [kv_spec(k_col, m) for m in range(4)] + [kv_spec(v_col, m) for m in range(4)])


def _na_block_type(j, n_j):
    return jnp.where(j == 0, 0, jnp.where(j == n_j - 1, 2, 1))


def _head_masks():
    lane = lax.broadcasted_iota(jnp.int32, (1, HEAD_PAIR), 1)
    return [lane < NA_HEAD_DIM, lane >= NA_HEAD_DIM]


def _attn_fwd(qkv, tiles, name):
    s = qkv.shape[0]
    n_j = s // Q_TOK
    tables = _na_group_tables(s // GRID_W)
    koff = tables[0]

    def body(q_ref, k0, k1, k2, k3, v0, v1, v2, v3, tiles_ref, o_ref, lse_ref, bias_s, k_s, vh_s, sc_s, p_s):
        j = pl.program_id(1)
        block_type = _na_block_type(j, n_j)
        pl.when((j == 0) | (j == 1) | (j == n_j - 1))(functools.partial(_fill_bias, bias_s, tiles_ref, block_type, tables))
        masks = _head_masks()
        for m, (kr, vr) in enumerate(zip((k0, k1, k2, k3), (v0, v1, v2, v3))):
            rows = slice(m * K_BLK, (m + 1) * K_BLK)
            k_s[rows, :] = kr[...]
            v = vr[...]
            for hh, mask in enumerate(masks):
                vh_s[hh, rows, :] = jnp.where(mask, v, jnp.zeros_like(v))
        q = q_ref[...]
        qh = [jnp.where(mask, q, jnp.zeros_like(q)) for mask in masks]
        offs = [_group_offset(block_type, koff, g) for g in range(N_GRP)]
        for g in range(N_GRP):
            kg = k_s[pl.ds(offs[g], GK_TOK), :]
            for hh in range(2):
                sc_s[_score_rows(g, hh), :] = _dot_nt(qh[hh][g * G_TOK:(g + 1) * G_TOK], kg)
        sc = sc_s[...] + bias_s[...]
        mx = jnp.max(sc, axis=-1, keepdims=True)
        p = jnp.exp(sc - mx)
        den = jnp.sum(p, axis=-1, keepdims=True)
        p_s[...] = p.astype(BF16)
        inv = 1.0 / den
        lse = mx + jnp.log(den)
        for g in range(N_GRP):
            rows = slice(g * G_TOK, (g + 1) * G_TOK)
            out = jnp.zeros((G_TOK, HEAD_PAIR), F32)
            for hh in range(2):
                sr = _score_rows(g, hh)
                out = out + _dot(p_s[sr, :], vh_s[hh, pl.ds(offs[g], GK_TOK), :]) * inv[sr]
            o_ref[rows, :] = out
            lse_ref[0, rows, :] = jnp.where(masks[0], lse[_score_rows(g, 0)], lse[_score_rows(g, 1)])

    return pl.pallas_call(
        body, name=name, grid=(NA_HEADS // 2, n_j),
        in_specs=_na_specs(s, (0, D_NA, 2 * D_NA)) + [
            pl.BlockSpec((2, N_SLOT + 1, GRID_W, 2 * GRID_W), lambda hp, j: (hp, 0, 0, 0))],
        out_specs=[pl.BlockSpec((Q_TOK, HEAD_PAIR), lambda hp, j: (j, hp)),
                   pl.BlockSpec((1, Q_TOK, HEAD_PAIR), lambda hp, j: (hp, j, 0))],
        out_shape=[jax.ShapeDtypeStruct((s, D_NA), F32), jax.ShapeDtypeStruct((NA_HEADS // 2, s, HEAD_PAIR), F32)],
        scratch_shapes=[pltpu.VMEM((STACK_TOK, GK_TOK), F32), pltpu.VMEM((K_TOK, HEAD_PAIR), BF16),
                        pltpu.VMEM((2, K_TOK, HEAD_PAIR), BF16), pltpu.VMEM((STACK_TOK, GK_TOK), F32),
                        pltpu.VMEM((STACK_TOK, GK_TOK), BF16)],
        compiler_params=_params())(*([qkv] * 9), tiles)


def _add_tiles(dtile_ref, hh, g, ds_ref, block_type, slot):
    for qq in range(G_ROWS):
        for kp in range(GK_ROWS // 2):
            per_type = slot[:, g, qq, kp]
            if (per_type == N_SLOT).all():
                continue
            rs, cs = _tile_at(g, hh, qq, kp)
            dtile_ref[hh, _by_type(block_type, np.where(per_type == N_SLOT, 0, per_type))] += ds_ref[rs, cs]


def _attn_bwd(qkv, tiles, o, dmix, lse, name, job=None):
    s = qkv.shape[0]
    n_j = s // Q_TOK
    n_blk = s // K_BLK
    per = Q_TOK // K_BLK
    scale = NA_HEAD_DIM ** -0.5
    do_col = (D_POOL + D_CONV) // HEAD_PAIR
    tables = _na_group_tables(s // GRID_W)
    koff, slot = tables[0], tables[1]

    def body(q_ref, k0, k1, k2, k3, v0, v1, v2, v3, tiles_ref, o_ref, do_ref, lse_ref,
             dq_ref, dk_ref, dv_ref, dtile_ref, bias_s, k_s, kh_s, v_s, s_s, dp_s, pb_s, dsb_s):
        j = pl.program_id(1)

        @pl.when(j == 0)
        def _():
            dk_ref[...] = jnp.zeros_like(dk_ref)
            dv_ref[...] = jnp.zeros_like(dv_ref)
            dtile_ref[...] = jnp.zeros_like(dtile_ref)

        block_type = _na_block_type(j, n_j)
        pl.when((j == 0) | (j == 1) | (j == n_j - 1))(functools.partial(_fill_bias, bias_s, tiles_ref, block_type, tables))
        base = pl.multiple_of(jnp.clip(per * j - 1, 0, n_blk - 4) * K_BLK, K_BLK)
        masks = _head_masks()
        for m, (kr, vr) in enumerate(zip((k0, k1, k2, k3), (v0, v1, v2, v3))):
            rows = slice(m * K_BLK, (m + 1) * K_BLK)
            k = kr[...]
            k_s[rows, :] = k
            v_s[rows, :] = vr[...]
            for hh, mask in enumerate(masks):
                kh_s[hh, rows, :] = jnp.where(mask, k, jnp.zeros_like(k))
        q = q_ref[...]
        qh = [jnp.where(mask, q, jnp.zeros_like(q)) for mask in masks]
        lane = lax.broadcasted_iota(jnp.int32, (1, HEAD_PAIR), 1)
        offs = [_group_offset(block_type, koff, g) for g in range(N_GRP)]
        do, ov, lse = do_ref[...], o_ref[...], lse_ref[0]
        dob, lse_col, delta_col = {}, [], []
        for g in range(N_GRP):
            rows = slice(g * G_TOK, (g + 1) * G_TOK)
            kg = k_s[pl.ds(offs[g], GK_TOK), :]
            vg = v_s[pl.ds(offs[g], GK_TOK), :]
            for hh, mask in enumerate(masks):
                doh = jnp.where(mask, do[rows], 0.0)
                dob[g, hh] = doh.astype(BF16)
                lse_col.append(jnp.sum(jnp.where(lane == hh * NA_HEAD_DIM, lse[rows], 0.0), axis=-1, keepdims=True))
                delta_col.append(jnp.sum(doh * ov[rows], axis=-1, keepdims=True))
                s_s[_score_rows(g, hh), :] = _dot_nt(qh[hh][rows], kg)
                dp_s[_score_rows(g, hh), :] = _dot_nt(dob[g, hh], vg)
        p = jnp.exp(s_s[...] + bias_s[...] - jnp.concatenate(lse_col, axis=0))
        ds = p * (dp_s[...] - jnp.concatenate(delta_col, axis=0))
        pb_s[...] = p.astype(BF16)
        dsb_s[...] = ds.astype(BF16)
        s_s[...] = ds
        for g in range(N_GRP):
            rows = slice(g * G_TOK, (g + 1) * G_TOK)
            dq = jnp.zeros((G_TOK, HEAD_PAIR), F32)
            dk = jnp.zeros((GK_TOK, HEAD_PAIR), F32)
            dv = jnp.zeros((GK_TOK, HEAD_PAIR), F32)
            for hh in range(2):
                sr = _score_rows(g, hh)
                _add_tiles(dtile_ref, hh, g, s_s, block_type, slot)
                dsb = dsb_s[sr, :]
                dq = dq + _dot(dsb, kh_s[hh, pl.ds(offs[g], GK_TOK), :])
                dk = dk + _dot_tn(dsb, qh[hh][rows])
                dv = dv + _dot_tn(pb_s[sr, :], dob[g, hh])
            dq_ref[rows, :] = (dq * scale).astype(BF16)
            at = pl.multiple_of(base + offs[g], 2 * GRID_W)
            dk_ref[pl.ds(at, GK_TOK), :] += dk
            dv_ref[pl.ds(at, GK_TOK), :] += dv

    pair = pl.BlockSpec((Q_TOK, HEAD_PAIR), lambda hp, j: (j, hp))
    whole = pl.BlockSpec((s, HEAD_PAIR), lambda hp, j: (0, hp))
    call = _riding_call(
        body, job, 13, 4, (NA_HEADS // 2) * n_j, lambda: pl.program_id(0) * n_j + pl.program_id(1),
        name=name, grid=(NA_HEADS // 2, n_j),
        in_specs=_na_specs(s, (0, D_NA, 2 * D_NA)) + [
            pl.BlockSpec((2, N_SLOT + 1, GRID_W, 2 * GRID_W), lambda hp, j: (hp, 0, 0, 0)),
            pair, pl.BlockSpec((Q_TOK, HEAD_PAIR), lambda hp, j: (j, do_col + hp)),
            pl.BlockSpec((1, Q_TOK, HEAD_PAIR), lambda hp, j: (hp, j, 0))],
        out_specs=[pair, whole, whole, pl.BlockSpec((2, N_SLOT, GRID_W, 2 * GRID_W), lambda hp, j: (hp, 0, 0, 0))],
        out_shape=[jax.ShapeDtypeStruct((s, D_NA), BF16), jax.ShapeDtypeStruct((s, D_NA), F32),
                   jax.ShapeDtypeStruct((s, D_NA), F32),
                   jax.ShapeDtypeStruct((NA_HEADS, N_SLOT, GRID_W, 2 * GRID_W), F32)],
        scratch_shapes=[pltpu.VMEM((STACK_TOK, GK_TOK), F32), pltpu.VMEM((K_TOK, HEAD_PAIR), BF16),
                        pltpu.VMEM((2, K_TOK, HEAD_PAIR), BF16), pltpu.VMEM((K_TOK, HEAD_PAIR), BF16),
                        pltpu.VMEM((STACK_TOK, GK_TOK), F32), pltpu.VMEM((STACK_TOK, GK_TOK), F32),
                        pltpu.VMEM((STACK_TOK, GK_TOK), BF16), pltpu.VMEM((STACK_TOK, GK_TOK), BF16)],
        compiler_params=_params())
    return call(*([qkv] * 9), tiles, o, dmix, lse)


def _rpb_finish(tiles, name):
    valid, dc = _na_col_tables()
    n_dc = 2 * NA_COLS - 1
    sel = np.zeros((GRID_W, 2 * GRID_W, LANES), np.float32)
    for qc in range(GRID_W):
        for kc in range(GRID_W):
            if valid[qc, kc]:
                sel[qc, kc, dc[qc, kc]] = 1.0
                sel[qc, GRID_W + kc, LANES // 2 + dc[qc, kc]] = 1.0
    sel = jnp.asarray(sel.reshape(GRID_W * 2 * GRID_W, LANES))
    flat = tiles.reshape(NA_HEADS * 2 * NA_ROWS, GRID_W * 2 * GRID_W)

    def body(a_ref, b_ref, out_ref):
        out_ref[...] = jnp.dot(a_ref[...], b_ref[...], preferred_element_type=F32, precision=lax.Precision.HIGHEST)

    sums = pl.pallas_call(
        body, name=name, out_shape=jax.ShapeDtypeStruct((flat.shape[0], LANES), F32),
        compiler_params=_params())(flat, sel).reshape(NA_HEADS, 2 * NA_ROWS, LANES)
    return sums[:, 1:, :n_dc] + sums[:, :2 * NA_ROWS - 1, LANES // 2:LANES // 2 + n_dc]


def _loss_grad(y, target, name):
    s, d = y.shape
    tm = min(s, 1024)

    def body(y_ref, t_ref, sum_ref, dy_ref):
        diff = y_ref[...] - t_ref[...]
        dy_ref[...] = diff * (1.0 / d)
        part = jnp.zeros((8, LANES), F32) + jnp.sum(diff * diff)
        _accumulate(sum_ref, part, pl.program_id(0) == 0)

    row = pl.BlockSpec((tm, d), lambda i: (i, 0))
    return pl.pallas_call(
        body, name=name, grid=(s // tm,), in_specs=[row, row],
        out_specs=[pl.BlockSpec((8, LANES), lambda i: (0, 0)), row],
        out_shape=[jax.ShapeDtypeStruct((8, LANES), F32), jax.ShapeDtypeStruct((s, d), F32)],
        compiler_params=_params())(y, target)


def _adamw(w, g, m, v, name):
    rows, cols = w.shape
    tr = _row_tile(rows, 512, 8)

    def body(w_ref, g_ref, m_ref, v_ref, d_ref, nm_ref, nv_ref):
        gv = g_ref[...]
        nm = ADAM_B1 * m_ref[...] + (1.0 - ADAM_B1) * gv
        nv = ADAM_B2 * v_ref[...] + (1.0 - ADAM_B2) * (gv * gv)
        m_hat = nm / (1.0 - ADAM_B1 ** ADAM_STEP)
        v_hat = nv / (1.0 - ADAM_B2 ** ADAM_STEP)
        d_ref[...] = -ADAM_LR * (m_hat / (jnp.sqrt(v_hat) + ADAM_EPS) + ADAM_WD * w_ref[...])
        nm_ref[...] = nm
        nv_ref[...] = nv

    blk = pl.BlockSpec((tr, cols), lambda r: (r, 0))
    return pl.pallas_call(
        body, name=name, grid=(rows // tr,), in_specs=[blk] * 4, out_specs=[blk] * 3,
        out_shape=[jax.ShapeDtypeStruct((rows, cols), F32)] * 3, compiler_params=_params())(w, g, m, v)


def _adamw_nd(w, g, m, v, name):
    shape = w.shape
    flat = lambda t: t.reshape(-1, shape[-1])
    return tuple(t.reshape(shape) for t in _adamw(flat(w), flat(g), flat(m), flat(v), name))


def _pack(parts, rows_mult=64):
    flat = jnp.concatenate([p.reshape(-1).astype(F32) for p in parts])
    per = LANES * rows_mult
    total = -(-flat.shape[0] // per) * per
    return jnp.pad(flat, (0, total - flat.shape[0])).reshape(-1, LANES)


def _unpack(packed, shapes):
    flat = packed.reshape(-1)
    out, pos = [], 0
    for shp in shapes:
        n = int(np.prod(shp))
        out.append(flat[pos:pos + n].reshape(shp))
        pos += n
    return out


def kernel(x, ffn1_w_gate, ffn1_w_up, ffn1_w_down, ffn2_w_gate, ffn2_w_up, ffn2_w_down, w_in, pool_w, pool_scale, conv_w, rpb, w_out, ln_g, ln_b, loss_target, m_ffn1_w_gate, m_ffn1_w_up, m_ffn1_w_down, m_ffn2_w_gate, m_ffn2_w_up, m_ffn2_w_down, m_w_in, m_pool_w, m_pool_scale, m_conv_w, m_rpb, m_w_out, m_ln_g, m_ln_b, v_ffn1_w_gate, v_ffn1_w_up, v_ffn1_w_down, v_ffn2_w_gate, v_ffn2_w_up, v_ffn2_w_down, v_w_in, v_pool_w, v_pool_scale, v_conv_w, v_rpb, v_w_out, v_ln_g, v_ln_b):
    n_l, d, fs = ffn1_w_gate.shape
    s = x.shape[1]
    rows = s // GRID_W
    assert x.shape[0] == 1 and s % Q_TOK == 0 and rows >= K_ROWS and fs % BF16_ROWS == 0
    alpha = (2.0 * n_l) ** 0.25
    xi, yi, ci = _mesh_pos()
    me = 4 * xi + 2 * yi + ci
    core = jnp.reshape(ci, (1,)).astype(jnp.int32)
    ln_w, cw_w = ln_g.shape[2], conv_w.shape[2]

    tr = lambda w: jnp.swapaxes(w, 1, 2)
    ffn1_shard = jnp.stack([tr(ffn1_w_gate), tr(ffn1_w_up), ffn1_w_down], axis=1).astype(BF16)
    ffn2_shard = jnp.stack([tr(ffn2_w_gate), tr(ffn2_w_up), ffn2_w_down], axis=1).astype(BF16)
    win_shard, wout_shard = tr(w_in).astype(BF16), w_out.astype(BF16)
    small_shard = _pack([ln_g, ln_b, conv_w])
    w_ffn1, small = _exchange_alone(_Gather([ffn1_shard[0], small_shard]), "gather_first")
    n_ln = n_l * 3 * ln_w
    small = small.reshape(N_DEV, -1)
    unshard = lambda t, width: jnp.moveaxis(t.reshape(N_DEV, n_l, 3, width), 0, 2).reshape(n_l, 3, N_DEV * width)
    ln_g_all = unshard(small[:, :n_ln], ln_w)
    ln_b_all = unshard(small[:, n_ln:2 * n_ln], ln_w)
    conv_all = unshard(small[:, 2 * n_ln:2 * n_ln + n_l * 3 * cw_w], cw_w)
    pool_bd = jnp.zeros((n_l, D_POOL, D_POOL), F32)
    for g in range(len(POOL_WINDOWS)):
        sl = slice(g * POOL_GROUP, (g + 1) * POOL_GROUP)
        pool_bd = pool_bd.at[:, sl, sl].set(pool_w[:, g])
    pool_bd = pool_bd.astype(BF16)
    lnp = lambda arr, l, j: arr[l, j].reshape(1, d)

    saved = []
    h = x.reshape(s, d)
    for l in range(n_l):
        a1, u1, z1, x1, w_in_l, w_out_l, w_ffn2 = _ffn_fwd(
            h, w_ffn1, lnp(ln_g_all, l, 0), lnp(ln_b_all, l, 0), alpha, f"ffn1_fwd_{l}",
            job=_Gather([win_shard[l], wout_shard[l], ffn2_shard[l]]))
        proj = _win_fwd(x1, w_in_l, f"win_fwd_{l}")
        bias = _na_tiles(rpb[l])
        yab = _local_fwd(proj[0], pool_bd[l], pool_scale[l].reshape(1, D_POOL), conv_all[l], f"local_fwd_{l}")
        yc, lse = _attn_fwd(proj[1], bias, f"attn_fwd_{l}")
        z2, x2 = _wout_fwd(x1, yab, yc, w_out_l, lnp(ln_g_all, l, 1), lnp(ln_b_all, l, 1), alpha, f"wout_fwd_{l}")
        a2, u2, z3, x3, *w_next = _ffn_fwd(
            x2, w_ffn2, lnp(ln_g_all, l, 2), lnp(ln_b_all, l, 2), alpha, f"ffn2_fwd_{l}",
            job=_Gather([ffn1_shard[l + 1]]) if l + 1 < n_l else None)
        saved.append((h, a1, u1, z1, x1, proj, bias, yab, yc, lse, z2, x2, a2, u2, z3, w_ffn1, w_in_l, w_out_l, w_ffn2))
        h = x3
        if w_next:
            w_ffn1 = w_next[0]

    sq, dh = _loss_grad(h, loss_target.reshape(s, d), "loss_head")
    loss = lax.psum(sq[0, 0] * (0.5 / d), MESH_AXES)

    def pair_sum(blocks, tag):
        flat = [b.reshape(N_DEV, -1, d) for b in blocks]
        got = _pair_exchange(flat, f"grads_pair_exchange_{tag}")
        return [_pair_add(b, g, core, f"grads_pair_add_{tag}_{i}") for i, (b, g) in enumerate(zip(flat, got))]

    small_grads = [None] * n_l
    reduced = [None] * n_l
    waiting = None
    for l in reversed(range(n_l)):
        x0, a1, u1, z1, x1, proj, bias, yab, yc, lse, z2, x2, a2, u2, z3, w_ffn1, w_in_l, w_out_l, w_ffn2 = saved[l]
        dx2, da, du, df, dg3, db3, *crossed = _ffn_bwd_dx(
            dh, z3, a2, u2, w_ffn2, lnp(ln_g_all, l, 2), alpha, f"ffn2_bwd_dx_{l}",
            job=_ChipExchange(waiting) if waiting else None)
        if waiting:
            reduced[l + 1] += crossed
        g2 = _ffn_bwd_dwd(a2, u2, df, _ffn_bwd_dwgu(da, du, x2, fs, f"ffn2_bwd_dwgu_{l}"), f"ffn2_bwd_dwd_{l}")
        p2 = pair_sum([g2], f"ffn2_{l}")
        dmix, dxp, dg2, db2, g_out = _wout_bwd(dx2, z2, yab, yc, w_out_l, lnp(ln_g_all, l, 1), alpha, f"wout_bwd_{l}")
        dq, dk, dv, dtiles, *crossed = _attn_bwd(proj[1], bias, yc, dmix, lse, f"attn_bwd_{l}", job=_ChipExchange(p2))
        reduced[l] = list(crossed)
        dloc, dpw, dsc, dcw = _local_bwd(proj[0], dmix, pool_bd[l], pool_scale[l].reshape(1, D_POOL), conv_all[l],
                                         f"local_bwd_{l}")
        dx1, g_in = _win_bwd(dxp, dloc, dq, dk, dv, x1, w_in_l, f"win_bwd_{l}")
        dx0, da, du, df, dg1, db1 = _ffn_bwd_dx(dx1, z1, a1, u1, w_ffn1, lnp(ln_g_all, l, 0), alpha, f"ffn1_bwd_dx_{l}")
        g1 = _ffn_bwd_dwd(a1, u1, df, _ffn_bwd_dwgu(da, du, x0, fs, f"ffn1_bwd_dwgu_{l}"), f"ffn1_bwd_dwd_{l}")
        waiting = pair_sum([g_out, g_in, g1], f"mix_{l}")
        drpb = _rpb_finish(dtiles, f"rpb_finish_{l}")
        dpool = jnp.stack([dpw[g * POOL_GROUP:(g + 1) * POOL_GROUP, g * POOL_GROUP:(g + 1) * POOL_GROUP]
                           for g in range(len(POOL_WINDOWS))])
        small_grads[l] = (jnp.concatenate([dg1, dg2, dg3]), jnp.concatenate([db1, db2, db3]), dcw[0:3], dpool, dsc[0], drpb)
        dh = dx0
    grad_x = dh.reshape(x.shape)

    reduced[0] += _exchange_alone(_ChipExchange(waiting), "grads_chip_exchange_last")
    sums = [[_sum_blocks(q, f"grads_chip_sum_{l}_{i}") for i, q in enumerate(reduced[l])] for l in range(n_l)]
    r_ffn2, r_out, r_in, r_ffn1 = [jnp.stack([sums[l][i] for l in range(n_l)]) for i in range(4)]
    r_ffn1, r_ffn2 = r_ffn1.reshape(n_l, 3, fs, d), r_ffn2.reshape(n_l, 3, fs, d)
    grads = {
        "ffn1_w_gate": tr(r_ffn1[:, 0]), "ffn1_w_up": tr(r_ffn1[:, 1]), "ffn1_w_down": r_ffn1[:, 2],
        "ffn2_w_gate": tr(r_ffn2[:, 0]), "ffn2_w_up": tr(r_ffn2[:, 1]), "ffn2_w_down": r_ffn2[:, 2],
        "w_in": tr(r_in), "w_out": r_out}

    stack = lambda k: jnp.stack([small_grads[l][k] for l in range(n_l)])
    small_shapes = [(n_l, 3, d), (n_l, 3, d), (n_l, 3, D_CONV), pool_w.shape, pool_scale.shape, rpb.shape]
    (small_all,) = _exchange_alone(_Gather([_pack([stack(k) for k in range(6)])]), "gather_small_grads")
    small_sum = _sum_blocks(small_all, "small_grads_sum")
    g_ln_g, g_ln_b, g_conv, g_pool_w, g_pool_scale, g_rpb = _unpack(small_sum, small_shapes)
    own = lambda t, width: lax.dynamic_slice_in_dim(t, me * width, width, axis=2)
    grads.update({"ln_g": own(g_ln_g, ln_w), "ln_b": own(g_ln_b, ln_w), "conv_w": own(g_conv, cw_w),
                  "pool_w": g_pool_w, "pool_scale": g_pool_scale, "rpb": g_rpb})

    weights = dict(ffn1_w_gate=ffn1_w_gate, ffn1_w_up=ffn1_w_up, ffn1_w_down=ffn1_w_down, ffn2_w_gate=ffn2_w_gate,
                   ffn2_w_up=ffn2_w_up, ffn2_w_down=ffn2_w_down, w_in=w_in, pool_w=pool_w, pool_scale=pool_scale,
                   conv_w=conv_w, rpb=rpb, w_out=w_out, ln_g=ln_g, ln_b=ln_b)
    m_in = dict(ffn1_w_gate=m_ffn1_w_gate, ffn1_w_up=m_ffn1_w_up, ffn1_w_down=m_ffn1_w_down, ffn2_w_gate=m_ffn2_w_gate,
                ffn2_w_up=m_ffn2_w_up, ffn2_w_down=m_ffn2_w_down, w_in=m_w_in, pool_w=m_pool_w, pool_scale=m_pool_scale,
                conv_w=m_conv_w, rpb=m_rpb, w_out=m_w_out, ln_g=m_ln_g, ln_b=m_ln_b)
    v_in = dict(ffn1_w_gate=v_ffn1_w_gate, ffn1_w_up=v_ffn1_w_up, ffn1_w_down=v_ffn1_w_down, ffn2_w_gate=v_ffn2_w_gate,
                ffn2_w_up=v_ffn2_w_up, ffn2_w_down=v_ffn2_w_down, w_in=v_w_in, pool_w=v_pool_w, pool_scale=v_pool_scale,
                conv_w=v_conv_w, rpb=v_rpb, w_out=v_w_out, ln_g=v_ln_g, ln_b=v_ln_b)
    names = list(weights)
    large = ["ffn1_w_gate", "ffn1_w_up", "ffn1_w_down", "ffn2_w_gate", "ffn2_w_up", "ffn2_w_down", "w_in", "w_out"]
    tiny = [n for n in names if n not in large]
    delta, new_m, new_v = {}, {}, {}
    for n in large:
        delta[n], new_m[n], new_v[n] = _adamw_nd(weights[n], grads[n], m_in[n], v_in[n], f"adamw_{n}")
    packed = [_pack([t[n] for n in tiny]) for t in (weights, grads, m_in, v_in)]
    tiny_out = _adamw(*packed, "adamw_small")
    tiny_shapes = [weights[n].shape for n in tiny]
    for res, t in zip((delta, new_m, new_v), tiny_out):
        res.update(dict(zip(tiny, _unpack(t, tiny_shapes))))

    return (loss, grad_x, *[grads[n] for n in names], *[delta[n] for n in names],
            *[new_m[n] for n in names], *[new_v[n] for n in names])
```

```python
import functools

import numpy as np
import jax
import jax.numpy as jnp
from jax import lax
from jax.experimental import pallas as pl
from jax.experimental.pallas import tpu as pltpu

F32, BF16 = jnp.float32, jnp.bfloat16
MESH = pl.DeviceIdType.MESH
N_DEV = 8
MESH_AXES = ("x", "y", "c")

LN_EPS = 1e-5
NEG_INF = -1e30
D_POOL = 256
POOL_WINDOWS = (2, 4, 8, 16)
POOL_GROUP = 64
D_CONV = 256
NA_HEADS = 8
NA_HEAD_DIM = 64
D_NA = NA_HEADS * NA_HEAD_DIM
GRID_W = 64
NA_ROWS = 8
NA_COLS = 16
D_LOC = D_POOL + 3 * D_CONV
D_MIX = D_POOL + D_CONV + D_NA
ADAM_LR, ADAM_B1, ADAM_B2, ADAM_EPS, ADAM_WD, ADAM_STEP = 0.001, 0.9, 0.999, 1e-08, 0.01, 10

VMEM_LIMIT_BYTES = 56 * 1024 * 1024
LANES = 128
BF16_ROWS = 16
HALO = 16
Q_ROWS = 8
K_ROWS = 16
Q_TOK = Q_ROWS * GRID_W
K_TOK = K_ROWS * GRID_W
K_BLK = 4 * GRID_W
HEAD_PAIR = 2 * NA_HEAD_DIM
FFN_CHUNK_DEVS = 4

NT = (((1,), (1,)), ((), ()))
TN = (((0,), (0,)), ((), ()))


def _dot(a, b):
    return jnp.dot(a, b, preferred_element_type=F32)


def _dot_nt(a, b):
    return lax.dot_general(a, b, NT, preferred_element_type=F32)


def _dot_tn(a, b):
    return lax.dot_general(a, b, TN, preferred_element_type=F32)


def _params():
    return pltpu.CompilerParams(vmem_limit_bytes=VMEM_LIMIT_BYTES)


def _row_tile(rows, pref, mult=BF16_ROWS):
    t = min(rows, pref)
    t -= t % mult
    while t > mult and rows % t:
        t -= mult
    assert t > 0 and rows % t == 0, (rows, pref)
    return t


def _mesh_pos():
    return tuple(lax.axis_index(a) for a in MESH_AXES)


def _any_spec():
    return pl.BlockSpec(memory_space=pl.ANY)


class _Gather:
    def __init__(self, shards):
        self.arrays = list(shards)
        n = len(shards)
        self.out_shape = [jax.ShapeDtypeStruct((N_DEV,) + s.shape, s.dtype) for s in shards]
        self.scratch = [pltpu.SemaphoreType.DMA((n, 7)), pltpu.SemaphoreType.DMA((n, 7)), pltpu.SemaphoreType.DMA((n,))]

    def phases(self, ins, outs, sems):
        n = len(ins)
        send_sems, recv_sems, local_sems = sems
        x, y, c = _mesh_pos()
        me, sibling = (x, y, c), (x, y, 1 - c)
        chips = [(1 - x, y), (x, 1 - y), (1 - x, 1 - y)]

        def copy(a, k, block, to, src=None):
            dst = outs[a].at[4 * block[0] + 2 * block[1] + block[2]]
            return pltpu.make_async_remote_copy(
                src_ref=dst if src is None else src, dst_ref=dst,
                send_sem=send_sems.at[a, k], recv_sem=recv_sems.at[a, k],
                device_id=to, device_id_type=MESH)

        def mine():
            return [pltpu.make_async_copy(ins[a], outs[a].at[4 * x + 2 * y + c], local_sems.at[a]) for a in range(n)]

        def first():
            return [cp for a in range(n) for cp in
                    [copy(a, 0, me, sibling, src=ins[a])]
                    + [copy(a, 1 + j, me, (*chip, c), src=ins[a]) for j, chip in enumerate(chips)]]

        def passed():
            return [copy(a, 4 + j, (*chip, c), sibling) for j, chip in enumerate(chips) for a in range(n)]

        def start():
            for cp in mine() + first():
                cp.start()

        def middle():
            for j, chip in enumerate(chips):
                for a in range(n):
                    copy(a, 1 + j, (*chip, c), me).wait_recv()
            for cp in passed():
                cp.start()

        def finish():
            for a in range(n):
                copy(a, 0, sibling, me).wait_recv()
                for j, chip in enumerate(chips):
                    copy(a, 4 + j, (*chip, 1 - c), me).wait_recv()
            for cp in first() + passed():
                cp.wait_send()
            for cp in mine():
                cp.wait()

        return start, middle, finish


class _ChipExchange:
    def __init__(self, parts):
        self.arrays = list(parts)
        n = len(parts)
        self.out_shape = [jax.ShapeDtypeStruct(s.shape, s.dtype) for s in parts]
        self.scratch = [pltpu.SemaphoreType.DMA((n, 3)), pltpu.SemaphoreType.DMA((n, 3)), pltpu.SemaphoreType.DMA((n,))]

    def phases(self, ins, outs, sems):
        n = len(ins)
        send_sems, recv_sems, local_sems = sems
        x, y, c = _mesh_pos()
        my_chip = 2 * x + y
        chips = [(1 - x, y), (x, 1 - y), (1 - x, 1 - y)]

        def own():
            return [pltpu.make_async_copy(ins[a].at[my_chip], outs[a].at[my_chip], local_sems.at[a]) for a in range(n)]

        def copy(a, k, src_chip, dst_chip, to):
            return pltpu.make_async_remote_copy(
                src_ref=ins[a].at[src_chip], dst_ref=outs[a].at[dst_chip],
                send_sem=send_sems.at[a, k], recv_sem=recv_sems.at[a, k],
                device_id=to, device_id_type=MESH)

        def sends():
            return [copy(a, k, 2 * px + py, my_chip, (px, py, c)) for a in range(n) for k, (px, py) in enumerate(chips)]

        def start():
            for cp in own() + sends():
                cp.start()

        def finish():
            for cp in sends():
                cp.wait_send()
            for a in range(n):
                for k, (px, py) in enumerate(chips):
                    copy(a, k, my_chip, 2 * px + py, (px, py, c)).wait_recv()
            for cp in own():
                cp.wait()

        return start, None, finish


def _exchange_alone(job, name):
    n = len(job.arrays)

    def body(*refs):
        for phase in job.phases(refs[:n], refs[n:2 * n], refs[2 * n:]):
            if phase is not None:
                phase()

    return pl.pallas_call(
        body, name=name, out_shape=job.out_shape,
        in_specs=[_any_spec()] * n, out_specs=[_any_spec()] * n, scratch_shapes=job.scratch,
    )(*job.arrays)


def _riding_call(body, job, n_in, n_out, n_steps, step, **kw):
    if job is None:
        return pl.pallas_call(body, **kw)
    n_job, n_sem = len(job.arrays), len(job.scratch)
    assert n_steps >= 3
    kw = dict(kw, in_specs=list(kw["in_specs"]) + [_any_spec()] * n_job,
              out_specs=list(kw["out_specs"]) + [_any_spec()] * n_job,
              out_shape=list(kw["out_shape"]) + job.out_shape,
              scratch_shapes=list(kw.get("scratch_shapes", ())) + job.scratch)

    def riding(*refs):
        ins, job_ins = refs[:n_in], refs[n_in:n_in + n_job]
        outs = refs[n_in + n_job:n_in + n_job + n_out]
        job_outs = refs[n_in + n_job + n_out:n_in + 2 * n_job + n_out]
        scratch = refs[n_in + 2 * n_job + n_out:]
        start, middle, finish = job.phases(job_ins, job_outs, scratch[len(scratch) - n_sem:])
        now = step()
        pl.when(now == 0)(start)
        if middle is not None:
            pl.when(now == (5 * n_steps) // 8)(middle)
        body(*ins, *outs, *scratch[:len(scratch) - n_sem])
        pl.when(now == n_steps - 1)(finish)

    call = pl.pallas_call(riding, **kw)
    return lambda *args: call(*args, *job.arrays)


def _pair_exchange(slabs, name):
    n = len(slabs)

    def body(*refs):
        ins, outs = refs[:n], refs[n:2 * n]
        send_sems, recv_sems = refs[2 * n:]
        x, y, c = _mesh_pos()
        copies = [
            pltpu.make_async_remote_copy(
                src_ref=ins[a].at[2 * j + 1 - c], dst_ref=outs[a].at[j],
                send_sem=send_sems.at[a, j], recv_sem=recv_sems.at[a, j],
                device_id=(x, y, 1 - c), device_id_type=MESH)
            for a in range(n) for j in range(4)]
        for cp in copies:
            cp.start()
        for cp in copies:
            cp.wait_send()
        for cp in copies:
            cp.wait_recv()

    return pl.pallas_call(
        body, name=name,
        out_shape=[jax.ShapeDtypeStruct((4,) + s.shape[1:], s.dtype) for s in slabs],
        in_specs=[_any_spec()] * n, out_specs=[_any_spec()] * n,
        scratch_shapes=[pltpu.SemaphoreType.DMA((n, 4)), pltpu.SemaphoreType.DMA((n, 4))],
    )(*slabs)


def _pair_add(slab, got, core, name):
    _, rows, d = slab.shape
    tr = _row_tile(rows, 1024)

    def body(core_ref, mine_ref, got_ref, out_ref):
        out_ref[...] = (mine_ref[...].astype(F32) + got_ref[...].astype(F32)).astype(out_ref.dtype)

    grid_spec = pltpu.PrefetchScalarGridSpec(
        num_scalar_prefetch=1, grid=(4, rows // tr),
        in_specs=[pl.BlockSpec((1, tr, d), lambda j, r, core_ref: (2 * j + core_ref[0], r, 0)),
                  pl.BlockSpec((1, tr, d), lambda j, r, core_ref: (j, r, 0))],
        out_specs=pl.BlockSpec((1, tr, d), lambda j, r, core_ref: (j, r, 0)))
    return pl.pallas_call(body, name=name, grid_spec=grid_spec,
                          out_shape=jax.ShapeDtypeStruct((4, rows, d), slab.dtype),
                          compiler_params=_params())(core, slab, got)


def _sum_blocks(parts, name):
    k, rows, d = parts.shape
    tr = _row_tile(rows, 512, BF16_ROWS if parts.dtype == BF16 else 8)

    def body(in_ref, out_ref):
        acc = in_ref[0].astype(F32)
        for j in range(1, k):
            acc = acc + in_ref[j].astype(F32)
        out_ref[...] = acc

    return pl.pallas_call(
        body, name=name, grid=(rows // tr,),
        in_specs=[pl.BlockSpec((k, tr, d), lambda r: (0, r, 0))],
        out_specs=pl.BlockSpec((tr, d), lambda r: (r, 0)),
        out_shape=jax.ShapeDtypeStruct((rows, d), F32), compiler_params=_params())(parts)


def _ln_stats(z):
    mu = jnp.mean(z, axis=-1, keepdims=True)
    zc = z - mu
    var = jnp.mean(zc * zc, axis=-1, keepdims=True)
    rstd = lax.rsqrt(var + LN_EPS)
    return zc * rstd, rstd


def _ln_bwd(dy, z, g):
    zhat, rstd = _ln_stats(z)
    dyg = dy * g
    m1 = jnp.mean(dyg, axis=-1, keepdims=True)
    m2 = jnp.mean(dyg * zhat, axis=-1, keepdims=True)
    dz = rstd * (dyg - m1 - zhat * m2)
    return dz, jnp.sum(dy * zhat, axis=0, keepdims=True), jnp.sum(dy, axis=0, keepdims=True)


def _accumulate(ref, value, first):
    @pl.when(first)
    def _():
        ref[...] = value

    @pl.when(jnp.logical_not(first))
    def _():
        ref[...] += value


def _add_matmul(acc_ref, first, matmul):
    @pl.when(first)
    def _():
        acc_ref[...] = jnp.zeros_like(acc_ref)

    acc_ref[...] += matmul()


def _ffn_weight_specs(fs, d):
    def spec(row):
        return pl.BlockSpec((N_DEV, 1, fs, d), lambda i: (0, row, 0, 0), pipeline_mode=pl.Buffered(1))
    return [spec(0), spec(1), spec(2)]


def _ffn_fwd(x, w, ln_g, ln_b, alpha, name, job=None):
    s, d = x.shape
    fs = w.shape[2]
    f = N_DEV * fs
    tm = min(s, 256)

    def body(x_ref, wg_ref, wu_ref, wd_ref, g_ref, b_ref, a_ref, u_ref, h_ref, z_ref, y_ref):
        xv = x_ref[...]
        xb = xv.astype(BF16)
        a = _dot_nt(xb, wg_ref[...].reshape(f, d))
        u = _dot_nt(xb, wu_ref[...].reshape(f, d))
        a_ref[...] = a.astype(BF16)
        u_ref[...] = u.astype(BF16)
        h = ((a * jax.nn.sigmoid(a)) * u).astype(BF16)
        h_ref[...] = h
        z = alpha * xv + 0.5 * _dot(h, wd_ref[...].reshape(f, d))
        zhat, _ = _ln_stats(z)
        z_ref[...] = z
        y_ref[...] = zhat * g_ref[...] + b_ref[...]

    row = pl.BlockSpec((tm, d), lambda i: (i, 0))
    vec = pl.BlockSpec((1, d), lambda i: (0, 0))
    hid = pl.BlockSpec((tm, f), lambda i: (i, 0))
    call = _riding_call(
        body, job, 6, 5, s // tm, lambda: pl.program_id(0),
        name=name, grid=(s // tm,),
        in_specs=[row] + _ffn_weight_specs(fs, d) + [vec, vec],
        out_specs=[hid, hid, hid, row, row],
        out_shape=[jax.ShapeDtypeStruct((s, f), BF16)] * 3 + [jax.ShapeDtypeStruct((s, d), F32)] * 2,
        compiler_params=_params())
    return call(x, w, w, w, ln_g, ln_b)


def _ffn_bwd_dx(dy, z, a, u, w, ln_g, alpha, name, job=None):
    s, d = dy.shape
    fs = w.shape[2]
    f = N_DEV * fs
    tm = min(s, 256)

    def body(dy_ref, z_ref, a_ref, u_ref, wg_ref, wu_ref, wd_ref, g_ref,
             dx_ref, da_ref, du_ref, df_ref, dg_ref, db_ref):
        i = pl.program_id(0)
        dz, dg, db = _ln_bwd(dy_ref[...], z_ref[...], g_ref[...])
        _accumulate(dg_ref, dg, i == 0)
        _accumulate(db_ref, db, i == 0)
        df = (0.5 * dz).astype(BF16)
        df_ref[...] = df
        av = a_ref[...].astype(F32)
        uv = u_ref[...].astype(F32)
        sg = jax.nn.sigmoid(av)
        dh = _dot_nt(df, wd_ref[...].reshape(f, d))
        du = (dh * (av * sg)).astype(BF16)
        da = (dh * uv * (sg * (1.0 + av * (1.0 - sg)))).astype(BF16)
        da_ref[...] = da
        du_ref[...] = du
        dx_ref[...] = alpha * dz + _dot(da, wg_ref[...].reshape(f, d)) + _dot(du, wu_ref[...].reshape(f, d))

    row = pl.BlockSpec((tm, d), lambda i: (i, 0))
    vec = pl.BlockSpec((1, d), lambda i: (0, 0))
    hid = pl.BlockSpec((tm, f), lambda i: (i, 0))
    call = _riding_call(
        body, job, 8, 6, s // tm, lambda: pl.program_id(0),
        name=name, grid=(s // tm,),
        in_specs=[row, row, hid, hid] + _ffn_weight_specs(fs, d) + [vec],
        out_specs=[row, hid, hid, row, vec, vec],
        out_shape=[jax.ShapeDtypeStruct((s, d), F32)] + [jax.ShapeDtypeStruct((s, f), BF16)] * 2
                  + [jax.ShapeDtypeStruct((s, d), BF16)] + [jax.ShapeDtypeStruct((1, d), F32)] * 2,
        compiler_params=_params())
    return call(dy, z, a, u, w, w, w, ln_g)


def _ffn_bwd_dwgu(da, du, x, fs, name):
    s, d = x.shape
    tf = FFN_CHUNK_DEVS * fs
    n_c = N_DEV // FFN_CHUNK_DEVS
    tk = min(s, 1024)
    n_k = s // tk

    def body(da_ref, du_ref, x_ref, out_ref, accg_s, accu_s):
        k = pl.program_id(1)
        xb = x_ref[...].astype(BF16)
        _add_matmul(accg_s, k == 0, lambda: _dot_tn(da_ref[...], xb))
        _add_matmul(accu_s, k == 0, lambda: _dot_tn(du_ref[...], xb))

        @pl.when(k == n_k - 1)
        def _():
            out_ref[:, 0] = accg_s[...].astype(BF16).reshape(FFN_CHUNK_DEVS, fs, d)
            out_ref[:, 1] = accu_s[...].astype(BF16).reshape(FFN_CHUNK_DEVS, fs, d)

    hid = pl.BlockSpec((tk, tf), lambda c, k: (k, c))
    return pl.pallas_call(
        body, name=name, grid=(n_c, n_k),
        in_specs=[hid, hid, pl.BlockSpec((tk, d), lambda c, k: (k, 0))],
        out_specs=pl.BlockSpec((FFN_CHUNK_DEVS, 2, fs, d), lambda c, k: (c, 0, 0, 0), pipeline_mode=pl.Buffered(1)),
        out_shape=jax.ShapeDtypeStruct((N_DEV, 3, fs, d), BF16),
        scratch_shapes=[pltpu.VMEM((tf, d), F32), pltpu.VMEM((tf, d), F32)],
        compiler_params=_params())(da, du, x)


def _ffn_bwd_dwd(h, df, blocks, name):
    s, d = df.shape
    fs = blocks.shape[2]
    tf = FFN_CHUNK_DEVS * fs
    n_c = N_DEV // FFN_CHUNK_DEVS
    tk = min(s, 1024)
    n_k = s // tk

    def body(h_ref, df_ref, blocks_ref, out_ref, acc_s):
        k = pl.program_id(1)
        _add_matmul(acc_s, k == 0, lambda: _dot_tn(h_ref[...], df_ref[...]))

        @pl.when(k == n_k - 1)
        def _():
            out_ref[:, 0] = acc_s[...].astype(BF16).reshape(FFN_CHUNK_DEVS, fs, d)

    return pl.pallas_call(
        body, name=name, grid=(n_c, n_k),
        in_specs=[pl.BlockSpec((tk, tf), lambda c, k: (k, c)), pl.BlockSpec((tk, d), lambda c, k: (k, 0)), _any_spec()],
        out_specs=pl.BlockSpec((FFN_CHUNK_DEVS, 1, fs, d), lambda c, k: (c, 2, 0, 0), pipeline_mode=pl.Buffered(1)),
        out_shape=jax.ShapeDtypeStruct(blocks.shape, BF16), input_output_aliases={2: 0},
        scratch_shapes=[pltpu.VMEM((tf, d), F32)],
        compiler_params=_params())(h, df, blocks)


def _whole(arr):
    return pl.BlockSpec(arr.shape, lambda i: (0,) * arr.ndim)


def _win_fwd(x, w_in, name):
    s, d = x.shape
    d_in = N_DEV * w_in.shape[1]
    tm = min(s, 512)
    scale = NA_HEAD_DIM ** -0.5
    assert d_in == D_LOC + 3 * D_NA and scale == 0.125

    def body(x_ref, w_ref, loc_ref, qkv_ref):
        proj = _dot_nt(x_ref[...].astype(BF16), w_ref[...].reshape(d_in, d))
        loc_ref[...] = proj[:, :D_LOC]
        qkv_ref[:, :D_NA] = (proj[:, D_LOC:D_LOC + D_NA] * scale).astype(BF16)
        qkv_ref[:, D_NA:] = proj[:, D_LOC + D_NA:].astype(BF16)

    return pl.pallas_call(
        body, name=name, grid=(s // tm,),
        in_specs=[pl.BlockSpec((tm, d), lambda i: (i, 0)), _whole(w_in)],
        out_specs=[pl.BlockSpec((tm, D_LOC), lambda i: (i, 0)), pl.BlockSpec((tm, 3 * D_NA), lambda i: (i, 0))],
        out_shape=[jax.ShapeDtypeStruct((s, D_LOC), F32), jax.ShapeDtypeStruct((s, 3 * D_NA), BF16)],
        compiler_params=_params())(x, w_in)


def _wout_fwd(x, yab, yc, w_out, ln_g, ln_b, alpha, name):
    s, d = x.shape
    tm = min(s, 512)

    def body(x_ref, yab_ref, yc_ref, w_ref, g_ref, b_ref, z_ref, y_ref):
        mix = jnp.concatenate([yab_ref[...], yc_ref[...]], axis=1).astype(BF16)
        z = alpha * x_ref[...] + _dot(mix, w_ref[...].reshape(D_MIX, d))
        zhat, _ = _ln_stats(z)
        z_ref[...] = z
        y_ref[...] = zhat * g_ref[...] + b_ref[...]

    row = pl.BlockSpec((tm, d), lambda i: (i, 0))
    half = pl.BlockSpec((tm, D_MIX // 2), lambda i: (i, 0))
    vec = pl.BlockSpec((1, d), lambda i: (0, 0))
    return pl.pallas_call(
        body, name=name, grid=(s // tm,),
        in_specs=[row, half, half, _whole(w_out), vec, vec],
        out_specs=[row, row], out_shape=[jax.ShapeDtypeStruct((s, d), F32)] * 2,
        compiler_params=_params())(x, yab, yc, w_out, ln_g, ln_b)


def _wout_bwd(dy, z, yab, yc, w_out, ln_g, alpha, name):
    s, d = dy.shape
    rs = w_out.shape[1]
    tm = min(s, 512)
    n_i = s // tm

    def body(dy_ref, z_ref, yab_ref, yc_ref, w_ref, g_ref, dmix_ref, dxp_ref, dg_ref, db_ref, out_ref, acc_s):
        i = pl.program_id(0)
        dz, dg, db = _ln_bwd(dy_ref[...], z_ref[...], g_ref[...])
        _accumulate(dg_ref, dg, i == 0)
        _accumulate(db_ref, db, i == 0)
        dxp_ref[...] = alpha * dz
        dzb = dz.astype(BF16)
        dmix_ref[...] = _dot_nt(dzb, w_ref[...].reshape(D_MIX, d))
        mix = jnp.concatenate([yab_ref[...], yc_ref[...]], axis=1).astype(BF16)
        _add_matmul(acc_s, i == 0, lambda: _dot_tn(mix, dzb))

        @pl.when(i == n_i - 1)
        def _():
            out_ref[...] = acc_s[...].astype(BF16).reshape(N_DEV, rs, d)

    row = pl.BlockSpec((tm, d), lambda i: (i, 0))
    half = pl.BlockSpec((tm, D_MIX // 2), lambda i: (i, 0))
    vec = pl.BlockSpec((1, d), lambda i: (0, 0))
    return pl.pallas_call(
        body, name=name, grid=(n_i,),
        in_specs=[row, row, half, half, _whole(w_out), vec],
        out_specs=[pl.BlockSpec((tm, D_MIX), lambda i: (i, 0)), row, vec, vec, _whole(w_out)],
        out_shape=[jax.ShapeDtypeStruct((s, D_MIX), F32), jax.ShapeDtypeStruct((s, d), F32),
                   jax.ShapeDtypeStruct((1, d), F32), jax.ShapeDtypeStruct((1, d), F32),
                   jax.ShapeDtypeStruct(w_out.shape, BF16)],
        scratch_shapes=[pltpu.VMEM((D_MIX, d), F32)],
        compiler_params=_params())(dy, z, yab, yc, w_out, ln_g)


def _win_bwd(dxp, dloc, dq, dk, dv, x, w_in, name):
    s, d = x.shape
    rs = w_in.shape[1]
    d_in = N_DEV * rs
    tm = min(s, 256)
    n_i = s // tm

    def body(dxp_ref, dloc_ref, dq_ref, dk_ref, dv_ref, x_ref, w_ref, dx_ref, out_ref, acc_s):
        i = pl.program_id(0)
        dp = jnp.concatenate([dloc_ref[...], dq_ref[...], dk_ref[...].astype(BF16), dv_ref[...].astype(BF16)], axis=1)
        dx_ref[...] = dxp_ref[...] + _dot(dp, w_ref[...].reshape(d_in, d))
        _add_matmul(acc_s, i == 0, lambda: _dot_tn(dp, x_ref[...].astype(BF16)))

        @pl.when(i == n_i - 1)
        def _():
            out_ref[...] = acc_s[...].astype(BF16).reshape(N_DEV, rs, d)

    row = pl.BlockSpec((tm, d), lambda i: (i, 0))
    na = pl.BlockSpec((tm, D_NA), lambda i: (i, 0))
    return pl.pallas_call(
        body, name=name, grid=(n_i,),
        in_specs=[row, pl.BlockSpec((tm, D_LOC), lambda i: (i, 0)), na, na, na, row, _whole(w_in)],
        out_specs=[row, _whole(w_in)],
        out_shape=[jax.ShapeDtypeStruct((s, d), F32), jax.ShapeDtypeStruct(w_in.shape, BF16)],
        scratch_shapes=[pltpu.VMEM((d_in, d), F32)],
        compiler_params=_params())(dxp, dloc, dq, dk, dv, x, w_in)


def _shift_rows(v, k):
    n = v.shape[0]
    return pltpu.roll(v, k % n, 0)


def _halo_specs(tm, s, width, col):
    per = tm // HALO
    last = s // HALO - 1
    return [pl.BlockSpec((HALO, width), lambda i: (jnp.maximum(i * per - 1, 0), col)),
            pl.BlockSpec((tm, width), lambda i: (i, col)),
            pl.BlockSpec((HALO, width), lambda i: (jnp.minimum((i + 1) * per, last), col))]


def _token_index(i, tm):
    return i * tm - HALO + lax.broadcasted_iota(jnp.int32, (tm + 2 * HALO, 1), 0)


def _pool_lane_tables():
    lane = lax.broadcasted_iota(jnp.int32, (1, D_POOL), 1)
    group = sum((lane >= g * POOL_GROUP).astype(jnp.int32) for g in range(1, len(POOL_WINDOWS)))
    half = jnp.where(group == 0, 1, jnp.where(group == 1, 2, jnp.where(group == 2, 4, 8)))
    return group, half


def _window_sums(v, group, offsets):
    s2 = v + _shift_rows(v, 1)
    s4 = s2 + _shift_rows(s2, 2)
    s8 = s4 + _shift_rows(s4, 4)
    s16 = s8 + _shift_rows(s8, 8)
    parts = [_shift_rows(p, -o) if o else p for p, o in zip((s2, s4, s8, s16), offsets)]
    return jnp.where(group == 0, parts[0], jnp.where(group == 1, parts[1], jnp.where(group == 2, parts[2], parts[3])))


def _pool_counts(tok, half, s):
    return (jnp.minimum(tok + half, s) - jnp.maximum(tok - half, 0)).astype(F32)


def _pool_forward(u, tok, s):
    group, half = _pool_lane_tables()
    sums = _window_sums(u, group, [w // 2 - 1 for w in POOL_WINDOWS])
    return sums / _pool_counts(tok, half, s) - u


def _conv_forward(zc, cw_ref):
    return cw_ref[0:1, :] * _shift_rows(zc, 1) + cw_ref[1:2, :] * zc + cw_ref[2:3, :] * _shift_rows(zc, -1)


def _local_fwd(proj, pool_bd, pool_scale, conv_w, name):
    s = proj.shape[0]
    tm = min(s, 512)
    ctr = slice(HALO, HALO + tm)

    def body(prev_ref, cur_ref, next_ref, pw_ref, sc_ref, cw_ref, out_ref):
        i = pl.program_id(0)
        ext = jnp.concatenate([prev_ref[...], cur_ref[...], next_ref[...]], axis=0)
        tok = _token_index(i, tm)
        inside = (tok >= 0) & (tok < s)
        u = jnp.where(inside, ext[:, 0:D_POOL], 0.0)
        p = _pool_forward(u, tok, s)[ctr]
        ya = _dot(p.astype(BF16), pw_ref[...]) * sc_ref[...]
        gb = ext[:, D_POOL:D_POOL + D_CONV]
        zc = jnp.where(inside, ext[:, D_POOL + D_CONV:D_POOL + 2 * D_CONV] * ext[:, D_POOL + 2 * D_CONV:D_LOC], 0.0)
        yb = (gb * _conv_forward(zc, cw_ref))[ctr]
        out_ref[...] = jnp.concatenate([ya, yb], axis=1)

    return pl.pallas_call(
        body, name=name, grid=(s // tm,),
        in_specs=_halo_specs(tm, s, D_LOC, 0) + [
            pl.BlockSpec((D_POOL, D_POOL), lambda i: (0, 0)), pl.BlockSpec((1, D_POOL), lambda i: (0, 0)),
            pl.BlockSpec((3, D_CONV), lambda i: (0, 0))],
        out_specs=pl.BlockSpec((tm, D_POOL + D_CONV), lambda i: (i, 0)),
        out_shape=jax.ShapeDtypeStruct((s, D_POOL + D_CONV), F32),
        compiler_params=_params())(proj, proj, proj, pool_bd, pool_scale, conv_w)


def _local_bwd(proj, dmix, pool_bd, pool_scale, conv_w, name):
    s = proj.shape[0]
    tm = min(s, 512)
    ctr = slice(HALO, HALO + tm)

    def body(prev_ref, cur_ref, next_ref, dprev_ref, dcur_ref, dnext_ref, pw_ref, sc_ref, cw_ref,
             dloc_ref, dpw_ref, dsc_ref, dcw_ref):
        i = pl.program_id(0)
        first = i == 0
        ext = jnp.concatenate([prev_ref[...], cur_ref[...], next_ref[...]], axis=0)
        dext = jnp.concatenate([dprev_ref[...], dcur_ref[...], dnext_ref[...]], axis=0)
        tok = _token_index(i, tm)
        inside = (tok >= 0) & (tok < s)
        group, half = _pool_lane_tables()
        cnt = _pool_counts(tok, half, s)
        u = jnp.where(inside, ext[:, 0:D_POOL], 0.0)
        dya = jnp.where(inside, dext[:, 0:D_POOL], 0.0)
        p_c = _pool_forward(u, tok, s)[ctr].astype(BF16)
        lin = _dot(p_c, pw_ref[...])
        _accumulate(dsc_ref, jnp.sum(dya[ctr] * lin, axis=0, keepdims=True), first)
        e1 = (dya * sc_ref[...]).astype(BF16)
        _accumulate(dpw_ref, _dot_tn(p_c, e1[ctr]), first)
        dp = _dot_nt(e1, pw_ref[...])
        du = _window_sums(dp / cnt, group, [w // 2 for w in POOL_WINDOWS]) - dp
        gb = ext[:, D_POOL:D_POOL + D_CONV]
        gc = ext[:, D_POOL + D_CONV:D_POOL + 2 * D_CONV]
        hv = ext[:, D_POOL + 2 * D_CONV:D_LOC]
        zc = jnp.where(inside, gc * hv, 0.0)
        dyb = jnp.where(inside, dext[:, D_POOL:D_POOL + D_CONV], 0.0)
        dgb = dyb * _conv_forward(zc, cw_ref)
        dyc = dyb * gb
        for k in range(3):
            part = jnp.sum(dyc[ctr] * _shift_rows(zc, 1 - k)[ctr], axis=0, keepdims=True)
            _accumulate(dcw_ref.at[k:k + 1, :], part, first)
        dzc = cw_ref[0:1, :] * _shift_rows(dyc, -1) + cw_ref[1:2, :] * dyc + cw_ref[2:3, :] * _shift_rows(dyc, 1)
        dloc = jnp.concatenate([du, dgb, dzc * hv, dzc * gc], axis=1)
        dloc_ref[...] = dloc[ctr].astype(BF16)

    return pl.pallas_call(
        body, name=name, grid=(s // tm,),
        in_specs=_halo_specs(tm, s, D_LOC, 0) + _halo_specs(tm, s, D_POOL + D_CONV, 0) + [
            pl.BlockSpec((D_POOL, D_POOL), lambda i: (0, 0)), pl.BlockSpec((1, D_POOL), lambda i: (0, 0)),
            pl.BlockSpec((3, D_CONV), lambda i: (0, 0))],
        out_specs=[pl.BlockSpec((tm, D_LOC), lambda i: (i, 0)), pl.BlockSpec((D_POOL, D_POOL), lambda i: (0, 0)),
                   pl.BlockSpec((1, D_POOL), lambda i: (0, 0)), pl.BlockSpec((8, D_CONV), lambda i: (0, 0))],
        out_shape=[jax.ShapeDtypeStruct((s, D_LOC), BF16), jax.ShapeDtypeStruct((D_POOL, D_POOL), F32),
                   jax.ShapeDtypeStruct((1, D_POOL), F32), jax.ShapeDtypeStruct((8, D_CONV), F32)],
        compiler_params=_params())(proj, proj, proj, dmix, dmix, dmix, pool_bd, pool_scale, conv_w)


def _na_geometry(rows):
    n_j = rows // Q_ROWS
    dr = np.full((3, Q_ROWS, K_ROWS), 2 * NA_ROWS - 1, np.int64)
    for t, j in enumerate((0, min(1, n_j - 1), n_j - 1)):
        base = int(np.clip(Q_ROWS * j - NA_ROWS // 2, 0, rows - K_ROWS))
        for qr in range(Q_ROWS):
            r = Q_ROWS * j + qr
            start = int(np.clip(r - NA_ROWS // 2, 0, rows - NA_ROWS))
            for kr in range(K_ROWS):
                if start <= base + kr < start + NA_ROWS:
                    dr[t, qr, kr] = base + kr - r + NA_ROWS - 1
    return dr


def _na_col_tables():
    c = np.arange(GRID_W)
    start = np.clip(c - NA_COLS // 2, 0, GRID_W - NA_COLS)
    valid = (c[None, :] >= start[:, None]) & (c[None, :] < start[:, None] + NA_COLS)
    dc = np.clip(c[None, :] - c[:, None], -(NA_COLS - 1), NA_COLS - 1) + (NA_COLS - 1)
    return valid, dc


NO_ROW = 2 * NA_ROWS - 1
N_SLOT = 2 * NA_ROWS


def _na_tiles(rpb):
    valid, dc = _na_col_tables()
    onehot = jnp.asarray((dc[None] == np.arange(2 * NA_COLS - 1)[:, None, None]).astype(np.float32))
    table = jnp.einsum("hrd,dqk->hrqk", rpb, onehot, precision=lax.Precision.HIGHEST)
    table = jnp.where(jnp.asarray(valid)[None, None], table, NEG_INF)
    outside = jnp.full((NA_HEADS, 1, GRID_W, GRID_W), NEG_INF, F32)
    padded = jnp.concatenate([outside, table, outside], axis=1)
    pairs = jnp.concatenate([padded[:, :N_SLOT], padded[:, 1:]], axis=-1)
    return jnp.concatenate([pairs, jnp.full((NA_HEADS, 1, GRID_W, 2 * GRID_W), NEG_INF, F32)], axis=1)


G_ROWS = 2
N_GRP = Q_ROWS // G_ROWS
G_TOK = G_ROWS * GRID_W
GK_ROWS = NA_ROWS + G_ROWS
GK_TOK = GK_ROWS * GRID_W
STACK_TOK = N_GRP * 2 * G_TOK


def _na_group_tables(rows):
    dr = _na_geometry(rows)
    koff = np.zeros((3, N_GRP), np.int64)
    slot = np.zeros((3, N_GRP, G_ROWS, GK_ROWS // 2), np.int64)
    even_in, odd_in = np.zeros_like(slot), np.zeros_like(slot)
    for t in range(3):
        for g in range(N_GRP):
            qrs = range(G_ROWS * g, G_ROWS * (g + 1))
            inside = [kr for kr in range(K_ROWS) if any(dr[t, qr, kr] != NO_ROW for qr in qrs)]
            lo, hi = min(inside), max(inside) + 1
            off = min(lo - lo % 2, K_ROWS - GK_ROWS)
            assert off <= lo and hi <= off + GK_ROWS
            koff[t, g] = off
            for qq, qr in enumerate(qrs):
                for kp in range(GK_ROWS // 2):
                    even, odd = int(dr[t, qr, off + 2 * kp]), int(dr[t, qr, off + 2 * kp + 1])
                    even_in[t, g, qq, kp], odd_in[t, g, qq, kp] = even != NO_ROW, odd != NO_ROW
                    slot[t, g, qq, kp] = (N_SLOT if even == NO_ROW and odd == NO_ROW
                                          else (even if even != NO_ROW else odd - 1) + 1)
    return koff, slot, even_in, odd_in


def _by_type(block_type, per_type):
    a, b, c = (int(v) for v in per_type)
    if a == b == c:
        return a
    return jnp.where(block_type == 0, a, jnp.where(block_type == 2, c, b))


def _score_rows(g, hh):
    first = (2 * g + hh) * G_TOK
    return slice(first, first + G_TOK)


def _tile_at(g, hh, qq, kp):
    first = _score_rows(g, hh).start + qq * GRID_W
    return slice(first, first + GRID_W), slice(kp * 2 * GRID_W, (kp + 1) * 2 * GRID_W)


def _fill_bias(bias_s, tiles_ref, block_type, tables):
    _, slot, even_in, odd_in = tables
    left = lax.broadcasted_iota(jnp.int32, (1, 2 * GRID_W), 1) < GRID_W
    for hh in range(2):
        for g in range(N_GRP):
            for qq in range(G_ROWS):
                for kp in range(GK_ROWS // 2):
                    tile = tiles_ref[hh, _by_type(block_type, slot[:, g, qq, kp])]
                    tile = jnp.where(left & (_by_type(block_type, even_in[:, g, qq, kp]) == 0), NEG_INF, tile)
                    tile = jnp.where(jnp.logical_not(left) & (_by_type(block_type, odd_in[:, g, qq, kp]) == 0), NEG_INF, tile)
                    rs, cs = _tile_at(g, hh, qq, kp)
                    bias_s[rs, cs] = tile


def _group_offset(block_type, koff, g):
    off = _by_type(block_type, koff[:, g]) * GRID_W
    return off if isinstance(off, int) else pl.multiple_of(off, 2 * GRID_W)


def _na_specs(s, proj_cols):
    n_blk = s // K_BLK
    per = Q_TOK // K_BLK

    def kv_spec(col0, m):
        return pl.BlockSpec((K_BLK, HEAD_PAIR), lambda hp, j: (jnp.clip(per * j - 1, 0, n_blk - 4) + m, col0 + hp))

    q_col, k_col, v_col = (c // HEAD_PAIR for c in proj_cols)
    return ([pl.BlockSpec((Q_TOK, HEAD_PAIR), lambda hp, j: (j, q_col + hp))]
            + [kv_spec(k_col, m) for m in range(4)] + [kv_spec(v_col, m) for m in range(4)])


def _na_block_type(j, n_j):
    return jnp.where(j == 0, 0, jnp.where(j == n_j - 1, 2, 1))


def _head_masks():
    lane = lax.broadcasted_iota(jnp.int32, (1, HEAD_PAIR), 1)
    return [lane < NA_HEAD_DIM, lane >= NA_HEAD_DIM]


def _attn_fwd(qkv, tiles, name):
    s = qkv.shape[0]
    n_j = s // Q_TOK
    tables = _na_group_tables(s // GRID_W)
    koff = tables[0]

    def body(q_ref, k0, k1, k2, k3, v0, v1, v2, v3, tiles_ref, o_ref, lse_ref, bias_s, k_s, vh_s, sc_s, p_s):
        j = pl.program_id(1)
        block_type = _na_block_type(j, n_j)
        pl.when((j == 0) | (j == 1) | (j == n_j - 1))(functools.partial(_fill_bias, bias_s, tiles_ref, block_type, tables))
        masks = _head_masks()
        for m, (kr, vr) in enumerate(zip((k0, k1, k2, k3), (v0, v1, v2, v3))):
            rows = slice(m * K_BLK, (m + 1) * K_BLK)
            k_s[rows, :] = kr[...]
            v = vr[...]
            for hh, mask in enumerate(masks):
                vh_s[hh, rows, :] = jnp.where(mask, v, jnp.zeros_like(v))
        q = q_ref[...]
        qh = [jnp.where(mask, q, jnp.zeros_like(q)) for mask in masks]
        offs = [_group_offset(block_type, koff, g) for g in range(N_GRP)]
        for g in range(N_GRP):
            kg = k_s[pl.ds(offs[g], GK_TOK), :]
            for hh in range(2):
                sc_s[_score_rows(g, hh), :] = _dot_nt(qh[hh][g * G_TOK:(g + 1) * G_TOK], kg)
        sc = sc_s[...] + bias_s[...]
        mx = jnp.max(sc, axis=-1, keepdims=True)
        p = jnp.exp(sc - mx)
        den = jnp.sum(p, axis=-1, keepdims=True)
        p_s[...] = p.astype(BF16)
        inv = 1.0 / den
        lse = mx + jnp.log(den)
        for g in range(N_GRP):
            rows = slice(g * G_TOK, (g + 1) * G_TOK)
            out = jnp.zeros((G_TOK, HEAD_PAIR), F32)
            for hh in range(2):
                sr = _score_rows(g, hh)
                out = out + _dot(p_s[sr, :], vh_s[hh, pl.ds(offs[g], GK_TOK), :]) * inv[sr]
            o_ref[rows, :] = out
            lse_ref[0, rows, :] = jnp.where(masks[0], lse[_score_rows(g, 0)], lse[_score_rows(g, 1)])

    return pl.pallas_call(
        body, name=name, grid=(NA_HEADS // 2, n_j),
        in_specs=_na_specs(s, (0, D_NA, 2 * D_NA)) + [
            pl.BlockSpec((2, N_SLOT + 1, GRID_W, 2 * GRID_W), lambda hp, j: (hp, 0, 0, 0))],
        out_specs=[pl.BlockSpec((Q_TOK, HEAD_PAIR), lambda hp, j: (j, hp)),
                   pl.BlockSpec((1, Q_TOK, HEAD_PAIR), lambda hp, j: (hp, j, 0))],
        out_shape=[jax.ShapeDtypeStruct((s, D_NA), F32), jax.ShapeDtypeStruct((NA_HEADS // 2, s, HEAD_PAIR), F32)],
        scratch_shapes=[pltpu.VMEM((STACK_TOK, GK_TOK), F32), pltpu.VMEM((K_TOK, HEAD_PAIR), BF16),
                        pltpu.VMEM((2, K_TOK, HEAD_PAIR), BF16), pltpu.VMEM((STACK_TOK, GK_TOK), F32),
                        pltpu.VMEM((STACK_TOK, GK_TOK), BF16)],
        compiler_params=_params())(*([qkv] * 9), tiles)


def _add_tiles(dtile_ref, ds_ref, block_type, slot, has_interior):
    def tile(g, hh, qq, kp):
        rs, cs = _tile_at(g, hh, qq, kp)
        return ds_ref[rs, cs].astype(F32)

    def interior():
        for hh in range(2):
            for qq in range(G_ROWS):
                for kp in range(GK_ROWS // 2):
                    assert (slot[1, :, qq, kp] == slot[1, 0, qq, kp]).all()
                    if slot[1, 0, qq, kp] != N_SLOT:
                        dtile_ref[hh, int(slot[1, 0, qq, kp])] += sum(tile(g, hh, qq, kp) for g in range(N_GRP))

    def edge():
        for hh in range(2):
            for g in range(N_GRP):
                for qq in range(G_ROWS):
                    for kp in range(GK_ROWS // 2):
                        first, last = (0 if e == N_SLOT else int(e) for e in slot[[0, 2], g, qq, kp])
                        if (slot[[0, 2], g, qq, kp] != N_SLOT).any():
                            dtile_ref[hh, _by_type(block_type, (first, first, last))] += tile(g, hh, qq, kp)

    if has_interior:
        pl.when(block_type == 1)(interior)
    pl.when(block_type != 1)(edge)


def _attn_bwd(qkv, tiles, o, dmix, lse, name, job=None):
    s = qkv.shape[0]
    n_j = s // Q_TOK
    n_blk = s // K_BLK
    per = Q_TOK // K_BLK
    scale = NA_HEAD_DIM ** -0.5
    do_col = (D_POOL + D_CONV) // HEAD_PAIR
    tables = _na_group_tables(s // GRID_W)
    koff, slot = tables[0], tables[1]

    def body(q_ref, k0, k1, k2, k3, v0, v1, v2, v3, tiles_ref, o_ref, do_ref, lse_ref,
             dq_ref, dk_ref, dv_ref, dtile_ref, bias_s, k_s, kh_s, v_s, s_s, dp_s, pb_s, dsb_s):
        j = pl.program_id(1)

        @pl.when(j == 0)
        def _():
            dk_ref[...] = jnp.zeros_like(dk_ref)
            dv_ref[...] = jnp.zeros_like(dv_ref)
            dtile_ref[...] = jnp.zeros_like(dtile_ref)

        block_type = _na_block_type(j, n_j)
        pl.when((j == 0) | (j == 1) | (j == n_j - 1))(functools.partial(_fill_bias, bias_s, tiles_ref, block_type, tables))
        base = pl.multiple_of(jnp.clip(per * j - 1, 0, n_blk - 4) * K_BLK, K_BLK)
        masks = _head_masks()
        for m, (kr, vr) in enumerate(zip((k0, k1, k2, k3), (v0, v1, v2, v3))):
            rows = slice(m * K_BLK, (m + 1) * K_BLK)
            k = kr[...]
            k_s[rows, :] = k
            v_s[rows, :] = vr[...]
            for hh, mask in enumerate(masks):
                kh_s[hh, rows, :] = jnp.where(mask, k, jnp.zeros_like(k))
        q = q_ref[...]
        qh = [jnp.where(mask, q, jnp.zeros_like(q)) for mask in masks]
        lane = lax.broadcasted_iota(jnp.int32, (1, HEAD_PAIR), 1)
        offs = [_group_offset(block_type, koff, g) for g in range(N_GRP)]
        do, ov, lse = do_ref[...], o_ref[...], lse_ref[0]
        dob, lse_col, delta_col = {}, [], []
        for g in range(N_GRP):
            rows = slice(g * G_TOK, (g + 1) * G_TOK)
            kg = k_s[pl.ds(offs[g], GK_TOK), :]
            vg = v_s[pl.ds(offs[g], GK_TOK), :]
            for hh, mask in enumerate(masks):
                doh = jnp.where(mask, do[rows], 0.0)
                dob[g, hh] = doh.astype(BF16)
                lse_col.append(jnp.sum(jnp.where(lane == hh * NA_HEAD_DIM, lse[rows], 0.0), axis=-1, keepdims=True))
                delta_col.append(jnp.sum(doh * ov[rows], axis=-1, keepdims=True))
                s_s[_score_rows(g, hh), :] = _dot_nt(qh[hh][rows], kg)
                dp_s[_score_rows(g, hh), :] = _dot_nt(dob[g, hh], vg)
        p = jnp.exp(s_s[...] + bias_s[...] - jnp.concatenate(lse_col, axis=0))
        ds = p * (dp_s[...] - jnp.concatenate(delta_col, axis=0))
        pb_s[...] = p.astype(BF16)
        dsb_s[...] = ds.astype(BF16)
        _add_tiles(dtile_ref, dsb_s, block_type, slot, n_j > 2)
        for g in range(N_GRP):
            rows = slice(g * G_TOK, (g + 1) * G_TOK)
            dq = jnp.zeros((G_TOK, HEAD_PAIR), F32)
            dk = jnp.zeros((GK_TOK, HEAD_PAIR), F32)
            dv = jnp.zeros((GK_TOK, HEAD_PAIR), F32)
            for hh in range(2):
                sr = _score_rows(g, hh)
                dsb = dsb_s[sr, :]
                dq = dq + _dot(dsb, kh_s[hh, pl.ds(offs[g], GK_TOK), :])
                dk = dk + _dot_tn(dsb, qh[hh][rows])
                dv = dv + _dot_tn(pb_s[sr, :], dob[g, hh])
            dq_ref[rows, :] = (dq * scale).astype(BF16)
            at = pl.multiple_of(base + offs[g], 2 * GRID_W)
            dk_ref[pl.ds(at, GK_TOK), :] += dk
            dv_ref[pl.ds(at, GK_TOK), :] += dv

    pair = pl.BlockSpec((Q_TOK, HEAD_PAIR), lambda hp, j: (j, hp))
    whole = pl.BlockSpec((s, HEAD_PAIR), lambda hp, j: (0, hp))
    call = _riding_call(
        body, job, 13, 4, (NA_HEADS // 2) * n_j, lambda: pl.program_id(0) * n_j + pl.program_id(1),
        name=name, grid=(NA_HEADS // 2, n_j),
        in_specs=_na_specs(s, (0, D_NA, 2 * D_NA)) + [
            pl.BlockSpec((2, N_SLOT + 1, GRID_W, 2 * GRID_W), lambda hp, j: (hp, 0, 0, 0)),
            pair, pl.BlockSpec((Q_TOK, HEAD_PAIR), lambda hp, j: (j, do_col + hp)),
            pl.BlockSpec((1, Q_TOK, HEAD_PAIR), lambda hp, j: (hp, j, 0))],
        out_specs=[pair, whole, whole, pl.BlockSpec((2, N_SLOT, GRID_W, 2 * GRID_W), lambda hp, j: (hp, 0, 0, 0))],
        out_shape=[jax.ShapeDtypeStruct((s, D_NA), BF16), jax.ShapeDtypeStruct((s, D_NA), F32),
                   jax.ShapeDtypeStruct((s, D_NA), F32),
                   jax.ShapeDtypeStruct((NA_HEADS, N_SLOT, GRID_W, 2 * GRID_W), F32)],
        scratch_shapes=[pltpu.VMEM((STACK_TOK, GK_TOK), F32), pltpu.VMEM((K_TOK, HEAD_PAIR), BF16),
                        pltpu.VMEM((2, K_TOK, HEAD_PAIR), BF16), pltpu.VMEM((K_TOK, HEAD_PAIR), BF16),
                        pltpu.VMEM((STACK_TOK, GK_TOK), F32), pltpu.VMEM((STACK_TOK, GK_TOK), F32),
                        pltpu.VMEM((STACK_TOK, GK_TOK), BF16), pltpu.VMEM((STACK_TOK, GK_TOK), BF16)],
        compiler_params=_params())
    return call(*([qkv] * 9), tiles, o, dmix, lse)


def _rpb_finish(tiles, name):
    valid, dc = _na_col_tables()
    n_dc = 2 * NA_COLS - 1
    sel = np.zeros((GRID_W, 2 * GRID_W, LANES), np.float32)
    for qc in range(GRID_W):
        for kc in range(GRID_W):
            if valid[qc, kc]:
                sel[qc, kc, dc[qc, kc]] = 1.0
                sel[qc, GRID_W + kc, LANES // 2 + dc[qc, kc]] = 1.0
    sel = jnp.asarray(sel.reshape(GRID_W * 2 * GRID_W, LANES))
    flat = tiles.reshape(NA_HEADS * 2 * NA_ROWS, GRID_W * 2 * GRID_W)

    def body(a_ref, b_ref, out_ref):
        out_ref[...] = jnp.dot(a_ref[...], b_ref[...], preferred_element_type=F32, precision=lax.Precision.HIGHEST)

    sums = pl.pallas_call(
        body, name=name, out_shape=jax.ShapeDtypeStruct((flat.shape[0], LANES), F32),
        compiler_params=_params())(flat, sel).reshape(NA_HEADS, 2 * NA_ROWS, LANES)
    return sums[:, 1:, :n_dc] + sums[:, :2 * NA_ROWS - 1, LANES // 2:LANES // 2 + n_dc]


def _loss_grad(y, target, name):
    s, d = y.shape
    tm = min(s, 1024)

    def body(y_ref, t_ref, sum_ref, dy_ref):
        diff = y_ref[...] - t_ref[...]
        dy_ref[...] = diff * (1.0 / d)
        part = jnp.zeros((8, LANES), F32) + jnp.sum(diff * diff)
        _accumulate(sum_ref, part, pl.program_id(0) == 0)

    row = pl.BlockSpec((tm, d), lambda i: (i, 0))
    return pl.pallas_call(
        body, name=name, grid=(s // tm,), in_specs=[row, row],
        out_specs=[pl.BlockSpec((8, LANES), lambda i: (0, 0)), row],
        out_shape=[jax.ShapeDtypeStruct((8, LANES), F32), jax.ShapeDtypeStruct((s, d), F32)],
        compiler_params=_params())(y, target)


def _adamw(w, g, m, v, name):
    rows, cols = w.shape
    tr = _row_tile(rows, 512, 8)

    def body(w_ref, g_ref, m_ref, v_ref, d_ref, nm_ref, nv_ref):
        gv = g_ref[...]
        nm = ADAM_B1 * m_ref[...] + (1.0 - ADAM_B1) * gv
        nv = ADAM_B2 * v_ref[...] + (1.0 - ADAM_B2) * (gv * gv)
        m_hat = nm / (1.0 - ADAM_B1 ** ADAM_STEP)
        v_hat = nv / (1.0 - ADAM_B2 ** ADAM_STEP)
        d_ref[...] = -ADAM_LR * (m_hat / (jnp.sqrt(v_hat) + ADAM_EPS) + ADAM_WD * w_ref[...])
        nm_ref[...] = nm
        nv_ref[...] = nv

    blk = pl.BlockSpec((tr, cols), lambda r: (r, 0))
    return pl.pallas_call(
        body, name=name, grid=(rows // tr,), in_specs=[blk] * 4, out_specs=[blk] * 3,
        out_shape=[jax.ShapeDtypeStruct((rows, cols), F32)] * 3, compiler_params=_params())(w, g, m, v)


def _adamw_nd(w, g, m, v, name):
    shape = w.shape
    flat = lambda t: t.reshape(-1, shape[-1])
    return tuple(t.reshape(shape) for t in _adamw(flat(w), flat(g), flat(m), flat(v), name))


def _pack(parts, rows_mult=64):
    flat = jnp.concatenate([p.reshape(-1).astype(F32) for p in parts])
    per = LANES * rows_mult
    total = -(-flat.shape[0] // per) * per
    return jnp.pad(flat, (0, total - flat.shape[0])).reshape(-1, LANES)


def _unpack(packed, shapes):
    flat = packed.reshape(-1)
    out, pos = [], 0
    for shp in shapes:
        n = int(np.prod(shp))
        out.append(flat[pos:pos + n].reshape(shp))
        pos += n
    return out


def kernel(x, ffn1_w_gate, ffn1_w_up, ffn1_w_down, ffn2_w_gate, ffn2_w_up, ffn2_w_down, w_in, pool_w, pool_scale, conv_w, rpb, w_out, ln_g, ln_b, loss_target, m_ffn1_w_gate, m_ffn1_w_up, m_ffn1_w_down, m_ffn2_w_gate, m_ffn2_w_up, m_ffn2_w_down, m_w_in, m_pool_w, m_pool_scale, m_conv_w, m_rpb, m_w_out, m_ln_g, m_ln_b, v_ffn1_w_gate, v_ffn1_w_up, v_ffn1_w_down, v_ffn2_w_gate, v_ffn2_w_up, v_ffn2_w_down, v_w_in, v_pool_w, v_pool_scale, v_conv_w, v_rpb, v_w_out, v_ln_g, v_ln_b):
    n_l, d, fs = ffn1_w_gate.shape
    s = x.shape[1]
    rows = s // GRID_W
    assert x.shape[0] == 1 and s % Q_TOK == 0 and rows >= K_ROWS and fs % BF16_ROWS == 0
    alpha = (2.0 * n_l) ** 0.25
    xi, yi, ci = _mesh_pos()
    me = 4 * xi + 2 * yi + ci
    core = jnp.reshape(ci, (1,)).astype(jnp.int32)
    ln_w, cw_w = ln_g.shape[2], conv_w.shape[2]

    tr = lambda w: jnp.swapaxes(w, 1, 2)
    ffn1_shard = jnp.stack([tr(ffn1_w_gate), tr(ffn1_w_up), ffn1_w_down], axis=1).astype(BF16)
    ffn2_shard = jnp.stack([tr(ffn2_w_gate), tr(ffn2_w_up), ffn2_w_down], axis=1).astype(BF16)
    win_shard, wout_shard = tr(w_in).astype(BF16), w_out.astype(BF16)
    small_shard = _pack([ln_g, ln_b, conv_w])
    w_ffn1, small = _exchange_alone(_Gather([ffn1_shard[0], small_shard]), "gather_first")
    n_ln = n_l * 3 * ln_w
    small = small.reshape(N_DEV, -1)
    unshard = lambda t, width: jnp.moveaxis(t.reshape(N_DEV, n_l, 3, width), 0, 2).reshape(n_l, 3, N_DEV * width)
    ln_g_all = unshard(small[:, :n_ln], ln_w)
    ln_b_all = unshard(small[:, n_ln:2 * n_ln], ln_w)
    conv_all = unshard(small[:, 2 * n_ln:2 * n_ln + n_l * 3 * cw_w], cw_w)
    pool_bd = jnp.zeros((n_l, D_POOL, D_POOL), F32)
    for g in range(len(POOL_WINDOWS)):
        sl = slice(g * POOL_GROUP, (g + 1) * POOL_GROUP)
        pool_bd = pool_bd.at[:, sl, sl].set(pool_w[:, g])
    pool_bd = pool_bd.astype(BF16)
    lnp = lambda arr, l, j: arr[l, j].reshape(1, d)

    saved = []
    h = x.reshape(s, d)
    for l in range(n_l):
        a1, u1, h1, z1, x1, w_in_l, w_out_l, w_ffn2 = _ffn_fwd(
            h, w_ffn1, lnp(ln_g_all, l, 0), lnp(ln_b_all, l, 0), alpha, f"ffn1_fwd_{l}",
            job=_Gather([win_shard[l], wout_shard[l], ffn2_shard[l]]))
        proj = _win_fwd(x1, w_in_l, f"win_fwd_{l}")
        bias = _na_tiles(rpb[l])
        yab = _local_fwd(proj[0], pool_bd[l], pool_scale[l].reshape(1, D_POOL), conv_all[l], f"local_fwd_{l}")
        yc, lse = _attn_fwd(proj[1], bias, f"attn_fwd_{l}")
        z2, x2 = _wout_fwd(x1, yab, yc, w_out_l, lnp(ln_g_all, l, 1), lnp(ln_b_all, l, 1), alpha, f"wout_fwd_{l}")
        a2, u2, h2, z3, x3, *w_next = _ffn_fwd(
            x2, w_ffn2, lnp(ln_g_all, l, 2), lnp(ln_b_all, l, 2), alpha, f"ffn2_fwd_{l}",
            job=_Gather([ffn1_shard[l + 1]]) if l + 1 < n_l else None)
        saved.append((h, a1, u1, h1, z1, x1, proj, bias, yab, yc, lse, z2, x2, a2, u2, h2, z3, w_ffn1, w_in_l, w_out_l, w_ffn2))
        h = x3
        if w_next:
            w_ffn1 = w_next[0]

    sq, dh = _loss_grad(h, loss_target.reshape(s, d), "loss_head")
    loss = lax.psum(sq[0, 0] * (0.5 / d), MESH_AXES)

    def pair_sum(blocks, tag):
        flat = [b.reshape(N_DEV, -1, d) for b in blocks]
        got = _pair_exchange(flat, f"grads_pair_exchange_{tag}")
        return [_pair_add(b, g, core, f"grads_pair_add_{tag}_{i}") for i, (b, g) in enumerate(zip(flat, got))]

    small_grads = [None] * n_l
    reduced = [None] * n_l
    waiting = None
    for l in reversed(range(n_l)):
        x0, a1, u1, h1, z1, x1, proj, bias, yab, yc, lse, z2, x2, a2, u2, h2, z3, w_ffn1, w_in_l, w_out_l, w_ffn2 = saved[l]
        dx2, da, du, df, dg3, db3, *crossed = _ffn_bwd_dx(
            dh, z3, a2, u2, w_ffn2, lnp(ln_g_all, l, 2), alpha, f"ffn2_bwd_dx_{l}",
            job=_ChipExchange(waiting) if waiting else None)
        if waiting:
            reduced[l + 1] += crossed
        g2 = _ffn_bwd_dwd(h2, df, _ffn_bwd_dwgu(da, du, x2, fs, f"ffn2_bwd_dwgu_{l}"), f"ffn2_bwd_dwd_{l}")
        p2 = pair_sum([g2], f"ffn2_{l}")
        dmix, dxp, dg2, db2, g_out = _wout_bwd(dx2, z2, yab, yc, w_out_l, lnp(ln_g_all, l, 1), alpha, f"wout_bwd_{l}")
        dq, dk, dv, dtiles, *crossed = _attn_bwd(proj[1], bias, yc, dmix, lse, f"attn_bwd_{l}", job=_ChipExchange(p2))
        reduced[l] = list(crossed)
        dloc, dpw, dsc, dcw = _local_bwd(proj[0], dmix, pool_bd[l], pool_scale[l].reshape(1, D_POOL), conv_all[l],
                                         f"local_bwd_{l}")
        dx1, g_in = _win_bwd(dxp, dloc, dq, dk, dv, x1, w_in_l, f"win_bwd_{l}")
        dx0, da, du, df, dg1, db1 = _ffn_bwd_dx(dx1, z1, a1, u1, w_ffn1, lnp(ln_g_all, l, 0), alpha, f"ffn1_bwd_dx_{l}")
        g1 = _ffn_bwd_dwd(h1, df, _ffn_bwd_dwgu(da, du, x0, fs, f"ffn1_bwd_dwgu_{l}"), f"ffn1_bwd_dwd_{l}")
        waiting = pair_sum([g_out, g_in, g1], f"mix_{l}")
        drpb = _rpb_finish(dtiles, f"rpb_finish_{l}")
        dpool = jnp.stack([dpw[g * POOL_GROUP:(g + 1) * POOL_GROUP, g * POOL_GROUP:(g + 1) * POOL_GROUP]
                           for g in range(len(POOL_WINDOWS))])
        small_grads[l] = (jnp.concatenate([dg1, dg2, dg3]), jnp.concatenate([db1, db2, db3]), dcw[0:3], dpool, dsc[0], drpb)
        dh = dx0
    grad_x = dh.reshape(x.shape)

    reduced[0] += _exchange_alone(_ChipExchange(waiting), "grads_chip_exchange_last")
    sums = [[_sum_blocks(q, f"grads_chip_sum_{l}_{i}") for i, q in enumerate(reduced[l])] for l in range(n_l)]
    r_ffn2, r_out, r_in, r_ffn1 = [jnp.stack([sums[l][i] for l in range(n_l)]) for i in range(4)]
    r_ffn1, r_ffn2 = r_ffn1.reshape(n_l, 3, fs, d), r_ffn2.reshape(n_l, 3, fs, d)
    grads = {
        "ffn1_w_gate": tr(r_ffn1[:, 0]), "ffn1_w_up": tr(r_ffn1[:, 1]), "ffn1_w_down": r_ffn1[:, 2],
        "ffn2_w_gate": tr(r_ffn2[:, 0]), "ffn2_w_up": tr(r_ffn2[:, 1]), "ffn2_w_down": r_ffn2[:, 2],
        "w_in": tr(r_in), "w_out": r_out}

    stack = lambda k: jnp.stack([small_grads[l][k] for l in range(n_l)])
    small_shapes = [(n_l, 3, d), (n_l, 3, d), (n_l, 3, D_CONV), pool_w.shape, pool_scale.shape, rpb.shape]
    (small_all,) = _exchange_alone(_Gather([_pack([stack(k) for k in range(6)])]), "gather_small_grads")
    small_sum = _sum_blocks(small_all, "small_grads_sum")
    g_ln_g, g_ln_b, g_conv, g_pool_w, g_pool_scale, g_rpb = _unpack(small_sum, small_shapes)
    own = lambda t, width: lax.dynamic_slice_in_dim(t, me * width, width, axis=2)
    grads.update({"ln_g": own(g_ln_g, ln_w), "ln_b": own(g_ln_b, ln_w), "conv_w": own(g_conv, cw_w),
                  "pool_w": g_pool_w, "pool_scale": g_pool_scale, "rpb": g_rpb})

    weights = dict(ffn1_w_gate=ffn1_w_gate, ffn1_w_up=ffn1_w_up, ffn1_w_down=ffn1_w_down, ffn2_w_gate=ffn2_w_gate,
                   ffn2_w_up=ffn2_w_up, ffn2_w_down=ffn2_w_down, w_in=w_in, pool_w=pool_w, pool_scale=pool_scale,
                   conv_w=conv_w, rpb=rpb, w_out=w_out, ln_g=ln_g, ln_b=ln_b)
    m_in = dict(ffn1_w_gate=m_ffn1_w_gate, ffn1_w_up=m_ffn1_w_up, ffn1_w_down=m_ffn1_w_down, ffn2_w_gate=m_ffn2_w_gate,
                ffn2_w_up=m_ffn2_w_up, ffn2_w_down=m_ffn2_w_down, w_in=m_w_in, pool_w=m_pool_w, pool_scale=m_pool_scale,
                conv_w=m_conv_w, rpb=m_rpb, w_out=m_w_out, ln_g=m_ln_g, ln_b=m_ln_b)
    v_in = dict(ffn1_w_gate=v_ffn1_w_gate, ffn1_w_up=v_ffn1_w_up, ffn1_w_down=v_ffn1_w_down, ffn2_w_gate=v_ffn2_w_gate,
                ffn2_w_up=v_ffn2_w_up, ffn2_w_down=v_ffn2_w_down, w_in=v_w_in, pool_w=v_pool_w, pool_scale=v_pool_scale,
                conv_w=v_conv_w, rpb=v_rpb, w_out=v_w_out, ln_g=v_ln_g, ln_b=v_ln_b)
    names = list(weights)
    large = ["ffn1_w_gate", "ffn1_w_up", "ffn1_w_down", "ffn2_w_gate", "ffn2_w_up", "ffn2_w_down", "w_in", "w_out"]
    tiny = [n for n in names if n not in large]
    delta, new_m, new_v = {}, {}, {}
    for n in large:
        delta[n], new_m[n], new_v[n] = _adamw_nd(weights[n], grads[n], m_in[n], v_in[n], f"adamw_{n}")
    packed = [_pack([t[n] for n in tiny]) for t in (weights, grads, m_in, v_in)]
    tiny_out = _adamw(*packed, "adamw_small")
    tiny_shapes = [weights[n].shape for n in tiny]
    for res, t in zip((delta, new_m, new_v), tiny_out):
        res.update(dict(zip(tiny, _unpack(t, tiny_shapes))))

    return (loss, grad_x, *[grads[n] for n in names], *[delta[n] for n in names],
            *[new_m[n] for n in names], *[new_v[n] for n in names])
```

```python
import functools

import numpy as np
import jax
import jax.numpy as jnp
from jax import lax
from jax.experimental import pallas as pl
from jax.experimental.pallas import tpu as pltpu

F32, BF16 = jnp.float32, jnp.bfloat16
MESH = pl.DeviceIdType.MESH
N_DEV = 8
MESH_AXES = ("x", "y", "c")

LN_EPS = 1e-5
NEG_INF = -1e30
D_POOL = 256
POOL_WINDOWS = (2, 4, 8, 16)
POOL_GROUP = 64
D_CONV = 256
NA_HEADS = 8
NA_HEAD_DIM = 64
D_NA = NA_HEADS * NA_HEAD_DIM
GRID_W = 64
NA_ROWS = 8
NA_COLS = 16
D_LOC = D_POOL + 3 * D_CONV
D_MIX = D_POOL + D_CONV + D_NA
ADAM_LR, ADAM_B1, ADAM_B2, ADAM_EPS, ADAM_WD, ADAM_STEP = 0.001, 0.9, 0.999, 1e-08, 0.01, 10

VMEM_LIMIT_BYTES = 56 * 1024 * 1024
LANES = 128
BF16_ROWS = 16
HALO = 16
Q_ROWS = 8
K_ROWS = 16
Q_TOK = Q_ROWS * GRID_W
K_TOK = K_ROWS * GRID_W
K_BLK = 4 * GRID_W
HEAD_PAIR = 2 * NA_HEAD_DIM
FFN_CHUNK_DEVS = 4

NT = (((1,), (1,)), ((), ()))
TN = (((0,), (0,)), ((), ()))


def _dot(a, b):
    return jnp.dot(a, b, preferred_element_type=F32)


def _dot_nt(a, b):
    return lax.dot_general(a, b, NT, preferred_element_type=F32)


def _dot_tn(a, b):
    return lax.dot_general(a, b, TN, preferred_element_type=F32)


def _params():
    return pltpu.CompilerParams(vmem_limit_bytes=VMEM_LIMIT_BYTES)


def _row_tile(rows, pref, mult=BF16_ROWS):
    t = min(rows, pref)
    t -= t % mult
    while t > mult and rows % t:
        t -= mult
    assert t > 0 and rows % t == 0, (rows, pref)
    return t


def _mesh_pos():
    return tuple(lax.axis_index(a) for a in MESH_AXES)


def _any_spec():
    return pl.BlockSpec(memory_space=pl.ANY)


class _Gather:
    def __init__(self, shards):
        self.arrays = list(shards)
        n = len(shards)
        self.out_shape = [jax.ShapeDtypeStruct((N_DEV,) + s.shape, s.dtype) for s in shards]
        self.scratch = [pltpu.SemaphoreType.DMA((n, 7)), pltpu.SemaphoreType.DMA((n, 7)), pltpu.SemaphoreType.DMA((n,))]

    def phases(self, ins, outs, sems):
        n = len(ins)
        send_sems, recv_sems, local_sems = sems
        x, y, c = _mesh_pos()
        me, sibling = (x, y, c), (x, y, 1 - c)
        chips = [(1 - x, y), (x, 1 - y), (1 - x, 1 - y)]

        def copy(a, k, block, to, src=None):
            dst = outs[a].at[4 * block[0] + 2 * block[1] + block[2]]
            return pltpu.make_async_remote_copy(
                src_ref=dst if src is None else src, dst_ref=dst,
                send_sem=send_sems.at[a, k], recv_sem=recv_sems.at[a, k],
                device_id=to, device_id_type=MESH)

        def mine():
            return [pltpu.make_async_copy(ins[a], outs[a].at[4 * x + 2 * y + c], local_sems.at[a]) for a in range(n)]

        def first():
            return [cp for a in range(n) for cp in
                    [copy(a, 0, me, sibling, src=ins[a])]
                    + [copy(a, 1 + j, me, (*chip, c), src=ins[a]) for j, chip in enumerate(chips)]]

        def passed():
            return [copy(a, 4 + j, (*chip, c), sibling) for j, chip in enumerate(chips) for a in range(n)]

        def start():
            for cp in mine() + first():
                cp.start()

        def middle():
            for j, chip in enumerate(chips):
                for a in range(n):
                    copy(a, 1 + j, (*chip, c), me).wait_recv()
            for cp in passed():
                cp.start()

        def finish():
            for a in range(n):
                copy(a, 0, sibling, me).wait_recv()
                for j, chip in enumerate(chips):
                    copy(a, 4 + j, (*chip, 1 - c), me).wait_recv()
            for cp in first() + passed():
                cp.wait_send()
            for cp in mine():
                cp.wait()

        return start, middle, finish


class _ChipExchange:
    def __init__(self, parts):
        self.arrays = list(parts)
        n = len(parts)
        self.out_shape = [jax.ShapeDtypeStruct(s.shape, s.dtype) for s in parts]
        self.scratch = [pltpu.SemaphoreType.DMA((n, 3)), pltpu.SemaphoreType.DMA((n, 3)), pltpu.SemaphoreType.DMA((n,))]

    def phases(self, ins, outs, sems):
        n = len(ins)
        send_sems, recv_sems, local_sems = sems
        x, y, c = _mesh_pos()
        my_chip = 2 * x + y
        chips = [(1 - x, y), (x, 1 - y), (1 - x, 1 - y)]

        def own():
            return [pltpu.make_async_copy(ins[a].at[my_chip], outs[a].at[my_chip], local_sems.at[a]) for a in range(n)]

        def copy(a, k, src_chip, dst_chip, to):
            return pltpu.make_async_remote_copy(
                src_ref=ins[a].at[src_chip], dst_ref=outs[a].at[dst_chip],
                send_sem=send_sems.at[a, k], recv_sem=recv_sems.at[a, k],
                device_id=to, device_id_type=MESH)

        def sends():
            return [copy(a, k, 2 * px + py, my_chip, (px, py, c)) for a in range(n) for k, (px, py) in enumerate(chips)]

        def start():
            for cp in own() + sends():
                cp.start()

        def finish():
            for cp in sends():
                cp.wait_send()
            for a in range(n):
                for k, (px, py) in enumerate(chips):
                    copy(a, k, my_chip, 2 * px + py, (px, py, c)).wait_recv()
            for cp in own():
                cp.wait()

        return start, None, finish


def _exchange_alone(job, name):
    n = len(job.arrays)

    def body(*refs):
        for phase in job.phases(refs[:n], refs[n:2 * n], refs[2 * n:]):
            if phase is not None:
                phase()

    return pl.pallas_call(
        body, name=name, out_shape=job.out_shape,
        in_specs=[_any_spec()] * n, out_specs=[_any_spec()] * n, scratch_shapes=job.scratch,
    )(*job.arrays)


def _riding_call(body, job, n_in, n_out, n_steps, step, **kw):
    if job is None:
        return pl.pallas_call(body, **kw)
    n_job, n_sem = len(job.arrays), len(job.scratch)
    kw = dict(kw, in_specs=list(kw["in_specs"]) + [_any_spec()] * n_job,
              out_specs=list(kw["out_specs"]) + [_any_spec()] * n_job,
              out_shape=list(kw["out_shape"]) + job.out_shape,
              scratch_shapes=list(kw.get("scratch_shapes", ())) + job.scratch)

    def riding(*refs):
        ins, job_ins = refs[:n_in], refs[n_in:n_in + n_job]
        outs = refs[n_in + n_job:n_in + n_job + n_out]
        job_outs = refs[n_in + n_job + n_out:n_in + 2 * n_job + n_out]
        scratch = refs[n_in + 2 * n_job + n_out:]
        start, middle, finish = job.phases(job_ins, job_outs, scratch[len(scratch) - n_sem:])
        now = step()
        pl.when(now == 0)(start)
        if middle is not None:
            assert n_steps >= 3
            pl.when(now == (7 * n_steps) // 8 - 1)(middle)
        body(*ins, *outs, *scratch[:len(scratch) - n_sem])
        pl.when(now == n_steps - 1)(finish)

    call = pl.pallas_call(riding, **kw)
    return lambda *args: call(*args, *job.arrays)


class _PairExchange:
    def __init__(self, slabs):
        self.arrays = list(slabs)
        n = len(slabs)
        self.out_shape = [jax.ShapeDtypeStruct((4,) + s.shape[1:], s.dtype) for s in slabs]
        self.scratch = [pltpu.SemaphoreType.DMA((n, 4)), pltpu.SemaphoreType.DMA((n, 4))]

    def phases(self, ins, outs, sems):
        n = len(ins)
        send_sems, recv_sems = sems
        x, y, c = _mesh_pos()

        def copies():
            return [pltpu.make_async_remote_copy(
                src_ref=ins[a].at[2 * j + 1 - c], dst_ref=outs[a].at[j],
                send_sem=send_sems.at[a, j], recv_sem=recv_sems.at[a, j],
                device_id=(x, y, 1 - c), device_id_type=MESH) for a in range(n) for j in range(4)]

        def start():
            for cp in copies():
                cp.start()

        def finish():
            for cp in copies():
                cp.wait_send()
            for cp in copies():
                cp.wait_recv()

        return start, None, finish


def _pair_add(slab, got, core, name):
    _, rows, d = slab.shape
    tr = _row_tile(rows, 1024)

    def body(core_ref, mine_ref, got_ref, out_ref):
        out_ref[...] = (mine_ref[...].astype(F32) + got_ref[...].astype(F32)).astype(out_ref.dtype)

    grid_spec = pltpu.PrefetchScalarGridSpec(
        num_scalar_prefetch=1, grid=(4, rows // tr),
        in_specs=[pl.BlockSpec((1, tr, d), lambda j, r, core_ref: (2 * j + core_ref[0], r, 0)),
                  pl.BlockSpec((1, tr, d), lambda j, r, core_ref: (j, r, 0))],
        out_specs=pl.BlockSpec((1, tr, d), lambda j, r, core_ref: (j, r, 0)))
    return pl.pallas_call(body, name=name, grid_spec=grid_spec,
                          out_shape=jax.ShapeDtypeStruct((4, rows, d), slab.dtype),
                          compiler_params=_params())(core, slab, got)


def _sum_blocks(parts, name):
    k, rows, d = parts.shape
    tr = _row_tile(rows, 512, BF16_ROWS if parts.dtype == BF16 else 8)

    def body(in_ref, out_ref):
        acc = in_ref[0].astype(F32)
        for j in range(1, k):
            acc = acc + in_ref[j].astype(F32)
        out_ref[...] = acc

    return pl.pallas_call(
        body, name=name, grid=(rows // tr,),
        in_specs=[pl.BlockSpec((k, tr, d), lambda r: (0, r, 0))],
        out_specs=pl.BlockSpec((tr, d), lambda r: (r, 0)),
        out_shape=jax.ShapeDtypeStruct((rows, d), F32), compiler_params=_params())(parts)


def _ln_stats(z):
    mu = jnp.mean(z, axis=-1, keepdims=True)
    zc = z - mu
    var = jnp.mean(zc * zc, axis=-1, keepdims=True)
    rstd = lax.rsqrt(var + LN_EPS)
    return zc * rstd, rstd


def _ln_bwd(dy, z, g):
    zhat, rstd = _ln_stats(z)
    dyg = dy * g
    m1 = jnp.mean(dyg, axis=-1, keepdims=True)
    m2 = jnp.mean(dyg * zhat, axis=-1, keepdims=True)
    dz = rstd * (dyg - m1 - zhat * m2)
    return dz, jnp.sum(dy * zhat, axis=0, keepdims=True), jnp.sum(dy, axis=0, keepdims=True)


def _accumulate(ref, value, first):
    @pl.when(first)
    def _():
        ref[...] = value

    @pl.when(jnp.logical_not(first))
    def _():
        ref[...] += value


def _add_matmul(acc_ref, first, matmul):
    @pl.when(first)
    def _():
        acc_ref[...] = jnp.zeros_like(acc_ref)

    acc_ref[...] += matmul()


def _ffn_weight_specs(fs, d):
    def spec(row):
        return pl.BlockSpec((N_DEV, 1, fs, d), lambda i: (0, row, 0, 0), pipeline_mode=pl.Buffered(1))
    return [spec(0), spec(1), spec(2)]


def _ffn_fwd(x, w, ln_g, ln_b, alpha, name, job=None):
    s, d = x.shape
    fs = w.shape[2]
    f = N_DEV * fs
    tm = min(s, 256)

    def body(x_ref, wg_ref, wu_ref, wd_ref, g_ref, b_ref, a_ref, u_ref, h_ref, z_ref, y_ref):
        xv = x_ref[...]
        xb = xv.astype(BF16)
        a = _dot_nt(xb, wg_ref[...].reshape(f, d))
        u = _dot_nt(xb, wu_ref[...].reshape(f, d))
        a_ref[...] = a.astype(BF16)
        u_ref[...] = u.astype(BF16)
        h = ((a * jax.nn.sigmoid(a)) * u).astype(BF16)
        h_ref[...] = h
        z = alpha * xv + 0.5 * _dot(h, wd_ref[...].reshape(f, d))
        zhat, _ = _ln_stats(z)
        z_ref[...] = z
        y_ref[...] = zhat * g_ref[...] + b_ref[...]

    row = pl.BlockSpec((tm, d), lambda i: (i, 0))
    vec = pl.BlockSpec((1, d), lambda i: (0, 0))
    hid = pl.BlockSpec((tm, f), lambda i: (i, 0))
    call = _riding_call(
        body, job, 6, 5, s // tm, lambda: pl.program_id(0),
        name=name, grid=(s // tm,),
        in_specs=[row] + _ffn_weight_specs(fs, d) + [vec, vec],
        out_specs=[hid, hid, hid, row, row],
        out_shape=[jax.ShapeDtypeStruct((s, f), BF16)] * 3 + [jax.ShapeDtypeStruct((s, d), F32)] * 2,
        compiler_params=_params())
    return call(x, w, w, w, ln_g, ln_b)


def _ffn_bwd_dx(dy, z, a, u, w, ln_g, alpha, name, job=None):
    s, d = dy.shape
    fs = w.shape[2]
    f = N_DEV * fs
    tm = min(s, 256)

    def body(dy_ref, z_ref, a_ref, u_ref, wg_ref, wu_ref, wd_ref, g_ref,
             dx_ref, da_ref, du_ref, df_ref, dg_ref, db_ref):
        i = pl.program_id(0)
        dz, dg, db = _ln_bwd(dy_ref[...], z_ref[...], g_ref[...])
        _accumulate(dg_ref, dg, i == 0)
        _accumulate(db_ref, db, i == 0)
        df = (0.5 * dz).astype(BF16)
        df_ref[...] = df
        av = a_ref[...].astype(F32)
        uv = u_ref[...].astype(F32)
        sg = jax.nn.sigmoid(av)
        dh = _dot_nt(df, wd_ref[...].reshape(f, d))
        du = (dh * (av * sg)).astype(BF16)
        da = (dh * uv * (sg * (1.0 + av * (1.0 - sg)))).astype(BF16)
        da_ref[...] = da
        du_ref[...] = du
        dx_ref[...] = alpha * dz + _dot(da, wg_ref[...].reshape(f, d)) + _dot(du, wu_ref[...].reshape(f, d))

    row = pl.BlockSpec((tm, d), lambda i: (i, 0))
    vec = pl.BlockSpec((1, d), lambda i: (0, 0))
    hid = pl.BlockSpec((tm, f), lambda i: (i, 0))
    call = _riding_call(
        body, job, 8, 6, s // tm, lambda: pl.program_id(0),
        name=name, grid=(s // tm,),
        in_specs=[row, row, hid, hid] + _ffn_weight_specs(fs, d) + [vec],
        out_specs=[row, hid, hid, row, vec, vec],
        out_shape=[jax.ShapeDtypeStruct((s, d), F32)] + [jax.ShapeDtypeStruct((s, f), BF16)] * 2
                  + [jax.ShapeDtypeStruct((s, d), BF16)] + [jax.ShapeDtypeStruct((1, d), F32)] * 2,
        compiler_params=_params())
    return call(dy, z, a, u, w, w, w, ln_g)


def _ffn_bwd_dwgu(da, du, x, fs, name):
    s, d = x.shape
    tf = FFN_CHUNK_DEVS * fs
    n_c = N_DEV // FFN_CHUNK_DEVS
    tk = min(s, 1024)
    n_k = s // tk

    def body(da_ref, du_ref, x_ref, out_ref, accg_s, accu_s):
        k = pl.program_id(1)
        xb = x_ref[...].astype(BF16)
        _add_matmul(accg_s, k == 0, lambda: _dot_tn(da_ref[...], xb))
        _add_matmul(accu_s, k == 0, lambda: _dot_tn(du_ref[...], xb))

        @pl.when(k == n_k - 1)
        def _():
            out_ref[:, 0] = accg_s[...].astype(BF16).reshape(FFN_CHUNK_DEVS, fs, d)
            out_ref[:, 1] = accu_s[...].astype(BF16).reshape(FFN_CHUNK_DEVS, fs, d)

    hid = pl.BlockSpec((tk, tf), lambda c, k: (k, c))
    return pl.pallas_call(
        body, name=name, grid=(n_c, n_k),
        in_specs=[hid, hid, pl.BlockSpec((tk, d), lambda c, k: (k, 0))],
        out_specs=pl.BlockSpec((FFN_CHUNK_DEVS, 2, fs, d), lambda c, k: (c, 0, 0, 0), pipeline_mode=pl.Buffered(1)),
        out_shape=jax.ShapeDtypeStruct((N_DEV, 3, fs, d), BF16),
        scratch_shapes=[pltpu.VMEM((tf, d), F32), pltpu.VMEM((tf, d), F32)],
        compiler_params=_params())(da, du, x)


def _ffn_bwd_dwd(h, df, blocks, name):
    s, d = df.shape
    fs = blocks.shape[2]
    tf = FFN_CHUNK_DEVS * fs
    n_c = N_DEV // FFN_CHUNK_DEVS
    tk = min(s, 1024)
    n_k = s // tk

    def body(h_ref, df_ref, blocks_ref, out_ref, acc_s):
        k = pl.program_id(1)
        _add_matmul(acc_s, k == 0, lambda: _dot_tn(h_ref[...], df_ref[...]))

        @pl.when(k == n_k - 1)
        def _():
            out_ref[:, 0] = acc_s[...].astype(BF16).reshape(FFN_CHUNK_DEVS, fs, d)

    return pl.pallas_call(
        body, name=name, grid=(n_c, n_k),
        in_specs=[pl.BlockSpec((tk, tf), lambda c, k: (k, c)), pl.BlockSpec((tk, d), lambda c, k: (k, 0)), _any_spec()],
        out_specs=pl.BlockSpec((FFN_CHUNK_DEVS, 1, fs, d), lambda c, k: (c, 2, 0, 0), pipeline_mode=pl.Buffered(1)),
        out_shape=jax.ShapeDtypeStruct(blocks.shape, BF16), input_output_aliases={2: 0},
        scratch_shapes=[pltpu.VMEM((tf, d), F32)],
        compiler_params=_params())(h, df, blocks)


def _whole(arr):
    return pl.BlockSpec(arr.shape, lambda i: (0,) * arr.ndim, pipeline_mode=pl.Buffered(1))


def _win_fwd(x, w_in, name):
    s, d = x.shape
    d_in = N_DEV * w_in.shape[1]
    tm = min(s, 512)
    scale = NA_HEAD_DIM ** -0.5
    assert d_in == D_LOC + 3 * D_NA and scale == 0.125

    def body(x_ref, w_ref, loc_ref, qkv_ref):
        proj = _dot_nt(x_ref[...].astype(BF16), w_ref[...].reshape(d_in, d))
        loc_ref[...] = proj[:, :D_LOC]
        qkv_ref[:, :D_NA] = (proj[:, D_LOC:D_LOC + D_NA] * scale).astype(BF16)
        qkv_ref[:, D_NA:] = proj[:, D_LOC + D_NA:].astype(BF16)

    return pl.pallas_call(
        body, name=name, grid=(s // tm,),
        in_specs=[pl.BlockSpec((tm, d), lambda i: (i, 0)), _whole(w_in)],
        out_specs=[pl.BlockSpec((tm, D_LOC), lambda i: (i, 0)), pl.BlockSpec((tm, 3 * D_NA), lambda i: (i, 0))],
        out_shape=[jax.ShapeDtypeStruct((s, D_LOC), F32), jax.ShapeDtypeStruct((s, 3 * D_NA), BF16)],
        compiler_params=_params())(x, w_in)


def _wout_fwd(x, yab, yc, w_out, ln_g, ln_b, alpha, name):
    s, d = x.shape
    tm = min(s, 512)

    def body(x_ref, yab_ref, yc_ref, w_ref, g_ref, b_ref, z_ref, y_ref):
        mix = jnp.concatenate([yab_ref[...], yc_ref[...]], axis=1).astype(BF16)
        z = alpha * x_ref[...] + _dot(mix, w_ref[...].reshape(D_MIX, d))
        zhat, _ = _ln_stats(z)
        z_ref[...] = z
        y_ref[...] = zhat * g_ref[...] + b_ref[...]

    row = pl.BlockSpec((tm, d), lambda i: (i, 0))
    half = pl.BlockSpec((tm, D_MIX // 2), lambda i: (i, 0))
    vec = pl.BlockSpec((1, d), lambda i: (0, 0))
    return pl.pallas_call(
        body, name=name, grid=(s // tm,),
        in_specs=[row, half, half, _whole(w_out), vec, vec],
        out_specs=[row, row], out_shape=[jax.ShapeDtypeStruct((s, d), F32)] * 2,
        compiler_params=_params())(x, yab, yc, w_out, ln_g, ln_b)


def _wout_bwd(dy, z, yab, yc, w_out, ln_g, alpha, name, job=None):
    s, d = dy.shape
    rs = w_out.shape[1]
    tm = min(s, 512)
    n_i = s // tm

    def body(dy_ref, z_ref, yab_ref, yc_ref, w_ref, g_ref, dmix_ref, dxp_ref, dg_ref, db_ref, out_ref, acc_s):
        i = pl.program_id(0)
        dz, dg, db = _ln_bwd(dy_ref[...], z_ref[...], g_ref[...])
        _accumulate(dg_ref, dg, i == 0)
        _accumulate(db_ref, db, i == 0)
        dxp_ref[...] = alpha * dz
        dzb = dz.astype(BF16)
        dmix_ref[...] = _dot_nt(dzb, w_ref[...].reshape(D_MIX, d))
        mix = jnp.concatenate([yab_ref[...], yc_ref[...]], axis=1).astype(BF16)
        _add_matmul(acc_s, i == 0, lambda: _dot_tn(mix, dzb))

        @pl.when(i == n_i - 1)
        def _():
            out_ref[...] = acc_s[...].astype(BF16).reshape(N_DEV, rs, d)

    row = pl.BlockSpec((tm, d), lambda i: (i, 0))
    half = pl.BlockSpec((tm, D_MIX // 2), lambda i: (i, 0))
    vec = pl.BlockSpec((1, d), lambda i: (0, 0))
    call = _riding_call(
        body, job, 6, 5, n_i, lambda: pl.program_id(0),
        name=name, grid=(n_i,),
        in_specs=[row, row, half, half, _whole(w_out), vec],
        out_specs=[pl.BlockSpec((tm, D_MIX), lambda i: (i, 0)), row, vec, vec, _whole(w_out)],
        out_shape=[jax.ShapeDtypeStruct((s, D_MIX), F32), jax.ShapeDtypeStruct((s, d), F32),
                   jax.ShapeDtypeStruct((1, d), F32), jax.ShapeDtypeStruct((1, d), F32),
                   jax.ShapeDtypeStruct(w_out.shape, BF16)],
        scratch_shapes=[pltpu.VMEM((D_MIX, d), F32)],
        compiler_params=_params())
    return call(dy, z, yab, yc, w_out, ln_g)


def _win_bwd(dxp, dloc, dq, dk, dv, x, w_in, name):
    s, d = x.shape
    rs = w_in.shape[1]
    d_in = N_DEV * rs
    tm = min(s, 512)
    n_i = s // tm

    def body(dxp_ref, dloc_ref, dq_ref, dk_ref, dv_ref, x_ref, w_ref, dx_ref, out_ref, acc_s):
        i = pl.program_id(0)
        dp = jnp.concatenate([dloc_ref[...], dq_ref[...], dk_ref[...].astype(BF16), dv_ref[...].astype(BF16)], axis=1)
        dx_ref[...] = dxp_ref[...] + _dot(dp, w_ref[...].reshape(d_in, d))
        _add_matmul(acc_s, i == 0, lambda: _dot_tn(dp, x_ref[...].astype(BF16)))

        @pl.when(i == n_i - 1)
        def _():
            out_ref[...] = acc_s[...].astype(BF16).reshape(N_DEV, rs, d)

    row = pl.BlockSpec((tm, d), lambda i: (i, 0))
    na = pl.BlockSpec((tm, D_NA), lambda i: (i, 0))
    return pl.pallas_call(
        body, name=name, grid=(n_i,),
        in_specs=[row, pl.BlockSpec((tm, D_LOC), lambda i: (i, 0)), na, na, na, row, _whole(w_in)],
        out_specs=[row, _whole(w_in)],
        out_shape=[jax.ShapeDtypeStruct((s, d), F32), jax.ShapeDtypeStruct(w_in.shape, BF16)],
        scratch_shapes=[pltpu.VMEM((d_in, d), F32)],
        compiler_params=_params())(dxp, dloc, dq, dk, dv, x, w_in)


def _shift_rows(v, k):
    n = v.shape[0]
    return pltpu.roll(v, k % n, 0)


def _halo_specs(tm, s, width, col):
    per = tm // HALO
    last = s // HALO - 1
    return [pl.BlockSpec((HALO, width), lambda i: (jnp.maximum(i * per - 1, 0), col)),
            pl.BlockSpec((tm, width), lambda i: (i, col)),
            pl.BlockSpec((HALO, width), lambda i: (jnp.minimum((i + 1) * per, last), col))]


def _token_index(i, tm):
    return i * tm - HALO + lax.broadcasted_iota(jnp.int32, (tm + 2 * HALO, 1), 0)


def _pool_lane_tables():
    lane = lax.broadcasted_iota(jnp.int32, (1, D_POOL), 1)
    group = sum((lane >= g * POOL_GROUP).astype(jnp.int32) for g in range(1, len(POOL_WINDOWS)))
    half = jnp.where(group == 0, 1, jnp.where(group == 1, 2, jnp.where(group == 2, 4, 8)))
    return group, half


def _window_sums(v, group, offsets):
    s2 = v + _shift_rows(v, 1)
    s4 = s2 + _shift_rows(s2, 2)
    s8 = s4 + _shift_rows(s4, 4)
    s16 = s8 + _shift_rows(s8, 8)
    parts = [_shift_rows(p, -o) if o else p for p, o in zip((s2, s4, s8, s16), offsets)]
    return jnp.where(group == 0, parts[0], jnp.where(group == 1, parts[1], jnp.where(group == 2, parts[2], parts[3])))


def _pool_counts(tok, half, s):
    return (jnp.minimum(tok + half, s) - jnp.maximum(tok - half, 0)).astype(F32)


def _pool_forward(u, tok, s):
    group, half = _pool_lane_tables()
    sums = _window_sums(u, group, [w // 2 - 1 for w in POOL_WINDOWS])
    return sums / _pool_counts(tok, half, s) - u


def _conv_forward(zc, cw_ref):
    return cw_ref[0:1, :] * _shift_rows(zc, 1) + cw_ref[1:2, :] * zc + cw_ref[2:3, :] * _shift_rows(zc, -1)


def _local_fwd(proj, pool_bd, pool_scale, conv_w, name):
    s = proj.shape[0]
    tm = min(s, 512)
    ctr = slice(HALO, HALO + tm)

    def body(prev_ref, cur_ref, next_ref, pw_ref, sc_ref, cw_ref, out_ref):
        i = pl.program_id(0)
        ext = jnp.concatenate([prev_ref[...], cur_ref[...], next_ref[...]], axis=0)
        tok = _token_index(i, tm)
        inside = (tok >= 0) & (tok < s)
        u = jnp.where(inside, ext[:, 0:D_POOL], 0.0)
        p = _pool_forward(u, tok, s)[ctr]
        ya = _dot(p.astype(BF16), pw_ref[...]) * sc_ref[...]
        gb = ext[:, D_POOL:D_POOL + D_CONV]
        zc = jnp.where(inside, ext[:, D_POOL + D_CONV:D_POOL + 2 * D_CONV] * ext[:, D_POOL + 2 * D_CONV:D_LOC], 0.0)
        yb = (gb * _conv_forward(zc, cw_ref))[ctr]
        out_ref[...] = jnp.concatenate([ya, yb], axis=1)

    return pl.pallas_call(
        body, name=name, grid=(s // tm,),
        in_specs=_halo_specs(tm, s, D_LOC, 0) + [
            pl.BlockSpec((D_POOL, D_POOL), lambda i: (0, 0)), pl.BlockSpec((1, D_POOL), lambda i: (0, 0)),
            pl.BlockSpec((3, D_CONV), lambda i: (0, 0))],
        out_specs=pl.BlockSpec((tm, D_POOL + D_CONV), lambda i: (i, 0)),
        out_shape=jax.ShapeDtypeStruct((s, D_POOL + D_CONV), F32),
        compiler_params=_params())(proj, proj, proj, pool_bd, pool_scale, conv_w)


def _local_bwd(proj, dmix, pool_bd, pool_scale, conv_w, name):
    s = proj.shape[0]
    tm = min(s, 512)
    ctr = slice(HALO, HALO + tm)

    def body(prev_ref, cur_ref, next_ref, dprev_ref, dcur_ref, dnext_ref, pw_ref, sc_ref, cw_ref,
             dloc_ref, dpw_ref, dsc_ref, dcw_ref):
        i = pl.program_id(0)
        first = i == 0
        ext = jnp.concatenate([prev_ref[...], cur_ref[...], next_ref[...]], axis=0)
        dext = jnp.concatenate([dprev_ref[...], dcur_ref[...], dnext_ref[...]], axis=0)
        tok = _token_index(i, tm)
        inside = (tok >= 0) & (tok < s)
        group, half = _pool_lane_tables()
        cnt = _pool_counts(tok, half, s)
        u = jnp.where(inside, ext[:, 0:D_POOL], 0.0)
        dya = jnp.where(inside, dext[:, 0:D_POOL], 0.0)
        p_c = _pool_forward(u, tok, s)[ctr].astype(BF16)
        lin = _dot(p_c, pw_ref[...])
        _accumulate(dsc_ref, jnp.sum(dya[ctr] * lin, axis=0, keepdims=True), first)
        e1 = (dya * sc_ref[...]).astype(BF16)
        _accumulate(dpw_ref, _dot_tn(p_c, e1[ctr]), first)
        dp = _dot_nt(e1, pw_ref[...])
        du = _window_sums(dp / cnt, group, [w // 2 for w in POOL_WINDOWS]) - dp
        gb = ext[:, D_POOL:D_POOL + D_CONV]
        gc = ext[:, D_POOL + D_CONV:D_POOL + 2 * D_CONV]
        hv = ext[:, D_POOL + 2 * D_CONV:D_LOC]
        zc = jnp.where(inside, gc * hv, 0.0)
        dyb = jnp.where(inside, dext[:, D_POOL:D_POOL + D_CONV], 0.0)
        dgb = dyb * _conv_forward(zc, cw_ref)
        dyc = dyb * gb
        for k in range(3):
            part = jnp.sum(dyc[ctr] * _shift_rows(zc, 1 - k)[ctr], axis=0, keepdims=True)
            _accumulate(dcw_ref.at[k:k + 1, :], part, first)
        dzc = cw_ref[0:1, :] * _shift_rows(dyc, -1) + cw_ref[1:2, :] * dyc + cw_ref[2:3, :] * _shift_rows(dyc, 1)
        dloc = jnp.concatenate([du, dgb, dzc * hv, dzc * gc], axis=1)
        dloc_ref[...] = dloc[ctr].astype(BF16)

    return pl.pallas_call(
        body, name=name, grid=(s // tm,),
        in_specs=_halo_specs(tm, s, D_LOC, 0) + _halo_specs(tm, s, D_POOL + D_CONV, 0) + [
            pl.BlockSpec((D_POOL, D_POOL), lambda i: (0, 0)), pl.BlockSpec((1, D_POOL), lambda i: (0, 0)),
            pl.BlockSpec((3, D_CONV), lambda i: (0, 0))],
        out_specs=[pl.BlockSpec((tm, D_LOC), lambda i: (i, 0)), pl.BlockSpec((D_POOL, D_POOL), lambda i: (0, 0)),
                   pl.BlockSpec((1, D_POOL), lambda i: (0, 0)), pl.BlockSpec((8, D_CONV), lambda i: (0, 0))],
        out_shape=[jax.ShapeDtypeStruct((s, D_LOC), BF16), jax.ShapeDtypeStruct((D_POOL, D_POOL), F32),
                   jax.ShapeDtypeStruct((1, D_POOL), F32), jax.ShapeDtypeStruct((8, D_CONV), F32)],
        compiler_params=_params())(proj, proj, proj, dmix, dmix, dmix, pool_bd, pool_scale, conv_w)


def _na_geometry(rows):
    n_j = rows // Q_ROWS
    dr = np.full((3, Q_ROWS, K_ROWS), 2 * NA_ROWS - 1, np.int64)
    for t, j in enumerate((0, min(1, n_j - 1), n_j - 1)):
        base = int(np.clip(Q_ROWS * j - NA_ROWS // 2, 0, rows - K_ROWS))
        for qr in range(Q_ROWS):
            r = Q_ROWS * j + qr
            start = int(np.clip(r - NA_ROWS // 2, 0, rows - NA_ROWS))
            for kr in range(K_ROWS):
                if start <= base + kr < start + NA_ROWS:
                    dr[t, qr, kr] = base + kr - r + NA_ROWS - 1
    return dr


def _na_col_tables():
    c = np.arange(GRID_W)
    start = np.clip(c - NA_COLS // 2, 0, GRID_W - NA_COLS)
    valid = (c[None, :] >= start[:, None]) & (c[None, :] < start[:, None] + NA_COLS)
    dc = np.clip(c[None, :] - c[:, None], -(NA_COLS - 1), NA_COLS - 1) + (NA_COLS - 1)
    return valid, dc


NO_ROW = 2 * NA_ROWS - 1
N_SLOT = 2 * NA_ROWS


def _na_tiles(rpb):
    valid, dc = _na_col_tables()
    onehot = jnp.asarray((dc[None] == np.arange(2 * NA_COLS - 1)[:, None, None]).astype(np.float32))
    table = jnp.einsum("hrd,dqk->hrqk", rpb, onehot, precision=lax.Precision.HIGHEST)
    table = jnp.where(jnp.asarray(valid)[None, None], table, NEG_INF)
    outside = jnp.full((NA_HEADS, 1, GRID_W, GRID_W), NEG_INF, F32)
    padded = jnp.concatenate([outside, table, outside], axis=1)
    pairs = jnp.concatenate([padded[:, :N_SLOT], padded[:, 1:]], axis=-1)
    return jnp.concatenate([pairs, jnp.full((NA_HEADS, 1, GRID_W, 2 * GRID_W), NEG_INF, F32)], axis=1)


G_ROWS = 2
N_GRP = Q_ROWS // G_ROWS
G_TOK = G_ROWS * GRID_W
GK_ROWS = NA_ROWS + G_ROWS
GK_TOK = GK_ROWS * GRID_W
STACK_TOK = N_GRP * 2 * G_TOK


def _na_group_tables(rows):
    dr = _na_geometry(rows)
    koff = np.zeros((3, N_GRP), np.int64)
    slot = np.zeros((3, N_GRP, G_ROWS, GK_ROWS // 2), np.int64)
    even_in, odd_in = np.zeros_like(slot), np.zeros_like(slot)
    for t in range(3):
        for g in range(N_GRP):
            qrs = range(G_ROWS * g, G_ROWS * (g + 1))
            inside = [kr for kr in range(K_ROWS) if any(dr[t, qr, kr] != NO_ROW for qr in qrs)]
            lo, hi = min(inside), max(inside) + 1
            off = min(lo - lo % 2, K_ROWS - GK_ROWS)
            assert off <= lo and hi <= off + GK_ROWS
            koff[t, g] = off
            for qq, qr in enumerate(qrs):
                for kp in range(GK_ROWS // 2):
                    even, odd = int(dr[t, qr, off + 2 * kp]), int(dr[t, qr, off + 2 * kp + 1])
                    even_in[t, g, qq, kp], odd_in[t, g, qq, kp] = even != NO_ROW, odd != NO_ROW
                    slot[t, g, qq, kp] = (N_SLOT if even == NO_ROW and odd == NO_ROW
                                          else (even if even != NO_ROW else odd - 1) + 1)
    return koff, slot, even_in, odd_in


def _by_type(block_type, per_type):
    a, b, c = (int(v) for v in per_type)
    if a == b == c:
        return a
    return jnp.where(block_type == 0, a, jnp.where(block_type == 2, c, b))


def _score_rows(g, hh):
    first = (2 * g + hh) * G_TOK
    return slice(first, first + G_TOK)


def _tile_at(g, hh, qq, kp):
    first = _score_rows(g, hh).start + qq * GRID_W
    return slice(first, first + GRID_W), slice(kp * 2 * GRID_W, (kp + 1) * 2 * GRID_W)


def _fill_bias(bias_s, tiles_ref, block_type, tables):
    _, slot, even_in, odd_in = tables
    left = lax.broadcasted_iota(jnp.int32, (1, 2 * GRID_W), 1) < GRID_W
    for hh in range(2):
        for g in range(N_GRP):
            for qq in range(G_ROWS):
                for kp in range(GK_ROWS // 2):
                    tile = tiles_ref[hh, _by_type(block_type, slot[:, g, qq, kp])]
                    tile = jnp.where(left & (_by_type(block_type, even_in[:, g, qq, kp]) == 0), NEG_INF, tile)
                    tile = jnp.where(jnp.logical_not(left) & (_by_type(block_type, odd_in[:, g, qq, kp]) == 0), NEG_INF, tile)
                    rs, cs = _tile_at(g, hh, qq, kp)
                    bias_s[rs, cs] = tile


def _group_offset(block_type, koff, g):
    off = _by_type(block_type, koff[:, g]) * GRID_W
    return off if isinstance(off, int) else pl.multiple_of(off, 2 * GRID_W)


def _na_specs(s, proj_cols):
    n_blk = s // K_BLK
    per = Q_TOK // K_BLK

    def kv_spec(col0, m):
        return pl.BlockSpec((K_BLK, HEAD_PAIR), lambda hp, j: (jnp.clip(per * j - 1, 0, n_blk - 4) + m, col0 + hp))

    q_col, k_col, v_col = (c // HEAD_PAIR for c in proj_cols)
    return ([pl.BlockSpec((Q_TOK, HEAD_PAIR), lambda hp, j: (j, q_col + hp))]
            + [kv_spec(k_col, m) for m in range(4)] + [kv_spec(v_col, m) for m in range(4)])


def _na_block_type(j, n_j):
    return jnp.where(j == 0, 0, jnp.where(j == n_j - 1, 2, 1))


def _head_masks():
    lane = lax.broadcasted_iota(jnp.int32, (1, HEAD_PAIR), 1)
    return [lane < NA_HEAD_DIM, lane >= NA_HEAD_DIM]


def _attn_fwd(qkv, tiles, name):
    s = qkv.shape[0]
    n_j = s // Q_TOK
    tables = _na_group_tables(s // GRID_W)
    koff = tables[0]

    def body(q_ref, k0, k1, k2, k3, v0, v1, v2, v3, tiles_ref, o_ref, lse_ref, bias_s, k_s, vh_s, sc_s, p_s):
        j = pl.program_id(1)
        block_type = _na_block_type(j, n_j)
        pl.when((j == 0) | (j == 1) | (j == n_j - 1))(functools.partial(_fill_bias, bias_s, tiles_ref, block_type, tables))
        masks = _head_masks()
        for m, (kr, vr) in enumerate(zip((k0, k1, k2, k3), (v0, v1, v2, v3))):
            rows = slice(m * K_BLK, (m + 1) * K_BLK)
            k_s[rows, :] = kr[...]
            v = vr[...]
            for hh, mask in enumerate(masks):
                vh_s[hh, rows, :] = jnp.where(mask, v, jnp.zeros_like(v))
        q = q_ref[...]
        qh = [jnp.where(mask, q, jnp.zeros_like(q)) for mask in masks]
        offs = [_group_offset(block_type, koff, g) for g in range(N_GRP)]
        for g in range(N_GRP):
            kg = k_s[pl.ds(offs[g], GK_TOK), :]
            for hh in range(2):
                sc_s[_score_rows(g, hh), :] = _dot_nt(qh[hh][g * G_TOK:(g + 1) * G_TOK], kg)
        sc = sc_s[...] + bias_s[...]
        mx = jnp.max(sc, axis=-1, keepdims=True)
        p = jnp.exp(sc - mx)
        den = jnp.sum(p, axis=-1, keepdims=True)
        p_s[...] = p.astype(BF16)
        inv = 1.0 / den
        lse = mx + jnp.log(den)
        for g in range(N_GRP):
            rows = slice(g * G_TOK, (g + 1) * G_TOK)
            out = jnp.zeros((G_TOK, HEAD_PAIR), F32)
            for hh in range(2):
                sr = _score_rows(g, hh)
                out = out + _dot(p_s[sr, :], vh_s[hh, pl.ds(offs[g], GK_TOK), :]) * inv[sr]
            o_ref[rows, :] = out
            lse_ref[0, rows, :] = jnp.where(masks[0], lse[_score_rows(g, 0)], lse[_score_rows(g, 1)])

    return pl.pallas_call(
        body, name=name, grid=(NA_HEADS // 2, n_j),
        in_specs=_na_specs(s, (0, D_NA, 2 * D_NA)) + [
            pl.BlockSpec((2, N_SLOT + 1, GRID_W, 2 * GRID_W), lambda hp, j: (hp, 0, 0, 0))],
        out_specs=[pl.BlockSpec((Q_TOK, HEAD_PAIR), lambda hp, j: (j, hp)),
                   pl.BlockSpec((1, Q_TOK, HEAD_PAIR), lambda hp, j: (hp, j, 0))],
        out_shape=[jax.ShapeDtypeStruct((s, D_NA), F32), jax.ShapeDtypeStruct((NA_HEADS // 2, s, HEAD_PAIR), F32)],
        scratch_shapes=[pltpu.VMEM((STACK_TOK, GK_TOK), F32), pltpu.VMEM((K_TOK, HEAD_PAIR), BF16),
                        pltpu.VMEM((2, K_TOK, HEAD_PAIR), BF16), pltpu.VMEM((STACK_TOK, GK_TOK), F32),
                        pltpu.VMEM((STACK_TOK, GK_TOK), BF16)],
        compiler_params=_params())(*([qkv] * 9), tiles)


def _add_tiles(dtile_ref, ds_ref, block_type, slot, has_interior):
    def tile(g, hh, qq, kp):
        rs, cs = _tile_at(g, hh, qq, kp)
        return ds_ref[rs, cs].astype(F32)

    def interior():
        for hh in range(2):
            for qq in range(G_ROWS):
                for kp in range(GK_ROWS // 2):
                    assert (slot[1, :, qq, kp] == slot[1, 0, qq, kp]).all()
                    if slot[1, 0, qq, kp] != N_SLOT:
                        dtile_ref[hh, int(slot[1, 0, qq, kp])] += sum(tile(g, hh, qq, kp) for g in range(N_GRP))

    def edge():
        for hh in range(2):
            for g in range(N_GRP):
                for qq in range(G_ROWS):
                    for kp in range(GK_ROWS // 2):
                        first, last = (0 if e == N_SLOT else int(e) for e in slot[[0, 2], g, qq, kp])
                        if (slot[[0, 2], g, qq, kp] != N_SLOT).any():
                            dtile_ref[hh, _by_type(block_type, (first, first, last))] += tile(g, hh, qq, kp)

    if has_interior:
        pl.when(block_type == 1)(interior)
    pl.when(block_type != 1)(edge)


def _attn_bwd(qkv, tiles, o, dmix, lse, name, job=None):
    s = qkv.shape[0]
    n_j = s // Q_TOK
    n_blk = s // K_BLK
    per = Q_TOK // K_BLK
    scale = NA_HEAD_DIM ** -0.5
    do_col = (D_POOL + D_CONV) // HEAD_PAIR
    tables = _na_group_tables(s // GRID_W)
    koff, slot = tables[0], tables[1]

    def body(q_ref, k0, k1, k2, k3, v0, v1, v2, v3, tiles_ref, o_ref, do_ref, lse_ref,
             dq_ref, dk_ref, dv_ref, dtile_ref, bias_s, k_s, kh_s, v_s, s_s, dp_s, pb_s, dsb_s):
        j = pl.program_id(1)

        @pl.when(j == 0)
        def _():
            dk_ref[...] = jnp.zeros_like(dk_ref)
            dv_ref[...] = jnp.zeros_like(dv_ref)
            dtile_ref[...] = jnp.zeros_like(dtile_ref)

        block_type = _na_block_type(j, n_j)
        pl.when((j == 0) | (j == 1) | (j == n_j - 1))(functools.partial(_fill_bias, bias_s, tiles_ref, block_type, tables))
        base = pl.multiple_of(jnp.clip(per * j - 1, 0, n_blk - 4) * K_BLK, K_BLK)
        masks = _head_masks()
        for m, (kr, vr) in enumerate(zip((k0, k1, k2, k3), (v0, v1, v2, v3))):
            rows = slice(m * K_BLK, (m + 1) * K_BLK)
            k = kr[...]
            k_s[rows, :] = k
            v_s[rows, :] = vr[...]
            for hh, mask in enumerate(masks):
                kh_s[hh, rows, :] = jnp.where(mask, k, jnp.zeros_like(k))
        q = q_ref[...]
        qh = [jnp.where(mask, q, jnp.zeros_like(q)) for mask in masks]
        lane = lax.broadcasted_iota(jnp.int32, (1, HEAD_PAIR), 1)
        offs = [_group_offset(block_type, koff, g) for g in range(N_GRP)]
        do, ov, lse = do_ref[...], o_ref[...], lse_ref[0]
        dob, lse_col, delta_col = {}, [], []
        for g in range(N_GRP):
            rows = slice(g * G_TOK, (g + 1) * G_TOK)
            kg = k_s[pl.ds(offs[g], GK_TOK), :]
            vg = v_s[pl.ds(offs[g], GK_TOK), :]
            for hh, mask in enumerate(masks):
                doh = jnp.where(mask, do[rows], 0.0)
                dob[g, hh] = doh.astype(BF16)
                lse_col.append(jnp.sum(jnp.where(lane == hh * NA_HEAD_DIM, lse[rows], 0.0), axis=-1, keepdims=True))
                delta_col.append(jnp.sum(doh * ov[rows], axis=-1, keepdims=True))
                s_s[_score_rows(g, hh), :] = _dot_nt(qh[hh][rows], kg)
                dp_s[_score_rows(g, hh), :] = _dot_nt(dob[g, hh], vg)
        p = jnp.exp(s_s[...] + bias_s[...] - jnp.concatenate(lse_col, axis=0))
        ds = p * (dp_s[...] - jnp.concatenate(delta_col, axis=0))
        pb_s[...] = p.astype(BF16)
        dsb_s[...] = ds.astype(BF16)
        _add_tiles(dtile_ref, dsb_s, block_type, slot, n_j > 2)
        for g in range(N_GRP):
            rows = slice(g * G_TOK, (g + 1) * G_TOK)
            dq = jnp.zeros((G_TOK, HEAD_PAIR), F32)
            dk = jnp.zeros((GK_TOK, HEAD_PAIR), F32)
            dv = jnp.zeros((GK_TOK, HEAD_PAIR), F32)
            for hh in range(2):
                sr = _score_rows(g, hh)
                dsb = dsb_s[sr, :]
                dq = dq + _dot(dsb, kh_s[hh, pl.ds(offs[g], GK_TOK), :])
                dk = dk + _dot_tn(dsb, qh[hh][rows])
                dv = dv + _dot_tn(pb_s[sr, :], dob[g, hh])
            dq_ref[rows, :] = (dq * scale).astype(BF16)
            at = pl.multiple_of(base + offs[g], 2 * GRID_W)
            dk_ref[pl.ds(at, GK_TOK), :] += dk
            dv_ref[pl.ds(at, GK_TOK), :] += dv

    pair = pl.BlockSpec((Q_TOK, HEAD_PAIR), lambda hp, j: (j, hp))
    whole = pl.BlockSpec((s, HEAD_PAIR), lambda hp, j: (0, hp))
    call = _riding_call(
        body, job, 13, 4, (NA_HEADS // 2) * n_j, lambda: pl.program_id(0) * n_j + pl.program_id(1),
        name=name, grid=(NA_HEADS // 2, n_j),
        in_specs=_na_specs(s, (0, D_NA, 2 * D_NA)) + [
            pl.BlockSpec((2, N_SLOT + 1, GRID_W, 2 * GRID_W), lambda hp, j: (hp, 0, 0, 0)),
            pair, pl.BlockSpec((Q_TOK, HEAD_PAIR), lambda hp, j: (j, do_col + hp)),
            pl.BlockSpec((1, Q_TOK, HEAD_PAIR), lambda hp, j: (hp, j, 0))],
        out_specs=[pair, whole, whole, pl.BlockSpec((2, N_SLOT, GRID_W, 2 * GRID_W), lambda hp, j: (hp, 0, 0, 0))],
        out_shape=[jax.ShapeDtypeStruct((s, D_NA), BF16), jax.ShapeDtypeStruct((s, D_NA), F32),
                   jax.ShapeDtypeStruct((s, D_NA), F32),
                   jax.ShapeDtypeStruct((NA_HEADS, N_SLOT, GRID_W, 2 * GRID_W), F32)],
        scratch_shapes=[pltpu.VMEM((STACK_TOK, GK_TOK), F32), pltpu.VMEM((K_TOK, HEAD_PAIR), BF16),
                        pltpu.VMEM((2, K_TOK, HEAD_PAIR), BF16), pltpu.VMEM((K_TOK, HEAD_PAIR), BF16),
                        pltpu.VMEM((STACK_TOK, GK_TOK), F32), pltpu.VMEM((STACK_TOK, GK_TOK), F32),
                        pltpu.VMEM((STACK_TOK, GK_TOK), BF16), pltpu.VMEM((STACK_TOK, GK_TOK), BF16)],
        compiler_params=_params())
    return call(*([qkv] * 9), tiles, o, dmix, lse)


def _rpb_finish(tiles, name):
    valid, dc = _na_col_tables()
    n_dc = 2 * NA_COLS - 1
    sel = np.zeros((GRID_W, 2 * GRID_W, LANES), np.float32)
    for qc in range(GRID_W):
        for kc in range(GRID_W):
            if valid[qc, kc]:
                sel[qc, kc, dc[qc, kc]] = 1.0
                sel[qc, GRID_W + kc, LANES // 2 + dc[qc, kc]] = 1.0
    sel = jnp.asarray(sel.reshape(GRID_W * 2 * GRID_W, LANES))
    flat = tiles.reshape(NA_HEADS * 2 * NA_ROWS, GRID_W * 2 * GRID_W)

    def body(a_ref, b_ref, out_ref):
        out_ref[...] = jnp.dot(a_ref[...], b_ref[...], preferred_element_type=F32, precision=lax.Precision.HIGHEST)

    sums = pl.pallas_call(
        body, name=name, out_shape=jax.ShapeDtypeStruct((flat.shape[0], LANES), F32),
        compiler_params=_params())(flat, sel).reshape(NA_HEADS, 2 * NA_ROWS, LANES)
    return sums[:, 1:, :n_dc] + sums[:, :2 * NA_ROWS - 1, LANES // 2:LANES // 2 + n_dc]


def _loss_grad(y, target, name):
    s, d = y.shape
    tm = min(s, 1024)

    def body(y_ref, t_ref, sum_ref, dy_ref):
        diff = y_ref[...] - t_ref[...]
        dy_ref[...] = diff * (1.0 / d)
        part = jnp.zeros((8, LANES), F32) + jnp.sum(diff * diff)
        _accumulate(sum_ref, part, pl.program_id(0) == 0)

    row = pl.BlockSpec((tm, d), lambda i: (i, 0))
    return pl.pallas_call(
        body, name=name, grid=(s // tm,), in_specs=[row, row],
        out_specs=[pl.BlockSpec((8, LANES), lambda i: (0, 0)), row],
        out_shape=[jax.ShapeDtypeStruct((8, LANES), F32), jax.ShapeDtypeStruct((s, d), F32)],
        compiler_params=_params())(y, target)


def _adamw(w, g, m, v, name):
    rows, cols = w.shape
    tr = _row_tile(rows, 512, 8)

    def body(w_ref, g_ref, m_ref, v_ref, d_ref, nm_ref, nv_ref):
        gv = g_ref[...]
        nm = ADAM_B1 * m_ref[...] + (1.0 - ADAM_B1) * gv
        nv = ADAM_B2 * v_ref[...] + (1.0 - ADAM_B2) * (gv * gv)
        m_hat = nm / (1.0 - ADAM_B1 ** ADAM_STEP)
        v_hat = nv / (1.0 - ADAM_B2 ** ADAM_STEP)
        d_ref[...] = -ADAM_LR * (m_hat / (jnp.sqrt(v_hat) + ADAM_EPS) + ADAM_WD * w_ref[...])
        nm_ref[...] = nm
        nv_ref[...] = nv

    blk = pl.BlockSpec((tr, cols), lambda r: (r, 0))
    return pl.pallas_call(
        body, name=name, grid=(rows // tr,), in_specs=[blk] * 4, out_specs=[blk] * 3,
        out_shape=[jax.ShapeDtypeStruct((rows, cols), F32)] * 3, compiler_params=_params())(w, g, m, v)


def _adamw_nd(w, g, m, v, name):
    shape = w.shape
    flat = lambda t: t.reshape(-1, shape[-1])
    return tuple(t.reshape(shape) for t in _adamw(flat(w), flat(g), flat(m), flat(v), name))


def _pack(parts, rows_mult=64):
    flat = jnp.concatenate([p.reshape(-1).astype(F32) for p in parts])
    per = LANES * rows_mult
    total = -(-flat.shape[0] // per) * per
    return jnp.pad(flat, (0, total - flat.shape[0])).reshape(-1, LANES)


def _unpack(packed, shapes):
    flat = packed.reshape(-1)
    out, pos = [], 0
    for shp in shapes:
        n = int(np.prod(shp))
        out.append(flat[pos:pos + n].reshape(shp))
        pos += n
    return out


def kernel(x, ffn1_w_gate, ffn1_w_up, ffn1_w_down, ffn2_w_gate, ffn2_w_up, ffn2_w_down, w_in, pool_w, pool_scale, conv_w, rpb, w_out, ln_g, ln_b, loss_target, m_ffn1_w_gate, m_ffn1_w_up, m_ffn1_w_down, m_ffn2_w_gate, m_ffn2_w_up, m_ffn2_w_down, m_w_in, m_pool_w, m_pool_scale, m_conv_w, m_rpb, m_w_out, m_ln_g, m_ln_b, v_ffn1_w_gate, v_ffn1_w_up, v_ffn1_w_down, v_ffn2_w_gate, v_ffn2_w_up, v_ffn2_w_down, v_w_in, v_pool_w, v_pool_scale, v_conv_w, v_rpb, v_w_out, v_ln_g, v_ln_b):
    n_l, d, fs = ffn1_w_gate.shape
    s = x.shape[1]
    rows = s // GRID_W
    assert x.shape[0] == 1 and s % Q_TOK == 0 and rows >= K_ROWS and fs % BF16_ROWS == 0
    alpha = (2.0 * n_l) ** 0.25
    xi, yi, ci = _mesh_pos()
    me = 4 * xi + 2 * yi + ci
    core = jnp.reshape(ci, (1,)).astype(jnp.int32)
    ln_w, cw_w = ln_g.shape[2], conv_w.shape[2]

    tr = lambda w: jnp.swapaxes(w, 1, 2)
    ffn1_shard = jnp.stack([tr(ffn1_w_gate), tr(ffn1_w_up), ffn1_w_down], axis=1).astype(BF16)
    ffn2_shard = jnp.stack([tr(ffn2_w_gate), tr(ffn2_w_up), ffn2_w_down], axis=1).astype(BF16)
    win_shard, wout_shard = tr(w_in).astype(BF16), w_out.astype(BF16)
    small_shard = _pack([ln_g, ln_b, conv_w])
    w_ffn1, small = _exchange_alone(_Gather([ffn1_shard[0], small_shard]), "gather_first")
    n_ln = n_l * 3 * ln_w
    small = small.reshape(N_DEV, -1)
    unshard = lambda t, width: jnp.moveaxis(t.reshape(N_DEV, n_l, 3, width), 0, 2).reshape(n_l, 3, N_DEV * width)
    ln_g_all = unshard(small[:, :n_ln], ln_w)
    ln_b_all = unshard(small[:, n_ln:2 * n_ln], ln_w)
    conv_all = unshard(small[:, 2 * n_ln:2 * n_ln + n_l * 3 * cw_w], cw_w)
    pool_bd = jnp.zeros((n_l, D_POOL, D_POOL), F32)
    for g in range(len(POOL_WINDOWS)):
        sl = slice(g * POOL_GROUP, (g + 1) * POOL_GROUP)
        pool_bd = pool_bd.at[:, sl, sl].set(pool_w[:, g])
    pool_bd = pool_bd.astype(BF16)
    lnp = lambda arr, l, j: arr[l, j].reshape(1, d)

    saved = []
    h = x.reshape(s, d)
    for l in range(n_l):
        a1, u1, h1, z1, x1, w_in_l, w_out_l, w_ffn2 = _ffn_fwd(
            h, w_ffn1, lnp(ln_g_all, l, 0), lnp(ln_b_all, l, 0), alpha, f"ffn1_fwd_{l}",
            job=_Gather([win_shard[l], wout_shard[l], ffn2_shard[l]]))
        proj = _win_fwd(x1, w_in_l, f"win_fwd_{l}")
        bias = _na_tiles(rpb[l])
        yab = _local_fwd(proj[0], pool_bd[l], pool_scale[l].reshape(1, D_POOL), conv_all[l], f"local_fwd_{l}")
        yc, lse = _attn_fwd(proj[1], bias, f"attn_fwd_{l}")
        z2, x2 = _wout_fwd(x1, yab, yc, w_out_l, lnp(ln_g_all, l, 1), lnp(ln_b_all, l, 1), alpha, f"wout_fwd_{l}")
        a2, u2, h2, z3, x3, *w_next = _ffn_fwd(
            x2, w_ffn2, lnp(ln_g_all, l, 2), lnp(ln_b_all, l, 2), alpha, f"ffn2_fwd_{l}",
            job=_Gather([ffn1_shard[l + 1]]) if l + 1 < n_l else None)
        saved.append((h, a1, u1, h1, z1, x1, proj, bias, yab, yc, lse, z2, x2, a2, u2, h2, z3, w_ffn1, w_in_l, w_out_l, w_ffn2))
        h = x3
        if w_next:
            w_ffn1 = w_next[0]

    sq, dh = _loss_grad(h, loss_target.reshape(s, d), "loss_head")
    loss = lax.psum(sq[0, 0] * (0.5 / d), MESH_AXES)

    flat = lambda blocks: [b.reshape(N_DEV, -1, d) for b in blocks]
    pair_add = lambda blocks, got, tag: [_pair_add(b, g, core, f"grads_pair_add_{tag}_{i}")
                                         for i, (b, g) in enumerate(zip(blocks, got))]
    small_grads = [None] * n_l
    reduced = [None] * n_l
    above = None
    for l in reversed(range(n_l)):
        x0, a1, u1, h1, z1, x1, proj, bias, yab, yc, lse, z2, x2, a2, u2, h2, z3, w_ffn1, w_in_l, w_out_l, w_ffn2 = saved[l]
        dx2, da, du, df, dg3, db3, *got = _ffn_bwd_dx(
            dh, z3, a2, u2, w_ffn2, lnp(ln_g_all, l, 2), alpha, f"ffn2_bwd_dx_{l}",
            job=_PairExchange(above) if above else None)
        above_pairs = pair_add(above, got, f"mix_{l + 1}") if above else None
        g2 = flat([_ffn_bwd_dwd(h2, df, _ffn_bwd_dwgu(da, du, x2, fs, f"ffn2_bwd_dwgu_{l}"), f"ffn2_bwd_dwd_{l}")])
        dmix, dxp, dg2, db2, g_out, *got = _wout_bwd(dx2, z2, yab, yc, w_out_l, lnp(ln_g_all, l, 1), alpha,
                                                     f"wout_bwd_{l}", job=_PairExchange(g2))
        p2 = pair_add(g2, got, f"ffn2_{l}")
        dq, dk, dv, dtiles, *crossed = _attn_bwd(proj[1], bias, yc, dmix, lse, f"attn_bwd_{l}",
                                                 job=_ChipExchange(above_pairs) if above else None)
        if above:
            reduced[l + 1] += crossed
        dloc, dpw, dsc, dcw = _local_bwd(proj[0], dmix, pool_bd[l], pool_scale[l].reshape(1, D_POOL), conv_all[l],
                                         f"local_bwd_{l}")
        dx1, g_in = _win_bwd(dxp, dloc, dq, dk, dv, x1, w_in_l, f"win_bwd_{l}")
        dx0, da, du, df, dg1, db1, *crossed = _ffn_bwd_dx(dx1, z1, a1, u1, w_ffn1, lnp(ln_g_all, l, 0), alpha,
                                                          f"ffn1_bwd_dx_{l}", job=_ChipExchange(p2))
        reduced[l] = list(crossed)
        g1 = _ffn_bwd_dwd(h1, df, _ffn_bwd_dwgu(da, du, x0, fs, f"ffn1_bwd_dwgu_{l}"), f"ffn1_bwd_dwd_{l}")
        above = flat([g_out, g_in, g1])
        drpb = _rpb_finish(dtiles, f"rpb_finish_{l}")
        dpool = jnp.stack([dpw[g * POOL_GROUP:(g + 1) * POOL_GROUP, g * POOL_GROUP:(g + 1) * POOL_GROUP]
                           for g in range(len(POOL_WINDOWS))])
        small_grads[l] = (jnp.concatenate([dg1, dg2, dg3]), jnp.concatenate([db1, db2, db3]), dcw[0:3], dpool, dsc[0], drpb)
        dh = dx0
    grad_x = dh.reshape(x.shape)

    last_pairs = pair_add(above, _exchange_alone(_PairExchange(above), "grads_pair_exchange_last"), "mix_0")
    reduced[0] += _exchange_alone(_ChipExchange(last_pairs), "grads_chip_exchange_last")
    sums = [[_sum_blocks(q, f"grads_chip_sum_{l}_{i}") for i, q in enumerate(reduced[l])] for l in range(n_l)]
    r_ffn2, r_out, r_in, r_ffn1 = [jnp.stack([sums[l][i] for l in range(n_l)]) for i in range(4)]
    r_ffn1, r_ffn2 = r_ffn1.reshape(n_l, 3, fs, d), r_ffn2.reshape(n_l, 3, fs, d)
    grads = {
        "ffn1_w_gate": tr(r_ffn1[:, 0]), "ffn1_w_up": tr(r_ffn1[:, 1]), "ffn1_w_down": r_ffn1[:, 2],
        "ffn2_w_gate": tr(r_ffn2[:, 0]), "ffn2_w_up": tr(r_ffn2[:, 1]), "ffn2_w_down": r_ffn2[:, 2],
        "w_in": tr(r_in), "w_out": r_out}

    stack = lambda k: jnp.stack([small_grads[l][k] for l in range(n_l)])
    small_shapes = [(n_l, 3, d), (n_l, 3, d), (n_l, 3, D_CONV), pool_w.shape, pool_scale.shape, rpb.shape]
    (small_all,) = _exchange_alone(_Gather([_pack([stack(k) for k in range(6)])]), "gather_small_grads")
    small_sum = _sum_blocks(small_all, "small_grads_sum")
    g_ln_g, g_ln_b, g_conv, g_pool_w, g_pool_scale, g_rpb = _unpack(small_sum, small_shapes)
    own = lambda t, width: lax.dynamic_slice_in_dim(t, me * width, width, axis=2)
    grads.update({"ln_g": own(g_ln_g, ln_w), "ln_b": own(g_ln_b, ln_w), "conv_w": own(g_conv, cw_w),
                  "pool_w": g_pool_w, "pool_scale": g_pool_scale, "rpb": g_rpb})

    weights = dict(ffn1_w_gate=ffn1_w_gate, ffn1_w_up=ffn1_w_up, ffn1_w_down=ffn1_w_down, ffn2_w_gate=ffn2_w_gate,
                   ffn2_w_up=ffn2_w_up, ffn2_w_down=ffn2_w_down, w_in=w_in, pool_w=pool_w, pool_scale=pool_scale,
                   conv_w=conv_w, rpb=rpb, w_out=w_out, ln_g=ln_g, ln_b=ln_b)
    m_in = dict(ffn1_w_gate=m_ffn1_w_gate, ffn1_w_up=m_ffn1_w_up, ffn1_w_down=m_ffn1_w_down, ffn2_w_gate=m_ffn2_w_gate,
                ffn2_w_up=m_ffn2_w_up, ffn2_w_down=m_ffn2_w_down, w_in=m_w_in, pool_w=m_pool_w, pool_scale=m_pool_scale,
                conv_w=m_conv_w, rpb=m_rpb, w_out=m_w_out, ln_g=m_ln_g, ln_b=m_ln_b)
    v_in = dict(ffn1_w_gate=v_ffn1_w_gate, ffn1_w_up=v_ffn1_w_up, ffn1_w_down=v_ffn1_w_down, ffn2_w_gate=v_ffn2_w_gate,
                ffn2_w_up=v_ffn2_w_up, ffn2_w_down=v_ffn2_w_down, w_in=v_w_in, pool_w=v_pool_w, pool_scale=v_pool_scale,
                conv_w=v_conv_w, rpb=v_rpb, w_out=v_w_out, ln_g=v_ln_g, ln_b=v_ln_b)
    names = list(weights)
    large = ["ffn1_w_gate", "ffn1_w_up", "ffn1_w_down", "ffn2_w_gate", "ffn2_w_up", "ffn2_w_down", "w_in", "w_out"]
    tiny = [n for n in names if n not in large]
    delta, new_m, new_v = {}, {}, {}
    for n in large:
        delta[n], new_m[n], new_v[n] = _adamw_nd(weights[n], grads[n], m_in[n], v_in[n], f"adamw_{n}")
    packed = [_pack([t[n] for n in tiny]) for t in (weights, grads, m_in, v_in)]
    tiny_out = _adamw(*packed, "adamw_small")
    tiny_shapes = [weights[n].shape for n in tiny]
    for res, t in zip((delta, new_m, new_v), tiny_out):
        res.update(dict(zip(tiny, _unpack(t, tiny_shapes))))

    return (loss, grad_x, *[grads[n] for n in names], *[delta[n] for n in names],
            *[new_m[n] for n in names], *[new_v[n] for n in names])
```

```python
import functools

import numpy as np
import jax
import jax.numpy as jnp
from jax import lax
from jax.experimental import pallas as pl
from jax.experimental.pallas import tpu as pltpu

F32, BF16 = jnp.float32, jnp.bfloat16
MESH = pl.DeviceIdType.MESH
N_DEV = 8
MESH_AXES = ("x", "y", "c")

LN_EPS = 1e-5
NEG_INF = -1e30
D_POOL = 256
POOL_WINDOWS = (2, 4, 8, 16)
POOL_GROUP = 64
D_CONV = 256
NA_HEADS = 8
NA_HEAD_DIM = 64
D_NA = NA_HEADS * NA_HEAD_DIM
GRID_W = 64
NA_ROWS = 8
NA_COLS = 16
D_LOC = D_POOL + 3 * D_CONV
D_MIX = D_POOL + D_CONV + D_NA
ADAM_LR, ADAM_B1, ADAM_B2, ADAM_EPS, ADAM_WD, ADAM_STEP = 0.001, 0.9, 0.999, 1e-08, 0.01, 10

VMEM_LIMIT_BYTES = 56 * 1024 * 1024
LANES = 128
BF16_ROWS = 16
HALO = 16
Q_ROWS = 8
K_ROWS = 16
Q_TOK = Q_ROWS * GRID_W
K_TOK = K_ROWS * GRID_W
K_BLK = 4 * GRID_W
HEAD_PAIR = 2 * NA_HEAD_DIM
FFN_CHUNK_DEVS = 4

NT = (((1,), (1,)), ((), ()))
TN = (((0,), (0,)), ((), ()))


def _dot(a, b):
    return jnp.dot(a, b, preferred_element_type=F32)


def _dot_nt(a, b):
    return lax.dot_general(a, b, NT, preferred_element_type=F32)


def _dot_tn(a, b):
    return lax.dot_general(a, b, TN, preferred_element_type=F32)


def _params():
    return pltpu.CompilerParams(vmem_limit_bytes=VMEM_LIMIT_BYTES)


def _row_tile(rows, pref, mult=BF16_ROWS):
    t = min(rows, pref)
    t -= t % mult
    while t > mult and rows % t:
        t -= mult
    assert t > 0 and rows % t == 0, (rows, pref)
    return t


def _mesh_pos():
    return tuple(lax.axis_index(a) for a in MESH_AXES)


def _any_spec():
    return pl.BlockSpec(memory_space=pl.ANY)


class _Gather:
    def __init__(self, shards):
        self.arrays = list(shards)
        n = len(shards)
        self.out_shape = [jax.ShapeDtypeStruct((N_DEV,) + s.shape, s.dtype) for s in shards]
        self.scratch = [pltpu.SemaphoreType.DMA((n, 7)), pltpu.SemaphoreType.DMA((n, 7)), pltpu.SemaphoreType.DMA((n,))]

    def phases(self, ins, outs, sems):
        n = len(ins)
        send_sems, recv_sems, local_sems = sems
        x, y, c = _mesh_pos()
        me, sibling = (x, y, c), (x, y, 1 - c)
        chips = [(1 - x, y), (x, 1 - y), (1 - x, 1 - y)]

        def copy(a, k, block, to, src=None):
            dst = outs[a].at[4 * block[0] + 2 * block[1] + block[2]]
            return pltpu.make_async_remote_copy(
                src_ref=dst if src is None else src, dst_ref=dst,
                send_sem=send_sems.at[a, k], recv_sem=recv_sems.at[a, k],
                device_id=to, device_id_type=MESH)

        def mine():
            return [pltpu.make_async_copy(ins[a], outs[a].at[4 * x + 2 * y + c], local_sems.at[a]) for a in range(n)]

        def first():
            return [cp for a in range(n) for cp in
                    [copy(a, 0, me, sibling, src=ins[a])]
                    + [copy(a, 1 + j, me, (*chip, c), src=ins[a]) for j, chip in enumerate(chips)]]

        def passed():
            return [copy(a, 4 + j, (*chip, c), sibling) for j, chip in enumerate(chips) for a in range(n)]

        def start():
            for cp in mine() + first():
                cp.start()

        def middle():
            for j, chip in enumerate(chips):
                for a in range(n):
                    copy(a, 1 + j, (*chip, c), me).wait_recv()
            for cp in passed():
                cp.start()

        def finish():
            for a in range(n):
                copy(a, 0, sibling, me).wait_recv()
                for j, chip in enumerate(chips):
                    copy(a, 4 + j, (*chip, 1 - c), me).wait_recv()
            for cp in first() + passed():
                cp.wait_send()
            for cp in mine():
                cp.wait()

        return start, middle, finish


class _ChipExchange:
    def __init__(self, parts):
        self.arrays = list(parts)
        n = len(parts)
        self.out_shape = [jax.ShapeDtypeStruct(s.shape, s.dtype) for s in parts]
        self.scratch = [pltpu.SemaphoreType.DMA((n, 3)), pltpu.SemaphoreType.DMA((n, 3)), pltpu.SemaphoreType.DMA((n,))]

    def phases(self, ins, outs, sems):
        n = len(ins)
        send_sems, recv_sems, local_sems = sems
        x, y, c = _mesh_pos()
        my_chip = 2 * x + y
        chips = [(1 - x, y), (x, 1 - y), (1 - x, 1 - y)]

        def own():
            return [pltpu.make_async_copy(ins[a].at[my_chip], outs[a].at[my_chip], local_sems.at[a]) for a in range(n)]

        def copy(a, k, src_chip, dst_chip, to):
            return pltpu.make_async_remote_copy(
                src_ref=ins[a].at[src_chip], dst_ref=outs[a].at[dst_chip],
                send_sem=send_sems.at[a, k], recv_sem=recv_sems.at[a, k],
                device_id=to, device_id_type=MESH)

        def sends():
            return [copy(a, k, 2 * px + py, my_chip, (px, py, c)) for a in range(n) for k, (px, py) in enumerate(chips)]

        def start():
            for cp in own() + sends():
                cp.start()

        def finish():
            for cp in sends():
                cp.wait_send()
            for a in range(n):
                for k, (px, py) in enumerate(chips):
                    copy(a, k, my_chip, 2 * px + py, (px, py, c)).wait_recv()
            for cp in own():
                cp.wait()

        return start, None, finish


def _exchange_alone(job, name):
    n = len(job.arrays)

    def body(*refs):
        for phase in job.phases(refs[:n], refs[n:2 * n], refs[2 * n:]):
            if phase is not None:
                phase()

    return pl.pallas_call(
        body, name=name, out_shape=job.out_shape,
        in_specs=[_any_spec()] * n, out_specs=[_any_spec()] * n, scratch_shapes=job.scratch,
    )(*job.arrays)


def _riding_call(body, job, n_in, n_out, n_steps, step, **kw):
    if job is None:
        return pl.pallas_call(body, **kw)
    n_job, n_sem = len(job.arrays), len(job.scratch)
    kw = dict(kw, in_specs=list(kw["in_specs"]) + [_any_spec()] * n_job,
              out_specs=list(kw["out_specs"]) + [_any_spec()] * n_job,
              out_shape=list(kw["out_shape"]) + job.out_shape,
              scratch_shapes=list(kw.get("scratch_shapes", ())) + job.scratch)

    def riding(*refs):
        ins, job_ins = refs[:n_in], refs[n_in:n_in + n_job]
        outs = refs[n_in + n_job:n_in + n_job + n_out]
        job_outs = refs[n_in + n_job + n_out:n_in + 2 * n_job + n_out]
        scratch = refs[n_in + 2 * n_job + n_out:]
        start, middle, finish = job.phases(job_ins, job_outs, scratch[len(scratch) - n_sem:])
        now = step()
        pl.when(now == 0)(start)
        if middle is not None:
            assert n_steps >= 3
            pl.when(now == (7 * n_steps) // 8 - 1)(middle)
        body(*ins, *outs, *scratch[:len(scratch) - n_sem])
        pl.when(now == n_steps - 1)(finish)

    call = pl.pallas_call(riding, **kw)
    return lambda *args: call(*args, *job.arrays)


class _PairExchange:
    def __init__(self, slabs):
        self.arrays = list(slabs)
        n = len(slabs)
        self.out_shape = [jax.ShapeDtypeStruct((4,) + s.shape[1:], s.dtype) for s in slabs]
        self.scratch = [pltpu.SemaphoreType.DMA((n, 4)), pltpu.SemaphoreType.DMA((n, 4))]

    def phases(self, ins, outs, sems):
        n = len(ins)
        send_sems, recv_sems = sems
        x, y, c = _mesh_pos()

        def copies():
            return [pltpu.make_async_remote_copy(
                src_ref=ins[a].at[2 * j + 1 - c], dst_ref=outs[a].at[j],
                send_sem=send_sems.at[a, j], recv_sem=recv_sems.at[a, j],
                device_id=(x, y, 1 - c), device_id_type=MESH) for a in range(n) for j in range(4)]

        def start():
            for cp in copies():
                cp.start()

        def finish():
            for cp in copies():
                cp.wait_send()
            for cp in copies():
                cp.wait_recv()

        return start, None, finish


def _pair_add(slab, got, core, name):
    _, rows, d = slab.shape
    tr = _row_tile(rows, 1024)

    def body(core_ref, mine_ref, got_ref, out_ref):
        out_ref[...] = (mine_ref[...].astype(F32) + got_ref[...].astype(F32)).astype(out_ref.dtype)

    grid_spec = pltpu.PrefetchScalarGridSpec(
        num_scalar_prefetch=1, grid=(4, rows // tr),
        in_specs=[pl.BlockSpec((1, tr, d), lambda j, r, core_ref: (2 * j + core_ref[0], r, 0)),
                  pl.BlockSpec((1, tr, d), lambda j, r, core_ref: (j, r, 0))],
        out_specs=pl.BlockSpec((1, tr, d), lambda j, r, core_ref: (j, r, 0)))
    return pl.pallas_call(body, name=name, grid_spec=grid_spec,
                          out_shape=jax.ShapeDtypeStruct((4, rows, d), slab.dtype),
                          compiler_params=_params())(core, slab, got)


def _sum_blocks(parts, name):
    k, rows, d = parts.shape
    tr = _row_tile(rows, 512, BF16_ROWS if parts.dtype == BF16 else 8)

    def body(in_ref, out_ref):
        acc = in_ref[0].astype(F32)
        for j in range(1, k):
            acc = acc + in_ref[j].astype(F32)
        out_ref[...] = acc

    return pl.pallas_call(
        body, name=name, grid=(rows // tr,),
        in_specs=[pl.BlockSpec((k, tr, d), lambda r: (0, r, 0))],
        out_specs=pl.BlockSpec((tr, d), lambda r: (r, 0)),
        out_shape=jax.ShapeDtypeStruct((rows, d), F32), compiler_params=_params())(parts)


def _ln_stats(z):
    mu = jnp.mean(z, axis=-1, keepdims=True)
    zc = z - mu
    var = jnp.mean(zc * zc, axis=-1, keepdims=True)
    rstd = lax.rsqrt(var + LN_EPS)
    return zc * rstd, rstd


def _ln_bwd(dy, z, g):
    zhat, rstd = _ln_stats(z)
    dyg = dy * g
    m1 = jnp.mean(dyg, axis=-1, keepdims=True)
    m2 = jnp.mean(dyg * zhat, axis=-1, keepdims=True)
    dz = rstd * (dyg - m1 - zhat * m2)
    return dz, jnp.sum(dy * zhat, axis=0, keepdims=True), jnp.sum(dy, axis=0, keepdims=True)


def _accumulate(ref, value, first):
    @pl.when(first)
    def _():
        ref[...] = value

    @pl.when(jnp.logical_not(first))
    def _():
        ref[...] += value


def _add_matmul(acc_ref, first, matmul):
    @pl.when(first)
    def _():
        acc_ref[...] = jnp.zeros_like(acc_ref)

    acc_ref[...] += matmul()


def _ffn_weight_specs(fs, d):
    def spec(row):
        return pl.BlockSpec((N_DEV, 1, fs, d), lambda i: (0, row, 0, 0), pipeline_mode=pl.Buffered(1))
    return [spec(0), spec(1), spec(2)]


def _ffn_fwd(x, w, ln_g, ln_b, alpha, name, job=None):
    s, d = x.shape
    fs = w.shape[2]
    f = N_DEV * fs
    tm = min(s, 256)

    def body(x_ref, wg_ref, wu_ref, wd_ref, g_ref, b_ref, a_ref, u_ref, h_ref, z_ref, y_ref):
        xv = x_ref[...]
        xb = xv.astype(BF16)
        a = _dot_nt(xb, wg_ref[...].reshape(f, d))
        u = _dot_nt(xb, wu_ref[...].reshape(f, d))
        a_ref[...] = a.astype(BF16)
        u_ref[...] = u.astype(BF16)
        h = ((a * jax.nn.sigmoid(a)) * u).astype(BF16)
        h_ref[...] = h
        z = alpha * xv + 0.5 * _dot(h, wd_ref[...].reshape(f, d))
        zhat, _ = _ln_stats(z)
        z_ref[...] = z
        y_ref[...] = zhat * g_ref[...] + b_ref[...]

    row = pl.BlockSpec((tm, d), lambda i: (i, 0))
    vec = pl.BlockSpec((1, d), lambda i: (0, 0))
    hid = pl.BlockSpec((tm, f), lambda i: (i, 0))
    call = _riding_call(
        body, job, 6, 5, s // tm, lambda: pl.program_id(0),
        name=name, grid=(s // tm,),
        in_specs=[row] + _ffn_weight_specs(fs, d) + [vec, vec],
        out_specs=[hid, hid, hid, row, row],
        out_shape=[jax.ShapeDtypeStruct((s, f), BF16)] * 3 + [jax.ShapeDtypeStruct((s, d), F32)] * 2,
        compiler_params=_params())
    return call(x, w, w, w, ln_g, ln_b)


def _ffn_bwd_dx(dy, z, a, u, w, ln_g, alpha, name, job=None):
    s, d = dy.shape
    fs = w.shape[2]
    f = N_DEV * fs
    tm = min(s, 256)

    def body(dy_ref, z_ref, a_ref, u_ref, wg_ref, wu_ref, wd_ref, g_ref,
             dx_ref, da_ref, du_ref, df_ref, dg_ref, db_ref):
        i = pl.program_id(0)
        dz, dg, db = _ln_bwd(dy_ref[...], z_ref[...], g_ref[...])
        _accumulate(dg_ref, dg, i == 0)
        _accumulate(db_ref, db, i == 0)
        df = (0.5 * dz).astype(BF16)
        df_ref[...] = df
        av = a_ref[...].astype(F32)
        uv = u_ref[...].astype(F32)
        sg = jax.nn.sigmoid(av)
        dh = _dot_nt(df, wd_ref[...].reshape(f, d))
        du = (dh * (av * sg)).astype(BF16)
        da = (dh * uv * (sg * (1.0 + av * (1.0 - sg)))).astype(BF16)
        da_ref[...] = da
        du_ref[...] = du
        dx_ref[...] = alpha * dz + _dot(da, wg_ref[...].reshape(f, d)) + _dot(du, wu_ref[...].reshape(f, d))

    row = pl.BlockSpec((tm, d), lambda i: (i, 0))
    vec = pl.BlockSpec((1, d), lambda i: (0, 0))
    hid = pl.BlockSpec((tm, f), lambda i: (i, 0))
    call = _riding_call(
        body, job, 8, 6, s // tm, lambda: pl.program_id(0),
        name=name, grid=(s // tm,),
        in_specs=[row, row, hid, hid] + _ffn_weight_specs(fs, d) + [vec],
        out_specs=[row, hid, hid, row, vec, vec],
        out_shape=[jax.ShapeDtypeStruct((s, d), F32)] + [jax.ShapeDtypeStruct((s, f), BF16)] * 2
                  + [jax.ShapeDtypeStruct((s, d), BF16)] + [jax.ShapeDtypeStruct((1, d), F32)] * 2,
        compiler_params=_params())
    return call(dy, z, a, u, w, w, w, ln_g)


def _ffn_bwd_dwgu(da, du, x, fs, name):
    s, d = x.shape
    tf = FFN_CHUNK_DEVS * fs
    n_c = N_DEV // FFN_CHUNK_DEVS
    tk = min(s, 1024)
    n_k = s // tk

    def body(da_ref, du_ref, x_ref, out_ref, accg_s, accu_s):
        k = pl.program_id(1)
        xb = x_ref[...].astype(BF16)
        _add_matmul(accg_s, k == 0, lambda: _dot_tn(da_ref[...], xb))
        _add_matmul(accu_s, k == 0, lambda: _dot_tn(du_ref[...], xb))

        @pl.when(k == n_k - 1)
        def _():
            out_ref[:, 0] = accg_s[...].astype(BF16).reshape(FFN_CHUNK_DEVS, fs, d)
            out_ref[:, 1] = accu_s[...].astype(BF16).reshape(FFN_CHUNK_DEVS, fs, d)

    hid = pl.BlockSpec((tk, tf), lambda c, k: (k, c))
    return pl.pallas_call(
        body, name=name, grid=(n_c, n_k),
        in_specs=[hid, hid, pl.BlockSpec((tk, d), lambda c, k: (k, 0))],
        out_specs=pl.BlockSpec((FFN_CHUNK_DEVS, 2, fs, d), lambda c, k: (c, 0, 0, 0), pipeline_mode=pl.Buffered(1)),
        out_shape=jax.ShapeDtypeStruct((N_DEV, 3, fs, d), BF16),
        scratch_shapes=[pltpu.VMEM((tf, d), F32), pltpu.VMEM((tf, d), F32)],
        compiler_params=_params())(da, du, x)


def _ffn_bwd_dwd(h, df, blocks, name):
    s, d = df.shape
    fs = blocks.shape[2]
    tf = FFN_CHUNK_DEVS * fs
    n_c = N_DEV // FFN_CHUNK_DEVS
    tk = min(s, 1024)
    n_k = s // tk

    def body(h_ref, df_ref, blocks_ref, out_ref, acc_s):
        k = pl.program_id(1)
        _add_matmul(acc_s, k == 0, lambda: _dot_tn(h_ref[...], df_ref[...]))

        @pl.when(k == n_k - 1)
        def _():
            out_ref[:, 0] = acc_s[...].astype(BF16).reshape(FFN_CHUNK_DEVS, fs, d)

    return pl.pallas_call(
        body, name=name, grid=(n_c, n_k),
        in_specs=[pl.BlockSpec((tk, tf), lambda c, k: (k, c)), pl.BlockSpec((tk, d), lambda c, k: (k, 0)), _any_spec()],
        out_specs=pl.BlockSpec((FFN_CHUNK_DEVS, 1, fs, d), lambda c, k: (c, 2, 0, 0), pipeline_mode=pl.Buffered(1)),
        out_shape=jax.ShapeDtypeStruct(blocks.shape, BF16), input_output_aliases={2: 0},
        scratch_shapes=[pltpu.VMEM((tf, d), F32)],
        compiler_params=_params())(h, df, blocks)


def _whole(arr):
    return pl.BlockSpec(arr.shape, lambda i: (0,) * arr.ndim, pipeline_mode=pl.Buffered(1))


def _win_fwd(x, w_in, name):
    s, d = x.shape
    d_in = N_DEV * w_in.shape[1]
    tm = min(s, 512)
    scale = NA_HEAD_DIM ** -0.5
    assert d_in == D_LOC + 3 * D_NA and scale == 0.125

    def body(x_ref, w_ref, loc_ref, qkv_ref):
        proj = _dot_nt(x_ref[...].astype(BF16), w_ref[...].reshape(d_in, d))
        loc_ref[...] = proj[:, :D_LOC]
        qkv_ref[:, :D_NA] = (proj[:, D_LOC:D_LOC + D_NA] * scale).astype(BF16)
        qkv_ref[:, D_NA:] = proj[:, D_LOC + D_NA:].astype(BF16)

    return pl.pallas_call(
        body, name=name, grid=(s // tm,),
        in_specs=[pl.BlockSpec((tm, d), lambda i: (i, 0)), _whole(w_in)],
        out_specs=[pl.BlockSpec((tm, D_LOC), lambda i: (i, 0)), pl.BlockSpec((tm, 3 * D_NA), lambda i: (i, 0))],
        out_shape=[jax.ShapeDtypeStruct((s, D_LOC), F32), jax.ShapeDtypeStruct((s, 3 * D_NA), BF16)],
        compiler_params=_params())(x, w_in)


def _wout_fwd(x, yab, yc, w_out, ln_g, ln_b, alpha, name):
    s, d = x.shape
    tm = min(s, 512)

    def body(x_ref, yab_ref, yc_ref, w_ref, g_ref, b_ref, z_ref, y_ref):
        mix = jnp.concatenate([yab_ref[...], yc_ref[...]], axis=1).astype(BF16)
        z = alpha * x_ref[...] + _dot(mix, w_ref[...].reshape(D_MIX, d))
        zhat, _ = _ln_stats(z)
        z_ref[...] = z
        y_ref[...] = zhat * g_ref[...] + b_ref[...]

    row = pl.BlockSpec((tm, d), lambda i: (i, 0))
    half = pl.BlockSpec((tm, D_MIX // 2), lambda i: (i, 0))
    vec = pl.BlockSpec((1, d), lambda i: (0, 0))
    return pl.pallas_call(
        body, name=name, grid=(s // tm,),
        in_specs=[row, half, half, _whole(w_out), vec, vec],
        out_specs=[row, row], out_shape=[jax.ShapeDtypeStruct((s, d), F32)] * 2,
        compiler_params=_params())(x, yab, yc, w_out, ln_g, ln_b)


def _wout_bwd(dy, z, yab, yc, w_out, ln_g, alpha, name, job=None):
    s, d = dy.shape
    rs = w_out.shape[1]
    tm = min(s, 512)
    n_i = s // tm

    def body(dy_ref, z_ref, yab_ref, yc_ref, w_ref, g_ref, dmix_ref, dxp_ref, dg_ref, db_ref, out_ref, acc_s):
        i = pl.program_id(0)
        dz, dg, db = _ln_bwd(dy_ref[...], z_ref[...], g_ref[...])
        _accumulate(dg_ref, dg, i == 0)
        _accumulate(db_ref, db, i == 0)
        dxp_ref[...] = alpha * dz
        dzb = dz.astype(BF16)
        dmix_ref[...] = _dot_nt(dzb, w_ref[...].reshape(D_MIX, d))
        mix = jnp.concatenate([yab_ref[...], yc_ref[...]], axis=1).astype(BF16)
        _add_matmul(acc_s, i == 0, lambda: _dot_tn(mix, dzb))

        @pl.when(i == n_i - 1)
        def _():
            out_ref[...] = acc_s[...].astype(BF16).reshape(N_DEV, rs, d)

    row = pl.BlockSpec((tm, d), lambda i: (i, 0))
    half = pl.BlockSpec((tm, D_MIX // 2), lambda i: (i, 0))
    vec = pl.BlockSpec((1, d), lambda i: (0, 0))
    call = _riding_call(
        body, job, 6, 5, n_i, lambda: pl.program_id(0),
        name=name, grid=(n_i,),
        in_specs=[row, row, half, half, _whole(w_out), vec],
        out_specs=[pl.BlockSpec((tm, D_MIX), lambda i: (i, 0)), row, vec, vec, _whole(w_out)],
        out_shape=[jax.ShapeDtypeStruct((s, D_MIX), F32), jax.ShapeDtypeStruct((s, d), F32),
                   jax.ShapeDtypeStruct((1, d), F32), jax.ShapeDtypeStruct((1, d), F32),
                   jax.ShapeDtypeStruct(w_out.shape, BF16)],
        scratch_shapes=[pltpu.VMEM((D_MIX, d), F32)],
        compiler_params=_params())
    return call(dy, z, yab, yc, w_out, ln_g)


def _win_bwd(dxp, dloc, dq, dk, dv, x, w_in, name):
    s, d = x.shape
    rs = w_in.shape[1]
    d_in = N_DEV * rs
    tm = min(s, 512)
    n_i = s // tm

    def body(dxp_ref, dloc_ref, dq_ref, dk_ref, dv_ref, x_ref, w_ref, dx_ref, out_ref, acc_s):
        i = pl.program_id(0)
        dp = jnp.concatenate([dloc_ref[...], dq_ref[...], dk_ref[...].astype(BF16), dv_ref[...].astype(BF16)], axis=1)
        dx_ref[...] = dxp_ref[...] + _dot(dp, w_ref[...].reshape(d_in, d))
        _add_matmul(acc_s, i == 0, lambda: _dot_tn(dp, x_ref[...].astype(BF16)))

        @pl.when(i == n_i - 1)
        def _():
            out_ref[...] = acc_s[...].astype(BF16).reshape(N_DEV, rs, d)

    row = pl.BlockSpec((tm, d), lambda i: (i, 0))
    na = pl.BlockSpec((tm, D_NA), lambda i: (i, 0))
    return pl.pallas_call(
        body, name=name, grid=(n_i,),
        in_specs=[row, pl.BlockSpec((tm, D_LOC), lambda i: (i, 0)), na, na, na, row, _whole(w_in)],
        out_specs=[row, _whole(w_in)],
        out_shape=[jax.ShapeDtypeStruct((s, d), F32), jax.ShapeDtypeStruct(w_in.shape, BF16)],
        scratch_shapes=[pltpu.VMEM((d_in, d), F32)],
        compiler_params=_params())(dxp, dloc, dq, dk, dv, x, w_in)


def _shift_rows(v, k):
    n = v.shape[0]
    return pltpu.roll(v, k % n, 0)


def _halo_specs(tm, s, width, col):
    per = tm // HALO
    last = s // HALO - 1
    return [pl.BlockSpec((HALO, width), lambda i: (jnp.maximum(i * per - 1, 0), col)),
            pl.BlockSpec((tm, width), lambda i: (i, col)),
            pl.BlockSpec((HALO, width), lambda i: (jnp.minimum((i + 1) * per, last), col))]


def _token_index(i, tm):
    return i * tm - HALO + lax.broadcasted_iota(jnp.int32, (tm + 2 * HALO, 1), 0)


def _pool_lane_tables():
    lane = lax.broadcasted_iota(jnp.int32, (1, D_POOL), 1)
    group = sum((lane >= g * POOL_GROUP).astype(jnp.int32) for g in range(1, len(POOL_WINDOWS)))
    half = jnp.where(group == 0, 1, jnp.where(group == 1, 2, jnp.where(group == 2, 4, 8)))
    return group, half


def _window_sums(v, group, offsets):
    s2 = v + _shift_rows(v, 1)
    s4 = s2 + _shift_rows(s2, 2)
    s8 = s4 + _shift_rows(s4, 4)
    s16 = s8 + _shift_rows(s8, 8)
    parts = [_shift_rows(p, -o) if o else p for p, o in zip((s2, s4, s8, s16), offsets)]
    return jnp.where(group == 0, parts[0], jnp.where(group == 1, parts[1], jnp.where(group == 2, parts[2], parts[3])))


def _pool_counts(tok, half, s):
    return (jnp.minimum(tok + half, s) - jnp.maximum(tok - half, 0)).astype(F32)


def _pool_forward(u, tok, s):
    group, half = _pool_lane_tables()
    sums = _window_sums(u, group, [w // 2 - 1 for w in POOL_WINDOWS])
    return sums / _pool_counts(tok, half, s) - u


def _conv_forward(zc, cw_ref):
    return cw_ref[0:1, :] * _shift_rows(zc, 1) + cw_ref[1:2, :] * zc + cw_ref[2:3, :] * _shift_rows(zc, -1)


def _local_fwd(proj, pool_bd, pool_scale, conv_w, name):
    s = proj.shape[0]
    tm = min(s, 512)
    ctr = slice(HALO, HALO + tm)

    def body(prev_ref, cur_ref, next_ref, pw_ref, sc_ref, cw_ref, out_ref):
        i = pl.program_id(0)
        ext = jnp.concatenate([prev_ref[...], cur_ref[...], next_ref[...]], axis=0)
        tok = _token_index(i, tm)
        inside = (tok >= 0) & (tok < s)
        u = jnp.where(inside, ext[:, 0:D_POOL], 0.0)
        p = _pool_forward(u, tok, s)[ctr]
        ya = _dot(p.astype(BF16), pw_ref[...]) * sc_ref[...]
        gb = ext[:, D_POOL:D_POOL + D_CONV]
        zc = jnp.where(inside, ext[:, D_POOL + D_CONV:D_POOL + 2 * D_CONV] * ext[:, D_POOL + 2 * D_CONV:D_LOC], 0.0)
        yb = (gb * _conv_forward(zc, cw_ref))[ctr]
        out_ref[...] = jnp.concatenate([ya, yb], axis=1)

    return pl.pallas_call(
        body, name=name, grid=(s // tm,),
        in_specs=_halo_specs(tm, s, D_LOC, 0) + [
            pl.BlockSpec((D_POOL, D_POOL), lambda i: (0, 0)), pl.BlockSpec((1, D_POOL), lambda i: (0, 0)),
            pl.BlockSpec((3, D_CONV), lambda i: (0, 0))],
        out_specs=pl.BlockSpec((tm, D_POOL + D_CONV), lambda i: (i, 0)),
        out_shape=jax.ShapeDtypeStruct((s, D_POOL + D_CONV), F32),
        compiler_params=_params())(proj, proj, proj, pool_bd, pool_scale, conv_w)


def _local_bwd(proj, dmix, pool_bd, pool_scale, conv_w, name):
    s = proj.shape[0]
    tm = min(s, 512)
    ctr = slice(HALO, HALO + tm)

    def body(prev_ref, cur_ref, next_ref, dprev_ref, dcur_ref, dnext_ref, pw_ref, sc_ref, cw_ref,
             dloc_ref, dpw_ref, dsc_ref, dcw_ref):
        i = pl.program_id(0)
        first = i == 0
        ext = jnp.concatenate([prev_ref[...], cur_ref[...], next_ref[...]], axis=0)
        dext = jnp.concatenate([dprev_ref[...], dcur_ref[...], dnext_ref[...]], axis=0)
        tok = _token_index(i, tm)
        inside = (tok >= 0) & (tok < s)
        group, half = _pool_lane_tables()
        cnt = _pool_counts(tok, half, s)
        u = jnp.where(inside, ext[:, 0:D_POOL], 0.0)
        dya = jnp.where(inside, dext[:, 0:D_POOL], 0.0)
        p_c = _pool_forward(u, tok, s)[ctr].astype(BF16)
        lin = _dot(p_c, pw_ref[...])
        _accumulate(dsc_ref, jnp.sum(dya[ctr] * lin, axis=0, keepdims=True), first)
        e1 = (dya * sc_ref[...]).astype(BF16)
        _accumulate(dpw_ref, _dot_tn(p_c, e1[ctr]), first)
        dp = _dot_nt(e1, pw_ref[...])
        du = _window_sums(dp / cnt, group, [w // 2 for w in POOL_WINDOWS]) - dp
        gb = ext[:, D_POOL:D_POOL + D_CONV]
        gc = ext[:, D_POOL + D_CONV:D_POOL + 2 * D_CONV]
        hv = ext[:, D_POOL + 2 * D_CONV:D_LOC]
        zc = jnp.where(inside, gc * hv, 0.0)
        dyb = jnp.where(inside, dext[:, D_POOL:D_POOL + D_CONV], 0.0)
        dgb = dyb * _conv_forward(zc, cw_ref)
        dyc = dyb * gb
        for k in range(3):
            part = jnp.sum(dyc[ctr] * _shift_rows(zc, 1 - k)[ctr], axis=0, keepdims=True)
            _accumulate(dcw_ref.at[k:k + 1, :], part, first)
        dzc = cw_ref[0:1, :] * _shift_rows(dyc, -1) + cw_ref[1:2, :] * dyc + cw_ref[2:3, :] * _shift_rows(dyc, 1)
        dloc = jnp.concatenate([du, dgb, dzc * hv, dzc * gc], axis=1)
        dloc_ref[...] = dloc[ctr].astype(BF16)

    return pl.pallas_call(
        body, name=name, grid=(s // tm,),
        in_specs=_halo_specs(tm, s, D_LOC, 0) + _halo_specs(tm, s, D_POOL + D_CONV, 0) + [
            pl.BlockSpec((D_POOL, D_POOL), lambda i: (0, 0)), pl.BlockSpec((1, D_POOL), lambda i: (0, 0)),
            pl.BlockSpec((3, D_CONV), lambda i: (0, 0))],
        out_specs=[pl.BlockSpec((tm, D_LOC), lambda i: (i, 0)), pl.BlockSpec((D_POOL, D_POOL), lambda i: (0, 0)),
                   pl.BlockSpec((1, D_POOL), lambda i: (0, 0)), pl.BlockSpec((8, D_CONV), lambda i: (0, 0))],
        out_shape=[jax.ShapeDtypeStruct((s, D_LOC), BF16), jax.ShapeDtypeStruct((D_POOL, D_POOL), F32),
                   jax.ShapeDtypeStruct((1, D_POOL), F32), jax.ShapeDtypeStruct((8, D_CONV), F32)],
        compiler_params=_params())(proj, proj, proj, dmix, dmix, dmix, pool_bd, pool_scale, conv_w)


def _na_geometry(rows):
    n_j = rows // Q_ROWS
    dr = np.full((3, Q_ROWS, K_ROWS), 2 * NA_ROWS - 1, np.int64)
    for t, j in enumerate((0, min(1, n_j - 1), n_j - 1)):
        base = int(np.clip(Q_ROWS * j - NA_ROWS // 2, 0, rows - K_ROWS))
        for qr in range(Q_ROWS):
            r = Q_ROWS * j + qr
            start = int(np.clip(r - NA_ROWS // 2, 0, rows - NA_ROWS))
            for kr in range(K_ROWS):
                if start <= base + kr < start + NA_ROWS:
                    dr[t, qr, kr] = base + kr - r + NA_ROWS - 1
    return dr


def _na_col_tables():
    c = np.arange(GRID_W)
    start = np.clip(c - NA_COLS // 2, 0, GRID_W - NA_COLS)
    valid = (c[None, :] >= start[:, None]) & (c[None, :] < start[:, None] + NA_COLS)
    dc = np.clip(c[None, :] - c[:, None], -(NA_COLS - 1), NA_COLS - 1) + (NA_COLS - 1)
    return valid, dc


NO_ROW = 2 * NA_ROWS - 1
N_SLOT = 2 * NA_ROWS


def _na_tiles(rpb):
    valid, dc = _na_col_tables()
    onehot = jnp.asarray((dc[None] == np.arange(2 * NA_COLS - 1)[:, None, None]).astype(np.float32))
    table = jnp.einsum("hrd,dqk->hrqk", rpb, onehot, precision=lax.Precision.HIGHEST)
    table = jnp.where(jnp.asarray(valid)[None, None], table, NEG_INF)
    outside = jnp.full((NA_HEADS, 1, GRID_W, GRID_W), NEG_INF, F32)
    padded = jnp.concatenate([outside, table, outside], axis=1)
    pairs = jnp.concatenate([padded[:, :N_SLOT], padded[:, 1:]], axis=-1)
    return jnp.concatenate([pairs, jnp.full((NA_HEADS, 1, GRID_W, 2 * GRID_W), NEG_INF, F32)], axis=1)


G_ROWS = 2
N_GRP = Q_ROWS // G_ROWS
G_TOK = G_ROWS * GRID_W
GK_ROWS = NA_ROWS + G_ROWS
GK_TOK = GK_ROWS * GRID_W
STACK_TOK = N_GRP * 2 * G_TOK


def _na_group_tables(rows):
    dr = _na_geometry(rows)
    koff = np.zeros((3, N_GRP), np.int64)
    slot = np.zeros((3, N_GRP, G_ROWS, GK_ROWS // 2), np.int64)
    even_in, odd_in = np.zeros_like(slot), np.zeros_like(slot)
    for t in range(3):
        for g in range(N_GRP):
            qrs = range(G_ROWS * g, G_ROWS * (g + 1))
            inside = [kr for kr in range(K_ROWS) if any(dr[t, qr, kr] != NO_ROW for qr in qrs)]
            lo, hi = min(inside), max(inside) + 1
            off = min(lo - lo % 2, K_ROWS - GK_ROWS)
            assert off <= lo and hi <= off + GK_ROWS
            koff[t, g] = off
            for qq, qr in enumerate(qrs):
                for kp in range(GK_ROWS // 2):
                    even, odd = int(dr[t, qr, off + 2 * kp]), int(dr[t, qr, off + 2 * kp + 1])
                    even_in[t, g, qq, kp], odd_in[t, g, qq, kp] = even != NO_ROW, odd != NO_ROW
                    slot[t, g, qq, kp] = (N_SLOT if even == NO_ROW and odd == NO_ROW
                                          else (even if even != NO_ROW else odd - 1) + 1)
    return koff, slot, even_in, odd_in


def _by_type(block_type, per_type):
    a, b, c = (int(v) for v in per_type)
    if a == b == c:
        return a
    return jnp.where(block_type == 0, a, jnp.where(block_type == 2, c, b))


def _score_rows(g, hh):
    first = (2 * g + hh) * G_TOK
    return slice(first, first + G_TOK)


def _tile_at(g, hh, qq, kp):
    first = _score_rows(g, hh).start + qq * GRID_W
    return slice(first, first + GRID_W), slice(kp * 2 * GRID_W, (kp + 1) * 2 * GRID_W)


def _fill_bias(bias_s, tiles_ref, block_type, tables):
    _, slot, even_in, odd_in = tables
    left = lax.broadcasted_iota(jnp.int32, (1, 2 * GRID_W), 1) < GRID_W
    for hh in range(2):
        for g in range(N_GRP):
            for qq in range(G_ROWS):
                for kp in range(GK_ROWS // 2):
                    tile = tiles_ref[hh, _by_type(block_type, slot[:, g, qq, kp])]
                    tile = jnp.where(left & (_by_type(block_type, even_in[:, g, qq, kp]) == 0), NEG_INF, tile)
                    tile = jnp.where(jnp.logical_not(left) & (_by_type(block_type, odd_in[:, g, qq, kp]) == 0), NEG_INF, tile)
                    rs, cs = _tile_at(g, hh, qq, kp)
                    bias_s[rs, cs] = tile


def _group_offset(block_type, koff, g):
    off = _by_type(block_type, koff[:, g]) * GRID_W
    return off if isinstance(off, int) else pl.multiple_of(off, 2 * GRID_W)


def _na_specs(s, proj_cols):
    n_blk = s // K_BLK
    per = Q_TOK // K_BLK

    def kv_spec(col0, m):
        return pl.BlockSpec((K_BLK, HEAD_PAIR), lambda hp, j: (jnp.clip(per * j - 1, 0, n_blk - 4) + m, col0 + hp))

    q_col, k_col, v_col = (c // HEAD_PAIR for c in proj_cols)
    return ([pl.BlockSpec((Q_TOK, HEAD_PAIR), lambda hp, j: (j, q_col + hp))]
            + [kv_spec(k_col, m) for m in range(4)] + [kv_spec(v_col, m) for m in range(4)])


def _na_block_type(j, n_j):
    return jnp.where(j == 0, 0, jnp.where(j == n_j - 1, 2, 1))


def _head_masks():
    lane = lax.broadcasted_iota(jnp.int32, (1, HEAD_PAIR), 1)
    return [lane < NA_HEAD_DIM, lane >= NA_HEAD_DIM]


def _attn_fwd(qkv, tiles, name):
    s = qkv.shape[0]
    n_j = s // Q_TOK
    tables = _na_group_tables(s // GRID_W)
    koff = tables[0]

    def body(q_ref, k0, k1, k2, k3, v0, v1, v2, v3, tiles_ref, o_ref, lse_ref, bias_s, k_s, vh_s, sc_s, p_s):
        j = pl.program_id(1)
        block_type = _na_block_type(j, n_j)
        pl.when((j == 0) | (j == 1) | (j == n_j - 1))(functools.partial(_fill_bias, bias_s, tiles_ref, block_type, tables))
        masks = _head_masks()
        for m, (kr, vr) in enumerate(zip((k0, k1, k2, k3), (v0, v1, v2, v3))):
            rows = slice(m * K_BLK, (m + 1) * K_BLK)
            k_s[rows, :] = kr[...]
            v = vr[...]
            for hh, mask in enumerate(masks):
                vh_s[hh, rows, :] = jnp.where(mask, v, jnp.zeros_like(v))
        q = q_ref[...]
        qh = [jnp.where(mask, q, jnp.zeros_like(q)) for mask in masks]
        offs = [_group_offset(block_type, koff, g) for g in range(N_GRP)]
        for g in range(N_GRP):
            kg = k_s[pl.ds(offs[g], GK_TOK), :]
            for hh in range(2):
                sc_s[_score_rows(g, hh), :] = _dot_nt(qh[hh][g * G_TOK:(g + 1) * G_TOK], kg)
        sc = sc_s[...] + bias_s[...]
        mx = jnp.max(sc, axis=-1, keepdims=True)
        p = jnp.exp(sc - mx)
        den = jnp.sum(p, axis=-1, keepdims=True)
        p_s[...] = p.astype(BF16)
        inv = 1.0 / den
        lse = mx + jnp.log(den)
        for g in range(N_GRP):
            rows = slice(g * G_TOK, (g + 1) * G_TOK)
            out = jnp.zeros((G_TOK, HEAD_PAIR), F32)
            for hh in range(2):
                sr = _score_rows(g, hh)
                out = out + _dot(p_s[sr, :], vh_s[hh, pl.ds(offs[g], GK_TOK), :]) * inv[sr]
            o_ref[rows, :] = out
            lse_ref[0, rows, :] = jnp.where(masks[0], lse[_score_rows(g, 0)], lse[_score_rows(g, 1)])

    return pl.pallas_call(
        body, name=name, grid=(NA_HEADS // 2, n_j),
        in_specs=_na_specs(s, (0, D_NA, 2 * D_NA)) + [
            pl.BlockSpec((2, N_SLOT + 1, GRID_W, 2 * GRID_W), lambda hp, j: (hp, 0, 0, 0))],
        out_specs=[pl.BlockSpec((Q_TOK, HEAD_PAIR), lambda hp, j: (j, hp)),
                   pl.BlockSpec((1, Q_TOK, HEAD_PAIR), lambda hp, j: (hp, j, 0))],
        out_shape=[jax.ShapeDtypeStruct((s, D_NA), F32), jax.ShapeDtypeStruct((NA_HEADS // 2, s, HEAD_PAIR), F32)],
        scratch_shapes=[pltpu.VMEM((STACK_TOK, GK_TOK), F32), pltpu.VMEM((K_TOK, HEAD_PAIR), BF16),
                        pltpu.VMEM((2, K_TOK, HEAD_PAIR), BF16), pltpu.VMEM((STACK_TOK, GK_TOK), F32),
                        pltpu.VMEM((STACK_TOK, GK_TOK), BF16)],
        compiler_params=_params())(*([qkv] * 9), tiles)


def _add_tiles(dtile_ref, ds_ref, block_type, slot, has_interior):
    def tile(g, hh, qq, kp):
        rs, cs = _tile_at(g, hh, qq, kp)
        return ds_ref[rs, cs].astype(F32)

    def interior():
        for hh in range(2):
            for qq in range(G_ROWS):
                for kp in range(GK_ROWS // 2):
                    assert (slot[1, :, qq, kp] == slot[1, 0, qq, kp]).all()
                    if slot[1, 0, qq, kp] != N_SLOT:
                        dtile_ref[hh, int(slot[1, 0, qq, kp])] += sum(tile(g, hh, qq, kp) for g in range(N_GRP))

    def edge():
        for hh in range(2):
            for g in range(N_GRP):
                for qq in range(G_ROWS):
                    for kp in range(GK_ROWS // 2):
                        first, last = (0 if e == N_SLOT else int(e) for e in slot[[0, 2], g, qq, kp])
                        if (slot[[0, 2], g, qq, kp] != N_SLOT).any():
                            dtile_ref[hh, _by_type(block_type, (first, first, last))] += tile(g, hh, qq, kp)

    if has_interior:
        pl.when(block_type == 1)(interior)
    pl.when(block_type != 1)(edge)


def _attn_bwd(qkv, tiles, o, dmix, lse, name, job=None):
    s = qkv.shape[0]
    n_j = s // Q_TOK
    n_blk = s // K_BLK
    per = Q_TOK // K_BLK
    scale = NA_HEAD_DIM ** -0.5
    do_col = (D_POOL + D_CONV) // HEAD_PAIR
    tables = _na_group_tables(s // GRID_W)
    koff, slot = tables[0], tables[1]

    def body(q_ref, k0, k1, k2, k3, v0, v1, v2, v3, tiles_ref, o_ref, do_ref, lse_ref,
             dq_ref, dk_ref, dv_ref, dtile_ref, bias_s, k_s, kh_s, v_s, s_s, dp_s, pb_s, dsb_s):
        j = pl.program_id(1)

        @pl.when(j == 0)
        def _():
            dk_ref[...] = jnp.zeros_like(dk_ref)
            dv_ref[...] = jnp.zeros_like(dv_ref)
            dtile_ref[...] = jnp.zeros_like(dtile_ref)

        block_type = _na_block_type(j, n_j)
        pl.when((j == 0) | (j == 1) | (j == n_j - 1))(functools.partial(_fill_bias, bias_s, tiles_ref, block_type, tables))
        base = pl.multiple_of(jnp.clip(per * j - 1, 0, n_blk - 4) * K_BLK, K_BLK)
        masks = _head_masks()
        for m, (kr, vr) in enumerate(zip((k0, k1, k2, k3), (v0, v1, v2, v3))):
            rows = slice(m * K_BLK, (m + 1) * K_BLK)
            k = kr[...]
            k_s[rows, :] = k
            v_s[rows, :] = vr[...]
            for hh, mask in enumerate(masks):
                kh_s[hh, rows, :] = jnp.where(mask, k, jnp.zeros_like(k))
        q = q_ref[...]
        qh = [jnp.where(mask, q, jnp.zeros_like(q)) for mask in masks]
        lane = lax.broadcasted_iota(jnp.int32, (1, HEAD_PAIR), 1)
        offs = [_group_offset(block_type, koff, g) for g in range(N_GRP)]
        do, ov, lse = do_ref[...], o_ref[...], lse_ref[0]
        dob, lse_col, delta_col = {}, [], []
        for g in range(N_GRP):
            rows = slice(g * G_TOK, (g + 1) * G_TOK)
            kg = k_s[pl.ds(offs[g], GK_TOK), :]
            vg = v_s[pl.ds(offs[g], GK_TOK), :]
            for hh, mask in enumerate(masks):
                doh = jnp.where(mask, do[rows], 0.0)
                dob[g, hh] = doh.astype(BF16)
                lse_col.append(jnp.sum(jnp.where(lane == hh * NA_HEAD_DIM, lse[rows], 0.0), axis=-1, keepdims=True))
                delta_col.append(jnp.sum(doh * ov[rows], axis=-1, keepdims=True))
                s_s[_score_rows(g, hh), :] = _dot_nt(qh[hh][rows], kg)
                dp_s[_score_rows(g, hh), :] = _dot_nt(dob[g, hh], vg)
        p = jnp.exp(s_s[...] + bias_s[...] - jnp.concatenate(lse_col, axis=0))
        ds = p * (dp_s[...] - jnp.concatenate(delta_col, axis=0))
        pb_s[...] = p.astype(BF16)
        dsb_s[...] = ds.astype(BF16)
        _add_tiles(dtile_ref, dsb_s, block_type, slot, n_j > 2)
        for g in range(N_GRP):
            rows = slice(g * G_TOK, (g + 1) * G_TOK)
            dq = jnp.zeros((G_TOK, HEAD_PAIR), F32)
            dk = jnp.zeros((GK_TOK, HEAD_PAIR), F32)
            dv = jnp.zeros((GK_TOK, HEAD_PAIR), F32)
            for hh in range(2):
                sr = _score_rows(g, hh)
                dsb = dsb_s[sr, :]
                dq = dq + _dot(dsb, kh_s[hh, pl.ds(offs[g], GK_TOK), :])
                dk = dk + _dot_tn(dsb, qh[hh][rows])
                dv = dv + _dot_tn(pb_s[sr, :], dob[g, hh])
            dq_ref[rows, :] = (dq * scale).astype(BF16)
            at = pl.multiple_of(base + offs[g], 2 * GRID_W)
            dk_ref[pl.ds(at, GK_TOK), :] += dk
            dv_ref[pl.ds(at, GK_TOK), :] += dv

    pair = pl.BlockSpec((Q_TOK, HEAD_PAIR), lambda hp, j: (j, hp))
    whole = pl.BlockSpec((s, HEAD_PAIR), lambda hp, j: (0, hp))
    call = _riding_call(
        body, job, 13, 4, (NA_HEADS // 2) * n_j, lambda: pl.program_id(0) * n_j + pl.program_id(1),
        name=name, grid=(NA_HEADS // 2, n_j),
        in_specs=_na_specs(s, (0, D_NA, 2 * D_NA)) + [
            pl.BlockSpec((2, N_SLOT + 1, GRID_W, 2 * GRID_W), lambda hp, j: (hp, 0, 0, 0)),
            pair, pl.BlockSpec((Q_TOK, HEAD_PAIR), lambda hp, j: (j, do_col + hp)),
            pl.BlockSpec((1, Q_TOK, HEAD_PAIR), lambda hp, j: (hp, j, 0))],
        out_specs=[pair, whole, whole, pl.BlockSpec((2, N_SLOT, GRID_W, 2 * GRID_W), lambda hp, j: (hp, 0, 0, 0))],
        out_shape=[jax.ShapeDtypeStruct((s, D_NA), BF16), jax.ShapeDtypeStruct((s, D_NA), F32),
                   jax.ShapeDtypeStruct((s, D_NA), F32),
                   jax.ShapeDtypeStruct((NA_HEADS, N_SLOT, GRID_W, 2 * GRID_W), F32)],
        scratch_shapes=[pltpu.VMEM((STACK_TOK, GK_TOK), F32), pltpu.VMEM((K_TOK, HEAD_PAIR), BF16),
                        pltpu.VMEM((2, K_TOK, HEAD_PAIR), BF16), pltpu.VMEM((K_TOK, HEAD_PAIR), BF16),
                        pltpu.VMEM((STACK_TOK, GK_TOK), F32), pltpu.VMEM((STACK_TOK, GK_TOK), F32),
                        pltpu.VMEM((STACK_TOK, GK_TOK), BF16), pltpu.VMEM((STACK_TOK, GK_TOK), BF16)],
        compiler_params=_params())
    return call(*([qkv] * 9), tiles, o, dmix, lse)


def _rpb_finish(tiles, name):
    valid, dc = _na_col_tables()
    n_dc = 2 * NA_COLS - 1
    sel = np.zeros((GRID_W, 2 * GRID_W, LANES), np.float32)
    for qc in range(GRID_W):
        for kc in range(GRID_W):
            if valid[qc, kc]:
                sel[qc, kc, dc[qc, kc]] = 1.0
                sel[qc, GRID_W + kc, LANES // 2 + dc[qc, kc]] = 1.0
    sel = jnp.asarray(sel.reshape(GRID_W * 2 * GRID_W, LANES))
    flat = tiles.reshape(NA_HEADS * 2 * NA_ROWS, GRID_W * 2 * GRID_W)

    def body(a_ref, b_ref, out_ref):
        out_ref[...] = jnp.dot(a_ref[...], b_ref[...], preferred_element_type=F32, precision=lax.Precision.HIGHEST)

    sums = pl.pallas_call(
        body, name=name, out_shape=jax.ShapeDtypeStruct((flat.shape[0], LANES), F32),
        compiler_params=_params())(flat, sel).reshape(NA_HEADS, 2 * NA_ROWS, LANES)
    return sums[:, 1:, :n_dc] + sums[:, :2 * NA_ROWS - 1, LANES // 2:LANES // 2 + n_dc]


def _loss_grad(y, target, name):
    s, d = y.shape
    tm = min(s, 1024)

    def body(y_ref, t_ref, sum_ref, dy_ref):
        diff = y_ref[...] - t_ref[...]
        dy_ref[...] = diff * (1.0 / d)
        part = jnp.zeros((8, LANES), F32) + jnp.sum(diff * diff)
        _accumulate(sum_ref, part, pl.program_id(0) == 0)

    row = pl.BlockSpec((tm, d), lambda i: (i, 0))
    return pl.pallas_call(
        body, name=name, grid=(s // tm,), in_specs=[row, row],
        out_specs=[pl.BlockSpec((8, LANES), lambda i: (0, 0)), row],
        out_shape=[jax.ShapeDtypeStruct((8, LANES), F32), jax.ShapeDtypeStruct((s, d), F32)],
        compiler_params=_params())(y, target)


def _adamw(w, g, m, v, name):
    rows, cols = w.shape
    tr = _row_tile(rows, 512, 8)

    def body(w_ref, g_ref, m_ref, v_ref, d_ref, nm_ref, nv_ref):
        gv = g_ref[...]
        nm = ADAM_B1 * m_ref[...] + (1.0 - ADAM_B1) * gv
        nv = ADAM_B2 * v_ref[...] + (1.0 - ADAM_B2) * (gv * gv)
        m_hat = nm / (1.0 - ADAM_B1 ** ADAM_STEP)
        v_hat = nv / (1.0 - ADAM_B2 ** ADAM_STEP)
        d_ref[...] = -ADAM_LR * (m_hat / (jnp.sqrt(v_hat) + ADAM_EPS) + ADAM_WD * w_ref[...])
        nm_ref[...] = nm
        nv_ref[...] = nv

    blk = pl.BlockSpec((tr, cols), lambda r: (r, 0))
    return pl.pallas_call(
        body, name=name, grid=(rows // tr,), in_specs=[blk] * 4, out_specs=[blk] * 3,
        out_shape=[jax.ShapeDtypeStruct((rows, cols), F32)] * 3, compiler_params=_params())(w, g, m, v)


def _adamw_nd(w, g, m, v, name):
    shape = w.shape
    flat = lambda t: t.reshape(-1, shape[-1])
    return tuple(t.reshape(shape) for t in _adamw(flat(w), flat(g), flat(m), flat(v), name))


def _pack(parts, rows_mult=64):
    flat = jnp.concatenate([p.reshape(-1).astype(F32) for p in parts])
    per = LANES * rows_mult
    total = -(-flat.shape[0] // per) * per
    return jnp.pad(flat, (0, total - flat.shape[0])).reshape(-1, LANES)


def _unpack(packed, shapes):
    flat = packed.reshape(-1)
    out, pos = [], 0
    for shp in shapes:
        n = int(np.prod(shp))
        out.append(flat[pos:pos + n].reshape(shp))
        pos += n
    return out


def kernel(x, ffn1_w_gate, ffn1_w_up, ffn1_w_down, ffn2_w_gate, ffn2_w_up, ffn2_w_down, w_in, pool_w, pool_scale, conv_w, rpb, w_out, ln_g, ln_b, loss_target, m_ffn1_w_gate, m_ffn1_w_up, m_ffn1_w_down, m_ffn2_w_gate, m_ffn2_w_up, m_ffn2_w_down, m_w_in, m_pool_w, m_pool_scale, m_conv_w, m_rpb, m_w_out, m_ln_g, m_ln_b, v_ffn1_w_gate, v_ffn1_w_up, v_ffn1_w_down, v_ffn2_w_gate, v_ffn2_w_up, v_ffn2_w_down, v_w_in, v_pool_w, v_pool_scale, v_conv_w, v_rpb, v_w_out, v_ln_g, v_ln_b):
    n_l, d, fs = ffn1_w_gate.shape
    s = x.shape[1]
    rows = s // GRID_W
    assert x.shape[0] == 1 and s % Q_TOK == 0 and rows >= K_ROWS and fs % BF16_ROWS == 0
    alpha = (2.0 * n_l) ** 0.25
    xi, yi, ci = _mesh_pos()
    me = 4 * xi + 2 * yi + ci
    core = jnp.reshape(ci, (1,)).astype(jnp.int32)
    ln_w, cw_w = ln_g.shape[2], conv_w.shape[2]

    tr = lambda w: jnp.swapaxes(w, 1, 2)
    ffn1_shard = jnp.stack([tr(ffn1_w_gate), tr(ffn1_w_up), ffn1_w_down], axis=1).astype(BF16)
    ffn2_shard = jnp.stack([tr(ffn2_w_gate), tr(ffn2_w_up), ffn2_w_down], axis=1).astype(BF16)
    win_shard, wout_shard = tr(w_in).astype(BF16), w_out.astype(BF16)
    small_shard = _pack([ln_g, ln_b, conv_w])
    w_ffn1, small = _exchange_alone(_Gather([ffn1_shard[0], small_shard]), "gather_first")
    n_ln = n_l * 3 * ln_w
    small = small.reshape(N_DEV, -1)
    unshard = lambda t, width: jnp.moveaxis(t.reshape(N_DEV, n_l, 3, width), 0, 2).reshape(n_l, 3, N_DEV * width)
    ln_g_all = unshard(small[:, :n_ln], ln_w)
    ln_b_all = unshard(small[:, n_ln:2 * n_ln], ln_w)
    conv_all = unshard(small[:, 2 * n_ln:2 * n_ln + n_l * 3 * cw_w], cw_w)
    pool_bd = jnp.zeros((n_l, D_POOL, D_POOL), F32)
    for g in range(len(POOL_WINDOWS)):
        sl = slice(g * POOL_GROUP, (g + 1) * POOL_GROUP)
        pool_bd = pool_bd.at[:, sl, sl].set(pool_w[:, g])
    pool_bd = pool_bd.astype(BF16)
    lnp = lambda arr, l, j: arr[l, j].reshape(1, d)

    saved = []
    h = x.reshape(s, d)
    for l in range(n_l):
        a1, u1, h1, z1, x1, w_in_l, w_out_l, w_ffn2 = _ffn_fwd(
            h, w_ffn1, lnp(ln_g_all, l, 0), lnp(ln_b_all, l, 0), alpha, f"ffn1_fwd_{l}",
            job=_Gather([win_shard[l], wout_shard[l], ffn2_shard[l]]))
        proj = _win_fwd(x1, w_in_l, f"win_fwd_{l}")
        bias = _na_tiles(rpb[l])
        yab = _local_fwd(proj[0], pool_bd[l], pool_scale[l].reshape(1, D_POOL), conv_all[l], f"local_fwd_{l}")
        yc, lse = _attn_fwd(proj[1], bias, f"attn_fwd_{l}")
        z2, x2 = _wout_fwd(x1, yab, yc, w_out_l, lnp(ln_g_all, l, 1), lnp(ln_b_all, l, 1), alpha, f"wout_fwd_{l}")
        a2, u2, h2, z3, x3, *w_next = _ffn_fwd(
            x2, w_ffn2, lnp(ln_g_all, l, 2), lnp(ln_b_all, l, 2), alpha, f"ffn2_fwd_{l}",
            job=_Gather([ffn1_shard[l + 1]]) if l + 1 < n_l else None)
        saved.append((h, a1, u1, h1, z1, x1, proj, bias, yab, yc, lse, z2, x2, a2, u2, h2, z3, w_ffn1, w_in_l, w_out_l, w_ffn2))
        h = x3
        if w_next:
            w_ffn1 = w_next[0]

    sq, dh = _loss_grad(h, loss_target.reshape(s, d), "loss_head")
    loss = lax.psum(sq[0, 0] * (0.5 / d), MESH_AXES)

    flat = lambda blocks: [b.reshape(N_DEV, -1, d) for b in blocks]
    pair_add = lambda blocks, got, tag: [_pair_add(b, g, core, f"grads_pair_add_{tag}_{i}")
                                         for i, (b, g) in enumerate(zip(blocks, got))]
    small_grads = [None] * n_l
    reduced = [None] * n_l
    above = None
    for l in reversed(range(n_l)):
        x0, a1, u1, h1, z1, x1, proj, bias, yab, yc, lse, z2, x2, a2, u2, h2, z3, w_ffn1, w_in_l, w_out_l, w_ffn2 = saved[l]
        dx2, da, du, df, dg3, db3, *got = _ffn_bwd_dx(
            dh, z3, a2, u2, w_ffn2, lnp(ln_g_all, l, 2), alpha, f"ffn2_bwd_dx_{l}",
            job=_PairExchange(above) if above else None)
        above_pairs = pair_add(above, got, f"mix_{l + 1}") if above else None
        g2 = flat([_ffn_bwd_dwd(h2, df, _ffn_bwd_dwgu(da, du, x2, fs, f"ffn2_bwd_dwgu_{l}"), f"ffn2_bwd_dwd_{l}")])
        dmix, dxp, dg2, db2, g_out, *got = _wout_bwd(dx2, z2, yab, yc, w_out_l, lnp(ln_g_all, l, 1), alpha,
                                                     f"wout_bwd_{l}", job=_PairExchange(g2))
        p2 = pair_add(g2, got, f"ffn2_{l}")
        dq, dk, dv, dtiles, *crossed = _attn_bwd(proj[1], bias, yc, dmix, lse, f"attn_bwd_{l}",
                                                 job=_ChipExchange(above_pairs) if above else None)
        if above:
            reduced[l + 1] += crossed
        dloc, dpw, dsc, dcw = _local_bwd(proj[0], dmix, pool_bd[l], pool_scale[l].reshape(1, D_POOL), conv_all[l],
                                         f"local_bwd_{l}")
        dx1, g_in = _win_bwd(dxp, dloc, dq, dk, dv, x1, w_in_l, f"win_bwd_{l}")
        dx0, da, du, df, dg1, db1, *crossed = _ffn_bwd_dx(dx1, z1, a1, u1, w_ffn1, lnp(ln_g_all, l, 0), alpha,
                                                          f"ffn1_bwd_dx_{l}", job=_ChipExchange(p2))
        reduced[l] = list(crossed)
        g1 = _ffn_bwd_dwd(h1, df, _ffn_bwd_dwgu(da, du, x0, fs, f"ffn1_bwd_dwgu_{l}"), f"ffn1_bwd_dwd_{l}")
        above = flat([g_out, g_in, g1])
        drpb = _rpb_finish(dtiles, f"rpb_finish_{l}")
        dpool = jnp.stack([dpw[g * POOL_GROUP:(g + 1) * POOL_GROUP, g * POOL_GROUP:(g + 1) * POOL_GROUP]
                           for g in range(len(POOL_WINDOWS))])
        small_grads[l] = (jnp.concatenate([dg1, dg2, dg3]), jnp.concatenate([db1, db2, db3]), dcw[0:3], dpool, dsc[0], drpb)
        dh = dx0
    grad_x = dh.reshape(x.shape)

    last_pairs = pair_add(above, _exchange_alone(_PairExchange(above), "grads_pair_exchange_last"), "mix_0")
    reduced[0] += _exchange_alone(_ChipExchange(last_pairs), "grads_chip_exchange_last")
    sums = [[_sum_blocks(q, f"grads_chip_sum_{l}_{i}") for i, q in enumerate(reduced[l])] for l in range(n_l)]
    r_ffn2, r_out, r_in, r_ffn1 = [jnp.stack([sums[l][i] for l in range(n_l)]) for i in range(4)]
    r_ffn1, r_ffn2 = r_ffn1.reshape(n_l, 3, fs, d), r_ffn2.reshape(n_l, 3, fs, d)
    row_grads = {"ffn1_w_gate": r_ffn1[:, 0], "ffn1_w_up": r_ffn1[:, 1], "ffn2_w_gate": r_ffn2[:, 0],
                 "ffn2_w_up": r_ffn2[:, 1], "w_in": r_in}
    grads = {"ffn1_w_down": r_ffn1[:, 2], "ffn2_w_down": r_ffn2[:, 2], "w_out": r_out}
    grads.update({n: tr(g) for n, g in row_grads.items()})

    stack = lambda k: jnp.stack([small_grads[l][k] for l in range(n_l)])
    small_shapes = [(n_l, 3, d), (n_l, 3, d), (n_l, 3, D_CONV), pool_w.shape, pool_scale.shape, rpb.shape]
    (small_all,) = _exchange_alone(_Gather([_pack([stack(k) for k in range(6)])]), "gather_small_grads")
    small_sum = _sum_blocks(small_all, "small_grads_sum")
    g_ln_g, g_ln_b, g_conv, g_pool_w, g_pool_scale, g_rpb = _unpack(small_sum, small_shapes)
    own = lambda t, width: lax.dynamic_slice_in_dim(t, me * width, width, axis=2)
    grads.update({"ln_g": own(g_ln_g, ln_w), "ln_b": own(g_ln_b, ln_w), "conv_w": own(g_conv, cw_w),
                  "pool_w": g_pool_w, "pool_scale": g_pool_scale, "rpb": g_rpb})

    weights = dict(ffn1_w_gate=ffn1_w_gate, ffn1_w_up=ffn1_w_up, ffn1_w_down=ffn1_w_down, ffn2_w_gate=ffn2_w_gate,
                   ffn2_w_up=ffn2_w_up, ffn2_w_down=ffn2_w_down, w_in=w_in, pool_w=pool_w, pool_scale=pool_scale,
                   conv_w=conv_w, rpb=rpb, w_out=w_out, ln_g=ln_g, ln_b=ln_b)
    m_in = dict(ffn1_w_gate=m_ffn1_w_gate, ffn1_w_up=m_ffn1_w_up, ffn1_w_down=m_ffn1_w_down, ffn2_w_gate=m_ffn2_w_gate,
                ffn2_w_up=m_ffn2_w_up, ffn2_w_down=m_ffn2_w_down, w_in=m_w_in, pool_w=m_pool_w, pool_scale=m_pool_scale,
                conv_w=m_conv_w, rpb=m_rpb, w_out=m_w_out, ln_g=m_ln_g, ln_b=m_ln_b)
    v_in = dict(ffn1_w_gate=v_ffn1_w_gate, ffn1_w_up=v_ffn1_w_up, ffn1_w_down=v_ffn1_w_down, ffn2_w_gate=v_ffn2_w_gate,
                ffn2_w_up=v_ffn2_w_up, ffn2_w_down=v_ffn2_w_down, w_in=v_w_in, pool_w=v_pool_w, pool_scale=v_pool_scale,
                conv_w=v_conv_w, rpb=v_rpb, w_out=v_w_out, ln_g=v_ln_g, ln_b=v_ln_b)
    names = list(weights)
    large = ["ffn1_w_gate", "ffn1_w_up", "ffn1_w_down", "ffn2_w_gate", "ffn2_w_up", "ffn2_w_down", "w_in", "w_out"]
    tiny = [n for n in names if n not in large]
    delta, new_m, new_v = {}, {}, {}
    for n in large:
        if n in row_grads:
            out = _adamw_nd(tr(weights[n]), row_grads[n], tr(m_in[n]), tr(v_in[n]), f"adamw_{n}")
            delta[n], new_m[n], new_v[n] = (tr(t) for t in out)
        else:
            delta[n], new_m[n], new_v[n] = _adamw_nd(weights[n], grads[n], m_in[n], v_in[n], f"adamw_{n}")
    packed = [_pack([t[n] for n in tiny]) for t in (weights, grads, m_in, v_in)]
    tiny_out = _adamw(*packed, "adamw_small")
    tiny_shapes = [weights[n].shape for n in tiny]
    for res, t in zip((delta, new_m, new_v), tiny_out):
        res.update(dict(zip(tiny, _unpack(t, tiny_shapes))))

    return (loss, grad_x, *[grads[n] for n in names], *[delta[n] for n in names],
            *[new_m[n] for n in names], *[new_v[n] for n in names])
```

```python
import functools

import numpy as np
import jax
import jax.numpy as jnp
from jax import lax
from jax.experimental import pallas as pl
from jax.experimental.pallas import tpu as pltpu

F32, BF16 = jnp.float32, jnp.bfloat16
MESH = pl.DeviceIdType.MESH
N_DEV = 8
MESH_AXES = ("x", "y", "c")

LN_EPS = 1e-5
NEG_INF = -1e30
D_POOL = 256
POOL_WINDOWS = (2, 4, 8, 16)
POOL_GROUP = 64
D_CONV = 256
NA_HEADS = 8
NA_HEAD_DIM = 64
D_NA = NA_HEADS * NA_HEAD_DIM
GRID_W = 64
NA_ROWS = 8
NA_COLS = 16
D_LOC = D_POOL + 3 * D_CONV
D_MIX = D_POOL + D_CONV + D_NA
ADAM_LR, ADAM_B1, ADAM_B2, ADAM_EPS, ADAM_WD, ADAM_STEP = 0.001, 0.9, 0.999, 1e-08, 0.01, 10

VMEM_LIMIT_BYTES = 56 * 1024 * 1024
LANES = 128
BF16_ROWS = 16
HALO = 16
Q_ROWS = 8
K_ROWS = 16
Q_TOK = Q_ROWS * GRID_W
K_TOK = K_ROWS * GRID_W
K_BLK = 4 * GRID_W
HEAD_PAIR = 2 * NA_HEAD_DIM
FFN_CHUNK_DEVS = 4

NT = (((1,), (1,)), ((), ()))
TN = (((0,), (0,)), ((), ()))


def _dot(a, b):
    return jnp.dot(a, b, preferred_element_type=F32)


def _dot_nt(a, b):
    return lax.dot_general(a, b, NT, preferred_element_type=F32)


def _dot_tn(a, b):
    return lax.dot_general(a, b, TN, preferred_element_type=F32)


def _params():
    return pltpu.CompilerParams(vmem_limit_bytes=VMEM_LIMIT_BYTES)


def _row_tile(rows, pref, mult=BF16_ROWS):
    t = min(rows, pref)
    t -= t % mult
    while t > mult and rows % t:
        t -= mult
    assert t > 0 and rows % t == 0, (rows, pref)
    return t


def _mesh_pos():
    return tuple(lax.axis_index(a) for a in MESH_AXES)


def _any_spec():
    return pl.BlockSpec(memory_space=pl.ANY)


class _Gather:
    def __init__(self, shards):
        self.arrays = list(shards)
        n = len(shards)
        self.out_shape = [jax.ShapeDtypeStruct((N_DEV,) + s.shape, s.dtype) for s in shards]
        self.scratch = [pltpu.SemaphoreType.DMA((n, 7)), pltpu.SemaphoreType.DMA((n, 7)), pltpu.SemaphoreType.DMA((n,))]

    def phases(self, ins, outs, sems):
        n = len(ins)
        send_sems, recv_sems, local_sems = sems
        x, y, c = _mesh_pos()
        me, sibling = (x, y, c), (x, y, 1 - c)
        chips = [(1 - x, y), (x, 1 - y), (1 - x, 1 - y)]

        def copy(a, k, block, to, src=None):
            dst = outs[a].at[4 * block[0] + 2 * block[1] + block[2]]
            return pltpu.make_async_remote_copy(
                src_ref=dst if src is None else src, dst_ref=dst,
                send_sem=send_sems.at[a, k], recv_sem=recv_sems.at[a, k],
                device_id=to, device_id_type=MESH)

        def mine():
            return [pltpu.make_async_copy(ins[a], outs[a].at[4 * x + 2 * y + c], local_sems.at[a]) for a in range(n)]

        def first():
            return [cp for a in range(n) for cp in
                    [copy(a, 0, me, sibling, src=ins[a])]
                    + [copy(a, 1 + j, me, (*chip, c), src=ins[a]) for j, chip in enumerate(chips)]]

        def passed():
            return [copy(a, 4 + j, (*chip, c), sibling) for j, chip in enumerate(chips) for a in range(n)]

        def start():
            for cp in mine() + first():
                cp.start()

        def middle():
            for j, chip in enumerate(chips):
                for a in range(n):
                    copy(a, 1 + j, (*chip, c), me).wait_recv()
            for cp in passed():
                cp.start()

        def finish():
            for a in range(n):
                copy(a, 0, sibling, me).wait_recv()
                for j, chip in enumerate(chips):
                    copy(a, 4 + j, (*chip, 1 - c), me).wait_recv()
            for cp in first() + passed():
                cp.wait_send()
            for cp in mine():
                cp.wait()

        return start, middle, finish


class _ChipExchange:
    def __init__(self, parts):
        self.arrays = list(parts)
        n = len(parts)
        self.out_shape = [jax.ShapeDtypeStruct(s.shape, s.dtype) for s in parts]
        self.scratch = [pltpu.SemaphoreType.DMA((n, 3)), pltpu.SemaphoreType.DMA((n, 3)), pltpu.SemaphoreType.DMA((n,))]

    def phases(self, ins, outs, sems):
        n = len(ins)
        send_sems, recv_sems, local_sems = sems
        x, y, c = _mesh_pos()
        my_chip = 2 * x + y
        chips = [(1 - x, y), (x, 1 - y), (1 - x, 1 - y)]

        def own():
            return [pltpu.make_async_copy(ins[a].at[my_chip], outs[a].at[my_chip], local_sems.at[a]) for a in range(n)]

        def copy(a, k, src_chip, dst_chip, to):
            return pltpu.make_async_remote_copy(
                src_ref=ins[a].at[src_chip], dst_ref=outs[a].at[dst_chip],
                send_sem=send_sems.at[a, k], recv_sem=recv_sems.at[a, k],
                device_id=to, device_id_type=MESH)

        def sends():
            return [copy(a, k, 2 * px + py, my_chip, (px, py, c)) for a in range(n) for k, (px, py) in enumerate(chips)]

        def start():
            for cp in own() + sends():
                cp.start()

        def finish():
            for cp in sends():
                cp.wait_send()
            for a in range(n):
                for k, (px, py) in enumerate(chips):
                    copy(a, k, my_chip, 2 * px + py, (px, py, c)).wait_recv()
            for cp in own():
                cp.wait()

        return start, None, finish


def _exchange_alone(job, name):
    n = len(job.arrays)

    def body(*refs):
        for phase in job.phases(refs[:n], refs[n:2 * n], refs[2 * n:]):
            if phase is not None:
                phase()

    return pl.pallas_call(
        body, name=name, out_shape=job.out_shape,
        in_specs=[_any_spec()] * n, out_specs=[_any_spec()] * n, scratch_shapes=job.scratch,
    )(*job.arrays)


def _riding_call(body, job, n_in, n_out, n_steps, step, **kw):
    if job is None:
        return pl.pallas_call(body, **kw)
    n_job, n_sem = len(job.arrays), len(job.scratch)
    kw = dict(kw, in_specs=list(kw["in_specs"]) + [_any_spec()] * n_job,
              out_specs=list(kw["out_specs"]) + [_any_spec()] * n_job,
              out_shape=list(kw["out_shape"]) + job.out_shape,
              scratch_shapes=list(kw.get("scratch_shapes", ())) + job.scratch)

    def riding(*refs):
        ins, job_ins = refs[:n_in], refs[n_in:n_in + n_job]
        outs = refs[n_in + n_job:n_in + n_job + n_out]
        job_outs = refs[n_in + n_job + n_out:n_in + 2 * n_job + n_out]
        scratch = refs[n_in + 2 * n_job + n_out:]
        start, middle, finish = job.phases(job_ins, job_outs, scratch[len(scratch) - n_sem:])
        now = step()
        pl.when(now == 0)(start)
        if middle is not None:
            assert n_steps >= 3
            pl.when(now == (7 * n_steps) // 8 - 1)(middle)
        body(*ins, *outs, *scratch[:len(scratch) - n_sem])
        pl.when(now == n_steps - 1)(finish)

    call = pl.pallas_call(riding, **kw)
    return lambda *args: call(*args, *job.arrays)


class _PairExchange:
    def __init__(self, slabs):
        self.arrays = list(slabs)
        n = len(slabs)
        self.out_shape = [jax.ShapeDtypeStruct((4,) + s.shape[1:], s.dtype) for s in slabs]
        self.scratch = [pltpu.SemaphoreType.DMA((n, 4)), pltpu.SemaphoreType.DMA((n, 4))]

    def phases(self, ins, outs, sems):
        n = len(ins)
        send_sems, recv_sems = sems
        x, y, c = _mesh_pos()

        def copies():
            return [pltpu.make_async_remote_copy(
                src_ref=ins[a].at[2 * j + 1 - c], dst_ref=outs[a].at[j],
                send_sem=send_sems.at[a, j], recv_sem=recv_sems.at[a, j],
                device_id=(x, y, 1 - c), device_id_type=MESH) for a in range(n) for j in range(4)]

        def start():
            for cp in copies():
                cp.start()

        def finish():
            for cp in copies():
                cp.wait_send()
            for cp in copies():
                cp.wait_recv()

        return start, None, finish


def _pair_add(slab, got, core, name):
    _, rows, d = slab.shape
    tr = _row_tile(rows, 1024)

    def body(core_ref, mine_ref, got_ref, out_ref):
        out_ref[...] = (mine_ref[...].astype(F32) + got_ref[...].astype(F32)).astype(out_ref.dtype)

    grid_spec = pltpu.PrefetchScalarGridSpec(
        num_scalar_prefetch=1, grid=(4, rows // tr),
        in_specs=[pl.BlockSpec((1, tr, d), lambda j, r, core_ref: (2 * j + core_ref[0], r, 0)),
                  pl.BlockSpec((1, tr, d), lambda j, r, core_ref: (j, r, 0))],
        out_specs=pl.BlockSpec((1, tr, d), lambda j, r, core_ref: (j, r, 0)))
    return pl.pallas_call(body, name=name, grid_spec=grid_spec,
                          out_shape=jax.ShapeDtypeStruct((4, rows, d), slab.dtype),
                          compiler_params=_params())(core, slab, got)


def _sum_blocks(parts, name):
    k, rows, d = parts.shape
    tr = _row_tile(rows, 512, BF16_ROWS if parts.dtype == BF16 else 8)

    def body(in_ref, out_ref):
        acc = in_ref[0].astype(F32)
        for j in range(1, k):
            acc = acc + in_ref[j].astype(F32)
        out_ref[...] = acc

    return pl.pallas_call(
        body, name=name, grid=(rows // tr,),
        in_specs=[pl.BlockSpec((k, tr, d), lambda r: (0, r, 0))],
        out_specs=pl.BlockSpec((tr, d), lambda r: (r, 0)),
        out_shape=jax.ShapeDtypeStruct((rows, d), F32), compiler_params=_params())(parts)


def _ln_stats(z):
    mu = jnp.mean(z, axis=-1, keepdims=True)
    zc = z - mu
    var = jnp.mean(zc * zc, axis=-1, keepdims=True)
    rstd = lax.rsqrt(var + LN_EPS)
    return zc * rstd, rstd


def _ln_bwd(dy, zhat, rstd, g):
    dyg = dy * g
    m1 = jnp.mean(dyg, axis=-1, keepdims=True)
    m2 = jnp.mean(dyg * zhat, axis=-1, keepdims=True)
    dz = rstd * (dyg - m1 - zhat * m2)
    return dz, jnp.sum(dy * zhat, axis=0, keepdims=True), jnp.sum(dy, axis=0, keepdims=True)


def _accumulate(ref, value, first):
    @pl.when(first)
    def _():
        ref[...] = value

    @pl.when(jnp.logical_not(first))
    def _():
        ref[...] += value


def _add_matmul(acc_ref, first, matmul):
    @pl.when(first)
    def _():
        acc_ref[...] = jnp.zeros_like(acc_ref)

    acc_ref[...] += matmul()


def _ffn_weight_specs(fs, d):
    def spec(row):
        return pl.BlockSpec((N_DEV, 1, fs, d), lambda i: (0, row, 0, 0), pipeline_mode=pl.Buffered(1))
    return [spec(0), spec(1), spec(2)]


def _ffn_fwd(x, w, ln_g, ln_b, alpha, name, job=None):
    s, d = x.shape
    fs = w.shape[2]
    f = N_DEV * fs
    tm = min(s, 256)

    def body(x_ref, wg_ref, wu_ref, wd_ref, g_ref, b_ref, a_ref, u_ref, h_ref, zhat_ref, rstd_ref, y_ref):
        xv = x_ref[...]
        xb = xv.astype(BF16)
        a = _dot_nt(xb, wg_ref[...].reshape(f, d))
        u = _dot_nt(xb, wu_ref[...].reshape(f, d))
        a_ref[...] = a.astype(BF16)
        u_ref[...] = u.astype(BF16)
        h = ((a * jax.nn.sigmoid(a)) * u).astype(BF16)
        h_ref[...] = h
        z = alpha * xv + 0.5 * _dot(h, wd_ref[...].reshape(f, d))
        zhat, rstd = _ln_stats(z)
        zhat_ref[...] = zhat
        rstd_ref[...] = rstd
        y_ref[...] = zhat * g_ref[...] + b_ref[...]

    row = pl.BlockSpec((tm, d), lambda i: (i, 0))
    col = pl.BlockSpec((tm, 1), lambda i: (i, 0))
    vec = pl.BlockSpec((1, d), lambda i: (0, 0))
    hid = pl.BlockSpec((tm, f), lambda i: (i, 0))
    call = _riding_call(
        body, job, 6, 6, s // tm, lambda: pl.program_id(0),
        name=name, grid=(s // tm,),
        in_specs=[row] + _ffn_weight_specs(fs, d) + [vec, vec],
        out_specs=[hid, hid, hid, row, col, row],
        out_shape=[jax.ShapeDtypeStruct((s, f), BF16)] * 3 + [jax.ShapeDtypeStruct((s, d), F32),
                                                               jax.ShapeDtypeStruct((s, 1), F32),
                                                               jax.ShapeDtypeStruct((s, d), F32)],
        compiler_params=_params())
    a, u, h, zhat, rstd, y, *rest = call(x, w, w, w, ln_g, ln_b)
    return [a, u, h, (zhat, rstd), y] + rest


def _ffn_bwd_dx(dy, z, a, u, w, ln_g, alpha, name, job=None):
    s, d = dy.shape
    fs = w.shape[2]
    f = N_DEV * fs
    tm = min(s, 256)

    def body(dy_ref, zhat_ref, rstd_ref, a_ref, u_ref, wg_ref, wu_ref, wd_ref, g_ref,
             dx_ref, da_ref, du_ref, df_ref, dg_ref, db_ref):
        i = pl.program_id(0)
        dz, dg, db = _ln_bwd(dy_ref[...], zhat_ref[...], rstd_ref[...], g_ref[...])
        _accumulate(dg_ref, dg, i == 0)
        _accumulate(db_ref, db, i == 0)
        df = (0.5 * dz).astype(BF16)
        df_ref[...] = df
        av = a_ref[...].astype(F32)
        uv = u_ref[...].astype(F32)
        sg = jax.nn.sigmoid(av)
        dh = _dot_nt(df, wd_ref[...].reshape(f, d))
        du = (dh * (av * sg)).astype(BF16)
        da = (dh * uv * (sg * (1.0 + av * (1.0 - sg)))).astype(BF16)
        da_ref[...] = da
        du_ref[...] = du
        dx_ref[...] = alpha * dz + _dot(da, wg_ref[...].reshape(f, d)) + _dot(du, wu_ref[...].reshape(f, d))

    row = pl.BlockSpec((tm, d), lambda i: (i, 0))
    col = pl.BlockSpec((tm, 1), lambda i: (i, 0))
    vec = pl.BlockSpec((1, d), lambda i: (0, 0))
    hid = pl.BlockSpec((tm, f), lambda i: (i, 0))
    call = _riding_call(
        body, job, 9, 6, s // tm, lambda: pl.program_id(0),
        name=name, grid=(s // tm,),
        in_specs=[row, row, col, hid, hid] + _ffn_weight_specs(fs, d) + [vec],
        out_specs=[row, hid, hid, row, vec, vec],
        out_shape=[jax.ShapeDtypeStruct((s, d), F32)] + [jax.ShapeDtypeStruct((s, f), BF16)] * 2
                  + [jax.ShapeDtypeStruct((s, d), BF16)] + [jax.ShapeDtypeStruct((1, d), F32)] * 2,
        compiler_params=_params())
    return call(dy, *z, a, u, w, w, w, ln_g)


def _ffn_bwd_dwgu(da, du, x, fs, name):
    s, d = x.shape
    tf = FFN_CHUNK_DEVS * fs
    n_c = N_DEV // FFN_CHUNK_DEVS
    tk = min(s, 1024)
    n_k = s // tk

    def body(da_ref, du_ref, x_ref, out_ref, accg_s, accu_s):
        k = pl.program_id(1)
        xb = x_ref[...].astype(BF16)
        _add_matmul(accg_s, k == 0, lambda: _dot_tn(da_ref[...], xb))
        _add_matmul(accu_s, k == 0, lambda: _dot_tn(du_ref[...], xb))

        @pl.when(k == n_k - 1)
        def _():
            out_ref[:, 0] = accg_s[...].astype(BF16).reshape(FFN_CHUNK_DEVS, fs, d)
            out_ref[:, 1] = accu_s[...].astype(BF16).reshape(FFN_CHUNK_DEVS, fs, d)

    hid = pl.BlockSpec((tk, tf), lambda c, k: (k, c))
    return pl.pallas_call(
        body, name=name, grid=(n_c, n_k),
        in_specs=[hid, hid, pl.BlockSpec((tk, d), lambda c, k: (k, 0))],
        out_specs=pl.BlockSpec((FFN_CHUNK_DEVS, 2, fs, d), lambda c, k: (c, 0, 0, 0), pipeline_mode=pl.Buffered(1)),
        out_shape=jax.ShapeDtypeStruct((N_DEV, 3, fs, d), BF16),
        scratch_shapes=[pltpu.VMEM((tf, d), F32), pltpu.VMEM((tf, d), F32)],
        compiler_params=_params())(da, du, x)


def _ffn_bwd_dwd(h, df, blocks, name):
    s, d = df.shape
    fs = blocks.shape[2]
    tf = FFN_CHUNK_DEVS * fs
    n_c = N_DEV // FFN_CHUNK_DEVS
    tk = min(s, 1024)
    n_k = s // tk

    def body(h_ref, df_ref, blocks_ref, out_ref, acc_s):
        k = pl.program_id(1)
        _add_matmul(acc_s, k == 0, lambda: _dot_tn(h_ref[...], df_ref[...]))

        @pl.when(k == n_k - 1)
        def _():
            out_ref[:, 0] = acc_s[...].astype(BF16).reshape(FFN_CHUNK_DEVS, fs, d)

    return pl.pallas_call(
        body, name=name, grid=(n_c, n_k),
        in_specs=[pl.BlockSpec((tk, tf), lambda c, k: (k, c)), pl.BlockSpec((tk, d), lambda c, k: (k, 0)), _any_spec()],
        out_specs=pl.BlockSpec((FFN_CHUNK_DEVS, 1, fs, d), lambda c, k: (c, 2, 0, 0), pipeline_mode=pl.Buffered(1)),
        out_shape=jax.ShapeDtypeStruct(blocks.shape, BF16), input_output_aliases={2: 0},
        scratch_shapes=[pltpu.VMEM((tf, d), F32)],
        compiler_params=_params())(h, df, blocks)


def _whole(arr):
    return pl.BlockSpec(arr.shape, lambda i: (0,) * arr.ndim, pipeline_mode=pl.Buffered(1))


def _win_fwd(x, w_in, name):
    s, d = x.shape
    d_in = N_DEV * w_in.shape[1]
    tm = min(s, 512)
    scale = NA_HEAD_DIM ** -0.5
    assert d_in == D_LOC + 3 * D_NA and scale == 0.125

    def body(x_ref, w_ref, loc_ref, qkv_ref):
        proj = _dot_nt(x_ref[...].astype(BF16), w_ref[...].reshape(d_in, d))
        loc_ref[...] = proj[:, :D_LOC]
        qkv_ref[:, :D_NA] = (proj[:, D_LOC:D_LOC + D_NA] * scale).astype(BF16)
        qkv_ref[:, D_NA:] = proj[:, D_LOC + D_NA:].astype(BF16)

    return pl.pallas_call(
        body, name=name, grid=(s // tm,),
        in_specs=[pl.BlockSpec((tm, d), lambda i: (i, 0)), _whole(w_in)],
        out_specs=[pl.BlockSpec((tm, D_LOC), lambda i: (i, 0)), pl.BlockSpec((tm, 3 * D_NA), lambda i: (i, 0))],
        out_shape=[jax.ShapeDtypeStruct((s, D_LOC), F32), jax.ShapeDtypeStruct((s, 3 * D_NA), BF16)],
        compiler_params=_params())(x, w_in)


def _wout_fwd(x, yab, yc, w_out, ln_g, ln_b, alpha, name):
    s, d = x.shape
    tm = min(s, 512)

    def body(x_ref, yab_ref, yc_ref, w_ref, g_ref, b_ref, zhat_ref, rstd_ref, y_ref):
        mix = jnp.concatenate([yab_ref[...], yc_ref[...]], axis=1).astype(BF16)
        z = alpha * x_ref[...] + _dot(mix, w_ref[...].reshape(D_MIX, d))
        zhat, rstd = _ln_stats(z)
        zhat_ref[...] = zhat
        rstd_ref[...] = rstd
        y_ref[...] = zhat * g_ref[...] + b_ref[...]

    row = pl.BlockSpec((tm, d), lambda i: (i, 0))
    col = pl.BlockSpec((tm, 1), lambda i: (i, 0))
    half = pl.BlockSpec((tm, D_MIX // 2), lambda i: (i, 0))
    vec = pl.BlockSpec((1, d), lambda i: (0, 0))
    zhat, rstd, y = pl.pallas_call(
        body, name=name, grid=(s // tm,),
        in_specs=[row, half, half, _whole(w_out), vec, vec],
        out_specs=[row, col, row],
        out_shape=[jax.ShapeDtypeStruct((s, d), F32), jax.ShapeDtypeStruct((s, 1), F32), jax.ShapeDtypeStruct((s, d), F32)],
        compiler_params=_params())(x, yab, yc, w_out, ln_g, ln_b)
    return (zhat, rstd), y


def _wout_bwd(dy, z, yab, yc, w_out, ln_g, alpha, name, job=None):
    s, d = dy.shape
    rs = w_out.shape[1]
    tm = min(s, 512)
    n_i = s // tm

    def body(dy_ref, zhat_ref, rstd_ref, yab_ref, yc_ref, w_ref, g_ref, dmix_ref, dxp_ref, dg_ref, db_ref, out_ref, acc_s):
        i = pl.program_id(0)
        dz, dg, db = _ln_bwd(dy_ref[...], zhat_ref[...], rstd_ref[...], g_ref[...])
        _accumulate(dg_ref, dg, i == 0)
        _accumulate(db_ref, db, i == 0)
        dxp_ref[...] = alpha * dz
        dzb = dz.astype(BF16)
        dmix_ref[...] = _dot_nt(dzb, w_ref[...].reshape(D_MIX, d))
        mix = jnp.concatenate([yab_ref[...], yc_ref[...]], axis=1).astype(BF16)
        _add_matmul(acc_s, i == 0, lambda: _dot_tn(mix, dzb))

        @pl.when(i == n_i - 1)
        def _():
            out_ref[...] = acc_s[...].astype(BF16).reshape(N_DEV, rs, d)

    row = pl.BlockSpec((tm, d), lambda i: (i, 0))
    half = pl.BlockSpec((tm, D_MIX // 2), lambda i: (i, 0))
    vec = pl.BlockSpec((1, d), lambda i: (0, 0))
    call = _riding_call(
        body, job, 7, 5, n_i, lambda: pl.program_id(0),
        name=name, grid=(n_i,),
        in_specs=[row, row, pl.BlockSpec((tm, 1), lambda i: (i, 0)), half, half, _whole(w_out), vec],
        out_specs=[pl.BlockSpec((tm, D_MIX), lambda i: (i, 0)), row, vec, vec, _whole(w_out)],
        out_shape=[jax.ShapeDtypeStruct((s, D_MIX), F32), jax.ShapeDtypeStruct((s, d), F32),
                   jax.ShapeDtypeStruct((1, d), F32), jax.ShapeDtypeStruct((1, d), F32),
                   jax.ShapeDtypeStruct(w_out.shape, BF16)],
        scratch_shapes=[pltpu.VMEM((D_MIX, d), F32)],
        compiler_params=_params())
    return call(dy, *z, yab, yc, w_out, ln_g)


def _win_bwd(dxp, dloc, dq, dk, dv, x, w_in, name):
    s, d = x.shape
    rs = w_in.shape[1]
    d_in = N_DEV * rs
    tm = min(s, 512)
    n_i = s // tm

    def body(dxp_ref, dloc_ref, dq_ref, dk_ref, dv_ref, x_ref, w_ref, dx_ref, out_ref, acc_s):
        i = pl.program_id(0)
        dp = jnp.concatenate([dloc_ref[...], dq_ref[...], dk_ref[...].astype(BF16), dv_ref[...].astype(BF16)], axis=1)
        dx_ref[...] = dxp_ref[...] + _dot(dp, w_ref[...].reshape(d_in, d))
        _add_matmul(acc_s, i == 0, lambda: _dot_tn(dp, x_ref[...].astype(BF16)))

        @pl.when(i == n_i - 1)
        def _():
            out_ref[...] = acc_s[...].astype(BF16).reshape(N_DEV, rs, d)

    row = pl.BlockSpec((tm, d), lambda i: (i, 0))
    na = pl.BlockSpec((tm, D_NA), lambda i: (i, 0))
    return pl.pallas_call(
        body, name=name, grid=(n_i,),
        in_specs=[row, pl.BlockSpec((tm, D_LOC), lambda i: (i, 0)), na, na, na, row, _whole(w_in)],
        out_specs=[row, _whole(w_in)],
        out_shape=[jax.ShapeDtypeStruct((s, d), F32), jax.ShapeDtypeStruct(w_in.shape, BF16)],
        scratch_shapes=[pltpu.VMEM((d_in, d), F32)],
        compiler_params=_params())(dxp, dloc, dq, dk, dv, x, w_in)


def _shift_rows(v, k):
    n = v.shape[0]
    return pltpu.roll(v, k % n, 0)


def _halo_specs(tm, s, width, col):
    per = tm // HALO
    last = s // HALO - 1
    return [pl.BlockSpec((HALO, width), lambda i: (jnp.maximum(i * per - 1, 0), col)),
            pl.BlockSpec((tm, width), lambda i: (i, col)),
            pl.BlockSpec((HALO, width), lambda i: (jnp.minimum((i + 1) * per, last), col))]


def _token_index(i, tm):
    return i * tm - HALO + lax.broadcasted_iota(jnp.int32, (tm + 2 * HALO, 1), 0)


def _pool_lane_tables():
    lane = lax.broadcasted_iota(jnp.int32, (1, D_POOL), 1)
    group = sum((lane >= g * POOL_GROUP).astype(jnp.int32) for g in range(1, len(POOL_WINDOWS)))
    half = jnp.where(group == 0, 1, jnp.where(group == 1, 2, jnp.where(group == 2, 4, 8)))
    return group, half


def _window_sums(v, group, offsets):
    s2 = v + _shift_rows(v, 1)
    s4 = s2 + _shift_rows(s2, 2)
    s8 = s4 + _shift_rows(s4, 4)
    s16 = s8 + _shift_rows(s8, 8)
    parts = [_shift_rows(p, -o) if o else p for p, o in zip((s2, s4, s8, s16), offsets)]
    return jnp.where(group == 0, parts[0], jnp.where(group == 1, parts[1], jnp.where(group == 2, parts[2], parts[3])))


def _pool_counts(tok, half, s):
    return (jnp.minimum(tok + half, s) - jnp.maximum(tok - half, 0)).astype(F32)


def _pool_forward(u, tok, s):
    group, half = _pool_lane_tables()
    sums = _window_sums(u, group, [w // 2 - 1 for w in POOL_WINDOWS])
    return sums / _pool_counts(tok, half, s) - u


def _conv_forward(zc, cw_ref):
    return cw_ref[0:1, :] * _shift_rows(zc, 1) + cw_ref[1:2, :] * zc + cw_ref[2:3, :] * _shift_rows(zc, -1)


def _local_fwd(proj, pool_bd, pool_scale, conv_w, name):
    s = proj.shape[0]
    tm = min(s, 512)
    ctr = slice(HALO, HALO + tm)

    def body(prev_ref, cur_ref, next_ref, pw_ref, sc_ref, cw_ref, out_ref):
        i = pl.program_id(0)
        ext = jnp.concatenate([prev_ref[...], cur_ref[...], next_ref[...]], axis=0)
        tok = _token_index(i, tm)
        inside = (tok >= 0) & (tok < s)
        u = jnp.where(inside, ext[:, 0:D_POOL], 0.0)
        p = _pool_forward(u, tok, s)[ctr]
        ya = _dot(p.astype(BF16), pw_ref[...]) * sc_ref[...]
        gb = ext[:, D_POOL:D_POOL + D_CONV]
        zc = jnp.where(inside, ext[:, D_POOL + D_CONV:D_POOL + 2 * D_CONV] * ext[:, D_POOL + 2 * D_CONV:D_LOC], 0.0)
        yb = (gb * _conv_forward(zc, cw_ref))[ctr]
        out_ref[...] = jnp.concatenate([ya, yb], axis=1)

    return pl.pallas_call(
        body, name=name, grid=(s // tm,),
        in_specs=_halo_specs(tm, s, D_LOC, 0) + [
            pl.BlockSpec((D_POOL, D_POOL), lambda i: (0, 0)), pl.BlockSpec((1, D_POOL), lambda i: (0, 0)),
            pl.BlockSpec((3, D_CONV), lambda i: (0, 0))],
        out_specs=pl.BlockSpec((tm, D_POOL + D_CONV), lambda i: (i, 0)),
        out_shape=jax.ShapeDtypeStruct((s, D_POOL + D_CONV), F32),
        compiler_params=_params())(proj, proj, proj, pool_bd, pool_scale, conv_w)


def _local_bwd(proj, dmix, pool_bd, pool_scale, conv_w, name):
    s = proj.shape[0]
    tm = min(s, 512)
    ctr = slice(HALO, HALO + tm)

    def body(prev_ref, cur_ref, next_ref, dprev_ref, dcur_ref, dnext_ref, pw_ref, sc_ref, cw_ref,
             dloc_ref, dpw_ref, dsc_ref, dcw_ref):
        i = pl.program_id(0)
        first = i == 0
        ext = jnp.concatenate([prev_ref[...], cur_ref[...], next_ref[...]], axis=0)
        dext = jnp.concatenate([dprev_ref[...], dcur_ref[...], dnext_ref[...]], axis=0)
        tok = _token_index(i, tm)
        inside = (tok >= 0) & (tok < s)
        group, half = _pool_lane_tables()
        cnt = _pool_counts(tok, half, s)
        u = jnp.where(inside, ext[:, 0:D_POOL], 0.0)
        dya = jnp.where(inside, dext[:, 0:D_POOL], 0.0)
        p_c = _pool_forward(u, tok, s)[ctr].astype(BF16)
        lin = _dot(p_c, pw_ref[...])
        _accumulate(dsc_ref, jnp.sum(dya[ctr] * lin, axis=0, keepdims=True), first)
        e1 = (dya * sc_ref[...]).astype(BF16)
        _accumulate(dpw_ref, _dot_tn(p_c, e1[ctr]), first)
        dp = _dot_nt(e1, pw_ref[...])
        du = _window_sums(dp / cnt, group, [w // 2 for w in POOL_WINDOWS]) - dp
        gb = ext[:, D_POOL:D_POOL + D_CONV]
        gc = ext[:, D_POOL + D_CONV:D_POOL + 2 * D_CONV]
        hv = ext[:, D_POOL + 2 * D_CONV:D_LOC]
        zc = jnp.where(inside, gc * hv, 0.0)
        dyb = jnp.where(inside, dext[:, D_POOL:D_POOL + D_CONV], 0.0)
        dgb = dyb * _conv_forward(zc, cw_ref)
        dyc = dyb * gb
        for k in range(3):
            part = jnp.sum(dyc[ctr] * _shift_rows(zc, 1 - k)[ctr], axis=0, keepdims=True)
            _accumulate(dcw_ref.at[k:k + 1, :], part, first)
        dzc = cw_ref[0:1, :] * _shift_rows(dyc, -1) + cw_ref[1:2, :] * dyc + cw_ref[2:3, :] * _shift_rows(dyc, 1)
        dloc = jnp.concatenate([du, dgb, dzc * hv, dzc * gc], axis=1)
        dloc_ref[...] = dloc[ctr].astype(BF16)

    return pl.pallas_call(
        body, name=name, grid=(s // tm,),
        in_specs=_halo_specs(tm, s, D_LOC, 0) + _halo_specs(tm, s, D_POOL + D_CONV, 0) + [
            pl.BlockSpec((D_POOL, D_POOL), lambda i: (0, 0)), pl.BlockSpec((1, D_POOL), lambda i: (0, 0)),
            pl.BlockSpec((3, D_CONV), lambda i: (0, 0))],
        out_specs=[pl.BlockSpec((tm, D_LOC), lambda i: (i, 0)), pl.BlockSpec((D_POOL, D_POOL), lambda i: (0, 0)),
                   pl.BlockSpec((1, D_POOL), lambda i: (0, 0)), pl.BlockSpec((8, D_CONV), lambda i: (0, 0))],
        out_shape=[jax.ShapeDtypeStruct((s, D_LOC), BF16), jax.ShapeDtypeStruct((D_POOL, D_POOL), F32),
                   jax.ShapeDtypeStruct((1, D_POOL), F32), jax.ShapeDtypeStruct((8, D_CONV), F32)],
        compiler_params=_params())(proj, proj, proj, dmix, dmix, dmix, pool_bd, pool_scale, conv_w)


def _na_geometry(rows):
    n_j = rows // Q_ROWS
    dr = np.full((3, Q_ROWS, K_ROWS), 2 * NA_ROWS - 1, np.int64)
    for t, j in enumerate((0, min(1, n_j - 1), n_j - 1)):
        base = int(np.clip(Q_ROWS * j - NA_ROWS // 2, 0, rows - K_ROWS))
        for qr in range(Q_ROWS):
            r = Q_ROWS * j + qr
            start = int(np.clip(r - NA_ROWS // 2, 0, rows - NA_ROWS))
            for kr in range(K_ROWS):
                if start <= base + kr < start + NA_ROWS:
                    dr[t, qr, kr] = base + kr - r + NA_ROWS - 1
    return dr


def _na_col_tables():
    c = np.arange(GRID_W)
    start = np.clip(c - NA_COLS // 2, 0, GRID_W - NA_COLS)
    valid = (c[None, :] >= start[:, None]) & (c[None, :] < start[:, None] + NA_COLS)
    dc = np.clip(c[None, :] - c[:, None], -(NA_COLS - 1), NA_COLS - 1) + (NA_COLS - 1)
    return valid, dc


NO_ROW = 2 * NA_ROWS - 1
N_SLOT = 2 * NA_ROWS


def _na_tiles(rpb):
    valid, dc = _na_col_tables()
    onehot = jnp.asarray((dc[None] == np.arange(2 * NA_COLS - 1)[:, None, None]).astype(np.float32))
    table = jnp.einsum("hrd,dqk->hrqk", rpb, onehot, precision=lax.Precision.HIGHEST)
    table = jnp.where(jnp.asarray(valid)[None, None], table, NEG_INF)
    outside = jnp.full((NA_HEADS, 1, GRID_W, GRID_W), NEG_INF, F32)
    padded = jnp.concatenate([outside, table, outside], axis=1)
    pairs = jnp.concatenate([padded[:, :N_SLOT], padded[:, 1:]], axis=-1)
    return jnp.concatenate([pairs, jnp.full((NA_HEADS, 1, GRID_W, 2 * GRID_W), NEG_INF, F32)], axis=1)


G_ROWS = 2
N_GRP = Q_ROWS // G_ROWS
G_TOK = G_ROWS * GRID_W
GK_ROWS = NA_ROWS + G_ROWS
GK_TOK = GK_ROWS * GRID_W
STACK_TOK = N_GRP * 2 * G_TOK


def _na_group_tables(rows):
    dr = _na_geometry(rows)
    koff = np.zeros((3, N_GRP), np.int64)
    slot = np.zeros((3, N_GRP, G_ROWS, GK_ROWS // 2), np.int64)
    even_in, odd_in = np.zeros_like(slot), np.zeros_like(slot)
    for t in range(3):
        for g in range(N_GRP):
            qrs = range(G_ROWS * g, G_ROWS * (g + 1))
            inside = [kr for kr in range(K_ROWS) if any(dr[t, qr, kr] != NO_ROW for qr in qrs)]
            lo, hi = min(inside), max(inside) + 1
            off = min(lo - lo % 2, K_ROWS - GK_ROWS)
            assert off <= lo and hi <= off + GK_ROWS
            koff[t, g] = off
            for qq, qr in enumerate(qrs):
                for kp in range(GK_ROWS // 2):
                    even, odd = int(dr[t, qr, off + 2 * kp]), int(dr[t, qr, off + 2 * kp + 1])
                    even_in[t, g, qq, kp], odd_in[t, g, qq, kp] = even != NO_ROW, odd != NO_ROW
                    slot[t, g, qq, kp] = (N_SLOT if even == NO_ROW and odd == NO_ROW
                                          else (even if even != NO_ROW else odd - 1) + 1)
    return koff, slot, even_in, odd_in


def _by_type(block_type, per_type):
    a, b, c = (int(v) for v in per_type)
    if a == b == c:
        return a
    return jnp.where(block_type == 0, a, jnp.where(block_type == 2, c, b))


def _score_rows(g, hh):
    first = (2 * g + hh) * G_TOK
    return slice(first, first + G_TOK)


def _tile_at(g, hh, qq, kp):
    first = _score_rows(g, hh).start + qq * GRID_W
    return slice(first, first + GRID_W), slice(kp * 2 * GRID_W, (kp + 1) * 2 * GRID_W)


def _fill_bias(bias_s, tiles_ref, block_type, tables):
    _, slot, even_in, odd_in = tables
    left = lax.broadcasted_iota(jnp.int32, (1, 2 * GRID_W), 1) < GRID_W
    for hh in range(2):
        for g in range(N_GRP):
            for qq in range(G_ROWS):
                for kp in range(GK_ROWS // 2):
                    tile = tiles_ref[hh, _by_type(block_type, slot[:, g, qq, kp])]
                    tile = jnp.where(left & (_by_type(block_type, even_in[:, g, qq, kp]) == 0), NEG_INF, tile)
                    tile = jnp.where(jnp.logical_not(left) & (_by_type(block_type, odd_in[:, g, qq, kp]) == 0), NEG_INF, tile)
                    rs, cs = _tile_at(g, hh, qq, kp)
                    bias_s[rs, cs] = tile


def _group_offset(block_type, koff, g):
    off = _by_type(block_type, koff[:, g]) * GRID_W
    return off if isinstance(off, int) else pl.multiple_of(off, 2 * GRID_W)


def _na_specs(s, proj_cols):
    n_blk = s // K_BLK
    per = Q_TOK // K_BLK

    def kv_spec(col0, m):
        return pl.BlockSpec((K_BLK, HEAD_PAIR), lambda hp, j: (jnp.clip(per * j - 1, 0, n_blk - 4) + m, col0 + hp))

    q_col, k_col, v_col = (c // HEAD_PAIR for c in proj_cols)
    return ([pl.BlockSpec((Q_TOK, HEAD_PAIR), lambda hp, j: (j, q_col + hp))]
            + [kv_spec(k_col, m) for m in range(4)] + [kv_spec(v_col, m) for m in range(4)])


def _na_block_type(j, n_j):
    return jnp.where(j == 0, 0, jnp.where(j == n_j - 1, 2, 1))


def _head_masks():
    lane = lax.broadcasted_iota(jnp.int32, (1, HEAD_PAIR), 1)
    return [lane < NA_HEAD_DIM, lane >= NA_HEAD_DIM]


def _attn_fwd(qkv, tiles, name):
    s = qkv.shape[0]
    n_j = s // Q_TOK
    tables = _na_group_tables(s // GRID_W)
    koff = tables[0]

    def body(q_ref, k0, k1, k2, k3, v0, v1, v2, v3, tiles_ref, o_ref, lse_ref, bias_s, k_s, vh_s, sc_s, p_s):
        j = pl.program_id(1)
        block_type = _na_block_type(j, n_j)
        pl.when((j == 0) | (j == 1) | (j == n_j - 1))(functools.partial(_fill_bias, bias_s, tiles_ref, block_type, tables))
        masks = _head_masks()
        for m, (kr, vr) in enumerate(zip((k0, k1, k2, k3), (v0, v1, v2, v3))):
            rows = slice(m * K_BLK, (m + 1) * K_BLK)
            k_s[rows, :] = kr[...]
            v = vr[...]
            for hh, mask in enumerate(masks):
                vh_s[hh, rows, :] = jnp.where(mask, v, jnp.zeros_like(v))
        q = q_ref[...]
        qh = [jnp.where(mask, q, jnp.zeros_like(q)) for mask in masks]
        offs = [_group_offset(block_type, koff, g) for g in range(N_GRP)]
        for g in range(N_GRP):
            kg = k_s[pl.ds(offs[g], GK_TOK), :]
            for hh in range(2):
                sc_s[_score_rows(g, hh), :] = _dot_nt(qh[hh][g * G_TOK:(g + 1) * G_TOK], kg)
        sc = sc_s[...] + bias_s[...]
        mx = jnp.max(sc, axis=-1, keepdims=True)
        p = jnp.exp(sc - mx)
        den = jnp.sum(p, axis=-1, keepdims=True)
        p_s[...] = p.astype(BF16)
        inv = 1.0 / den
        lse = mx + jnp.log(den)
        for g in range(N_GRP):
            rows = slice(g * G_TOK, (g + 1) * G_TOK)
            out = jnp.zeros((G_TOK, HEAD_PAIR), F32)
            for hh in range(2):
                sr = _score_rows(g, hh)
                out = out + _dot(p_s[sr, :], vh_s[hh, pl.ds(offs[g], GK_TOK), :]) * inv[sr]
            o_ref[rows, :] = out
            lse_ref[0, rows, :] = jnp.where(masks[0], lse[_score_rows(g, 0)], lse[_score_rows(g, 1)])

    return pl.pallas_call(
        body, name=name, grid=(NA_HEADS // 2, n_j),
        in_specs=_na_specs(s, (0, D_NA, 2 * D_NA)) + [
            pl.BlockSpec((2, N_SLOT + 1, GRID_W, 2 * GRID_W), lambda hp, j: (hp, 0, 0, 0))],
        out_specs=[pl.BlockSpec((Q_TOK, HEAD_PAIR), lambda hp, j: (j, hp)),
                   pl.BlockSpec((1, Q_TOK, HEAD_PAIR), lambda hp, j: (hp, j, 0))],
        out_shape=[jax.ShapeDtypeStruct((s, D_NA), F32), jax.ShapeDtypeStruct((NA_HEADS // 2, s, HEAD_PAIR), F32)],
        scratch_shapes=[pltpu.VMEM((STACK_TOK, GK_TOK), F32), pltpu.VMEM((K_TOK, HEAD_PAIR), BF16),
                        pltpu.VMEM((2, K_TOK, HEAD_PAIR), BF16), pltpu.VMEM((STACK_TOK, GK_TOK), F32),
                        pltpu.VMEM((STACK_TOK, GK_TOK), BF16)],
        compiler_params=_params())(*([qkv] * 9), tiles)


def _add_tiles(dtile_ref, ds_ref, block_type, slot, has_interior):
    def tile(g, hh, qq, kp):
        rs, cs = _tile_at(g, hh, qq, kp)
        return ds_ref[rs, cs].astype(F32)

    def interior():
        for hh in range(2):
            for qq in range(G_ROWS):
                for kp in range(GK_ROWS // 2):
                    assert (slot[1, :, qq, kp] == slot[1, 0, qq, kp]).all()
                    if slot[1, 0, qq, kp] != N_SLOT:
                        dtile_ref[hh, int(slot[1, 0, qq, kp])] += sum(tile(g, hh, qq, kp) for g in range(N_GRP))

    def edge():
        for hh in range(2):
            for g in range(N_GRP):
                for qq in range(G_ROWS):
                    for kp in range(GK_ROWS // 2):
                        first, last = (0 if e == N_SLOT else int(e) for e in slot[[0, 2], g, qq, kp])
                        if (slot[[0, 2], g, qq, kp] != N_SLOT).any():
                            dtile_ref[hh, _by_type(block_type, (first, first, last))] += tile(g, hh, qq, kp)

    if has_interior:
        pl.when(block_type == 1)(interior)
    pl.when(block_type != 1)(edge)


def _attn_bwd(qkv, tiles, o, dmix, lse, name, job=None):
    s = qkv.shape[0]
    n_j = s // Q_TOK
    n_blk = s // K_BLK
    per = Q_TOK // K_BLK
    scale = NA_HEAD_DIM ** -0.5
    do_col = (D_POOL + D_CONV) // HEAD_PAIR
    tables = _na_group_tables(s // GRID_W)
    koff, slot = tables[0], tables[1]

    def body(q_ref, k0, k1, k2, k3, v0, v1, v2, v3, tiles_ref, o_ref, do_ref, lse_ref,
             dq_ref, dk_ref, dv_ref, dtile_ref, bias_s, k_s, kh_s, v_s, s_s, dp_s, pb_s, dsb_s):
        j = pl.program_id(1)

        @pl.when(j == 0)
        def _():
            dk_ref[...] = jnp.zeros_like(dk_ref)
            dv_ref[...] = jnp.zeros_like(dv_ref)
            dtile_ref[...] = jnp.zeros_like(dtile_ref)

        block_type = _na_block_type(j, n_j)
        pl.when((j == 0) | (j == 1) | (j == n_j - 1))(functools.partial(_fill_bias, bias_s, tiles_ref, block_type, tables))
        base = pl.multiple_of(jnp.clip(per * j - 1, 0, n_blk - 4) * K_BLK, K_BLK)
        masks = _head_masks()
        for m, (kr, vr) in enumerate(zip((k0, k1, k2, k3), (v0, v1, v2, v3))):
            rows = slice(m * K_BLK, (m + 1) * K_BLK)
            k = kr[...]
            k_s[rows, :] = k
            v_s[rows, :] = vr[...]
            for hh, mask in enumerate(masks):
                kh_s[hh, rows, :] = jnp.where(mask, k, jnp.zeros_like(k))
        q = q_ref[...]
        qh = [jnp.where(mask, q, jnp.zeros_like(q)) for mask in masks]
        lane = lax.broadcasted_iota(jnp.int32, (1, HEAD_PAIR), 1)
        offs = [_group_offset(block_type, koff, g) for g in range(N_GRP)]
        do, ov, lse = do_ref[...], o_ref[...], lse_ref[0]
        dob, lse_col, delta_col = {}, [], []
        for g in range(N_GRP):
            rows = slice(g * G_TOK, (g + 1) * G_TOK)
            kg = k_s[pl.ds(offs[g], GK_TOK), :]
            vg = v_s[pl.ds(offs[g], GK_TOK), :]
            for hh, mask in enumerate(masks):
                doh = jnp.where(mask, do[rows], 0.0)
                dob[g, hh] = doh.astype(BF16)
                lse_col.append(jnp.sum(jnp.where(lane == hh * NA_HEAD_DIM, lse[rows], 0.0), axis=-1, keepdims=True))
                delta_col.append(jnp.sum(doh * ov[rows], axis=-1, keepdims=True))
                s_s[_score_rows(g, hh), :] = _dot_nt(qh[hh][rows], kg)
                dp_s[_score_rows(g, hh), :] = _dot_nt(dob[g, hh], vg)
        p = jnp.exp(s_s[...] + bias_s[...] - jnp.concatenate(lse_col, axis=0))
        ds = p * (dp_s[...] - jnp.concatenate(delta_col, axis=0))
        pb_s[...] = p.astype(BF16)
        dsb_s[...] = ds.astype(BF16)
        _add_tiles(dtile_ref, dsb_s, block_type, slot, n_j > 2)
        for g in range(N_GRP):
            rows = slice(g * G_TOK, (g + 1) * G_TOK)
            dq = jnp.zeros((G_TOK, HEAD_PAIR), F32)
            dk = jnp.zeros((GK_TOK, HEAD_PAIR), F32)
            dv = jnp.zeros((GK_TOK, HEAD_PAIR), F32)
            for hh in range(2):
                sr = _score_rows(g, hh)
                dsb = dsb_s[sr, :]
                dq = dq + _dot(dsb, kh_s[hh, pl.ds(offs[g], GK_TOK), :])
                dk = dk + _dot_tn(dsb, qh[hh][rows])
                dv = dv + _dot_tn(pb_s[sr, :], dob[g, hh])
            dq_ref[rows, :] = (dq * scale).astype(BF16)
            at = pl.multiple_of(base + offs[g], 2 * GRID_W)
            dk_ref[pl.ds(at, GK_TOK), :] += dk
            dv_ref[pl.ds(at, GK_TOK), :] += dv

    pair = pl.BlockSpec((Q_TOK, HEAD_PAIR), lambda hp, j: (j, hp))
    whole = pl.BlockSpec((s, HEAD_PAIR), lambda hp, j: (0, hp))
    call = _riding_call(
        body, job, 13, 4, (NA_HEADS // 2) * n_j, lambda: pl.program_id(0) * n_j + pl.program_id(1),
        name=name, grid=(NA_HEADS // 2, n_j),
        in_specs=_na_specs(s, (0, D_NA, 2 * D_NA)) + [
            pl.BlockSpec((2, N_SLOT + 1, GRID_W, 2 * GRID_W), lambda hp, j: (hp, 0, 0, 0)),
            pair, pl.BlockSpec((Q_TOK, HEAD_PAIR), lambda hp, j: (j, do_col + hp)),
            pl.BlockSpec((1, Q_TOK, HEAD_PAIR), lambda hp, j: (hp, j, 0))],
        out_specs=[pair, whole, whole, pl.BlockSpec((2, N_SLOT, GRID_W, 2 * GRID_W), lambda hp, j: (hp, 0, 0, 0))],
        out_shape=[jax.ShapeDtypeStruct((s, D_NA), BF16), jax.ShapeDtypeStruct((s, D_NA), F32),
                   jax.ShapeDtypeStruct((s, D_NA), F32),
                   jax.ShapeDtypeStruct((NA_HEADS, N_SLOT, GRID_W, 2 * GRID_W), F32)],
        scratch_shapes=[pltpu.VMEM((STACK_TOK, GK_TOK), F32), pltpu.VMEM((K_TOK, HEAD_PAIR), BF16),
                        pltpu.VMEM((2, K_TOK, HEAD_PAIR), BF16), pltpu.VMEM((K_TOK, HEAD_PAIR), BF16),
                        pltpu.VMEM((STACK_TOK, GK_TOK), F32), pltpu.VMEM((STACK_TOK, GK_TOK), F32),
                        pltpu.VMEM((STACK_TOK, GK_TOK), BF16), pltpu.VMEM((STACK_TOK, GK_TOK), BF16)],
        compiler_params=_params())
    return call(*([qkv] * 9), tiles, o, dmix, lse)


def _rpb_finish(tiles, name):
    valid, dc = _na_col_tables()
    n_dc = 2 * NA_COLS - 1
    sel = np.zeros((GRID_W, 2 * GRID_W, LANES), np.float32)
    for qc in range(GRID_W):
        for kc in range(GRID_W):
            if valid[qc, kc]:
                sel[qc, kc, dc[qc, kc]] = 1.0
                sel[qc, GRID_W + kc, LANES // 2 + dc[qc, kc]] = 1.0
    sel = jnp.asarray(sel.reshape(GRID_W * 2 * GRID_W, LANES))
    flat = tiles.reshape(NA_HEADS * 2 * NA_ROWS, GRID_W * 2 * GRID_W)

    def body(a_ref, b_ref, out_ref):
        out_ref[...] = jnp.dot(a_ref[...], b_ref[...], preferred_element_type=F32, precision=lax.Precision.HIGHEST)

    sums = pl.pallas_call(
        body, name=name, out_shape=jax.ShapeDtypeStruct((flat.shape[0], LANES), F32),
        compiler_params=_params())(flat, sel).reshape(NA_HEADS, 2 * NA_ROWS, LANES)
    return sums[:, 1:, :n_dc] + sums[:, :2 * NA_ROWS - 1, LANES // 2:LANES // 2 + n_dc]


def _loss_grad(y, target, name):
    s, d = y.shape
    tm = min(s, 1024)

    def body(y_ref, t_ref, sum_ref, dy_ref):
        diff = y_ref[...] - t_ref[...]
        dy_ref[...] = diff * (1.0 / d)
        part = jnp.zeros((8, LANES), F32) + jnp.sum(diff * diff)
        _accumulate(sum_ref, part, pl.program_id(0) == 0)

    row = pl.BlockSpec((tm, d), lambda i: (i, 0))
    return pl.pallas_call(
        body, name=name, grid=(s // tm,), in_specs=[row, row],
        out_specs=[pl.BlockSpec((8, LANES), lambda i: (0, 0)), row],
        out_shape=[jax.ShapeDtypeStruct((8, LANES), F32), jax.ShapeDtypeStruct((s, d), F32)],
        compiler_params=_params())(y, target)


def _adamw(w, g, m, v, name):
    rows, cols = w.shape
    tr = _row_tile(rows, 512, 8)

    def body(w_ref, g_ref, m_ref, v_ref, d_ref, nm_ref, nv_ref):
        gv = g_ref[...]
        nm = ADAM_B1 * m_ref[...] + (1.0 - ADAM_B1) * gv
        nv = ADAM_B2 * v_ref[...] + (1.0 - ADAM_B2) * (gv * gv)
        m_hat = nm / (1.0 - ADAM_B1 ** ADAM_STEP)
        v_hat = nv / (1.0 - ADAM_B2 ** ADAM_STEP)
        d_ref[...] = -ADAM_LR * (m_hat / (jnp.sqrt(v_hat) + ADAM_EPS) + ADAM_WD * w_ref[...])
        nm_ref[...] = nm
        nv_ref[...] = nv

    blk = pl.BlockSpec((tr, cols), lambda r: (r, 0))
    return pl.pallas_call(
        body, name=name, grid=(rows // tr,), in_specs=[blk] * 4, out_specs=[blk] * 3,
        out_shape=[jax.ShapeDtypeStruct((rows, cols), F32)] * 3, compiler_params=_params())(w, g, m, v)


def _adamw_nd(w, g, m, v, name):
    shape = w.shape
    flat = lambda t: t.reshape(-1, shape[-1])
    return tuple(t.reshape(shape) for t in _adamw(flat(w), flat(g), flat(m), flat(v), name))


def _pack(parts, rows_mult=64):
    flat = jnp.concatenate([p.reshape(-1).astype(F32) for p in parts])
    per = LANES * rows_mult
    total = -(-flat.shape[0] // per) * per
    return jnp.pad(flat, (0, total - flat.shape[0])).reshape(-1, LANES)


def _unpack(packed, shapes):
    flat = packed.reshape(-1)
    out, pos = [], 0
    for shp in shapes:
        n = int(np.prod(shp))
        out.append(flat[pos:pos + n].reshape(shp))
        pos += n
    return out


def kernel(x, ffn1_w_gate, ffn1_w_up, ffn1_w_down, ffn2_w_gate, ffn2_w_up, ffn2_w_down, w_in, pool_w, pool_scale, conv_w, rpb, w_out, ln_g, ln_b, loss_target, m_ffn1_w_gate, m_ffn1_w_up, m_ffn1_w_down, m_ffn2_w_gate, m_ffn2_w_up, m_ffn2_w_down, m_w_in, m_pool_w, m_pool_scale, m_conv_w, m_rpb, m_w_out, m_ln_g, m_ln_b, v_ffn1_w_gate, v_ffn1_w_up, v_ffn1_w_down, v_ffn2_w_gate, v_ffn2_w_up, v_ffn2_w_down, v_w_in, v_pool_w, v_pool_scale, v_conv_w, v_rpb, v_w_out, v_ln_g, v_ln_b):
    n_l, d, fs = ffn1_w_gate.shape
    s = x.shape[1]
    rows = s // GRID_W
    assert x.shape[0] == 1 and s % Q_TOK == 0 and rows >= K_ROWS and fs % BF16_ROWS == 0
    alpha = (2.0 * n_l) ** 0.25
    xi, yi, ci = _mesh_pos()
    me = 4 * xi + 2 * yi + ci
    core = jnp.reshape(ci, (1,)).astype(jnp.int32)
    ln_w, cw_w = ln_g.shape[2], conv_w.shape[2]

    tr = lambda w: jnp.swapaxes(w, 1, 2)
    ffn1_shard = jnp.stack([tr(ffn1_w_gate), tr(ffn1_w_up), ffn1_w_down], axis=1).astype(BF16)
    ffn2_shard = jnp.stack([tr(ffn2_w_gate), tr(ffn2_w_up), ffn2_w_down], axis=1).astype(BF16)
    win_shard, wout_shard = tr(w_in).astype(BF16), w_out.astype(BF16)
    small_shard = _pack([ln_g, ln_b, conv_w])
    w_ffn1, small = _exchange_alone(_Gather([ffn1_shard[0], small_shard]), "gather_first")
    n_ln = n_l * 3 * ln_w
    small = small.reshape(N_DEV, -1)
    unshard = lambda t, width: jnp.moveaxis(t.reshape(N_DEV, n_l, 3, width), 0, 2).reshape(n_l, 3, N_DEV * width)
    ln_g_all = unshard(small[:, :n_ln], ln_w)
    ln_b_all = unshard(small[:, n_ln:2 * n_ln], ln_w)
    conv_all = unshard(small[:, 2 * n_ln:2 * n_ln + n_l * 3 * cw_w], cw_w)
    pool_bd = jnp.zeros((n_l, D_POOL, D_POOL), F32)
    for g in range(len(POOL_WINDOWS)):
        sl = slice(g * POOL_GROUP, (g + 1) * POOL_GROUP)
        pool_bd = pool_bd.at[:, sl, sl].set(pool_w[:, g])
    pool_bd = pool_bd.astype(BF16)
    lnp = lambda arr, l, j: arr[l, j].reshape(1, d)

    saved = []
    h = x.reshape(s, d)
    for l in range(n_l):
        a1, u1, h1, z1, x1, w_in_l, w_out_l, w_ffn2 = _ffn_fwd(
            h, w_ffn1, lnp(ln_g_all, l, 0), lnp(ln_b_all, l, 0), alpha, f"ffn1_fwd_{l}",
            job=_Gather([win_shard[l], wout_shard[l], ffn2_shard[l]]))
        proj = _win_fwd(x1, w_in_l, f"win_fwd_{l}")
        bias = _na_tiles(rpb[l])
        yab = _local_fwd(proj[0], pool_bd[l], pool_scale[l].reshape(1, D_POOL), conv_all[l], f"local_fwd_{l}")
        yc, lse = _attn_fwd(proj[1], bias, f"attn_fwd_{l}")
        z2, x2 = _wout_fwd(x1, yab, yc, w_out_l, lnp(ln_g_all, l, 1), lnp(ln_b_all, l, 1), alpha, f"wout_fwd_{l}")
        a2, u2, h2, z3, x3, *w_next = _ffn_fwd(
            x2, w_ffn2, lnp(ln_g_all, l, 2), lnp(ln_b_all, l, 2), alpha, f"ffn2_fwd_{l}",
            job=_Gather([ffn1_shard[l + 1]]) if l + 1 < n_l else None)
        saved.append((h, a1, u1, h1, z1, x1, proj, bias, yab, yc, lse, z2, x2, a2, u2, h2, z3, w_ffn1, w_in_l, w_out_l, w_ffn2))
        h = x3
        if w_next:
            w_ffn1 = w_next[0]

    sq, dh = _loss_grad(h, loss_target.reshape(s, d), "loss_head")
    loss = lax.psum(sq[0, 0] * (0.5 / d), MESH_AXES)

    flat = lambda blocks: [b.reshape(N_DEV, -1, d) for b in blocks]
    pair_add = lambda blocks, got, tag: [_pair_add(b, g, core, f"grads_pair_add_{tag}_{i}")
                                         for i, (b, g) in enumerate(zip(blocks, got))]
    small_grads = [None] * n_l
    reduced = [None] * n_l
    above = None
    for l in reversed(range(n_l)):
        x0, a1, u1, h1, z1, x1, proj, bias, yab, yc, lse, z2, x2, a2, u2, h2, z3, w_ffn1, w_in_l, w_out_l, w_ffn2 = saved[l]
        dx2, da, du, df, dg3, db3, *got = _ffn_bwd_dx(
            dh, z3, a2, u2, w_ffn2, lnp(ln_g_all, l, 2), alpha, f"ffn2_bwd_dx_{l}",
            job=_PairExchange(above) if above else None)
        above_pairs = pair_add(above, got, f"mix_{l + 1}") if above else None
        g2 = flat([_ffn_bwd_dwd(h2, df, _ffn_bwd_dwgu(da, du, x2, fs, f"ffn2_bwd_dwgu_{l}"), f"ffn2_bwd_dwd_{l}")])
        dmix, dxp, dg2, db2, g_out, *got = _wout_bwd(dx2, z2, yab, yc, w_out_l, lnp(ln_g_all, l, 1), alpha,
                                                     f"wout_bwd_{l}", job=_PairExchange(g2))
        p2 = pair_add(g2, got, f"ffn2_{l}")
        dq, dk, dv, dtiles, *crossed = _attn_bwd(proj[1], bias, yc, dmix, lse, f"attn_bwd_{l}",
                                                 job=_ChipExchange(above_pairs) if above else None)
        if above:
            reduced[l + 1] += crossed
        dloc, dpw, dsc, dcw = _local_bwd(proj[0], dmix, pool_bd[l], pool_scale[l].reshape(1, D_POOL), conv_all[l],
                                         f"local_bwd_{l}")
        dx1, g_in = _win_bwd(dxp, dloc, dq, dk, dv, x1, w_in_l, f"win_bwd_{l}")
        dx0, da, du, df, dg1, db1, *crossed = _ffn_bwd_dx(dx1, z1, a1, u1, w_ffn1, lnp(ln_g_all, l, 0), alpha,
                                                          f"ffn1_bwd_dx_{l}", job=_ChipExchange(p2))
        reduced[l] = list(crossed)
        g1 = _ffn_bwd_dwd(h1, df, _ffn_bwd_dwgu(da, du, x0, fs, f"ffn1_bwd_dwgu_{l}"), f"ffn1_bwd_dwd_{l}")
        above = flat([g_out, g_in, g1])
        drpb = _rpb_finish(dtiles, f"rpb_finish_{l}")
        dpool = jnp.stack([dpw[g * POOL_GROUP:(g + 1) * POOL_GROUP, g * POOL_GROUP:(g + 1) * POOL_GROUP]
                           for g in range(len(POOL_WINDOWS))])
        small_grads[l] = (jnp.concatenate([dg1, dg2, dg3]), jnp.concatenate([db1, db2, db3]), dcw[0:3], dpool, dsc[0], drpb)
        dh = dx0
    grad_x = dh.reshape(x.shape)

    last_pairs = pair_add(above, _exchange_alone(_PairExchange(above), "grads_pair_exchange_last"), "mix_0")
    reduced[0] += _exchange_alone(_ChipExchange(last_pairs), "grads_chip_exchange_last")
    sums = [[_sum_blocks(q, f"grads_chip_sum_{l}_{i}") for i, q in enumerate(reduced[l])] for l in range(n_l)]
    r_ffn2, r_out, r_in, r_ffn1 = [jnp.stack([sums[l][i] for l in range(n_l)]) for i in range(4)]
    r_ffn1, r_ffn2 = r_ffn1.reshape(n_l, 3, fs, d), r_ffn2.reshape(n_l, 3, fs, d)
    row_grads = {"ffn1_w_gate": r_ffn1[:, 0], "ffn1_w_up": r_ffn1[:, 1], "ffn2_w_gate": r_ffn2[:, 0],
                 "ffn2_w_up": r_ffn2[:, 1], "w_in": r_in}
    grads = {"ffn1_w_down": r_ffn1[:, 2], "ffn2_w_down": r_ffn2[:, 2], "w_out": r_out}
    grads.update({n: tr(g) for n, g in row_grads.items()})

    stack = lambda k: jnp.stack([small_grads[l][k] for l in range(n_l)])
    small_shapes = [(n_l, 3, d), (n_l, 3, d), (n_l, 3, D_CONV), pool_w.shape, pool_scale.shape, rpb.shape]
    (small_all,) = _exchange_alone(_Gather([_pack([stack(k) for k in range(6)])]), "gather_small_grads")
    small_sum = _sum_blocks(small_all, "small_grads_sum")
    g_ln_g, g_ln_b, g_conv, g_pool_w, g_pool_scale, g_rpb = _unpack(small_sum, small_shapes)
    own = lambda t, width: lax.dynamic_slice_in_dim(t, me * width, width, axis=2)
    grads.update({"ln_g": own(g_ln_g, ln_w), "ln_b": own(g_ln_b, ln_w), "conv_w": own(g_conv, cw_w),
                  "pool_w": g_pool_w, "pool_scale": g_pool_scale, "rpb": g_rpb})

    weights = dict(ffn1_w_gate=ffn1_w_gate, ffn1_w_up=ffn1_w_up, ffn1_w_down=ffn1_w_down, ffn2_w_gate=ffn2_w_gate,
                   ffn2_w_up=ffn2_w_up, ffn2_w_down=ffn2_w_down, w_in=w_in, pool_w=pool_w, pool_scale=pool_scale,
                   conv_w=conv_w, rpb=rpb, w_out=w_out, ln_g=ln_g, ln_b=ln_b)
    m_in = dict(ffn1_w_gate=m_ffn1_w_gate, ffn1_w_up=m_ffn1_w_up, ffn1_w_down=m_ffn1_w_down, ffn2_w_gate=m_ffn2_w_gate,
                ffn2_w_up=m_ffn2_w_up, ffn2_w_down=m_ffn2_w_down, w_in=m_w_in, pool_w=m_pool_w, pool_scale=m_pool_scale,
                conv_w=m_conv_w, rpb=m_rpb, w_out=m_w_out, ln_g=m_ln_g, ln_b=m_ln_b)
    v_in = dict(ffn1_w_gate=v_ffn1_w_gate, ffn1_w_up=v_ffn1_w_up, ffn1_w_down=v_ffn1_w_down, ffn2_w_gate=v_ffn2_w_gate,
                ffn2_w_up=v_ffn2_w_up, ffn2_w_down=v_ffn2_w_down, w_in=v_w_in, pool_w=v_pool_w, pool_scale=v_pool_scale,
                conv_w=v_conv_w, rpb=v_rpb, w_out=v_w_out, ln_g=v_ln_g, ln_b=v_ln_b)
    names = list(weights)
    large = ["ffn1_w_gate", "ffn1_w_up", "ffn1_w_down", "ffn2_w_gate", "ffn2_w_up", "ffn2_w_down", "w_in", "w_out"]
    tiny = [n for n in names if n not in large]
    delta, new_m, new_v = {}, {}, {}
    for n in large:
        if n in row_grads:
            out = _adamw_nd(tr(weights[n]), row_grads[n], tr(m_in[n]), tr(v_in[n]), f"adamw_{n}")
            delta[n], new_m[n], new_v[n] = (tr(t) for t in out)
        else:
            delta[n], new_m[n], new_v[n] = _adamw_nd(weights[n], grads[n], m_in[n], v_in[n], f"adamw_{n}")
    packed = [_pack([t[n] for n in tiny]) for t in (weights, grads, m_in, v_in)]
    tiny_out = _adamw(*packed, "adamw_small")
    tiny_shapes = [weights[n].shape for n in tiny]
    for res, t in zip((delta, new_m, new_v), tiny_out):
        res.update(dict(zip(tiny, _unpack(t, tiny_shapes))))

    return (loss, grad_x, *[grads[n] for n in names], *[delta[n] for n in names],
            *[new_m[n] for n in names], *[new_v[n] for n in names])
```

```python
import functools

import numpy as np
import jax
import jax.numpy as jnp
from jax import lax
from jax.experimental import pallas as pl
from jax.experimental.pallas import tpu as pltpu

F32, BF16 = jnp.float32, jnp.bfloat16
MESH = pl.DeviceIdType.MESH
N_DEV = 8
MESH_AXES = ("x", "y", "c")

LN_EPS = 1e-5
NEG_INF = -1e30
D_POOL = 256
POOL_WINDOWS = (2, 4, 8, 16)
POOL_GROUP = 64
D_CONV = 256
NA_HEADS = 8
NA_HEAD_DIM = 64
D_NA = NA_HEADS * NA_HEAD_DIM
GRID_W = 64
NA_ROWS = 8
NA_COLS = 16
D_LOC = D_POOL + 3 * D_CONV
D_MIX = D_POOL + D_CONV + D_NA
ADAM_LR, ADAM_B1, ADAM_B2, ADAM_EPS, ADAM_WD, ADAM_STEP = 0.001, 0.9, 0.999, 1e-08, 0.01, 10

VMEM_LIMIT_BYTES = 56 * 1024 * 1024
LANES = 128
BF16_ROWS = 16
HALO = 16
Q_ROWS = 8
K_ROWS = 16
Q_TOK = Q_ROWS * GRID_W
K_TOK = K_ROWS * GRID_W
K_BLK = 4 * GRID_W
HEAD_PAIR = 2 * NA_HEAD_DIM
FFN_CHUNK_DEVS = 4
FFN_TOKENS = 256
DW_TOKENS = 1024
MIX_TOKENS = 512
LOSS_TOKENS = 1024
PASS_ON_AT = 7 / 8

NT = (((1,), (1,)), ((), ()))
TN = (((0,), (0,)), ((), ()))


def _dot(a, b):
    return jnp.dot(a, b, preferred_element_type=F32)


def _dot_nt(a, b):
    return lax.dot_general(a, b, NT, preferred_element_type=F32)


def _dot_tn(a, b):
    return lax.dot_general(a, b, TN, preferred_element_type=F32)


def _params():
    return pltpu.CompilerParams(vmem_limit_bytes=VMEM_LIMIT_BYTES)


def _row_tile(rows, pref, mult=BF16_ROWS):
    t = min(rows, pref)
    t -= t % mult
    while t > mult and rows % t:
        t -= mult
    assert t > 0 and rows % t == 0, (rows, pref)
    return t


def _mesh_pos():
    return tuple(lax.axis_index(a) for a in MESH_AXES)


def _any_spec():
    return pl.BlockSpec(memory_space=pl.ANY)


class _Gather:
    def __init__(self, shards):
        self.arrays = list(shards)
        n = len(shards)
        self.out_shape = [jax.ShapeDtypeStruct((N_DEV,) + s.shape, s.dtype) for s in shards]
        self.scratch = [pltpu.SemaphoreType.DMA((n, 7)), pltpu.SemaphoreType.DMA((n, 7)), pltpu.SemaphoreType.DMA((n,))]

    def phases(self, ins, outs, sems):
        n = len(ins)
        send_sems, recv_sems, local_sems = sems
        x, y, c = _mesh_pos()
        me, sibling = (x, y, c), (x, y, 1 - c)
        chips = [(1 - x, y), (x, 1 - y), (1 - x, 1 - y)]

        def copy(a, k, block, to, src=None):
            dst = outs[a].at[4 * block[0] + 2 * block[1] + block[2]]
            return pltpu.make_async_remote_copy(
                src_ref=dst if src is None else src, dst_ref=dst,
                send_sem=send_sems.at[a, k], recv_sem=recv_sems.at[a, k],
                device_id=to, device_id_type=MESH)

        def mine():
            return [pltpu.make_async_copy(ins[a], outs[a].at[4 * x + 2 * y + c], local_sems.at[a]) for a in range(n)]

        def first():
            return [cp for a in range(n) for cp in
                    [copy(a, 0, me, sibling, src=ins[a])]
                    + [copy(a, 1 + j, me, (*chip, c), src=ins[a]) for j, chip in enumerate(chips)]]

        def passed():
            return [copy(a, 4 + j, (*chip, c), sibling) for j, chip in enumerate(chips) for a in range(n)]

        def start():
            for cp in mine() + first():
                cp.start()

        def middle():
            for j, chip in enumerate(chips):
                for a in range(n):
                    copy(a, 1 + j, (*chip, c), me).wait_recv()
            for cp in passed():
                cp.start()

        def finish():
            for a in range(n):
                copy(a, 0, sibling, me).wait_recv()
                for j, chip in enumerate(chips):
                    copy(a, 4 + j, (*chip, 1 - c), me).wait_recv()
            for cp in first() + passed():
                cp.wait_send()
            for cp in mine():
                cp.wait()

        return start, middle, finish


class _ChipExchange:
    def __init__(self, parts):
        self.arrays = list(parts)
        n = len(parts)
        self.out_shape = [jax.ShapeDtypeStruct(s.shape, s.dtype) for s in parts]
        self.scratch = [pltpu.SemaphoreType.DMA((n, 3)), pltpu.SemaphoreType.DMA((n, 3)), pltpu.SemaphoreType.DMA((n,))]

    def phases(self, ins, outs, sems):
        n = len(ins)
        send_sems, recv_sems, local_sems = sems
        x, y, c = _mesh_pos()
        my_chip = 2 * x + y
        chips = [(1 - x, y), (x, 1 - y), (1 - x, 1 - y)]

        def own():
            return [pltpu.make_async_copy(ins[a].at[my_chip], outs[a].at[my_chip], local_sems.at[a]) for a in range(n)]

        def copy(a, k, src_chip, dst_chip, to):
            return pltpu.make_async_remote_copy(
                src_ref=ins[a].at[src_chip], dst_ref=outs[a].at[dst_chip],
                send_sem=send_sems.at[a, k], recv_sem=recv_sems.at[a, k],
                device_id=to, device_id_type=MESH)

        def sends():
            return [copy(a, k, 2 * px + py, my_chip, (px, py, c)) for a in range(n) for k, (px, py) in enumerate(chips)]

        def start():
            for cp in own() + sends():
                cp.start()

        def finish():
            for cp in sends():
                cp.wait_send()
            for a in range(n):
                for k, (px, py) in enumerate(chips):
                    copy(a, k, my_chip, 2 * px + py, (px, py, c)).wait_recv()
            for cp in own():
                cp.wait()

        return start, None, finish


def _exchange_alone(job, name):
    n = len(job.arrays)

    def body(*refs):
        for phase in job.phases(refs[:n], refs[n:2 * n], refs[2 * n:]):
            if phase is not None:
                phase()

    return pl.pallas_call(
        body, name=name, out_shape=job.out_shape,
        in_specs=[_any_spec()] * n, out_specs=[_any_spec()] * n, scratch_shapes=job.scratch,
    )(*job.arrays)


def _riding_call(body, job, n_in, n_out, n_steps, step, **kw):
    if job is None:
        return pl.pallas_call(body, **kw)
    n_job, n_sem = len(job.arrays), len(job.scratch)
    kw = dict(kw, in_specs=list(kw["in_specs"]) + [_any_spec()] * n_job,
              out_specs=list(kw["out_specs"]) + [_any_spec()] * n_job,
              out_shape=list(kw["out_shape"]) + job.out_shape,
              scratch_shapes=list(kw.get("scratch_shapes", ())) + job.scratch)

    def riding(*refs):
        ins, job_ins = refs[:n_in], refs[n_in:n_in + n_job]
        outs = refs[n_in + n_job:n_in + n_job + n_out]
        job_outs = refs[n_in + n_job + n_out:n_in + 2 * n_job + n_out]
        scratch = refs[n_in + 2 * n_job + n_out:]
        start, middle, finish = job.phases(job_ins, job_outs, scratch[len(scratch) - n_sem:])
        now = step()
        pl.when(now == 0)(start)
        if middle is not None:
            assert n_steps >= 3
            pl.when(now == int(PASS_ON_AT * n_steps) - 1)(middle)
        body(*ins, *outs, *scratch[:len(scratch) - n_sem])
        pl.when(now == n_steps - 1)(finish)

    call = pl.pallas_call(riding, **kw)
    return lambda *args: call(*args, *job.arrays)


class _PairExchange:
    def __init__(self, slabs):
        self.arrays = list(slabs)
        n = len(slabs)
        self.out_shape = [jax.ShapeDtypeStruct((4,) + s.shape[1:], s.dtype) for s in slabs]
        self.scratch = [pltpu.SemaphoreType.DMA((n, 4)), pltpu.SemaphoreType.DMA((n, 4))]

    def phases(self, ins, outs, sems):
        n = len(ins)
        send_sems, recv_sems = sems
        x, y, c = _mesh_pos()

        def copies():
            return [pltpu.make_async_remote_copy(
                src_ref=ins[a].at[2 * j + 1 - c], dst_ref=outs[a].at[j],
                send_sem=send_sems.at[a, j], recv_sem=recv_sems.at[a, j],
                device_id=(x, y, 1 - c), device_id_type=MESH) for a in range(n) for j in range(4)]

        def start():
            for cp in copies():
                cp.start()

        def finish():
            for cp in copies():
                cp.wait_send()
            for cp in copies():
                cp.wait_recv()

        return start, None, finish


def _pair_add(slab, got, core, name):
    _, rows, d = slab.shape
    tr = _row_tile(rows, 1024)

    def body(core_ref, mine_ref, got_ref, out_ref):
        out_ref[...] = (mine_ref[...].astype(F32) + got_ref[...].astype(F32)).astype(out_ref.dtype)

    grid_spec = pltpu.PrefetchScalarGridSpec(
        num_scalar_prefetch=1, grid=(4, rows // tr),
        in_specs=[pl.BlockSpec((1, tr, d), lambda j, r, core_ref: (2 * j + core_ref[0], r, 0)),
                  pl.BlockSpec((1, tr, d), lambda j, r, core_ref: (j, r, 0))],
        out_specs=pl.BlockSpec((1, tr, d), lambda j, r, core_ref: (j, r, 0)))
    return pl.pallas_call(body, name=name, grid_spec=grid_spec,
                          out_shape=jax.ShapeDtypeStruct((4, rows, d), slab.dtype),
                          compiler_params=_params())(core, slab, got)


def _sum_blocks(parts, name):
    k, rows, d = parts.shape
    tr = _row_tile(rows, 512, BF16_ROWS if parts.dtype == BF16 else 8)

    def body(in_ref, out_ref):
        acc = in_ref[0].astype(F32)
        for j in range(1, k):
            acc = acc + in_ref[j].astype(F32)
        out_ref[...] = acc

    return pl.pallas_call(
        body, name=name, grid=(rows // tr,),
        in_specs=[pl.BlockSpec((k, tr, d), lambda r: (0, r, 0))],
        out_specs=pl.BlockSpec((tr, d), lambda r: (r, 0)),
        out_shape=jax.ShapeDtypeStruct((rows, d), F32), compiler_params=_params())(parts)


def _ln_stats(z):
    mu = jnp.mean(z, axis=-1, keepdims=True)
    zc = z - mu
    var = jnp.mean(zc * zc, axis=-1, keepdims=True)
    rstd = lax.rsqrt(var + LN_EPS)
    return zc * rstd, rstd


def _ln_bwd(dy, zhat, rstd, g):
    dyg = dy * g
    m1 = jnp.mean(dyg, axis=-1, keepdims=True)
    m2 = jnp.mean(dyg * zhat, axis=-1, keepdims=True)
    dz = rstd * (dyg - m1 - zhat * m2)
    return dz, jnp.sum(dy * zhat, axis=0, keepdims=True), jnp.sum(dy, axis=0, keepdims=True)


def _accumulate(ref, value, first):
    @pl.when(first)
    def _():
        ref[...] = value

    @pl.when(jnp.logical_not(first))
    def _():
        ref[...] += value


def _add_matmul(acc_ref, first, matmul):
    @pl.when(first)
    def _():
        acc_ref[...] = jnp.zeros_like(acc_ref)

    acc_ref[...] += matmul()


def _ffn_weight_specs(fs, d):
    def spec(row):
        return pl.BlockSpec((N_DEV, 1, fs, d), lambda i: (0, row, 0, 0), pipeline_mode=pl.Buffered(1))
    return [spec(0), spec(1), spec(2)]


def _ffn_fwd(x, w, ln_g, ln_b, alpha, name, job=None, mixer=None):
    s, d = x.shape
    fs = w.shape[2]
    f = N_DEV * fs
    tm = min(s, FFN_TOKENS)
    n_pre = 5 if mixer else 0

    def body(x_ref, *refs):
        xv = x_ref[...]
        if mixer:
            yab_ref, yc_ref, wo_ref, g0_ref, b0_ref = refs[:n_pre]
            zhat0_ref, rstd0_ref, x_out_ref = refs[len(refs) - 3:]
            mix = jnp.concatenate([yab_ref[...], yc_ref[...]], axis=1).astype(BF16)
            zhat0, rstd0 = _ln_stats(alpha * xv + _dot(mix, wo_ref[...].reshape(D_MIX, d)))
            zhat0_ref[...] = zhat0
            rstd0_ref[...] = rstd0
            xv = zhat0 * g0_ref[...] + b0_ref[...]
            x_out_ref[...] = xv
        wg_ref, wu_ref, wd_ref, g_ref, b_ref, a_ref, u_ref, h_ref, zhat_ref, rstd_ref, y_ref = refs[n_pre:n_pre + 11]
        xb = xv.astype(BF16)
        a = _dot_nt(xb, wg_ref[...].reshape(f, d))
        u = _dot_nt(xb, wu_ref[...].reshape(f, d))
        a_ref[...] = a.astype(BF16)
        u_ref[...] = u.astype(BF16)
        h = ((a * jax.nn.sigmoid(a)) * u).astype(BF16)
        h_ref[...] = h
        z = alpha * xv + 0.5 * _dot(h, wd_ref[...].reshape(f, d))
        zhat, rstd = _ln_stats(z)
        zhat_ref[...] = zhat
        rstd_ref[...] = rstd
        y_ref[...] = zhat * g_ref[...] + b_ref[...]

    row = pl.BlockSpec((tm, d), lambda i: (i, 0))
    col = pl.BlockSpec((tm, 1), lambda i: (i, 0))
    vec = pl.BlockSpec((1, d), lambda i: (0, 0))
    hid = pl.BlockSpec((tm, f), lambda i: (i, 0))
    half = pl.BlockSpec((tm, D_MIX // 2), lambda i: (i, 0))
    ln_out = [row, col, row]
    ln_shape = [jax.ShapeDtypeStruct((s, d), F32), jax.ShapeDtypeStruct((s, 1), F32), jax.ShapeDtypeStruct((s, d), F32)]
    pre_specs = [half, half, _whole(mixer[2]), vec, vec] if mixer else []
    call = _riding_call(
        body, job, 6 + n_pre, 6 + (3 if mixer else 0), s // tm, lambda: pl.program_id(0),
        name=name, grid=(s // tm,),
        in_specs=[row] + pre_specs + _ffn_weight_specs(fs, d) + [vec, vec],
        out_specs=[hid, hid, hid] + ln_out + (ln_out if mixer else []),
        out_shape=[jax.ShapeDtypeStruct((s, f), BF16)] * 3 + ln_shape + (ln_shape if mixer else []),
        compiler_params=_params())
    a, u, h, zhat, rstd, y, *rest = call(x, *(mixer or ()), w, w, w, ln_g, ln_b)
    if mixer:
        rest = [(rest[0], rest[1]), rest[2]] + rest[3:]
    return [a, u, h, (zhat, rstd), y] + rest


def _ffn_bwd_dx(dy, z, a, u, w, ln_g, alpha, name, job=None):
    s, d = dy.shape
    fs = w.shape[2]
    f = N_DEV * fs
    tm = min(s, FFN_TOKENS)

    def body(dy_ref, zhat_ref, rstd_ref, a_ref, u_ref, wg_ref, wu_ref, wd_ref, g_ref,
             dx_ref, da_ref, du_ref, df_ref, dg_ref, db_ref):
        i = pl.program_id(0)
        dz, dg, db = _ln_bwd(dy_ref[...], zhat_ref[...], rstd_ref[...], g_ref[...])
        _accumulate(dg_ref, dg, i == 0)
        _accumulate(db_ref, db, i == 0)
        df = (0.5 * dz).astype(BF16)
        df_ref[...] = df
        av = a_ref[...].astype(F32)
        uv = u_ref[...].astype(F32)
        sg = jax.nn.sigmoid(av)
        dh = _dot_nt(df, wd_ref[...].reshape(f, d))
        du = (dh * (av * sg)).astype(BF16)
        da = (dh * uv * (sg * (1.0 + av * (1.0 - sg)))).astype(BF16)
        da_ref[...] = da
        du_ref[...] = du
        dx_ref[...] = alpha * dz + _dot(da, wg_ref[...].reshape(f, d)) + _dot(du, wu_ref[...].reshape(f, d))

    row = pl.BlockSpec((tm, d), lambda i: (i, 0))
    col = pl.BlockSpec((tm, 1), lambda i: (i, 0))
    vec = pl.BlockSpec((1, d), lambda i: (0, 0))
    hid = pl.BlockSpec((tm, f), lambda i: (i, 0))
    call = _riding_call(
        body, job, 9, 6, s // tm, lambda: pl.program_id(0),
        name=name, grid=(s // tm,),
        in_specs=[row, row, col, hid, hid] + _ffn_weight_specs(fs, d) + [vec],
        out_specs=[row, hid, hid, row, vec, vec],
        out_shape=[jax.ShapeDtypeStruct((s, d), F32)] + [jax.ShapeDtypeStruct((s, f), BF16)] * 2
                  + [jax.ShapeDtypeStruct((s, d), BF16)] + [jax.ShapeDtypeStruct((1, d), F32)] * 2,
        compiler_params=_params())
    return call(dy, *z, a, u, w, w, w, ln_g)


def _ffn_bwd_dwgu(da, du, x, fs, name):
    s, d = x.shape
    tf = FFN_CHUNK_DEVS * fs
    n_c = N_DEV // FFN_CHUNK_DEVS
    tk = min(s, DW_TOKENS)
    n_k = s // tk

    def body(da_ref, du_ref, x_ref, out_ref, accg_s, accu_s):
        k = pl.program_id(1)
        xb = x_ref[...].astype(BF16)
        _add_matmul(accg_s, k == 0, lambda: _dot_tn(da_ref[...], xb))
        _add_matmul(accu_s, k == 0, lambda: _dot_tn(du_ref[...], xb))

        @pl.when(k == n_k - 1)
        def _():
            out_ref[:, 0] = accg_s[...].astype(BF16).reshape(FFN_CHUNK_DEVS, fs, d)
            out_ref[:, 1] = accu_s[...].astype(BF16).reshape(FFN_CHUNK_DEVS, fs, d)

    hid = pl.BlockSpec((tk, tf), lambda c, k: (k, c))
    return pl.pallas_call(
        body, name=name, grid=(n_c, n_k),
        in_specs=[hid, hid, pl.BlockSpec((tk, d), lambda c, k: (k, 0))],
        out_specs=pl.BlockSpec((FFN_CHUNK_DEVS, 2, fs, d), lambda c, k: (c, 0, 0, 0), pipeline_mode=pl.Buffered(1)),
        out_shape=jax.ShapeDtypeStruct((N_DEV, 3, fs, d), BF16),
        scratch_shapes=[pltpu.VMEM((tf, d), F32), pltpu.VMEM((tf, d), F32)],
        compiler_params=_params())(da, du, x)


def _ffn_bwd_dwd(h, df, blocks, name):
    s, d = df.shape
    fs = blocks.shape[2]
    tf = FFN_CHUNK_DEVS * fs
    n_c = N_DEV // FFN_CHUNK_DEVS
    tk = min(s, DW_TOKENS)
    n_k = s // tk

    def body(h_ref, df_ref, blocks_ref, out_ref, acc_s):
        k = pl.program_id(1)
        _add_matmul(acc_s, k == 0, lambda: _dot_tn(h_ref[...], df_ref[...]))

        @pl.when(k == n_k - 1)
        def _():
            out_ref[:, 0] = acc_s[...].astype(BF16).reshape(FFN_CHUNK_DEVS, fs, d)

    return pl.pallas_call(
        body, name=name, grid=(n_c, n_k),
        in_specs=[pl.BlockSpec((tk, tf), lambda c, k: (k, c)), pl.BlockSpec((tk, d), lambda c, k: (k, 0)), _any_spec()],
        out_specs=pl.BlockSpec((FFN_CHUNK_DEVS, 1, fs, d), lambda c, k: (c, 2, 0, 0), pipeline_mode=pl.Buffered(1)),
        out_shape=jax.ShapeDtypeStruct(blocks.shape, BF16), input_output_aliases={2: 0},
        scratch_shapes=[pltpu.VMEM((tf, d), F32)],
        compiler_params=_params())(h, df, blocks)


def _whole(arr):
    return pl.BlockSpec(arr.shape, lambda i: (0,) * arr.ndim, pipeline_mode=pl.Buffered(1))


def _win_fwd(x, w_in, name):
    s, d = x.shape
    d_in = N_DEV * w_in.shape[1]
    tm = min(s, MIX_TOKENS)
    scale = NA_HEAD_DIM ** -0.5
    assert d_in == D_LOC + 3 * D_NA and scale == 0.125

    def body(x_ref, w_ref, loc_ref, qkv_ref):
        proj = _dot_nt(x_ref[...].astype(BF16), w_ref[...].reshape(d_in, d))
        loc_ref[...] = proj[:, :D_LOC]
        qkv_ref[:, :D_NA] = (proj[:, D_LOC:D_LOC + D_NA] * scale).astype(BF16)
        qkv_ref[:, D_NA:] = proj[:, D_LOC + D_NA:].astype(BF16)

    return pl.pallas_call(
        body, name=name, grid=(s // tm,),
        in_specs=[pl.BlockSpec((tm, d), lambda i: (i, 0)), _whole(w_in)],
        out_specs=[pl.BlockSpec((tm, D_LOC), lambda i: (i, 0)), pl.BlockSpec((tm, 3 * D_NA), lambda i: (i, 0))],
        out_shape=[jax.ShapeDtypeStruct((s, D_LOC), F32), jax.ShapeDtypeStruct((s, 3 * D_NA), BF16)],
        compiler_params=_params())(x, w_in)


def _wout_bwd(dy, z, yab, yc, w_out, ln_g, alpha, name, job=None):
    s, d = dy.shape
    rs = w_out.shape[1]
    tm = min(s, MIX_TOKENS)
    n_i = s // tm

    def body(dy_ref, zhat_ref, rstd_ref, yab_ref, yc_ref, w_ref, g_ref, dmix_ref, dxp_ref, dg_ref, db_ref, out_ref, acc_s):
        i = pl.program_id(0)
        dz, dg, db = _ln_bwd(dy_ref[...], zhat_ref[...], rstd_ref[...], g_ref[...])
        _accumulate(dg_ref, dg, i == 0)
        _accumulate(db_ref, db, i == 0)
        dxp_ref[...] = alpha * dz
        dzb = dz.astype(BF16)
        dmix_ref[...] = _dot_nt(dzb, w_ref[...].reshape(D_MIX, d))
        mix = jnp.concatenate([yab_ref[...], yc_ref[...]], axis=1).astype(BF16)
        _add_matmul(acc_s, i == 0, lambda: _dot_tn(mix, dzb))

        @pl.when(i == n_i - 1)
        def _():
            out_ref[...] = acc_s[...].astype(BF16).reshape(N_DEV, rs, d)

    row = pl.BlockSpec((tm, d), lambda i: (i, 0))
    half = pl.BlockSpec((tm, D_MIX // 2), lambda i: (i, 0))
    vec = pl.BlockSpec((1, d), lambda i: (0, 0))
    call = _riding_call(
        body, job, 7, 5, n_i, lambda: pl.program_id(0),
        name=name, grid=(n_i,),
        in_specs=[row, row, pl.BlockSpec((tm, 1), lambda i: (i, 0)), half, half, _whole(w_out), vec],
        out_specs=[pl.BlockSpec((tm, D_MIX), lambda i: (i, 0)), row, vec, vec, _whole(w_out)],
        out_shape=[jax.ShapeDtypeStruct((s, D_MIX), F32), jax.ShapeDtypeStruct((s, d), F32),
                   jax.ShapeDtypeStruct((1, d), F32), jax.ShapeDtypeStruct((1, d), F32),
                   jax.ShapeDtypeStruct(w_out.shape, BF16)],
        scratch_shapes=[pltpu.VMEM((D_MIX, d), F32)],
        compiler_params=_params())
    return call(dy, *z, yab, yc, w_out, ln_g)


def _win_bwd(dxp, dloc, dq, dk, dv, x, w_in, name):
    s, d = x.shape
    rs = w_in.shape[1]
    d_in = N_DEV * rs
    tm = min(s, MIX_TOKENS)
    n_i = s // tm

    def body(dxp_ref, dloc_ref, dq_ref, dk_ref, dv_ref, x_ref, w_ref, dx_ref, out_ref, acc_s):
        i = pl.program_id(0)
        dp = jnp.concatenate([dloc_ref[...], dq_ref[...], dk_ref[...].astype(BF16), dv_ref[...].astype(BF16)], axis=1)
        dx_ref[...] = dxp_ref[...] + _dot(dp, w_ref[...].reshape(d_in, d))
        _add_matmul(acc_s, i == 0, lambda: _dot_tn(dp, x_ref[...].astype(BF16)))

        @pl.when(i == n_i - 1)
        def _():
            out_ref[...] = acc_s[...].astype(BF16).reshape(N_DEV, rs, d)

    row = pl.BlockSpec((tm, d), lambda i: (i, 0))
    na = pl.BlockSpec((tm, D_NA), lambda i: (i, 0))
    return pl.pallas_call(
        body, name=name, grid=(n_i,),
        in_specs=[row, pl.BlockSpec((tm, D_LOC), lambda i: (i, 0)), na, na, na, row, _whole(w_in)],
        out_specs=[row, _whole(w_in)],
        out_shape=[jax.ShapeDtypeStruct((s, d), F32), jax.ShapeDtypeStruct(w_in.shape, BF16)],
        scratch_shapes=[pltpu.VMEM((d_in, d), F32)],
        compiler_params=_params())(dxp, dloc, dq, dk, dv, x, w_in)


def _shift_rows(v, k):
    n = v.shape[0]
    return pltpu.roll(v, k % n, 0)


def _halo_specs(tm, s, width, col):
    per = tm // HALO
    last = s // HALO - 1
    return [pl.BlockSpec((HALO, width), lambda i: (jnp.maximum(i * per - 1, 0), col)),
            pl.BlockSpec((tm, width), lambda i: (i, col)),
            pl.BlockSpec((HALO, width), lambda i: (jnp.minimum((i + 1) * per, last), col))]


def _token_index(i, tm):
    return i * tm - HALO + lax.broadcasted_iota(jnp.int32, (tm + 2 * HALO, 1), 0)


def _pool_lane_tables():
    lane = lax.broadcasted_iota(jnp.int32, (1, D_POOL), 1)
    group = sum((lane >= g * POOL_GROUP).astype(jnp.int32) for g in range(1, len(POOL_WINDOWS)))
    half = jnp.where(group == 0, 1, jnp.where(group == 1, 2, jnp.where(group == 2, 4, 8)))
    return group, half


def _window_sums(v, group, offsets):
    s2 = v + _shift_rows(v, 1)
    s4 = s2 + _shift_rows(s2, 2)
    s8 = s4 + _shift_rows(s4, 4)
    s16 = s8 + _shift_rows(s8, 8)
    parts = [_shift_rows(p, -o) if o else p for p, o in zip((s2, s4, s8, s16), offsets)]
    return jnp.where(group == 0, parts[0], jnp.where(group == 1, parts[1], jnp.where(group == 2, parts[2], parts[3])))


def _pool_counts(tok, half, s):
    return (jnp.minimum(tok + half, s) - jnp.maximum(tok - half, 0)).astype(F32)


def _pool_forward(u, tok, s):
    group, half = _pool_lane_tables()
    sums = _window_sums(u, group, [w // 2 - 1 for w in POOL_WINDOWS])
    return sums / _pool_counts(tok, half, s) - u


def _conv_forward(zc, cw_ref):
    return cw_ref[0:1, :] * _shift_rows(zc, 1) + cw_ref[1:2, :] * zc + cw_ref[2:3, :] * _shift_rows(zc, -1)


def _local_fwd(proj, pool_bd, pool_scale, conv_w, name):
    s = proj.shape[0]
    tm = min(s, MIX_TOKENS)
    ctr = slice(HALO, HALO + tm)

    def body(prev_ref, cur_ref, next_ref, pw_ref, sc_ref, cw_ref, out_ref):
        i = pl.program_id(0)
        ext = jnp.concatenate([prev_ref[...], cur_ref[...], next_ref[...]], axis=0)
        tok = _token_index(i, tm)
        inside = (tok >= 0) & (tok < s)
        u = jnp.where(inside, ext[:, 0:D_POOL], 0.0)
        p = _pool_forward(u, tok, s)[ctr]
        ya = _dot(p.astype(BF16), pw_ref[...]) * sc_ref[...]
        gb = ext[:, D_POOL:D_POOL + D_CONV]
        zc = jnp.where(inside, ext[:, D_POOL + D_CONV:D_POOL + 2 * D_CONV] * ext[:, D_POOL + 2 * D_CONV:D_LOC], 0.0)
        yb = (gb * _conv_forward(zc, cw_ref))[ctr]
        out_ref[...] = jnp.concatenate([ya, yb], axis=1)

    return pl.pallas_call(
        body, name=name, grid=(s // tm,),
        in_specs=_halo_specs(tm, s, D_LOC, 0) + [
            pl.BlockSpec((D_POOL, D_POOL), lambda i: (0, 0)), pl.BlockSpec((1, D_POOL), lambda i: (0, 0)),
            pl.BlockSpec((3, D_CONV), lambda i: (0, 0))],
        out_specs=pl.BlockSpec((tm, D_POOL + D_CONV), lambda i: (i, 0)),
        out_shape=jax.ShapeDtypeStruct((s, D_POOL + D_CONV), F32),
        compiler_params=_params())(proj, proj, proj, pool_bd, pool_scale, conv_w)


def _local_bwd(proj, dmix, pool_bd, pool_scale, conv_w, name):
    s = proj.shape[0]
    tm = min(s, MIX_TOKENS)
    ctr = slice(HALO, HALO + tm)

    def body(prev_ref, cur_ref, next_ref, dprev_ref, dcur_ref, dnext_ref, pw_ref, sc_ref, cw_ref,
             dloc_ref, dpw_ref, dsc_ref, dcw_ref):
        i = pl.program_id(0)
        first = i == 0
        ext = jnp.concatenate([prev_ref[...], cur_ref[...], next_ref[...]], axis=0)
        dext = jnp.concatenate([dprev_ref[...], dcur_ref[...], dnext_ref[...]], axis=0)
        tok = _token_index(i, tm)
        inside = (tok >= 0) & (tok < s)
        group, half = _pool_lane_tables()
        cnt = _pool_counts(tok, half, s)
        u = jnp.where(inside, ext[:, 0:D_POOL], 0.0)
        dya = jnp.where(inside, dext[:, 0:D_POOL], 0.0)
        p_c = _pool_forward(u, tok, s)[ctr].astype(BF16)
        lin = _dot(p_c, pw_ref[...])
        _accumulate(dsc_ref, jnp.sum(dya[ctr] * lin, axis=0, keepdims=True), first)
        e1 = (dya * sc_ref[...]).astype(BF16)
        _accumulate(dpw_ref, _dot_tn(p_c, e1[ctr]), first)
        dp = _dot_nt(e1, pw_ref[...])
        du = _window_sums(dp / cnt, group, [w // 2 for w in POOL_WINDOWS]) - dp
        gb = ext[:, D_POOL:D_POOL + D_CONV]
        gc = ext[:, D_POOL + D_CONV:D_POOL + 2 * D_CONV]
        hv = ext[:, D_POOL + 2 * D_CONV:D_LOC]
        zc = jnp.where(inside, gc * hv, 0.0)
        dyb = jnp.where(inside, dext[:, D_POOL:D_POOL + D_CONV], 0.0)
        dgb = dyb * _conv_forward(zc, cw_ref)
        dyc = dyb * gb
        for k in range(3):
            part = jnp.sum(dyc[ctr] * _shift_rows(zc, 1 - k)[ctr], axis=0, keepdims=True)
            _accumulate(dcw_ref.at[k:k + 1, :], part, first)
        dzc = cw_ref[0:1, :] * _shift_rows(dyc, -1) + cw_ref[1:2, :] * dyc + cw_ref[2:3, :] * _shift_rows(dyc, 1)
        dloc = jnp.concatenate([du, dgb, dzc * hv, dzc * gc], axis=1)
        dloc_ref[...] = dloc[ctr].astype(BF16)

    return pl.pallas_call(
        body, name=name, grid=(s // tm,),
        in_specs=_halo_specs(tm, s, D_LOC, 0) + _halo_specs(tm, s, D_POOL + D_CONV, 0) + [
            pl.BlockSpec((D_POOL, D_POOL), lambda i: (0, 0)), pl.BlockSpec((1, D_POOL), lambda i: (0, 0)),
            pl.BlockSpec((3, D_CONV), lambda i: (0, 0))],
        out_specs=[pl.BlockSpec((tm, D_LOC), lambda i: (i, 0)), pl.BlockSpec((D_POOL, D_POOL), lambda i: (0, 0)),
                   pl.BlockSpec((1, D_POOL), lambda i: (0, 0)), pl.BlockSpec((8, D_CONV), lambda i: (0, 0))],
        out_shape=[jax.ShapeDtypeStruct((s, D_LOC), BF16), jax.ShapeDtypeStruct((D_POOL, D_POOL), F32),
                   jax.ShapeDtypeStruct((1, D_POOL), F32), jax.ShapeDtypeStruct((8, D_CONV), F32)],
        compiler_params=_params())(proj, proj, proj, dmix, dmix, dmix, pool_bd, pool_scale, conv_w)


def _na_geometry(rows):
    n_j = rows // Q_ROWS
    dr = np.full((3, Q_ROWS, K_ROWS), 2 * NA_ROWS - 1, np.int64)
    for t, j in enumerate((0, min(1, n_j - 1), n_j - 1)):
        base = int(np.clip(Q_ROWS * j - NA_ROWS // 2, 0, rows - K_ROWS))
        for qr in range(Q_ROWS):
            r = Q_ROWS * j + qr
            start = int(np.clip(r - NA_ROWS // 2, 0, rows - NA_ROWS))
            for kr in range(K_ROWS):
                if start <= base + kr < start + NA_ROWS:
                    dr[t, qr, kr] = base + kr - r + NA_ROWS - 1
    return dr


def _na_col_tables():
    c = np.arange(GRID_W)
    start = np.clip(c - NA_COLS // 2, 0, GRID_W - NA_COLS)
    valid = (c[None, :] >= start[:, None]) & (c[None, :] < start[:, None] + NA_COLS)
    dc = np.clip(c[None, :] - c[:, None], -(NA_COLS - 1), NA_COLS - 1) + (NA_COLS - 1)
    return valid, dc


NO_ROW = 2 * NA_ROWS - 1
N_SLOT = 2 * NA_ROWS


def _na_tiles(rpb):
    valid, dc = _na_col_tables()
    onehot = jnp.asarray((dc[None] == np.arange(2 * NA_COLS - 1)[:, None, None]).astype(np.float32))
    table = jnp.einsum("hrd,dqk->hrqk", rpb, onehot, precision=lax.Precision.HIGHEST)
    table = jnp.where(jnp.asarray(valid)[None, None], table, NEG_INF)
    outside = jnp.full((NA_HEADS, 1, GRID_W, GRID_W), NEG_INF, F32)
    padded = jnp.concatenate([outside, table, outside], axis=1)
    pairs = jnp.concatenate([padded[:, :N_SLOT], padded[:, 1:]], axis=-1)
    return jnp.concatenate([pairs, jnp.full((NA_HEADS, 1, GRID_W, 2 * GRID_W), NEG_INF, F32)], axis=1)


G_ROWS = 2
N_GRP = Q_ROWS // G_ROWS
G_TOK = G_ROWS * GRID_W
GK_ROWS = NA_ROWS + G_ROWS
GK_TOK = GK_ROWS * GRID_W
STACK_TOK = N_GRP * 2 * G_TOK


def _na_group_tables(rows):
    dr = _na_geometry(rows)
    koff = np.zeros((3, N_GRP), np.int64)
    slot = np.zeros((3, N_GRP, G_ROWS, GK_ROWS // 2), np.int64)
    even_in, odd_in = np.zeros_like(slot), np.zeros_like(slot)
    for t in range(3):
        for g in range(N_GRP):
            qrs = range(G_ROWS * g, G_ROWS * (g + 1))
            inside = [kr for kr in range(K_ROWS) if any(dr[t, qr, kr] != NO_ROW for qr in qrs)]
            lo, hi = min(inside), max(inside) + 1
            off = min(lo - lo % 2, K_ROWS - GK_ROWS)
            assert off <= lo and hi <= off + GK_ROWS
            koff[t, g] = off
            for qq, qr in enumerate(qrs):
                for kp in range(GK_ROWS // 2):
                    even, odd = int(dr[t, qr, off + 2 * kp]), int(dr[t, qr, off + 2 * kp + 1])
                    even_in[t, g, qq, kp], odd_in[t, g, qq, kp] = even != NO_ROW, odd != NO_ROW
                    slot[t, g, qq, kp] = (N_SLOT if even == NO_ROW and odd == NO_ROW
                                          else (even if even != NO_ROW else odd - 1) + 1)
    return koff, slot, even_in, odd_in


def _by_type(block_type, per_type):
    a, b, c = (int(v) for v in per_type)
    if a == b == c:
        return a
    return jnp.where(block_type == 0, a, jnp.where(block_type == 2, c, b))


def _score_rows(g, hh):
    first = (2 * g + hh) * G_TOK
    return slice(first, first + G_TOK)


def _tile_at(g, hh, qq, kp):
    first = _score_rows(g, hh).start + qq * GRID_W
    return slice(first, first + GRID_W), slice(kp * 2 * GRID_W, (kp + 1) * 2 * GRID_W)


def _fill_bias(bias_s, tiles_ref, block_type, tables):
    _, slot, even_in, odd_in = tables
    left = lax.broadcasted_iota(jnp.int32, (1, 2 * GRID_W), 1) < GRID_W
    for hh in range(2):
        for g in range(N_GRP):
            for qq in range(G_ROWS):
                for kp in range(GK_ROWS // 2):
                    tile = tiles_ref[hh, _by_type(block_type, slot[:, g, qq, kp])]
                    tile = jnp.where(left & (_by_type(block_type, even_in[:, g, qq, kp]) == 0), NEG_INF, tile)
                    tile = jnp.where(jnp.logical_not(left) & (_by_type(block_type, odd_in[:, g, qq, kp]) == 0), NEG_INF, tile)
                    rs, cs = _tile_at(g, hh, qq, kp)
                    bias_s[rs, cs] = tile


def _group_offset(block_type, koff, g):
    off = _by_type(block_type, koff[:, g]) * GRID_W
    return off if isinstance(off, int) else pl.multiple_of(off, 2 * GRID_W)


def _na_specs(s, proj_cols):
    n_blk = s // K_BLK
    per = Q_TOK // K_BLK

    def kv_spec(col0, m):
        return pl.BlockSpec((K_BLK, HEAD_PAIR), lambda hp, j: (jnp.clip(per * j - 1, 0, n_blk - 4) + m, col0 + hp))

    q_col, k_col, v_col = (c // HEAD_PAIR for c in proj_cols)
    return ([pl.BlockSpec((Q_TOK, HEAD_PAIR), lambda hp, j: (j, q_col + hp))]
            + [kv_spec(k_col, m) for m in range(4)] + [kv_spec(v_col, m) for m in range(4)])


def _na_block_type(j, n_j):
    return jnp.where(j == 0, 0, jnp.where(j == n_j - 1, 2, 1))


def _head_masks():
    lane = lax.broadcasted_iota(jnp.int32, (1, HEAD_PAIR), 1)
    return [lane < NA_HEAD_DIM, lane >= NA_HEAD_DIM]


def _attn_fwd(qkv, tiles, name):
    s = qkv.shape[0]
    n_j = s // Q_TOK
    tables = _na_group_tables(s // GRID_W)
    koff = tables[0]

    def body(q_ref, k0, k1, k2, k3, v0, v1, v2, v3, tiles_ref, o_ref, lse_ref, bias_s, k_s, vh_s, sc_s, p_s):
        j = pl.program_id(1)
        block_type = _na_block_type(j, n_j)
        pl.when((j == 0) | (j == 1) | (j == n_j - 1))(functools.partial(_fill_bias, bias_s, tiles_ref, block_type, tables))
        masks = _head_masks()
        for m, (kr, vr) in enumerate(zip((k0, k1, k2, k3), (v0, v1, v2, v3))):
            rows = slice(m * K_BLK, (m + 1) * K_BLK)
            k_s[rows, :] = kr[...]
            v = vr[...]
            for hh, mask in enumerate(masks):
                vh_s[hh, rows, :] = jnp.where(mask, v, jnp.zeros_like(v))
        q = q_ref[...]
        qh = [jnp.where(mask, q, jnp.zeros_like(q)) for mask in masks]
        offs = [_group_offset(block_type, koff, g) for g in range(N_GRP)]
        for g in range(N_GRP):
            kg = k_s[pl.ds(offs[g], GK_TOK), :]
            for hh in range(2):
                sc_s[_score_rows(g, hh), :] = _dot_nt(qh[hh][g * G_TOK:(g + 1) * G_TOK], kg)
        sc = sc_s[...] + bias_s[...]
        mx = jnp.max(sc, axis=-1, keepdims=True)
        p = jnp.exp(sc - mx)
        den = jnp.sum(p, axis=-1, keepdims=True)
        p_s[...] = p.astype(BF16)
        inv = 1.0 / den
        lse = mx + jnp.log(den)
        for g in range(N_GRP):
            rows = slice(g * G_TOK, (g + 1) * G_TOK)
            out = jnp.zeros((G_TOK, HEAD_PAIR), F32)
            for hh in range(2):
                sr = _score_rows(g, hh)
                out = out + _dot(p_s[sr, :], vh_s[hh, pl.ds(offs[g], GK_TOK), :]) * inv[sr]
            o_ref[rows, :] = out
            lse_ref[0, rows, :] = jnp.where(masks[0], lse[_score_rows(g, 0)], lse[_score_rows(g, 1)])

    return pl.pallas_call(
        body, name=name, grid=(NA_HEADS // 2, n_j),
        in_specs=_na_specs(s, (0, D_NA, 2 * D_NA)) + [
            pl.BlockSpec((2, N_SLOT + 1, GRID_W, 2 * GRID_W), lambda hp, j: (hp, 0, 0, 0))],
        out_specs=[pl.BlockSpec((Q_TOK, HEAD_PAIR), lambda hp, j: (j, hp)),
                   pl.BlockSpec((1, Q_TOK, HEAD_PAIR), lambda hp, j: (hp, j, 0))],
        out_shape=[jax.ShapeDtypeStruct((s, D_NA), F32), jax.ShapeDtypeStruct((NA_HEADS // 2, s, HEAD_PAIR), F32)],
        scratch_shapes=[pltpu.VMEM((STACK_TOK, GK_TOK), F32), pltpu.VMEM((K_TOK, HEAD_PAIR), BF16),
                        pltpu.VMEM((2, K_TOK, HEAD_PAIR), BF16), pltpu.VMEM((STACK_TOK, GK_TOK), F32),
                        pltpu.VMEM((STACK_TOK, GK_TOK), BF16)],
        compiler_params=_params())(*([qkv] * 9), tiles)


def _add_tiles(dtile_ref, ds_ref, block_type, slot, has_interior):
    def tile(g, hh, qq, kp):
        rs, cs = _tile_at(g, hh, qq, kp)
        return ds_ref[rs, cs].astype(F32)

    def interior():
        for hh in range(2):
            for qq in range(G_ROWS):
                for kp in range(GK_ROWS // 2):
                    assert (slot[1, :, qq, kp] == slot[1, 0, qq, kp]).all()
                    if slot[1, 0, qq, kp] != N_SLOT:
                        dtile_ref[hh, int(slot[1, 0, qq, kp])] += sum(tile(g, hh, qq, kp) for g in range(N_GRP))

    def edge():
        for hh in range(2):
            for g in range(N_GRP):
                for qq in range(G_ROWS):
                    for kp in range(GK_ROWS // 2):
                        first, last = (0 if e == N_SLOT else int(e) for e in slot[[0, 2], g, qq, kp])
                        if (slot[[0, 2], g, qq, kp] != N_SLOT).any():
                            dtile_ref[hh, _by_type(block_type, (first, first, last))] += tile(g, hh, qq, kp)

    if has_interior:
        pl.when(block_type == 1)(interior)
    pl.when(block_type != 1)(edge)


def _attn_bwd(qkv, tiles, o, dmix, lse, name, job=None):
    s = qkv.shape[0]
    n_j = s // Q_TOK
    n_blk = s // K_BLK
    per = Q_TOK // K_BLK
    scale = NA_HEAD_DIM ** -0.5
    do_col = (D_POOL + D_CONV) // HEAD_PAIR
    tables = _na_group_tables(s // GRID_W)
    koff, slot = tables[0], tables[1]

    def body(q_ref, k0, k1, k2, k3, v0, v1, v2, v3, tiles_ref, o_ref, do_ref, lse_ref,
             dq_ref, dk_ref, dv_ref, dtile_ref, bias_s, k_s, kh_s, v_s, s_s, dp_s, pb_s, dsb_s):
        j = pl.program_id(1)

        @pl.when(j == 0)
        def _():
            dk_ref[...] = jnp.zeros_like(dk_ref)
            dv_ref[...] = jnp.zeros_like(dv_ref)
            dtile_ref[...] = jnp.zeros_like(dtile_ref)

        block_type = _na_block_type(j, n_j)
        pl.when((j == 0) | (j == 1) | (j == n_j - 1))(functools.partial(_fill_bias, bias_s, tiles_ref, block_type, tables))
        base = pl.multiple_of(jnp.clip(per * j - 1, 0, n_blk - 4) * K_BLK, K_BLK)
        masks = _head_masks()
        for m, (kr, vr) in enumerate(zip((k0, k1, k2, k3), (v0, v1, v2, v3))):
            rows = slice(m * K_BLK, (m + 1) * K_BLK)
            k = kr[...]
            k_s[rows, :] = k
            v_s[rows, :] = vr[...]
            for hh, mask in enumerate(masks):
                kh_s[hh, rows, :] = jnp.where(mask, k, jnp.zeros_like(k))
        q = q_ref[...]
        qh = [jnp.where(mask, q, jnp.zeros_like(q)) for mask in masks]
        lane = lax.broadcasted_iota(jnp.int32, (1, HEAD_PAIR), 1)
        offs = [_group_offset(block_type, koff, g) for g in range(N_GRP)]
        do, ov, lse = do_ref[...], o_ref[...], lse_ref[0]
        dob, lse_col, delta_col = {}, [], []
        for g in range(N_GRP):
            rows = slice(g * G_TOK, (g + 1) * G_TOK)
            kg = k_s[pl.ds(offs[g], GK_TOK), :]
            vg = v_s[pl.ds(offs[g], GK_TOK), :]
            for hh, mask in enumerate(masks):
                doh = jnp.where(mask, do[rows], 0.0)
                dob[g, hh] = doh.astype(BF16)
                lse_col.append(jnp.sum(jnp.where(lane == hh * NA_HEAD_DIM, lse[rows], 0.0), axis=-1, keepdims=True))
                delta_col.append(jnp.sum(doh * ov[rows], axis=-1, keepdims=True))
                s_s[_score_rows(g, hh), :] = _dot_nt(qh[hh][rows], kg)
                dp_s[_score_rows(g, hh), :] = _dot_nt(dob[g, hh], vg)
        p = jnp.exp(s_s[...] + bias_s[...] - jnp.concatenate(lse_col, axis=0))
        ds = p * (dp_s[...] - jnp.concatenate(delta_col, axis=0))
        pb_s[...] = p.astype(BF16)
        dsb_s[...] = ds.astype(BF16)
        _add_tiles(dtile_ref, dsb_s, block_type, slot, n_j > 2)
        for g in range(N_GRP):
            rows = slice(g * G_TOK, (g + 1) * G_TOK)
            dq = jnp.zeros((G_TOK, HEAD_PAIR), F32)
            dk = jnp.zeros((GK_TOK, HEAD_PAIR), F32)
            dv = jnp.zeros((GK_TOK, HEAD_PAIR), F32)
            for hh in range(2):
                sr = _score_rows(g, hh)
                dsb = dsb_s[sr, :]
                dq = dq + _dot(dsb, kh_s[hh, pl.ds(offs[g], GK_TOK), :])
                dk = dk + _dot_tn(dsb, qh[hh][rows])
                dv = dv + _dot_tn(pb_s[sr, :], dob[g, hh])
            dq_ref[rows, :] = (dq * scale).astype(BF16)
            at = pl.multiple_of(base + offs[g], 2 * GRID_W)
            dk_ref[pl.ds(at, GK_TOK), :] += dk
            dv_ref[pl.ds(at, GK_TOK), :] += dv

    pair = pl.BlockSpec((Q_TOK, HEAD_PAIR), lambda hp, j: (j, hp))
    whole = pl.BlockSpec((s, HEAD_PAIR), lambda hp, j: (0, hp))
    call = _riding_call(
        body, job, 13, 4, (NA_HEADS // 2) * n_j, lambda: pl.program_id(0) * n_j + pl.program_id(1),
        name=name, grid=(NA_HEADS // 2, n_j),
        in_specs=_na_specs(s, (0, D_NA, 2 * D_NA)) + [
            pl.BlockSpec((2, N_SLOT + 1, GRID_W, 2 * GRID_W), lambda hp, j: (hp, 0, 0, 0)),
            pair, pl.BlockSpec((Q_TOK, HEAD_PAIR), lambda hp, j: (j, do_col + hp)),
            pl.BlockSpec((1, Q_TOK, HEAD_PAIR), lambda hp, j: (hp, j, 0))],
        out_specs=[pair, whole, whole, pl.BlockSpec((2, N_SLOT, GRID_W, 2 * GRID_W), lambda hp, j: (hp, 0, 0, 0))],
        out_shape=[jax.ShapeDtypeStruct((s, D_NA), BF16), jax.ShapeDtypeStruct((s, D_NA), F32),
                   jax.ShapeDtypeStruct((s, D_NA), F32),
                   jax.ShapeDtypeStruct((NA_HEADS, N_SLOT, GRID_W, 2 * GRID_W), F32)],
        scratch_shapes=[pltpu.VMEM((STACK_TOK, GK_TOK), F32), pltpu.VMEM((K_TOK, HEAD_PAIR), BF16),
                        pltpu.VMEM((2, K_TOK, HEAD_PAIR), BF16), pltpu.VMEM((K_TOK, HEAD_PAIR), BF16),
                        pltpu.VMEM((STACK_TOK, GK_TOK), F32), pltpu.VMEM((STACK_TOK, GK_TOK), F32),
                        pltpu.VMEM((STACK_TOK, GK_TOK), BF16), pltpu.VMEM((STACK_TOK, GK_TOK), BF16)],
        compiler_params=_params())
    return call(*([qkv] * 9), tiles, o, dmix, lse)


def _rpb_finish(tiles, name):
    valid, dc = _na_col_tables()
    n_dc = 2 * NA_COLS - 1
    sel = np.zeros((GRID_W, 2 * GRID_W, LANES), np.float32)
    for qc in range(GRID_W):
        for kc in range(GRID_W):
            if valid[qc, kc]:
                sel[qc, kc, dc[qc, kc]] = 1.0
                sel[qc, GRID_W + kc, LANES // 2 + dc[qc, kc]] = 1.0
    sel = jnp.asarray(sel.reshape(GRID_W * 2 * GRID_W, LANES))
    flat = tiles.reshape(NA_HEADS * 2 * NA_ROWS, GRID_W * 2 * GRID_W)

    def body(a_ref, b_ref, out_ref):
        out_ref[...] = jnp.dot(a_ref[...], b_ref[...], preferred_element_type=F32, precision=lax.Precision.HIGHEST)

    sums = pl.pallas_call(
        body, name=name, out_shape=jax.ShapeDtypeStruct((flat.shape[0], LANES), F32),
        compiler_params=_params())(flat, sel).reshape(NA_HEADS, 2 * NA_ROWS, LANES)
    return sums[:, 1:, :n_dc] + sums[:, :2 * NA_ROWS - 1, LANES // 2:LANES // 2 + n_dc]


def _loss_grad(y, target, name):
    s, d = y.shape
    tm = min(s, LOSS_TOKENS)

    def body(y_ref, t_ref, sum_ref, dy_ref):
        diff = y_ref[...] - t_ref[...]
        dy_ref[...] = diff * (1.0 / d)
        part = jnp.zeros((8, LANES), F32) + jnp.sum(diff * diff)
        _accumulate(sum_ref, part, pl.program_id(0) == 0)

    row = pl.BlockSpec((tm, d), lambda i: (i, 0))
    return pl.pallas_call(
        body, name=name, grid=(s // tm,), in_specs=[row, row],
        out_specs=[pl.BlockSpec((8, LANES), lambda i: (0, 0)), row],
        out_shape=[jax.ShapeDtypeStruct((8, LANES), F32), jax.ShapeDtypeStruct((s, d), F32)],
        compiler_params=_params())(y, target)


def _adamw(w, g, m, v, name):
    rows, cols = w.shape
    tr = _row_tile(rows, 512, 8)

    def body(w_ref, g_ref, m_ref, v_ref, d_ref, nm_ref, nv_ref):
        gv = g_ref[...]
        nm = ADAM_B1 * m_ref[...] + (1.0 - ADAM_B1) * gv
        nv = ADAM_B2 * v_ref[...] + (1.0 - ADAM_B2) * (gv * gv)
        m_hat = nm / (1.0 - ADAM_B1 ** ADAM_STEP)
        v_hat = nv / (1.0 - ADAM_B2 ** ADAM_STEP)
        d_ref[...] = -ADAM_LR * (m_hat / (jnp.sqrt(v_hat) + ADAM_EPS) + ADAM_WD * w_ref[...])
        nm_ref[...] = nm
        nv_ref[...] = nv

    blk = pl.BlockSpec((tr, cols), lambda r: (r, 0))
    return pl.pallas_call(
        body, name=name, grid=(rows // tr,), in_specs=[blk] * 4, out_specs=[blk] * 3,
        out_shape=[jax.ShapeDtypeStruct((rows, cols), F32)] * 3, compiler_params=_params())(w, g, m, v)


def _adamw_nd(w, g, m, v, name):
    shape = w.shape
    flat = lambda t: t.reshape(-1, shape[-1])
    return tuple(t.reshape(shape) for t in _adamw(flat(w), flat(g), flat(m), flat(v), name))


def _pack(parts, rows_mult=64):
    flat = jnp.concatenate([p.reshape(-1).astype(F32) for p in parts])
    per = LANES * rows_mult
    total = -(-flat.shape[0] // per) * per
    return jnp.pad(flat, (0, total - flat.shape[0])).reshape(-1, LANES)


def _unpack(packed, shapes):
    flat = packed.reshape(-1)
    out, pos = [], 0
    for shp in shapes:
        n = int(np.prod(shp))
        out.append(flat[pos:pos + n].reshape(shp))
        pos += n
    return out


def kernel(x, ffn1_w_gate, ffn1_w_up, ffn1_w_down, ffn2_w_gate, ffn2_w_up, ffn2_w_down, w_in, pool_w, pool_scale, conv_w, rpb, w_out, ln_g, ln_b, loss_target, m_ffn1_w_gate, m_ffn1_w_up, m_ffn1_w_down, m_ffn2_w_gate, m_ffn2_w_up, m_ffn2_w_down, m_w_in, m_pool_w, m_pool_scale, m_conv_w, m_rpb, m_w_out, m_ln_g, m_ln_b, v_ffn1_w_gate, v_ffn1_w_up, v_ffn1_w_down, v_ffn2_w_gate, v_ffn2_w_up, v_ffn2_w_down, v_w_in, v_pool_w, v_pool_scale, v_conv_w, v_rpb, v_w_out, v_ln_g, v_ln_b):
    n_l, d, fs = ffn1_w_gate.shape
    s = x.shape[1]
    rows = s // GRID_W
    assert x.shape[0] == 1 and s % Q_TOK == 0 and rows >= K_ROWS and fs % BF16_ROWS == 0
    alpha = (2.0 * n_l) ** 0.25
    xi, yi, ci = _mesh_pos()
    me = 4 * xi + 2 * yi + ci
    core = jnp.reshape(ci, (1,)).astype(jnp.int32)
    ln_w, cw_w = ln_g.shape[2], conv_w.shape[2]

    tr = lambda w: jnp.swapaxes(w, 1, 2)
    ffn1_shard = jnp.stack([tr(ffn1_w_gate), tr(ffn1_w_up), ffn1_w_down], axis=1).astype(BF16)
    ffn2_shard = jnp.stack([tr(ffn2_w_gate), tr(ffn2_w_up), ffn2_w_down], axis=1).astype(BF16)
    win_shard, wout_shard = tr(w_in).astype(BF16), w_out.astype(BF16)
    small_shard = _pack([ln_g, ln_b, conv_w])
    w_ffn1, small = _exchange_alone(_Gather([ffn1_shard[0], small_shard]), "gather_first")
    n_ln = n_l * 3 * ln_w
    small = small.reshape(N_DEV, -1)
    unshard = lambda t, width: jnp.moveaxis(t.reshape(N_DEV, n_l, 3, width), 0, 2).reshape(n_l, 3, N_DEV * width)
    ln_g_all = unshard(small[:, :n_ln], ln_w)
    ln_b_all = unshard(small[:, n_ln:2 * n_ln], ln_w)
    conv_all = unshard(small[:, 2 * n_ln:2 * n_ln + n_l * 3 * cw_w], cw_w)
    pool_bd = jnp.zeros((n_l, D_POOL, D_POOL), F32)
    for g in range(len(POOL_WINDOWS)):
        sl = slice(g * POOL_GROUP, (g + 1) * POOL_GROUP)
        pool_bd = pool_bd.at[:, sl, sl].set(pool_w[:, g])
    pool_bd = pool_bd.astype(BF16)
    lnp = lambda arr, l, j: arr[l, j].reshape(1, d)

    saved = []
    h = x.reshape(s, d)
    for l in range(n_l):
        a1, u1, h1, z1, x1, w_in_l, w_out_l, w_ffn2 = _ffn_fwd(
            h, w_ffn1, lnp(ln_g_all, l, 0), lnp(ln_b_all, l, 0), alpha, f"ffn1_fwd_{l}",
            job=_Gather([win_shard[l], wout_shard[l], ffn2_shard[l]]))
        proj = _win_fwd(x1, w_in_l, f"win_fwd_{l}")
        bias = _na_tiles(rpb[l])
        yab = _local_fwd(proj[0], pool_bd[l], pool_scale[l].reshape(1, D_POOL), conv_all[l], f"local_fwd_{l}")
        yc, lse = _attn_fwd(proj[1], bias, f"attn_fwd_{l}")
        a2, u2, h2, z3, x3, z2, x2, *w_next = _ffn_fwd(
            x1, w_ffn2, lnp(ln_g_all, l, 2), lnp(ln_b_all, l, 2), alpha, f"ffn2_fwd_{l}",
            job=_Gather([ffn1_shard[l + 1]]) if l + 1 < n_l else None,
            mixer=(yab, yc, w_out_l, lnp(ln_g_all, l, 1), lnp(ln_b_all, l, 1)))
        saved.append((h, a1, u1, h1, z1, x1, proj, bias, yab, yc, lse, z2, x2, a2, u2, h2, z3, w_ffn1, w_in_l, w_out_l, w_ffn2))
        h = x3
        if w_next:
            w_ffn1 = w_next[0]

    sq, dh = _loss_grad(h, loss_target.reshape(s, d), "loss_head")
    loss = lax.psum(sq[0, 0] * (0.5 / d), MESH_AXES)

    flat = lambda blocks: [b.reshape(N_DEV, -1, d) for b in blocks]
    pair_add = lambda blocks, got, tag: [_pair_add(b, g, core, f"grads_pair_add_{tag}_{i}")
                                         for i, (b, g) in enumerate(zip(blocks, got))]
    small_grads = [None] * n_l
    reduced = [None] * n_l
    above = None
    for l in reversed(range(n_l)):
        x0, a1, u1, h1, z1, x1, proj, bias, yab, yc, lse, z2, x2, a2, u2, h2, z3, w_ffn1, w_in_l, w_out_l, w_ffn2 = saved[l]
        dx2, da, du, df, dg3, db3, *got = _ffn_bwd_dx(
            dh, z3, a2, u2, w_ffn2, lnp(ln_g_all, l, 2), alpha, f"ffn2_bwd_dx_{l}",
            job=_PairExchange(above) if above else None)
        above_pairs = pair_add(above, got, f"mix_{l + 1}") if above else None
        g2 = flat([_ffn_bwd_dwd(h2, df, _ffn_bwd_dwgu(da, du, x2, fs, f"ffn2_bwd_dwgu_{l}"), f"ffn2_bwd_dwd_{l}")])
        dmix, dxp, dg2, db2, g_out, *got = _wout_bwd(dx2, z2, yab, yc, w_out_l, lnp(ln_g_all, l, 1), alpha,
                                                     f"wout_bwd_{l}", job=_PairExchange(g2))
        p2 = pair_add(g2, got, f"ffn2_{l}")
        dq, dk, dv, dtiles, *crossed = _attn_bwd(proj[1], bias, yc, dmix, lse, f"attn_bwd_{l}",
                                                 job=_ChipExchange(above_pairs) if above else None)
        if above:
            reduced[l + 1] += crossed
        dloc, dpw, dsc, dcw = _local_bwd(proj[0], dmix, pool_bd[l], pool_scale[l].reshape(1, D_POOL), conv_all[l],
                                         f"local_bwd_{l}")
        dx1, g_in = _win_bwd(dxp, dloc, dq, dk, dv, x1, w_in_l, f"win_bwd_{l}")
        dx0, da, du, df, dg1, db1, *crossed = _ffn_bwd_dx(dx1, z1, a1, u1, w_ffn1, lnp(ln_g_all, l, 0), alpha,
                                                          f"ffn1_bwd_dx_{l}", job=_ChipExchange(p2))
        reduced[l] = list(crossed)
        g1 = _ffn_bwd_dwd(h1, df, _ffn_bwd_dwgu(da, du, x0, fs, f"ffn1_bwd_dwgu_{l}"), f"ffn1_bwd_dwd_{l}")
        above = flat([g_out, g_in, g1])
        drpb = _rpb_finish(dtiles, f"rpb_finish_{l}")
        dpool = jnp.stack([dpw[g * POOL_GROUP:(g + 1) * POOL_GROUP, g * POOL_GROUP:(g + 1) * POOL_GROUP]
                           for g in range(len(POOL_WINDOWS))])
        small_grads[l] = (jnp.concatenate([dg1, dg2, dg3]), jnp.concatenate([db1, db2, db3]), dcw[0:3], dpool, dsc[0], drpb)
        dh = dx0
    grad_x = dh.reshape(x.shape)

    last_pairs = pair_add(above, _exchange_alone(_PairExchange(above), "grads_pair_exchange_last"), "mix_0")
    reduced[0] += _exchange_alone(_ChipExchange(last_pairs), "grads_chip_exchange_last")
    sums = [[_sum_blocks(q, f"grads_chip_sum_{l}_{i}") for i, q in enumerate(reduced[l])] for l in range(n_l)]
    r_ffn2, r_out, r_in, r_ffn1 = [jnp.stack([sums[l][i] for l in range(n_l)]) for i in range(4)]
    r_ffn1, r_ffn2 = r_ffn1.reshape(n_l, 3, fs, d), r_ffn2.reshape(n_l, 3, fs, d)
    row_grads = {"ffn1_w_gate": r_ffn1[:, 0], "ffn1_w_up": r_ffn1[:, 1], "ffn2_w_gate": r_ffn2[:, 0],
                 "ffn2_w_up": r_ffn2[:, 1], "w_in": r_in}
    grads = {"ffn1_w_down": r_ffn1[:, 2], "ffn2_w_down": r_ffn2[:, 2], "w_out": r_out}
    grads.update({n: tr(g) for n, g in row_grads.items()})

    stack = lambda k: jnp.stack([small_grads[l][k] for l in range(n_l)])
    small_shapes = [(n_l, 3, d), (n_l, 3, d), (n_l, 3, D_CONV), pool_w.shape, pool_scale.shape, rpb.shape]
    (small_all,) = _exchange_alone(_Gather([_pack([stack(k) for k in range(6)])]), "gather_small_grads")
    small_sum = _sum_blocks(small_all, "small_grads_sum")
    g_ln_g, g_ln_b, g_conv, g_pool_w, g_pool_scale, g_rpb = _unpack(small_sum, small_shapes)
    own = lambda t, width: lax.dynamic_slice_in_dim(t, me * width, width, axis=2)
    grads.update({"ln_g": own(g_ln_g, ln_w), "ln_b": own(g_ln_b, ln_w), "conv_w": own(g_conv, cw_w),
                  "pool_w": g_pool_w, "pool_scale": g_pool_scale, "rpb": g_rpb})

    weights = dict(ffn1_w_gate=ffn1_w_gate, ffn1_w_up=ffn1_w_up, ffn1_w_down=ffn1_w_down, ffn2_w_gate=ffn2_w_gate,
                   ffn2_w_up=ffn2_w_up, ffn2_w_down=ffn2_w_down, w_in=w_in, pool_w=pool_w, pool_scale=pool_scale,
                   conv_w=conv_w, rpb=rpb, w_out=w_out, ln_g=ln_g, ln_b=ln_b)
    m_in = dict(ffn1_w_gate=m_ffn1_w_gate, ffn1_w_up=m_ffn1_w_up, ffn1_w_down=m_ffn1_w_down, ffn2_w_gate=m_ffn2_w_gate,
                ffn2_w_up=m_ffn2_w_up, ffn2_w_down=m_ffn2_w_down, w_in=m_w_in, pool_w=m_pool_w, pool_scale=m_pool_scale,
                conv_w=m_conv_w, rpb=m_rpb, w_out=m_w_out, ln_g=m_ln_g, ln_b=m_ln_b)
    v_in = dict(ffn1_w_gate=v_ffn1_w_gate, ffn1_w_up=v_ffn1_w_up, ffn1_w_down=v_ffn1_w_down, ffn2_w_gate=v_ffn2_w_gate,
                ffn2_w_up=v_ffn2_w_up, ffn2_w_down=v_ffn2_w_down, w_in=v_w_in, pool_w=v_pool_w, pool_scale=v_pool_scale,
                conv_w=v_conv_w, rpb=v_rpb, w_out=v_w_out, ln_g=v_ln_g, ln_b=v_ln_b)
    names = list(weights)
    large = ["ffn1_w_gate", "ffn1_w_up", "ffn1_w_down", "ffn2_w_gate", "ffn2_w_up", "ffn2_w_down", "w_in", "w_out"]
    tiny = [n for n in names if n not in large]
    delta, new_m, new_v = {}, {}, {}
    for n in large:
        if n in row_grads:
            out = _adamw_nd(tr(weights[n]), row_grads[n], tr(m_in[n]), tr(v_in[n]), f"adamw_{n}")
            delta[n], new_m[n], new_v[n] = (tr(t) for t in out)
        else:
            delta[n], new_m[n], new_v[n] = _adamw_nd(weights[n], grads[n], m_in[n], v_in[n], f"adamw_{n}")
    packed = [_pack([t[n] for n in tiny]) for t in (weights, grads, m_in, v_in)]
    tiny_out = _adamw(*packed, "adamw_small")
    tiny_shapes = [weights[n].shape for n in tiny]
    for res, t in zip((delta, new_m, new_v), tiny_out):
        res.update(dict(zip(tiny, _unpack(t, tiny_shapes))))

    return (loss, grad_x, *[grads[n] for n in names], *[delta[n] for n in names],
            *[new_m[n] for n in names], *[new_v[n] for n in names])
```

```python
import functools

import numpy as np
import jax
import jax.numpy as jnp
from jax import lax
from jax.experimental import pallas as pl
from jax.experimental.pallas import tpu as pltpu

F32, BF16 = jnp.float32, jnp.bfloat16
MESH = pl.DeviceIdType.MESH
N_DEV = 8
MESH_AXES = ("x", "y", "c")

LN_EPS = 1e-5
NEG_INF = -1e30
D_POOL = 256
POOL_WINDOWS = (2, 4, 8, 16)
POOL_GROUP = 64
D_CONV = 256
NA_HEADS = 8
NA_HEAD_DIM = 64
D_NA = NA_HEADS * NA_HEAD_DIM
GRID_W = 64
NA_ROWS = 8
NA_COLS = 16
D_LOC = D_POOL + 3 * D_CONV
D_MIX = D_POOL + D_CONV + D_NA
ADAM_LR, ADAM_B1, ADAM_B2, ADAM_EPS, ADAM_WD, ADAM_STEP = 0.001, 0.9, 0.999, 1e-08, 0.01, 10

VMEM_LIMIT_BYTES = 56 * 1024 * 1024
LANES = 128
BF16_ROWS = 16
HALO = 16
Q_ROWS = 8
K_ROWS = 16
Q_TOK = Q_ROWS * GRID_W
K_TOK = K_ROWS * GRID_W
K_BLK = 4 * GRID_W
HEAD_PAIR = 2 * NA_HEAD_DIM
FFN_CHUNK_DEVS = 4
FFN_TOKENS = 256
DW_TOKENS = 1024
MIX_TOKENS = 512
LOSS_TOKENS = 1024
PASS_ON_AT = 7 / 8

NT = (((1,), (1,)), ((), ()))
TN = (((0,), (0,)), ((), ()))


def _dot(a, b):
    return jnp.dot(a, b, preferred_element_type=F32)


def _dot_nt(a, b):
    return lax.dot_general(a, b, NT, preferred_element_type=F32)


def _dot_tn(a, b):
    return lax.dot_general(a, b, TN, preferred_element_type=F32)


def _params():
    return pltpu.CompilerParams(vmem_limit_bytes=VMEM_LIMIT_BYTES)


def _row_tile(rows, pref, mult=BF16_ROWS):
    t = min(rows, pref)
    t -= t % mult
    while t > mult and rows % t:
        t -= mult
    assert t > 0 and rows % t == 0, (rows, pref)
    return t


def _mesh_pos():
    return tuple(lax.axis_index(a) for a in MESH_AXES)


def _any_spec():
    return pl.BlockSpec(memory_space=pl.ANY)


class _Gather:
    def __init__(self, shards):
        self.arrays = list(shards)
        n = len(shards)
        self.out_shape = [jax.ShapeDtypeStruct((N_DEV,) + s.shape, s.dtype) for s in shards]
        self.scratch = [pltpu.SemaphoreType.DMA((n, 7)), pltpu.SemaphoreType.DMA((n, 7)), pltpu.SemaphoreType.DMA((n,))]

    def phases(self, ins, outs, sems):
        n = len(ins)
        send_sems, recv_sems, local_sems = sems
        x, y, c = _mesh_pos()
        me, sibling = (x, y, c), (x, y, 1 - c)
        chips = [(1 - x, y), (x, 1 - y), (1 - x, 1 - y)]

        def copy(a, k, block, to, src=None):
            dst = outs[a].at[4 * block[0] + 2 * block[1] + block[2]]
            return pltpu.make_async_remote_copy(
                src_ref=dst if src is None else src, dst_ref=dst,
                send_sem=send_sems.at[a, k], recv_sem=recv_sems.at[a, k],
                device_id=to, device_id_type=MESH)

        def mine():
            return [pltpu.make_async_copy(ins[a], outs[a].at[4 * x + 2 * y + c], local_sems.at[a]) for a in range(n)]

        def first():
            return [cp for a in range(n) for cp in
                    [copy(a, 0, me, sibling, src=ins[a])]
                    + [copy(a, 1 + j, me, (*chip, c), src=ins[a]) for j, chip in enumerate(chips)]]

        def passed():
            return [copy(a, 4 + j, (*chip, c), sibling) for j, chip in enumerate(chips) for a in range(n)]

        def start():
            for cp in mine() + first():
                cp.start()

        def middle():
            for j, chip in enumerate(chips):
                for a in range(n):
                    copy(a, 1 + j, (*chip, c), me).wait_recv()
            for cp in passed():
                cp.start()

        def finish():
            for a in range(n):
                copy(a, 0, sibling, me).wait_recv()
                for j, chip in enumerate(chips):
                    copy(a, 4 + j, (*chip, 1 - c), me).wait_recv()
            for cp in first() + passed():
                cp.wait_send()
            for cp in mine():
                cp.wait()

        return start, middle, finish


class _ChipExchange:
    def __init__(self, parts):
        self.arrays = list(parts)
        n = len(parts)
        self.out_shape = [jax.ShapeDtypeStruct(s.shape, s.dtype) for s in parts]
        self.scratch = [pltpu.SemaphoreType.DMA((n, 3)), pltpu.SemaphoreType.DMA((n, 3)), pltpu.SemaphoreType.DMA((n,))]

    def phases(self, ins, outs, sems):
        n = len(ins)
        send_sems, recv_sems, local_sems = sems
        x, y, c = _mesh_pos()
        my_chip = 2 * x + y
        chips = [(1 - x, y), (x, 1 - y), (1 - x, 1 - y)]

        def own():
            return [pltpu.make_async_copy(ins[a].at[my_chip], outs[a].at[my_chip], local_sems.at[a]) for a in range(n)]

        def copy(a, k, src_chip, dst_chip, to):
            return pltpu.make_async_remote_copy(
                src_ref=ins[a].at[src_chip], dst_ref=outs[a].at[dst_chip],
                send_sem=send_sems.at[a, k], recv_sem=recv_sems.at[a, k],
                device_id=to, device_id_type=MESH)

        def sends():
            return [copy(a, k, 2 * px + py, my_chip, (px, py, c)) for a in range(n) for k, (px, py) in enumerate(chips)]

        def start():
            for cp in own() + sends():
                cp.start()

        def finish():
            for cp in sends():
                cp.wait_send()
            for a in range(n):
                for k, (px, py) in enumerate(chips):
                    copy(a, k, my_chip, 2 * px + py, (px, py, c)).wait_recv()
            for cp in own():
                cp.wait()

        return start, None, finish


def _exchange_alone(job, name):
    n = len(job.arrays)

    def body(*refs):
        for phase in job.phases(refs[:n], refs[n:2 * n], refs[2 * n:]):
            if phase is not None:
                phase()

    return pl.pallas_call(
        body, name=name, out_shape=job.out_shape,
        in_specs=[_any_spec()] * n, out_specs=[_any_spec()] * n, scratch_shapes=job.scratch,
    )(*job.arrays)


def _riding_call(body, job, n_in, n_out, n_steps, step, **kw):
    if job is None:
        return pl.pallas_call(body, **kw)
    n_job, n_sem = len(job.arrays), len(job.scratch)
    kw = dict(kw, in_specs=list(kw["in_specs"]) + [_any_spec()] * n_job,
              out_specs=list(kw["out_specs"]) + [_any_spec()] * n_job,
              out_shape=list(kw["out_shape"]) + job.out_shape,
              scratch_shapes=list(kw.get("scratch_shapes", ())) + job.scratch)

    def riding(*refs):
        ins, job_ins = refs[:n_in], refs[n_in:n_in + n_job]
        outs = refs[n_in + n_job:n_in + n_job + n_out]
        job_outs = refs[n_in + n_job + n_out:n_in + 2 * n_job + n_out]
        scratch = refs[n_in + 2 * n_job + n_out:]
        start, middle, finish = job.phases(job_ins, job_outs, scratch[len(scratch) - n_sem:])
        now = step()
        pl.when(now == 0)(start)
        if middle is not None:
            assert n_steps >= 3
            pl.when(now == int(PASS_ON_AT * n_steps) - 1)(middle)
        body(*ins, *outs, *scratch[:len(scratch) - n_sem])
        pl.when(now == n_steps - 1)(finish)

    call = pl.pallas_call(riding, **kw)
    return lambda *args: call(*args, *job.arrays)


class _PairExchange:
    def __init__(self, slabs):
        self.arrays = list(slabs)
        n = len(slabs)
        self.out_shape = [jax.ShapeDtypeStruct((4,) + s.shape[1:], s.dtype) for s in slabs]
        self.scratch = [pltpu.SemaphoreType.DMA((n, 4)), pltpu.SemaphoreType.DMA((n, 4))]

    def phases(self, ins, outs, sems):
        n = len(ins)
        send_sems, recv_sems = sems
        x, y, c = _mesh_pos()

        def copies():
            return [pltpu.make_async_remote_copy(
                src_ref=ins[a].at[2 * j + 1 - c], dst_ref=outs[a].at[j],
                send_sem=send_sems.at[a, j], recv_sem=recv_sems.at[a, j],
                device_id=(x, y, 1 - c), device_id_type=MESH) for a in range(n) for j in range(4)]

        def start():
            for cp in copies():
                cp.start()

        def finish():
            for cp in copies():
                cp.wait_send()
            for cp in copies():
                cp.wait_recv()

        return start, None, finish


def _pair_add(slab, got, core, name):
    _, rows, d = slab.shape
    tr = _row_tile(rows, 1024)

    def body(core_ref, mine_ref, got_ref, out_ref):
        out_ref[...] = (mine_ref[...].astype(F32) + got_ref[...].astype(F32)).astype(out_ref.dtype)

    grid_spec = pltpu.PrefetchScalarGridSpec(
        num_scalar_prefetch=1, grid=(4, rows // tr),
        in_specs=[pl.BlockSpec((1, tr, d), lambda j, r, core_ref: (2 * j + core_ref[0], r, 0)),
                  pl.BlockSpec((1, tr, d), lambda j, r, core_ref: (j, r, 0))],
        out_specs=pl.BlockSpec((1, tr, d), lambda j, r, core_ref: (j, r, 0)))
    return pl.pallas_call(body, name=name, grid_spec=grid_spec,
                          out_shape=jax.ShapeDtypeStruct((4, rows, d), slab.dtype),
                          compiler_params=_params())(core, slab, got)


def _sum_blocks(parts, name):
    k, rows, d = parts.shape
    tr = _row_tile(rows, 512, BF16_ROWS if parts.dtype == BF16 else 8)

    def body(in_ref, out_ref):
        acc = in_ref[0].astype(F32)
        for j in range(1, k):
            acc = acc + in_ref[j].astype(F32)
        out_ref[...] = acc

    return pl.pallas_call(
        body, name=name, grid=(rows // tr,),
        in_specs=[pl.BlockSpec((k, tr, d), lambda r: (0, r, 0))],
        out_specs=pl.BlockSpec((tr, d), lambda r: (r, 0)),
        out_shape=jax.ShapeDtypeStruct((rows, d), F32), compiler_params=_params())(parts)


def _ln_stats(z):
    mu = jnp.mean(z, axis=-1, keepdims=True)
    zc = z - mu
    var = jnp.mean(zc * zc, axis=-1, keepdims=True)
    rstd = lax.rsqrt(var + LN_EPS)
    return zc * rstd, rstd


def _ln_bwd(dy, zhat, rstd, g):
    dyg = dy * g
    m1 = jnp.mean(dyg, axis=-1, keepdims=True)
    m2 = jnp.mean(dyg * zhat, axis=-1, keepdims=True)
    dz = rstd * (dyg - m1 - zhat * m2)
    return dz, jnp.sum(dy * zhat, axis=0, keepdims=True), jnp.sum(dy, axis=0, keepdims=True)


def _accumulate(ref, value, first):
    @pl.when(first)
    def _():
        ref[...] = value

    @pl.when(jnp.logical_not(first))
    def _():
        ref[...] += value


def _add_matmul(acc_ref, first, matmul):
    @pl.when(first)
    def _():
        acc_ref[...] = jnp.zeros_like(acc_ref)

    acc_ref[...] += matmul()


def _ffn_weight_specs(fs, d):
    def spec(row):
        return pl.BlockSpec((N_DEV, 1, fs, d), lambda i: (0, row, 0, 0), pipeline_mode=pl.Buffered(1))
    return [spec(0), spec(1), spec(2)]


def _ffn_fwd(x, w, ln_g, ln_b, alpha, name, job=None, mixer=None):
    s, d = x.shape
    fs = w.shape[2]
    f = N_DEV * fs
    tm = min(s, FFN_TOKENS)
    n_pre = 5 if mixer else 0

    def body(x_ref, *refs):
        xv = x_ref[...]
        if mixer:
            yab_ref, yc_ref, wo_ref, g0_ref, b0_ref = refs[:n_pre]
            zhat0_ref, rstd0_ref, x_out_ref = refs[len(refs) - 3:]
            mix = jnp.concatenate([yab_ref[...], yc_ref[...]], axis=1).astype(BF16)
            zhat0, rstd0 = _ln_stats(alpha * xv + _dot(mix, wo_ref[...].reshape(D_MIX, d)))
            zhat0_ref[...] = zhat0
            rstd0_ref[...] = rstd0
            xv = zhat0 * g0_ref[...] + b0_ref[...]
            x_out_ref[...] = xv
        wg_ref, wu_ref, wd_ref, g_ref, b_ref, a_ref, u_ref, h_ref, zhat_ref, rstd_ref, y_ref = refs[n_pre:n_pre + 11]
        xb = xv.astype(BF16)
        a = _dot_nt(xb, wg_ref[...].reshape(f, d))
        u = _dot_nt(xb, wu_ref[...].reshape(f, d))
        a_ref[...] = a.astype(BF16)
        u_ref[...] = u.astype(BF16)
        h = ((a * jax.nn.sigmoid(a)) * u).astype(BF16)
        h_ref[...] = h
        z = alpha * xv + 0.5 * _dot(h, wd_ref[...].reshape(f, d))
        zhat, rstd = _ln_stats(z)
        zhat_ref[...] = zhat
        rstd_ref[...] = rstd
        y_ref[...] = zhat * g_ref[...] + b_ref[...]

    row = pl.BlockSpec((tm, d), lambda i: (i, 0))
    col = pl.BlockSpec((tm, 1), lambda i: (i, 0))
    vec = pl.BlockSpec((1, d), lambda i: (0, 0))
    hid = pl.BlockSpec((tm, f), lambda i: (i, 0))
    half = pl.BlockSpec((tm, D_MIX // 2), lambda i: (i, 0))
    ln_out = [row, col, row]
    ln_shape = [jax.ShapeDtypeStruct((s, d), F32), jax.ShapeDtypeStruct((s, 1), F32), jax.ShapeDtypeStruct((s, d), F32)]
    pre_specs = [half, half, _whole(mixer[2]), vec, vec] if mixer else []
    call = _riding_call(
        body, job, 6 + n_pre, 6 + (3 if mixer else 0), s // tm, lambda: pl.program_id(0),
        name=name, grid=(s // tm,),
        in_specs=[row] + pre_specs + _ffn_weight_specs(fs, d) + [vec, vec],
        out_specs=[hid, hid, hid] + ln_out + (ln_out if mixer else []),
        out_shape=[jax.ShapeDtypeStruct((s, f), BF16)] * 3 + ln_shape + (ln_shape if mixer else []),
        compiler_params=_params())
    a, u, h, zhat, rstd, y, *rest = call(x, *(mixer or ()), w, w, w, ln_g, ln_b)
    if mixer:
        rest = [(rest[0], rest[1]), rest[2]] + rest[3:]
    return [a, u, h, (zhat, rstd), y] + rest


def _ffn_bwd_dx(dy, z, a, u, w, ln_g, alpha, name, job=None):
    s, d = dy.shape
    fs = w.shape[2]
    f = N_DEV * fs
    tm = min(s, FFN_TOKENS)

    def body(dy_ref, zhat_ref, rstd_ref, a_ref, u_ref, wg_ref, wu_ref, wd_ref, g_ref,
             dx_ref, da_ref, du_ref, df_ref, dg_ref, db_ref):
        i = pl.program_id(0)
        dz, dg, db = _ln_bwd(dy_ref[...], zhat_ref[...], rstd_ref[...], g_ref[...])
        _accumulate(dg_ref, dg, i == 0)
        _accumulate(db_ref, db, i == 0)
        df = (0.5 * dz).astype(BF16)
        df_ref[...] = df
        av = a_ref[...].astype(F32)
        uv = u_ref[...].astype(F32)
        sg = jax.nn.sigmoid(av)
        dh = _dot_nt(df, wd_ref[...].reshape(f, d))
        du = (dh * (av * sg)).astype(BF16)
        da = (dh * uv * (sg * (1.0 + av * (1.0 - sg)))).astype(BF16)
        da_ref[...] = da
        du_ref[...] = du
        dx_ref[...] = alpha * dz + _dot(da, wg_ref[...].reshape(f, d)) + _dot(du, wu_ref[...].reshape(f, d))

    row = pl.BlockSpec((tm, d), lambda i: (i, 0))
    col = pl.BlockSpec((tm, 1), lambda i: (i, 0))
    vec = pl.BlockSpec((1, d), lambda i: (0, 0))
    hid = pl.BlockSpec((tm, f), lambda i: (i, 0))
    call = _riding_call(
        body, job, 9, 6, s // tm, lambda: pl.program_id(0),
        name=name, grid=(s // tm,),
        in_specs=[row, row, col, hid, hid] + _ffn_weight_specs(fs, d) + [vec],
        out_specs=[row, hid, hid, row, vec, vec],
        out_shape=[jax.ShapeDtypeStruct((s, d), F32)] + [jax.ShapeDtypeStruct((s, f), BF16)] * 2
                  + [jax.ShapeDtypeStruct((s, d), BF16)] + [jax.ShapeDtypeStruct((1, d), F32)] * 2,
        compiler_params=_params())
    return call(dy, *z, a, u, w, w, w, ln_g)


def _ffn_bwd_dwgu(da, du, x, fs, name):
    s, d = x.shape
    tf = FFN_CHUNK_DEVS * fs
    n_c = N_DEV // FFN_CHUNK_DEVS
    tk = min(s, DW_TOKENS)
    n_k = s // tk

    def body(da_ref, du_ref, x_ref, out_ref, accg_s, accu_s):
        k = pl.program_id(1)
        xb = x_ref[...].astype(BF16)
        _add_matmul(accg_s, k == 0, lambda: _dot_tn(da_ref[...], xb))
        _add_matmul(accu_s, k == 0, lambda: _dot_tn(du_ref[...], xb))

        @pl.when(k == n_k - 1)
        def _():
            out_ref[:, 0] = accg_s[...].astype(BF16).reshape(FFN_CHUNK_DEVS, fs, d)
            out_ref[:, 1] = accu_s[...].astype(BF16).reshape(FFN_CHUNK_DEVS, fs, d)

    hid = pl.BlockSpec((tk, tf), lambda c, k: (k, c))
    return pl.pallas_call(
        body, name=name, grid=(n_c, n_k),
        in_specs=[hid, hid, pl.BlockSpec((tk, d), lambda c, k: (k, 0))],
        out_specs=pl.BlockSpec((FFN_CHUNK_DEVS, 2, fs, d), lambda c, k: (c, 0, 0, 0), pipeline_mode=pl.Buffered(1)),
        out_shape=jax.ShapeDtypeStruct((N_DEV, 3, fs, d), BF16),
        scratch_shapes=[pltpu.VMEM((tf, d), F32), pltpu.VMEM((tf, d), F32)],
        compiler_params=_params())(da, du, x)


def _ffn_bwd_dwd(h, df, blocks, name):
    s, d = df.shape
    fs = blocks.shape[2]
    tf = FFN_CHUNK_DEVS * fs
    n_c = N_DEV // FFN_CHUNK_DEVS
    tk = min(s, 2 * DW_TOKENS)
    n_k = s // tk

    def body(h_ref, df_ref, blocks_ref, out_ref, acc_s):
        k = pl.program_id(1)
        _add_matmul(acc_s, k == 0, lambda: _dot_tn(h_ref[...], df_ref[...]))

        @pl.when(k == n_k - 1)
        def _():
            out_ref[:, 0] = acc_s[...].astype(BF16).reshape(FFN_CHUNK_DEVS, fs, d)

    return pl.pallas_call(
        body, name=name, grid=(n_c, n_k),
        in_specs=[pl.BlockSpec((tk, tf), lambda c, k: (k, c)), pl.BlockSpec((tk, d), lambda c, k: (k, 0)), _any_spec()],
        out_specs=pl.BlockSpec((FFN_CHUNK_DEVS, 1, fs, d), lambda c, k: (c, 2, 0, 0), pipeline_mode=pl.Buffered(1)),
        out_shape=jax.ShapeDtypeStruct(blocks.shape, BF16), input_output_aliases={2: 0},
        scratch_shapes=[pltpu.VMEM((tf, d), F32)],
        compiler_params=_params())(h, df, blocks)


def _whole(arr):
    return pl.BlockSpec(arr.shape, lambda i: (0,) * arr.ndim, pipeline_mode=pl.Buffered(1))


def _win_fwd(x, w_in, name):
    s, d = x.shape
    d_in = N_DEV * w_in.shape[1]
    tm = min(s, MIX_TOKENS)
    scale = NA_HEAD_DIM ** -0.5
    assert d_in == D_LOC + 3 * D_NA and scale == 0.125

    def body(x_ref, w_ref, loc_ref, qkv_ref):
        proj = _dot_nt(x_ref[...].astype(BF16), w_ref[...].reshape(d_in, d))
        loc_ref[...] = proj[:, :D_LOC]
        qkv_ref[:, :D_NA] = (proj[:, D_LOC:D_LOC + D_NA] * scale).astype(BF16)
        qkv_ref[:, D_NA:] = proj[:, D_LOC + D_NA:].astype(BF16)

    return pl.pallas_call(
        body, name=name, grid=(s // tm,),
        in_specs=[pl.BlockSpec((tm, d), lambda i: (i, 0)), _whole(w_in)],
        out_specs=[pl.BlockSpec((tm, D_LOC), lambda i: (i, 0)), pl.BlockSpec((tm, 3 * D_NA), lambda i: (i, 0))],
        out_shape=[jax.ShapeDtypeStruct((s, D_LOC), F32), jax.ShapeDtypeStruct((s, 3 * D_NA), BF16)],
        compiler_params=_params())(x, w_in)


def _wout_bwd(dy, z, yab, yc, w_out, ln_g, alpha, name, job=None):
    s, d = dy.shape
    rs = w_out.shape[1]
    tm = min(s, MIX_TOKENS)
    n_i = s // tm

    def body(dy_ref, zhat_ref, rstd_ref, yab_ref, yc_ref, w_ref, g_ref, dmix_ref, dxp_ref, dg_ref, db_ref, out_ref, acc_s):
        i = pl.program_id(0)
        dz, dg, db = _ln_bwd(dy_ref[...], zhat_ref[...], rstd_ref[...], g_ref[...])
        _accumulate(dg_ref, dg, i == 0)
        _accumulate(db_ref, db, i == 0)
        dxp_ref[...] = alpha * dz
        dzb = dz.astype(BF16)
        dmix_ref[...] = _dot_nt(dzb, w_ref[...].reshape(D_MIX, d))
        mix = jnp.concatenate([yab_ref[...], yc_ref[...]], axis=1).astype(BF16)
        _add_matmul(acc_s, i == 0, lambda: _dot_tn(mix, dzb))

        @pl.when(i == n_i - 1)
        def _():
            out_ref[...] = acc_s[...].astype(BF16).reshape(N_DEV, rs, d)

    row = pl.BlockSpec((tm, d), lambda i: (i, 0))
    half = pl.BlockSpec((tm, D_MIX // 2), lambda i: (i, 0))
    vec = pl.BlockSpec((1, d), lambda i: (0, 0))
    call = _riding_call(
        body, job, 7, 5, n_i, lambda: pl.program_id(0),
        name=name, grid=(n_i,),
        in_specs=[row, row, pl.BlockSpec((tm, 1), lambda i: (i, 0)), half, half, _whole(w_out), vec],
        out_specs=[pl.BlockSpec((tm, D_MIX), lambda i: (i, 0)), row, vec, vec, _whole(w_out)],
        out_shape=[jax.ShapeDtypeStruct((s, D_MIX), F32), jax.ShapeDtypeStruct((s, d), F32),
                   jax.ShapeDtypeStruct((1, d), F32), jax.ShapeDtypeStruct((1, d), F32),
                   jax.ShapeDtypeStruct(w_out.shape, BF16)],
        scratch_shapes=[pltpu.VMEM((D_MIX, d), F32)],
        compiler_params=_params())
    return call(dy, *z, yab, yc, w_out, ln_g)


def _win_bwd(dxp, dloc, dq, dk, dv, x, w_in, name):
    s, d = x.shape
    rs = w_in.shape[1]
    d_in = N_DEV * rs
    tm = min(s, MIX_TOKENS)
    n_i = s // tm

    def body(dxp_ref, dloc_ref, dq_ref, dk_ref, dv_ref, x_ref, w_ref, dx_ref, out_ref, acc_s):
        i = pl.program_id(0)
        dp = jnp.concatenate([dloc_ref[...], dq_ref[...], dk_ref[...].astype(BF16), dv_ref[...].astype(BF16)], axis=1)
        dx_ref[...] = dxp_ref[...] + _dot(dp, w_ref[...].reshape(d_in, d))
        _add_matmul(acc_s, i == 0, lambda: _dot_tn(dp, x_ref[...].astype(BF16)))

        @pl.when(i == n_i - 1)
        def _():
            out_ref[...] = acc_s[...].astype(BF16).reshape(N_DEV, rs, d)

    row = pl.BlockSpec((tm, d), lambda i: (i, 0))
    na = pl.BlockSpec((tm, D_NA), lambda i: (i, 0))
    return pl.pallas_call(
        body, name=name, grid=(n_i,),
        in_specs=[row, pl.BlockSpec((tm, D_LOC), lambda i: (i, 0)), na, na, na, row, _whole(w_in)],
        out_specs=[row, _whole(w_in)],
        out_shape=[jax.ShapeDtypeStruct((s, d), F32), jax.ShapeDtypeStruct(w_in.shape, BF16)],
        scratch_shapes=[pltpu.VMEM((d_in, d), F32)],
        compiler_params=_params())(dxp, dloc, dq, dk, dv, x, w_in)


def _shift_rows(v, k):
    n = v.shape[0]
    return pltpu.roll(v, k % n, 0)


def _halo_specs(tm, s, width, col):
    per = tm // HALO
    last = s // HALO - 1
    return [pl.BlockSpec((HALO, width), lambda i: (jnp.maximum(i * per - 1, 0), col)),
            pl.BlockSpec((tm, width), lambda i: (i, col)),
            pl.BlockSpec((HALO, width), lambda i: (jnp.minimum((i + 1) * per, last), col))]


def _token_index(i, tm):
    return i * tm - HALO + lax.broadcasted_iota(jnp.int32, (tm + 2 * HALO, 1), 0)


def _pool_lane_tables():
    lane = lax.broadcasted_iota(jnp.int32, (1, D_POOL), 1)
    group = sum((lane >= g * POOL_GROUP).astype(jnp.int32) for g in range(1, len(POOL_WINDOWS)))
    half = jnp.where(group == 0, 1, jnp.where(group == 1, 2, jnp.where(group == 2, 4, 8)))
    return group, half


def _window_sums(v, group, offsets):
    s2 = v + _shift_rows(v, 1)
    s4 = s2 + _shift_rows(s2, 2)
    s8 = s4 + _shift_rows(s4, 4)
    s16 = s8 + _shift_rows(s8, 8)
    parts = [_shift_rows(p, -o) if o else p for p, o in zip((s2, s4, s8, s16), offsets)]
    return jnp.where(group == 0, parts[0], jnp.where(group == 1, parts[1], jnp.where(group == 2, parts[2], parts[3])))


def _pool_counts(tok, half, s):
    return (jnp.minimum(tok + half, s) - jnp.maximum(tok - half, 0)).astype(F32)


def _pool_forward(u, tok, s):
    group, half = _pool_lane_tables()
    sums = _window_sums(u, group, [w // 2 - 1 for w in POOL_WINDOWS])
    return sums / _pool_counts(tok, half, s) - u


def _conv_forward(zc, cw_ref):
    return cw_ref[0:1, :] * _shift_rows(zc, 1) + cw_ref[1:2, :] * zc + cw_ref[2:3, :] * _shift_rows(zc, -1)


def _local_fwd(proj, pool_bd, pool_scale, conv_w, name):
    s = proj.shape[0]
    tm = min(s, MIX_TOKENS)
    ctr = slice(HALO, HALO + tm)

    def body(prev_ref, cur_ref, next_ref, pw_ref, sc_ref, cw_ref, out_ref):
        i = pl.program_id(0)
        ext = jnp.concatenate([prev_ref[...], cur_ref[...], next_ref[...]], axis=0)
        tok = _token_index(i, tm)
        inside = (tok >= 0) & (tok < s)
        u = jnp.where(inside, ext[:, 0:D_POOL], 0.0)
        p = _pool_forward(u, tok, s)[ctr]
        ya = _dot(p.astype(BF16), pw_ref[...]) * sc_ref[...]
        gb = ext[:, D_POOL:D_POOL + D_CONV]
        zc = jnp.where(inside, ext[:, D_POOL + D_CONV:D_POOL + 2 * D_CONV] * ext[:, D_POOL + 2 * D_CONV:D_LOC], 0.0)
        yb = (gb * _conv_forward(zc, cw_ref))[ctr]
        out_ref[...] = jnp.concatenate([ya, yb], axis=1)

    return pl.pallas_call(
        body, name=name, grid=(s // tm,),
        in_specs=_halo_specs(tm, s, D_LOC, 0) + [
            pl.BlockSpec((D_POOL, D_POOL), lambda i: (0, 0)), pl.BlockSpec((1, D_POOL), lambda i: (0, 0)),
            pl.BlockSpec((3, D_CONV), lambda i: (0, 0))],
        out_specs=pl.BlockSpec((tm, D_POOL + D_CONV), lambda i: (i, 0)),
        out_shape=jax.ShapeDtypeStruct((s, D_POOL + D_CONV), F32),
        compiler_params=_params())(proj, proj, proj, pool_bd, pool_scale, conv_w)


def _local_bwd(proj, dmix, pool_bd, pool_scale, conv_w, name):
    s = proj.shape[0]
    tm = min(s, MIX_TOKENS)
    ctr = slice(HALO, HALO + tm)

    def body(prev_ref, cur_ref, next_ref, dprev_ref, dcur_ref, dnext_ref, pw_ref, sc_ref, cw_ref,
             dloc_ref, dpw_ref, dsc_ref, dcw_ref):
        i = pl.program_id(0)
        first = i == 0
        ext = jnp.concatenate([prev_ref[...], cur_ref[...], next_ref[...]], axis=0)
        dext = jnp.concatenate([dprev_ref[...], dcur_ref[...], dnext_ref[...]], axis=0)
        tok = _token_index(i, tm)
        inside = (tok >= 0) & (tok < s)
        group, half = _pool_lane_tables()
        cnt = _pool_counts(tok, half, s)
        u = jnp.where(inside, ext[:, 0:D_POOL], 0.0)
        dya = jnp.where(inside, dext[:, 0:D_POOL], 0.0)
        p_c = _pool_forward(u, tok, s)[ctr].astype(BF16)
        lin = _dot(p_c, pw_ref[...])
        _accumulate(dsc_ref, jnp.sum(dya[ctr] * lin, axis=0, keepdims=True), first)
        e1 = (dya * sc_ref[...]).astype(BF16)
        _accumulate(dpw_ref, _dot_tn(p_c, e1[ctr]), first)
        dp = _dot_nt(e1, pw_ref[...])
        du = _window_sums(dp / cnt, group, [w // 2 for w in POOL_WINDOWS]) - dp
        gb = ext[:, D_POOL:D_POOL + D_CONV]
        gc = ext[:, D_POOL + D_CONV:D_POOL + 2 * D_CONV]
        hv = ext[:, D_POOL + 2 * D_CONV:D_LOC]
        zc = jnp.where(inside, gc * hv, 0.0)
        dyb = jnp.where(inside, dext[:, D_POOL:D_POOL + D_CONV], 0.0)
        dgb = dyb * _conv_forward(zc, cw_ref)
        dyc = dyb * gb
        for k in range(3):
            part = jnp.sum(dyc[ctr] * _shift_rows(zc, 1 - k)[ctr], axis=0, keepdims=True)
            _accumulate(dcw_ref.at[k:k + 1, :], part, first)
        dzc = cw_ref[0:1, :] * _shift_rows(dyc, -1) + cw_ref[1:2, :] * dyc + cw_ref[2:3, :] * _shift_rows(dyc, 1)
        dloc = jnp.concatenate([du, dgb, dzc * hv, dzc * gc], axis=1)
        dloc_ref[...] = dloc[ctr].astype(BF16)

    return pl.pallas_call(
        body, name=name, grid=(s // tm,),
        in_specs=_halo_specs(tm, s, D_LOC, 0) + _halo_specs(tm, s, D_POOL + D_CONV, 0) + [
            pl.BlockSpec((D_POOL, D_POOL), lambda i: (0, 0)), pl.BlockSpec((1, D_POOL), lambda i: (0, 0)),
            pl.BlockSpec((3, D_CONV), lambda i: (0, 0))],
        out_specs=[pl.BlockSpec((tm, D_LOC), lambda i: (i, 0)), pl.BlockSpec((D_POOL, D_POOL), lambda i: (0, 0)),
                   pl.BlockSpec((1, D_POOL), lambda i: (0, 0)), pl.BlockSpec((8, D_CONV), lambda i: (0, 0))],
        out_shape=[jax.ShapeDtypeStruct((s, D_LOC), BF16), jax.ShapeDtypeStruct((D_POOL, D_POOL), F32),
                   jax.ShapeDtypeStruct((1, D_POOL), F32), jax.ShapeDtypeStruct((8, D_CONV), F32)],
        compiler_params=_params())(proj, proj, proj, dmix, dmix, dmix, pool_bd, pool_scale, conv_w)


def _na_geometry(rows):
    n_j = rows // Q_ROWS
    dr = np.full((3, Q_ROWS, K_ROWS), 2 * NA_ROWS - 1, np.int64)
    for t, j in enumerate((0, min(1, n_j - 1), n_j - 1)):
        base = int(np.clip(Q_ROWS * j - NA_ROWS // 2, 0, rows - K_ROWS))
        for qr in range(Q_ROWS):
            r = Q_ROWS * j + qr
            start = int(np.clip(r - NA_ROWS // 2, 0, rows - NA_ROWS))
            for kr in range(K_ROWS):
                if start <= base + kr < start + NA_ROWS:
                    dr[t, qr, kr] = base + kr - r + NA_ROWS - 1
    return dr


def _na_col_tables():
    c = np.arange(GRID_W)
    start = np.clip(c - NA_COLS // 2, 0, GRID_W - NA_COLS)
    valid = (c[None, :] >= start[:, None]) & (c[None, :] < start[:, None] + NA_COLS)
    dc = np.clip(c[None, :] - c[:, None], -(NA_COLS - 1), NA_COLS - 1) + (NA_COLS - 1)
    return valid, dc


NO_ROW = 2 * NA_ROWS - 1
N_SLOT = 2 * NA_ROWS


def _na_tiles(rpb):
    valid, dc = _na_col_tables()
    onehot = jnp.asarray((dc[None] == np.arange(2 * NA_COLS - 1)[:, None, None]).astype(np.float32))
    table = jnp.einsum("hrd,dqk->hrqk", rpb, onehot, precision=lax.Precision.HIGHEST)
    table = jnp.where(jnp.asarray(valid)[None, None], table, NEG_INF)
    outside = jnp.full((NA_HEADS, 1, GRID_W, GRID_W), NEG_INF, F32)
    padded = jnp.concatenate([outside, table, outside], axis=1)
    pairs = jnp.concatenate([padded[:, :N_SLOT], padded[:, 1:]], axis=-1)
    return jnp.concatenate([pairs, jnp.full((NA_HEADS, 1, GRID_W, 2 * GRID_W), NEG_INF, F32)], axis=1)


G_ROWS = 2
N_GRP = Q_ROWS // G_ROWS
G_TOK = G_ROWS * GRID_W
GK_ROWS = NA_ROWS + G_ROWS
GK_TOK = GK_ROWS * GRID_W
STACK_TOK = N_GRP * 2 * G_TOK


def _na_group_tables(rows):
    dr = _na_geometry(rows)
    koff = np.zeros((3, N_GRP), np.int64)
    slot = np.zeros((3, N_GRP, G_ROWS, GK_ROWS // 2), np.int64)
    even_in, odd_in = np.zeros_like(slot), np.zeros_like(slot)
    for t in range(3):
        for g in range(N_GRP):
            qrs = range(G_ROWS * g, G_ROWS * (g + 1))
            inside = [kr for kr in range(K_ROWS) if any(dr[t, qr, kr] != NO_ROW for qr in qrs)]
            lo, hi = min(inside), max(inside) + 1
            off = min(lo - lo % 2, K_ROWS - GK_ROWS)
            assert off <= lo and hi <= off + GK_ROWS
            koff[t, g] = off
            for qq, qr in enumerate(qrs):
                for kp in range(GK_ROWS // 2):
                    even, odd = int(dr[t, qr, off + 2 * kp]), int(dr[t, qr, off + 2 * kp + 1])
                    even_in[t, g, qq, kp], odd_in[t, g, qq, kp] = even != NO_ROW, odd != NO_ROW
                    slot[t, g, qq, kp] = (N_SLOT if even == NO_ROW and odd == NO_ROW
                                          else (even if even != NO_ROW else odd - 1) + 1)
    return koff, slot, even_in, odd_in


def _by_type(block_type, per_type):
    a, b, c = (int(v) for v in per_type)
    if a == b == c:
        return a
    return jnp.where(block_type == 0, a, jnp.where(block_type == 2, c, b))


def _score_rows(g, hh):
    first = (2 * g + hh) * G_TOK
    return slice(first, first + G_TOK)


def _tile_at(g, hh, qq, kp):
    first = _score_rows(g, hh).start + qq * GRID_W
    return slice(first, first + GRID_W), slice(kp * 2 * GRID_W, (kp + 1) * 2 * GRID_W)


def _fill_bias(bias_s, tiles_ref, block_type, tables):
    _, slot, even_in, odd_in = tables
    left = lax.broadcasted_iota(jnp.int32, (1, 2 * GRID_W), 1) < GRID_W
    for hh in range(2):
        for g in range(N_GRP):
            for qq in range(G_ROWS):
                for kp in range(GK_ROWS // 2):
                    tile = tiles_ref[hh, _by_type(block_type, slot[:, g, qq, kp])]
                    tile = jnp.where(left & (_by_type(block_type, even_in[:, g, qq, kp]) == 0), NEG_INF, tile)
                    tile = jnp.where(jnp.logical_not(left) & (_by_type(block_type, odd_in[:, g, qq, kp]) == 0), NEG_INF, tile)
                    rs, cs = _tile_at(g, hh, qq, kp)
                    bias_s[rs, cs] = tile


def _group_offset(block_type, koff, g):
    off = _by_type(block_type, koff[:, g]) * GRID_W
    return off if isinstance(off, int) else pl.multiple_of(off, 2 * GRID_W)


def _na_specs(s, proj_cols):
    n_blk = s // K_BLK
    per = Q_TOK // K_BLK

    def kv_spec(col0, m):
        return pl.BlockSpec((K_BLK, HEAD_PAIR), lambda hp, j: (jnp.clip(per * j - 1, 0, n_blk - 4) + m, col0 + hp))

    q_col, k_col, v_col = (c // HEAD_PAIR for c in proj_cols)
    return ([pl.BlockSpec((Q_TOK, HEAD_PAIR), lambda hp, j: (j, q_col + hp))]
            + [kv_spec(k_col, m) for m in range(4)] + [kv_spec(v_col, m) for m in range(4)])


def _na_block_type(j, n_j):
    return jnp.where(j == 0, 0, jnp.where(j == n_j - 1, 2, 1))


def _head_masks():
    lane = lax.broadcasted_iota(jnp.int32, (1, HEAD_PAIR), 1)
    return [lane < NA_HEAD_DIM, lane >= NA_HEAD_DIM]


def _attn_fwd(qkv, tiles, name):
    s = qkv.shape[0]
    n_j = s // Q_TOK
    tables = _na_group_tables(s // GRID_W)
    koff = tables[0]

    def body(q_ref, k0, k1, k2, k3, v0, v1, v2, v3, tiles_ref, o_ref, lse_ref, bias_s, k_s, vh_s, sc_s, p_s):
        j = pl.program_id(1)
        block_type = _na_block_type(j, n_j)
        pl.when((j == 0) | (j == 1) | (j == n_j - 1))(functools.partial(_fill_bias, bias_s, tiles_ref, block_type, tables))
        masks = _head_masks()
        for m, (kr, vr) in enumerate(zip((k0, k1, k2, k3), (v0, v1, v2, v3))):
            rows = slice(m * K_BLK, (m + 1) * K_BLK)
            k_s[rows, :] = kr[...]
            v = vr[...]
            for hh, mask in enumerate(masks):
                vh_s[hh, rows, :] = jnp.where(mask, v, jnp.zeros_like(v))
        q = q_ref[...]
        qh = [jnp.where(mask, q, jnp.zeros_like(q)) for mask in masks]
        offs = [_group_offset(block_type, koff, g) for g in range(N_GRP)]
        for g in range(N_GRP):
            kg = k_s[pl.ds(offs[g], GK_TOK), :]
            for hh in range(2):
                sc_s[_score_rows(g, hh), :] = _dot_nt(qh[hh][g * G_TOK:(g + 1) * G_TOK], kg)
        sc = sc_s[...] + bias_s[...]
        mx = jnp.max(sc, axis=-1, keepdims=True)
        p = jnp.exp(sc - mx)
        den = jnp.sum(p, axis=-1, keepdims=True)
        p_s[...] = p.astype(BF16)
        inv = 1.0 / den
        lse = mx + jnp.log(den)
        for g in range(N_GRP):
            rows = slice(g * G_TOK, (g + 1) * G_TOK)
            out = jnp.zeros((G_TOK, HEAD_PAIR), F32)
            for hh in range(2):
                sr = _score_rows(g, hh)
                out = out + _dot(p_s[sr, :], vh_s[hh, pl.ds(offs[g], GK_TOK), :]) * inv[sr]
            o_ref[rows, :] = out
            lse_ref[0, rows, :] = jnp.where(masks[0], lse[_score_rows(g, 0)], lse[_score_rows(g, 1)])

    return pl.pallas_call(
        body, name=name, grid=(NA_HEADS // 2, n_j),
        in_specs=_na_specs(s, (0, D_NA, 2 * D_NA)) + [
            pl.BlockSpec((2, N_SLOT + 1, GRID_W, 2 * GRID_W), lambda hp, j: (hp, 0, 0, 0))],
        out_specs=[pl.BlockSpec((Q_TOK, HEAD_PAIR), lambda hp, j: (j, hp)),
                   pl.BlockSpec((1, Q_TOK, HEAD_PAIR), lambda hp, j: (hp, j, 0))],
        out_shape=[jax.ShapeDtypeStruct((s, D_NA), F32), jax.ShapeDtypeStruct((NA_HEADS // 2, s, HEAD_PAIR), F32)],
        scratch_shapes=[pltpu.VMEM((STACK_TOK, GK_TOK), F32), pltpu.VMEM((K_TOK, HEAD_PAIR), BF16),
                        pltpu.VMEM((2, K_TOK, HEAD_PAIR), BF16), pltpu.VMEM((STACK_TOK, GK_TOK), F32),
                        pltpu.VMEM((STACK_TOK, GK_TOK), BF16)],
        compiler_params=_params())(*([qkv] * 9), tiles)


def _add_tiles(dtile_ref, ds_ref, block_type, slot, has_interior):
    def tile(g, hh, qq, kp):
        rs, cs = _tile_at(g, hh, qq, kp)
        return ds_ref[rs, cs].astype(F32)

    def interior():
        for hh in range(2):
            for qq in range(G_ROWS):
                for kp in range(GK_ROWS // 2):
                    assert (slot[1, :, qq, kp] == slot[1, 0, qq, kp]).all()
                    if slot[1, 0, qq, kp] != N_SLOT:
                        dtile_ref[hh, int(slot[1, 0, qq, kp])] += sum(tile(g, hh, qq, kp) for g in range(N_GRP))

    def edge():
        for hh in range(2):
            for g in range(N_GRP):
                for qq in range(G_ROWS):
                    for kp in range(GK_ROWS // 2):
                        first, last = (0 if e == N_SLOT else int(e) for e in slot[[0, 2], g, qq, kp])
                        if (slot[[0, 2], g, qq, kp] != N_SLOT).any():
                            dtile_ref[hh, _by_type(block_type, (first, first, last))] += tile(g, hh, qq, kp)

    if has_interior:
        pl.when(block_type == 1)(interior)
    pl.when(block_type != 1)(edge)


def _attn_bwd(qkv, tiles, o, dmix, lse, name, job=None):
    s = qkv.shape[0]
    n_j = s // Q_TOK
    n_blk = s // K_BLK
    per = Q_TOK // K_BLK
    scale = NA_HEAD_DIM ** -0.5
    do_col = (D_POOL + D_CONV) // HEAD_PAIR
    tables = _na_group_tables(s // GRID_W)
    koff, slot = tables[0], tables[1]

    def body(q_ref, k0, k1, k2, k3, v0, v1, v2, v3, tiles_ref, o_ref, do_ref, lse_ref,
             dq_ref, dk_ref, dv_ref, dtile_ref, bias_s, k_s, kh_s, v_s, s_s, dp_s, pb_s, dsb_s):
        j = pl.program_id(1)

        @pl.when(j == 0)
        def _():
            dk_ref[...] = jnp.zeros_like(dk_ref)
            dv_ref[...] = jnp.zeros_like(dv_ref)
            dtile_ref[...] = jnp.zeros_like(dtile_ref)

        block_type = _na_block_type(j, n_j)
        pl.when((j == 0) | (j == 1) | (j == n_j - 1))(functools.partial(_fill_bias, bias_s, tiles_ref, block_type, tables))
        base = pl.multiple_of(jnp.clip(per * j - 1, 0, n_blk - 4) * K_BLK, K_BLK)
        masks = _head_masks()
        for m, (kr, vr) in enumerate(zip((k0, k1, k2, k3), (v0, v1, v2, v3))):
            rows = slice(m * K_BLK, (m + 1) * K_BLK)
            k = kr[...]
            k_s[rows, :] = k
            v_s[rows, :] = vr[...]
            for hh, mask in enumerate(masks):
                kh_s[hh, rows, :] = jnp.where(mask, k, jnp.zeros_like(k))
        q = q_ref[...]
        qh = [jnp.where(mask, q, jnp.zeros_like(q)) for mask in masks]
        lane = lax.broadcasted_iota(jnp.int32, (1, HEAD_PAIR), 1)
        offs = [_group_offset(block_type, koff, g) for g in range(N_GRP)]
        do, ov, lse = do_ref[...], o_ref[...], lse_ref[0]
        dob, lse_col, delta_col = {}, [], []
        for g in range(N_GRP):
            rows = slice(g * G_TOK, (g + 1) * G_TOK)
            kg = k_s[pl.ds(offs[g], GK_TOK), :]
            vg = v_s[pl.ds(offs[g], GK_TOK), :]
            for hh, mask in enumerate(masks):
                doh = jnp.where(mask, do[rows], 0.0)
                dob[g, hh] = doh.astype(BF16)
                lse_col.append(jnp.sum(jnp.where(lane == hh * NA_HEAD_DIM, lse[rows], 0.0), axis=-1, keepdims=True))
                delta_col.append(jnp.sum(doh * ov[rows], axis=-1, keepdims=True))
                s_s[_score_rows(g, hh), :] = _dot_nt(qh[hh][rows], kg)
                dp_s[_score_rows(g, hh), :] = _dot_nt(dob[g, hh], vg)
        p = jnp.exp(s_s[...] + bias_s[...] - jnp.concatenate(lse_col, axis=0))
        ds = p * (dp_s[...] - jnp.concatenate(delta_col, axis=0))
        pb_s[...] = p.astype(BF16)
        dsb_s[...] = ds.astype(BF16)
        _add_tiles(dtile_ref, dsb_s, block_type, slot, n_j > 2)
        for g in range(N_GRP):
            rows = slice(g * G_TOK, (g + 1) * G_TOK)
            dq = jnp.zeros((G_TOK, HEAD_PAIR), F32)
            dk = jnp.zeros((GK_TOK, HEAD_PAIR), F32)
            dv = jnp.zeros((GK_TOK, HEAD_PAIR), F32)
            for hh in range(2):
                sr = _score_rows(g, hh)
                dsb = dsb_s[sr, :]
                dq = dq + _dot(dsb, kh_s[hh, pl.ds(offs[g], GK_TOK), :])
                dk = dk + _dot_tn(dsb, qh[hh][rows])
                dv = dv + _dot_tn(pb_s[sr, :], dob[g, hh])
            dq_ref[rows, :] = (dq * scale).astype(BF16)
            at = pl.multiple_of(base + offs[g], 2 * GRID_W)
            dk_ref[pl.ds(at, GK_TOK), :] += dk
            dv_ref[pl.ds(at, GK_TOK), :] += dv

    pair = pl.BlockSpec((Q_TOK, HEAD_PAIR), lambda hp, j: (j, hp))
    whole = pl.BlockSpec((s, HEAD_PAIR), lambda hp, j: (0, hp))
    call = _riding_call(
        body, job, 13, 4, (NA_HEADS // 2) * n_j, lambda: pl.program_id(0) * n_j + pl.program_id(1),
        name=name, grid=(NA_HEADS // 2, n_j),
        in_specs=_na_specs(s, (0, D_NA, 2 * D_NA)) + [
            pl.BlockSpec((2, N_SLOT + 1, GRID_W, 2 * GRID_W), lambda hp, j: (hp, 0, 0, 0)),
            pair, pl.BlockSpec((Q_TOK, HEAD_PAIR), lambda hp, j: (j, do_col + hp)),
            pl.BlockSpec((1, Q_TOK, HEAD_PAIR), lambda hp, j: (hp, j, 0))],
        out_specs=[pair, whole, whole, pl.BlockSpec((2, N_SLOT, GRID_W, 2 * GRID_W), lambda hp, j: (hp, 0, 0, 0))],
        out_shape=[jax.ShapeDtypeStruct((s, D_NA), BF16), jax.ShapeDtypeStruct((s, D_NA), F32),
                   jax.ShapeDtypeStruct((s, D_NA), F32),
                   jax.ShapeDtypeStruct((NA_HEADS, N_SLOT, GRID_W, 2 * GRID_W), F32)],
        scratch_shapes=[pltpu.VMEM((STACK_TOK, GK_TOK), F32), pltpu.VMEM((K_TOK, HEAD_PAIR), BF16),
                        pltpu.VMEM((2, K_TOK, HEAD_PAIR), BF16), pltpu.VMEM((K_TOK, HEAD_PAIR), BF16),
                        pltpu.VMEM((STACK_TOK, GK_TOK), F32), pltpu.VMEM((STACK_TOK, GK_TOK), F32),
                        pltpu.VMEM((STACK_TOK, GK_TOK), BF16), pltpu.VMEM((STACK_TOK, GK_TOK), BF16)],
        compiler_params=_params())
    return call(*([qkv] * 9), tiles, o, dmix, lse)


def _rpb_finish(tiles, name):
    valid, dc = _na_col_tables()
    n_dc = 2 * NA_COLS - 1
    sel = np.zeros((GRID_W, 2 * GRID_W, LANES), np.float32)
    for qc in range(GRID_W):
        for kc in range(GRID_W):
            if valid[qc, kc]:
                sel[qc, kc, dc[qc, kc]] = 1.0
                sel[qc, GRID_W + kc, LANES // 2 + dc[qc, kc]] = 1.0
    sel = jnp.asarray(sel.reshape(GRID_W * 2 * GRID_W, LANES))
    flat = tiles.reshape(NA_HEADS * 2 * NA_ROWS, GRID_W * 2 * GRID_W)

    def body(a_ref, b_ref, out_ref):
        out_ref[...] = jnp.dot(a_ref[...], b_ref[...], preferred_element_type=F32, precision=lax.Precision.HIGHEST)

    sums = pl.pallas_call(
        body, name=name, out_shape=jax.ShapeDtypeStruct((flat.shape[0], LANES), F32),
        compiler_params=_params())(flat, sel).reshape(NA_HEADS, 2 * NA_ROWS, LANES)
    return sums[:, 1:, :n_dc] + sums[:, :2 * NA_ROWS - 1, LANES // 2:LANES // 2 + n_dc]


def _loss_grad(y, target, name):
    s, d = y.shape
    tm = min(s, LOSS_TOKENS)

    def body(y_ref, t_ref, sum_ref, dy_ref):
        diff = y_ref[...] - t_ref[...]
        dy_ref[...] = diff * (1.0 / d)
        part = jnp.zeros((8, LANES), F32) + jnp.sum(diff * diff)
        _accumulate(sum_ref, part, pl.program_id(0) == 0)

    row = pl.BlockSpec((tm, d), lambda i: (i, 0))
    return pl.pallas_call(
        body, name=name, grid=(s // tm,), in_specs=[row, row],
        out_specs=[pl.BlockSpec((8, LANES), lambda i: (0, 0)), row],
        out_shape=[jax.ShapeDtypeStruct((8, LANES), F32), jax.ShapeDtypeStruct((s, d), F32)],
        compiler_params=_params())(y, target)


def _adamw(w, g, m, v, name):
    rows, cols = w.shape
    tr = _row_tile(rows, 512, 8)

    def body(w_ref, g_ref, m_ref, v_ref, d_ref, nm_ref, nv_ref):
        gv = g_ref[...]
        nm = ADAM_B1 * m_ref[...] + (1.0 - ADAM_B1) * gv
        nv = ADAM_B2 * v_ref[...] + (1.0 - ADAM_B2) * (gv * gv)
        m_hat = nm / (1.0 - ADAM_B1 ** ADAM_STEP)
        v_hat = nv / (1.0 - ADAM_B2 ** ADAM_STEP)
        d_ref[...] = -ADAM_LR * (m_hat / (jnp.sqrt(v_hat) + ADAM_EPS) + ADAM_WD * w_ref[...])
        nm_ref[...] = nm
        nv_ref[...] = nv

    blk = pl.BlockSpec((tr, cols), lambda r: (r, 0))
    return pl.pallas_call(
        body, name=name, grid=(rows // tr,), in_specs=[blk] * 4, out_specs=[blk] * 3,
        out_shape=[jax.ShapeDtypeStruct((rows, cols), F32)] * 3, compiler_params=_params())(w, g, m, v)


def _adamw_nd(w, g, m, v, name):
    shape = w.shape
    flat = lambda t: t.reshape(-1, shape[-1])
    return tuple(t.reshape(shape) for t in _adamw(flat(w), flat(g), flat(m), flat(v), name))


def _pack(parts, rows_mult=64):
    flat = jnp.concatenate([p.reshape(-1).astype(F32) for p in parts])
    per = LANES * rows_mult
    total = -(-flat.shape[0] // per) * per
    return jnp.pad(flat, (0, total - flat.shape[0])).reshape(-1, LANES)


def _unpack(packed, shapes):
    flat = packed.reshape(-1)
    out, pos = [], 0
    for shp in shapes:
        n = int(np.prod(shp))
        out.append(flat[pos:pos + n].reshape(shp))
        pos += n
    return out


def kernel(x, ffn1_w_gate, ffn1_w_up, ffn1_w_down, ffn2_w_gate, ffn2_w_up, ffn2_w_down, w_in, pool_w, pool_scale, conv_w, rpb, w_out, ln_g, ln_b, loss_target, m_ffn1_w_gate, m_ffn1_w_up, m_ffn1_w_down, m_ffn2_w_gate, m_ffn2_w_up, m_ffn2_w_down, m_w_in, m_pool_w, m_pool_scale, m_conv_w, m_rpb, m_w_out, m_ln_g, m_ln_b, v_ffn1_w_gate, v_ffn1_w_up, v_ffn1_w_down, v_ffn2_w_gate, v_ffn2_w_up, v_ffn2_w_down, v_w_in, v_pool_w, v_pool_scale, v_conv_w, v_rpb, v_w_out, v_ln_g, v_ln_b):
    n_l, d, fs = ffn1_w_gate.shape
    s = x.shape[1]
    rows = s // GRID_W
    assert x.shape[0] == 1 and s % Q_TOK == 0 and rows >= K_ROWS and fs % BF16_ROWS == 0
    alpha = (2.0 * n_l) ** 0.25
    xi, yi, ci = _mesh_pos()
    me = 4 * xi + 2 * yi + ci
    core = jnp.reshape(ci, (1,)).astype(jnp.int32)
    ln_w, cw_w = ln_g.shape[2], conv_w.shape[2]

    tr = lambda w: jnp.swapaxes(w, 1, 2)
    ffn1_shard = jnp.stack([tr(ffn1_w_gate), tr(ffn1_w_up), ffn1_w_down], axis=1).astype(BF16)
    ffn2_shard = jnp.stack([tr(ffn2_w_gate), tr(ffn2_w_up), ffn2_w_down], axis=1).astype(BF16)
    win_shard, wout_shard = tr(w_in).astype(BF16), w_out.astype(BF16)
    small_shard = _pack([ln_g, ln_b, conv_w])
    w_ffn1, small = _exchange_alone(_Gather([ffn1_shard[0], small_shard]), "gather_first")
    n_ln = n_l * 3 * ln_w
    small = small.reshape(N_DEV, -1)
    unshard = lambda t, width: jnp.moveaxis(t.reshape(N_DEV, n_l, 3, width), 0, 2).reshape(n_l, 3, N_DEV * width)
    ln_g_all = unshard(small[:, :n_ln], ln_w)
    ln_b_all = unshard(small[:, n_ln:2 * n_ln], ln_w)
    conv_all = unshard(small[:, 2 * n_ln:2 * n_ln + n_l * 3 * cw_w], cw_w)
    pool_bd = jnp.zeros((n_l, D_POOL, D_POOL), F32)
    for g in range(len(POOL_WINDOWS)):
        sl = slice(g * POOL_GROUP, (g + 1) * POOL_GROUP)
        pool_bd = pool_bd.at[:, sl, sl].set(pool_w[:, g])
    pool_bd = pool_bd.astype(BF16)
    lnp = lambda arr, l, j: arr[l, j].reshape(1, d)

    saved = []
    h = x.reshape(s, d)
    for l in range(n_l):
        a1, u1, h1, z1, x1, w_in_l, w_out_l, w_ffn2 = _ffn_fwd(
            h, w_ffn1, lnp(ln_g_all, l, 0), lnp(ln_b_all, l, 0), alpha, f"ffn1_fwd_{l}",
            job=_Gather([win_shard[l], wout_shard[l], ffn2_shard[l]]))
        proj = _win_fwd(x1, w_in_l, f"win_fwd_{l}")
        bias = _na_tiles(rpb[l])
        yab = _local_fwd(proj[0], pool_bd[l], pool_scale[l].reshape(1, D_POOL), conv_all[l], f"local_fwd_{l}")
        yc, lse = _attn_fwd(proj[1], bias, f"attn_fwd_{l}")
        a2, u2, h2, z3, x3, z2, x2, *w_next = _ffn_fwd(
            x1, w_ffn2, lnp(ln_g_all, l, 2), lnp(ln_b_all, l, 2), alpha, f"ffn2_fwd_{l}",
            job=_Gather([ffn1_shard[l + 1]]) if l + 1 < n_l else None,
            mixer=(yab, yc, w_out_l, lnp(ln_g_all, l, 1), lnp(ln_b_all, l, 1)))
        saved.append((h, a1, u1, h1, z1, x1, proj, bias, yab, yc, lse, z2, x2, a2, u2, h2, z3, w_ffn1, w_in_l, w_out_l, w_ffn2))
        h = x3
        if w_next:
            w_ffn1 = w_next[0]

    sq, dh = _loss_grad(h, loss_target.reshape(s, d), "loss_head")
    loss = lax.psum(sq[0, 0] * (0.5 / d), MESH_AXES)

    flat = lambda blocks: [b.reshape(N_DEV, -1, d) for b in blocks]
    pair_add = lambda blocks, got, tag: [_pair_add(b, g, core, f"grads_pair_add_{tag}_{i}")
                                         for i, (b, g) in enumerate(zip(blocks, got))]
    small_grads = [None] * n_l
    reduced = [None] * n_l
    above = None
    for l in reversed(range(n_l)):
        x0, a1, u1, h1, z1, x1, proj, bias, yab, yc, lse, z2, x2, a2, u2, h2, z3, w_ffn1, w_in_l, w_out_l, w_ffn2 = saved[l]
        dx2, da, du, df, dg3, db3, *got = _ffn_bwd_dx(
            dh, z3, a2, u2, w_ffn2, lnp(ln_g_all, l, 2), alpha, f"ffn2_bwd_dx_{l}",
            job=_PairExchange(above) if above else None)
        above_pairs = pair_add(above, got, f"mix_{l + 1}") if above else None
        g2 = flat([_ffn_bwd_dwd(h2, df, _ffn_bwd_dwgu(da, du, x2, fs, f"ffn2_bwd_dwgu_{l}"), f"ffn2_bwd_dwd_{l}")])
        dmix, dxp, dg2, db2, g_out, *got = _wout_bwd(dx2, z2, yab, yc, w_out_l, lnp(ln_g_all, l, 1), alpha,
                                                     f"wout_bwd_{l}", job=_PairExchange(g2))
        p2 = pair_add(g2, got, f"ffn2_{l}")
        dq, dk, dv, dtiles, *crossed = _attn_bwd(proj[1], bias, yc, dmix, lse, f"attn_bwd_{l}",
                                                 job=_ChipExchange(above_pairs) if above else None)
        if above:
            reduced[l + 1] += crossed
        dloc, dpw, dsc, dcw = _local_bwd(proj[0], dmix, pool_bd[l], pool_scale[l].reshape(1, D_POOL), conv_all[l],
                                         f"local_bwd_{l}")
        dx1, g_in = _win_bwd(dxp, dloc, dq, dk, dv, x1, w_in_l, f"win_bwd_{l}")
        dx0, da, du, df, dg1, db1, *crossed = _ffn_bwd_dx(dx1, z1, a1, u1, w_ffn1, lnp(ln_g_all, l, 0), alpha,
                                                          f"ffn1_bwd_dx_{l}", job=_ChipExchange(p2))
        reduced[l] = list(crossed)
        g1 = _ffn_bwd_dwd(h1, df, _ffn_bwd_dwgu(da, du, x0, fs, f"ffn1_bwd_dwgu_{l}"), f"ffn1_bwd_dwd_{l}")
        above = flat([g_out, g_in, g1])
        drpb = _rpb_finish(dtiles, f"rpb_finish_{l}")
        dpool = jnp.stack([dpw[g * POOL_GROUP:(g + 1) * POOL_GROUP, g * POOL_GROUP:(g + 1) * POOL_GROUP]
                           for g in range(len(POOL_WINDOWS))])
        small_grads[l] = (jnp.concatenate([dg1, dg2, dg3]), jnp.concatenate([db1, db2, db3]), dcw[0:3], dpool, dsc[0], drpb)
        dh = dx0
    grad_x = dh.reshape(x.shape)

    last_pairs = pair_add(above, _exchange_alone(_PairExchange(above), "grads_pair_exchange_last"), "mix_0")
    reduced[0] += _exchange_alone(_ChipExchange(last_pairs), "grads_chip_exchange_last")
    sums = [[_sum_blocks(q, f"grads_chip_sum_{l}_{i}") for i, q in enumerate(reduced[l])] for l in range(n_l)]
    r_ffn2, r_out, r_in, r_ffn1 = [jnp.stack([sums[l][i] for l in range(n_l)]) for i in range(4)]
    r_ffn1, r_ffn2 = r_ffn1.reshape(n_l, 3, fs, d), r_ffn2.reshape(n_l, 3, fs, d)
    row_grads = {"ffn1_w_gate": r_ffn1[:, 0], "ffn1_w_up": r_ffn1[:, 1], "ffn2_w_gate": r_ffn2[:, 0],
                 "ffn2_w_up": r_ffn2[:, 1], "w_in": r_in}
    grads = {"ffn1_w_down": r_ffn1[:, 2], "ffn2_w_down": r_ffn2[:, 2], "w_out": r_out}
    grads.update({n: tr(g) for n, g in row_grads.items()})

    stack = lambda k: jnp.stack([small_grads[l][k] for l in range(n_l)])
    small_shapes = [(n_l, 3, d), (n_l, 3, d), (n_l, 3, D_CONV), pool_w.shape, pool_scale.shape, rpb.shape]
    (small_all,) = _exchange_alone(_Gather([_pack([stack(k) for k in range(6)])]), "gather_small_grads")
    small_sum = _sum_blocks(small_all, "small_grads_sum")
    g_ln_g, g_ln_b, g_conv, g_pool_w, g_pool_scale, g_rpb = _unpack(small_sum, small_shapes)
    own = lambda t, width: lax.dynamic_slice_in_dim(t, me * width, width, axis=2)
    grads.update({"ln_g": own(g_ln_g, ln_w), "ln_b": own(g_ln_b, ln_w), "conv_w": own(g_conv, cw_w),
                  "pool_w": g_pool_w, "pool_scale": g_pool_scale, "rpb": g_rpb})

    weights = dict(ffn1_w_gate=ffn1_w_gate, ffn1_w_up=ffn1_w_up, ffn1_w_down=ffn1_w_down, ffn2_w_gate=ffn2_w_gate,
                   ffn2_w_up=ffn2_w_up, ffn2_w_down=ffn2_w_down, w_in=w_in, pool_w=pool_w, pool_scale=pool_scale,
                   conv_w=conv_w, rpb=rpb, w_out=w_out, ln_g=ln_g, ln_b=ln_b)
    m_in = dict(ffn1_w_gate=m_ffn1_w_gate, ffn1_w_up=m_ffn1_w_up, ffn1_w_down=m_ffn1_w_down, ffn2_w_gate=m_ffn2_w_gate,
                ffn2_w_up=m_ffn2_w_up, ffn2_w_down=m_ffn2_w_down, w_in=m_w_in, pool_w=m_pool_w, pool_scale=m_pool_scale,
                conv_w=m_conv_w, rpb=m_rpb, w_out=m_w_out, ln_g=m_ln_g, ln_b=m_ln_b)
    v_in = dict(ffn1_w_gate=v_ffn1_w_gate, ffn1_w_up=v_ffn1_w_up, ffn1_w_down=v_ffn1_w_down, ffn2_w_gate=v_ffn2_w_gate,
                ffn2_w_up=v_ffn2_w_up, ffn2_w_down=v_ffn2_w_down, w_in=v_w_in, pool_w=v_pool_w, pool_scale=v_pool_scale,
                conv_w=v_conv_w, rpb=v_rpb, w_out=v_w_out, ln_g=v_ln_g, ln_b=v_ln_b)
    names = list(weights)
    large = ["ffn1_w_gate", "ffn1_w_up", "ffn1_w_down", "ffn2_w_gate", "ffn2_w_up", "ffn2_w_down", "w_in", "w_out"]
    tiny = [n for n in names if n not in large]
    delta, new_m, new_v = {}, {}, {}
    for n in large:
        if n in row_grads:
            out = _adamw_nd(tr(weights[n]), row_grads[n], tr(m_in[n]), tr(v_in[n]), f"adamw_{n}")
            delta[n], new_m[n], new_v[n] = (tr(t) for t in out)
        else:
            delta[n], new_m[n], new_v[n] = _adamw_nd(weights[n], grads[n], m_in[n], v_in[n], f"adamw_{n}")
    packed = [_pack([t[n] for n in tiny]) for t in (weights, grads, m_in, v_in)]
    tiny_out = _adamw(*packed, "adamw_small")
    tiny_shapes = [weights[n].shape for n in tiny]
    for res, t in zip((delta, new_m, new_v), tiny_out):
        res.update(dict(zip(tiny, _unpack(t, tiny_shapes))))

    return (loss, grad_x, *[grads[n] for n in names], *[delta[n] for n in names],
            *[new_m[n] for n in names], *[new_v[n] for n in names])
```

```python
import functools

import numpy as np
import jax
import jax.numpy as jnp
from jax import lax
from jax.experimental import pallas as pl
from jax.experimental.pallas import tpu as pltpu

F32, BF16 = jnp.float32, jnp.bfloat16
MESH = pl.DeviceIdType.MESH
N_DEV = 8
MESH_AXES = ("x", "y", "c")

LN_EPS = 1e-5
NEG_INF = -1e30
D_POOL = 256
POOL_WINDOWS = (2, 4, 8, 16)
POOL_GROUP = 64
D_CONV = 256
NA_HEADS = 8
NA_HEAD_DIM = 64
D_NA = NA_HEADS * NA_HEAD_DIM
GRID_W = 64
NA_ROWS = 8
NA_COLS = 16
D_LOC = D_POOL + 3 * D_CONV
D_MIX = D_POOL + D_CONV + D_NA
ADAM_LR, ADAM_B1, ADAM_B2, ADAM_EPS, ADAM_WD, ADAM_STEP = 0.001, 0.9, 0.999, 1e-08, 0.01, 10

VMEM_LIMIT_BYTES = 56 * 1024 * 1024
LANES = 128
BF16_ROWS = 16
HALO = 16
Q_ROWS = 8
K_ROWS = 16
Q_TOK = Q_ROWS * GRID_W
K_TOK = K_ROWS * GRID_W
K_BLK = 4 * GRID_W
HEAD_PAIR = 2 * NA_HEAD_DIM
FFN_CHUNK_DEVS = 4
FFN_TOKENS = 256
DW_TOKENS = 1024
MIX_TOKENS = 512
LOSS_TOKENS = 1024
PASS_ON_AT = 7 / 8

NT = (((1,), (1,)), ((), ()))
TN = (((0,), (0,)), ((), ()))


def _dot(a, b):
    return jnp.dot(a, b, preferred_element_type=F32)


def _dot_nt(a, b):
    return lax.dot_general(a, b, NT, preferred_element_type=F32)


def _dot_tn(a, b):
    return lax.dot_general(a, b, TN, preferred_element_type=F32)


def _params():
    return pltpu.CompilerParams(vmem_limit_bytes=VMEM_LIMIT_BYTES)


def _row_tile(rows, pref, mult=BF16_ROWS):
    t = min(rows, pref)
    t -= t % mult
    while t > mult and rows % t:
        t -= mult
    assert t > 0 and rows % t == 0, (rows, pref)
    return t


def _mesh_pos():
    return tuple(lax.axis_index(a) for a in MESH_AXES)


def _any_spec():
    return pl.BlockSpec(memory_space=pl.ANY)


class _Gather:
    def __init__(self, shards):
        self.arrays = list(shards)
        n = len(shards)
        self.out_shape = [jax.ShapeDtypeStruct((N_DEV,) + s.shape, s.dtype) for s in shards]
        self.scratch = [pltpu.SemaphoreType.DMA((n, 7)), pltpu.SemaphoreType.DMA((n, 7)), pltpu.SemaphoreType.DMA((n,))]

    def phases(self, ins, outs, sems):
        n = len(ins)
        send_sems, recv_sems, local_sems = sems
        x, y, c = _mesh_pos()
        me, sibling = (x, y, c), (x, y, 1 - c)
        chips = [(1 - x, y), (x, 1 - y), (1 - x, 1 - y)]

        def copy(a, k, block, to, src=None):
            dst = outs[a].at[4 * block[0] + 2 * block[1] + block[2]]
            return pltpu.make_async_remote_copy(
                src_ref=dst if src is None else src, dst_ref=dst,
                send_sem=send_sems.at[a, k], recv_sem=recv_sems.at[a, k],
                device_id=to, device_id_type=MESH)

        def mine():
            return [pltpu.make_async_copy(ins[a], outs[a].at[4 * x + 2 * y + c], local_sems.at[a]) for a in range(n)]

        def first():
            return [cp for a in range(n) for cp in
                    [copy(a, 0, me, sibling, src=ins[a])]
                    + [copy(a, 1 + j, me, (*chip, c), src=ins[a]) for j, chip in enumerate(chips)]]

        def passed():
            return [copy(a, 4 + j, (*chip, c), sibling) for j, chip in enumerate(chips) for a in range(n)]

        def start():
            for cp in mine() + first():
                cp.start()

        def middle():
            for j, chip in enumerate(chips):
                for a in range(n):
                    copy(a, 1 + j, (*chip, c), me).wait_recv()
            for cp in passed():
                cp.start()

        def finish():
            for a in range(n):
                copy(a, 0, sibling, me).wait_recv()
                for j, chip in enumerate(chips):
                    copy(a, 4 + j, (*chip, 1 - c), me).wait_recv()
            for cp in first() + passed():
                cp.wait_send()
            for cp in mine():
                cp.wait()

        return start, middle, finish


class _ChipExchange:
    def __init__(self, parts):
        self.arrays = list(parts)
        n = len(parts)
        self.out_shape = [jax.ShapeDtypeStruct(s.shape, s.dtype) for s in parts]
        self.scratch = [pltpu.SemaphoreType.DMA((n, 3)), pltpu.SemaphoreType.DMA((n, 3)), pltpu.SemaphoreType.DMA((n,))]

    def phases(self, ins, outs, sems):
        n = len(ins)
        send_sems, recv_sems, local_sems = sems
        x, y, c = _mesh_pos()
        my_chip = 2 * x + y
        chips = [(1 - x, y), (x, 1 - y), (1 - x, 1 - y)]

        def own():
            return [pltpu.make_async_copy(ins[a].at[my_chip], outs[a].at[my_chip], local_sems.at[a]) for a in range(n)]

        def copy(a, k, src_chip, dst_chip, to):
            return pltpu.make_async_remote_copy(
                src_ref=ins[a].at[src_chip], dst_ref=outs[a].at[dst_chip],
                send_sem=send_sems.at[a, k], recv_sem=recv_sems.at[a, k],
                device_id=to, device_id_type=MESH)

        def sends():
            return [copy(a, k, 2 * px + py, my_chip, (px, py, c)) for a in range(n) for k, (px, py) in enumerate(chips)]

        def start():
            for cp in own() + sends():
                cp.start()

        def finish():
            for cp in sends():
                cp.wait_send()
            for a in range(n):
                for k, (px, py) in enumerate(chips):
                    copy(a, k, my_chip, 2 * px + py, (px, py, c)).wait_recv()
            for cp in own():
                cp.wait()

        return start, None, finish


def _exchange_alone(job, name):
    n = len(job.arrays)

    def body(*refs):
        for phase in job.phases(refs[:n], refs[n:2 * n], refs[2 * n:]):
            if phase is not None:
                phase()

    return pl.pallas_call(
        body, name=name, out_shape=job.out_shape,
        in_specs=[_any_spec()] * n, out_specs=[_any_spec()] * n, scratch_shapes=job.scratch,
    )(*job.arrays)


def _riding_call(body, job, n_in, n_out, n_steps, step, **kw):
    if job is None:
        return pl.pallas_call(body, **kw)
    n_job, n_sem = len(job.arrays), len(job.scratch)
    kw = dict(kw, in_specs=list(kw["in_specs"]) + [_any_spec()] * n_job,
              out_specs=list(kw["out_specs"]) + [_any_spec()] * n_job,
              out_shape=list(kw["out_shape"]) + job.out_shape,
              scratch_shapes=list(kw.get("scratch_shapes", ())) + job.scratch)

    def riding(*refs):
        ins, job_ins = refs[:n_in], refs[n_in:n_in + n_job]
        outs = refs[n_in + n_job:n_in + n_job + n_out]
        job_outs = refs[n_in + n_job + n_out:n_in + 2 * n_job + n_out]
        scratch = refs[n_in + 2 * n_job + n_out:]
        start, middle, finish = job.phases(job_ins, job_outs, scratch[len(scratch) - n_sem:])
        now = step()
        pl.when(now == 0)(start)
        if middle is not None:
            assert n_steps >= 3
            pl.when(now == int(PASS_ON_AT * n_steps) - 1)(middle)
        body(*ins, *outs, *scratch[:len(scratch) - n_sem])
        pl.when(now == n_steps - 1)(finish)

    call = pl.pallas_call(riding, **kw)
    return lambda *args: call(*args, *job.arrays)


class _PairExchange:
    def __init__(self, slabs):
        self.arrays = list(slabs)
        n = len(slabs)
        self.out_shape = [jax.ShapeDtypeStruct((4,) + s.shape[1:], s.dtype) for s in slabs]
        self.scratch = [pltpu.SemaphoreType.DMA((n, 4)), pltpu.SemaphoreType.DMA((n, 4))]

    def phases(self, ins, outs, sems):
        n = len(ins)
        send_sems, recv_sems = sems
        x, y, c = _mesh_pos()

        def copies():
            return [pltpu.make_async_remote_copy(
                src_ref=ins[a].at[2 * j + 1 - c], dst_ref=outs[a].at[j],
                send_sem=send_sems.at[a, j], recv_sem=recv_sems.at[a, j],
                device_id=(x, y, 1 - c), device_id_type=MESH) for a in range(n) for j in range(4)]

        def start():
            for cp in copies():
                cp.start()

        def finish():
            for cp in copies():
                cp.wait_send()
            for cp in copies():
                cp.wait_recv()

        return start, None, finish


def _pair_add(slab, got, core, name):
    _, rows, d = slab.shape
    tr = _row_tile(rows, 1024)

    def body(core_ref, mine_ref, got_ref, out_ref):
        out_ref[...] = (mine_ref[...].astype(F32) + got_ref[...].astype(F32)).astype(out_ref.dtype)

    grid_spec = pltpu.PrefetchScalarGridSpec(
        num_scalar_prefetch=1, grid=(4, rows // tr),
        in_specs=[pl.BlockSpec((1, tr, d), lambda j, r, core_ref: (2 * j + core_ref[0], r, 0)),
                  pl.BlockSpec((1, tr, d), lambda j, r, core_ref: (j, r, 0))],
        out_specs=pl.BlockSpec((1, tr, d), lambda j, r, core_ref: (j, r, 0)))
    return pl.pallas_call(body, name=name, grid_spec=grid_spec,
                          out_shape=jax.ShapeDtypeStruct((4, rows, d), slab.dtype),
                          compiler_params=_params())(core, slab, got)


def _sum_blocks(parts, name):
    k, rows, d = parts.shape
    tr = _row_tile(rows, 512, BF16_ROWS if parts.dtype == BF16 else 8)

    def body(in_ref, out_ref):
        acc = in_ref[0].astype(F32)
        for j in range(1, k):
            acc = acc + in_ref[j].astype(F32)
        out_ref[...] = acc

    return pl.pallas_call(
        body, name=name, grid=(rows // tr,),
        in_specs=[pl.BlockSpec((k, tr, d), lambda r: (0, r, 0))],
        out_specs=pl.BlockSpec((tr, d), lambda r: (r, 0)),
        out_shape=jax.ShapeDtypeStruct((rows, d), F32), compiler_params=_params())(parts)


def _ln_stats(z):
    mu = jnp.mean(z, axis=-1, keepdims=True)
    zc = z - mu
    var = jnp.mean(zc * zc, axis=-1, keepdims=True)
    rstd = lax.rsqrt(var + LN_EPS)
    return zc * rstd, rstd


def _ln_bwd(dy, zhat, rstd, g):
    dyg = dy * g
    m1 = jnp.mean(dyg, axis=-1, keepdims=True)
    m2 = jnp.mean(dyg * zhat, axis=-1, keepdims=True)
    dz = rstd * (dyg - m1 - zhat * m2)
    return dz, jnp.sum(dy * zhat, axis=0, keepdims=True), jnp.sum(dy, axis=0, keepdims=True)


def _accumulate(ref, value, first):
    @pl.when(first)
    def _():
        ref[...] = value

    @pl.when(jnp.logical_not(first))
    def _():
        ref[...] += value


def _add_matmul(acc_ref, first, matmul):
    @pl.when(first)
    def _():
        acc_ref[...] = jnp.zeros_like(acc_ref)

    acc_ref[...] += matmul()


def _ffn_weight_specs(fs, d):
    def spec(row):
        return pl.BlockSpec((N_DEV, 1, fs, d), lambda i: (0, row, 0, 0), pipeline_mode=pl.Buffered(1))
    return [spec(0), spec(1), spec(2)]


def _ffn_fwd(x, w, ln_g, ln_b, alpha, name, job=None, mixer=None):
    s, d = x.shape
    fs = w.shape[2]
    f = N_DEV * fs
    tm = min(s, FFN_TOKENS)
    n_pre = 5 if mixer else 0

    def body(x_ref, *refs):
        xv = x_ref[...]
        if mixer:
            yab_ref, yc_ref, wo_ref, g0_ref, b0_ref = refs[:n_pre]
            zhat0_ref, rstd0_ref, x_out_ref = refs[len(refs) - 3:]
            mix = jnp.concatenate([yab_ref[...], yc_ref[...]], axis=1).astype(BF16)
            zhat0, rstd0 = _ln_stats(alpha * xv + _dot(mix, wo_ref[...].reshape(D_MIX, d)))
            zhat0_ref[...] = zhat0
            rstd0_ref[...] = rstd0
            xv = zhat0 * g0_ref[...] + b0_ref[...]
            x_out_ref[...] = xv
        wg_ref, wu_ref, wd_ref, g_ref, b_ref, a_ref, u_ref, h_ref, zhat_ref, rstd_ref, y_ref = refs[n_pre:n_pre + 11]
        xb = xv.astype(BF16)
        a = _dot_nt(xb, wg_ref[...].reshape(f, d))
        u = _dot_nt(xb, wu_ref[...].reshape(f, d))
        a_ref[...] = a.astype(BF16)
        u_ref[...] = u.astype(BF16)
        h = ((a * jax.nn.sigmoid(a)) * u).astype(BF16)
        h_ref[...] = h
        z = alpha * xv + 0.5 * _dot(h, wd_ref[...].reshape(f, d))
        zhat, rstd = _ln_stats(z)
        zhat_ref[...] = zhat
        rstd_ref[...] = rstd
        y_ref[...] = zhat * g_ref[...] + b_ref[...]

    row = pl.BlockSpec((tm, d), lambda i: (i, 0))
    col = pl.BlockSpec((tm, 1), lambda i: (i, 0))
    vec = pl.BlockSpec((1, d), lambda i: (0, 0))
    hid = pl.BlockSpec((tm, f), lambda i: (i, 0))
    half = pl.BlockSpec((tm, D_MIX // 2), lambda i: (i, 0))
    ln_out = [row, col, row]
    ln_shape = [jax.ShapeDtypeStruct((s, d), F32), jax.ShapeDtypeStruct((s, 1), F32), jax.ShapeDtypeStruct((s, d), F32)]
    pre_specs = [half, half, _whole(mixer[2]), vec, vec] if mixer else []
    call = _riding_call(
        body, job, 6 + n_pre, 6 + (3 if mixer else 0), s // tm, lambda: pl.program_id(0),
        name=name, grid=(s // tm,),
        in_specs=[row] + pre_specs + _ffn_weight_specs(fs, d) + [vec, vec],
        out_specs=[hid, hid, hid] + ln_out + (ln_out if mixer else []),
        out_shape=[jax.ShapeDtypeStruct((s, f), BF16)] * 3 + ln_shape + (ln_shape if mixer else []),
        compiler_params=_params())
    a, u, h, zhat, rstd, y, *rest = call(x, *(mixer or ()), w, w, w, ln_g, ln_b)
    if mixer:
        rest = [(rest[0], rest[1]), rest[2]] + rest[3:]
    return [a, u, h, (zhat, rstd), y] + rest


def _ffn_bwd_dx(dy, z, a, u, w, ln_g, alpha, name, job=None):
    s, d = dy.shape
    fs = w.shape[2]
    f = N_DEV * fs
    tm = min(s, FFN_TOKENS)

    def body(dy_ref, zhat_ref, rstd_ref, a_ref, u_ref, wg_ref, wu_ref, wd_ref, g_ref,
             dx_ref, da_ref, du_ref, df_ref, dg_ref, db_ref):
        i = pl.program_id(0)
        dz, dg, db = _ln_bwd(dy_ref[...], zhat_ref[...], rstd_ref[...], g_ref[...])
        _accumulate(dg_ref, dg, i == 0)
        _accumulate(db_ref, db, i == 0)
        df = (0.5 * dz).astype(BF16)
        df_ref[...] = df
        av = a_ref[...].astype(F32)
        uv = u_ref[...].astype(F32)
        sg = jax.nn.sigmoid(av)
        dh = _dot_nt(df, wd_ref[...].reshape(f, d))
        du = (dh * (av * sg)).astype(BF16)
        da = (dh * uv * (sg * (1.0 + av * (1.0 - sg)))).astype(BF16)
        da_ref[...] = da
        du_ref[...] = du
        dx_ref[...] = alpha * dz + _dot(da, wg_ref[...].reshape(f, d)) + _dot(du, wu_ref[...].reshape(f, d))

    row = pl.BlockSpec((tm, d), lambda i: (i, 0))
    col = pl.BlockSpec((tm, 1), lambda i: (i, 0))
    vec = pl.BlockSpec((1, d), lambda i: (0, 0))
    hid = pl.BlockSpec((tm, f), lambda i: (i, 0))
    call = _riding_call(
        body, job, 9, 6, s // tm, lambda: pl.program_id(0),
        name=name, grid=(s // tm,),
        in_specs=[row, row, col, hid, hid] + _ffn_weight_specs(fs, d) + [vec],
        out_specs=[row, hid, hid, row, vec, vec],
        out_shape=[jax.ShapeDtypeStruct((s, d), F32)] + [jax.ShapeDtypeStruct((s, f), BF16)] * 2
                  + [jax.ShapeDtypeStruct((s, d), BF16)] + [jax.ShapeDtypeStruct((1, d), F32)] * 2,
        compiler_params=_params())
    return call(dy, *z, a, u, w, w, w, ln_g)


def _ffn_bwd_dwgu(da, du, x, fs, name):
    s, d = x.shape
    tf = FFN_CHUNK_DEVS * fs
    n_c = N_DEV // FFN_CHUNK_DEVS
    tk = min(s, DW_TOKENS)
    n_k = s // tk

    def body(da_ref, du_ref, x_ref, out_ref, accg_s, accu_s):
        k = pl.program_id(1)
        xb = x_ref[...].astype(BF16)
        _add_matmul(accg_s, k == 0, lambda: _dot_tn(da_ref[...], xb))
        _add_matmul(accu_s, k == 0, lambda: _dot_tn(du_ref[...], xb))

        @pl.when(k == n_k - 1)
        def _():
            out_ref[:, 0] = accg_s[...].astype(BF16).reshape(FFN_CHUNK_DEVS, fs, d)
            out_ref[:, 1] = accu_s[...].astype(BF16).reshape(FFN_CHUNK_DEVS, fs, d)

    hid = pl.BlockSpec((tk, tf), lambda c, k: (k, c))
    return pl.pallas_call(
        body, name=name, grid=(n_c, n_k),
        in_specs=[hid, hid, pl.BlockSpec((tk, d), lambda c, k: (k, 0))],
        out_specs=pl.BlockSpec((FFN_CHUNK_DEVS, 2, fs, d), lambda c, k: (c, 0, 0, 0), pipeline_mode=pl.Buffered(1)),
        out_shape=jax.ShapeDtypeStruct((N_DEV, 3, fs, d), BF16),
        scratch_shapes=[pltpu.VMEM((tf, d), F32), pltpu.VMEM((tf, d), F32)],
        compiler_params=_params())(da, du, x)


def _ffn_bwd_dwd(h, df, blocks, name):
    s, d = df.shape
    fs = blocks.shape[2]
    tf = FFN_CHUNK_DEVS * fs
    n_c = N_DEV // FFN_CHUNK_DEVS
    tk = min(s, 2 * DW_TOKENS)
    n_k = s // tk

    def body(h_ref, df_ref, blocks_ref, out_ref, acc_s):
        k = pl.program_id(1)
        _add_matmul(acc_s, k == 0, lambda: _dot_tn(h_ref[...], df_ref[...]))

        @pl.when(k == n_k - 1)
        def _():
            out_ref[:, 0] = acc_s[...].astype(BF16).reshape(FFN_CHUNK_DEVS, fs, d)

    return pl.pallas_call(
        body, name=name, grid=(n_c, n_k),
        in_specs=[pl.BlockSpec((tk, tf), lambda c, k: (k, c)), pl.BlockSpec((tk, d), lambda c, k: (k, 0)), _any_spec()],
        out_specs=pl.BlockSpec((FFN_CHUNK_DEVS, 1, fs, d), lambda c, k: (c, 2, 0, 0), pipeline_mode=pl.Buffered(1)),
        out_shape=jax.ShapeDtypeStruct(blocks.shape, BF16), input_output_aliases={2: 0},
        scratch_shapes=[pltpu.VMEM((tf, d), F32)],
        compiler_params=_params())(h, df, blocks)


def _whole(arr):
    return pl.BlockSpec(arr.shape, lambda i: (0,) * arr.ndim, pipeline_mode=pl.Buffered(1))


def _win_fwd(x, w_in, name):
    s, d = x.shape
    d_in = N_DEV * w_in.shape[1]
    tm = min(s, MIX_TOKENS)
    scale = NA_HEAD_DIM ** -0.5
    assert d_in == D_LOC + 3 * D_NA and scale == 0.125

    def body(x_ref, w_ref, loc_ref, qkv_ref):
        proj = _dot_nt(x_ref[...].astype(BF16), w_ref[...].reshape(d_in, d))
        loc_ref[...] = proj[:, :D_LOC]
        qkv_ref[:, :D_NA] = (proj[:, D_LOC:D_LOC + D_NA] * scale).astype(BF16)
        qkv_ref[:, D_NA:] = proj[:, D_LOC + D_NA:].astype(BF16)

    return pl.pallas_call(
        body, name=name, grid=(s // tm,),
        in_specs=[pl.BlockSpec((tm, d), lambda i: (i, 0)), _whole(w_in)],
        out_specs=[pl.BlockSpec((tm, D_LOC), lambda i: (i, 0)), pl.BlockSpec((tm, 3 * D_NA), lambda i: (i, 0))],
        out_shape=[jax.ShapeDtypeStruct((s, D_LOC), F32), jax.ShapeDtypeStruct((s, 3 * D_NA), BF16)],
        compiler_params=_params())(x, w_in)


def _wout_bwd(dy, z, yab, yc, w_out, ln_g, alpha, name, job=None):
    s, d = dy.shape
    rs = w_out.shape[1]
    tm = min(s, MIX_TOKENS)
    n_i = s // tm

    def body(dy_ref, zhat_ref, rstd_ref, yab_ref, yc_ref, w_ref, g_ref, dmix_ref, dxp_ref, dg_ref, db_ref, out_ref, acc_s):
        i = pl.program_id(0)
        dz, dg, db = _ln_bwd(dy_ref[...], zhat_ref[...], rstd_ref[...], g_ref[...])
        _accumulate(dg_ref, dg, i == 0)
        _accumulate(db_ref, db, i == 0)
        dxp_ref[...] = alpha * dz
        dzb = dz.astype(BF16)
        dmix_ref[...] = _dot_nt(dzb, w_ref[...].reshape(D_MIX, d))
        mix = jnp.concatenate([yab_ref[...], yc_ref[...]], axis=1).astype(BF16)
        _add_matmul(acc_s, i == 0, lambda: _dot_tn(mix, dzb))

        @pl.when(i == n_i - 1)
        def _():
            out_ref[...] = acc_s[...].astype(BF16).reshape(N_DEV, rs, d)

    row = pl.BlockSpec((tm, d), lambda i: (i, 0))
    half = pl.BlockSpec((tm, D_MIX // 2), lambda i: (i, 0))
    vec = pl.BlockSpec((1, d), lambda i: (0, 0))
    call = _riding_call(
        body, job, 7, 5, n_i, lambda: pl.program_id(0),
        name=name, grid=(n_i,),
        in_specs=[row, row, pl.BlockSpec((tm, 1), lambda i: (i, 0)), half, half, _whole(w_out), vec],
        out_specs=[pl.BlockSpec((tm, D_MIX), lambda i: (i, 0)), row, vec, vec, _whole(w_out)],
        out_shape=[jax.ShapeDtypeStruct((s, D_MIX), F32), jax.ShapeDtypeStruct((s, d), F32),
                   jax.ShapeDtypeStruct((1, d), F32), jax.ShapeDtypeStruct((1, d), F32),
                   jax.ShapeDtypeStruct(w_out.shape, BF16)],
        scratch_shapes=[pltpu.VMEM((D_MIX, d), F32)],
        compiler_params=_params())
    return call(dy, *z, yab, yc, w_out, ln_g)


def _win_bwd(dxp, dloc, dq, dk, dv, x, w_in, name):
    s, d = x.shape
    rs = w_in.shape[1]
    d_in = N_DEV * rs
    tm = min(s, MIX_TOKENS)
    n_i = s // tm

    def body(dxp_ref, dloc_ref, dq_ref, dk_ref, dv_ref, x_ref, w_ref, dx_ref, out_ref, acc_s):
        i = pl.program_id(0)
        dp = jnp.concatenate([dloc_ref[...], dq_ref[...], dk_ref[...].astype(BF16), dv_ref[...].astype(BF16)], axis=1)
        dx_ref[...] = dxp_ref[...] + _dot(dp, w_ref[...].reshape(d_in, d))
        _add_matmul(acc_s, i == 0, lambda: _dot_tn(dp, x_ref[...].astype(BF16)))

        @pl.when(i == n_i - 1)
        def _():
            out_ref[...] = acc_s[...].astype(BF16).reshape(N_DEV, rs, d)

    row = pl.BlockSpec((tm, d), lambda i: (i, 0))
    na = pl.BlockSpec((tm, D_NA), lambda i: (i, 0))
    return pl.pallas_call(
        body, name=name, grid=(n_i,),
        in_specs=[row, pl.BlockSpec((tm, D_LOC), lambda i: (i, 0)), na, na, na, row, _whole(w_in)],
        out_specs=[row, _whole(w_in)],
        out_shape=[jax.ShapeDtypeStruct((s, d), F32), jax.ShapeDtypeStruct(w_in.shape, BF16)],
        scratch_shapes=[pltpu.VMEM((d_in, d), F32)],
        compiler_params=_params())(dxp, dloc, dq, dk, dv, x, w_in)


def _shift_rows(v, k):
    n = v.shape[0]
    return pltpu.roll(v, k % n, 0)


def _halo_specs(tm, s, width, col):
    per = tm // HALO
    last = s // HALO - 1
    return [pl.BlockSpec((HALO, width), lambda i: (jnp.maximum(i * per - 1, 0), col)),
            pl.BlockSpec((tm, width), lambda i: (i, col)),
            pl.BlockSpec((HALO, width), lambda i: (jnp.minimum((i + 1) * per, last), col))]


def _token_index(i, tm):
    return i * tm - HALO + lax.broadcasted_iota(jnp.int32, (tm + 2 * HALO, 1), 0)


def _pool_lane_tables():
    lane = lax.broadcasted_iota(jnp.int32, (1, D_POOL), 1)
    group = sum((lane >= g * POOL_GROUP).astype(jnp.int32) for g in range(1, len(POOL_WINDOWS)))
    half = jnp.where(group == 0, 1, jnp.where(group == 1, 2, jnp.where(group == 2, 4, 8)))
    return group, half


def _window_sums(v, group, offsets):
    s2 = v + _shift_rows(v, 1)
    s4 = s2 + _shift_rows(s2, 2)
    s8 = s4 + _shift_rows(s4, 4)
    s16 = s8 + _shift_rows(s8, 8)
    parts = [_shift_rows(p, -o) if o else p for p, o in zip((s2, s4, s8, s16), offsets)]
    return jnp.where(group == 0, parts[0], jnp.where(group == 1, parts[1], jnp.where(group == 2, parts[2], parts[3])))


def _pool_counts(tok, half, s):
    return (jnp.minimum(tok + half, s) - jnp.maximum(tok - half, 0)).astype(F32)


def _pool_forward(u, tok, s):
    group, half = _pool_lane_tables()
    sums = _window_sums(u, group, [w // 2 - 1 for w in POOL_WINDOWS])
    return sums / _pool_counts(tok, half, s) - u


def _conv_forward(zc, cw_ref):
    return cw_ref[0:1, :] * _shift_rows(zc, 1) + cw_ref[1:2, :] * zc + cw_ref[2:3, :] * _shift_rows(zc, -1)


def _local_fwd(proj, pool_bd, pool_scale, conv_w, name):
    s = proj.shape[0]
    tm = min(s, MIX_TOKENS)
    ctr = slice(HALO, HALO + tm)

    def body(prev_ref, cur_ref, next_ref, pw_ref, sc_ref, cw_ref, out_ref):
        i = pl.program_id(0)
        ext = jnp.concatenate([prev_ref[...], cur_ref[...], next_ref[...]], axis=0)
        tok = _token_index(i, tm)
        inside = (tok >= 0) & (tok < s)
        u = jnp.where(inside, ext[:, 0:D_POOL], 0.0)
        p = _pool_forward(u, tok, s)[ctr]
        ya = _dot(p.astype(BF16), pw_ref[...]) * sc_ref[...]
        gb = ext[:, D_POOL:D_POOL + D_CONV]
        zc = jnp.where(inside, ext[:, D_POOL + D_CONV:D_POOL + 2 * D_CONV] * ext[:, D_POOL + 2 * D_CONV:D_LOC], 0.0)
        yb = (gb * _conv_forward(zc, cw_ref))[ctr]
        out_ref[...] = jnp.concatenate([ya, yb], axis=1)

    return pl.pallas_call(
        body, name=name, grid=(s // tm,),
        in_specs=_halo_specs(tm, s, D_LOC, 0) + [
            pl.BlockSpec((D_POOL, D_POOL), lambda i: (0, 0)), pl.BlockSpec((1, D_POOL), lambda i: (0, 0)),
            pl.BlockSpec((3, D_CONV), lambda i: (0, 0))],
        out_specs=pl.BlockSpec((tm, D_POOL + D_CONV), lambda i: (i, 0)),
        out_shape=jax.ShapeDtypeStruct((s, D_POOL + D_CONV), F32),
        compiler_params=_params())(proj, proj, proj, pool_bd, pool_scale, conv_w)


def _local_bwd(proj, dmix, pool_bd, pool_scale, conv_w, name):
    s = proj.shape[0]
    tm = min(s, MIX_TOKENS)
    ctr = slice(HALO, HALO + tm)

    def body(prev_ref, cur_ref, next_ref, dprev_ref, dcur_ref, dnext_ref, pw_ref, sc_ref, cw_ref,
             dloc_ref, dpw_ref, dsc_ref, dcw_ref):
        i = pl.program_id(0)
        first = i == 0
        ext = jnp.concatenate([prev_ref[...], cur_ref[...], next_ref[...]], axis=0)
        dext = jnp.concatenate([dprev_ref[...], dcur_ref[...], dnext_ref[...]], axis=0)
        tok = _token_index(i, tm)
        inside = (tok >= 0) & (tok < s)
        group, half = _pool_lane_tables()
        cnt = _pool_counts(tok, half, s)
        u = jnp.where(inside, ext[:, 0:D_POOL], 0.0)
        dya = jnp.where(inside, dext[:, 0:D_POOL], 0.0)
        p_c = _pool_forward(u, tok, s)[ctr].astype(BF16)
        lin = _dot(p_c, pw_ref[...])
        _accumulate(dsc_ref, jnp.sum(dya[ctr] * lin, axis=0, keepdims=True), first)
        e1 = (dya * sc_ref[...]).astype(BF16)
        _accumulate(dpw_ref, _dot_tn(p_c, e1[ctr]), first)
        dp = _dot_nt(e1, pw_ref[...])
        du = _window_sums(dp / cnt, group, [w // 2 for w in POOL_WINDOWS]) - dp
        gb = ext[:, D_POOL:D_POOL + D_CONV]
        gc = ext[:, D_POOL + D_CONV:D_POOL + 2 * D_CONV]
        hv = ext[:, D_POOL + 2 * D_CONV:D_LOC]
        zc = jnp.where(inside, gc * hv, 0.0)
        dyb = jnp.where(inside, dext[:, D_POOL:D_POOL + D_CONV], 0.0)
        dgb = dyb * _conv_forward(zc, cw_ref)
        dyc = dyb * gb
        for k in range(3):
            part = jnp.sum(dyc[ctr] * _shift_rows(zc, 1 - k)[ctr], axis=0, keepdims=True)
            _accumulate(dcw_ref.at[k:k + 1, :], part, first)
        dzc = cw_ref[0:1, :] * _shift_rows(dyc, -1) + cw_ref[1:2, :] * dyc + cw_ref[2:3, :] * _shift_rows(dyc, 1)
        dloc = jnp.concatenate([du, dgb, dzc * hv, dzc * gc], axis=1)
        dloc_ref[...] = dloc[ctr].astype(BF16)

    return pl.pallas_call(
        body, name=name, grid=(s // tm,),
        in_specs=_halo_specs(tm, s, D_LOC, 0) + _halo_specs(tm, s, D_POOL + D_CONV, 0) + [
            pl.BlockSpec((D_POOL, D_POOL), lambda i: (0, 0)), pl.BlockSpec((1, D_POOL), lambda i: (0, 0)),
            pl.BlockSpec((3, D_CONV), lambda i: (0, 0))],
        out_specs=[pl.BlockSpec((tm, D_LOC), lambda i: (i, 0)), pl.BlockSpec((D_POOL, D_POOL), lambda i: (0, 0)),
                   pl.BlockSpec((1, D_POOL), lambda i: (0, 0)), pl.BlockSpec((8, D_CONV), lambda i: (0, 0))],
        out_shape=[jax.ShapeDtypeStruct((s, D_LOC), BF16), jax.ShapeDtypeStruct((D_POOL, D_POOL), F32),
                   jax.ShapeDtypeStruct((1, D_POOL), F32), jax.ShapeDtypeStruct((8, D_CONV), F32)],
        compiler_params=_params())(proj, proj, proj, dmix, dmix, dmix, pool_bd, pool_scale, conv_w)


def _na_geometry(rows):
    n_j = rows // Q_ROWS
    dr = np.full((3, Q_ROWS, K_ROWS), 2 * NA_ROWS - 1, np.int64)
    for t, j in enumerate((0, min(1, n_j - 1), n_j - 1)):
        base = int(np.clip(Q_ROWS * j - NA_ROWS // 2, 0, rows - K_ROWS))
        for qr in range(Q_ROWS):
            r = Q_ROWS * j + qr
            start = int(np.clip(r - NA_ROWS // 2, 0, rows - NA_ROWS))
            for kr in range(K_ROWS):
                if start <= base + kr < start + NA_ROWS:
                    dr[t, qr, kr] = base + kr - r + NA_ROWS - 1
    return dr


def _na_col_tables():
    c = np.arange(GRID_W)
    start = np.clip(c - NA_COLS // 2, 0, GRID_W - NA_COLS)
    valid = (c[None, :] >= start[:, None]) & (c[None, :] < start[:, None] + NA_COLS)
    dc = np.clip(c[None, :] - c[:, None], -(NA_COLS - 1), NA_COLS - 1) + (NA_COLS - 1)
    return valid, dc


NO_ROW = 2 * NA_ROWS - 1
N_SLOT = 2 * NA_ROWS


def _na_tiles(rpb):
    valid, dc = _na_col_tables()
    onehot = jnp.asarray((dc[None] == np.arange(2 * NA_COLS - 1)[:, None, None]).astype(np.float32))
    table = jnp.einsum("hrd,dqk->hrqk", rpb, onehot, precision=lax.Precision.HIGHEST)
    table = jnp.where(jnp.asarray(valid)[None, None], table, NEG_INF)
    outside = jnp.full((NA_HEADS, 1, GRID_W, GRID_W), NEG_INF, F32)
    padded = jnp.concatenate([outside, table, outside], axis=1)
    pairs = jnp.concatenate([padded[:, :N_SLOT], padded[:, 1:]], axis=-1)
    return jnp.concatenate([pairs, jnp.full((NA_HEADS, 1, GRID_W, 2 * GRID_W), NEG_INF, F32)], axis=1)


G_ROWS = 2
N_GRP = Q_ROWS // G_ROWS
G_TOK = G_ROWS * GRID_W
GK_ROWS = NA_ROWS + G_ROWS
GK_TOK = GK_ROWS * GRID_W
STACK_TOK = N_GRP * 2 * G_TOK


def _na_group_tables(rows):
    dr = _na_geometry(rows)
    koff = np.zeros((3, N_GRP), np.int64)
    slot = np.zeros((3, N_GRP, G_ROWS, GK_ROWS // 2), np.int64)
    even_in, odd_in = np.zeros_like(slot), np.zeros_like(slot)
    for t in range(3):
        for g in range(N_GRP):
            qrs = range(G_ROWS * g, G_ROWS * (g + 1))
            inside = [kr for kr in range(K_ROWS) if any(dr[t, qr, kr] != NO_ROW for qr in qrs)]
            lo, hi = min(inside), max(inside) + 1
            off = min(lo - lo % 2, K_ROWS - GK_ROWS)
            assert off <= lo and hi <= off + GK_ROWS
            koff[t, g] = off
            for qq, qr in enumerate(qrs):
                for kp in range(GK_ROWS // 2):
                    even, odd = int(dr[t, qr, off + 2 * kp]), int(dr[t, qr, off + 2 * kp + 1])
                    even_in[t, g, qq, kp], odd_in[t, g, qq, kp] = even != NO_ROW, odd != NO_ROW
                    slot[t, g, qq, kp] = (N_SLOT if even == NO_ROW and odd == NO_ROW
                                          else (even if even != NO_ROW else odd - 1) + 1)
    return koff, slot, even_in, odd_in


def _by_type(block_type, per_type):
    a, b, c = (int(v) for v in per_type)
    if a == b == c:
        return a
    return jnp.where(block_type == 0, a, jnp.where(block_type == 2, c, b))


def _score_rows(g, hh):
    first = (2 * g + hh) * G_TOK
    return slice(first, first + G_TOK)


def _tile_at(g, hh, qq, kp):
    first = _score_rows(g, hh).start + qq * GRID_W
    return slice(first, first + GRID_W), slice(kp * 2 * GRID_W, (kp + 1) * 2 * GRID_W)


def _fill_bias(bias_s, tiles_ref, block_type, tables):
    _, slot, even_in, odd_in = tables
    left = lax.broadcasted_iota(jnp.int32, (1, 2 * GRID_W), 1) < GRID_W
    for hh in range(2):
        for g in range(N_GRP):
            for qq in range(G_ROWS):
                for kp in range(GK_ROWS // 2):
                    tile = tiles_ref[hh, _by_type(block_type, slot[:, g, qq, kp])]
                    tile = jnp.where(left & (_by_type(block_type, even_in[:, g, qq, kp]) == 0), NEG_INF, tile)
                    tile = jnp.where(jnp.logical_not(left) & (_by_type(block_type, odd_in[:, g, qq, kp]) == 0), NEG_INF, tile)
                    rs, cs = _tile_at(g, hh, qq, kp)
                    bias_s[rs, cs] = tile


def _group_offset(block_type, koff, g):
    off = _by_type(block_type, koff[:, g]) * GRID_W
    return off if isinstance(off, int) else pl.multiple_of(off, 2 * GRID_W)


def _na_specs(s, proj_cols):
    n_blk = s // K_BLK
    per = Q_TOK // K_BLK

    def kv_spec(col0, m):
        return pl.BlockSpec((K_BLK, HEAD_PAIR), lambda hp, j: (jnp.clip(per * j - 1, 0, n_blk - 4) + m, col0 + hp))

    q_col, k_col, v_col = (c // HEAD_PAIR for c in proj_cols)
    return ([pl.BlockSpec((Q_TOK, HEAD_PAIR), lambda hp, j: (j, q_col + hp))]
            + [kv_spec(k_col, m) for m in range(4)] + [kv_spec(v_col, m) for m in range(4)])


def _na_block_type(j, n_j):
    return jnp.where(j == 0, 0, jnp.where(j == n_j - 1, 2, 1))


def _head_masks():
    lane = lax.broadcasted_iota(jnp.int32, (1, HEAD_PAIR), 1)
    return [lane < NA_HEAD_DIM, lane >= NA_HEAD_DIM]


def _attn_fwd(qkv, tiles, name):
    s = qkv.shape[0]
    n_j = s // Q_TOK
    tables = _na_group_tables(s // GRID_W)
    koff = tables[0]

    def body(q_ref, k0, k1, k2, k3, v0, v1, v2, v3, tiles_ref, o_ref, lse_ref, bias_s, k_s, v_s, sc_s, p_s):
        j = pl.program_id(1)
        block_type = _na_block_type(j, n_j)
        pl.when((j == 0) | (j == 1) | (j == n_j - 1))(functools.partial(_fill_bias, bias_s, tiles_ref, block_type, tables))
        masks = _head_masks()
        for m, (kr, vr) in enumerate(zip((k0, k1, k2, k3), (v0, v1, v2, v3))):
            rows = slice(m * K_BLK, (m + 1) * K_BLK)
            k_s[rows, :] = kr[...]
            v_s[rows, :] = vr[...]
        q = q_ref[...]
        qh = [jnp.where(mask, q, jnp.zeros_like(q)) for mask in masks]
        offs = [_group_offset(block_type, koff, g) for g in range(N_GRP)]
        for g in range(N_GRP):
            kg = k_s[pl.ds(offs[g], GK_TOK), :]
            for hh in range(2):
                sc_s[_score_rows(g, hh), :] = _dot_nt(qh[hh][g * G_TOK:(g + 1) * G_TOK], kg)
        sc = sc_s[...] + bias_s[...]
        mx = jnp.max(sc, axis=-1, keepdims=True)
        p = jnp.exp(sc - mx)
        den = jnp.sum(p, axis=-1, keepdims=True)
        p_s[...] = p.astype(BF16)
        inv = 1.0 / den
        lse = mx + jnp.log(den)
        for g in range(N_GRP):
            rows = slice(g * G_TOK, (g + 1) * G_TOK)
            out = jnp.zeros((G_TOK, HEAD_PAIR), F32)
            for hh in range(2):
                sr = _score_rows(g, hh)
                out = out + jnp.where(masks[hh], _dot(p_s[sr, :], v_s[pl.ds(offs[g], GK_TOK), :]) * inv[sr], 0.0)
            o_ref[rows, :] = out
            lse_ref[0, rows, :] = jnp.where(masks[0], lse[_score_rows(g, 0)], lse[_score_rows(g, 1)])

    return pl.pallas_call(
        body, name=name, grid=(NA_HEADS // 2, n_j),
        in_specs=_na_specs(s, (0, D_NA, 2 * D_NA)) + [
            pl.BlockSpec((2, N_SLOT + 1, GRID_W, 2 * GRID_W), lambda hp, j: (hp, 0, 0, 0))],
        out_specs=[pl.BlockSpec((Q_TOK, HEAD_PAIR), lambda hp, j: (j, hp)),
                   pl.BlockSpec((1, Q_TOK, HEAD_PAIR), lambda hp, j: (hp, j, 0))],
        out_shape=[jax.ShapeDtypeStruct((s, D_NA), F32), jax.ShapeDtypeStruct((NA_HEADS // 2, s, HEAD_PAIR), F32)],
        scratch_shapes=[pltpu.VMEM((STACK_TOK, GK_TOK), F32), pltpu.VMEM((K_TOK, HEAD_PAIR), BF16),
                        pltpu.VMEM((K_TOK, HEAD_PAIR), BF16), pltpu.VMEM((STACK_TOK, GK_TOK), F32),
                        pltpu.VMEM((STACK_TOK, GK_TOK), BF16)],
        compiler_params=_params())(*([qkv] * 9), tiles)


def _add_tiles(dtile_ref, ds_ref, block_type, slot, has_interior):
    def tile(g, hh, qq, kp):
        rs, cs = _tile_at(g, hh, qq, kp)
        return ds_ref[rs, cs].astype(F32)

    def interior():
        for hh in range(2):
            for qq in range(G_ROWS):
                for kp in range(GK_ROWS // 2):
                    assert (slot[1, :, qq, kp] == slot[1, 0, qq, kp]).all()
                    if slot[1, 0, qq, kp] != N_SLOT:
                        dtile_ref[hh, int(slot[1, 0, qq, kp])] += sum(tile(g, hh, qq, kp) for g in range(N_GRP))

    def edge():
        for hh in range(2):
            for g in range(N_GRP):
                for qq in range(G_ROWS):
                    for kp in range(GK_ROWS // 2):
                        first, last = (0 if e == N_SLOT else int(e) for e in slot[[0, 2], g, qq, kp])
                        if (slot[[0, 2], g, qq, kp] != N_SLOT).any():
                            dtile_ref[hh, _by_type(block_type, (first, first, last))] += tile(g, hh, qq, kp)

    if has_interior:
        pl.when(block_type == 1)(interior)
    pl.when(block_type != 1)(edge)


def _attn_bwd(qkv, tiles, o, dmix, lse, name, job=None):
    s = qkv.shape[0]
    n_j = s // Q_TOK
    n_blk = s // K_BLK
    per = Q_TOK // K_BLK
    scale = NA_HEAD_DIM ** -0.5
    do_col = (D_POOL + D_CONV) // HEAD_PAIR
    tables = _na_group_tables(s // GRID_W)
    koff, slot = tables[0], tables[1]

    def body(q_ref, k0, k1, k2, k3, v0, v1, v2, v3, tiles_ref, o_ref, do_ref, lse_ref,
             dq_ref, dk_ref, dv_ref, dtile_ref, bias_s, k_s, v_s, s_s, dp_s, pb_s, dsb_s):
        j = pl.program_id(1)

        @pl.when(j == 0)
        def _():
            dk_ref[...] = jnp.zeros_like(dk_ref)
            dv_ref[...] = jnp.zeros_like(dv_ref)
            dtile_ref[...] = jnp.zeros_like(dtile_ref)

        block_type = _na_block_type(j, n_j)
        pl.when((j == 0) | (j == 1) | (j == n_j - 1))(functools.partial(_fill_bias, bias_s, tiles_ref, block_type, tables))
        base = pl.multiple_of(jnp.clip(per * j - 1, 0, n_blk - 4) * K_BLK, K_BLK)
        masks = _head_masks()
        for m, (kr, vr) in enumerate(zip((k0, k1, k2, k3), (v0, v1, v2, v3))):
            rows = slice(m * K_BLK, (m + 1) * K_BLK)
            k_s[rows, :] = kr[...]
            v_s[rows, :] = vr[...]
        q = q_ref[...]
        qh = [jnp.where(mask, q, jnp.zeros_like(q)) for mask in masks]
        lane = lax.broadcasted_iota(jnp.int32, (1, HEAD_PAIR), 1)
        offs = [_group_offset(block_type, koff, g) for g in range(N_GRP)]
        do, ov, lse = do_ref[...], o_ref[...], lse_ref[0]
        dob, lse_col, delta_col = {}, [], []
        for g in range(N_GRP):
            rows = slice(g * G_TOK, (g + 1) * G_TOK)
            kg = k_s[pl.ds(offs[g], GK_TOK), :]
            vg = v_s[pl.ds(offs[g], GK_TOK), :]
            for hh, mask in enumerate(masks):
                doh = jnp.where(mask, do[rows], 0.0)
                dob[g, hh] = doh.astype(BF16)
                lse_col.append(jnp.sum(jnp.where(lane == hh * NA_HEAD_DIM, lse[rows], 0.0), axis=-1, keepdims=True))
                delta_col.append(jnp.sum(doh * ov[rows], axis=-1, keepdims=True))
                s_s[_score_rows(g, hh), :] = _dot_nt(qh[hh][rows], kg)
                dp_s[_score_rows(g, hh), :] = _dot_nt(dob[g, hh], vg)
        p = jnp.exp(s_s[...] + bias_s[...] - jnp.concatenate(lse_col, axis=0))
        ds = p * (dp_s[...] - jnp.concatenate(delta_col, axis=0))
        pb_s[...] = p.astype(BF16)
        dsb_s[...] = ds.astype(BF16)
        _add_tiles(dtile_ref, dsb_s, block_type, slot, n_j > 2)
        for g in range(N_GRP):
            rows = slice(g * G_TOK, (g + 1) * G_TOK)
            dq = jnp.zeros((G_TOK, HEAD_PAIR), F32)
            dk = jnp.zeros((GK_TOK, HEAD_PAIR), F32)
            dv = jnp.zeros((GK_TOK, HEAD_PAIR), F32)
            for hh in range(2):
                sr = _score_rows(g, hh)
                dsb = dsb_s[sr, :]
                dq = dq + jnp.where(masks[hh], _dot(dsb, k_s[pl.ds(offs[g], GK_TOK), :]), 0.0)
                dk = dk + _dot_tn(dsb, qh[hh][rows])
                dv = dv + _dot_tn(pb_s[sr, :], dob[g, hh])
            dq_ref[rows, :] = (dq * scale).astype(BF16)
            at = pl.multiple_of(base + offs[g], 2 * GRID_W)
            dk_ref[pl.ds(at, GK_TOK), :] += dk
            dv_ref[pl.ds(at, GK_TOK), :] += dv

    pair = pl.BlockSpec((Q_TOK, HEAD_PAIR), lambda hp, j: (j, hp))
    whole = pl.BlockSpec((s, HEAD_PAIR), lambda hp, j: (0, hp))
    call = _riding_call(
        body, job, 13, 4, (NA_HEADS // 2) * n_j, lambda: pl.program_id(0) * n_j + pl.program_id(1),
        name=name, grid=(NA_HEADS // 2, n_j),
        in_specs=_na_specs(s, (0, D_NA, 2 * D_NA)) + [
            pl.BlockSpec((2, N_SLOT + 1, GRID_W, 2 * GRID_W), lambda hp, j: (hp, 0, 0, 0)),
            pair, pl.BlockSpec((Q_TOK, HEAD_PAIR), lambda hp, j: (j, do_col + hp)),
            pl.BlockSpec((1, Q_TOK, HEAD_PAIR), lambda hp, j: (hp, j, 0))],
        out_specs=[pair, whole, whole, pl.BlockSpec((2, N_SLOT, GRID_W, 2 * GRID_W), lambda hp, j: (hp, 0, 0, 0))],
        out_shape=[jax.ShapeDtypeStruct((s, D_NA), BF16), jax.ShapeDtypeStruct((s, D_NA), F32),
                   jax.ShapeDtypeStruct((s, D_NA), F32),
                   jax.ShapeDtypeStruct((NA_HEADS, N_SLOT, GRID_W, 2 * GRID_W), F32)],
        scratch_shapes=[pltpu.VMEM((STACK_TOK, GK_TOK), F32), pltpu.VMEM((K_TOK, HEAD_PAIR), BF16),
                        pltpu.VMEM((K_TOK, HEAD_PAIR), BF16),
                        pltpu.VMEM((STACK_TOK, GK_TOK), F32), pltpu.VMEM((STACK_TOK, GK_TOK), F32),
                        pltpu.VMEM((STACK_TOK, GK_TOK), BF16), pltpu.VMEM((STACK_TOK, GK_TOK), BF16)],
        compiler_params=_params())
    return call(*([qkv] * 9), tiles, o, dmix, lse)


def _rpb_finish(tiles, name):
    valid, dc = _na_col_tables()
    n_dc = 2 * NA_COLS - 1
    sel = np.zeros((GRID_W, 2 * GRID_W, LANES), np.float32)
    for qc in range(GRID_W):
        for kc in range(GRID_W):
            if valid[qc, kc]:
                sel[qc, kc, dc[qc, kc]] = 1.0
                sel[qc, GRID_W + kc, LANES // 2 + dc[qc, kc]] = 1.0
    sel = jnp.asarray(sel.reshape(GRID_W * 2 * GRID_W, LANES))
    flat = tiles.reshape(NA_HEADS * 2 * NA_ROWS, GRID_W * 2 * GRID_W)

    def body(a_ref, b_ref, out_ref):
        out_ref[...] = jnp.dot(a_ref[...], b_ref[...], preferred_element_type=F32, precision=lax.Precision.HIGHEST)

    sums = pl.pallas_call(
        body, name=name, out_shape=jax.ShapeDtypeStruct((flat.shape[0], LANES), F32),
        compiler_params=_params())(flat, sel).reshape(NA_HEADS, 2 * NA_ROWS, LANES)
    return sums[:, 1:, :n_dc] + sums[:, :2 * NA_ROWS - 1, LANES // 2:LANES // 2 + n_dc]


def _loss_grad(y, target, name):
    s, d = y.shape
    tm = min(s, LOSS_TOKENS)

    def body(y_ref, t_ref, sum_ref, dy_ref):
        diff = y_ref[...] - t_ref[...]
        dy_ref[...] = diff * (1.0 / d)
        part = jnp.zeros((8, LANES), F32) + jnp.sum(diff * diff)
        _accumulate(sum_ref, part, pl.program_id(0) == 0)

    row = pl.BlockSpec((tm, d), lambda i: (i, 0))
    return pl.pallas_call(
        body, name=name, grid=(s // tm,), in_specs=[row, row],
        out_specs=[pl.BlockSpec((8, LANES), lambda i: (0, 0)), row],
        out_shape=[jax.ShapeDtypeStruct((8, LANES), F32), jax.ShapeDtypeStruct((s, d), F32)],
        compiler_params=_params())(y, target)


def _adamw(w, g, m, v, name):
    rows, cols = w.shape
    tr = _row_tile(rows, 512, 8)

    def body(w_ref, g_ref, m_ref, v_ref, d_ref, nm_ref, nv_ref):
        gv = g_ref[...]
        nm = ADAM_B1 * m_ref[...] + (1.0 - ADAM_B1) * gv
        nv = ADAM_B2 * v_ref[...] + (1.0 - ADAM_B2) * (gv * gv)
        m_hat = nm / (1.0 - ADAM_B1 ** ADAM_STEP)
        v_hat = nv / (1.0 - ADAM_B2 ** ADAM_STEP)
        d_ref[...] = -ADAM_LR * (m_hat / (jnp.sqrt(v_hat) + ADAM_EPS) + ADAM_WD * w_ref[...])
        nm_ref[...] = nm
        nv_ref[...] = nv

    blk = pl.BlockSpec((tr, cols), lambda r: (r, 0))
    return pl.pallas_call(
        body, name=name, grid=(rows // tr,), in_specs=[blk] * 4, out_specs=[blk] * 3,
        out_shape=[jax.ShapeDtypeStruct((rows, cols), F32)] * 3, compiler_params=_params())(w, g, m, v)


def _adamw_nd(w, g, m, v, name):
    shape = w.shape
    flat = lambda t: t.reshape(-1, shape[-1])
    return tuple(t.reshape(shape) for t in _adamw(flat(w), flat(g), flat(m), flat(v), name))


def _pack(parts, rows_mult=64):
    flat = jnp.concatenate([p.reshape(-1).astype(F32) for p in parts])
    per = LANES * rows_mult
    total = -(-flat.shape[0] // per) * per
    return jnp.pad(flat, (0, total - flat.shape[0])).reshape(-1, LANES)


def _unpack(packed, shapes):
    flat = packed.reshape(-1)
    out, pos = [], 0
    for shp in shapes:
        n = int(np.prod(shp))
        out.append(flat[pos:pos + n].reshape(shp))
        pos += n
    return out


def kernel(x, ffn1_w_gate, ffn1_w_up, ffn1_w_down, ffn2_w_gate, ffn2_w_up, ffn2_w_down, w_in, pool_w, pool_scale, conv_w, rpb, w_out, ln_g, ln_b, loss_target, m_ffn1_w_gate, m_ffn1_w_up, m_ffn1_w_down, m_ffn2_w_gate, m_ffn2_w_up, m_ffn2_w_down, m_w_in, m_pool_w, m_pool_scale, m_conv_w, m_rpb, m_w_out, m_ln_g, m_ln_b, v_ffn1_w_gate, v_ffn1_w_up, v_ffn1_w_down, v_ffn2_w_gate, v_ffn2_w_up, v_ffn2_w_down, v_w_in, v_pool_w, v_pool_scale, v_conv_w, v_rpb, v_w_out, v_ln_g, v_ln_b):
    n_l, d, fs = ffn1_w_gate.shape
    s = x.shape[1]
    rows = s // GRID_W
    assert x.shape[0] == 1 and s % Q_TOK == 0 and rows >= K_ROWS and fs % BF16_ROWS == 0
    alpha = (2.0 * n_l) ** 0.25
    xi, yi, ci = _mesh_pos()
    me = 4 * xi + 2 * yi + ci
    core = jnp.reshape(ci, (1,)).astype(jnp.int32)
    ln_w, cw_w = ln_g.shape[2], conv_w.shape[2]

    tr = lambda w: jnp.swapaxes(w, 1, 2)
    ffn1_shard = jnp.stack([tr(ffn1_w_gate), tr(ffn1_w_up), ffn1_w_down], axis=1).astype(BF16)
    ffn2_shard = jnp.stack([tr(ffn2_w_gate), tr(ffn2_w_up), ffn2_w_down], axis=1).astype(BF16)
    win_shard, wout_shard = tr(w_in).astype(BF16), w_out.astype(BF16)
    small_shard = _pack([ln_g, ln_b, conv_w])
    w_ffn1, small = _exchange_alone(_Gather([ffn1_shard[0], small_shard]), "gather_first")
    n_ln = n_l * 3 * ln_w
    small = small.reshape(N_DEV, -1)
    unshard = lambda t, width: jnp.moveaxis(t.reshape(N_DEV, n_l, 3, width), 0, 2).reshape(n_l, 3, N_DEV * width)
    ln_g_all = unshard(small[:, :n_ln], ln_w)
    ln_b_all = unshard(small[:, n_ln:2 * n_ln], ln_w)
    conv_all = unshard(small[:, 2 * n_ln:2 * n_ln + n_l * 3 * cw_w], cw_w)
    pool_bd = jnp.zeros((n_l, D_POOL, D_POOL), F32)
    for g in range(len(POOL_WINDOWS)):
        sl = slice(g * POOL_GROUP, (g + 1) * POOL_GROUP)
        pool_bd = pool_bd.at[:, sl, sl].set(pool_w[:, g])
    pool_bd = pool_bd.astype(BF16)
    lnp = lambda arr, l, j: arr[l, j].reshape(1, d)

    saved = []
    h = x.reshape(s, d)
    for l in range(n_l):
        a1, u1, h1, z1, x1, w_in_l, w_out_l, w_ffn2 = _ffn_fwd(
            h, w_ffn1, lnp(ln_g_all, l, 0), lnp(ln_b_all, l, 0), alpha, f"ffn1_fwd_{l}",
            job=_Gather([win_shard[l], wout_shard[l], ffn2_shard[l]]))
        proj = _win_fwd(x1, w_in_l, f"win_fwd_{l}")
        bias = _na_tiles(rpb[l])
        yab = _local_fwd(proj[0], pool_bd[l], pool_scale[l].reshape(1, D_POOL), conv_all[l], f"local_fwd_{l}")
        yc, lse = _attn_fwd(proj[1], bias, f"attn_fwd_{l}")
        a2, u2, h2, z3, x3, z2, x2, *w_next = _ffn_fwd(
            x1, w_ffn2, lnp(ln_g_all, l, 2), lnp(ln_b_all, l, 2), alpha, f"ffn2_fwd_{l}",
            job=_Gather([ffn1_shard[l + 1]]) if l + 1 < n_l else None,
            mixer=(yab, yc, w_out_l, lnp(ln_g_all, l, 1), lnp(ln_b_all, l, 1)))
        saved.append((h, a1, u1, h1, z1, x1, proj, bias, yab, yc, lse, z2, x2, a2, u2, h2, z3, w_ffn1, w_in_l, w_out_l, w_ffn2))
        h = x3
        if w_next:
            w_ffn1 = w_next[0]

    sq, dh = _loss_grad(h, loss_target.reshape(s, d), "loss_head")
    loss = lax.psum(sq[0, 0] * (0.5 / d), MESH_AXES)

    flat = lambda blocks: [b.reshape(N_DEV, -1, d) for b in blocks]
    pair_add = lambda blocks, got, tag: [_pair_add(b, g, core, f"grads_pair_add_{tag}_{i}")
                                         for i, (b, g) in enumerate(zip(blocks, got))]
    small_grads = [None] * n_l
    reduced = [None] * n_l
    above = None
    for l in reversed(range(n_l)):
        x0, a1, u1, h1, z1, x1, proj, bias, yab, yc, lse, z2, x2, a2, u2, h2, z3, w_ffn1, w_in_l, w_out_l, w_ffn2 = saved[l]
        dx2, da, du, df, dg3, db3, *got = _ffn_bwd_dx(
            dh, z3, a2, u2, w_ffn2, lnp(ln_g_all, l, 2), alpha, f"ffn2_bwd_dx_{l}",
            job=_PairExchange(above) if above else None)
        above_pairs = pair_add(above, got, f"mix_{l + 1}") if above else None
        g2 = flat([_ffn_bwd_dwd(h2, df, _ffn_bwd_dwgu(da, du, x2, fs, f"ffn2_bwd_dwgu_{l}"), f"ffn2_bwd_dwd_{l}")])
        dmix, dxp, dg2, db2, g_out, *got = _wout_bwd(dx2, z2, yab, yc, w_out_l, lnp(ln_g_all, l, 1), alpha,
                                                     f"wout_bwd_{l}", job=_PairExchange(g2))
        p2 = pair_add(g2, got, f"ffn2_{l}")
        dq, dk, dv, dtiles, *crossed = _attn_bwd(proj[1], bias, yc, dmix, lse, f"attn_bwd_{l}",
                                                 job=_ChipExchange(above_pairs) if above else None)
        if above:
            reduced[l + 1] += crossed
        dloc, dpw, dsc, dcw = _local_bwd(proj[0], dmix, pool_bd[l], pool_scale[l].reshape(1, D_POOL), conv_all[l],
                                         f"local_bwd_{l}")
        dx1, g_in = _win_bwd(dxp, dloc, dq, dk, dv, x1, w_in_l, f"win_bwd_{l}")
        dx0, da, du, df, dg1, db1, *crossed = _ffn_bwd_dx(dx1, z1, a1, u1, w_ffn1, lnp(ln_g_all, l, 0), alpha,
                                                          f"ffn1_bwd_dx_{l}", job=_ChipExchange(p2))
        reduced[l] = list(crossed)
        g1 = _ffn_bwd_dwd(h1, df, _ffn_bwd_dwgu(da, du, x0, fs, f"ffn1_bwd_dwgu_{l}"), f"ffn1_bwd_dwd_{l}")
        above = flat([g_out, g_in, g1])
        drpb = _rpb_finish(dtiles, f"rpb_finish_{l}")
        dpool = jnp.stack([dpw[g * POOL_GROUP:(g + 1) * POOL_GROUP, g * POOL_GROUP:(g + 1) * POOL_GROUP]
                           for g in range(len(POOL_WINDOWS))])
        small_grads[l] = (jnp.concatenate([dg1, dg2, dg3]), jnp.concatenate([db1, db2, db3]), dcw[0:3], dpool, dsc[0], drpb)
        dh = dx0
    grad_x = dh.reshape(x.shape)

    last_pairs = pair_add(above, _exchange_alone(_PairExchange(above), "grads_pair_exchange_last"), "mix_0")
    reduced[0] += _exchange_alone(_ChipExchange(last_pairs), "grads_chip_exchange_last")
    sums = [[_sum_blocks(q, f"grads_chip_sum_{l}_{i}") for i, q in enumerate(reduced[l])] for l in range(n_l)]
    r_ffn2, r_out, r_in, r_ffn1 = [jnp.stack([sums[l][i] for l in range(n_l)]) for i in range(4)]
    r_ffn1, r_ffn2 = r_ffn1.reshape(n_l, 3, fs, d), r_ffn2.reshape(n_l, 3, fs, d)
    row_grads = {"ffn1_w_gate": r_ffn1[:, 0], "ffn1_w_up": r_ffn1[:, 1], "ffn2_w_gate": r_ffn2[:, 0],
                 "ffn2_w_up": r_ffn2[:, 1], "w_in": r_in}
    grads = {"ffn1_w_down": r_ffn1[:, 2], "ffn2_w_down": r_ffn2[:, 2], "w_out": r_out}
    grads.update({n: tr(g) for n, g in row_grads.items()})

    stack = lambda k: jnp.stack([small_grads[l][k] for l in range(n_l)])
    small_shapes = [(n_l, 3, d), (n_l, 3, d), (n_l, 3, D_CONV), pool_w.shape, pool_scale.shape, rpb.shape]
    (small_all,) = _exchange_alone(_Gather([_pack([stack(k) for k in range(6)])]), "gather_small_grads")
    small_sum = _sum_blocks(small_all, "small_grads_sum")
    g_ln_g, g_ln_b, g_conv, g_pool_w, g_pool_scale, g_rpb = _unpack(small_sum, small_shapes)
    own = lambda t, width: lax.dynamic_slice_in_dim(t, me * width, width, axis=2)
    grads.update({"ln_g": own(g_ln_g, ln_w), "ln_b": own(g_ln_b, ln_w), "conv_w": own(g_conv, cw_w),
                  "pool_w": g_pool_w, "pool_scale": g_pool_scale, "rpb": g_rpb})

    weights = dict(ffn1_w_gate=ffn1_w_gate, ffn1_w_up=ffn1_w_up, ffn1_w_down=ffn1_w_down, ffn2_w_gate=ffn2_w_gate,
                   ffn2_w_up=ffn2_w_up, ffn2_w_down=ffn2_w_down, w_in=w_in, pool_w=pool_w, pool_scale=pool_scale,
                   conv_w=conv_w, rpb=rpb, w_out=w_out, ln_g=ln_g, ln_b=ln_b)
    m_in = dict(ffn1_w_gate=m_ffn1_w_gate, ffn1_w_up=m_ffn1_w_up, ffn1_w_down=m_ffn1_w_down, ffn2_w_gate=m_ffn2_w_gate,
                ffn2_w_up=m_ffn2_w_up, ffn2_w_down=m_ffn2_w_down, w_in=m_w_in, pool_w=m_pool_w, pool_scale=m_pool_scale,
                conv_w=m_conv_w, rpb=m_rpb, w_out=m_w_out, ln_g=m_ln_g, ln_b=m_ln_b)
    v_in = dict(ffn1_w_gate=v_ffn1_w_gate, ffn1_w_up=v_ffn1_w_up, ffn1_w_down=v_ffn1_w_down, ffn2_w_gate=v_ffn2_w_gate,
                ffn2_w_up=v_ffn2_w_up, ffn2_w_down=v_ffn2_w_down, w_in=v_w_in, pool_w=v_pool_w, pool_scale=v_pool_scale,
                conv_w=v_conv_w, rpb=v_rpb, w_out=v_w_out, ln_g=v_ln_g, ln_b=v_ln_b)
    names = list(weights)
    large = ["ffn1_w_gate", "ffn1_w_up", "ffn1_w_down", "ffn2_w_gate", "ffn2_w_up", "ffn2_w_down", "w_in", "w_out"]
    tiny = [n for n in names if n not in large]
    delta, new_m, new_v = {}, {}, {}
    for n in large:
        if n in row_grads:
            out = _adamw_nd(tr(weights[n]), row_grads[n], tr(m_in[n]), tr(v_in[n]), f"adamw_{n}")
            delta[n], new_m[n], new_v[n] = (tr(t) for t in out)
        else:
            delta[n], new_m[n], new_v[n] = _adamw_nd(weights[n], grads[n], m_in[n], v_in[n], f"adamw_{n}")
    packed = [_pack([t[n] for n in tiny]) for t in (weights, grads, m_in, v_in)]
    tiny_out = _adamw(*packed, "adamw_small")
    tiny_shapes = [weights[n].shape for n in tiny]
    for res, t in zip((delta, new_m, new_v), tiny_out):
        res.update(dict(zip(tiny, _unpack(t, tiny_shapes))))

    return (loss, grad_x, *[grads[n] for n in names], *[delta[n] for n in names],
            *[new_m[n] for n in names], *[new_v[n] for n in names])
```

```python
import functools

import numpy as np
import jax
import jax.numpy as jnp
from jax import lax
from jax.experimental import pallas as pl
from jax.experimental.pallas import tpu as pltpu

F32, BF16 = jnp.float32, jnp.bfloat16
MESH = pl.DeviceIdType.MESH
N_DEV = 8
MESH_AXES = ("x", "y", "c")

LN_EPS = 1e-5
NEG_INF = -1e30
D_POOL = 256
POOL_WINDOWS = (2, 4, 8, 16)
POOL_GROUP = 64
D_CONV = 256
NA_HEADS = 8
NA_HEAD_DIM = 64
D_NA = NA_HEADS * NA_HEAD_DIM
GRID_W = 64
NA_ROWS = 8
NA_COLS = 16
D_LOC = D_POOL + 3 * D_CONV
D_MIX = D_POOL + D_CONV + D_NA
ADAM_LR, ADAM_B1, ADAM_B2, ADAM_EPS, ADAM_WD, ADAM_STEP = 0.001, 0.9, 0.999, 1e-08, 0.01, 10

VMEM_LIMIT_BYTES = 56 * 1024 * 1024
LANES = 128
BF16_ROWS = 16
HALO = 16
Q_ROWS = 8
K_ROWS = 16
Q_TOK = Q_ROWS * GRID_W
K_TOK = K_ROWS * GRID_W
K_BLK = 4 * GRID_W
HEAD_PAIR = 2 * NA_HEAD_DIM
FFN_CHUNK_DEVS = 4
FFN_TOKENS = 256
DW_TOKENS = 1024
MIX_TOKENS = 512
LOSS_TOKENS = 1024
PASS_ON_AT = 7 / 8

NT = (((1,), (1,)), ((), ()))
TN = (((0,), (0,)), ((), ()))


def _dot(a, b):
    return jnp.dot(a, b, preferred_element_type=F32)


def _dot_nt(a, b):
    return lax.dot_general(a, b, NT, preferred_element_type=F32)


def _dot_tn(a, b):
    return lax.dot_general(a, b, TN, preferred_element_type=F32)


def _params():
    return pltpu.CompilerParams(vmem_limit_bytes=VMEM_LIMIT_BYTES)


def _row_tile(rows, pref, mult=BF16_ROWS):
    t = min(rows, pref)
    t -= t % mult
    while t > mult and rows % t:
        t -= mult
    assert t > 0 and rows % t == 0, (rows, pref)
    return t


def _mesh_pos():
    return tuple(lax.axis_index(a) for a in MESH_AXES)


def _any_spec():
    return pl.BlockSpec(memory_space=pl.ANY)


class _Gather:
    def __init__(self, shards):
        self.arrays = list(shards)
        n = len(shards)
        self.out_shape = [jax.ShapeDtypeStruct((N_DEV,) + s.shape, s.dtype) for s in shards]
        self.scratch = [pltpu.SemaphoreType.DMA((n, 7)), pltpu.SemaphoreType.DMA((n, 7)), pltpu.SemaphoreType.DMA((n,))]

    def phases(self, ins, outs, sems):
        n = len(ins)
        send_sems, recv_sems, local_sems = sems
        x, y, c = _mesh_pos()
        me, sibling = (x, y, c), (x, y, 1 - c)
        chips = [(1 - x, y), (x, 1 - y), (1 - x, 1 - y)]

        def copy(a, k, block, to, src=None):
            dst = outs[a].at[4 * block[0] + 2 * block[1] + block[2]]
            return pltpu.make_async_remote_copy(
                src_ref=dst if src is None else src, dst_ref=dst,
                send_sem=send_sems.at[a, k], recv_sem=recv_sems.at[a, k],
                device_id=to, device_id_type=MESH)

        def mine():
            return [pltpu.make_async_copy(ins[a], outs[a].at[4 * x + 2 * y + c], local_sems.at[a]) for a in range(n)]

        def first():
            return [cp for a in range(n) for cp in
                    [copy(a, 0, me, sibling, src=ins[a])]
                    + [copy(a, 1 + j, me, (*chip, c), src=ins[a]) for j, chip in enumerate(chips)]]

        def passed():
            return [copy(a, 4 + j, (*chip, c), sibling) for j, chip in enumerate(chips) for a in range(n)]

        def start():
            for cp in mine() + first():
                cp.start()

        def middle():
            for j, chip in enumerate(chips):
                for a in range(n):
                    copy(a, 1 + j, (*chip, c), me).wait_recv()
            for cp in passed():
                cp.start()

        def finish():
            for a in range(n):
                copy(a, 0, sibling, me).wait_recv()
                for j, chip in enumerate(chips):
                    copy(a, 4 + j, (*chip, 1 - c), me).wait_recv()
            for cp in first() + passed():
                cp.wait_send()
            for cp in mine():
                cp.wait()

        return start, middle, finish


class _ChipExchange:
    def __init__(self, parts):
        self.arrays = list(parts)
        n = len(parts)
        self.out_shape = [jax.ShapeDtypeStruct(s.shape, s.dtype) for s in parts]
        self.scratch = [pltpu.SemaphoreType.DMA((n, 3)), pltpu.SemaphoreType.DMA((n, 3)), pltpu.SemaphoreType.DMA((n,))]

    def phases(self, ins, outs, sems):
        n = len(ins)
        send_sems, recv_sems, local_sems = sems
        x, y, c = _mesh_pos()
        my_chip = 2 * x + y
        chips = [(1 - x, y), (x, 1 - y), (1 - x, 1 - y)]

        def own():
            return [pltpu.make_async_copy(ins[a].at[my_chip], outs[a].at[my_chip], local_sems.at[a]) for a in range(n)]

        def copy(a, k, src_chip, dst_chip, to):
            return pltpu.make_async_remote_copy(
                src_ref=ins[a].at[src_chip], dst_ref=outs[a].at[dst_chip],
                send_sem=send_sems.at[a, k], recv_sem=recv_sems.at[a, k],
                device_id=to, device_id_type=MESH)

        def sends():
            return [copy(a, k, 2 * px + py, my_chip, (px, py, c)) for a in range(n) for k, (px, py) in enumerate(chips)]

        def start():
            for cp in own() + sends():
                cp.start()

        def finish():
            for cp in sends():
                cp.wait_send()
            for a in range(n):
                for k, (px, py) in enumerate(chips):
                    copy(a, k, my_chip, 2 * px + py, (px, py, c)).wait_recv()
            for cp in own():
                cp.wait()

        return start, None, finish


def _exchange_alone(job, name):
    n = len(job.arrays)

    def body(*refs):
        for phase in job.phases(refs[:n], refs[n:2 * n], refs[2 * n:]):
            if phase is not None:
                phase()

    return pl.pallas_call(
        body, name=name, out_shape=job.out_shape,
        in_specs=[_any_spec()] * n, out_specs=[_any_spec()] * n, scratch_shapes=job.scratch,
    )(*job.arrays)


def _riding_call(body, job, n_in, n_out, n_steps, step, **kw):
    if job is None:
        return pl.pallas_call(body, **kw)
    n_job, n_sem = len(job.arrays), len(job.scratch)
    kw = dict(kw, in_specs=list(kw["in_specs"]) + [_any_spec()] * n_job,
              out_specs=list(kw["out_specs"]) + [_any_spec()] * n_job,
              out_shape=list(kw["out_shape"]) + job.out_shape,
              scratch_shapes=list(kw.get("scratch_shapes", ())) + job.scratch)

    def riding(*refs):
        ins, job_ins = refs[:n_in], refs[n_in:n_in + n_job]
        outs = refs[n_in + n_job:n_in + n_job + n_out]
        job_outs = refs[n_in + n_job + n_out:n_in + 2 * n_job + n_out]
        scratch = refs[n_in + 2 * n_job + n_out:]
        start, middle, finish = job.phases(job_ins, job_outs, scratch[len(scratch) - n_sem:])
        now = step()
        pl.when(now == 0)(start)
        if middle is not None:
            assert n_steps >= 3
            pl.when(now == int(PASS_ON_AT * n_steps) - 1)(middle)
        body(*ins, *outs, *scratch[:len(scratch) - n_sem])
        pl.when(now == n_steps - 1)(finish)

    call = pl.pallas_call(riding, **kw)
    return lambda *args: call(*args, *job.arrays)


class _PairExchange:
    def __init__(self, slabs):
        self.arrays = list(slabs)
        n = len(slabs)
        self.out_shape = [jax.ShapeDtypeStruct((4,) + s.shape[1:], s.dtype) for s in slabs]
        self.scratch = [pltpu.SemaphoreType.DMA((n, 4)), pltpu.SemaphoreType.DMA((n, 4))]

    def phases(self, ins, outs, sems):
        n = len(ins)
        send_sems, recv_sems = sems
        x, y, c = _mesh_pos()

        def copies():
            return [pltpu.make_async_remote_copy(
                src_ref=ins[a].at[2 * j + 1 - c], dst_ref=outs[a].at[j],
                send_sem=send_sems.at[a, j], recv_sem=recv_sems.at[a, j],
                device_id=(x, y, 1 - c), device_id_type=MESH) for a in range(n) for j in range(4)]

        def start():
            for cp in copies():
                cp.start()

        def finish():
            for cp in copies():
                cp.wait_send()
            for cp in copies():
                cp.wait_recv()

        return start, None, finish


def _pair_add(slab, got, core, name):
    _, rows, d = slab.shape
    tr = _row_tile(rows, 1024)

    def body(core_ref, mine_ref, got_ref, out_ref):
        out_ref[...] = (mine_ref[...].astype(F32) + got_ref[...].astype(F32)).astype(out_ref.dtype)

    grid_spec = pltpu.PrefetchScalarGridSpec(
        num_scalar_prefetch=1, grid=(4, rows // tr),
        in_specs=[pl.BlockSpec((1, tr, d), lambda j, r, core_ref: (2 * j + core_ref[0], r, 0)),
                  pl.BlockSpec((1, tr, d), lambda j, r, core_ref: (j, r, 0))],
        out_specs=pl.BlockSpec((1, tr, d), lambda j, r, core_ref: (j, r, 0)))
    return pl.pallas_call(body, name=name, grid_spec=grid_spec,
                          out_shape=jax.ShapeDtypeStruct((4, rows, d), slab.dtype),
                          compiler_params=_params())(core, slab, got)


def _sum_blocks(parts, name):
    k, rows, d = parts.shape
    tr = _row_tile(rows, 512, BF16_ROWS if parts.dtype == BF16 else 8)

    def body(in_ref, out_ref):
        acc = in_ref[0].astype(F32)
        for j in range(1, k):
            acc = acc + in_ref[j].astype(F32)
        out_ref[...] = acc

    return pl.pallas_call(
        body, name=name, grid=(rows // tr,),
        in_specs=[pl.BlockSpec((k, tr, d), lambda r: (0, r, 0))],
        out_specs=pl.BlockSpec((tr, d), lambda r: (r, 0)),
        out_shape=jax.ShapeDtypeStruct((rows, d), F32), compiler_params=_params())(parts)


def _ln_stats(z):
    mu = jnp.mean(z, axis=-1, keepdims=True)
    zc = z - mu
    var = jnp.mean(zc * zc, axis=-1, keepdims=True)
    rstd = lax.rsqrt(var + LN_EPS)
    return zc * rstd, rstd


def _ln_bwd(dy, zhat, rstd, g):
    dyg = dy * g
    m1 = jnp.mean(dyg, axis=-1, keepdims=True)
    m2 = jnp.mean(dyg * zhat, axis=-1, keepdims=True)
    dz = rstd * (dyg - m1 - zhat * m2)
    return dz, jnp.sum(dy * zhat, axis=0, keepdims=True), jnp.sum(dy, axis=0, keepdims=True)


def _accumulate(ref, value, first):
    @pl.when(first)
    def _():
        ref[...] = value

    @pl.when(jnp.logical_not(first))
    def _():
        ref[...] += value


def _add_matmul(acc_ref, first, matmul):
    @pl.when(first)
    def _():
        acc_ref[...] = jnp.zeros_like(acc_ref)

    acc_ref[...] += matmul()


def _ffn_weight_specs(fs, d):
    def spec(row):
        return pl.BlockSpec((N_DEV, 1, fs, d), lambda i: (0, row, 0, 0), pipeline_mode=pl.Buffered(1))
    return [spec(0), spec(1), spec(2)]


def _ffn_fwd(x, w, ln_g, ln_b, alpha, name, job=None, mixer=None):
    s, d = x.shape
    fs = w.shape[2]
    f = N_DEV * fs
    tm = min(s, FFN_TOKENS)
    n_pre = 5 if mixer else 0

    def body(x_ref, *refs):
        xv = x_ref[...]
        if mixer:
            yab_ref, yc_ref, wo_ref, g0_ref, b0_ref = refs[:n_pre]
            zhat0_ref, rstd0_ref, x_out_ref = refs[len(refs) - 3:]
            mix = jnp.concatenate([yab_ref[...], yc_ref[...]], axis=1).astype(BF16)
            zhat0, rstd0 = _ln_stats(alpha * xv + _dot(mix, wo_ref[...].reshape(D_MIX, d)))
            zhat0_ref[...] = zhat0
            rstd0_ref[...] = rstd0
            xv = zhat0 * g0_ref[...] + b0_ref[...]
            x_out_ref[...] = xv
        wg_ref, wu_ref, wd_ref, g_ref, b_ref, a_ref, u_ref, h_ref, zhat_ref, rstd_ref, y_ref = refs[n_pre:n_pre + 11]
        xb = xv.astype(BF16)
        a = _dot_nt(xb, wg_ref[...].reshape(f, d))
        u = _dot_nt(xb, wu_ref[...].reshape(f, d))
        a_ref[...] = a.astype(BF16)
        u_ref[...] = u.astype(BF16)
        h = ((a * jax.nn.sigmoid(a)) * u).astype(BF16)
        h_ref[...] = h
        z = alpha * xv + 0.5 * _dot(h, wd_ref[...].reshape(f, d))
        zhat, rstd = _ln_stats(z)
        zhat_ref[...] = zhat
        rstd_ref[...] = rstd
        y_ref[...] = zhat * g_ref[...] + b_ref[...]

    row = pl.BlockSpec((tm, d), lambda i: (i, 0))
    col = pl.BlockSpec((tm, 1), lambda i: (i, 0))
    vec = pl.BlockSpec((1, d), lambda i: (0, 0))
    hid = pl.BlockSpec((tm, f), lambda i: (i, 0))
    half = pl.BlockSpec((tm, D_MIX // 2), lambda i: (i, 0))
    ln_out = [row, col, row]
    ln_shape = [jax.ShapeDtypeStruct((s, d), F32), jax.ShapeDtypeStruct((s, 1), F32), jax.ShapeDtypeStruct((s, d), F32)]
    pre_specs = [half, half, _whole(mixer[2]), vec, vec] if mixer else []
    call = _riding_call(
        body, job, 6 + n_pre, 6 + (3 if mixer else 0), s // tm, lambda: pl.program_id(0),
        name=name, grid=(s // tm,),
        in_specs=[row] + pre_specs + _ffn_weight_specs(fs, d) + [vec, vec],
        out_specs=[hid, hid, hid] + ln_out + (ln_out if mixer else []),
        out_shape=[jax.ShapeDtypeStruct((s, f), BF16)] * 3 + ln_shape + (ln_shape if mixer else []),
        compiler_params=_params())
    a, u, h, zhat, rstd, y, *rest = call(x, *(mixer or ()), w, w, w, ln_g, ln_b)
    if mixer:
        rest = [(rest[0], rest[1]), rest[2]] + rest[3:]
    return [a, u, h, (zhat, rstd), y] + rest


def _ffn_bwd_dx(dy, z, a, u, w, ln_g, alpha, name, job=None):
    s, d = dy.shape
    fs = w.shape[2]
    f = N_DEV * fs
    tm = min(s, FFN_TOKENS)

    def body(dy_ref, zhat_ref, rstd_ref, a_ref, u_ref, wg_ref, wu_ref, wd_ref, g_ref,
             dx_ref, da_ref, du_ref, df_ref, dg_ref, db_ref):
        i = pl.program_id(0)
        dz, dg, db = _ln_bwd(dy_ref[...], zhat_ref[...], rstd_ref[...], g_ref[...])
        _accumulate(dg_ref, dg, i == 0)
        _accumulate(db_ref, db, i == 0)
        df = (0.5 * dz).astype(BF16)
        df_ref[...] = df
        av = a_ref[...].astype(F32)
        uv = u_ref[...].astype(F32)
        sg = jax.nn.sigmoid(av)
        dh = _dot_nt(df, wd_ref[...].reshape(f, d))
        du = (dh * (av * sg)).astype(BF16)
        da = (dh * uv * (sg * (1.0 + av * (1.0 - sg)))).astype(BF16)
        da_ref[...] = da
        du_ref[...] = du
        dx_ref[...] = alpha * dz + _dot(da, wg_ref[...].reshape(f, d)) + _dot(du, wu_ref[...].reshape(f, d))

    row = pl.BlockSpec((tm, d), lambda i: (i, 0))
    col = pl.BlockSpec((tm, 1), lambda i: (i, 0))
    vec = pl.BlockSpec((1, d), lambda i: (0, 0))
    hid = pl.BlockSpec((tm, f), lambda i: (i, 0))
    call = _riding_call(
        body, job, 9, 6, s // tm, lambda: pl.program_id(0),
        name=name, grid=(s // tm,),
        in_specs=[row, row, col, hid, hid] + _ffn_weight_specs(fs, d) + [vec],
        out_specs=[row, hid, hid, row, vec, vec],
        out_shape=[jax.ShapeDtypeStruct((s, d), F32)] + [jax.ShapeDtypeStruct((s, f), BF16)] * 2
                  + [jax.ShapeDtypeStruct((s, d), BF16)] + [jax.ShapeDtypeStruct((1, d), F32)] * 2,
        compiler_params=_params())
    return call(dy, *z, a, u, w, w, w, ln_g)


def _ffn_bwd_dwgu(da, du, x, fs, name, job=None):
    s, d = x.shape
    tf = FFN_CHUNK_DEVS * fs
    n_c = N_DEV // FFN_CHUNK_DEVS
    tk = min(s, DW_TOKENS)
    n_k = s // tk

    def body(da_ref, du_ref, x_ref, out_ref, accg_s, accu_s):
        k = pl.program_id(1)
        xb = x_ref[...].astype(BF16)
        _add_matmul(accg_s, k == 0, lambda: _dot_tn(da_ref[...], xb))
        _add_matmul(accu_s, k == 0, lambda: _dot_tn(du_ref[...], xb))

        @pl.when(k == n_k - 1)
        def _():
            out_ref[:, 0] = accg_s[...].astype(BF16).reshape(FFN_CHUNK_DEVS, fs, d)
            out_ref[:, 1] = accu_s[...].astype(BF16).reshape(FFN_CHUNK_DEVS, fs, d)

    hid = pl.BlockSpec((tk, tf), lambda c, k: (k, c))
    call = _riding_call(
        body, job, 3, 1, n_c * n_k, lambda: pl.program_id(0) * n_k + pl.program_id(1),
        name=name, grid=(n_c, n_k),
        in_specs=[hid, hid, pl.BlockSpec((tk, d), lambda c, k: (k, 0))],
        out_specs=[pl.BlockSpec((FFN_CHUNK_DEVS, 2, fs, d), lambda c, k: (c, 0, 0, 0), pipeline_mode=pl.Buffered(1))],
        out_shape=[jax.ShapeDtypeStruct((N_DEV, 3, fs, d), BF16)],
        scratch_shapes=[pltpu.VMEM((tf, d), F32), pltpu.VMEM((tf, d), F32)],
        compiler_params=_params())
    return call(da, du, x)


def _ffn_bwd_dwd(h, df, blocks, name, job=None):
    s, d = df.shape
    fs = blocks.shape[2]
    tf = FFN_CHUNK_DEVS * fs
    n_c = N_DEV // FFN_CHUNK_DEVS
    tk = min(s, 2 * DW_TOKENS)
    n_k = s // tk

    def body(h_ref, df_ref, blocks_ref, out_ref, acc_s):
        k = pl.program_id(1)
        _add_matmul(acc_s, k == 0, lambda: _dot_tn(h_ref[...], df_ref[...]))

        @pl.when(k == n_k - 1)
        def _():
            out_ref[:, 0] = acc_s[...].astype(BF16).reshape(FFN_CHUNK_DEVS, fs, d)

    call = _riding_call(
        body, job, 3, 1, n_c * n_k, lambda: pl.program_id(0) * n_k + pl.program_id(1),
        name=name, grid=(n_c, n_k),
        in_specs=[pl.BlockSpec((tk, tf), lambda c, k: (k, c)), pl.BlockSpec((tk, d), lambda c, k: (k, 0)), _any_spec()],
        out_specs=[pl.BlockSpec((FFN_CHUNK_DEVS, 1, fs, d), lambda c, k: (c, 2, 0, 0), pipeline_mode=pl.Buffered(1))],
        out_shape=[jax.ShapeDtypeStruct(blocks.shape, BF16)], input_output_aliases={2: 0},
        scratch_shapes=[pltpu.VMEM((tf, d), F32)],
        compiler_params=_params())
    return call(h, df, blocks)


def _whole(arr):
    return pl.BlockSpec(arr.shape, lambda i: (0,) * arr.ndim, pipeline_mode=pl.Buffered(1))


def _win_fwd(x, w_in, name):
    s, d = x.shape
    d_in = N_DEV * w_in.shape[1]
    tm = min(s, MIX_TOKENS)
    scale = NA_HEAD_DIM ** -0.5
    assert d_in == D_LOC + 3 * D_NA and scale == 0.125

    def body(x_ref, w_ref, loc_ref, qkv_ref):
        proj = _dot_nt(x_ref[...].astype(BF16), w_ref[...].reshape(d_in, d))
        loc_ref[...] = proj[:, :D_LOC]
        qkv_ref[:, :D_NA] = (proj[:, D_LOC:D_LOC + D_NA] * scale).astype(BF16)
        qkv_ref[:, D_NA:] = proj[:, D_LOC + D_NA:].astype(BF16)

    return pl.pallas_call(
        body, name=name, grid=(s // tm,),
        in_specs=[pl.BlockSpec((tm, d), lambda i: (i, 0)), _whole(w_in)],
        out_specs=[pl.BlockSpec((tm, D_LOC), lambda i: (i, 0)), pl.BlockSpec((tm, 3 * D_NA), lambda i: (i, 0))],
        out_shape=[jax.ShapeDtypeStruct((s, D_LOC), F32), jax.ShapeDtypeStruct((s, 3 * D_NA), BF16)],
        compiler_params=_params())(x, w_in)


def _wout_bwd(dy, z, yab, yc, w_out, ln_g, alpha, name, job=None):
    s, d = dy.shape
    rs = w_out.shape[1]
    tm = min(s, MIX_TOKENS)
    n_i = s // tm

    def body(dy_ref, zhat_ref, rstd_ref, yab_ref, yc_ref, w_ref, g_ref, dmix_ref, dxp_ref, dg_ref, db_ref, out_ref, acc_s):
        i = pl.program_id(0)
        dz, dg, db = _ln_bwd(dy_ref[...], zhat_ref[...], rstd_ref[...], g_ref[...])
        _accumulate(dg_ref, dg, i == 0)
        _accumulate(db_ref, db, i == 0)
        dxp_ref[...] = alpha * dz
        dzb = dz.astype(BF16)
        dmix_ref[...] = _dot_nt(dzb, w_ref[...].reshape(D_MIX, d))
        mix = jnp.concatenate([yab_ref[...], yc_ref[...]], axis=1).astype(BF16)
        _add_matmul(acc_s, i == 0, lambda: _dot_tn(mix, dzb))

        @pl.when(i == n_i - 1)
        def _():
            out_ref[...] = acc_s[...].astype(BF16).reshape(N_DEV, rs, d)

    row = pl.BlockSpec((tm, d), lambda i: (i, 0))
    half = pl.BlockSpec((tm, D_MIX // 2), lambda i: (i, 0))
    vec = pl.BlockSpec((1, d), lambda i: (0, 0))
    call = _riding_call(
        body, job, 7, 5, n_i, lambda: pl.program_id(0),
        name=name, grid=(n_i,),
        in_specs=[row, row, pl.BlockSpec((tm, 1), lambda i: (i, 0)), half, half, _whole(w_out), vec],
        out_specs=[pl.BlockSpec((tm, D_MIX), lambda i: (i, 0)), row, vec, vec, _whole(w_out)],
        out_shape=[jax.ShapeDtypeStruct((s, D_MIX), F32), jax.ShapeDtypeStruct((s, d), F32),
                   jax.ShapeDtypeStruct((1, d), F32), jax.ShapeDtypeStruct((1, d), F32),
                   jax.ShapeDtypeStruct(w_out.shape, BF16)],
        scratch_shapes=[pltpu.VMEM((D_MIX, d), F32)],
        compiler_params=_params())
    return call(dy, *z, yab, yc, w_out, ln_g)


def _win_bwd(dxp, dloc, dq, dk, dv, x, w_in, name):
    s, d = x.shape
    rs = w_in.shape[1]
    d_in = N_DEV * rs
    tm = min(s, MIX_TOKENS)
    n_i = s // tm

    def body(dxp_ref, dloc_ref, dq_ref, dk_ref, dv_ref, x_ref, w_ref, dx_ref, out_ref, acc_s):
        i = pl.program_id(0)
        dp = jnp.concatenate([dloc_ref[...], dq_ref[...], dk_ref[...].astype(BF16), dv_ref[...].astype(BF16)], axis=1)
        dx_ref[...] = dxp_ref[...] + _dot(dp, w_ref[...].reshape(d_in, d))
        _add_matmul(acc_s, i == 0, lambda: _dot_tn(dp, x_ref[...].astype(BF16)))

        @pl.when(i == n_i - 1)
        def _():
            out_ref[...] = acc_s[...].astype(BF16).reshape(N_DEV, rs, d)

    row = pl.BlockSpec((tm, d), lambda i: (i, 0))
    na = pl.BlockSpec((tm, D_NA), lambda i: (i, 0))
    return pl.pallas_call(
        body, name=name, grid=(n_i,),
        in_specs=[row, pl.BlockSpec((tm, D_LOC), lambda i: (i, 0)), na, na, na, row, _whole(w_in)],
        out_specs=[row, _whole(w_in)],
        out_shape=[jax.ShapeDtypeStruct((s, d), F32), jax.ShapeDtypeStruct(w_in.shape, BF16)],
        scratch_shapes=[pltpu.VMEM((d_in, d), F32)],
        compiler_params=_params())(dxp, dloc, dq, dk, dv, x, w_in)


def _shift_rows(v, k):
    n = v.shape[0]
    return pltpu.roll(v, k % n, 0)


def _halo_specs(tm, s, width, col):
    per = tm // HALO
    last = s // HALO - 1
    return [pl.BlockSpec((HALO, width), lambda i: (jnp.maximum(i * per - 1, 0), col)),
            pl.BlockSpec((tm, width), lambda i: (i, col)),
            pl.BlockSpec((HALO, width), lambda i: (jnp.minimum((i + 1) * per, last), col))]


def _token_index(i, tm):
    return i * tm - HALO + lax.broadcasted_iota(jnp.int32, (tm + 2 * HALO, 1), 0)


def _pool_lane_tables():
    lane = lax.broadcasted_iota(jnp.int32, (1, D_POOL), 1)
    group = sum((lane >= g * POOL_GROUP).astype(jnp.int32) for g in range(1, len(POOL_WINDOWS)))
    half = jnp.where(group == 0, 1, jnp.where(group == 1, 2, jnp.where(group == 2, 4, 8)))
    return group, half


def _window_sums(v, group, offsets):
    s2 = v + _shift_rows(v, 1)
    s4 = s2 + _shift_rows(s2, 2)
    s8 = s4 + _shift_rows(s4, 4)
    s16 = s8 + _shift_rows(s8, 8)
    parts = [_shift_rows(p, -o) if o else p for p, o in zip((s2, s4, s8, s16), offsets)]
    return jnp.where(group == 0, parts[0], jnp.where(group == 1, parts[1], jnp.where(group == 2, parts[2], parts[3])))


def _pool_counts(tok, half, s):
    return (jnp.minimum(tok + half, s) - jnp.maximum(tok - half, 0)).astype(F32)


def _pool_forward(u, tok, s):
    group, half = _pool_lane_tables()
    sums = _window_sums(u, group, [w // 2 - 1 for w in POOL_WINDOWS])
    return sums / _pool_counts(tok, half, s) - u


def _conv_forward(zc, cw_ref):
    return cw_ref[0:1, :] * _shift_rows(zc, 1) + cw_ref[1:2, :] * zc + cw_ref[2:3, :] * _shift_rows(zc, -1)


def _local_fwd(proj, pool_bd, pool_scale, conv_w, name):
    s = proj.shape[0]
    tm = min(s, MIX_TOKENS)
    ctr = slice(HALO, HALO + tm)

    def body(prev_ref, cur_ref, next_ref, pw_ref, sc_ref, cw_ref, out_ref):
        i = pl.program_id(0)
        ext = jnp.concatenate([prev_ref[...], cur_ref[...], next_ref[...]], axis=0)
        tok = _token_index(i, tm)
        inside = (tok >= 0) & (tok < s)
        u = jnp.where(inside, ext[:, 0:D_POOL], 0.0)
        p = _pool_forward(u, tok, s)[ctr]
        ya = _dot(p.astype(BF16), pw_ref[...]) * sc_ref[...]
        gb = ext[:, D_POOL:D_POOL + D_CONV]
        zc = jnp.where(inside, ext[:, D_POOL + D_CONV:D_POOL + 2 * D_CONV] * ext[:, D_POOL + 2 * D_CONV:D_LOC], 0.0)
        yb = (gb * _conv_forward(zc, cw_ref))[ctr]
        out_ref[...] = jnp.concatenate([ya, yb], axis=1)

    return pl.pallas_call(
        body, name=name, grid=(s // tm,),
        in_specs=_halo_specs(tm, s, D_LOC, 0) + [
            pl.BlockSpec((D_POOL, D_POOL), lambda i: (0, 0)), pl.BlockSpec((1, D_POOL), lambda i: (0, 0)),
            pl.BlockSpec((3, D_CONV), lambda i: (0, 0))],
        out_specs=pl.BlockSpec((tm, D_POOL + D_CONV), lambda i: (i, 0)),
        out_shape=jax.ShapeDtypeStruct((s, D_POOL + D_CONV), F32),
        compiler_params=_params())(proj, proj, proj, pool_bd, pool_scale, conv_w)


def _local_bwd(proj, dmix, pool_bd, pool_scale, conv_w, name):
    s = proj.shape[0]
    tm = min(s, MIX_TOKENS)
    ctr = slice(HALO, HALO + tm)

    def body(prev_ref, cur_ref, next_ref, dprev_ref, dcur_ref, dnext_ref, pw_ref, sc_ref, cw_ref,
             dloc_ref, dpw_ref, dsc_ref, dcw_ref):
        i = pl.program_id(0)
        first = i == 0
        ext = jnp.concatenate([prev_ref[...], cur_ref[...], next_ref[...]], axis=0)
        dext = jnp.concatenate([dprev_ref[...], dcur_ref[...], dnext_ref[...]], axis=0)
        tok = _token_index(i, tm)
        inside = (tok >= 0) & (tok < s)
        group, half = _pool_lane_tables()
        cnt = _pool_counts(tok, half, s)
        u = jnp.where(inside, ext[:, 0:D_POOL], 0.0)
        dya = jnp.where(inside, dext[:, 0:D_POOL], 0.0)
        p_c = _pool_forward(u, tok, s)[ctr].astype(BF16)
        lin = _dot(p_c, pw_ref[...])
        _accumulate(dsc_ref, jnp.sum(dya[ctr] * lin, axis=0, keepdims=True), first)
        e1 = (dya * sc_ref[...]).astype(BF16)
        _accumulate(dpw_ref, _dot_tn(p_c, e1[ctr]), first)
        dp = _dot_nt(e1, pw_ref[...])
        du = _window_sums(dp / cnt, group, [w // 2 for w in POOL_WINDOWS]) - dp
        gb = ext[:, D_POOL:D_POOL + D_CONV]
        gc = ext[:, D_POOL + D_CONV:D_POOL + 2 * D_CONV]
        hv = ext[:, D_POOL + 2 * D_CONV:D_LOC]
        zc = jnp.where(inside, gc * hv, 0.0)
        dyb = jnp.where(inside, dext[:, D_POOL:D_POOL + D_CONV], 0.0)
        dgb = dyb * _conv_forward(zc, cw_ref)
        dyc = dyb * gb
        for k in range(3):
            part = jnp.sum(dyc[ctr] * _shift_rows(zc, 1 - k)[ctr], axis=0, keepdims=True)
            _accumulate(dcw_ref.at[k:k + 1, :], part, first)
        dzc = cw_ref[0:1, :] * _shift_rows(dyc, -1) + cw_ref[1:2, :] * dyc + cw_ref[2:3, :] * _shift_rows(dyc, 1)
        dloc = jnp.concatenate([du, dgb, dzc * hv, dzc * gc], axis=1)
        dloc_ref[...] = dloc[ctr].astype(BF16)

    return pl.pallas_call(
        body, name=name, grid=(s // tm,),
        in_specs=_halo_specs(tm, s, D_LOC, 0) + _halo_specs(tm, s, D_POOL + D_CONV, 0) + [
            pl.BlockSpec((D_POOL, D_POOL), lambda i: (0, 0)), pl.BlockSpec((1, D_POOL), lambda i: (0, 0)),
            pl.BlockSpec((3, D_CONV), lambda i: (0, 0))],
        out_specs=[pl.BlockSpec((tm, D_LOC), lambda i: (i, 0)), pl.BlockSpec((D_POOL, D_POOL), lambda i: (0, 0)),
                   pl.BlockSpec((1, D_POOL), lambda i: (0, 0)), pl.BlockSpec((8, D_CONV), lambda i: (0, 0))],
        out_shape=[jax.ShapeDtypeStruct((s, D_LOC), BF16), jax.ShapeDtypeStruct((D_POOL, D_POOL), F32),
                   jax.ShapeDtypeStruct((1, D_POOL), F32), jax.ShapeDtypeStruct((8, D_CONV), F32)],
        compiler_params=_params())(proj, proj, proj, dmix, dmix, dmix, pool_bd, pool_scale, conv_w)


def _na_geometry(rows):
    n_j = rows // Q_ROWS
    dr = np.full((3, Q_ROWS, K_ROWS), 2 * NA_ROWS - 1, np.int64)
    for t, j in enumerate((0, min(1, n_j - 1), n_j - 1)):
        base = int(np.clip(Q_ROWS * j - NA_ROWS // 2, 0, rows - K_ROWS))
        for qr in range(Q_ROWS):
            r = Q_ROWS * j + qr
            start = int(np.clip(r - NA_ROWS // 2, 0, rows - NA_ROWS))
            for kr in range(K_ROWS):
                if start <= base + kr < start + NA_ROWS:
                    dr[t, qr, kr] = base + kr - r + NA_ROWS - 1
    return dr


def _na_col_tables():
    c = np.arange(GRID_W)
    start = np.clip(c - NA_COLS // 2, 0, GRID_W - NA_COLS)
    valid = (c[None, :] >= start[:, None]) & (c[None, :] < start[:, None] + NA_COLS)
    dc = np.clip(c[None, :] - c[:, None], -(NA_COLS - 1), NA_COLS - 1) + (NA_COLS - 1)
    return valid, dc


NO_ROW = 2 * NA_ROWS - 1
N_SLOT = 2 * NA_ROWS


def _na_tiles(rpb):
    valid, dc = _na_col_tables()
    onehot = jnp.asarray((dc[None] == np.arange(2 * NA_COLS - 1)[:, None, None]).astype(np.float32))
    table = jnp.einsum("hrd,dqk->hrqk", rpb, onehot, precision=lax.Precision.HIGHEST)
    table = jnp.where(jnp.asarray(valid)[None, None], table, NEG_INF)
    outside = jnp.full((NA_HEADS, 1, GRID_W, GRID_W), NEG_INF, F32)
    padded = jnp.concatenate([outside, table, outside], axis=1)
    pairs = jnp.concatenate([padded[:, :N_SLOT], padded[:, 1:]], axis=-1)
    return jnp.concatenate([pairs, jnp.full((NA_HEADS, 1, GRID_W, 2 * GRID_W), NEG_INF, F32)], axis=1)


G_ROWS = 2
N_GRP = Q_ROWS // G_ROWS
G_TOK = G_ROWS * GRID_W
GK_ROWS = NA_ROWS + G_ROWS
GK_TOK = GK_ROWS * GRID_W
STACK_TOK = N_GRP * 2 * G_TOK


def _na_group_tables(rows):
    dr = _na_geometry(rows)
    koff = np.zeros((3, N_GRP), np.int64)
    slot = np.zeros((3, N_GRP, G_ROWS, GK_ROWS // 2), np.int64)
    even_in, odd_in = np.zeros_like(slot), np.zeros_like(slot)
    for t in range(3):
        for g in range(N_GRP):
            qrs = range(G_ROWS * g, G_ROWS * (g + 1))
            inside = [kr for kr in range(K_ROWS) if any(dr[t, qr, kr] != NO_ROW for qr in qrs)]
            lo, hi = min(inside), max(inside) + 1
            off = min(lo - lo % 2, K_ROWS - GK_ROWS)
            assert off <= lo and hi <= off + GK_ROWS
            koff[t, g] = off
            for qq, qr in enumerate(qrs):
                for kp in range(GK_ROWS // 2):
                    even, odd = int(dr[t, qr, off + 2 * kp]), int(dr[t, qr, off + 2 * kp + 1])
                    even_in[t, g, qq, kp], odd_in[t, g, qq, kp] = even != NO_ROW, odd != NO_ROW
                    slot[t, g, qq, kp] = (N_SLOT if even == NO_ROW and odd == NO_ROW
                                          else (even if even != NO_ROW else odd - 1) + 1)
    return koff, slot, even_in, odd_in


def _by_type(block_type, per_type):
    a, b, c = (int(v) for v in per_type)
    if a == b == c:
        return a
    return jnp.where(block_type == 0, a, jnp.where(block_type == 2, c, b))


def _score_rows(g, hh):
    first = (2 * g + hh) * G_TOK
    return slice(first, first + G_TOK)


def _tile_at(g, hh, qq, kp):
    first = _score_rows(g, hh).start + qq * GRID_W
    return slice(first, first + GRID_W), slice(kp * 2 * GRID_W, (kp + 1) * 2 * GRID_W)


def _fill_bias(bias_s, tiles_ref, block_type, tables):
    _, slot, even_in, odd_in = tables
    left = lax.broadcasted_iota(jnp.int32, (1, 2 * GRID_W), 1) < GRID_W
    for hh in range(2):
        for g in range(N_GRP):
            for qq in range(G_ROWS):
                for kp in range(GK_ROWS // 2):
                    tile = tiles_ref[hh, _by_type(block_type, slot[:, g, qq, kp])]
                    tile = jnp.where(left & (_by_type(block_type, even_in[:, g, qq, kp]) == 0), NEG_INF, tile)
                    tile = jnp.where(jnp.logical_not(left) & (_by_type(block_type, odd_in[:, g, qq, kp]) == 0), NEG_INF, tile)
                    rs, cs = _tile_at(g, hh, qq, kp)
                    bias_s[rs, cs] = tile


def _group_offset(block_type, koff, g):
    off = _by_type(block_type, koff[:, g]) * GRID_W
    return off if isinstance(off, int) else pl.multiple_of(off, 2 * GRID_W)


def _na_specs(s, proj_cols):
    n_blk = s // K_BLK
    per = Q_TOK // K_BLK

    def kv_spec(col0, m):
        return pl.BlockSpec((K_BLK, HEAD_PAIR), lambda hp, j: (jnp.clip(per * j - 1, 0, n_blk - 4) + m, col0 + hp))

    q_col, k_col, v_col = (c // HEAD_PAIR for c in proj_cols)
    return ([pl.BlockSpec((Q_TOK, HEAD_PAIR), lambda hp, j: (j, q_col + hp))]
            + [kv_spec(k_col, m) for m in range(4)] + [kv_spec(v_col, m) for m in range(4)])


def _na_block_type(j, n_j):
    return jnp.where(j == 0, 0, jnp.where(j == n_j - 1, 2, 1))


def _head_masks():
    lane = lax.broadcasted_iota(jnp.int32, (1, HEAD_PAIR), 1)
    return [lane < NA_HEAD_DIM, lane >= NA_HEAD_DIM]


def _attn_fwd(qkv, tiles, name):
    s = qkv.shape[0]
    n_j = s // Q_TOK
    tables = _na_group_tables(s // GRID_W)
    koff = tables[0]

    def body(q_ref, k0, k1, k2, k3, v0, v1, v2, v3, tiles_ref, o_ref, lse_ref, bias_s, k_s, v_s, sc_s, p_s):
        j = pl.program_id(1)
        block_type = _na_block_type(j, n_j)
        pl.when((j == 0) | (j == 1) | (j == n_j - 1))(functools.partial(_fill_bias, bias_s, tiles_ref, block_type, tables))
        masks = _head_masks()
        for m, (kr, vr) in enumerate(zip((k0, k1, k2, k3), (v0, v1, v2, v3))):
            rows = slice(m * K_BLK, (m + 1) * K_BLK)
            k_s[rows, :] = kr[...]
            v_s[rows, :] = vr[...]
        q = q_ref[...]
        qh = [jnp.where(mask, q, jnp.zeros_like(q)) for mask in masks]
        offs = [_group_offset(block_type, koff, g) for g in range(N_GRP)]
        for g in range(N_GRP):
            kg = k_s[pl.ds(offs[g], GK_TOK), :]
            for hh in range(2):
                sc_s[_score_rows(g, hh), :] = _dot_nt(qh[hh][g * G_TOK:(g + 1) * G_TOK], kg)
        sc = sc_s[...] + bias_s[...]
        mx = jnp.max(sc, axis=-1, keepdims=True)
        p = jnp.exp(sc - mx)
        den = jnp.sum(p, axis=-1, keepdims=True)
        p_s[...] = p.astype(BF16)
        inv = 1.0 / den
        lse = mx + jnp.log(den)
        for g in range(N_GRP):
            rows = slice(g * G_TOK, (g + 1) * G_TOK)
            out = jnp.zeros((G_TOK, HEAD_PAIR), F32)
            for hh in range(2):
                sr = _score_rows(g, hh)
                out = out + jnp.where(masks[hh], _dot(p_s[sr, :], v_s[pl.ds(offs[g], GK_TOK), :]) * inv[sr], 0.0)
            o_ref[rows, :] = out
            lse_ref[0, rows, :] = jnp.where(masks[0], lse[_score_rows(g, 0)], lse[_score_rows(g, 1)])

    return pl.pallas_call(
        body, name=name, grid=(NA_HEADS // 2, n_j),
        in_specs=_na_specs(s, (0, D_NA, 2 * D_NA)) + [
            pl.BlockSpec((2, N_SLOT + 1, GRID_W, 2 * GRID_W), lambda hp, j: (hp, 0, 0, 0))],
        out_specs=[pl.BlockSpec((Q_TOK, HEAD_PAIR), lambda hp, j: (j, hp)),
                   pl.BlockSpec((1, Q_TOK, HEAD_PAIR), lambda hp, j: (hp, j, 0))],
        out_shape=[jax.ShapeDtypeStruct((s, D_NA), F32), jax.ShapeDtypeStruct((NA_HEADS // 2, s, HEAD_PAIR), F32)],
        scratch_shapes=[pltpu.VMEM((STACK_TOK, GK_TOK), F32), pltpu.VMEM((K_TOK, HEAD_PAIR), BF16),
                        pltpu.VMEM((K_TOK, HEAD_PAIR), BF16), pltpu.VMEM((STACK_TOK, GK_TOK), F32),
                        pltpu.VMEM((STACK_TOK, GK_TOK), BF16)],
        compiler_params=_params())(*([qkv] * 9), tiles)


def _add_tiles(dtile_ref, ds_ref, block_type, slot, has_interior):
    def tile(g, hh, qq, kp):
        rs, cs = _tile_at(g, hh, qq, kp)
        return ds_ref[rs, cs].astype(F32)

    def interior():
        for hh in range(2):
            for qq in range(G_ROWS):
                for kp in range(GK_ROWS // 2):
                    assert (slot[1, :, qq, kp] == slot[1, 0, qq, kp]).all()
                    if slot[1, 0, qq, kp] != N_SLOT:
                        dtile_ref[hh, int(slot[1, 0, qq, kp])] += sum(tile(g, hh, qq, kp) for g in range(N_GRP))

    def edge():
        for hh in range(2):
            for g in range(N_GRP):
                for qq in range(G_ROWS):
                    for kp in range(GK_ROWS // 2):
                        first, last = (0 if e == N_SLOT else int(e) for e in slot[[0, 2], g, qq, kp])
                        if (slot[[0, 2], g, qq, kp] != N_SLOT).any():
                            dtile_ref[hh, _by_type(block_type, (first, first, last))] += tile(g, hh, qq, kp)

    if has_interior:
        pl.when(block_type == 1)(interior)
    pl.when(block_type != 1)(edge)


def _attn_bwd(qkv, tiles, o, dmix, lse, name, job=None):
    s = qkv.shape[0]
    n_j = s // Q_TOK
    n_blk = s // K_BLK
    per = Q_TOK // K_BLK
    scale = NA_HEAD_DIM ** -0.5
    do_col = (D_POOL + D_CONV) // HEAD_PAIR
    tables = _na_group_tables(s // GRID_W)
    koff, slot = tables[0], tables[1]

    def body(q_ref, k0, k1, k2, k3, v0, v1, v2, v3, tiles_ref, o_ref, do_ref, lse_ref,
             dq_ref, dk_ref, dv_ref, dtile_ref, bias_s, k_s, v_s, s_s, dp_s, pb_s, dsb_s):
        j = pl.program_id(1)

        @pl.when(j == 0)
        def _():
            dk_ref[...] = jnp.zeros_like(dk_ref)
            dv_ref[...] = jnp.zeros_like(dv_ref)
            dtile_ref[...] = jnp.zeros_like(dtile_ref)

        block_type = _na_block_type(j, n_j)
        pl.when((j == 0) | (j == 1) | (j == n_j - 1))(functools.partial(_fill_bias, bias_s, tiles_ref, block_type, tables))
        base = pl.multiple_of(jnp.clip(per * j - 1, 0, n_blk - 4) * K_BLK, K_BLK)
        masks = _head_masks()
        for m, (kr, vr) in enumerate(zip((k0, k1, k2, k3), (v0, v1, v2, v3))):
            rows = slice(m * K_BLK, (m + 1) * K_BLK)
            k_s[rows, :] = kr[...]
            v_s[rows, :] = vr[...]
        q = q_ref[...]
        qh = [jnp.where(mask, q, jnp.zeros_like(q)) for mask in masks]
        lane = lax.broadcasted_iota(jnp.int32, (1, HEAD_PAIR), 1)
        offs = [_group_offset(block_type, koff, g) for g in range(N_GRP)]
        do, ov, lse = do_ref[...], o_ref[...], lse_ref[0]
        dob, lse_col, delta_col = {}, [], []
        for g in range(N_GRP):
            rows = slice(g * G_TOK, (g + 1) * G_TOK)
            kg = k_s[pl.ds(offs[g], GK_TOK), :]
            vg = v_s[pl.ds(offs[g], GK_TOK), :]
            for hh, mask in enumerate(masks):
                doh = jnp.where(mask, do[rows], 0.0)
                dob[g, hh] = doh.astype(BF16)
                lse_col.append(jnp.sum(jnp.where(lane == hh * NA_HEAD_DIM, lse[rows], 0.0), axis=-1, keepdims=True))
                delta_col.append(jnp.sum(doh * ov[rows], axis=-1, keepdims=True))
                s_s[_score_rows(g, hh), :] = _dot_nt(qh[hh][rows], kg)
                dp_s[_score_rows(g, hh), :] = _dot_nt(dob[g, hh], vg)
        p = jnp.exp(s_s[...] + bias_s[...] - jnp.concatenate(lse_col, axis=0))
        ds = p * (dp_s[...] - jnp.concatenate(delta_col, axis=0))
        pb_s[...] = p.astype(BF16)
        dsb_s[...] = ds.astype(BF16)
        _add_tiles(dtile_ref, dsb_s, block_type, slot, n_j > 2)
        for g in range(N_GRP):
            rows = slice(g * G_TOK, (g + 1) * G_TOK)
            dq = jnp.zeros((G_TOK, HEAD_PAIR), F32)
            dk = jnp.zeros((GK_TOK, HEAD_PAIR), F32)
            dv = jnp.zeros((GK_TOK, HEAD_PAIR), F32)
            for hh in range(2):
                sr = _score_rows(g, hh)
                dsb = dsb_s[sr, :]
                dq = dq + jnp.where(masks[hh], _dot(dsb, k_s[pl.ds(offs[g], GK_TOK), :]), 0.0)
                dk = dk + _dot_tn(dsb, qh[hh][rows])
                dv = dv + _dot_tn(pb_s[sr, :], dob[g, hh])
            dq_ref[rows, :] = (dq * scale).astype(BF16)
            at = pl.multiple_of(base + offs[g], 2 * GRID_W)
            dk_ref[pl.ds(at, GK_TOK), :] += dk
            dv_ref[pl.ds(at, GK_TOK), :] += dv

    pair = pl.BlockSpec((Q_TOK, HEAD_PAIR), lambda hp, j: (j, hp))
    whole = pl.BlockSpec((s, HEAD_PAIR), lambda hp, j: (0, hp))
    call = _riding_call(
        body, job, 13, 4, (NA_HEADS // 2) * n_j, lambda: pl.program_id(0) * n_j + pl.program_id(1),
        name=name, grid=(NA_HEADS // 2, n_j),
        in_specs=_na_specs(s, (0, D_NA, 2 * D_NA)) + [
            pl.BlockSpec((2, N_SLOT + 1, GRID_W, 2 * GRID_W), lambda hp, j: (hp, 0, 0, 0)),
            pair, pl.BlockSpec((Q_TOK, HEAD_PAIR), lambda hp, j: (j, do_col + hp)),
            pl.BlockSpec((1, Q_TOK, HEAD_PAIR), lambda hp, j: (hp, j, 0))],
        out_specs=[pair, whole, whole, pl.BlockSpec((2, N_SLOT, GRID_W, 2 * GRID_W), lambda hp, j: (hp, 0, 0, 0))],
        out_shape=[jax.ShapeDtypeStruct((s, D_NA), BF16), jax.ShapeDtypeStruct((s, D_NA), F32),
                   jax.ShapeDtypeStruct((s, D_NA), F32),
                   jax.ShapeDtypeStruct((NA_HEADS, N_SLOT, GRID_W, 2 * GRID_W), F32)],
        scratch_shapes=[pltpu.VMEM((STACK_TOK, GK_TOK), F32), pltpu.VMEM((K_TOK, HEAD_PAIR), BF16),
                        pltpu.VMEM((K_TOK, HEAD_PAIR), BF16),
                        pltpu.VMEM((STACK_TOK, GK_TOK), F32), pltpu.VMEM((STACK_TOK, GK_TOK), F32),
                        pltpu.VMEM((STACK_TOK, GK_TOK), BF16), pltpu.VMEM((STACK_TOK, GK_TOK), BF16)],
        compiler_params=_params())
    return call(*([qkv] * 9), tiles, o, dmix, lse)


def _rpb_finish(tiles, name):
    valid, dc = _na_col_tables()
    n_dc = 2 * NA_COLS - 1
    sel = np.zeros((GRID_W, 2 * GRID_W, LANES), np.float32)
    for qc in range(GRID_W):
        for kc in range(GRID_W):
            if valid[qc, kc]:
                sel[qc, kc, dc[qc, kc]] = 1.0
                sel[qc, GRID_W + kc, LANES // 2 + dc[qc, kc]] = 1.0
    sel = jnp.asarray(sel.reshape(GRID_W * 2 * GRID_W, LANES))
    flat = tiles.reshape(NA_HEADS * 2 * NA_ROWS, GRID_W * 2 * GRID_W)

    def body(a_ref, b_ref, out_ref):
        out_ref[...] = jnp.dot(a_ref[...], b_ref[...], preferred_element_type=F32, precision=lax.Precision.HIGHEST)

    sums = pl.pallas_call(
        body, name=name, out_shape=jax.ShapeDtypeStruct((flat.shape[0], LANES), F32),
        compiler_params=_params())(flat, sel).reshape(NA_HEADS, 2 * NA_ROWS, LANES)
    return sums[:, 1:, :n_dc] + sums[:, :2 * NA_ROWS - 1, LANES // 2:LANES // 2 + n_dc]


def _loss_grad(y, target, name):
    s, d = y.shape
    tm = min(s, LOSS_TOKENS)

    def body(y_ref, t_ref, sum_ref, dy_ref):
        diff = y_ref[...] - t_ref[...]
        dy_ref[...] = diff * (1.0 / d)
        part = jnp.zeros((8, LANES), F32) + jnp.sum(diff * diff)
        _accumulate(sum_ref, part, pl.program_id(0) == 0)

    row = pl.BlockSpec((tm, d), lambda i: (i, 0))
    return pl.pallas_call(
        body, name=name, grid=(s // tm,), in_specs=[row, row],
        out_specs=[pl.BlockSpec((8, LANES), lambda i: (0, 0)), row],
        out_shape=[jax.ShapeDtypeStruct((8, LANES), F32), jax.ShapeDtypeStruct((s, d), F32)],
        compiler_params=_params())(y, target)


def _adamw(w, g, m, v, name):
    rows, cols = w.shape
    tr = _row_tile(rows, 512, 8)

    def body(w_ref, g_ref, m_ref, v_ref, d_ref, nm_ref, nv_ref):
        gv = g_ref[...]
        nm = ADAM_B1 * m_ref[...] + (1.0 - ADAM_B1) * gv
        nv = ADAM_B2 * v_ref[...] + (1.0 - ADAM_B2) * (gv * gv)
        m_hat = nm / (1.0 - ADAM_B1 ** ADAM_STEP)
        v_hat = nv / (1.0 - ADAM_B2 ** ADAM_STEP)
        d_ref[...] = -ADAM_LR * (m_hat / (jnp.sqrt(v_hat) + ADAM_EPS) + ADAM_WD * w_ref[...])
        nm_ref[...] = nm
        nv_ref[...] = nv

    blk = pl.BlockSpec((tr, cols), lambda r: (r, 0))
    return pl.pallas_call(
        body, name=name, grid=(rows // tr,), in_specs=[blk] * 4, out_specs=[blk] * 3,
        out_shape=[jax.ShapeDtypeStruct((rows, cols), F32)] * 3, compiler_params=_params())(w, g, m, v)


def _adamw_nd(w, g, m, v, name):
    shape = w.shape
    flat = lambda t: t.reshape(-1, shape[-1])
    return tuple(t.reshape(shape) for t in _adamw(flat(w), flat(g), flat(m), flat(v), name))


def _pack(parts, rows_mult=64):
    flat = jnp.concatenate([p.reshape(-1).astype(F32) for p in parts])
    per = LANES * rows_mult
    total = -(-flat.shape[0] // per) * per
    return jnp.pad(flat, (0, total - flat.shape[0])).reshape(-1, LANES)


def _unpack(packed, shapes):
    flat = packed.reshape(-1)
    out, pos = [], 0
    for shp in shapes:
        n = int(np.prod(shp))
        out.append(flat[pos:pos + n].reshape(shp))
        pos += n
    return out


def kernel(x, ffn1_w_gate, ffn1_w_up, ffn1_w_down, ffn2_w_gate, ffn2_w_up, ffn2_w_down, w_in, pool_w, pool_scale, conv_w, rpb, w_out, ln_g, ln_b, loss_target, m_ffn1_w_gate, m_ffn1_w_up, m_ffn1_w_down, m_ffn2_w_gate, m_ffn2_w_up, m_ffn2_w_down, m_w_in, m_pool_w, m_pool_scale, m_conv_w, m_rpb, m_w_out, m_ln_g, m_ln_b, v_ffn1_w_gate, v_ffn1_w_up, v_ffn1_w_down, v_ffn2_w_gate, v_ffn2_w_up, v_ffn2_w_down, v_w_in, v_pool_w, v_pool_scale, v_conv_w, v_rpb, v_w_out, v_ln_g, v_ln_b):
    n_l, d, fs = ffn1_w_gate.shape
    s = x.shape[1]
    rows = s // GRID_W
    assert x.shape[0] == 1 and s % Q_TOK == 0 and rows >= K_ROWS and fs % BF16_ROWS == 0
    alpha = (2.0 * n_l) ** 0.25
    xi, yi, ci = _mesh_pos()
    me = 4 * xi + 2 * yi + ci
    core = jnp.reshape(ci, (1,)).astype(jnp.int32)
    ln_w, cw_w = ln_g.shape[2], conv_w.shape[2]

    tr = lambda w: jnp.swapaxes(w, 1, 2)
    ffn1_shard = jnp.stack([tr(ffn1_w_gate), tr(ffn1_w_up), ffn1_w_down], axis=1).astype(BF16)
    ffn2_shard = jnp.stack([tr(ffn2_w_gate), tr(ffn2_w_up), ffn2_w_down], axis=1).astype(BF16)
    win_shard, wout_shard = tr(w_in).astype(BF16), w_out.astype(BF16)
    small_shard = _pack([ln_g, ln_b, conv_w])
    w_ffn1, small = _exchange_alone(_Gather([ffn1_shard[0], small_shard]), "gather_first")
    n_ln = n_l * 3 * ln_w
    small = small.reshape(N_DEV, -1)
    unshard = lambda t, width: jnp.moveaxis(t.reshape(N_DEV, n_l, 3, width), 0, 2).reshape(n_l, 3, N_DEV * width)
    ln_g_all = unshard(small[:, :n_ln], ln_w)
    ln_b_all = unshard(small[:, n_ln:2 * n_ln], ln_w)
    conv_all = unshard(small[:, 2 * n_ln:2 * n_ln + n_l * 3 * cw_w], cw_w)
    pool_bd = jnp.zeros((n_l, D_POOL, D_POOL), F32)
    for g in range(len(POOL_WINDOWS)):
        sl = slice(g * POOL_GROUP, (g + 1) * POOL_GROUP)
        pool_bd = pool_bd.at[:, sl, sl].set(pool_w[:, g])
    pool_bd = pool_bd.astype(BF16)
    lnp = lambda arr, l, j: arr[l, j].reshape(1, d)

    saved = []
    h = x.reshape(s, d)
    for l in range(n_l):
        a1, u1, h1, z1, x1, w_in_l, w_out_l, w_ffn2 = _ffn_fwd(
            h, w_ffn1, lnp(ln_g_all, l, 0), lnp(ln_b_all, l, 0), alpha, f"ffn1_fwd_{l}",
            job=_Gather([win_shard[l], wout_shard[l], ffn2_shard[l]]))
        proj = _win_fwd(x1, w_in_l, f"win_fwd_{l}")
        bias = _na_tiles(rpb[l])
        yab = _local_fwd(proj[0], pool_bd[l], pool_scale[l].reshape(1, D_POOL), conv_all[l], f"local_fwd_{l}")
        yc, lse = _attn_fwd(proj[1], bias, f"attn_fwd_{l}")
        a2, u2, h2, z3, x3, z2, x2, *w_next = _ffn_fwd(
            x1, w_ffn2, lnp(ln_g_all, l, 2), lnp(ln_b_all, l, 2), alpha, f"ffn2_fwd_{l}",
            job=_Gather([ffn1_shard[l + 1]]) if l + 1 < n_l else None,
            mixer=(yab, yc, w_out_l, lnp(ln_g_all, l, 1), lnp(ln_b_all, l, 1)))
        saved.append((h, a1, u1, h1, z1, x1, proj, bias, yab, yc, lse, z2, x2, a2, u2, h2, z3, w_ffn1, w_in_l, w_out_l, w_ffn2))
        h = x3
        if w_next:
            w_ffn1 = w_next[0]

    sq, dh = _loss_grad(h, loss_target.reshape(s, d), "loss_head")
    loss = lax.psum(sq[0, 0] * (0.5 / d), MESH_AXES)

    flat = lambda blocks: [b.reshape(N_DEV, -1, d) for b in blocks]
    pair_add = lambda blocks, got, tag: [_pair_add(b, g, core, f"grads_pair_add_{tag}_{i}")
                                         for i, (b, g) in enumerate(zip(blocks, got))]
    small_grads = [None] * n_l
    reduced = [None] * n_l
    above = None
    for l in reversed(range(n_l)):
        x0, a1, u1, h1, z1, x1, proj, bias, yab, yc, lse, z2, x2, a2, u2, h2, z3, w_ffn1, w_in_l, w_out_l, w_ffn2 = saved[l]
        dx2, da, du, df, dg3, db3, *got = _ffn_bwd_dx(
            dh, z3, a2, u2, w_ffn2, lnp(ln_g_all, l, 2), alpha, f"ffn2_bwd_dx_{l}",
            job=_PairExchange(above) if above else None)
        above_pairs = pair_add(above, got, f"mix_{l + 1}") if above else None
        g2 = flat(_ffn_bwd_dwd(h2, df, _ffn_bwd_dwgu(da, du, x2, fs, f"ffn2_bwd_dwgu_{l}")[0], f"ffn2_bwd_dwd_{l}"))
        dmix, dxp, dg2, db2, g_out, *got = _wout_bwd(dx2, z2, yab, yc, w_out_l, lnp(ln_g_all, l, 1), alpha,
                                                     f"wout_bwd_{l}", job=_PairExchange(g2))
        p2 = pair_add(g2, got, f"ffn2_{l}")
        dq, dk, dv, dtiles, *crossed = _attn_bwd(proj[1], bias, yc, dmix, lse, f"attn_bwd_{l}",
                                                 job=_ChipExchange(above_pairs) if above else None)
        if above:
            reduced[l + 1] += crossed
        dloc, dpw, dsc, dcw = _local_bwd(proj[0], dmix, pool_bd[l], pool_scale[l].reshape(1, D_POOL), conv_all[l],
                                         f"local_bwd_{l}")
        dx1, g_in = _win_bwd(dxp, dloc, dq, dk, dv, x1, w_in_l, f"win_bwd_{l}")
        dx0, da, du, df, dg1, db1, *crossed = _ffn_bwd_dx(dx1, z1, a1, u1, w_ffn1, lnp(ln_g_all, l, 0), alpha,
                                                          f"ffn1_bwd_dx_{l}", job=_ChipExchange(p2))
        reduced[l] = list(crossed)
        if l > 0:
            (g1,) = _ffn_bwd_dwd(h1, df, _ffn_bwd_dwgu(da, du, x0, fs, f"ffn1_bwd_dwgu_{l}")[0], f"ffn1_bwd_dwd_{l}")
            above, early = flat([g_out, g_in, g1]), []
        else:
            ready = flat([g_out, g_in])
            blocks, *got = _ffn_bwd_dwgu(da, du, x0, fs, f"ffn1_bwd_dwgu_{l}", job=_PairExchange(ready))
            g1, *early = _ffn_bwd_dwd(h1, df, blocks, f"ffn1_bwd_dwd_{l}",
                                      job=_ChipExchange(pair_add(ready, got, "early_0")))
            above = flat([g1])
        drpb = _rpb_finish(dtiles, f"rpb_finish_{l}")
        dpool = jnp.stack([dpw[g * POOL_GROUP:(g + 1) * POOL_GROUP, g * POOL_GROUP:(g + 1) * POOL_GROUP]
                           for g in range(len(POOL_WINDOWS))])
        small_grads[l] = (jnp.concatenate([dg1, dg2, dg3]), jnp.concatenate([db1, db2, db3]), dcw[0:3], dpool, dsc[0], drpb)
        dh = dx0
    grad_x = dh.reshape(x.shape)

    last_pairs = pair_add(above, _exchange_alone(_PairExchange(above), "grads_pair_exchange_last"), "mix_0")
    reduced[0] += early + list(_exchange_alone(_ChipExchange(last_pairs), "grads_chip_exchange_last"))
    sums = [[_sum_blocks(q, f"grads_chip_sum_{l}_{i}") for i, q in enumerate(reduced[l])] for l in range(n_l)]
    r_ffn2, r_out, r_in, r_ffn1 = [jnp.stack([sums[l][i] for l in range(n_l)]) for i in range(4)]
    r_ffn1, r_ffn2 = r_ffn1.reshape(n_l, 3, fs, d), r_ffn2.reshape(n_l, 3, fs, d)
    row_grads = {"ffn1_w_gate": r_ffn1[:, 0], "ffn1_w_up": r_ffn1[:, 1], "ffn2_w_gate": r_ffn2[:, 0],
                 "ffn2_w_up": r_ffn2[:, 1], "w_in": r_in}
    grads = {"ffn1_w_down": r_ffn1[:, 2], "ffn2_w_down": r_ffn2[:, 2], "w_out": r_out}
    grads.update({n: tr(g) for n, g in row_grads.items()})

    stack = lambda k: jnp.stack([small_grads[l][k] for l in range(n_l)])
    small_shapes = [(n_l, 3, d), (n_l, 3, d), (n_l, 3, D_CONV), pool_w.shape, pool_scale.shape, rpb.shape]
    (small_all,) = _exchange_alone(_Gather([_pack([stack(k) for k in range(6)])]), "gather_small_grads")
    small_sum = _sum_blocks(small_all, "small_grads_sum")
    g_ln_g, g_ln_b, g_conv, g_pool_w, g_pool_scale, g_rpb = _unpack(small_sum, small_shapes)
    own = lambda t, width: lax.dynamic_slice_in_dim(t, me * width, width, axis=2)
    grads.update({"ln_g": own(g_ln_g, ln_w), "ln_b": own(g_ln_b, ln_w), "conv_w": own(g_conv, cw_w),
                  "pool_w": g_pool_w, "pool_scale": g_pool_scale, "rpb": g_rpb})

    weights = dict(ffn1_w_gate=ffn1_w_gate, ffn1_w_up=ffn1_w_up, ffn1_w_down=ffn1_w_down, ffn2_w_gate=ffn2_w_gate,
                   ffn2_w_up=ffn2_w_up, ffn2_w_down=ffn2_w_down, w_in=w_in, pool_w=pool_w, pool_scale=pool_scale,
                   conv_w=conv_w, rpb=rpb, w_out=w_out, ln_g=ln_g, ln_b=ln_b)
    m_in = dict(ffn1_w_gate=m_ffn1_w_gate, ffn1_w_up=m_ffn1_w_up, ffn1_w_down=m_ffn1_w_down, ffn2_w_gate=m_ffn2_w_gate,
                ffn2_w_up=m_ffn2_w_up, ffn2_w_down=m_ffn2_w_down, w_in=m_w_in, pool_w=m_pool_w, pool_scale=m_pool_scale,
                conv_w=m_conv_w, rpb=m_rpb, w_out=m_w_out, ln_g=m_ln_g, ln_b=m_ln_b)
    v_in = dict(ffn1_w_gate=v_ffn1_w_gate, ffn1_w_up=v_ffn1_w_up, ffn1_w_down=v_ffn1_w_down, ffn2_w_gate=v_ffn2_w_gate,
                ffn2_w_up=v_ffn2_w_up, ffn2_w_down=v_ffn2_w_down, w_in=v_w_in, pool_w=v_pool_w, pool_scale=v_pool_scale,
                conv_w=v_conv_w, rpb=v_rpb, w_out=v_w_out, ln_g=v_ln_g, ln_b=v_ln_b)
    names = list(weights)
    large = ["ffn1_w_gate", "ffn1_w_up", "ffn1_w_down", "ffn2_w_gate", "ffn2_w_up", "ffn2_w_down", "w_in", "w_out"]
    tiny = [n for n in names if n not in large]
    delta, new_m, new_v = {}, {}, {}
    for n in large:
        if n in row_grads:
            out = _adamw_nd(tr(weights[n]), row_grads[n], tr(m_in[n]), tr(v_in[n]), f"adamw_{n}")
            delta[n], new_m[n], new_v[n] = (tr(t) for t in out)
        else:
            delta[n], new_m[n], new_v[n] = _adamw_nd(weights[n], grads[n], m_in[n], v_in[n], f"adamw_{n}")
    packed = [_pack([t[n] for n in tiny]) for t in (weights, grads, m_in, v_in)]
    tiny_out = _adamw(*packed, "adamw_small")
    tiny_shapes = [weights[n].shape for n in tiny]
    for res, t in zip((delta, new_m, new_v), tiny_out):
        res.update(dict(zip(tiny, _unpack(t, tiny_shapes))))

    return (loss, grad_x, *[grads[n] for n in names], *[delta[n] for n in names],
            *[new_m[n] for n in names], *[new_v[n] for n in names])
```
